```python
import jax, jax.numpy as jnp
from jax import lax
import numpy as np

D_MODEL = 2048
BATCH = 8
SEQ = 2048
DEPTH = 1

HEAD_DIM = 64
N_Q_HEADS = 32
N_KV_HEADS = 4
GROUP = N_Q_HEADS // N_KV_HEADS
WINDOW = 128
BLOCK = 128
D_ATTN = N_Q_HEADS * HEAD_DIM
D_KV = N_KV_HEADS * HEAD_DIM
D_RNN = 2560
N_RNN_BLOCKS = 16
RNN_BLOCK_W = D_RNN // N_RNN_BLOCKS
RNN_CONV_W = 4
LRU_C = 8.0
D_FF = 3 * D_MODEL
FFN_CONV_W = 3
LN_EPS = 1e-5
ALPHA = (2 * DEPTH) ** 0.25
BETA = (8 * DEPTH) ** -0.25
SPLIT_SIZES = (D_ATTN, D_KV, D_KV, D_RNN, D_RNN, 2 * D_MODEL)
SPLIT_POINTS = tuple(int(s) for s in np.cumsum(SPLIT_SIZES)[:-1])
D_IN = int(sum(SPLIT_SIZES))
V_START = D_ATTN + D_KV

kernel_name = "hybrid_swa_rglru_gated_merge_deepnorm"


def alibi_slopes(n_heads):
    h = np.arange(1, n_heads + 1, dtype=np.float32)
    return jnp.asarray(2.0 ** (-8.0 * h / n_heads), dtype=jnp.float32)


def layer_norm(x, g, b):
    x32 = x.astype(jnp.float32)
    mu = jnp.mean(x32, axis=-1, keepdims=True)
    var = jnp.mean(jnp.square(x32 - mu), axis=-1, keepdims=True)
    y = (x32 - mu) * lax.rsqrt(var + LN_EPS) * g.astype(jnp.float32) + b.astype(jnp.float32)
    return y.astype(x.dtype)


def causal_dwconv(x, w, b):
    width = w.shape[0]
    s = x.shape[1]
    xp = jnp.pad(x, ((0, 0), (width - 1, 0), (0, 0)))
    y = sum(xp[:, k:k + s] * w[k] for k in range(width))
    return y + b


def sliding_window_attention(q, k, v, sinks):
    bsz, s = q.shape[0], q.shape[1]
    nb = s // BLOCK
    qb = q.reshape(bsz, nb, BLOCK, N_KV_HEADS, GROUP, HEAD_DIM)

    def band(t):
        tb = t.reshape(bsz, nb, BLOCK, N_KV_HEADS, HEAD_DIM)
        prev = jnp.pad(tb, ((0, 0), (1, 0), (0, 0), (0, 0), (0, 0)))[:, :-1]
        return jnp.concatenate([prev, tb], axis=2)

    kb, vb = band(k), band(v)
    scores = jnp.einsum('bnqhgd,bnkhd->bhgnqk', qb, kb).astype(jnp.float32) * (HEAD_DIM ** -0.5)
    qi = jnp.arange(BLOCK)[:, None]
    kj = jnp.arange(2 * BLOCK)[None, :]
    dist = BLOCK + qi - kj
    blk = jnp.arange(nb)[:, None, None]
    key_pos = (blk - 1) * BLOCK + kj
    valid = (dist >= 0) & (dist < WINDOW) & (key_pos >= 0)
    slopes = alibi_slopes(N_Q_HEADS).reshape(N_KV_HEADS, GROUP, 1, 1, 1)
    scores = scores - slopes * dist.astype(jnp.float32)
    scores = jnp.where(valid, scores, -jnp.inf)
    sink = sinks.astype(jnp.float32).reshape(N_KV_HEADS, GROUP, 1, 1, 1)
    m = jnp.maximum(jnp.max(scores, axis=-1, keepdims=True), sink)
    p = jnp.exp(scores - m)
    denom = jnp.sum(p, axis=-1, keepdims=True) + jnp.exp(sink - m)
    p = (p / denom).astype(v.dtype)
    o = jnp.einsum('bhgnqk,bnkhd->bnqhgd', p, vb)
    return o.reshape(bsz, s, D_ATTN)


def rg_lru(x, w_a, b_a, w_i, b_i, lam):
    bsz, s, _ = x.shape
    xb = x.reshape(bsz, s, N_RNN_BLOCKS, RNN_BLOCK_W)
    r = jax.nn.sigmoid(jnp.einsum('bsnc,ncd->bsnd', xb, w_a).reshape(bsz, s, D_RNN) + b_a)
    i = jax.nn.sigmoid(jnp.einsum('bsnc,ncd->bsnd', xb, w_i).reshape(bsz, s, D_RNN) + b_i)
    log_a = -LRU_C * r.astype(jnp.float32) * jax.nn.softplus(-lam.astype(jnp.float32))
    a = jnp.exp(log_a)
    u = jnp.sqrt(-jnp.expm1(2.0 * log_a)) * (i * x).astype(jnp.float32)

    def combine(left, right):
        a_l, b_l = left
        a_r, b_r = right
        return a_l * a_r, a_r * b_l + b_r

    _, h = lax.associative_scan(combine, (a, u), axis=1)
    return h.astype(x.dtype)


def _fwd_setup_inputs(seed: int = 0) -> dict:
    key = jax.random.key(seed)
    ks = jax.random.split(key, 24)
    f32 = jnp.float32
    L = DEPTH
    nrm = lambda k, shape, scale: jax.random.normal(k, shape, f32) * scale
    x = jax.random.normal(ks[0], (BATCH, SEQ, D_MODEL), f32)
    w_in = nrm(ks[1], (L, D_MODEL, D_IN), D_MODEL ** -0.5)
    w_in = w_in.at[:, :, V_START:V_START + D_KV].multiply(BETA)
    b_gate = nrm(ks[2], (L, 2 * D_MODEL), 0.1)
    rnn_conv_w = nrm(ks[3], (L, RNN_CONV_W, D_RNN), RNN_CONV_W ** -0.5)
    rnn_conv_b = nrm(ks[4], (L, D_RNN), 0.02)
    lru_wa = nrm(ks[5], (L, N_RNN_BLOCKS, RNN_BLOCK_W, RNN_BLOCK_W), RNN_BLOCK_W ** -0.5)
    lru_ba = nrm(ks[6], (L, D_RNN), 0.02)
    lru_wi = nrm(ks[7], (L, N_RNN_BLOCKS, RNN_BLOCK_W, RNN_BLOCK_W), RNN_BLOCK_W ** -0.5)
    lru_bi = nrm(ks[8], (L, D_RNN), 0.02)
    a_c = jax.random.uniform(ks[9], (L, D_RNN), f32, minval=0.9, maxval=0.999)
    a0 = a_c ** (1.0 / LRU_C)
    lru_lambda = jnp.log(a0) - jnp.log1p(-a0)
    attn_sinks = nrm(ks[10], (L, N_Q_HEADS), 0.5)
    w_attn_proj = nrm(ks[11], (L, D_ATTN, D_MODEL), BETA * D_ATTN ** -0.5)
    w_rnn_proj = nrm(ks[12], (L, D_RNN, D_MODEL), BETA * D_RNN ** -0.5)
    w_out = nrm(ks[13], (L, D_MODEL, D_MODEL), BETA * D_MODEL ** -0.5)
    ln1_g = 1.0 + nrm(ks[14], (L, D_MODEL), 0.02)
    ln1_b = nrm(ks[15], (L, D_MODEL), 0.02)
    ffn_w_up = nrm(ks[16], (L, D_MODEL, D_FF), D_MODEL ** -0.5)
    ffn_w_gate = nrm(ks[17], (L, D_MODEL, D_FF), D_MODEL ** -0.5)
    ffn_conv_w = nrm(ks[18], (L, FFN_CONV_W, D_FF), FFN_CONV_W ** -0.5)
    ffn_conv_b = nrm(ks[19], (L, D_FF), 0.02)
    ffn_w_down = nrm(ks[20], (L, D_FF, D_MODEL), BETA * D_FF ** -0.5)
    ln2_g = 1.0 + nrm(ks[21], (L, D_MODEL), 0.02)
    ln2_b = nrm(ks[22], (L, D_MODEL), 0.02)
    return {"x": x, "w_in": w_in, "b_gate": b_gate, "rnn_conv_w": rnn_conv_w,
            "rnn_conv_b": rnn_conv_b, "lru_wa": lru_wa, "lru_ba": lru_ba, "lru_wi": lru_wi,
            "lru_bi": lru_bi, "lru_lambda": lru_lambda, "attn_sinks": attn_sinks,
            "w_attn_proj": w_attn_proj, "w_rnn_proj": w_rnn_proj, "w_out": w_out,
            "ln1_g": ln1_g, "ln1_b": ln1_b, "ffn_w_up": ffn_w_up, "ffn_w_gate": ffn_w_gate,
            "ffn_conv_w": ffn_conv_w, "ffn_conv_b": ffn_conv_b, "ffn_w_down": ffn_w_down,
            "ln2_g": ln2_g, "ln2_b": ln2_b}


def _fwd_reference(x, w_in, b_gate, rnn_conv_w, rnn_conv_b, lru_wa, lru_ba, lru_wi, lru_bi,
              lru_lambda, attn_sinks, w_attn_proj, w_rnn_proj, w_out, ln1_g, ln1_b,
              ffn_w_up, ffn_w_gate, ffn_conv_w, ffn_conv_b, ffn_w_down, ln2_g, ln2_b):
    bsz, s, _ = x.shape
    for l in range(DEPTH):
        proj = x @ w_in[l]
        q, k, v, rx, ry, gl = jnp.split(proj, SPLIT_POINTS, axis=-1)
        q = q.reshape(bsz, s, N_Q_HEADS, HEAD_DIM)
        k = k.reshape(bsz, s, N_KV_HEADS, HEAD_DIM)
        v = v.reshape(bsz, s, N_KV_HEADS, HEAD_DIM)
        y_attn = sliding_window_attention(q, k, v, attn_sinks[l]) @ w_attn_proj[l]
        rx = causal_dwconv(rx, rnn_conv_w[l], rnn_conv_b[l])
        hr = rg_lru(rx, lru_wa[l], lru_ba[l], lru_wi[l], lru_bi[l], lru_lambda[l])
        y_rnn = (hr * jax.nn.gelu(ry, approximate=True)) @ w_rnn_proj[l]
        g_attn, g_rnn = jnp.split(jax.nn.sigmoid(gl + b_gate[l]), 2, axis=-1)
        mix = (g_attn * y_attn + g_rnn * y_rnn) @ w_out[l]
        x = layer_norm(ALPHA * x + mix, ln1_g[l], ln1_b[l])
        up = x @ ffn_w_up[l]
        gate = causal_dwconv(x @ ffn_w_gate[l], ffn_conv_w[l], ffn_conv_b[l])
        f = (jax.nn.gelu(gate, approximate=True) * up) @ ffn_w_down[l]
        x = layer_norm(ALPHA * x + f, ln2_g[l], ln2_b[l])
    return x


import jax as _jax
import jax.numpy as _jnp

TWIN_FORMAT = 'train_step'
FWD_PARAMS = ['x', 'w_in', 'b_gate', 'rnn_conv_w', 'rnn_conv_b', 'lru_wa', 'lru_ba', 'lru_wi', 'lru_bi', 'lru_lambda', 'attn_sinks', 'w_attn_proj', 'w_rnn_proj', 'w_out', 'ln1_g', 'ln1_b', 'ffn_w_up', 'ffn_w_gate', 'ffn_conv_w', 'ffn_conv_b', 'ffn_w_down', 'ln2_g', 'ln2_b']
TWIN_WEIGHTS = ['w_in', 'b_gate', 'rnn_conv_w', 'rnn_conv_b', 'lru_wa', 'lru_ba', 'lru_wi', 'lru_bi', 'lru_lambda', 'attn_sinks', 'w_attn_proj', 'w_rnn_proj', 'w_out', 'ln1_g', 'ln1_b', 'ffn_w_up', 'ffn_w_gate', 'ffn_conv_w', 'ffn_conv_b', 'ffn_w_down', 'ln2_g', 'ln2_b']
TWIN_DIFF_INPUT = 'x'
TWIN_INPUTS = ['x', 'w_in', 'b_gate', 'rnn_conv_w', 'rnn_conv_b', 'lru_wa', 'lru_ba', 'lru_wi', 'lru_bi', 'lru_lambda', 'attn_sinks', 'w_attn_proj', 'w_rnn_proj', 'w_out', 'ln1_g', 'ln1_b', 'ffn_w_up', 'ffn_w_gate', 'ffn_conv_w', 'ffn_conv_b', 'ffn_w_down', 'ln2_g', 'ln2_b', 'loss_target', 'm_w_in', 'm_b_gate', 'm_rnn_conv_w', 'm_rnn_conv_b', 'm_lru_wa', 'm_lru_ba', 'm_lru_wi', 'm_lru_bi', 'm_lru_lambda', 'm_attn_sinks', 'm_w_attn_proj', 'm_w_rnn_proj', 'm_w_out', 'm_ln1_g', 'm_ln1_b', 'm_ffn_w_up', 'm_ffn_w_gate', 'm_ffn_conv_w', 'm_ffn_conv_b', 'm_ffn_w_down', 'm_ln2_g', 'm_ln2_b', 'v_w_in', 'v_b_gate', 'v_rnn_conv_w', 'v_rnn_conv_b', 'v_lru_wa', 'v_lru_ba', 'v_lru_wi', 'v_lru_bi', 'v_lru_lambda', 'v_attn_sinks', 'v_w_attn_proj', 'v_w_rnn_proj', 'v_w_out', 'v_ln1_g', 'v_ln1_b', 'v_ffn_w_up', 'v_ffn_w_gate', 'v_ffn_conv_w', 'v_ffn_conv_b', 'v_ffn_w_down', 'v_ln2_g', 'v_ln2_b']
TWIN_OUTPUTS = ['loss', 'grad_x', 'grad_w_in', 'grad_b_gate', 'grad_rnn_conv_w', 'grad_rnn_conv_b', 'grad_lru_wa', 'grad_lru_ba', 'grad_lru_wi', 'grad_lru_bi', 'grad_lru_lambda', 'grad_attn_sinks', 'grad_w_attn_proj', 'grad_w_rnn_proj', 'grad_w_out', 'grad_ln1_g', 'grad_ln1_b', 'grad_ffn_w_up', 'grad_ffn_w_gate', 'grad_ffn_conv_w', 'grad_ffn_conv_b', 'grad_ffn_w_down', 'grad_ln2_g', 'grad_ln2_b', 'delta_w_in', 'delta_b_gate', 'delta_rnn_conv_w', 'delta_rnn_conv_b', 'delta_lru_wa', 'delta_lru_ba', 'delta_lru_wi', 'delta_lru_bi', 'delta_lru_lambda', 'delta_attn_sinks', 'delta_w_attn_proj', 'delta_w_rnn_proj', 'delta_w_out', 'delta_ln1_g', 'delta_ln1_b', 'delta_ffn_w_up', 'delta_ffn_w_gate', 'delta_ffn_conv_w', 'delta_ffn_conv_b', 'delta_ffn_w_down', 'delta_ln2_g', 'delta_ln2_b', 'new_m_w_in', 'new_m_b_gate', 'new_m_rnn_conv_w', 'new_m_rnn_conv_b', 'new_m_lru_wa', 'new_m_lru_ba', 'new_m_lru_wi', 'new_m_lru_bi', 'new_m_lru_lambda', 'new_m_attn_sinks', 'new_m_w_attn_proj', 'new_m_w_rnn_proj', 'new_m_w_out', 'new_m_ln1_g', 'new_m_ln1_b', 'new_m_ffn_w_up', 'new_m_ffn_w_gate', 'new_m_ffn_conv_w', 'new_m_ffn_conv_b', 'new_m_ffn_w_down', 'new_m_ln2_g', 'new_m_ln2_b', 'new_v_w_in', 'new_v_b_gate', 'new_v_rnn_conv_w', 'new_v_rnn_conv_b', 'new_v_lru_wa', 'new_v_lru_ba', 'new_v_lru_wi', 'new_v_lru_bi', 'new_v_lru_lambda', 'new_v_attn_sinks', 'new_v_w_attn_proj', 'new_v_w_rnn_proj', 'new_v_w_out', 'new_v_ln1_g', 'new_v_ln1_b', 'new_v_ffn_w_up', 'new_v_ffn_w_gate', 'new_v_ffn_conv_w', 'new_v_ffn_conv_b', 'new_v_ffn_w_down', 'new_v_ln2_g', 'new_v_ln2_b']
TWIN_LEAF_KINDS = {'loss': 'loss', 'grad_x': 'grad_x', 'grad_w_in': 'grad_w', 'grad_b_gate': 'grad_w', 'grad_rnn_conv_w': 'grad_w', 'grad_rnn_conv_b': 'grad_w', 'grad_lru_wa': 'grad_w', 'grad_lru_ba': 'grad_w', 'grad_lru_wi': 'grad_w', 'grad_lru_bi': 'grad_w', 'grad_lru_lambda': 'grad_w', 'grad_attn_sinks': 'grad_w', 'grad_w_attn_proj': 'grad_w', 'grad_w_rnn_proj': 'grad_w', 'grad_w_out': 'grad_w', 'grad_ln1_g': 'grad_w', 'grad_ln1_b': 'grad_w', 'grad_ffn_w_up': 'grad_w', 'grad_ffn_w_gate': 'grad_w', 'grad_ffn_conv_w': 'grad_w', 'grad_ffn_conv_b': 'grad_w', 'grad_ffn_w_down': 'grad_w', 'grad_ln2_g': 'grad_w', 'grad_ln2_b': 'grad_w', 'delta_w_in': 'delta_w', 'delta_b_gate': 'delta_w', 'delta_rnn_conv_w': 'delta_w', 'delta_rnn_conv_b': 'delta_w', 'delta_lru_wa': 'delta_w', 'delta_lru_ba': 'delta_w', 'delta_lru_wi': 'delta_w', 'delta_lru_bi': 'delta_w', 'delta_lru_lambda': 'delta_w', 'delta_attn_sinks': 'delta_w', 'delta_w_attn_proj': 'delta_w', 'delta_w_rnn_proj': 'delta_w', 'delta_w_out': 'delta_w', 'delta_ln1_g': 'delta_w', 'delta_ln1_b': 'delta_w', 'delta_ffn_w_up': 'delta_w', 'delta_ffn_w_gate': 'delta_w', 'delta_ffn_conv_w': 'delta_w', 'delta_ffn_conv_b': 'delta_w', 'delta_ffn_w_down': 'delta_w', 'delta_ln2_g': 'delta_w', 'delta_ln2_b': 'delta_w', 'new_m_w_in': 'new_m', 'new_m_b_gate': 'new_m', 'new_m_rnn_conv_w': 'new_m', 'new_m_rnn_conv_b': 'new_m', 'new_m_lru_wa': 'new_m', 'new_m_lru_ba': 'new_m', 'new_m_lru_wi': 'new_m', 'new_m_lru_bi': 'new_m', 'new_m_lru_lambda': 'new_m', 'new_m_attn_sinks': 'new_m', 'new_m_w_attn_proj': 'new_m', 'new_m_w_rnn_proj': 'new_m', 'new_m_w_out': 'new_m', 'new_m_ln1_g': 'new_m', 'new_m_ln1_b': 'new_m', 'new_m_ffn_w_up': 'new_m', 'new_m_ffn_w_gate': 'new_m', 'new_m_ffn_conv_w': 'new_m', 'new_m_ffn_conv_b': 'new_m', 'new_m_ffn_w_down': 'new_m', 'new_m_ln2_g': 'new_m', 'new_m_ln2_b': 'new_m', 'new_v_w_in': 'new_v', 'new_v_b_gate': 'new_v', 'new_v_rnn_conv_w': 'new_v', 'new_v_rnn_conv_b': 'new_v', 'new_v_lru_wa': 'new_v', 'new_v_lru_ba': 'new_v', 'new_v_lru_wi': 'new_v', 'new_v_lru_bi': 'new_v', 'new_v_lru_lambda': 'new_v', 'new_v_attn_sinks': 'new_v', 'new_v_w_attn_proj': 'new_v', 'new_v_w_rnn_proj': 'new_v', 'new_v_w_out': 'new_v', 'new_v_ln1_g': 'new_v', 'new_v_ln1_b': 'new_v', 'new_v_ffn_w_up': 'new_v', 'new_v_ffn_w_gate': 'new_v', 'new_v_ffn_conv_w': 'new_v', 'new_v_ffn_conv_b': 'new_v', 'new_v_ffn_w_down': 'new_v', 'new_v_ln2_g': 'new_v', 'new_v_ln2_b': 'new_v'}


def _forward(args):
    return _fwd_reference(*[args[k] for k in FWD_PARAMS])


def _output_shape():
    out = _jax.eval_shape(lambda: _forward(_fwd_setup_inputs(0)))
    return out.shape, out.dtype

N_MICROBATCH = 1
ADAM_LR = 0.001
ADAM_B1 = 0.9
ADAM_B2 = 0.999
ADAM_EPS = 1e-08
ADAM_WD = 0.01
ADAM_STEP = 10
PER_EXAMPLE_BATCH_AXIS = {'x': 0, 'loss_target': 0}
SHARED_INPUTS = []
_WEIGHT_DTYPES = {'w_in': _jnp.float32, 'b_gate': _jnp.float32, 'rnn_conv_w': _jnp.float32, 'rnn_conv_b': _jnp.float32, 'lru_wa': _jnp.float32, 'lru_ba': _jnp.float32, 'lru_wi': _jnp.float32, 'lru_bi': _jnp.float32, 'lru_lambda': _jnp.float32, 'attn_sinks': _jnp.float32, 'w_attn_proj': _jnp.float32, 'w_rnn_proj': _jnp.float32, 'w_out': _jnp.float32, 'ln1_g': _jnp.float32, 'ln1_b': _jnp.float32, 'ffn_w_up': _jnp.float32, 'ffn_w_gate': _jnp.float32, 'ffn_conv_w': _jnp.float32, 'ffn_conv_b': _jnp.float32, 'ffn_w_down': _jnp.float32, 'ln2_g': _jnp.float32, 'ln2_b': _jnp.float32}
MOMENT_SCALE = {'w_in': 3.139538e-03, 'b_gate': 1.367818e-03, 'rnn_conv_w': 4.183166e-03, 'rnn_conv_b': 5.210235e-02, 'lru_wa': 1.378698e-03, 'lru_ba': 1.077610e-03, 'lru_wi': 2.460497e-03, 'lru_bi': 1.498425e-03, 'lru_lambda': 2.055272e-03, 'attn_sinks': 3.290671e-03, 'w_attn_proj': 3.367660e-03, 'w_rnn_proj': 7.975745e-03, 'w_out': 8.471690e-03, 'ln1_g': 2.690051e-01, 'ln1_b': 1.394262e-01, 'ffn_w_up': 1.124118e-02, 'ffn_w_gate': 1.161844e-02, 'ffn_conv_w': 1.167449e-02, 'ffn_conv_b': 1.128078e-02, 'ffn_w_down': 3.275929e-02, 'ln2_g': 8.005737e+00, 'ln2_b': 2.178613e-01}


def _to_microbatches(a, axis):
    t = _jnp.moveaxis(a, axis, 0)
    t = t.reshape((N_MICROBATCH, t.shape[0] // N_MICROBATCH) + t.shape[1:])
    return _jnp.moveaxis(t, 1, axis + 1)


def setup_inputs(seed: int = 0) -> dict:
    inp = _fwd_setup_inputs(seed)
    key = _jax.random.fold_in(_jax.random.key(seed), 7919)
    shape, _ = _output_shape()
    out = dict(inp)
    out["loss_target"] = _jax.random.normal(_jax.random.fold_in(key, 0), shape, _jnp.float32)
    for i, name in enumerate(TWIN_WEIGHTS):
        w = inp[name].astype(_jnp.float32)
        if MOMENT_SCALE is None:
            s = _jnp.sqrt(_jnp.mean(_jnp.square(w)) + 1e-30)
        else:
            s = MOMENT_SCALE[name]
        km, kv = _jax.random.split(_jax.random.fold_in(key, i + 1))
        out[name] = w
        out["m_" + name] = s * _jax.random.normal(km, w.shape, _jnp.float32)
        out["v_" + name] = (s * s) * _jax.random.uniform(kv, w.shape, _jnp.float32, 0.5, 1.5)
    if N_MICROBATCH > 1:
        for name, axis in PER_EXAMPLE_BATCH_AXIS.items():
            out[name] = _to_microbatches(out[name], axis)
    return {'x': out['x'], 'w_in': out['w_in'], 'b_gate': out['b_gate'], 'rnn_conv_w': out['rnn_conv_w'], 'rnn_conv_b': out['rnn_conv_b'], 'lru_wa': out['lru_wa'], 'lru_ba': out['lru_ba'], 'lru_wi': out['lru_wi'], 'lru_bi': out['lru_bi'], 'lru_lambda': out['lru_lambda'], 'attn_sinks': out['attn_sinks'], 'w_attn_proj': out['w_attn_proj'], 'w_rnn_proj': out['w_rnn_proj'], 'w_out': out['w_out'], 'ln1_g': out['ln1_g'], 'ln1_b': out['ln1_b'], 'ffn_w_up': out['ffn_w_up'], 'ffn_w_gate': out['ffn_w_gate'], 'ffn_conv_w': out['ffn_conv_w'], 'ffn_conv_b': out['ffn_conv_b'], 'ffn_w_down': out['ffn_w_down'], 'ln2_g': out['ln2_g'], 'ln2_b': out['ln2_b'], 'loss_target': out['loss_target'], 'm_w_in': out['m_w_in'], 'm_b_gate': out['m_b_gate'], 'm_rnn_conv_w': out['m_rnn_conv_w'], 'm_rnn_conv_b': out['m_rnn_conv_b'], 'm_lru_wa': out['m_lru_wa'], 'm_lru_ba': out['m_lru_ba'], 'm_lru_wi': out['m_lru_wi'], 'm_lru_bi': out['m_lru_bi'], 'm_lru_lambda': out['m_lru_lambda'], 'm_attn_sinks': out['m_attn_sinks'], 'm_w_attn_proj': out['m_w_attn_proj'], 'm_w_rnn_proj': out['m_w_rnn_proj'], 'm_w_out': out['m_w_out'], 'm_ln1_g': out['m_ln1_g'], 'm_ln1_b': out['m_ln1_b'], 'm_ffn_w_up': out['m_ffn_w_up'], 'm_ffn_w_gate': out['m_ffn_w_gate'], 'm_ffn_conv_w': out['m_ffn_conv_w'], 'm_ffn_conv_b': out['m_ffn_conv_b'], 'm_ffn_w_down': out['m_ffn_w_down'], 'm_ln2_g': out['m_ln2_g'], 'm_ln2_b': out['m_ln2_b'], 'v_w_in': out['v_w_in'], 'v_b_gate': out['v_b_gate'], 'v_rnn_conv_w': out['v_rnn_conv_w'], 'v_rnn_conv_b': out['v_rnn_conv_b'], 'v_lru_wa': out['v_lru_wa'], 'v_lru_ba': out['v_lru_ba'], 'v_lru_wi': out['v_lru_wi'], 'v_lru_bi': out['v_lru_bi'], 'v_lru_lambda': out['v_lru_lambda'], 'v_attn_sinks': out['v_attn_sinks'], 'v_w_attn_proj': out['v_w_attn_proj'], 'v_w_rnn_proj': out['v_w_rnn_proj'], 'v_w_out': out['v_w_out'], 'v_ln1_g': out['v_ln1_g'], 'v_ln1_b': out['v_ln1_b'], 'v_ffn_w_up': out['v_ffn_w_up'], 'v_ffn_w_gate': out['v_ffn_w_gate'], 'v_ffn_conv_w': out['v_ffn_conv_w'], 'v_ffn_conv_b': out['v_ffn_conv_b'], 'v_ffn_w_down': out['v_ffn_w_down'], 'v_ln2_g': out['v_ln2_g'], 'v_ln2_b': out['v_ln2_b']}


def _loss(weights, diff, rest, loss_target):
    with _jax.named_scope("forward"):
        args = {**rest, TWIN_DIFF_INPUT: diff, **{k: w.astype(_WEIGHT_DTYPES[k]) for k, w in weights.items()}}
        y = _forward(args)
    with _jax.named_scope("loss_head"):
        err = _jnp.square(y.astype(_jnp.float32) - loss_target)
        return 0.5 * _jnp.sum(_jnp.mean(err, axis=-1)) if err.ndim else 0.5 * err


def _adamw(w, g, m, v):
    m = ADAM_B1 * m + (1.0 - ADAM_B1) * g
    v = ADAM_B2 * v + (1.0 - ADAM_B2) * _jnp.square(g)
    m_hat = m / (1.0 - ADAM_B1 ** ADAM_STEP)
    v_hat = v / (1.0 - ADAM_B2 ** ADAM_STEP)
    delta = -ADAM_LR * (m_hat / (_jnp.sqrt(v_hat) + ADAM_EPS) + ADAM_WD * w)
    return delta, m, v


def reference(x, w_in, b_gate, rnn_conv_w, rnn_conv_b, lru_wa, lru_ba, lru_wi, lru_bi, lru_lambda, attn_sinks, w_attn_proj, w_rnn_proj, w_out, ln1_g, ln1_b, ffn_w_up, ffn_w_gate, ffn_conv_w, ffn_conv_b, ffn_w_down, ln2_g, ln2_b, loss_target, m_w_in, m_b_gate, m_rnn_conv_w, m_rnn_conv_b, m_lru_wa, m_lru_ba, m_lru_wi, m_lru_bi, m_lru_lambda, m_attn_sinks, m_w_attn_proj, m_w_rnn_proj, m_w_out, m_ln1_g, m_ln1_b, m_ffn_w_up, m_ffn_w_gate, m_ffn_conv_w, m_ffn_conv_b, m_ffn_w_down, m_ln2_g, m_ln2_b, v_w_in, v_b_gate, v_rnn_conv_w, v_rnn_conv_b, v_lru_wa, v_lru_ba, v_lru_wi, v_lru_bi, v_lru_lambda, v_attn_sinks, v_w_attn_proj, v_w_rnn_proj, v_w_out, v_ln1_g, v_ln1_b, v_ffn_w_up, v_ffn_w_gate, v_ffn_conv_w, v_ffn_conv_b, v_ffn_w_down, v_ln2_g, v_ln2_b):
    given = dict(x=x, w_in=w_in, b_gate=b_gate, rnn_conv_w=rnn_conv_w, rnn_conv_b=rnn_conv_b, lru_wa=lru_wa, lru_ba=lru_ba, lru_wi=lru_wi, lru_bi=lru_bi, lru_lambda=lru_lambda, attn_sinks=attn_sinks, w_attn_proj=w_attn_proj, w_rnn_proj=w_rnn_proj, w_out=w_out, ln1_g=ln1_g, ln1_b=ln1_b, ffn_w_up=ffn_w_up, ffn_w_gate=ffn_w_gate, ffn_conv_w=ffn_conv_w, ffn_conv_b=ffn_conv_b, ffn_w_down=ffn_w_down, ln2_g=ln2_g, ln2_b=ln2_b, loss_target=loss_target, m_w_in=m_w_in, m_b_gate=m_b_gate, m_rnn_conv_w=m_rnn_conv_w, m_rnn_conv_b=m_rnn_conv_b, m_lru_wa=m_lru_wa, m_lru_ba=m_lru_ba, m_lru_wi=m_lru_wi, m_lru_bi=m_lru_bi, m_lru_lambda=m_lru_lambda, m_attn_sinks=m_attn_sinks, m_w_attn_proj=m_w_attn_proj, m_w_rnn_proj=m_w_rnn_proj, m_w_out=m_w_out, m_ln1_g=m_ln1_g, m_ln1_b=m_ln1_b, m_ffn_w_up=m_ffn_w_up, m_ffn_w_gate=m_ffn_w_gate, m_ffn_conv_w=m_ffn_conv_w, m_ffn_conv_b=m_ffn_conv_b, m_ffn_w_down=m_ffn_w_down, m_ln2_g=m_ln2_g, m_ln2_b=m_ln2_b, v_w_in=v_w_in, v_b_gate=v_b_gate, v_rnn_conv_w=v_rnn_conv_w, v_rnn_conv_b=v_rnn_conv_b, v_lru_wa=v_lru_wa, v_lru_ba=v_lru_ba, v_lru_wi=v_lru_wi, v_lru_bi=v_lru_bi, v_lru_lambda=v_lru_lambda, v_attn_sinks=v_attn_sinks, v_w_attn_proj=v_w_attn_proj, v_w_rnn_proj=v_w_rnn_proj, v_w_out=v_w_out, v_ln1_g=v_ln1_g, v_ln1_b=v_ln1_b, v_ffn_w_up=v_ffn_w_up, v_ffn_w_gate=v_ffn_w_gate, v_ffn_conv_w=v_ffn_conv_w, v_ffn_conv_b=v_ffn_conv_b, v_ffn_w_down=v_ffn_w_down, v_ln2_g=v_ln2_g, v_ln2_b=v_ln2_b)
    weights = {n: given[n] for n in TWIN_WEIGHTS}
    shared = {n: given[n] for n in SHARED_INPUTS}
    per_example = {n: given[n] for n in ['x']}
    grad_fn = _jax.value_and_grad(_loss, argnums=(0, 1))

    def one_microbatch(ex, loss_target):
        ex = dict(ex)
        diff = ex.pop(TWIN_DIFF_INPUT)
        return grad_fn(weights, diff, {**shared, **ex}, loss_target)

    if N_MICROBATCH == 1:
        loss, (grad_w, grad_x) = one_microbatch(per_example, given["loss_target"])
    else:
        def body(carry, xs):
            loss_sum, grad_sum = carry
            l_k, (gw_k, gx_k) = one_microbatch(xs[0], xs[1])
            with _jax.named_scope("update"):
                return (loss_sum + l_k, _jax.tree.map(_jnp.add, grad_sum, gw_k)), gx_k

        init = (_jnp.zeros((), _jnp.float32), _jax.tree.map(_jnp.zeros_like, weights))
        (loss, grad_w), grad_x = _jax.lax.scan(body, init, (per_example, given["loss_target"]))
    with _jax.named_scope("update"):
        delta_w, new_m, new_v = {}, {}, {}
        for n in TWIN_WEIGHTS:
            delta_w[n], new_m[n], new_v[n] = _adamw(weights[n], grad_w[n], given["m_" + n], given["v_" + n])
    return (loss, grad_x, *[grad_w[n] for n in TWIN_WEIGHTS], *[delta_w[n] for n in TWIN_WEIGHTS],
            *[new_m[n] for n in TWIN_WEIGHTS], *[new_v[n] for n in TWIN_WEIGHTS])
```

```python
import functools
import math

import jax
import jax.numpy as jnp
from jax import lax
from jax.experimental import pallas as pl
from jax.experimental.pallas import tpu as pltpu

F32 = jnp.float32
BF16 = jnp.bfloat16
MESH = pl.DeviceIdType.MESH

HEAD_DIM = 64
GROUP = 8
ATTN_BLOCK = 128
LRU_C = 8.0
LN_EPS = 1e-5
ALPHA = 2.0 ** 0.25
LANES = 128
N_SHARDS = 4
N_DEV = 8
VMEM_LIMIT = 56 * 1024 * 1024
NEG = -1e30

ADAM_LR, ADAM_B1, ADAM_B2, ADAM_EPS, ADAM_WD, ADAM_STEP = 0.001, 0.9, 0.999, 1e-08, 0.01, 10

GELU_C = math.sqrt(2.0 / math.pi)
GELU_A = 0.044715


def _cparams(sem=None):
    kw = dict(vmem_limit_bytes=VMEM_LIMIT)
    if sem is not None:
        kw["dimension_semantics"] = sem
    return pltpu.CompilerParams(**kw)


def _pick(n, prefs):
    for p in prefs:
        if n % p == 0:
            return p
    return n


def _row_tile(rows, row_bytes, mult, budget=2 * 1024 * 1024):
    best = None
    for d in range(mult, rows + 1, mult):
        if rows % d == 0 and d * row_bytes <= budget:
            best = d
    return best if best is not None else rows


def _gelu(x):
    return 0.5 * x * (1.0 + jnp.tanh(GELU_C * (x + GELU_A * x * x * x)))


def _gelu_and_grad(x):
    t = jnp.tanh(GELU_C * (x + GELU_A * x * x * x))
    g = 0.5 * x * (1.0 + t)
    dg = 0.5 * (1.0 + t) + 0.5 * x * (1.0 - t * t) * GELU_C * (1.0 + 3.0 * GELU_A * x * x)
    return g, dg


def _shift_down(x, s, fill=0.0):
    row = lax.broadcasted_iota(jnp.int32, x.shape, 0)
    return jnp.where(row >= s, pltpu.roll(x, s, 0), fill)


def _shift_up(x, s, fill=0.0):
    n = x.shape[0]
    row = lax.broadcasted_iota(jnp.int32, x.shape, 0)
    return jnp.where(row < n - s, pltpu.roll(x, n - s, 0), fill)


def _mm(a, b, *, name, ta=False, tb=False, out_dtype=F32, adds=(), b_shards=1, out_shards=1,
        tm=None, tn=None, tk=None):
    if ta:
        K, M = a.shape
    else:
        M, K = a.shape
    if b_shards > 1:
        n_sh = b.shape[-1]
        if tb:
            N = b.shape[1]
            assert b_shards * n_sh == K
        else:
            N = b_shards * n_sh
            assert b.shape[1] == K
    else:
        n_sh = None
        if tb:
            N = b.shape[0]
            assert b.shape[1] == K
        else:
            N = b.shape[1]
            assert b.shape[0] == K
    tm = tm or _pick(M, (512, 256, 128))
    if tn is None:
        if b_shards > 1 and not tb:
            tn = _pick(n_sh, (1024, 512))
        elif out_shards > 1:
            tn = _pick(N // out_shards, (1024, 512))
        else:
            tn = _pick(N, (1024, 512, 256, 128))
    if tk is None:
        if b_shards > 1 and tb:
            tk = _pick(n_sh, (1024, 512))
        else:
            tk = _pick(K, (512, 256, 128))
    assert M % tm == 0 and N % tn == 0 and K % tk == 0, (name, M, N, K, tm, tn, tk)
    nk = K // tk
    n_add = len(adds)

    a_spec = pl.BlockSpec((tk, tm), lambda i, j, k: (k, i)) if ta else pl.BlockSpec((tm, tk), lambda i, j, k: (i, k))
    if b_shards > 1 and not tb:
        per = n_sh // tn
        b_spec = pl.BlockSpec((None, tk, tn), lambda i, j, k: (j // per, k, j % per))
    elif b_shards > 1 and tb:
        per = n_sh // tk
        b_spec = pl.BlockSpec((None, tn, tk), lambda i, j, k: (k // per, j, k % per))
    elif tb:
        b_spec = pl.BlockSpec((tn, tk), lambda i, j, k: (j, k))
    else:
        b_spec = pl.BlockSpec((tk, tn), lambda i, j, k: (k, j))
    add_specs = [pl.BlockSpec((tm, tn), lambda i, j, k: (i, j)) for _ in adds]
    if out_shards > 1:
        n_out = N // out_shards
        per_o = n_out // tn
        out_spec = pl.BlockSpec((None, tm, tn), lambda i, j, k: (j // per_o, i, j % per_o))
        out_shape = jax.ShapeDtypeStruct((out_shards, M, n_out), out_dtype)
    else:
        out_spec = pl.BlockSpec((tm, tn), lambda i, j, k: (i, j))
        out_shape = jax.ShapeDtypeStruct((M, N), out_dtype)

    if ta:
        dims = (((0,), (0,)), ((), ()))
    elif tb:
        dims = (((1,), (1,)), ((), ()))
    else:
        dims = (((1,), (0,)), ((), ()))
    scales = tuple(s for s, _ in adds)

    def body(a_ref, b_ref, *rest):
        add_refs = rest[:n_add]
        o_ref = rest[n_add]
        acc = rest[n_add + 1]
        k = pl.program_id(2)

        @pl.when(k == 0)
        def _():
            acc[...] = jnp.zeros_like(acc)

        acc[...] += lax.dot_general(a_ref[...].astype(BF16), b_ref[...].astype(BF16), dims,
                                    preferred_element_type=F32)

        @pl.when(k == nk - 1)
        def _():
            r = acc[...]
            for s, ref in zip(scales, add_refs):
                r = r + s * ref[...].astype(F32)
            o_ref[...] = r.astype(out_dtype)

    return pl.pallas_call(
        body, name=name, out_shape=out_shape, grid=(M // tm, N // tn, nk),
        in_specs=[a_spec, b_spec] + add_specs, out_specs=out_spec,
        scratch_shapes=[pltpu.VMEM((tm, tn), F32)],
        compiler_params=_cparams(("parallel", "parallel", "arbitrary")),
    )(a, b, *[x for _, x in adds])


def _cast_bf16(w, name):
    R, C = w.shape
    tr = _row_tile(R, C * 4, 16)

    def body(w_ref, o_ref):
        o_ref[...] = w_ref[...].astype(BF16)

    return pl.pallas_call(
        body, name=name, out_shape=jax.ShapeDtypeStruct((R, C), BF16), grid=(R // tr,),
        in_specs=[pl.BlockSpec((tr, C), lambda r: (r, 0))], out_specs=pl.BlockSpec((tr, C), lambda r: (r, 0)),
        compiler_params=_cparams(("parallel",)),
    )(w)


def _pair_sum(g, la, c_arr, name):
    S, R, C = g.shape
    half = R // 2
    tr = _row_tile(half, C * 4, 16)
    nrt = half // tr

    def body(c_ref, g_ref, la_ref, o_ref):
        o_ref[...] = (g_ref[...].astype(F32) + la_ref[...].astype(F32)).astype(BF16)

    gs = pltpu.PrefetchScalarGridSpec(
        num_scalar_prefetch=1, grid=(S, nrt),
        in_specs=[pl.BlockSpec((None, tr, C), lambda s, r, c: (s, c[0] * nrt + r, 0)),
                  pl.BlockSpec((None, tr, C), lambda s, r, c: (s, r, 0))],
        out_specs=pl.BlockSpec((None, tr, C), lambda s, r, c: (s, r, 0)))
    return pl.pallas_call(body, name=name, out_shape=jax.ShapeDtypeStruct((S, half, C), BF16), grid_spec=gs,
                          compiler_params=_cparams(("parallel", "parallel")))(c_arr, g, la)


def _shard_sum(cp, lb, j_arr, name):
    S, h, C = cp.shape
    tr = _row_tile(h, C * 4, 16)

    def body(j_ref, cp_ref, l0, l1, l2, o_ref):
        o_ref[...] = ((cp_ref[...].astype(F32) + l0[...].astype(F32)) + l1[...].astype(F32)) + l2[...].astype(F32)

    def lspec(kk):
        return pl.BlockSpec((None, tr, C), lambda r, j: (kk, r, 0))

    gs = pltpu.PrefetchScalarGridSpec(
        num_scalar_prefetch=1, grid=(h // tr,),
        in_specs=[pl.BlockSpec((None, tr, C), lambda r, j: (j[0], r, 0)), lspec(0), lspec(1), lspec(2)],
        out_specs=pl.BlockSpec((tr, C), lambda r, j: (r, 0)))
    return pl.pallas_call(body, name=name, out_shape=jax.ShapeDtypeStruct((h, C), F32), grid_spec=gs,
                          compiler_params=_cparams(("parallel",)))(j_arr, cp, lb, lb, lb)


def _slot_sum(slots, name):
    n, R, C = slots.shape
    tr = _row_tile(R, C * 4, 8, budget=512 * 1024)

    def body(*refs):
        o_ref = refs[n]
        r = refs[0][...]
        for d in range(1, n):
            r = r + refs[d][...]
        o_ref[...] = r

    def sspec(d):
        return pl.BlockSpec((None, tr, C), lambda r: (d, r, 0))

    return pl.pallas_call(
        body, name=name, out_shape=jax.ShapeDtypeStruct((R, C), F32), grid=(R // tr,),
        in_specs=[sspec(d) for d in range(n)], out_specs=pl.BlockSpec((tr, C), lambda r: (r, 0)),
        compiler_params=_cparams(("parallel",)),
    )(*([slots] * n))


ANY = pl.BlockSpec(memory_space=pl.ANY)


def _place():
    x, y, c = lax.axis_index("x"), lax.axis_index("y"), lax.axis_index("c")
    chips = [(1 - x, y), (x, 1 - y), (1 - x, 1 - y)]
    return x, y, c, chips


def _all_gather_weights(shards):
    n = len(shards)

    def body(*refs):
        w = refs[:n]
        out = refs[n:2 * n]
        local_sem, s_ici, r_ici, s_d2d, r_d2d = refs[2 * n:]
        x, y, c, chips = _place()
        j_me = 2 * x + y
        locals_ = []
        for i in range(n):
            cp = pltpu.make_async_copy(w[i], out[i].at[j_me], local_sem.at[i])
            cp.start()
            locals_.append(cp)
        sends = []
        for i in range(n):
            half = w[i].shape[0] // 2
            rows = pl.ds(c * half, half)
            for kk, (px, py) in enumerate(chips):
                cp = pltpu.make_async_remote_copy(
                    src_ref=w[i].at[rows, :], dst_ref=out[i].at[j_me, rows, :],
                    send_sem=s_ici.at[3 * i + kk], recv_sem=r_ici.at[3 * i + kk],
                    device_id=(px, py, c), device_id_type=MESH)
                cp.start()
                sends.append(cp)
        for i in range(n):
            half = w[i].shape[0] // 2
            rows = pl.ds(c * half, half)
            for kk, (px, py) in enumerate(chips):
                j_src = 2 * px + py
                landed = out[i].at[j_src, rows, :]
                pltpu.make_async_remote_copy(
                    src_ref=landed, dst_ref=landed, send_sem=s_ici.at[3 * i + kk], recv_sem=r_ici.at[3 * i + kk],
                    device_id=(px, py, c), device_id_type=MESH).wait_recv()
                fw = pltpu.make_async_remote_copy(
                    src_ref=landed, dst_ref=landed, send_sem=s_d2d.at[3 * i + kk], recv_sem=r_d2d.at[3 * i + kk],
                    device_id=(x, y, 1 - c), device_id_type=MESH)
                fw.start()
                sends.append(fw)
        for i in range(n):
            half = w[i].shape[0] // 2
            rows_sib = pl.ds((1 - c) * half, half)
            for kk, (px, py) in enumerate(chips):
                j_src = 2 * px + py
                landed = out[i].at[j_src, rows_sib, :]
                pltpu.make_async_remote_copy(
                    src_ref=landed, dst_ref=landed, send_sem=s_d2d.at[3 * i + kk], recv_sem=r_d2d.at[3 * i + kk],
                    device_id=(x, y, 1 - c), device_id_type=MESH).wait_recv()
        for cp in sends:
            cp.wait_send()
        for cp in locals_:
            cp.wait()

    out_shape = [jax.ShapeDtypeStruct((N_SHARDS,) + s.shape, s.dtype) for s in shards]
    return pl.pallas_call(
        body, name="all_gather_weights", out_shape=out_shape,
        in_specs=[ANY] * n, out_specs=[ANY] * n,
        scratch_shapes=[pltpu.SemaphoreType.DMA((n,))] + [pltpu.SemaphoreType.DMA((3 * n,))] * 4,
    )(*shards)


def _all_gather_small(shards):
    n = len(shards)

    def body(*refs):
        w = refs[:n]
        out = refs[n:2 * n]
        local_sem, s_sem, r_sem = refs[2 * n:]
        x, y, c, chips = _place()
        j_me = 2 * x + y
        cps = []
        for i in range(n):
            lc = pltpu.make_async_copy(w[i], out[i].at[j_me], local_sem.at[i])
            lc.start()
            cps.append(lc)
        sends = []
        for i in range(n):
            for kk, (px, py) in enumerate(chips):
                cp = pltpu.make_async_remote_copy(
                    src_ref=w[i], dst_ref=out[i].at[j_me], send_sem=s_sem.at[3 * i + kk],
                    recv_sem=r_sem.at[3 * i + kk], device_id=(px, py, c), device_id_type=MESH)
                cp.start()
                sends.append(cp)
        for i in range(n):
            for kk, (px, py) in enumerate(chips):
                sends[3 * i + kk].wait_send()
                pltpu.make_async_remote_copy(
                    src_ref=w[i], dst_ref=out[i].at[2 * px + py], send_sem=s_sem.at[3 * i + kk],
                    recv_sem=r_sem.at[3 * i + kk], device_id=(px, py, c), device_id_type=MESH).wait_recv()
        for lc in cps:
            lc.wait()

    out_shape = [jax.ShapeDtypeStruct((N_SHARDS,) + s.shape, s.dtype) for s in shards]
    return pl.pallas_call(
        body, name="all_gather_conv_weights", out_shape=out_shape, in_specs=[ANY] * n, out_specs=[ANY] * n,
        scratch_shapes=[pltpu.SemaphoreType.DMA((n,)), pltpu.SemaphoreType.DMA((3 * n,)),
                        pltpu.SemaphoreType.DMA((3 * n,))],
    )(*shards)


def _pair_exchange(grads):
    n = len(grads)

    def body(*refs):
        g = refs[:n]
        la = refs[n:2 * n]
        s_sem, r_sem = refs[2 * n:]
        x, y, c, _ = _place()
        cps = []
        for i in range(n):
            half = g[i].shape[1] // 2
            cp = pltpu.make_async_remote_copy(
                src_ref=g[i].at[:, pl.ds((1 - c) * half, half), :], dst_ref=la[i],
                send_sem=s_sem.at[i], recv_sem=r_sem.at[i], device_id=(x, y, 1 - c), device_id_type=MESH)
            cp.start()
            cps.append(cp)
        for cp in cps:
            cp.wait()

    out_shape = [jax.ShapeDtypeStruct((s.shape[0], s.shape[1] // 2, s.shape[2]), s.dtype) for s in grads]
    return pl.pallas_call(
        body, name="grad_pair_exchange", out_shape=out_shape, in_specs=[ANY] * n, out_specs=[ANY] * n,
        scratch_shapes=[pltpu.SemaphoreType.DMA((n,)), pltpu.SemaphoreType.DMA((n,))],
    )(*grads)


def _shard_exchange(cps_in):
    n = len(cps_in)

    def body(*refs):
        cp_ref = refs[:n]
        lb = refs[n:2 * n]
        s_sem, r_sem = refs[2 * n:]
        x, y, c, chips = _place()
        cps = []
        for i in range(n):
            for kk, (px, py) in enumerate(chips):
                cp = pltpu.make_async_remote_copy(
                    src_ref=cp_ref[i].at[2 * px + py], dst_ref=lb[i].at[kk],
                    send_sem=s_sem.at[3 * i + kk], recv_sem=r_sem.at[3 * i + kk],
                    device_id=(px, py, c), device_id_type=MESH)
                cp.start()
                cps.append(cp)
        for cp in cps:
            cp.wait()

    out_shape = [jax.ShapeDtypeStruct((3,) + s.shape[1:], s.dtype) for s in cps_in]
    return pl.pallas_call(
        body, name="grad_shard_exchange", out_shape=out_shape, in_specs=[ANY] * n, out_specs=[ANY] * n,
        scratch_shapes=[pltpu.SemaphoreType.DMA((3 * n,)), pltpu.SemaphoreType.DMA((3 * n,))],
    )(*cps_in)


def _share_halves(halves):
    n = len(halves)

    def body(*refs):
        hv = refs[:n]
        out = refs[n:2 * n]
        l_sem, s_sem, r_sem = refs[2 * n:]
        x, y, c, _ = _place()
        cps, lcs = [], []
        for i in range(n):
            h = hv[i].shape[0]
            lc = pltpu.make_async_copy(hv[i], out[i].at[pl.ds(c * h, h), :], l_sem.at[i])
            lc.start()
            lcs.append(lc)
            cp = pltpu.make_async_remote_copy(
                src_ref=hv[i], dst_ref=out[i].at[pl.ds(c * h, h), :],
                send_sem=s_sem.at[i], recv_sem=r_sem.at[i], device_id=(x, y, 1 - c), device_id_type=MESH)
            cp.start()
            cps.append(cp)
        for i in range(n):
            h = hv[i].shape[0]
            cps[i].wait_send()
            pltpu.make_async_remote_copy(
                src_ref=hv[i], dst_ref=out[i].at[pl.ds((1 - c) * h, h), :],
                send_sem=s_sem.at[i], recv_sem=r_sem.at[i], device_id=(x, y, 1 - c), device_id_type=MESH).wait_recv()
            lcs[i].wait()

    out_shape = [jax.ShapeDtypeStruct((2 * s.shape[0], s.shape[1]), s.dtype) for s in halves]
    return pl.pallas_call(
        body, name="grad_share_halves", out_shape=out_shape, in_specs=[ANY] * n, out_specs=[ANY] * n,
        scratch_shapes=[pltpu.SemaphoreType.DMA((n,))] * 3,
    )(*halves)


def _broadcast_small(vec):
    def body(v_ref, out_ref, l_sem, s_sem, r_sem):
        x, y, c, _ = _place()
        me = 4 * x + 2 * y + c
        lc = pltpu.make_async_copy(v_ref, out_ref.at[me], l_sem)
        lc.start()
        cps = []
        for r in range(1, N_DEV):
            px, py, pc = x ^ ((r >> 2) & 1), y ^ ((r >> 1) & 1), c ^ (r & 1)
            cp = pltpu.make_async_remote_copy(
                src_ref=v_ref, dst_ref=out_ref.at[me], send_sem=s_sem.at[r - 1], recv_sem=r_sem.at[r - 1],
                device_id=(px, py, pc), device_id_type=MESH)
            cp.start()
            cps.append(cp)
        for r in range(1, N_DEV):
            px, py, pc = x ^ ((r >> 2) & 1), y ^ ((r >> 1) & 1), c ^ (r & 1)
            cps[r - 1].wait_send()
            pltpu.make_async_remote_copy(
                src_ref=v_ref, dst_ref=out_ref.at[4 * px + 2 * py + pc], send_sem=s_sem.at[r - 1],
                recv_sem=r_sem.at[r - 1], device_id=(px, py, pc), device_id_type=MESH).wait_recv()
        lc.wait()

    return pl.pallas_call(
        body, name="small_grad_broadcast", out_shape=jax.ShapeDtypeStruct((N_DEV,) + vec.shape, vec.dtype),
        in_specs=[ANY], out_specs=ANY,
        scratch_shapes=[pltpu.SemaphoreType.DMA, pltpu.SemaphoreType.DMA((N_DEV - 1,)),
                        pltpu.SemaphoreType.DMA((N_DEV - 1,))],
    )(vec)


def _attn_scores(q, kp, kc, slope, sink, mask_p, mask_c, dist_p, dist_c):
    scale = HEAD_DIM ** -0.5
    dn = (((1,), (1,)), ((), ()))
    sp = lax.dot_general(q, kp, dn, preferred_element_type=F32) * scale - slope * dist_p
    sc = lax.dot_general(q, kc, dn, preferred_element_type=F32) * scale - slope * dist_c
    sp = jnp.where(mask_p, sp, NEG)
    sc = jnp.where(mask_c, sc, NEG)
    m = jnp.maximum(jnp.maximum(jnp.max(sp, axis=1, keepdims=True), jnp.max(sc, axis=1, keepdims=True)), sink)
    ep = jnp.exp(sp - m)
    ec = jnp.exp(sc - m)
    es = jnp.exp(sink - m)
    inv = 1.0 / (jnp.sum(ep, axis=1, keepdims=True) + jnp.sum(ec, axis=1, keepdims=True) + es)
    return ep * inv, ec * inv, es * inv


def _attn_masks(n):
    ti = lax.broadcasted_iota(jnp.int32, (ATTN_BLOCK, ATTN_BLOCK), 0)
    sj = lax.broadcasted_iota(jnp.int32, (ATTN_BLOCK, ATTN_BLOCK), 1)
    mask_c = sj <= ti
    mask_p = jnp.logical_and(sj > ti, n > 0)
    dist_c = (ti - sj).astype(F32)
    dist_p = dist_c + float(ATTN_BLOCK)
    return mask_p, mask_c, dist_p, dist_c


def _attn_specs(T, d_attn, d_kv, q_blk, k_blk, v_blk):
    bq = pl.BlockSpec((ATTN_BLOCK, d_attn), lambda n: (n, q_blk))
    kp = pl.BlockSpec((ATTN_BLOCK, d_kv), lambda n: (jnp.maximum(n - 1, 0), k_blk))
    kc = pl.BlockSpec((ATTN_BLOCK, d_kv), lambda n: (n, k_blk))
    vp = pl.BlockSpec((ATTN_BLOCK, d_kv), lambda n: (jnp.maximum(n - 1, 0), v_blk))
    vc = pl.BlockSpec((ATTN_BLOCK, d_kv), lambda n: (n, v_blk))
    return bq, kp, kc, vp, vc


def _attn_fwd(proj, sinks, nq, cols):
    T = proj.shape[0]
    nkv = nq // GROUP
    d_attn, d_kv = nq * HEAD_DIM, nkv * HEAD_DIM
    q_off, k_off, v_off = cols
    bq, kp, kc, vp, vc = _attn_specs(T, d_attn, d_kv, q_off // d_attn, k_off // d_kv, v_off // d_kv)

    def body(sink_ref, q_ref, kp_ref, kc_ref, vp_ref, vc_ref, o_ref):
        n = pl.program_id(0)
        masks = _attn_masks(n)
        for g in range(nkv):
            ks = slice(g * HEAD_DIM, (g + 1) * HEAD_DIM)
            k_p, k_c = kp_ref[:, ks].astype(BF16), kc_ref[:, ks].astype(BF16)
            v_p, v_c = vp_ref[:, ks].astype(BF16), vc_ref[:, ks].astype(BF16)
            for hh in range(GROUP):
                h = g * GROUP + hh
                hs = slice(h * HEAD_DIM, (h + 1) * HEAD_DIM)
                slope = 2.0 ** (-8.0 * (h + 1) / nq)
                q = q_ref[:, hs].astype(BF16)
                pp, pc, _ = _attn_scores(q, k_p, k_c, slope, sink_ref[0, h], *masks)
                o = (jnp.dot(pp.astype(BF16), v_p, preferred_element_type=F32)
                     + jnp.dot(pc.astype(BF16), v_c, preferred_element_type=F32))
                o_ref[:, hs] = o.astype(BF16)

    return pl.pallas_call(
        body, name="attn_fwd", out_shape=jax.ShapeDtypeStruct((T, d_attn), BF16), grid=(T // ATTN_BLOCK,),
        in_specs=[pl.BlockSpec(memory_space=pltpu.SMEM), bq, kp, kc, vp, vc],
        out_specs=pl.BlockSpec((ATTN_BLOCK, d_attn), lambda n: (n, 0)),
        compiler_params=_cparams(("parallel",)),
    )(sinks, proj, proj, proj, proj, proj)


def _attn_bwd(proj, d_attn_out, sinks, nq, cols):
    T = proj.shape[0]
    nkv = nq // GROUP
    d_attn, d_kv = nq * HEAD_DIM, nkv * HEAD_DIM
    q_off, k_off, v_off = cols
    bq, kp, kc, vp, vc = _attn_specs(T, d_attn, d_kv, q_off // d_attn, k_off // d_kv, v_off // d_kv)
    scale = HEAD_DIM ** -0.5
    dn_t = (((1,), (1,)), ((), ()))
    dn_r = (((0,), (0,)), ((), ()))

    def body(sink_ref, q_ref, kp_ref, kc_ref, vp_ref, vc_ref, do_ref, dq_ref, dk_ref, dv_ref, ds_ref):
        n = pl.program_id(0)

        @pl.when(n == 0)
        def _():
            dk_ref[...] = jnp.zeros_like(dk_ref)
            dv_ref[...] = jnp.zeros_like(dv_ref)
            ds_ref[...] = jnp.zeros_like(ds_ref)

        masks = _attn_masks(n)
        rows_c = pl.ds(pl.multiple_of(n * ATTN_BLOCK, ATTN_BLOCK), ATTN_BLOCK)
        rows_p = pl.ds(pl.multiple_of(jnp.maximum(n - 1, 0) * ATTN_BLOCK, ATTN_BLOCK), ATTN_BLOCK)
        lane = lax.broadcasted_iota(jnp.int32, ds_ref.shape, 1)
        srow = lax.broadcasted_iota(jnp.int32, ds_ref.shape, 0)
        ds_acc = jnp.zeros(ds_ref.shape, F32)
        for g in range(nkv):
            ks = slice(g * HEAD_DIM, (g + 1) * HEAD_DIM)
            k_p, k_c = kp_ref[:, ks].astype(BF16), kc_ref[:, ks].astype(BF16)
            v_p, v_c = vp_ref[:, ks].astype(BF16), vc_ref[:, ks].astype(BF16)
            dkp = jnp.zeros((ATTN_BLOCK, HEAD_DIM), F32)
            dkc = jnp.zeros((ATTN_BLOCK, HEAD_DIM), F32)
            dvp = jnp.zeros((ATTN_BLOCK, HEAD_DIM), F32)
            dvc = jnp.zeros((ATTN_BLOCK, HEAD_DIM), F32)
            for hh in range(GROUP):
                h = g * GROUP + hh
                hs = slice(h * HEAD_DIM, (h + 1) * HEAD_DIM)
                slope = 2.0 ** (-8.0 * (h + 1) / nq)
                q = q_ref[:, hs].astype(BF16)
                do = do_ref[:, hs].astype(BF16)
                pp, pc, ps = _attn_scores(q, k_p, k_c, slope, sink_ref[0, h], *masks)
                dpp = lax.dot_general(do, v_p, dn_t, preferred_element_type=F32)
                dpc = lax.dot_general(do, v_c, dn_t, preferred_element_type=F32)
                delta = jnp.sum(pp * dpp, axis=1, keepdims=True) + jnp.sum(pc * dpc, axis=1, keepdims=True)
                dsp = (pp * (dpp - delta)).astype(BF16)
                dsc = (pc * (dpc - delta)).astype(BF16)
                dsink = -jnp.sum(ps * delta)
                ds_acc = ds_acc + jnp.where(jnp.logical_and(lane == h, srow == 0), dsink, 0.0)
                dq = (jnp.dot(dsp, k_p, preferred_element_type=F32)
                      + jnp.dot(dsc, k_c, preferred_element_type=F32)) * scale
                dq_ref[:, hs] = dq.astype(BF16)
                dkp = dkp + lax.dot_general(dsp, q, dn_r, preferred_element_type=F32) * scale
                dkc = dkc + lax.dot_general(dsc, q, dn_r, preferred_element_type=F32) * scale
                dvp = dvp + lax.dot_general(pp.astype(BF16), do, dn_r, preferred_element_type=F32)
                dvc = dvc + lax.dot_general(pc.astype(BF16), do, dn_r, preferred_element_type=F32)
            dk_ref[rows_p, ks] += dkp
            dv_ref[rows_p, ks] += dvp
            dk_ref[rows_c, ks] += dkc
            dv_ref[rows_c, ks] += dvc
        ds_ref[...] += ds_acc

    out_shape = (jax.ShapeDtypeStruct((T, d_attn), BF16), jax.ShapeDtypeStruct((T, d_kv), F32),
                 jax.ShapeDtypeStruct((T, d_kv), F32), jax.ShapeDtypeStruct((8, LANES), F32))
    return pl.pallas_call(
        body, name="attn_bwd", out_shape=out_shape, grid=(T // ATTN_BLOCK,),
        in_specs=[pl.BlockSpec(memory_space=pltpu.SMEM), bq, kp, kc, vp, vc,
                  pl.BlockSpec((ATTN_BLOCK, d_attn), lambda n: (n, 0))],
        out_specs=(pl.BlockSpec((ATTN_BLOCK, d_attn), lambda n: (n, 0)),
                   pl.BlockSpec((T, d_kv), lambda n: (0, 0)), pl.BlockSpec((T, d_kv), lambda n: (0, 0)),
                   pl.BlockSpec((8, LANES), lambda n: (0, 0))),
        compiler_params=_cparams(("arbitrary",)),
    )(sinks, proj, proj, proj, proj, proj, d_attn_out)


def _rnn_tile(T):
    return _pick(T, (256, 128))


def _rnn_gates(x_ext, cw_ref, cb_ref, wa_ref, wi_ref, ba_ref, bi_ref, lam_ref, tt):
    xs = [pltpu.roll(x_ext, 3 - k, 0)[8:, :] if k < 3 else x_ext[8:, :] for k in range(4)]
    cx = cb_ref[...] + xs[0] * cw_ref[0:1, :]
    for k in range(1, 4):
        cx = cx + xs[k] * cw_ref[k:k + 1, :]
    cxb = cx.astype(BF16)
    r = jax.nn.sigmoid(jnp.dot(cxb, wa_ref[...], preferred_element_type=F32) + ba_ref[...])
    i = jax.nn.sigmoid(jnp.dot(cxb, wi_ref[...], preferred_element_type=F32) + bi_ref[...])
    lam = lam_ref[...]
    sp = jnp.maximum(-lam, 0.0) + jnp.log1p(jnp.exp(-jnp.abs(lam)))
    log_a = -LRU_C * r * sp
    a = jnp.exp(log_a)
    z = 2.0 * log_a
    em1 = jnp.where(z > -1e-2, z * (1.0 + z * (0.5 + z * (1.0 / 6.0 + z * (1.0 / 24.0)))), jnp.exp(z) - 1.0)
    s = jnp.sqrt(-em1)
    return xs, cx, r, i, sp, a, s


def _rnn_specs(T, gw, tt, rx_blk, ry_blk, rev):
    nT = T // tt
    hb = tt // 8

    def tile(t):
        return (nT - 1 - t) if rev else t

    rx = pl.BlockSpec((tt, gw), lambda g, t: (tile(t), rx_blk + g))
    rx_halo = pl.BlockSpec((8, gw), lambda g, t: (jnp.maximum(tile(t) * hb - 1, 0), rx_blk + g))
    ry = pl.BlockSpec((tt, gw), lambda g, t: (tile(t), ry_blk + g))
    cw = pl.BlockSpec((4, gw), lambda g, t: (0, g))
    vec = pl.BlockSpec((1, gw), lambda g, t: (0, g))
    wg = pl.BlockSpec((None, gw, gw), lambda g, t: (g, 0, 0))
    act = pl.BlockSpec((tt, gw), lambda g, t: (tile(t), g))
    act_halo = pl.BlockSpec((8, gw), lambda g, t: (jnp.maximum(tile(t) * hb - 1, 0), g))
    return rx, rx_halo, ry, cw, vec, wg, act, act_halo, tile


def _rnn_fwd(proj, cols, conv_w, conv_b, wa_g, wi_g, ba, bi, lam):
    T = proj.shape[0]
    G, gw, _ = wa_g.shape
    d_rnn = G * gw
    tt = _rnn_tile(T)
    rx_off, ry_off = cols
    rx, rx_halo, ry, cw, vec, wg, act, _, _ = _rnn_specs(T, gw, tt, rx_off // gw, ry_off // gw, False)

    def body(rx_ref, rxh_ref, ry_ref, cw_ref, cb_ref, wa_ref, wi_ref, ba_ref, bi_ref, lam_ref,
             b_ref, h_ref, carry):
        t = pl.program_id(1)

        @pl.when(t == 0)
        def _():
            carry[...] = jnp.zeros_like(carry)

        halo = jnp.where(t > 0, rxh_ref[...], 0.0)
        x_ext = jnp.concatenate([halo, rx_ref[...]], axis=0)
        _, cx, _, i, _, a, s = _rnn_gates(x_ext, cw_ref, cb_ref, wa_ref, wi_ref, ba_ref, bi_ref, lam_ref, tt)
        acc_a, acc_b = a, s * (i * cx)
        d = 1
        while d < tt:
            acc_b = acc_a * _shift_down(acc_b, d, 0.0) + acc_b
            acc_a = acc_a * _shift_down(acc_a, d, 1.0)
            d *= 2
        h = acc_b + acc_a * carry[7:8, :]
        carry[...] = h[tt - 8:, :]
        h_ref[...] = h
        b_ref[...] = (h * _gelu(ry_ref[...])).astype(BF16)

    return pl.pallas_call(
        body, name="rnn_fwd",
        out_shape=(jax.ShapeDtypeStruct((T, d_rnn), BF16), jax.ShapeDtypeStruct((T, d_rnn), F32)),
        grid=(G, T // tt),
        in_specs=[rx, rx_halo, ry, cw, vec, wg, wg, vec, vec, vec], out_specs=(act, act),
        scratch_shapes=[pltpu.VMEM((8, gw), F32)],
        compiler_params=_cparams(("parallel", "arbitrary")),
    )(proj, proj, proj, conv_w, conv_b, wa_g, wi_g, ba, bi, lam)


def _rnn_bwd(proj, cols, h_all, d_b, conv_w, conv_b, wa_g, wi_g, ba, bi, lam):
    T = proj.shape[0]
    G, gw, _ = wa_g.shape
    d_rnn = G * gw
    tt = _rnn_tile(T)
    nT = T // tt
    rx_off, ry_off = cols
    rx, rx_halo, ry, cw, vec, wg, act, act_halo, _ = _rnn_specs(T, gw, tt, rx_off // gw, ry_off // gw, True)
    dn_t = (((1,), (1,)), ((), ()))
    dn_r = (((0,), (0,)), ((), ()))

    def body(rx_ref, rxh_ref, ry_ref, h_ref, hh_ref, db_ref, cw_ref, cb_ref, wa_ref, wi_ref, ba_ref, bi_ref, lam_ref,
             drx_ref, dry_ref, dcw_ref, dcb_ref, dba_ref, dbi_ref, dlam_ref, dwa_ref, dwi_ref,
             lam_carry, dcx_carry):
        t = pl.program_id(1)
        first_tile = t == nT - 1

        @pl.when(t == 0)
        def _():
            lam_carry[...] = jnp.zeros_like(lam_carry)
            dcx_carry[...] = jnp.zeros_like(dcx_carry)
            dcw_ref[...] = jnp.zeros_like(dcw_ref)
            dcb_ref[...] = jnp.zeros_like(dcb_ref)
            dba_ref[...] = jnp.zeros_like(dba_ref)
            dbi_ref[...] = jnp.zeros_like(dbi_ref)
            dlam_ref[...] = jnp.zeros_like(dlam_ref)
            dwa_ref[...] = jnp.zeros_like(dwa_ref)
            dwi_ref[...] = jnp.zeros_like(dwi_ref)

        halo = jnp.where(first_tile, 0.0, rxh_ref[...])
        x_ext = jnp.concatenate([halo, rx_ref[...]], axis=0)
        xs, cx, r, i, sp, a, s = _rnn_gates(x_ext, cw_ref, cb_ref, wa_ref, wi_ref, ba_ref, bi_ref, lam_ref, tt)
        h = h_ref[...]
        h_halo = jnp.where(first_tile, 0.0, hh_ref[...])
        h_prev = pltpu.roll(jnp.concatenate([h_halo, h], axis=0), 1, 0)[8:, :]
        gel, dgel = _gelu_and_grad(ry_ref[...])
        d_b_t = db_ref[...]
        dry_ref[...] = (d_b_t * h * dgel).astype(BF16)
        dh = d_b_t * gel

        acc_c = _shift_up(a, 1, 1.0)
        acc_l = dh
        d = 1
        while d < tt:
            acc_l = acc_c * _shift_up(acc_l, d, 0.0) + acc_l
            acc_c = acc_c * _shift_up(acc_c, d, 1.0)
            d *= 2
        lam_t = acc_l + acc_c * lam_carry[0:1, :]
        lam_carry[...] = (a * lam_t)[0:8, :]

        icx = i * cx
        d_s = lam_t * icx
        d_i = lam_t * s * cx
        dcx = lam_t * s * i
        d_a = lam_t * h_prev - d_s * (a / s)
        dlog_a = d_a * a
        d_r = dlog_a * (-LRU_C * sp)
        lam = lam_ref[...]
        dlam_ref[...] += jnp.sum(dlog_a * r, axis=0, keepdims=True) * (LRU_C * jax.nn.sigmoid(-lam))
        dpr = d_r * r * (1.0 - r)
        dpi = d_i * i * (1.0 - i)
        dba_ref[...] += jnp.sum(dpr, axis=0, keepdims=True)
        dbi_ref[...] += jnp.sum(dpi, axis=0, keepdims=True)
        cxb = cx.astype(BF16)
        dprb, dpib = dpr.astype(BF16), dpi.astype(BF16)
        dwa_ref[...] += lax.dot_general(cxb, dprb, dn_r, preferred_element_type=F32)
        dwi_ref[...] += lax.dot_general(cxb, dpib, dn_r, preferred_element_type=F32)
        dcx = (dcx + lax.dot_general(dprb, wa_ref[...], dn_t, preferred_element_type=F32)
               + lax.dot_general(dpib, wi_ref[...], dn_t, preferred_element_type=F32))

        dcb_ref[...] += jnp.sum(dcx, axis=0, keepdims=True)
        for k in range(4):
            dcw_ref[k:k + 1, :] += jnp.sum(dcx * xs[k], axis=0, keepdims=True)
        d_ext = jnp.concatenate([dcx, dcx_carry[...]], axis=0)
        drx = dcx * cw_ref[3:4, :]
        for k in range(3):
            drx = drx + pltpu.roll(d_ext, tt + 8 - (3 - k), 0)[:tt, :] * cw_ref[k:k + 1, :]
        drx_ref[...] = drx.astype(BF16)
        dcx_carry[...] = dcx[0:8, :]

    out_shape = (jax.ShapeDtypeStruct((T, d_rnn), BF16), jax.ShapeDtypeStruct((T, d_rnn), BF16),
                 jax.ShapeDtypeStruct((4, d_rnn), F32), jax.ShapeDtypeStruct((1, d_rnn), F32),
                 jax.ShapeDtypeStruct((1, d_rnn), F32), jax.ShapeDtypeStruct((1, d_rnn), F32),
                 jax.ShapeDtypeStruct((1, d_rnn), F32), jax.ShapeDtypeStruct((G, gw, gw), F32),
                 jax.ShapeDtypeStruct((G, gw, gw), F32))
    return pl.pallas_call(
        body, name="rnn_bwd", out_shape=out_shape, grid=(G, nT),
        in_specs=[rx, rx_halo, ry, act, act_halo, act, cw, vec, wg, wg, vec, vec, vec],
        out_specs=(act, act, cw, vec, vec, vec, vec, wg, wg),
        scratch_shapes=[pltpu.VMEM((8, gw), F32), pltpu.VMEM((8, gw), F32)],
        compiler_params=_cparams(("parallel", "arbitrary")),
    )(proj, proj, proj, h_all, h_all, d_b, conv_w, conv_b, wa_g, wi_g, ba, bi, lam)


def _merge_fwd(proj, gl_off, b_gate, y_attn, y_rnn):
    T, D = y_attn.shape
    tm = _pick(T, (256, 128))
    ct = _pick(math.gcd(gl_off, D), (512, 256, 128))
    oa, orr, nd = gl_off // ct, (gl_off + D) // ct, D // ct

    def body(ga_ref, gr_ref, ba_ref, br_ref, ya_ref, yr_ref, m_ref):
        ga = jax.nn.sigmoid(ga_ref[...] + ba_ref[...])
        gr = jax.nn.sigmoid(gr_ref[...] + br_ref[...])
        m_ref[...] = (ga * ya_ref[...] + gr * yr_ref[...]).astype(BF16)

    blk = pl.BlockSpec((tm, ct), lambda i, j: (i, j))
    return pl.pallas_call(
        body, name="merge_fwd", out_shape=jax.ShapeDtypeStruct((T, D), BF16), grid=(T // tm, nd),
        in_specs=[pl.BlockSpec((tm, ct), lambda i, j: (i, oa + j)), pl.BlockSpec((tm, ct), lambda i, j: (i, orr + j)),
                  pl.BlockSpec((1, ct), lambda i, j: (0, j)), pl.BlockSpec((1, ct), lambda i, j: (0, nd + j)),
                  blk, blk],
        out_specs=blk, compiler_params=_cparams(("parallel", "parallel")),
    )(proj, proj, b_gate, b_gate, y_attn, y_rnn)


def _merge_bwd(proj, gl_off, b_gate, y_attn, y_rnn, d_m):
    T, D = y_attn.shape
    tm = _pick(T, (256, 128))
    ct = _pick(math.gcd(gl_off, D), (512, 256, 128))
    oa, orr, nd = gl_off // ct, (gl_off + D) // ct, D // ct

    def body(ga_ref, gr_ref, ba_ref, br_ref, ya_ref, yr_ref, dm_ref,
             dya_ref, dyr_ref, dga_ref, dgr_ref, dba_ref, dbr_ref):
        i = pl.program_id(1)

        @pl.when(i == 0)
        def _():
            dba_ref[...] = jnp.zeros_like(dba_ref)
            dbr_ref[...] = jnp.zeros_like(dbr_ref)

        ga = jax.nn.sigmoid(ga_ref[...] + ba_ref[...])
        gr = jax.nn.sigmoid(gr_ref[...] + br_ref[...])
        dm = dm_ref[...]
        dya_ref[...] = (dm * ga).astype(BF16)
        dyr_ref[...] = (dm * gr).astype(BF16)
        dga = dm * ya_ref[...] * ga * (1.0 - ga)
        dgr = dm * yr_ref[...] * gr * (1.0 - gr)
        dga_ref[...] = dga.astype(BF16)
        dgr_ref[...] = dgr.astype(BF16)
        dba_ref[...] += jnp.sum(dga, axis=0, keepdims=True)
        dbr_ref[...] += jnp.sum(dgr, axis=0, keepdims=True)

    blk = pl.BlockSpec((tm, ct), lambda j, i: (i, j))
    vec = pl.BlockSpec((1, ct), lambda j, i: (0, j))
    act = jax.ShapeDtypeStruct((T, D), BF16)
    v1 = jax.ShapeDtypeStruct((1, D), F32)
    return pl.pallas_call(
        body, name="merge_bwd", out_shape=(act, act, act, act, v1, v1), grid=(nd, T // tm),
        in_specs=[pl.BlockSpec((tm, ct), lambda j, i: (i, oa + j)), pl.BlockSpec((tm, ct), lambda j, i: (i, orr + j)),
                  vec, pl.BlockSpec((1, ct), lambda j, i: (0, nd + j)), blk, blk, blk],
        out_specs=(blk, blk, blk, blk, vec, vec),
        compiler_params=_cparams(("parallel", "arbitrary")),
    )(proj, proj, b_gate, b_gate, y_attn, y_rnn, d_m)


def _ln_fwd(x_res, delta, g, b, name):
    T, D = x_res.shape
    tm = _pick(T, (256, 128))

    def body(x_ref, d_ref, g_ref, b_ref, y_ref, xh_ref, rs_ref):
        z = ALPHA * x_ref[...] + d_ref[...]
        mu = jnp.mean(z, axis=1, keepdims=True)
        zc = z - mu
        var = jnp.mean(zc * zc, axis=1, keepdims=True)
        rstd = lax.rsqrt(var + LN_EPS)
        xh = zc * rstd
        xh_ref[...] = xh
        rs_ref[...] = rstd
        y_ref[...] = xh * g_ref[...] + b_ref[...]

    row = pl.BlockSpec((tm, D), lambda i: (i, 0))
    vec = pl.BlockSpec((1, D), lambda i: (0, 0))
    return pl.pallas_call(
        body, name=name,
        out_shape=(jax.ShapeDtypeStruct((T, D), F32), jax.ShapeDtypeStruct((T, D), F32),
                   jax.ShapeDtypeStruct((T, 1), F32)),
        grid=(T // tm,), in_specs=[row, row, vec, vec],
        out_specs=(row, row, pl.BlockSpec((tm, 1), lambda i: (i, 0))),
        compiler_params=_cparams(("parallel",)),
    )(x_res, delta, g, b)


def _ln_bwd_rows(dy, xh, rstd, g):
    dxh = dy * g
    m1 = jnp.mean(dxh, axis=1, keepdims=True)
    m2 = jnp.mean(dxh * xh, axis=1, keepdims=True)
    return rstd * (dxh - m1 - xh * m2)


def _ln_loss_bwd(x_res, delta, g, b, target):
    T, D = x_res.shape
    tm = _pick(T, (256, 128))

    def body(x_ref, d_ref, g_ref, b_ref, t_ref, dz_ref, loss_ref, dg_ref, db_ref):
        i = pl.program_id(0)

        @pl.when(i == 0)
        def _():
            loss_ref[...] = jnp.zeros_like(loss_ref)
            dg_ref[...] = jnp.zeros_like(dg_ref)
            db_ref[...] = jnp.zeros_like(db_ref)

        z = ALPHA * x_ref[...] + d_ref[...]
        mu = jnp.mean(z, axis=1, keepdims=True)
        zc = z - mu
        var = jnp.mean(zc * zc, axis=1, keepdims=True)
        rstd = lax.rsqrt(var + LN_EPS)
        xh = zc * rstd
        gv = g_ref[...]
        err = xh * gv + b_ref[...] - t_ref[...]
        loss_ref[...] += 0.5 * jnp.sum(jnp.mean(err * err, axis=1, keepdims=True))
        dy = err * (1.0 / D)
        dg_ref[...] += jnp.sum(dy * xh, axis=0, keepdims=True)
        db_ref[...] += jnp.sum(dy, axis=0, keepdims=True)
        dz_ref[...] = _ln_bwd_rows(dy, xh, rstd, gv)

    row = pl.BlockSpec((tm, D), lambda i: (i, 0))
    vec = pl.BlockSpec((1, D), lambda i: (0, 0))
    return pl.pallas_call(
        body, name="ln2_loss_bwd",
        out_shape=(jax.ShapeDtypeStruct((T, D), F32), jax.ShapeDtypeStruct((8, LANES), F32),
                   jax.ShapeDtypeStruct((1, D), F32), jax.ShapeDtypeStruct((1, D), F32)),
        grid=(T // tm,), in_specs=[row, row, vec, vec, row],
        out_specs=(row, pl.BlockSpec((8, LANES), lambda i: (0, 0)), vec, vec),
        compiler_params=_cparams(("arbitrary",)),
    )(x_res, delta, g, b, target)


def _ln_bwd(dy, xh, rstd, g):
    T, D = dy.shape
    tm = _pick(T, (256, 128))

    def body(dy_ref, xh_ref, rs_ref, g_ref, dz_ref, dg_ref, db_ref):
        i = pl.program_id(0)

        @pl.when(i == 0)
        def _():
            dg_ref[...] = jnp.zeros_like(dg_ref)
            db_ref[...] = jnp.zeros_like(db_ref)

        dyv, xhv = dy_ref[...], xh_ref[...]
        dg_ref[...] += jnp.sum(dyv * xhv, axis=0, keepdims=True)
        db_ref[...] += jnp.sum(dyv, axis=0, keepdims=True)
        dz_ref[...] = _ln_bwd_rows(dyv, xhv, rs_ref[...], g_ref[...])

    row = pl.BlockSpec((tm, D), lambda i: (i, 0))
    vec = pl.BlockSpec((1, D), lambda i: (0, 0))
    return pl.pallas_call(
        body, name="ln1_bwd",
        out_shape=(jax.ShapeDtypeStruct((T, D), F32), jax.ShapeDtypeStruct((1, D), F32),
                   jax.ShapeDtypeStruct((1, D), F32)),
        grid=(T // tm,), in_specs=[row, row, pl.BlockSpec((tm, 1), lambda i: (i, 0)), vec],
        out_specs=(row, vec, vec), compiler_params=_cparams(("arbitrary",)),
    )(dy, xh, rstd, g)


def _ffn_col_tile(T, d_ff):
    return _pick(d_ff, (256, 128)) if T >= 1024 else _pick(d_ff, (512, 256, 128))


def _ffn_gate(gp, cw_ref, cb_ref):
    return (cb_ref[...] + gp * cw_ref[2:3, :] + _shift_down(gp, 1) * cw_ref[1:2, :]
            + _shift_down(gp, 2) * cw_ref[0:1, :])


def _ffn_fwd(up, gpre, conv_w, conv_b):
    T, d_ff = up.shape
    ct = _ffn_col_tile(T, d_ff)

    def body(up_ref, gp_ref, cw_ref, cb_ref, f_ref):
        gate = _ffn_gate(gp_ref[...], cw_ref, cb_ref)
        f_ref[...] = (_gelu(gate) * up_ref[...]).astype(BF16)

    col = pl.BlockSpec((T, ct), lambda j: (0, j))
    return pl.pallas_call(
        body, name="ffn_act_fwd", out_shape=jax.ShapeDtypeStruct((T, d_ff), BF16), grid=(d_ff // ct,),
        in_specs=[col, col, pl.BlockSpec((3, ct), lambda j: (0, j)), pl.BlockSpec((1, ct), lambda j: (0, j))],
        out_specs=col, compiler_params=_cparams(("parallel",)),
    )(up, gpre, conv_w, conv_b)


def _ffn_bwd(up, gpre, conv_w, conv_b, d_f):
    T, d_ff = up.shape
    ct = _ffn_col_tile(T, d_ff)

    def body(up_ref, gp_ref, cw_ref, cb_ref, df_ref, dup_ref, dgp_ref, dcw_ref, dcb_ref):
        gp = gp_ref[...]
        gate = _ffn_gate(gp, cw_ref, cb_ref)
        gel, dgel = _gelu_and_grad(gate)
        df = df_ref[...]
        dup_ref[...] = (df * gel).astype(BF16)
        dgate = df * up_ref[...] * dgel
        dcb_ref[...] = jnp.sum(dgate, axis=0, keepdims=True)
        dcw_ref[2:3, :] = jnp.sum(dgate * gp, axis=0, keepdims=True)
        dcw_ref[1:2, :] = jnp.sum(dgate * _shift_down(gp, 1), axis=0, keepdims=True)
        dcw_ref[0:1, :] = jnp.sum(dgate * _shift_down(gp, 2), axis=0, keepdims=True)
        dgp = (dgate * cw_ref[2:3, :] + _shift_up(dgate, 1) * cw_ref[1:2, :]
               + _shift_up(dgate, 2) * cw_ref[0:1, :])
        dgp_ref[...] = dgp.astype(BF16)

    col = pl.BlockSpec((T, ct), lambda j: (0, j))
    w3 = pl.BlockSpec((3, ct), lambda j: (0, j))
    v1 = pl.BlockSpec((1, ct), lambda j: (0, j))
    return pl.pallas_call(
        body, name="ffn_act_bwd",
        out_shape=(jax.ShapeDtypeStruct((T, d_ff), BF16), jax.ShapeDtypeStruct((T, d_ff), BF16),
                   jax.ShapeDtypeStruct((3, d_ff), F32), jax.ShapeDtypeStruct((1, d_ff), F32)),
        grid=(d_ff // ct,), in_specs=[col, col, w3, v1, col], out_specs=(col, col, w3, v1),
        compiler_params=_cparams(("parallel",)),
    )(up, gpre, conv_w, conv_b, d_f)


def _adamw(w, g, m, v, name):
    R, C = w.shape
    tr = _row_tile(R, C * 4, 8, budget=1536 * 1024)
    c1 = 1.0 / (1.0 - ADAM_B1 ** ADAM_STEP)
    c2 = 1.0 / (1.0 - ADAM_B2 ** ADAM_STEP)

    def body(w_ref, g_ref, m_ref, v_ref, d_ref, nm_ref, nv_ref):
        gv = g_ref[...]
        nm = ADAM_B1 * m_ref[...] + (1.0 - ADAM_B1) * gv
        nv = ADAM_B2 * v_ref[...] + (1.0 - ADAM_B2) * (gv * gv)
        nm_ref[...] = nm
        nv_ref[...] = nv
        d_ref[...] = -ADAM_LR * ((nm * c1) / (jnp.sqrt(nv * c2) + ADAM_EPS) + ADAM_WD * w_ref[...])

    blk = pl.BlockSpec((tr, C), lambda r: (r, 0))
    sh = jax.ShapeDtypeStruct((R, C), F32)
    return pl.pallas_call(
        body, name=name, out_shape=(sh, sh, sh), grid=(R // tr,), in_specs=[blk] * 4, out_specs=(blk,) * 3,
        compiler_params=_cparams(("parallel",)),
    )(w, g, m, v)


def _group_blocks(w_blocks, per):
    nb, bw, _ = w_blocks.shape
    G = nb // per
    w4 = w_blocks.reshape(G, per, bw, bw)
    rows = []
    for p in range(per):
        parts = [w4[:, p] if q == p else jnp.zeros((G, bw, bw), w_blocks.dtype) for q in range(per)]
        rows.append(jnp.concatenate(parts, axis=2))
    return jnp.concatenate(rows, axis=1)


def _ungroup_blocks(w_groups, per):
    G, gw, _ = w_groups.shape
    bw = gw // per
    blocks = [w_groups[:, p * bw:(p + 1) * bw, p * bw:(p + 1) * bw] for p in range(per)]
    return jnp.stack(blocks, axis=1).reshape(G * per, bw, bw)


def _pack(parts):
    flat = jnp.concatenate([p.reshape(-1).astype(F32) for p in parts])
    n = flat.shape[0]
    rows = -(-n // LANES)
    rows = -(-rows // 8) * 8
    flat = jnp.pad(flat, (0, rows * LANES - n))
    return flat.reshape(rows, LANES)


def _unpack(packed, shapes):
    flat = packed.reshape(-1)
    out, off = [], 0
    for s in shapes:
        n = math.prod(s)
        out.append(flat[off:off + n].reshape(s))
        off += n
    return out


def kernel(x, w_in, b_gate, rnn_conv_w, rnn_conv_b, lru_wa, lru_ba, lru_wi, lru_bi, lru_lambda, attn_sinks, w_attn_proj, w_rnn_proj, w_out, ln1_g, ln1_b, ffn_w_up, ffn_w_gate, ffn_conv_w, ffn_conv_b, ffn_w_down, ln2_g, ln2_b, loss_target, m_w_in, m_b_gate, m_rnn_conv_w, m_rnn_conv_b, m_lru_wa, m_lru_ba, m_lru_wi, m_lru_bi, m_lru_lambda, m_attn_sinks, m_w_attn_proj, m_w_rnn_proj, m_w_out, m_ln1_g, m_ln1_b, m_ffn_w_up, m_ffn_w_gate, m_ffn_conv_w, m_ffn_conv_b, m_ffn_w_down, m_ln2_g, m_ln2_b, v_w_in, v_b_gate, v_rnn_conv_w, v_rnn_conv_b, v_lru_wa, v_lru_ba, v_lru_wi, v_lru_bi, v_lru_lambda, v_attn_sinks, v_w_attn_proj, v_w_rnn_proj, v_w_out, v_ln1_g, v_ln1_b, v_ffn_w_up, v_ffn_w_gate, v_ffn_conv_w, v_ffn_conv_b, v_ffn_w_down, v_ln2_g, v_ln2_b):
    weights = dict(w_in=w_in, b_gate=b_gate, rnn_conv_w=rnn_conv_w, rnn_conv_b=rnn_conv_b, lru_wa=lru_wa,
                   lru_ba=lru_ba, lru_wi=lru_wi, lru_bi=lru_bi, lru_lambda=lru_lambda, attn_sinks=attn_sinks,
                   w_attn_proj=w_attn_proj, w_rnn_proj=w_rnn_proj, w_out=w_out, ln1_g=ln1_g, ln1_b=ln1_b,
                   ffn_w_up=ffn_w_up, ffn_w_gate=ffn_w_gate, ffn_conv_w=ffn_conv_w, ffn_conv_b=ffn_conv_b,
                   ffn_w_down=ffn_w_down, ln2_g=ln2_g, ln2_b=ln2_b)
    m_in = dict(w_in=m_w_in, b_gate=m_b_gate, rnn_conv_w=m_rnn_conv_w, rnn_conv_b=m_rnn_conv_b, lru_wa=m_lru_wa,
                lru_ba=m_lru_ba, lru_wi=m_lru_wi, lru_bi=m_lru_bi, lru_lambda=m_lru_lambda, attn_sinks=m_attn_sinks,
                w_attn_proj=m_w_attn_proj, w_rnn_proj=m_w_rnn_proj, w_out=m_w_out, ln1_g=m_ln1_g, ln1_b=m_ln1_b,
                ffn_w_up=m_ffn_w_up, ffn_w_gate=m_ffn_w_gate, ffn_conv_w=m_ffn_conv_w, ffn_conv_b=m_ffn_conv_b,
                ffn_w_down=m_ffn_w_down, ln2_g=m_ln2_g, ln2_b=m_ln2_b)
    v_in = dict(w_in=v_w_in, b_gate=v_b_gate, rnn_conv_w=v_rnn_conv_w, rnn_conv_b=v_rnn_conv_b, lru_wa=v_lru_wa,
                lru_ba=v_lru_ba, lru_wi=v_lru_wi, lru_bi=v_lru_bi, lru_lambda=v_lru_lambda, attn_sinks=v_attn_sinks,
                w_attn_proj=v_w_attn_proj, w_rnn_proj=v_w_rnn_proj, w_out=v_w_out, ln1_g=v_ln1_g, ln1_b=v_ln1_b,
                ffn_w_up=v_ffn_w_up, ffn_w_gate=v_ffn_w_gate, ffn_conv_w=v_ffn_conv_w, ffn_conv_b=v_ffn_conv_b,
                ffn_w_down=v_ffn_w_down, ln2_g=v_ln2_g, ln2_b=v_ln2_b)
    order = list(weights)

    assert x.shape[0] == 1 and w_in.shape[0] == 1, "one sequence per device, depth 1"
    T, D = x.shape[1], x.shape[2]
    nq = attn_sinks.shape[-1]
    nkv = nq // GROUP
    d_attn, d_kv = nq * HEAD_DIM, nkv * HEAD_DIM
    d_rnn = rnn_conv_b.shape[-1]
    d_ff = ffn_conv_b.shape[-1]
    n_blocks, bw = lru_wa.shape[1], lru_wa.shape[2]
    per = (bw * LANES // math.gcd(bw, LANES)) // bw
    gw = per * bw
    assert n_blocks % per == 0 and d_rnn == n_blocks * bw
    q_off, k_off, v_off = 0, d_attn, d_attn + d_kv
    rx_off = d_attn + 2 * d_kv
    ry_off = rx_off + d_rnn
    gl_off = ry_off + d_rnn
    d_in = gl_off + 2 * D
    assert w_in.shape[-1] * N_SHARDS == d_in
    assert k_off % d_kv == 0 and rx_off % gw == 0 and T % ATTN_BLOCK == 0

    xi, yi, ci = lax.axis_index("x"), lax.axis_index("y"), lax.axis_index("c")
    j_me = 2 * xi + yi
    c_arr = jnp.reshape(ci, (1,)).astype(jnp.int32)
    j_arr = jnp.reshape(j_me, (1,)).astype(jnp.int32)

    x0 = x[0]
    tgt = loss_target[0]
    big = ["w_in", "w_attn_proj", "w_rnn_proj", "w_out", "ffn_w_up", "ffn_w_gate", "ffn_w_down"]
    shards_bf16 = [_cast_bf16(weights[n][0], "cast_" + n) for n in big]
    full = dict(zip(big, _all_gather_weights(shards_bf16)))
    w_in_s = full["w_in"]
    w_up_s, w_gate_s = full["ffn_w_up"], full["ffn_w_gate"]
    w_ap = full["w_attn_proj"].reshape(d_attn, D)
    w_rp = full["w_rnn_proj"].reshape(d_rnn, D)
    w_o = full["w_out"].reshape(D, D)
    w_dn = full["ffn_w_down"].reshape(d_ff, D)

    rcw_s, fcw_s = _all_gather_small([rnn_conv_w[0], ffn_conv_w[0]])
    rcw = jnp.concatenate([rcw_s[j] for j in range(N_SHARDS)], axis=1)
    fcw = jnp.concatenate([fcw_s[j] for j in range(N_SHARDS)], axis=1)

    wa_g = _group_blocks(lru_wa[0], per).astype(BF16)
    wi_g = _group_blocks(lru_wi[0], per).astype(BF16)

    proj = _mm(x0, w_in_s, name="mm_proj", b_shards=N_SHARDS)
    a_out = _attn_fwd(proj, attn_sinks, nq, (q_off, k_off, v_off))
    b_out, h_all = _rnn_fwd(proj, (rx_off, ry_off), rcw, rnn_conv_b, wa_g, wi_g, lru_ba, lru_bi, lru_lambda)
    y_attn = _mm(a_out, w_ap, name="mm_attn_proj")
    y_rnn = _mm(b_out, w_rp, name="mm_rnn_proj")
    merged = _merge_fwd(proj, gl_off, b_gate, y_attn, y_rnn)
    mix = _mm(merged, w_o, name="mm_out")
    x1, xh1, rstd1 = _ln_fwd(x0, mix, ln1_g, ln1_b, "ln1_fwd")
    up = _mm(x1, w_up_s, name="mm_up", b_shards=N_SHARDS)
    gpre = _mm(x1, w_gate_s, name="mm_gate", b_shards=N_SHARDS)
    f_act = _ffn_fwd(up, gpre, fcw, ffn_conv_b)
    f_out = _mm(f_act, w_dn, name="mm_down")
    dz2, loss_acc, dg2, db2 = _ln_loss_bwd(x1, f_out, ln2_g, ln2_b, tgt)

    g_down = _mm(f_act, dz2, name="mm_d_w_down", ta=True, out_dtype=BF16)
    d_f = _mm(dz2, w_dn, name="mm_d_f", tb=True)
    dup, dgp, d_fcw, d_fcb = _ffn_bwd(up, gpre, fcw, ffn_conv_b, d_f)
    g_up = _mm(x1, dup, name="mm_d_w_up", ta=True, out_dtype=BF16, out_shards=N_SHARDS)
    g_gate = _mm(x1, dgp, name="mm_d_w_gate", ta=True, out_dtype=BF16, out_shards=N_SHARDS)
    dx1_a = _mm(dup, w_up_s, name="mm_dx1_up", tb=True, b_shards=N_SHARDS, adds=((ALPHA, dz2),))
    dx1 = _mm(dgp, w_gate_s, name="mm_dx1_gate", tb=True, b_shards=N_SHARDS, adds=((1.0, dx1_a),))
    dz1, dg1, db1 = _ln_bwd(dx1, xh1, rstd1, ln1_g)
    g_out = _mm(merged, dz1, name="mm_d_w_out", ta=True, out_dtype=BF16)
    d_m = _mm(dz1, w_o, name="mm_d_merged", tb=True)
    dya, dyr, dgl_a, dgl_r, dbg_a, dbg_r = _merge_bwd(proj, gl_off, b_gate, y_attn, y_rnn, d_m)
    g_ap = _mm(a_out, dya, name="mm_d_w_attn_proj", ta=True, out_dtype=BF16)
    g_rp = _mm(b_out, dyr, name="mm_d_w_rnn_proj", ta=True, out_dtype=BF16)
    d_a = _mm(dya, w_ap, name="mm_d_attn", tb=True)
    d_b = _mm(dyr, w_rp, name="mm_d_rnn", tb=True)
    dq, dk, dv, dsink = _attn_bwd(proj, d_a, attn_sinks, nq, (q_off, k_off, v_off))
    (drx, dry, d_rcw, d_rcb, d_ba, d_bi, d_lam, d_wa_g, d_wi_g) = _rnn_bwd(
        proj, (rx_off, ry_off), h_all, d_b, rcw, rnn_conv_b, wa_g, wi_g, lru_ba, lru_bi, lru_lambda)
    d_proj = jnp.concatenate([dq, dk.astype(BF16), dv.astype(BF16), drx, dry, dgl_a, dgl_r], axis=1)
    g_in = _mm(x0, d_proj, name="mm_d_w_in", ta=True, out_dtype=BF16, out_shards=N_SHARDS)
    grad_x = _mm(d_proj, w_in_s, name="mm_d_x", tb=True, b_shards=N_SHARDS, adds=((ALPHA, dz1),))

    g_big = [g_in, g_ap.reshape(N_SHARDS, d_attn // N_SHARDS, D), g_rp.reshape(N_SHARDS, d_rnn // N_SHARDS, D),
             g_out.reshape(N_SHARDS, D // N_SHARDS, D), g_up, g_gate, g_down.reshape(N_SHARDS, d_ff // N_SHARDS, D)]
    from_sibling = _pair_exchange(g_big)
    chip_part = [_pair_sum(g, la, c_arr, "pair_sum_" + n) for g, la, n in zip(g_big, from_sibling, big)]
    from_chips = _shard_exchange(chip_part)
    halves = [_shard_sum(cp, lb, j_arr, "shard_sum_" + n) for cp, lb, n in zip(chip_part, from_chips, big)]
    grads = dict(zip(big, _share_halves(halves)))

    small_parts = [
        ("loss", loss_acc[0:1, 0:1]),
        ("b_gate", jnp.concatenate([dbg_a, dbg_r], axis=1)),
        ("rnn_conv_w", d_rcw), ("rnn_conv_b", d_rcb),
        ("lru_wa", _ungroup_blocks(d_wa_g, per)), ("lru_ba", d_ba),
        ("lru_wi", _ungroup_blocks(d_wi_g, per)), ("lru_bi", d_bi), ("lru_lambda", d_lam),
        ("attn_sinks", dsink[0:1, 0:nq]),
        ("ln1_g", dg1), ("ln1_b", db1),
        ("ffn_conv_w", d_fcw), ("ffn_conv_b", d_fcb),
        ("ln2_g", dg2), ("ln2_b", db2),
    ]
    packed = _pack([p for _, p in small_parts])
    reduced = _slot_sum(_broadcast_small(packed), "small_grad_sum")
    small = dict(zip([n for n, _ in small_parts], _unpack(reduced, [p.shape for _, p in small_parts])))
    loss = small.pop("loss").reshape(())
    rcw_n = d_rnn // N_SHARDS
    fcw_n = d_ff // N_SHARDS
    small["rnn_conv_w"] = lax.dynamic_slice(small["rnn_conv_w"], (0, j_me * rcw_n), (4, rcw_n))
    small["ffn_conv_w"] = lax.dynamic_slice(small["ffn_conv_w"], (0, j_me * fcw_n), (3, fcw_n))
    for n, g in small.items():
        grads[n] = g

    out_g, out_d, out_m, out_v = {}, {}, {}, {}
    for n in order:
        w_full = weights[n]
        shape = w_full.shape
        two_d = (math.prod(shape[:-1]), shape[-1])
        g2 = grads[n].reshape(two_d)
        d2, m2, v2 = _adamw(w_full.reshape(two_d), g2, m_in[n].reshape(two_d), v_in[n].reshape(two_d), "adamw_" + n)
        out_g[n] = g2.reshape(shape)
        out_d[n], out_m[n], out_v[n] = d2.reshape(shape), m2.reshape(shape), v2.reshape(shape)

    return (loss, grad_x.reshape(x.shape), *[out_g[n] for n in order], *[out_d[n] for n in order],
            *[out_m[n] for n in order], *[out_v[n] for n in order])
```

```python
import functools
import math

import jax
import jax.numpy as jnp
from jax import lax
from jax.experimental import pallas as pl
from jax.experimental.pallas import tpu as pltpu

F32 = jnp.float32
BF16 = jnp.bfloat16
MESH = pl.DeviceIdType.MESH

HEAD_DIM = 64
GROUP = 8
ATTN_BLOCK = 128
LRU_C = 8.0
LN_EPS = 1e-5
ALPHA = 2.0 ** 0.25
LANES = 128
N_SHARDS = 4
N_DEV = 8
VMEM_LIMIT = 56 * 1024 * 1024
MM_VMEM_BUDGET = 44 * 1024 * 1024
MM_MAX_TILE = 3072
PACK_ROW_MULT = 8 * 64
NEG = -1e30

ADAM_LR, ADAM_B1, ADAM_B2, ADAM_EPS, ADAM_WD, ADAM_STEP = 0.001, 0.9, 0.999, 1e-08, 0.01, 10

GELU_C = math.sqrt(2.0 / math.pi)
GELU_A = 0.044715


def _cparams(sem=None):
    kw = dict(vmem_limit_bytes=VMEM_LIMIT)
    if sem is not None:
        kw["dimension_semantics"] = sem
    return pltpu.CompilerParams(**kw)


def _pick(n, prefs):
    for p in prefs:
        if n % p == 0:
            return p
    return n


def _row_tile(rows, row_bytes, mult, budget=2 * 1024 * 1024):
    best = None
    for d in range(mult, rows + 1, mult):
        if rows % d == 0 and d * row_bytes <= budget:
            best = d
    return best if best is not None else rows


def _gelu(x):
    return 0.5 * x * (1.0 + jnp.tanh(GELU_C * (x + GELU_A * x * x * x)))


def _gelu_and_grad(x):
    t = jnp.tanh(GELU_C * (x + GELU_A * x * x * x))
    g = 0.5 * x * (1.0 + t)
    dg = 0.5 * (1.0 + t) + 0.5 * x * (1.0 - t * t) * GELU_C * (1.0 + 3.0 * GELU_A * x * x)
    return g, dg


def _shift_down(x, s, fill=0.0):
    row = lax.broadcasted_iota(jnp.int32, x.shape, 0)
    return jnp.where(row >= s, pltpu.roll(x, s, 0), fill)


def _shift_up(x, s, fill=0.0):
    n = x.shape[0]
    row = lax.broadcasted_iota(jnp.int32, x.shape, 0)
    return jnp.where(row < n - s, pltpu.roll(x, n - s, 0), fill)


def _mm(a, b, *, name, ta=False, tb=False, out_dtype=F32, adds=(), b_shards=1, out_shards=1,
        tm=None, tn=None, tk=None):
    if ta:
        K, M = a.shape
    else:
        M, K = a.shape
    if b_shards > 1:
        n_sh = b.shape[-1]
        if tb:
            N = b.shape[1]
            assert b_shards * n_sh == K
        else:
            N = b_shards * n_sh
            assert b.shape[1] == K
    else:
        n_sh = None
        if tb:
            N = b.shape[0]
            assert b.shape[1] == K
        else:
            N = b.shape[1]
            assert b.shape[0] == K
    wide = (1024, 1536, 1280, 768, 640, 512, 256, 128)
    if tn is None:
        if b_shards > 1 and not tb:
            tn = n_sh if n_sh <= MM_MAX_TILE else _pick(n_sh, wide)
        elif out_shards > 1:
            tn = N // out_shards if N // out_shards <= MM_MAX_TILE else _pick(N // out_shards, wide)
        else:
            tn = _pick(N, wide)
    if tk is None:
        if b_shards > 1 and tb:
            tk = n_sh if n_sh <= MM_MAX_TILE else _pick(n_sh, wide)
        else:
            tk = K if K <= MM_MAX_TILE else _pick(K, (2048,) + wide)
    assert N % tn == 0 and K % tk == 0, (name, M, N, K, tn, tk)
    nk = K // tk
    n_add = len(adds)
    sa, sb, so = a.dtype.itemsize, b.dtype.itemsize, jnp.dtype(out_dtype).itemsize

    def vmem_bytes(tm_):
        return (2 * (tm_ * tk * sa + tk * tn * sb + tm_ * tn * so + n_add * tm_ * tn * 4)
                + (tm_ * tn * 4 if nk > 1 else 0))

    if tm is None:
        tm = _pick(M, (512, 256, 128))
        while vmem_bytes(tm) > MM_VMEM_BUDGET and tm % 256 == 0:
            tm //= 2
    assert M % tm == 0, (name, M, tm)
    b_outer = b.size * sb >= a.size * sa

    def ij(g0, g1):
        return (g1, g0) if b_outer else (g0, g1)

    def amap(g0, g1, k):
        i, _ = ij(g0, g1)
        return (k, i) if ta else (i, k)

    def bmap(g0, g1, k):
        _, j = ij(g0, g1)
        if b_shards > 1 and not tb:
            per = n_sh // tn
            return (j // per, k, j % per)
        if b_shards > 1 and tb:
            per = n_sh // tk
            return (k // per, j, k % per)
        return (j, k) if tb else (k, j)

    def omap(g0, g1, k):
        i, j = ij(g0, g1)
        if out_shards > 1:
            per_o = (N // out_shards) // tn
            return (j // per_o, i, j % per_o)
        return (i, j)

    a_spec = pl.BlockSpec((tk, tm) if ta else (tm, tk), amap)
    if b_shards > 1:
        b_spec = pl.BlockSpec((None, tn, tk) if tb else (None, tk, tn), bmap)
    else:
        b_spec = pl.BlockSpec((tn, tk) if tb else (tk, tn), bmap)
    add_specs = [pl.BlockSpec((tm, tn), lambda g0, g1, k: ij(g0, g1)) for _ in adds]
    if out_shards > 1:
        out_spec = pl.BlockSpec((None, tm, tn), omap)
        out_shape = jax.ShapeDtypeStruct((out_shards, M, N // out_shards), out_dtype)
    else:
        out_spec = pl.BlockSpec((tm, tn), omap)
        out_shape = jax.ShapeDtypeStruct((M, N), out_dtype)

    if ta:
        dims = (((0,), (0,)), ((), ()))
    elif tb:
        dims = (((1,), (1,)), ((), ()))
    else:
        dims = (((1,), (0,)), ((), ()))
    scales = tuple(s for s, _ in adds)

    def finish(r, add_refs, o_ref):
        for s, ref in zip(scales, add_refs):
            r = r + s * ref[...].astype(F32)
        o_ref[...] = r.astype(out_dtype)

    def body(a_ref, b_ref, *rest):
        add_refs = rest[:n_add]
        o_ref = rest[n_add]
        part = lax.dot_general(a_ref[...].astype(BF16), b_ref[...].astype(BF16), dims, preferred_element_type=F32)
        if nk == 1:
            finish(part, add_refs, o_ref)
            return
        acc = rest[n_add + 1]
        k = pl.program_id(2)

        @pl.when(k == 0)
        def _():
            acc[...] = part

        @pl.when(k > 0)
        def _():
            acc[...] += part

        @pl.when(k == nk - 1)
        def _():
            finish(acc[...], add_refs, o_ref)

    grid = (N // tn, M // tm, nk) if b_outer else (M // tm, N // tn, nk)
    return pl.pallas_call(
        body, name=name, out_shape=out_shape, grid=grid,
        in_specs=[a_spec, b_spec] + add_specs, out_specs=out_spec,
        scratch_shapes=[pltpu.VMEM((tm, tn), F32)] if nk > 1 else [],
        compiler_params=_cparams(("parallel", "parallel", "arbitrary")),
    )(a, b, *[x for _, x in adds])


def _cast_bf16(w, name):
    R, C = w.shape
    tr = _row_tile(R, C * 4, 16)

    def body(w_ref, o_ref):
        o_ref[...] = w_ref[...].astype(BF16)

    return pl.pallas_call(
        body, name=name, out_shape=jax.ShapeDtypeStruct((R, C), BF16), grid=(R // tr,),
        in_specs=[pl.BlockSpec((tr, C), lambda r: (r, 0))], out_specs=pl.BlockSpec((tr, C), lambda r: (r, 0)),
        compiler_params=_cparams(("parallel",)),
    )(w)


def _cast_bf16_into_slot(w, jc_arr, name):
    R, C = w.shape
    tr = _row_tile(R, C * 4, 16)

    def body(jc_ref, w_ref, o_ref):
        o_ref[...] = w_ref[...].astype(BF16)

    gs = pltpu.PrefetchScalarGridSpec(
        num_scalar_prefetch=1, grid=(R // tr,),
        in_specs=[pl.BlockSpec((tr, C), lambda r, jc: (r, 0))],
        out_specs=pl.BlockSpec((None, tr, C), lambda r, jc: (jc[0], r, 0)))
    return pl.pallas_call(body, name=name, out_shape=jax.ShapeDtypeStruct((N_SHARDS, R, C), BF16), grid_spec=gs,
                          compiler_params=_cparams(("parallel",)))(jc_arr, w)


def _pair_sum(g, la, jc_arr, name):
    S, R, C = g.shape
    half = R // 2
    tr = _row_tile(half, C * 4, 16)
    nrt = half // tr
    dt = g.dtype

    def body(jc_ref, g_ref, la_ref, o_ref):
        o_ref[...] = (g_ref[...].astype(F32) + la_ref[...].astype(F32)).astype(dt)

    gs = pltpu.PrefetchScalarGridSpec(
        num_scalar_prefetch=1, grid=(S, nrt),
        in_specs=[pl.BlockSpec((None, tr, C), lambda s, r, jc: (s, jc[1] * nrt + r, 0)),
                  pl.BlockSpec((None, tr, C), lambda s, r, jc: (s, r, 0))],
        out_specs=pl.BlockSpec((None, tr, C), lambda s, r, jc: (s, r, 0)))
    return pl.pallas_call(body, name=name, out_shape=jax.ShapeDtypeStruct((S, half, C), dt), grid_spec=gs,
                          compiler_params=_cparams(("parallel", "parallel")))(jc_arr, g, la)


def _shard_sum(cp, lb, jc_arr, name, all_slots=False):
    S, h, C = cp.shape
    tr = _row_tile(h, C * 4, 16)

    def body(jc_ref, cp_ref, l0, l1, l2, o_ref):
        o_ref[...] = ((cp_ref[...].astype(F32) + l0[...].astype(F32)) + l1[...].astype(F32)) + l2[...].astype(F32)

    def lspec(kk):
        return pl.BlockSpec((None, tr, C), lambda r, jc: (kk, r, 0))

    if all_slots:
        out_spec = pl.BlockSpec((None, None, tr, C), lambda r, jc: (jc[0], jc[1], r, 0))
        out_shape = jax.ShapeDtypeStruct((S, 2, h, C), F32)
    else:
        out_spec = pl.BlockSpec((None, tr, C), lambda r, jc: (jc[1], r, 0))
        out_shape = jax.ShapeDtypeStruct((2, h, C), F32)
    gs = pltpu.PrefetchScalarGridSpec(
        num_scalar_prefetch=1, grid=(h // tr,),
        in_specs=[pl.BlockSpec((None, tr, C), lambda r, jc: (jc[0], r, 0)), lspec(0), lspec(1), lspec(2)],
        out_specs=out_spec)
    return pl.pallas_call(body, name=name, out_shape=out_shape, grid_spec=gs,
                          compiler_params=_cparams(("parallel",)))(jc_arr, cp, lb, lb, lb)


ANY = pl.BlockSpec(memory_space=pl.ANY)


def _place():
    x, y, c = lax.axis_index("x"), lax.axis_index("y"), lax.axis_index("c")
    chips = [(1 - x, y), (x, 1 - y), (1 - x, 1 - y)]
    return x, y, c, chips


def _all_gather_weights(bufs):
    n = len(bufs)

    def body(*refs):
        out = refs[n:2 * n]
        s_ici, r_ici, s_d2d, r_d2d = refs[2 * n:]
        x, y, c, chips = _place()
        j_me = 2 * x + y
        sends = []
        for i in range(n):
            half = out[i].shape[1] // 2
            rows = pl.ds(c * half, half)
            for kk, (px, py) in enumerate(chips):
                mine = out[i].at[j_me, rows, :]
                cp = pltpu.make_async_remote_copy(
                    src_ref=mine, dst_ref=mine,
                    send_sem=s_ici.at[3 * i + kk], recv_sem=r_ici.at[3 * i + kk],
                    device_id=(px, py, c), device_id_type=MESH)
                cp.start()
                sends.append(cp)
        for i in range(n):
            half = out[i].shape[1] // 2
            rows = pl.ds(c * half, half)
            for kk, (px, py) in enumerate(chips):
                j_src = 2 * px + py
                landed = out[i].at[j_src, rows, :]
                pltpu.make_async_remote_copy(
                    src_ref=landed, dst_ref=landed, send_sem=s_ici.at[3 * i + kk], recv_sem=r_ici.at[3 * i + kk],
                    device_id=(px, py, c), device_id_type=MESH).wait_recv()
                fw = pltpu.make_async_remote_copy(
                    src_ref=landed, dst_ref=landed, send_sem=s_d2d.at[3 * i + kk], recv_sem=r_d2d.at[3 * i + kk],
                    device_id=(x, y, 1 - c), device_id_type=MESH)
                fw.start()
                sends.append(fw)
        for i in range(n):
            half = out[i].shape[1] // 2
            rows_sib = pl.ds((1 - c) * half, half)
            for kk, (px, py) in enumerate(chips):
                j_src = 2 * px + py
                landed = out[i].at[j_src, rows_sib, :]
                pltpu.make_async_remote_copy(
                    src_ref=landed, dst_ref=landed, send_sem=s_d2d.at[3 * i + kk], recv_sem=r_d2d.at[3 * i + kk],
                    device_id=(x, y, 1 - c), device_id_type=MESH).wait_recv()
        for cp in sends:
            cp.wait_send()

    out_shape = [jax.ShapeDtypeStruct(s.shape, s.dtype) for s in bufs]
    return pl.pallas_call(
        body, name="all_gather_weights", out_shape=out_shape,
        in_specs=[ANY] * n, out_specs=[ANY] * n, input_output_aliases={i: i for i in range(n)},
        scratch_shapes=[pltpu.SemaphoreType.DMA((3 * n,))] * 4,
    )(*bufs)


def _all_gather_small(shards):
    n = len(shards)

    def body(*refs):
        w = refs[:n]
        out = refs[n:2 * n]
        local_sem, s_sem, r_sem = refs[2 * n:]
        x, y, c, chips = _place()
        j_me = 2 * x + y
        cps = []
        for i in range(n):
            lc = pltpu.make_async_copy(w[i], out[i].at[j_me], local_sem.at[i])
            lc.start()
            cps.append(lc)
        sends = []
        for i in range(n):
            for kk, (px, py) in enumerate(chips):
                cp = pltpu.make_async_remote_copy(
                    src_ref=w[i], dst_ref=out[i].at[j_me], send_sem=s_sem.at[3 * i + kk],
                    recv_sem=r_sem.at[3 * i + kk], device_id=(px, py, c), device_id_type=MESH)
                cp.start()
                sends.append(cp)
        for i in range(n):
            for kk, (px, py) in enumerate(chips):
                sends[3 * i + kk].wait_send()
                pltpu.make_async_remote_copy(
                    src_ref=w[i], dst_ref=out[i].at[2 * px + py], send_sem=s_sem.at[3 * i + kk],
                    recv_sem=r_sem.at[3 * i + kk], device_id=(px, py, c), device_id_type=MESH).wait_recv()
        for lc in cps:
            lc.wait()

    out_shape = [jax.ShapeDtypeStruct((N_SHARDS,) + s.shape, s.dtype) for s in shards]
    return pl.pallas_call(
        body, name="all_gather_conv_weights", out_shape=out_shape, in_specs=[ANY] * n, out_specs=[ANY] * n,
        scratch_shapes=[pltpu.SemaphoreType.DMA((n,)), pltpu.SemaphoreType.DMA((3 * n,)),
                        pltpu.SemaphoreType.DMA((3 * n,))],
    )(*shards)


def _pair_exchange(grads):
    n = len(grads)

    def body(*refs):
        g = refs[:n]
        la = refs[n:2 * n]
        s_sem, r_sem = refs[2 * n:]
        x, y, c, _ = _place()
        cps = []
        for i in range(n):
            half = g[i].shape[1] // 2
            cp = pltpu.make_async_remote_copy(
                src_ref=g[i].at[:, pl.ds((1 - c) * half, half), :], dst_ref=la[i],
                send_sem=s_sem.at[i], recv_sem=r_sem.at[i], device_id=(x, y, 1 - c), device_id_type=MESH)
            cp.start()
            cps.append(cp)
        for cp in cps:
            cp.wait()

    out_shape = [jax.ShapeDtypeStruct((s.shape[0], s.shape[1] // 2, s.shape[2]), s.dtype) for s in grads]
    return pl.pallas_call(
        body, name="grad_pair_exchange", out_shape=out_shape, in_specs=[ANY] * n, out_specs=[ANY] * n,
        scratch_shapes=[pltpu.SemaphoreType.DMA((n,)), pltpu.SemaphoreType.DMA((n,))],
    )(*grads)


def _shard_exchange(cps_in):
    n = len(cps_in)

    def body(*refs):
        cp_ref = refs[:n]
        lb = refs[n:2 * n]
        s_sem, r_sem = refs[2 * n:]
        x, y, c, chips = _place()
        cps = []
        for i in range(n):
            for kk, (px, py) in enumerate(chips):
                cp = pltpu.make_async_remote_copy(
                    src_ref=cp_ref[i].at[2 * px + py], dst_ref=lb[i].at[kk],
                    send_sem=s_sem.at[3 * i + kk], recv_sem=r_sem.at[3 * i + kk],
                    device_id=(px, py, c), device_id_type=MESH)
                cp.start()
                cps.append(cp)
        for cp in cps:
            cp.wait()

    out_shape = [jax.ShapeDtypeStruct((3,) + s.shape[1:], s.dtype) for s in cps_in]
    return pl.pallas_call(
        body, name="grad_shard_exchange", out_shape=out_shape, in_specs=[ANY] * n, out_specs=[ANY] * n,
        scratch_shapes=[pltpu.SemaphoreType.DMA((3 * n,)), pltpu.SemaphoreType.DMA((3 * n,))],
    )(*cps_in)


def _share_results(halves, eighths):
    n = len(halves)

    def body(*refs):
        out = refs[n + 1:2 * n + 1]
        eig = refs[2 * n + 1]
        s_sem, r_sem, se_sem, re_sem = refs[2 * n + 2:]
        x, y, c, _ = _place()
        j_me = 2 * x + y
        cps = []
        for i in range(n):
            cp = pltpu.make_async_remote_copy(
                src_ref=out[i].at[c], dst_ref=out[i].at[c],
                send_sem=s_sem.at[i], recv_sem=r_sem.at[i], device_id=(x, y, 1 - c), device_id_type=MESH)
            cp.start()
            cps.append(cp)
        for r in range(1, N_DEV):
            px, py, pc = x ^ ((r >> 2) & 1), y ^ ((r >> 1) & 1), c ^ (r & 1)
            cp = pltpu.make_async_remote_copy(
                src_ref=eig.at[j_me, c], dst_ref=eig.at[j_me, c], send_sem=se_sem.at[r - 1],
                recv_sem=re_sem.at[r - 1], device_id=(px, py, pc), device_id_type=MESH)
            cp.start()
            cps.append(cp)
        for i in range(n):
            theirs = out[i].at[1 - c]
            pltpu.make_async_remote_copy(
                src_ref=theirs, dst_ref=theirs,
                send_sem=s_sem.at[i], recv_sem=r_sem.at[i], device_id=(x, y, 1 - c), device_id_type=MESH).wait_recv()
        for r in range(1, N_DEV):
            px, py, pc = x ^ ((r >> 2) & 1), y ^ ((r >> 1) & 1), c ^ (r & 1)
            theirs = eig.at[2 * px + py, pc]
            pltpu.make_async_remote_copy(
                src_ref=theirs, dst_ref=theirs, send_sem=se_sem.at[r - 1],
                recv_sem=re_sem.at[r - 1], device_id=(px, py, pc), device_id_type=MESH).wait_recv()
        for cp in cps:
            cp.wait_send()

    bufs = list(halves) + [eighths]
    out_shape = [jax.ShapeDtypeStruct(s.shape, s.dtype) for s in bufs]
    return pl.pallas_call(
        body, name="grad_share_results", out_shape=out_shape, in_specs=[ANY] * (n + 1), out_specs=[ANY] * (n + 1),
        input_output_aliases={i: i for i in range(n + 1)},
        scratch_shapes=[pltpu.SemaphoreType.DMA((n,)), pltpu.SemaphoreType.DMA((n,)),
                        pltpu.SemaphoreType.DMA((N_DEV - 1,)), pltpu.SemaphoreType.DMA((N_DEV - 1,))],
    )(*bufs)


def _attn_scores(q, kp, kc, slope, sink, mask_p, mask_c, dist_p, dist_c):
    scale = HEAD_DIM ** -0.5
    dn = (((1,), (1,)), ((), ()))
    sp = lax.dot_general(q, kp, dn, preferred_element_type=F32) * scale - slope * dist_p
    sc = lax.dot_general(q, kc, dn, preferred_element_type=F32) * scale - slope * dist_c
    sp = jnp.where(mask_p, sp, NEG)
    sc = jnp.where(mask_c, sc, NEG)
    m = jnp.maximum(jnp.maximum(jnp.max(sp, axis=1, keepdims=True), jnp.max(sc, axis=1, keepdims=True)), sink)
    ep = jnp.exp(sp - m)
    ec = jnp.exp(sc - m)
    es = jnp.exp(sink - m)
    inv = 1.0 / (jnp.sum(ep, axis=1, keepdims=True) + jnp.sum(ec, axis=1, keepdims=True) + es)
    return ep * inv, ec * inv, es * inv


def _attn_masks(n):
    ti = lax.broadcasted_iota(jnp.int32, (ATTN_BLOCK, ATTN_BLOCK), 0)
    sj = lax.broadcasted_iota(jnp.int32, (ATTN_BLOCK, ATTN_BLOCK), 1)
    mask_c = sj <= ti
    mask_p = jnp.logical_and(sj > ti, n > 0)
    dist_c = (ti - sj).astype(F32)
    dist_p = dist_c + float(ATTN_BLOCK)
    return mask_p, mask_c, dist_p, dist_c


def _attn_specs(T, d_attn, d_kv, q_blk, k_blk, v_blk):
    bq = pl.BlockSpec((ATTN_BLOCK, d_attn), lambda n: (n, q_blk))
    kp = pl.BlockSpec((ATTN_BLOCK, d_kv), lambda n: (jnp.maximum(n - 1, 0), k_blk))
    kc = pl.BlockSpec((ATTN_BLOCK, d_kv), lambda n: (n, k_blk))
    vp = pl.BlockSpec((ATTN_BLOCK, d_kv), lambda n: (jnp.maximum(n - 1, 0), v_blk))
    vc = pl.BlockSpec((ATTN_BLOCK, d_kv), lambda n: (n, v_blk))
    return bq, kp, kc, vp, vc


def _attn_fwd(proj, sinks, nq, cols):
    T = proj.shape[0]
    nkv = nq // GROUP
    d_attn, d_kv = nq * HEAD_DIM, nkv * HEAD_DIM
    q_off, k_off, v_off = cols
    bq, kp, kc, vp, vc = _attn_specs(T, d_attn, d_kv, q_off // d_attn, k_off // d_kv, v_off // d_kv)

    def body(sink_ref, q_ref, kp_ref, kc_ref, vp_ref, vc_ref, o_ref):
        n = pl.program_id(0)
        masks = _attn_masks(n)
        for g in range(nkv):
            ks = slice(g * HEAD_DIM, (g + 1) * HEAD_DIM)
            k_p, k_c = kp_ref[:, ks].astype(BF16), kc_ref[:, ks].astype(BF16)
            v_p, v_c = vp_ref[:, ks].astype(BF16), vc_ref[:, ks].astype(BF16)
            for hh in range(GROUP):
                h = g * GROUP + hh
                hs = slice(h * HEAD_DIM, (h + 1) * HEAD_DIM)
                slope = 2.0 ** (-8.0 * (h + 1) / nq)
                q = q_ref[:, hs].astype(BF16)
                pp, pc, _ = _attn_scores(q, k_p, k_c, slope, sink_ref[0, h], *masks)
                o = (jnp.dot(pp.astype(BF16), v_p, preferred_element_type=F32)
                     + jnp.dot(pc.astype(BF16), v_c, preferred_element_type=F32))
                o_ref[:, hs] = o.astype(BF16)

    return pl.pallas_call(
        body, name="attn_fwd", out_shape=jax.ShapeDtypeStruct((T, d_attn), BF16), grid=(T // ATTN_BLOCK,),
        in_specs=[pl.BlockSpec(memory_space=pltpu.SMEM), bq, kp, kc, vp, vc],
        out_specs=pl.BlockSpec((ATTN_BLOCK, d_attn), lambda n: (n, 0)),
        compiler_params=_cparams(("parallel",)),
    )(sinks, proj, proj, proj, proj, proj)


def _attn_bwd(proj, d_attn_out, sinks, nq, cols):
    T = proj.shape[0]
    nkv = nq // GROUP
    d_attn, d_kv = nq * HEAD_DIM, nkv * HEAD_DIM
    q_off, k_off, v_off = cols
    bq, kp, kc, vp, vc = _attn_specs(T, d_attn, d_kv, q_off // d_attn, k_off // d_kv, v_off // d_kv)
    scale = HEAD_DIM ** -0.5
    dn_t = (((1,), (1,)), ((), ()))
    dn_r = (((0,), (0,)), ((), ()))

    def body(sink_ref, q_ref, kp_ref, kc_ref, vp_ref, vc_ref, do_ref, dq_ref, dk_ref, dv_ref, ds_ref):
        n = pl.program_id(0)

        @pl.when(n == 0)
        def _():
            dk_ref[...] = jnp.zeros_like(dk_ref)
            dv_ref[...] = jnp.zeros_like(dv_ref)
            ds_ref[...] = jnp.zeros_like(ds_ref)

        masks = _attn_masks(n)
        rows_c = pl.ds(pl.multiple_of(n * ATTN_BLOCK, ATTN_BLOCK), ATTN_BLOCK)
        rows_p = pl.ds(pl.multiple_of(jnp.maximum(n - 1, 0) * ATTN_BLOCK, ATTN_BLOCK), ATTN_BLOCK)
        lane = lax.broadcasted_iota(jnp.int32, ds_ref.shape, 1)
        srow = lax.broadcasted_iota(jnp.int32, ds_ref.shape, 0)
        ds_acc = jnp.zeros(ds_ref.shape, F32)
        for g in range(nkv):
            ks = slice(g * HEAD_DIM, (g + 1) * HEAD_DIM)
            k_p, k_c = kp_ref[:, ks].astype(BF16), kc_ref[:, ks].astype(BF16)
            v_p, v_c = vp_ref[:, ks].astype(BF16), vc_ref[:, ks].astype(BF16)
            dkp = jnp.zeros((ATTN_BLOCK, HEAD_DIM), F32)
            dkc = jnp.zeros((ATTN_BLOCK, HEAD_DIM), F32)
            dvp = jnp.zeros((ATTN_BLOCK, HEAD_DIM), F32)
            dvc = jnp.zeros((ATTN_BLOCK, HEAD_DIM), F32)
            for hh in range(GROUP):
                h = g * GROUP + hh
                hs = slice(h * HEAD_DIM, (h + 1) * HEAD_DIM)
                slope = 2.0 ** (-8.0 * (h + 1) / nq)
                q = q_ref[:, hs].astype(BF16)
                do = do_ref[:, hs].astype(BF16)
                pp, pc, ps = _attn_scores(q, k_p, k_c, slope, sink_ref[0, h], *masks)
                dpp = lax.dot_general(do, v_p, dn_t, preferred_element_type=F32)
                dpc = lax.dot_general(do, v_c, dn_t, preferred_element_type=F32)
                delta = jnp.sum(pp * dpp, axis=1, keepdims=True) + jnp.sum(pc * dpc, axis=1, keepdims=True)
                dsp = (pp * (dpp - delta)).astype(BF16)
                dsc = (pc * (dpc - delta)).astype(BF16)
                dsink = -jnp.sum(ps * delta)
                ds_acc = ds_acc + jnp.where(jnp.logical_and(lane == h, srow == 0), dsink, 0.0)
                dq = (jnp.dot(dsp, k_p, preferred_element_type=F32)
                      + jnp.dot(dsc, k_c, preferred_element_type=F32)) * scale
                dq_ref[:, hs] = dq.astype(BF16)
                dkp = dkp + lax.dot_general(dsp, q, dn_r, preferred_element_type=F32) * scale
                dkc = dkc + lax.dot_general(dsc, q, dn_r, preferred_element_type=F32) * scale
                dvp = dvp + lax.dot_general(pp.astype(BF16), do, dn_r, preferred_element_type=F32)
                dvc = dvc + lax.dot_general(pc.astype(BF16), do, dn_r, preferred_element_type=F32)
            dk_ref[rows_p, ks] += dkp
            dv_ref[rows_p, ks] += dvp
            dk_ref[rows_c, ks] += dkc
            dv_ref[rows_c, ks] += dvc
        ds_ref[...] += ds_acc

    out_shape = (jax.ShapeDtypeStruct((T, d_attn), BF16), jax.ShapeDtypeStruct((T, d_kv), F32),
                 jax.ShapeDtypeStruct((T, d_kv), F32), jax.ShapeDtypeStruct((8, LANES), F32))
    return pl.pallas_call(
        body, name="attn_bwd", out_shape=out_shape, grid=(T // ATTN_BLOCK,),
        in_specs=[pl.BlockSpec(memory_space=pltpu.SMEM), bq, kp, kc, vp, vc,
                  pl.BlockSpec((ATTN_BLOCK, d_attn), lambda n: (n, 0))],
        out_specs=(pl.BlockSpec((ATTN_BLOCK, d_attn), lambda n: (n, 0)),
                   pl.BlockSpec((T, d_kv), lambda n: (0, 0)), pl.BlockSpec((T, d_kv), lambda n: (0, 0)),
                   pl.BlockSpec((8, LANES), lambda n: (0, 0))),
        compiler_params=_cparams(("arbitrary",)),
    )(sinks, proj, proj, proj, proj, proj, d_attn_out)


def _rnn_tile(T):
    return _pick(T, (256, 128))


def _rnn_gates(x_ext, cw_ref, cb_ref, wa_ref, wi_ref, ba_ref, bi_ref, lam_ref, tt):
    xs = [pltpu.roll(x_ext, 3 - k, 0)[8:, :] if k < 3 else x_ext[8:, :] for k in range(4)]
    cx = cb_ref[...] + xs[0] * cw_ref[0:1, :]
    for k in range(1, 4):
        cx = cx + xs[k] * cw_ref[k:k + 1, :]
    cxb = cx.astype(BF16)
    r = jax.nn.sigmoid(jnp.dot(cxb, wa_ref[...], preferred_element_type=F32) + ba_ref[...])
    i = jax.nn.sigmoid(jnp.dot(cxb, wi_ref[...], preferred_element_type=F32) + bi_ref[...])
    lam = lam_ref[...]
    sp = jnp.maximum(-lam, 0.0) + jnp.log1p(jnp.exp(-jnp.abs(lam)))
    log_a = -LRU_C * r * sp
    a = jnp.exp(log_a)
    z = 2.0 * log_a
    em1 = jnp.where(z > -1e-2, z * (1.0 + z * (0.5 + z * (1.0 / 6.0 + z * (1.0 / 24.0)))), jnp.exp(z) - 1.0)
    s = jnp.sqrt(-em1)
    return xs, cx, r, i, sp, a, s


def _rnn_specs(T, gw, tt, rx_blk, ry_blk, rev):
    nT = T // tt
    hb = tt // 8

    def tile(t):
        return (nT - 1 - t) if rev else t

    rx = pl.BlockSpec((tt, gw), lambda g, t: (tile(t), rx_blk + g))
    rx_halo = pl.BlockSpec((8, gw), lambda g, t: (jnp.maximum(tile(t) * hb - 1, 0), rx_blk + g))
    ry = pl.BlockSpec((tt, gw), lambda g, t: (tile(t), ry_blk + g))
    cw = pl.BlockSpec((4, gw), lambda g, t: (0, g))
    vec = pl.BlockSpec((1, gw), lambda g, t: (0, g))
    wg = pl.BlockSpec((None, gw, gw), lambda g, t: (g, 0, 0))
    act = pl.BlockSpec((tt, gw), lambda g, t: (tile(t), g))
    act_halo = pl.BlockSpec((8, gw), lambda g, t: (jnp.maximum(tile(t) * hb - 1, 0), g))
    return rx, rx_halo, ry, cw, vec, wg, act, act_halo, tile


def _rnn_fwd(proj, cols, conv_w, conv_b, wa_g, wi_g, ba, bi, lam):
    T = proj.shape[0]
    G, gw, _ = wa_g.shape
    d_rnn = G * gw
    tt = _rnn_tile(T)
    rx_off, ry_off = cols
    rx, rx_halo, ry, cw, vec, wg, act, _, _ = _rnn_specs(T, gw, tt, rx_off // gw, ry_off // gw, False)

    def body(rx_ref, rxh_ref, ry_ref, cw_ref, cb_ref, wa_ref, wi_ref, ba_ref, bi_ref, lam_ref,
             b_ref, h_ref, carry):
        t = pl.program_id(1)

        @pl.when(t == 0)
        def _():
            carry[...] = jnp.zeros_like(carry)

        halo = jnp.where(t > 0, rxh_ref[...], 0.0)
        x_ext = jnp.concatenate([halo, rx_ref[...]], axis=0)
        _, cx, _, i, _, a, s = _rnn_gates(x_ext, cw_ref, cb_ref, wa_ref, wi_ref, ba_ref, bi_ref, lam_ref, tt)
        acc_a, acc_b = a, s * (i * cx)
        d = 1
        while d < tt:
            acc_b = acc_a * _shift_down(acc_b, d, 0.0) + acc_b
            acc_a = acc_a * _shift_down(acc_a, d, 1.0)
            d *= 2
        h = acc_b + acc_a * carry[7:8, :]
        carry[...] = h[tt - 8:, :]
        h_ref[...] = h
        b_ref[...] = (h * _gelu(ry_ref[...])).astype(BF16)

    return pl.pallas_call(
        body, name="rnn_fwd",
        out_shape=(jax.ShapeDtypeStruct((T, d_rnn), BF16), jax.ShapeDtypeStruct((T, d_rnn), F32)),
        grid=(G, T // tt),
        in_specs=[rx, rx_halo, ry, cw, vec, wg, wg, vec, vec, vec], out_specs=(act, act),
        scratch_shapes=[pltpu.VMEM((8, gw), F32)],
        compiler_params=_cparams(("parallel", "arbitrary")),
    )(proj, proj, proj, conv_w, conv_b, wa_g, wi_g, ba, bi, lam)


def _rnn_bwd(proj, cols, h_all, d_b, conv_w, conv_b, wa_g, wi_g, ba, bi, lam):
    T = proj.shape[0]
    G, gw, _ = wa_g.shape
    d_rnn = G * gw
    tt = _rnn_tile(T)
    nT = T // tt
    rx_off, ry_off = cols
    rx, rx_halo, ry, cw, vec, wg, act, act_halo, _ = _rnn_specs(T, gw, tt, rx_off // gw, ry_off // gw, True)
    dn_t = (((1,), (1,)), ((), ()))
    dn_r = (((0,), (0,)), ((), ()))

    def body(rx_ref, rxh_ref, ry_ref, h_ref, hh_ref, db_ref, cw_ref, cb_ref, wa_ref, wi_ref, ba_ref, bi_ref, lam_ref,
             drx_ref, dry_ref, dcw_ref, dcb_ref, dba_ref, dbi_ref, dlam_ref, dwa_ref, dwi_ref,
             lam_carry, dcx_carry):
        t = pl.program_id(1)
        first_tile = t == nT - 1

        @pl.when(t == 0)
        def _():
            lam_carry[...] = jnp.zeros_like(lam_carry)
            dcx_carry[...] = jnp.zeros_like(dcx_carry)
            dcw_ref[...] = jnp.zeros_like(dcw_ref)
            dcb_ref[...] = jnp.zeros_like(dcb_ref)
            dba_ref[...] = jnp.zeros_like(dba_ref)
            dbi_ref[...] = jnp.zeros_like(dbi_ref)
            dlam_ref[...] = jnp.zeros_like(dlam_ref)
            dwa_ref[...] = jnp.zeros_like(dwa_ref)
            dwi_ref[...] = jnp.zeros_like(dwi_ref)

        halo = jnp.where(first_tile, 0.0, rxh_ref[...])
        x_ext = jnp.concatenate([halo, rx_ref[...]], axis=0)
        xs, cx, r, i, sp, a, s = _rnn_gates(x_ext, cw_ref, cb_ref, wa_ref, wi_ref, ba_ref, bi_ref, lam_ref, tt)
        h = h_ref[...]
        h_halo = jnp.where(first_tile, 0.0, hh_ref[...])
        h_prev = pltpu.roll(jnp.concatenate([h_halo, h], axis=0), 1, 0)[8:, :]
        gel, dgel = _gelu_and_grad(ry_ref[...])
        d_b_t = db_ref[...]
        dry_ref[...] = (d_b_t * h * dgel).astype(BF16)
        dh = d_b_t * gel

        acc_c = _shift_up(a, 1, 1.0)
        acc_l = dh
        d = 1
        while d < tt:
            acc_l = acc_c * _shift_up(acc_l, d, 0.0) + acc_l
            acc_c = acc_c * _shift_up(acc_c, d, 1.0)
            d *= 2
        lam_t = acc_l + acc_c * lam_carry[0:1, :]
        lam_carry[...] = (a * lam_t)[0:8, :]

        icx = i * cx
        d_s = lam_t * icx
        d_i = lam_t * s * cx
        dcx = lam_t * s * i
        d_a = lam_t * h_prev - d_s * (a / s)
        dlog_a = d_a * a
        d_r = dlog_a * (-LRU_C * sp)
        lam = lam_ref[...]
        dlam_ref[...] += jnp.sum(dlog_a * r, axis=0, keepdims=True) * (LRU_C * jax.nn.sigmoid(-lam))
        dpr = d_r * r * (1.0 - r)
        dpi = d_i * i * (1.0 - i)
        dba_ref[...] += jnp.sum(dpr, axis=0, keepdims=True)
        dbi_ref[...] += jnp.sum(dpi, axis=0, keepdims=True)
        cxb = cx.astype(BF16)
        dprb, dpib = dpr.astype(BF16), dpi.astype(BF16)
        dwa_ref[...] += lax.dot_general(cxb, dprb, dn_r, preferred_element_type=F32)
        dwi_ref[...] += lax.dot_general(cxb, dpib, dn_r, preferred_element_type=F32)
        dcx = (dcx + lax.dot_general(dprb, wa_ref[...], dn_t, preferred_element_type=F32)
               + lax.dot_general(dpib, wi_ref[...], dn_t, preferred_element_type=F32))

        dcb_ref[...] += jnp.sum(dcx, axis=0, keepdims=True)
        for k in range(4):
            dcw_ref[k:k + 1, :] += jnp.sum(dcx * xs[k], axis=0, keepdims=True)
        d_ext = jnp.concatenate([dcx, dcx_carry[...]], axis=0)
        drx = dcx * cw_ref[3:4, :]
        for k in range(3):
            drx = drx + pltpu.roll(d_ext, tt + 8 - (3 - k), 0)[:tt, :] * cw_ref[k:k + 1, :]
        drx_ref[...] = drx.astype(BF16)
        dcx_carry[...] = dcx[0:8, :]

    out_shape = (jax.ShapeDtypeStruct((T, d_rnn), BF16), jax.ShapeDtypeStruct((T, d_rnn), BF16),
                 jax.ShapeDtypeStruct((4, d_rnn), F32), jax.ShapeDtypeStruct((1, d_rnn), F32),
                 jax.ShapeDtypeStruct((1, d_rnn), F32), jax.ShapeDtypeStruct((1, d_rnn), F32),
                 jax.ShapeDtypeStruct((1, d_rnn), F32), jax.ShapeDtypeStruct((G, gw, gw), F32),
                 jax.ShapeDtypeStruct((G, gw, gw), F32))
    return pl.pallas_call(
        body, name="rnn_bwd", out_shape=out_shape, grid=(G, nT),
        in_specs=[rx, rx_halo, ry, act, act_halo, act, cw, vec, wg, wg, vec, vec, vec],
        out_specs=(act, act, cw, vec, vec, vec, vec, wg, wg),
        scratch_shapes=[pltpu.VMEM((8, gw), F32), pltpu.VMEM((8, gw), F32)],
        compiler_params=_cparams(("parallel", "arbitrary")),
    )(proj, proj, proj, h_all, h_all, d_b, conv_w, conv_b, wa_g, wi_g, ba, bi, lam)


def _merge_fwd(proj, gl_off, b_gate, y_attn, y_rnn):
    T, D = y_attn.shape
    tm = _pick(T, (256, 128))
    ct = _pick(math.gcd(gl_off, D), (512, 256, 128))
    oa, orr, nd = gl_off // ct, (gl_off + D) // ct, D // ct

    def body(ga_ref, gr_ref, ba_ref, br_ref, ya_ref, yr_ref, m_ref):
        ga = jax.nn.sigmoid(ga_ref[...] + ba_ref[...])
        gr = jax.nn.sigmoid(gr_ref[...] + br_ref[...])
        m_ref[...] = (ga * ya_ref[...] + gr * yr_ref[...]).astype(BF16)

    blk = pl.BlockSpec((tm, ct), lambda i, j: (i, j))
    return pl.pallas_call(
        body, name="merge_fwd", out_shape=jax.ShapeDtypeStruct((T, D), BF16), grid=(T // tm, nd),
        in_specs=[pl.BlockSpec((tm, ct), lambda i, j: (i, oa + j)), pl.BlockSpec((tm, ct), lambda i, j: (i, orr + j)),
                  pl.BlockSpec((1, ct), lambda i, j: (0, j)), pl.BlockSpec((1, ct), lambda i, j: (0, nd + j)),
                  blk, blk],
        out_specs=blk, compiler_params=_cparams(("parallel", "parallel")),
    )(proj, proj, b_gate, b_gate, y_attn, y_rnn)


def _merge_bwd(proj, gl_off, b_gate, y_attn, y_rnn, d_m):
    T, D = y_attn.shape
    tm = _pick(T, (256, 128))
    ct = _pick(math.gcd(gl_off, D), (512, 256, 128))
    oa, orr, nd = gl_off // ct, (gl_off + D) // ct, D // ct

    def body(ga_ref, gr_ref, ba_ref, br_ref, ya_ref, yr_ref, dm_ref,
             dya_ref, dyr_ref, dga_ref, dgr_ref, dba_ref, dbr_ref):
        i = pl.program_id(1)

        @pl.when(i == 0)
        def _():
            dba_ref[...] = jnp.zeros_like(dba_ref)
            dbr_ref[...] = jnp.zeros_like(dbr_ref)

        ga = jax.nn.sigmoid(ga_ref[...] + ba_ref[...])
        gr = jax.nn.sigmoid(gr_ref[...] + br_ref[...])
        dm = dm_ref[...]
        dya_ref[...] = (dm * ga).astype(BF16)
        dyr_ref[...] = (dm * gr).astype(BF16)
        dga = dm * ya_ref[...] * ga * (1.0 - ga)
        dgr = dm * yr_ref[...] * gr * (1.0 - gr)
        dga_ref[...] = dga.astype(BF16)
        dgr_ref[...] = dgr.astype(BF16)
        dba_ref[...] += jnp.sum(dga, axis=0, keepdims=True)
        dbr_ref[...] += jnp.sum(dgr, axis=0, keepdims=True)

    blk = pl.BlockSpec((tm, ct), lambda j, i: (i, j))
    vec = pl.BlockSpec((1, ct), lambda j, i: (0, j))
    act = jax.ShapeDtypeStruct((T, D), BF16)
    v1 = jax.ShapeDtypeStruct((1, D), F32)
    return pl.pallas_call(
        body, name="merge_bwd", out_shape=(act, act, act, act, v1, v1), grid=(nd, T // tm),
        in_specs=[pl.BlockSpec((tm, ct), lambda j, i: (i, oa + j)), pl.BlockSpec((tm, ct), lambda j, i: (i, orr + j)),
                  vec, pl.BlockSpec((1, ct), lambda j, i: (0, nd + j)), blk, blk, blk],
        out_specs=(blk, blk, blk, blk, vec, vec),
        compiler_params=_cparams(("parallel", "arbitrary")),
    )(proj, proj, b_gate, b_gate, y_attn, y_rnn, d_m)


def _ln_fwd(x_res, delta, g, b, name):
    T, D = x_res.shape
    tm = _pick(T, (256, 128))

    def body(x_ref, d_ref, g_ref, b_ref, y_ref, yb_ref, xh_ref, rs_ref):
        z = ALPHA * x_ref[...] + d_ref[...]
        mu = jnp.mean(z, axis=1, keepdims=True)
        zc = z - mu
        var = jnp.mean(zc * zc, axis=1, keepdims=True)
        rstd = lax.rsqrt(var + LN_EPS)
        xh = zc * rstd
        xh_ref[...] = xh
        rs_ref[...] = rstd
        y = xh * g_ref[...] + b_ref[...]
        y_ref[...] = y
        yb_ref[...] = y.astype(BF16)

    row = pl.BlockSpec((tm, D), lambda i: (i, 0))
    vec = pl.BlockSpec((1, D), lambda i: (0, 0))
    return pl.pallas_call(
        body, name=name,
        out_shape=(jax.ShapeDtypeStruct((T, D), F32), jax.ShapeDtypeStruct((T, D), BF16),
                   jax.ShapeDtypeStruct((T, D), F32), jax.ShapeDtypeStruct((T, 1), F32)),
        grid=(T // tm,), in_specs=[row, row, vec, vec],
        out_specs=(row, row, row, pl.BlockSpec((tm, 1), lambda i: (i, 0))),
        compiler_params=_cparams(("parallel",)),
    )(x_res, delta, g, b)


def _ln_bwd_rows(dy, xh, rstd, g):
    dxh = dy * g
    m1 = jnp.mean(dxh, axis=1, keepdims=True)
    m2 = jnp.mean(dxh * xh, axis=1, keepdims=True)
    return rstd * (dxh - m1 - xh * m2)


def _ln_loss_bwd(x_res, delta, g, b, target):
    T, D = x_res.shape
    tm = _pick(T, (256, 128))

    def body(x_ref, d_ref, g_ref, b_ref, t_ref, dz_ref, dzb_ref, loss_ref, dg_ref, db_ref):
        i = pl.program_id(0)

        @pl.when(i == 0)
        def _():
            loss_ref[...] = jnp.zeros_like(loss_ref)
            dg_ref[...] = jnp.zeros_like(dg_ref)
            db_ref[...] = jnp.zeros_like(db_ref)

        z = ALPHA * x_ref[...] + d_ref[...]
        mu = jnp.mean(z, axis=1, keepdims=True)
        zc = z - mu
        var = jnp.mean(zc * zc, axis=1, keepdims=True)
        rstd = lax.rsqrt(var + LN_EPS)
        xh = zc * rstd
        gv = g_ref[...]
        err = xh * gv + b_ref[...] - t_ref[...]
        loss_ref[...] += 0.5 * jnp.sum(jnp.mean(err * err, axis=1, keepdims=True))
        dy = err * (1.0 / D)
        dg_ref[...] += jnp.sum(dy * xh, axis=0, keepdims=True)
        db_ref[...] += jnp.sum(dy, axis=0, keepdims=True)
        dz = _ln_bwd_rows(dy, xh, rstd, gv)
        dz_ref[...] = dz
        dzb_ref[...] = dz.astype(BF16)

    row = pl.BlockSpec((tm, D), lambda i: (i, 0))
    vec = pl.BlockSpec((1, D), lambda i: (0, 0))
    return pl.pallas_call(
        body, name="ln2_loss_bwd",
        out_shape=(jax.ShapeDtypeStruct((T, D), F32), jax.ShapeDtypeStruct((T, D), BF16),
                   jax.ShapeDtypeStruct((8, LANES), F32),
                   jax.ShapeDtypeStruct((1, D), F32), jax.ShapeDtypeStruct((1, D), F32)),
        grid=(T // tm,), in_specs=[row, row, vec, vec, row],
        out_specs=(row, row, pl.BlockSpec((8, LANES), lambda i: (0, 0)), vec, vec),
        compiler_params=_cparams(("arbitrary",)),
    )(x_res, delta, g, b, target)


def _ln_bwd(dy, xh, rstd, g):
    T, D = dy.shape
    tm = _pick(T, (256, 128))

    def body(dy_ref, xh_ref, rs_ref, g_ref, dz_ref, dzb_ref, dg_ref, db_ref):
        i = pl.program_id(0)

        @pl.when(i == 0)
        def _():
            dg_ref[...] = jnp.zeros_like(dg_ref)
            db_ref[...] = jnp.zeros_like(db_ref)

        dyv, xhv = dy_ref[...], xh_ref[...]
        dg_ref[...] += jnp.sum(dyv * xhv, axis=0, keepdims=True)
        db_ref[...] += jnp.sum(dyv, axis=0, keepdims=True)
        dz = _ln_bwd_rows(dyv, xhv, rs_ref[...], g_ref[...])
        dz_ref[...] = dz
        dzb_ref[...] = dz.astype(BF16)

    row = pl.BlockSpec((tm, D), lambda i: (i, 0))
    vec = pl.BlockSpec((1, D), lambda i: (0, 0))
    return pl.pallas_call(
        body, name="ln1_bwd",
        out_shape=(jax.ShapeDtypeStruct((T, D), F32), jax.ShapeDtypeStruct((T, D), BF16),
                   jax.ShapeDtypeStruct((1, D), F32), jax.ShapeDtypeStruct((1, D), F32)),
        grid=(T // tm,), in_specs=[row, row, pl.BlockSpec((tm, 1), lambda i: (i, 0)), vec],
        out_specs=(row, row, vec, vec), compiler_params=_cparams(("arbitrary",)),
    )(dy, xh, rstd, g)


def _ffn_col_tile(T, d_ff):
    return _pick(d_ff, (256, 128)) if T >= 1024 else _pick(d_ff, (512, 256, 128))


def _ffn_gate(gp, cw_ref, cb_ref):
    return (cb_ref[...] + gp * cw_ref[2:3, :] + _shift_down(gp, 1) * cw_ref[1:2, :]
            + _shift_down(gp, 2) * cw_ref[0:1, :])


def _ffn_fwd(up, gpre, conv_w, conv_b):
    T, d_ff = up.shape
    ct = _ffn_col_tile(T, d_ff)

    def body(up_ref, gp_ref, cw_ref, cb_ref, f_ref):
        gate = _ffn_gate(gp_ref[...], cw_ref, cb_ref)
        f_ref[...] = (_gelu(gate) * up_ref[...]).astype(BF16)

    col = pl.BlockSpec((T, ct), lambda j: (0, j))
    return pl.pallas_call(
        body, name="ffn_act_fwd", out_shape=jax.ShapeDtypeStruct((T, d_ff), BF16), grid=(d_ff // ct,),
        in_specs=[col, col, pl.BlockSpec((3, ct), lambda j: (0, j)), pl.BlockSpec((1, ct), lambda j: (0, j))],
        out_specs=col, compiler_params=_cparams(("parallel",)),
    )(up, gpre, conv_w, conv_b)


def _ffn_bwd(up, gpre, conv_w, conv_b, d_f):
    T, d_ff = up.shape
    ct = _ffn_col_tile(T, d_ff)

    def body(up_ref, gp_ref, cw_ref, cb_ref, df_ref, dup_ref, dgp_ref, dcw_ref, dcb_ref):
        gp = gp_ref[...]
        gate = _ffn_gate(gp, cw_ref, cb_ref)
        gel, dgel = _gelu_and_grad(gate)
        df = df_ref[...]
        dup_ref[...] = (df * gel).astype(BF16)
        dgate = df * up_ref[...] * dgel
        dcb_ref[...] = jnp.sum(dgate, axis=0, keepdims=True)
        dcw_ref[2:3, :] = jnp.sum(dgate * gp, axis=0, keepdims=True)
        dcw_ref[1:2, :] = jnp.sum(dgate * _shift_down(gp, 1), axis=0, keepdims=True)
        dcw_ref[0:1, :] = jnp.sum(dgate * _shift_down(gp, 2), axis=0, keepdims=True)
        dgp = (dgate * cw_ref[2:3, :] + _shift_up(dgate, 1) * cw_ref[1:2, :]
               + _shift_up(dgate, 2) * cw_ref[0:1, :])
        dgp_ref[...] = dgp.astype(BF16)

    col = pl.BlockSpec((T, ct), lambda j: (0, j))
    w3 = pl.BlockSpec((3, ct), lambda j: (0, j))
    v1 = pl.BlockSpec((1, ct), lambda j: (0, j))
    return pl.pallas_call(
        body, name="ffn_act_bwd",
        out_shape=(jax.ShapeDtypeStruct((T, d_ff), BF16), jax.ShapeDtypeStruct((T, d_ff), BF16),
                   jax.ShapeDtypeStruct((3, d_ff), F32), jax.ShapeDtypeStruct((1, d_ff), F32)),
        grid=(d_ff // ct,), in_specs=[col, col, w3, v1, col], out_specs=(col, col, w3, v1),
        compiler_params=_cparams(("parallel",)),
    )(up, gpre, conv_w, conv_b, d_f)


def _adamw(w, g, m, v, name):
    R, C = w.shape
    tr = _row_tile(R, C * 4, 8, budget=1536 * 1024)
    c1 = 1.0 / (1.0 - ADAM_B1 ** ADAM_STEP)
    c2 = 1.0 / (1.0 - ADAM_B2 ** ADAM_STEP)

    def body(w_ref, g_ref, m_ref, v_ref, d_ref, nm_ref, nv_ref):
        gv = g_ref[...]
        nm = ADAM_B1 * m_ref[...] + (1.0 - ADAM_B1) * gv
        nv = ADAM_B2 * v_ref[...] + (1.0 - ADAM_B2) * (gv * gv)
        nm_ref[...] = nm
        nv_ref[...] = nv
        d_ref[...] = -ADAM_LR * ((nm * c1) / (jnp.sqrt(nv * c2) + ADAM_EPS) + ADAM_WD * w_ref[...])

    blk = pl.BlockSpec((tr, C), lambda r: (r, 0))
    sh = jax.ShapeDtypeStruct((R, C), F32)
    return pl.pallas_call(
        body, name=name, out_shape=(sh, sh, sh), grid=(R // tr,), in_specs=[blk] * 4, out_specs=(blk,) * 3,
        compiler_params=_cparams(("parallel",)),
    )(w, g, m, v)


def _group_blocks(w_blocks, per):
    nb, bw, _ = w_blocks.shape
    G = nb // per
    w4 = w_blocks.reshape(G, per, bw, bw)
    rows = []
    for p in range(per):
        parts = [w4[:, p] if q == p else jnp.zeros((G, bw, bw), w_blocks.dtype) for q in range(per)]
        rows.append(jnp.concatenate(parts, axis=2))
    return jnp.concatenate(rows, axis=1)


def _ungroup_blocks(w_groups, per):
    G, gw, _ = w_groups.shape
    bw = gw // per
    blocks = [w_groups[:, p * bw:(p + 1) * bw, p * bw:(p + 1) * bw] for p in range(per)]
    return jnp.stack(blocks, axis=1).reshape(G * per, bw, bw)


def _pack(parts):
    flat = jnp.concatenate([p.reshape(-1).astype(F32) for p in parts])
    n = flat.shape[0]
    rows = -(-n // LANES)
    rows = -(-rows // PACK_ROW_MULT) * PACK_ROW_MULT
    flat = jnp.pad(flat, (0, rows * LANES - n))
    return flat.reshape(rows, LANES)


def _unpack(packed, shapes):
    flat = packed.reshape(-1)
    out, off = [], 0
    for s in shapes:
        n = math.prod(s)
        out.append(flat[off:off + n].reshape(s))
        off += n
    return out


def kernel(x, w_in, b_gate, rnn_conv_w, rnn_conv_b, lru_wa, lru_ba, lru_wi, lru_bi, lru_lambda, attn_sinks, w_attn_proj, w_rnn_proj, w_out, ln1_g, ln1_b, ffn_w_up, ffn_w_gate, ffn_conv_w, ffn_conv_b, ffn_w_down, ln2_g, ln2_b, loss_target, m_w_in, m_b_gate, m_rnn_conv_w, m_rnn_conv_b, m_lru_wa, m_lru_ba, m_lru_wi, m_lru_bi, m_lru_lambda, m_attn_sinks, m_w_attn_proj, m_w_rnn_proj, m_w_out, m_ln1_g, m_ln1_b, m_ffn_w_up, m_ffn_w_gate, m_ffn_conv_w, m_ffn_conv_b, m_ffn_w_down, m_ln2_g, m_ln2_b, v_w_in, v_b_gate, v_rnn_conv_w, v_rnn_conv_b, v_lru_wa, v_lru_ba, v_lru_wi, v_lru_bi, v_lru_lambda, v_attn_sinks, v_w_attn_proj, v_w_rnn_proj, v_w_out, v_ln1_g, v_ln1_b, v_ffn_w_up, v_ffn_w_gate, v_ffn_conv_w, v_ffn_conv_b, v_ffn_w_down, v_ln2_g, v_ln2_b):
    weights = dict(w_in=w_in, b_gate=b_gate, rnn_conv_w=rnn_conv_w, rnn_conv_b=rnn_conv_b, lru_wa=lru_wa,
                   lru_ba=lru_ba, lru_wi=lru_wi, lru_bi=lru_bi, lru_lambda=lru_lambda, attn_sinks=attn_sinks,
                   w_attn_proj=w_attn_proj, w_rnn_proj=w_rnn_proj, w_out=w_out, ln1_g=ln1_g, ln1_b=ln1_b,
                   ffn_w_up=ffn_w_up, ffn_w_gate=ffn_w_gate, ffn_conv_w=ffn_conv_w, ffn_conv_b=ffn_conv_b,
                   ffn_w_down=ffn_w_down, ln2_g=ln2_g, ln2_b=ln2_b)
    m_in = dict(w_in=m_w_in, b_gate=m_b_gate, rnn_conv_w=m_rnn_conv_w, rnn_conv_b=m_rnn_conv_b, lru_wa=m_lru_wa,
                lru_ba=m_lru_ba, lru_wi=m_lru_wi, lru_bi=m_lru_bi, lru_lambda=m_lru_lambda, attn_sinks=m_attn_sinks,
                w_attn_proj=m_w_attn_proj, w_rnn_proj=m_w_rnn_proj, w_out=m_w_out, ln1_g=m_ln1_g, ln1_b=m_ln1_b,
                ffn_w_up=m_ffn_w_up, ffn_w_gate=m_ffn_w_gate, ffn_conv_w=m_ffn_conv_w, ffn_conv_b=m_ffn_conv_b,
                ffn_w_down=m_ffn_w_down, ln2_g=m_ln2_g, ln2_b=m_ln2_b)
    v_in = dict(w_in=v_w_in, b_gate=v_b_gate, rnn_conv_w=v_rnn_conv_w, rnn_conv_b=v_rnn_conv_b, lru_wa=v_lru_wa,
                lru_ba=v_lru_ba, lru_wi=v_lru_wi, lru_bi=v_lru_bi, lru_lambda=v_lru_lambda, attn_sinks=v_attn_sinks,
                w_attn_proj=v_w_attn_proj, w_rnn_proj=v_w_rnn_proj, w_out=v_w_out, ln1_g=v_ln1_g, ln1_b=v_ln1_b,
                ffn_w_up=v_ffn_w_up, ffn_w_gate=v_ffn_w_gate, ffn_conv_w=v_ffn_conv_w, ffn_conv_b=v_ffn_conv_b,
                ffn_w_down=v_ffn_w_down, ln2_g=v_ln2_g, ln2_b=v_ln2_b)
    order = list(weights)

    assert x.shape[0] == 1 and w_in.shape[0] == 1, "one sequence per device, depth 1"
    T, D = x.shape[1], x.shape[2]
    nq = attn_sinks.shape[-1]
    nkv = nq // GROUP
    d_attn, d_kv = nq * HEAD_DIM, nkv * HEAD_DIM
    d_rnn = rnn_conv_b.shape[-1]
    d_ff = ffn_conv_b.shape[-1]
    n_blocks, bw = lru_wa.shape[1], lru_wa.shape[2]
    per = (bw * LANES // math.gcd(bw, LANES)) // bw
    gw = per * bw
    assert n_blocks % per == 0 and d_rnn == n_blocks * bw
    q_off, k_off, v_off = 0, d_attn, d_attn + d_kv
    rx_off = d_attn + 2 * d_kv
    ry_off = rx_off + d_rnn
    gl_off = ry_off + d_rnn
    d_in = gl_off + 2 * D
    assert w_in.shape[-1] * N_SHARDS == d_in
    assert k_off % d_kv == 0 and rx_off % gw == 0 and T % ATTN_BLOCK == 0

    xi, yi, ci = lax.axis_index("x"), lax.axis_index("y"), lax.axis_index("c")
    j_me = 2 * xi + yi
    jc_arr = jnp.stack([j_me, ci]).astype(jnp.int32)

    x0 = x[0]
    x0b = _cast_bf16(x0, "cast_x")
    tgt = loss_target[0]
    big = ["w_in", "w_attn_proj", "w_rnn_proj", "w_out", "ffn_w_up", "ffn_w_gate", "ffn_w_down"]
    own_slots = [_cast_bf16_into_slot(weights[n][0], jc_arr, "cast_" + n) for n in big]
    full = dict(zip(big, _all_gather_weights(own_slots)))
    w_in_s = full["w_in"]
    w_up_s, w_gate_s = full["ffn_w_up"], full["ffn_w_gate"]
    w_ap = full["w_attn_proj"].reshape(d_attn, D)
    w_rp = full["w_rnn_proj"].reshape(d_rnn, D)
    w_o = full["w_out"].reshape(D, D)
    w_dn = full["ffn_w_down"].reshape(d_ff, D)

    rcw_s, fcw_s = _all_gather_small([rnn_conv_w[0], ffn_conv_w[0]])
    rcw = jnp.concatenate([rcw_s[j] for j in range(N_SHARDS)], axis=1)
    fcw = jnp.concatenate([fcw_s[j] for j in range(N_SHARDS)], axis=1)

    wa_g = _group_blocks(lru_wa[0], per).astype(BF16)
    wi_g = _group_blocks(lru_wi[0], per).astype(BF16)

    proj = _mm(x0b, w_in_s, name="mm_proj", b_shards=N_SHARDS)
    a_out = _attn_fwd(proj, attn_sinks, nq, (q_off, k_off, v_off))
    b_out, h_all = _rnn_fwd(proj, (rx_off, ry_off), rcw, rnn_conv_b, wa_g, wi_g, lru_ba, lru_bi, lru_lambda)
    y_attn = _mm(a_out, w_ap, name="mm_attn_proj")
    y_rnn = _mm(b_out, w_rp, name="mm_rnn_proj")
    merged = _merge_fwd(proj, gl_off, b_gate, y_attn, y_rnn)
    mix = _mm(merged, w_o, name="mm_out")
    x1, x1b, xh1, rstd1 = _ln_fwd(x0, mix, ln1_g, ln1_b, "ln1_fwd")
    up = _mm(x1b, w_up_s, name="mm_up", b_shards=N_SHARDS)
    gpre = _mm(x1b, w_gate_s, name="mm_gate", b_shards=N_SHARDS)
    f_act = _ffn_fwd(up, gpre, fcw, ffn_conv_b)
    f_out = _mm(f_act, w_dn, name="mm_down")
    dz2, dz2b, loss_acc, dg2, db2 = _ln_loss_bwd(x1, f_out, ln2_g, ln2_b, tgt)

    g_down = _mm(f_act, dz2b, name="mm_d_w_down", ta=True, out_dtype=BF16)
    d_f = _mm(dz2b, w_dn, name="mm_d_f", tb=True)
    dup, dgp, d_fcw, d_fcb = _ffn_bwd(up, gpre, fcw, ffn_conv_b, d_f)
    g_up = _mm(x1b, dup, name="mm_d_w_up", ta=True, out_dtype=BF16, out_shards=N_SHARDS)
    g_gate = _mm(x1b, dgp, name="mm_d_w_gate", ta=True, out_dtype=BF16, out_shards=N_SHARDS)
    dx1_a = _mm(dup, w_up_s, name="mm_dx1_up", tb=True, b_shards=N_SHARDS, adds=((ALPHA, dz2),))
    dx1 = _mm(dgp, w_gate_s, name="mm_dx1_gate", tb=True, b_shards=N_SHARDS, adds=((1.0, dx1_a),))
    dz1, dz1b, dg1, db1 = _ln_bwd(dx1, xh1, rstd1, ln1_g)
    g_out = _mm(merged, dz1b, name="mm_d_w_out", ta=True, out_dtype=BF16)
    d_m = _mm(dz1b, w_o, name="mm_d_merged", tb=True)
    dya, dyr, dgl_a, dgl_r, dbg_a, dbg_r = _merge_bwd(proj, gl_off, b_gate, y_attn, y_rnn, d_m)
    g_ap = _mm(a_out, dya, name="mm_d_w_attn_proj", ta=True, out_dtype=BF16)
    g_rp = _mm(b_out, dyr, name="mm_d_w_rnn_proj", ta=True, out_dtype=BF16)
    d_a = _mm(dya, w_ap, name="mm_d_attn", tb=True)
    d_b = _mm(dyr, w_rp, name="mm_d_rnn", tb=True)
    dq, dk, dv, dsink = _attn_bwd(proj, d_a, attn_sinks, nq, (q_off, k_off, v_off))
    (drx, dry, d_rcw, d_rcb, d_ba, d_bi, d_lam, d_wa_g, d_wi_g) = _rnn_bwd(
        proj, (rx_off, ry_off), h_all, d_b, rcw, rnn_conv_b, wa_g, wi_g, lru_ba, lru_bi, lru_lambda)
    d_proj = jnp.concatenate([dq, dk.astype(BF16), dv.astype(BF16), drx, dry, dgl_a, dgl_r], axis=1)
    g_in = _mm(x0b, d_proj, name="mm_d_w_in", ta=True, out_dtype=BF16, out_shards=N_SHARDS)
    grad_x = _mm(d_proj, w_in_s, name="mm_d_x", tb=True, b_shards=N_SHARDS, adds=((ALPHA, dz1),))

    small_parts = [
        ("loss", loss_acc[0:1, 0:1]),
        ("b_gate", jnp.concatenate([dbg_a, dbg_r], axis=1)),
        ("rnn_conv_w", d_rcw), ("rnn_conv_b", d_rcb),
        ("lru_wa", _ungroup_blocks(d_wa_g, per)), ("lru_ba", d_ba),
        ("lru_wi", _ungroup_blocks(d_wi_g, per)), ("lru_bi", d_bi), ("lru_lambda", d_lam),
        ("attn_sinks", dsink[0:1, 0:nq]),
        ("ln1_g", dg1), ("ln1_b", db1),
        ("ffn_conv_w", d_fcw), ("ffn_conv_b", d_fcb),
        ("ln2_g", dg2), ("ln2_b", db2),
    ]
    packed = _pack([p for _, p in small_parts])
    rs = packed.shape[0]

    g_big = [g_in, g_ap.reshape(N_SHARDS, d_attn // N_SHARDS, D), g_rp.reshape(N_SHARDS, d_rnn // N_SHARDS, D),
             g_out.reshape(N_SHARDS, D // N_SHARDS, D), g_up, g_gate, g_down.reshape(N_SHARDS, d_ff // N_SHARDS, D)]
    parts = g_big + [packed.reshape(N_SHARDS, rs // N_SHARDS, LANES)]
    names = big + ["small"]
    from_sibling = _pair_exchange(parts)
    chip_part = [_pair_sum(g, la, jc_arr, "pair_sum_" + n) for g, la, n in zip(parts, from_sibling, names)]
    from_chips = _shard_exchange(chip_part)
    halves = [_shard_sum(cp, lb, jc_arr, "shard_sum_" + n) for cp, lb, n in zip(chip_part[:-1], from_chips[:-1], big)]
    eighths = _shard_sum(chip_part[-1], from_chips[-1], jc_arr, "shard_sum_small", all_slots=True)
    shared = _share_results(halves, eighths)
    grads = {n: g.reshape(2 * g.shape[1], g.shape[2]) for n, g in zip(big, shared[:-1])}
    reduced = shared[-1].reshape(rs, LANES)
    small = dict(zip([n for n, _ in small_parts], _unpack(reduced, [p.shape for _, p in small_parts])))
    loss = small.pop("loss").reshape(())
    rcw_n = d_rnn // N_SHARDS
    fcw_n = d_ff // N_SHARDS
    small["rnn_conv_w"] = lax.dynamic_slice(small["rnn_conv_w"], (0, j_me * rcw_n), (4, rcw_n))
    small["ffn_conv_w"] = lax.dynamic_slice(small["ffn_conv_w"], (0, j_me * fcw_n), (3, fcw_n))
    for n, g in small.items():
        grads[n] = g

    out_g, out_d, out_m, out_v = {}, {}, {}, {}
    for n in order:
        w_full = weights[n]
        shape = w_full.shape
        two_d = (math.prod(shape[:-1]), shape[-1])
        g2 = grads[n].reshape(two_d)
        d2, m2, v2 = _adamw(w_full.reshape(two_d), g2, m_in[n].reshape(two_d), v_in[n].reshape(two_d), "adamw_" + n)
        out_g[n] = g2.reshape(shape)
        out_d[n], out_m[n], out_v[n] = d2.reshape(shape), m2.reshape(shape), v2.reshape(shape)

    return (loss, grad_x.reshape(x.shape), *[out_g[n] for n in order], *[out_d[n] for n in order],
            *[out_m[n] for n in order], *[out_v[n] for n in order])
```

```python
import functools
import math

import jax
import jax.numpy as jnp
from jax import lax
from jax.experimental import pallas as pl
from jax.experimental.pallas import tpu as pltpu

F32 = jnp.float32
BF16 = jnp.bfloat16
MESH = pl.DeviceIdType.MESH

HEAD_DIM = 64
GROUP = 8
ATTN_BLOCK = 128
LRU_C = 8.0
LN_EPS = 1e-5
ALPHA = 2.0 ** 0.25
LANES = 128
N_SHARDS = 4
N_DEV = 8
VMEM_LIMIT = 56 * 1024 * 1024
MM_VMEM_BUDGET = 44 * 1024 * 1024
MM_MAX_TILE = 3072
PACK_ROW_MULT = 8 * 64
NEG = -1e30

ADAM_LR, ADAM_B1, ADAM_B2, ADAM_EPS, ADAM_WD, ADAM_STEP = 0.001, 0.9, 0.999, 1e-08, 0.01, 10

GELU_C = math.sqrt(2.0 / math.pi)
GELU_A = 0.044715


def _cparams(sem=None):
    kw = dict(vmem_limit_bytes=VMEM_LIMIT)
    if sem is not None:
        kw["dimension_semantics"] = sem
    return pltpu.CompilerParams(**kw)


def _pick(n, prefs):
    for p in prefs:
        if n % p == 0:
            return p
    return n


def _row_tile(rows, row_bytes, mult, budget=2 * 1024 * 1024):
    best = None
    for d in range(mult, rows + 1, mult):
        if rows % d == 0 and d * row_bytes <= budget:
            best = d
    return best if best is not None else rows


def _gelu(x):
    return 0.5 * x * (1.0 + jnp.tanh(GELU_C * (x + GELU_A * x * x * x)))


def _gelu_and_grad(x):
    t = jnp.tanh(GELU_C * (x + GELU_A * x * x * x))
    g = 0.5 * x * (1.0 + t)
    dg = 0.5 * (1.0 + t) + 0.5 * x * (1.0 - t * t) * GELU_C * (1.0 + 3.0 * GELU_A * x * x)
    return g, dg


def _shift_down(x, s, fill=0.0):
    row = lax.broadcasted_iota(jnp.int32, x.shape, 0)
    return jnp.where(row >= s, pltpu.roll(x, s, 0), fill)


def _shift_up(x, s, fill=0.0):
    n = x.shape[0]
    row = lax.broadcasted_iota(jnp.int32, x.shape, 0)
    return jnp.where(row < n - s, pltpu.roll(x, n - s, 0), fill)


def _mm(a, b, *, name, ta=False, tb=False, out_dtype=F32, adds=(), b_shards=1, out_shards=1,
        tm=None, tn=None, tk=None, rider=None):
    if ta:
        K, M = a.shape
    else:
        M, K = a.shape
    if b_shards > 1:
        n_sh = b.shape[-1]
        if tb:
            N = b.shape[1]
            assert b_shards * n_sh == K
        else:
            N = b_shards * n_sh
            assert b.shape[1] == K
    else:
        n_sh = None
        if tb:
            N = b.shape[0]
            assert b.shape[1] == K
        else:
            N = b.shape[1]
            assert b.shape[0] == K
    wide = (1024, 1536, 1280, 768, 640, 512, 256, 128)
    if tn is None:
        if b_shards > 1 and not tb:
            tn = n_sh if n_sh <= MM_MAX_TILE else _pick(n_sh, wide)
        elif out_shards > 1:
            tn = N // out_shards if N // out_shards <= MM_MAX_TILE else _pick(N // out_shards, wide)
        else:
            tn = _pick(N, wide)
    if tk is None:
        if b_shards > 1 and tb:
            tk = n_sh if n_sh <= MM_MAX_TILE else _pick(n_sh, wide)
        else:
            tk = K if K <= MM_MAX_TILE else _pick(K, (2048,) + wide)
    assert N % tn == 0 and K % tk == 0, (name, M, N, K, tn, tk)
    nk = K // tk
    n_add = len(adds)
    sa, sb, so = a.dtype.itemsize, b.dtype.itemsize, jnp.dtype(out_dtype).itemsize

    def vmem_bytes(tm_):
        return (2 * (tm_ * tk * sa + tk * tn * sb + tm_ * tn * so + n_add * tm_ * tn * 4)
                + (tm_ * tn * 4 if nk > 1 else 0))

    if tm is None:
        tm = _pick(M, (512, 256, 128))
        while vmem_bytes(tm) > MM_VMEM_BUDGET and tm % 256 == 0:
            tm //= 2
    assert M % tm == 0, (name, M, tm)
    b_outer = b.size * sb >= a.size * sa

    def ij(g0, g1):
        return (g1, g0) if b_outer else (g0, g1)

    def amap(g0, g1, k):
        i, _ = ij(g0, g1)
        return (k, i) if ta else (i, k)

    def bmap(g0, g1, k):
        _, j = ij(g0, g1)
        if b_shards > 1 and not tb:
            per = n_sh // tn
            return (j // per, k, j % per)
        if b_shards > 1 and tb:
            per = n_sh // tk
            return (k // per, j, k % per)
        return (j, k) if tb else (k, j)

    def omap(g0, g1, k):
        i, j = ij(g0, g1)
        if out_shards > 1:
            per_o = (N // out_shards) // tn
            return (j // per_o, i, j % per_o)
        return (i, j)

    a_spec = pl.BlockSpec((tk, tm) if ta else (tm, tk), amap)
    if b_shards > 1:
        b_spec = pl.BlockSpec((None, tn, tk) if tb else (None, tk, tn), bmap)
    else:
        b_spec = pl.BlockSpec((tn, tk) if tb else (tk, tn), bmap)
    add_specs = [pl.BlockSpec((tm, tn), lambda g0, g1, k: ij(g0, g1)) for _ in adds]
    if out_shards > 1:
        out_spec = pl.BlockSpec((None, tm, tn), omap)
        out_shape = jax.ShapeDtypeStruct((out_shards, M, N // out_shards), out_dtype)
    else:
        out_spec = pl.BlockSpec((tm, tn), omap)
        out_shape = jax.ShapeDtypeStruct((M, N), out_dtype)

    if ta:
        dims = (((0,), (0,)), ((), ()))
    elif tb:
        dims = (((1,), (1,)), ((), ()))
    else:
        dims = (((1,), (0,)), ((), ()))
    scales = tuple(s for s, _ in adds)

    def finish(r, add_refs, o_ref):
        for s, ref in zip(scales, add_refs):
            r = r + s * ref[...].astype(F32)
        o_ref[...] = r.astype(out_dtype)

    def body(a_ref, b_ref, *rest):
        add_refs = rest[:n_add]
        o_ref = rest[n_add]
        part = lax.dot_general(a_ref[...].astype(BF16), b_ref[...].astype(BF16), dims, preferred_element_type=F32)
        if nk == 1:
            finish(part, add_refs, o_ref)
            return
        acc = rest[n_add + 1]
        k = pl.program_id(2)

        @pl.when(k == 0)
        def _():
            acc[...] = part

        @pl.when(k > 0)
        def _():
            acc[...] += part

        @pl.when(k == nk - 1)
        def _():
            finish(acc[...], add_refs, o_ref)

    grid = (N // tn, M // tm, nk) if b_outer else (M // tm, N // tn, nk)
    (res,), carried = _call(
        body, name=name, grid=grid, in_specs=[a_spec, b_spec] + add_specs, out_specs=[out_spec],
        out_shape=[out_shape], scratch_shapes=[pltpu.VMEM((tm, tn), F32)] if nk > 1 else [],
        args=(a, b, *[x for _, x in adds]), sem=("parallel", "parallel", "arbitrary"), rider=rider)
    return (res, carried) if rider is not None else res


def _cast_bf16(w, name):
    R, C = w.shape
    tr = _row_tile(R, C * 4, 16)

    def body(w_ref, o_ref):
        o_ref[...] = w_ref[...].astype(BF16)

    return pl.pallas_call(
        body, name=name, out_shape=jax.ShapeDtypeStruct((R, C), BF16), grid=(R // tr,),
        in_specs=[pl.BlockSpec((tr, C), lambda r: (r, 0))], out_specs=pl.BlockSpec((tr, C), lambda r: (r, 0)),
        compiler_params=_cparams(("parallel",)),
    )(w)


def _cast_bf16_into_slot(w, jc_arr, name):
    R, C = w.shape
    tr = _row_tile(R, C * 4, 16)

    def body(jc_ref, w_ref, o_ref):
        o_ref[...] = w_ref[...].astype(BF16)

    gs = pltpu.PrefetchScalarGridSpec(
        num_scalar_prefetch=1, grid=(R // tr,),
        in_specs=[pl.BlockSpec((tr, C), lambda r, jc: (r, 0))],
        out_specs=pl.BlockSpec((None, tr, C), lambda r, jc: (jc[0], r, 0)))
    return pl.pallas_call(body, name=name, out_shape=jax.ShapeDtypeStruct((N_SHARDS, R, C), BF16), grid_spec=gs,
                          compiler_params=_cparams(("parallel",)))(jc_arr, w)


def _pair_sum(g, la, jc_arr, name):
    S, R, C = g.shape
    half = R // 2
    tr = _row_tile(half, C * 4, 16)
    nrt = half // tr
    dt = g.dtype

    def body(jc_ref, g_ref, la_ref, o_ref):
        o_ref[...] = (g_ref[...].astype(F32) + la_ref[...].astype(F32)).astype(dt)

    gs = pltpu.PrefetchScalarGridSpec(
        num_scalar_prefetch=1, grid=(S, nrt),
        in_specs=[pl.BlockSpec((None, tr, C), lambda s, r, jc: (s, jc[1] * nrt + r, 0)),
                  pl.BlockSpec((None, tr, C), lambda s, r, jc: (s, r, 0))],
        out_specs=pl.BlockSpec((None, tr, C), lambda s, r, jc: (s, r, 0)))
    return pl.pallas_call(body, name=name, out_shape=jax.ShapeDtypeStruct((S, half, C), dt), grid_spec=gs,
                          compiler_params=_cparams(("parallel", "parallel")))(jc_arr, g, la)


def _shard_sum(cp, lb, jc_arr, name, all_slots=False):
    S, h, C = cp.shape
    tr = _row_tile(h, C * 4, 16)

    def body(jc_ref, cp_ref, l0, l1, l2, o_ref):
        o_ref[...] = ((cp_ref[...].astype(F32) + l0[...].astype(F32)) + l1[...].astype(F32)) + l2[...].astype(F32)

    def lspec(kk):
        return pl.BlockSpec((None, tr, C), lambda r, jc: (kk, r, 0))

    if all_slots:
        out_spec = pl.BlockSpec((None, None, tr, C), lambda r, jc: (jc[0], jc[1], r, 0))
        out_shape = jax.ShapeDtypeStruct((S, 2, h, C), F32)
    else:
        out_spec = pl.BlockSpec((None, tr, C), lambda r, jc: (jc[1], r, 0))
        out_shape = jax.ShapeDtypeStruct((2, h, C), F32)
    gs = pltpu.PrefetchScalarGridSpec(
        num_scalar_prefetch=1, grid=(h // tr,),
        in_specs=[pl.BlockSpec((None, tr, C), lambda r, jc: (jc[0], r, 0)), lspec(0), lspec(1), lspec(2)],
        out_specs=out_spec)
    return pl.pallas_call(body, name=name, out_shape=out_shape, grid_spec=gs,
                          compiler_params=_cparams(("parallel",)))(jc_arr, cp, lb, lb, lb)


ANY = pl.BlockSpec(memory_space=pl.ANY)


def _place():
    x, y, c = lax.axis_index("x"), lax.axis_index("y"), lax.axis_index("c")
    chips = [(1 - x, y), (x, 1 - y), (1 - x, 1 - y)]
    return x, y, c, chips


class _Rider:
    def __init__(self, inputs, out_shape, aliases, sems, start, finish):
        self.inputs, self.out_shape, self.aliases, self.sems = list(inputs), list(out_shape), dict(aliases), list(sems)
        self.start, self.finish = start, finish


def _call(body, *, name, grid, in_specs, out_specs, out_shape, scratch_shapes, args, sem, rider=None):
    out_specs, out_shape = tuple(out_specs), tuple(out_shape)
    if rider is None:
        res = pl.pallas_call(body, name=name, out_shape=out_shape, grid=grid, in_specs=list(in_specs),
                             out_specs=out_specs, scratch_shapes=list(scratch_shapes),
                             compiler_params=_cparams(sem))(*args)
        return tuple(res), []
    n_in, n_out, n_sc = len(in_specs), len(out_specs), len(scratch_shapes)
    r_in, r_out = len(rider.inputs), len(rider.out_shape)

    def wrapped(*refs):
        p = 0
        host_in = refs[p:p + n_in]; p += n_in
        rid_in = refs[p:p + r_in]; p += r_in
        host_out = refs[p:p + n_out]; p += n_out
        rid_out = refs[p:p + r_out]; p += r_out
        host_sc = refs[p:p + n_sc]; p += n_sc
        rid_sem = refs[p:]
        first = functools.reduce(jnp.logical_and, [pl.program_id(a) == 0 for a in range(len(grid))])
        last = functools.reduce(jnp.logical_and, [pl.program_id(a) == grid[a] - 1 for a in range(len(grid))])

        @pl.when(first)
        def _():
            rider.start(rid_in, rid_out, rid_sem)

        body(*host_in, *host_out, *host_sc)

        @pl.when(last)
        def _():
            rider.finish(rid_in, rid_out, rid_sem)

    res = pl.pallas_call(
        wrapped, name=name, out_shape=out_shape + tuple(rider.out_shape), grid=grid,
        in_specs=list(in_specs) + [ANY] * r_in, out_specs=out_specs + (ANY,) * r_out,
        input_output_aliases={n_in + i: n_out + o for i, o in rider.aliases.items()},
        scratch_shapes=list(scratch_shapes) + rider.sems,
        compiler_params=_cparams(("arbitrary",) * len(grid)),
    )(*args, *rider.inputs)
    return tuple(res[:n_out]), list(res[n_out:])


def _run_rider(rider, name):
    def body(*refs):
        r_in, r_out = len(rider.inputs), len(rider.out_shape)
        ins, outs, sems = refs[:r_in], refs[r_in:r_in + r_out], refs[r_in + r_out:]
        rider.start(ins, outs, sems)
        rider.finish(ins, outs, sems)

    return pl.pallas_call(
        body, name=name, out_shape=rider.out_shape, in_specs=[ANY] * len(rider.inputs),
        out_specs=[ANY] * len(rider.out_shape), input_output_aliases=rider.aliases, scratch_shapes=rider.sems,
    )(*rider.inputs)


def _gather_rider(bufs):
    n = len(bufs)

    def ici_copy(out, sems, i, kk, slot, peer):
        c = lax.axis_index("c")
        half = out[i].shape[1] // 2
        blk = out[i].at[slot, pl.ds(c * half, half), :]
        return pltpu.make_async_remote_copy(
            src_ref=blk, dst_ref=blk, send_sem=sems[0].at[3 * i + kk], recv_sem=sems[1].at[3 * i + kk],
            device_id=(peer[0], peer[1], c), device_id_type=MESH)

    def d2d_copy(out, sems, i, kk, slot, from_core):
        x, y, c, _ = _place()
        half = out[i].shape[1] // 2
        blk = out[i].at[slot, pl.ds(from_core * half, half), :]
        return pltpu.make_async_remote_copy(
            src_ref=blk, dst_ref=blk, send_sem=sems[2].at[3 * i + kk], recv_sem=sems[3].at[3 * i + kk],
            device_id=(x, y, 1 - c), device_id_type=MESH)

    def start(ins, out, sems):
        x, y, c, chips = _place()
        for i in range(n):
            for kk, peer in enumerate(chips):
                ici_copy(out, sems, i, kk, 2 * x + y, peer).start()

    def finish(ins, out, sems):
        x, y, c, chips = _place()
        for i in range(n):
            for kk, peer in enumerate(chips):
                j_src = 2 * peer[0] + peer[1]
                ici_copy(out, sems, i, kk, j_src, peer).wait_recv()
                d2d_copy(out, sems, i, kk, j_src, c).start()
        for i in range(n):
            for kk, peer in enumerate(chips):
                d2d_copy(out, sems, i, kk, 2 * peer[0] + peer[1], 1 - c).wait_recv()
        for i in range(n):
            for kk, peer in enumerate(chips):
                ici_copy(out, sems, i, kk, 2 * x + y, peer).wait_send()
                d2d_copy(out, sems, i, kk, 2 * peer[0] + peer[1], c).wait_send()

    return _Rider(bufs, [jax.ShapeDtypeStruct(s.shape, s.dtype) for s in bufs], {i: i for i in range(n)},
                  [pltpu.SemaphoreType.DMA((3 * n,))] * 4, start, finish)


def _all_gather_small(shards):
    n = len(shards)

    def body(*refs):
        w = refs[:n]
        out = refs[n:2 * n]
        local_sem, s_sem, r_sem = refs[2 * n:]
        x, y, c, chips = _place()
        j_me = 2 * x + y
        cps = []
        for i in range(n):
            lc = pltpu.make_async_copy(w[i], out[i].at[j_me], local_sem.at[i])
            lc.start()
            cps.append(lc)
        sends = []
        for i in range(n):
            for kk, (px, py) in enumerate(chips):
                cp = pltpu.make_async_remote_copy(
                    src_ref=w[i], dst_ref=out[i].at[j_me], send_sem=s_sem.at[3 * i + kk],
                    recv_sem=r_sem.at[3 * i + kk], device_id=(px, py, c), device_id_type=MESH)
                cp.start()
                sends.append(cp)
        for i in range(n):
            for kk, (px, py) in enumerate(chips):
                sends[3 * i + kk].wait_send()
                pltpu.make_async_remote_copy(
                    src_ref=w[i], dst_ref=out[i].at[2 * px + py], send_sem=s_sem.at[3 * i + kk],
                    recv_sem=r_sem.at[3 * i + kk], device_id=(px, py, c), device_id_type=MESH).wait_recv()
        for lc in cps:
            lc.wait()

    out_shape = [jax.ShapeDtypeStruct((N_SHARDS,) + s.shape, s.dtype) for s in shards]
    return pl.pallas_call(
        body, name="all_gather_conv_weights", out_shape=out_shape, in_specs=[ANY] * n, out_specs=[ANY] * n,
        scratch_shapes=[pltpu.SemaphoreType.DMA((n,)), pltpu.SemaphoreType.DMA((3 * n,)),
                        pltpu.SemaphoreType.DMA((3 * n,))],
    )(*shards)


def _pair_exchange(grads, name):
    n = len(grads)

    def body(*refs):
        g = refs[:n]
        la = refs[n:2 * n]
        s_sem, r_sem = refs[2 * n:]
        x, y, c, _ = _place()
        cps = []
        for i in range(n):
            half = g[i].shape[1] // 2
            cp = pltpu.make_async_remote_copy(
                src_ref=g[i].at[:, pl.ds((1 - c) * half, half), :], dst_ref=la[i],
                send_sem=s_sem.at[i], recv_sem=r_sem.at[i], device_id=(x, y, 1 - c), device_id_type=MESH)
            cp.start()
            cps.append(cp)
        for cp in cps:
            cp.wait()

    out_shape = [jax.ShapeDtypeStruct((s.shape[0], s.shape[1] // 2, s.shape[2]), s.dtype) for s in grads]
    return pl.pallas_call(
        body, name=name, out_shape=out_shape, in_specs=[ANY] * n, out_specs=[ANY] * n,
        scratch_shapes=[pltpu.SemaphoreType.DMA((n,)), pltpu.SemaphoreType.DMA((n,))],
    )(*grads)


def _shard_exchange_rider(cps_in):
    n = len(cps_in)

    def copies(cp_ref, lb, sems):
        x, y, c, chips = _place()
        return [pltpu.make_async_remote_copy(
            src_ref=cp_ref[i].at[2 * px + py], dst_ref=lb[i].at[kk],
            send_sem=sems[0].at[3 * i + kk], recv_sem=sems[1].at[3 * i + kk],
            device_id=(px, py, c), device_id_type=MESH)
            for i in range(n) for kk, (px, py) in enumerate(chips)]

    def start(cp_ref, lb, sems):
        for cp in copies(cp_ref, lb, sems):
            cp.start()

    def finish(cp_ref, lb, sems):
        for cp in copies(cp_ref, lb, sems):
            cp.wait()

    return _Rider(cps_in, [jax.ShapeDtypeStruct((3,) + s.shape[1:], s.dtype) for s in cps_in], {},
                  [pltpu.SemaphoreType.DMA((3 * n,)), pltpu.SemaphoreType.DMA((3 * n,))], start, finish)


def _share_results(halves, eighths):
    n = len(halves)

    def body(*refs):
        out = refs[n + 1:2 * n + 1]
        eig = refs[2 * n + 1]
        s_sem, r_sem, se_sem, re_sem = refs[2 * n + 2:]
        x, y, c, _ = _place()
        j_me = 2 * x + y
        cps = []
        for i in range(n):
            cp = pltpu.make_async_remote_copy(
                src_ref=out[i].at[c], dst_ref=out[i].at[c],
                send_sem=s_sem.at[i], recv_sem=r_sem.at[i], device_id=(x, y, 1 - c), device_id_type=MESH)
            cp.start()
            cps.append(cp)
        for r in range(1, N_DEV):
            px, py, pc = x ^ ((r >> 2) & 1), y ^ ((r >> 1) & 1), c ^ (r & 1)
            cp = pltpu.make_async_remote_copy(
                src_ref=eig.at[j_me, c], dst_ref=eig.at[j_me, c], send_sem=se_sem.at[r - 1],
                recv_sem=re_sem.at[r - 1], device_id=(px, py, pc), device_id_type=MESH)
            cp.start()
            cps.append(cp)
        for i in range(n):
            theirs = out[i].at[1 - c]
            pltpu.make_async_remote_copy(
                src_ref=theirs, dst_ref=theirs,
                send_sem=s_sem.at[i], recv_sem=r_sem.at[i], device_id=(x, y, 1 - c), device_id_type=MESH).wait_recv()
        for r in range(1, N_DEV):
            px, py, pc = x ^ ((r >> 2) & 1), y ^ ((r >> 1) & 1), c ^ (r & 1)
            theirs = eig.at[2 * px + py, pc]
            pltpu.make_async_remote_copy(
                src_ref=theirs, dst_ref=theirs, send_sem=se_sem.at[r - 1],
                recv_sem=re_sem.at[r - 1], device_id=(px, py, pc), device_id_type=MESH).wait_recv()
        for cp in cps:
            cp.wait_send()

    bufs = list(halves) + [eighths]
    out_shape = [jax.ShapeDtypeStruct(s.shape, s.dtype) for s in bufs]
    return pl.pallas_call(
        body, name="grad_share_results", out_shape=out_shape, in_specs=[ANY] * (n + 1), out_specs=[ANY] * (n + 1),
        input_output_aliases={i: i for i in range(n + 1)},
        scratch_shapes=[pltpu.SemaphoreType.DMA((n,)), pltpu.SemaphoreType.DMA((n,)),
                        pltpu.SemaphoreType.DMA((N_DEV - 1,)), pltpu.SemaphoreType.DMA((N_DEV - 1,))],
    )(*bufs)


def _attn_scores(q, kp, kc, slope, sink, mask_p, mask_c, dist_p, dist_c):
    scale = HEAD_DIM ** -0.5
    dn = (((1,), (1,)), ((), ()))
    sp = lax.dot_general(q, kp, dn, preferred_element_type=F32) * scale - slope * dist_p
    sc = lax.dot_general(q, kc, dn, preferred_element_type=F32) * scale - slope * dist_c
    sp = jnp.where(mask_p, sp, NEG)
    sc = jnp.where(mask_c, sc, NEG)
    m = jnp.maximum(jnp.maximum(jnp.max(sp, axis=1, keepdims=True), jnp.max(sc, axis=1, keepdims=True)), sink)
    ep = jnp.exp(sp - m)
    ec = jnp.exp(sc - m)
    es = jnp.exp(sink - m)
    inv = 1.0 / (jnp.sum(ep, axis=1, keepdims=True) + jnp.sum(ec, axis=1, keepdims=True) + es)
    return ep * inv, ec * inv, es * inv


def _attn_masks(n):
    ti = lax.broadcasted_iota(jnp.int32, (ATTN_BLOCK, ATTN_BLOCK), 0)
    sj = lax.broadcasted_iota(jnp.int32, (ATTN_BLOCK, ATTN_BLOCK), 1)
    mask_c = sj <= ti
    mask_p = jnp.logical_and(sj > ti, n > 0)
    dist_c = (ti - sj).astype(F32)
    dist_p = dist_c + float(ATTN_BLOCK)
    return mask_p, mask_c, dist_p, dist_c


def _attn_specs(T, d_attn, d_kv, q_blk, k_blk, v_blk):
    bq = pl.BlockSpec((ATTN_BLOCK, d_attn), lambda n: (n, q_blk))
    kp = pl.BlockSpec((ATTN_BLOCK, d_kv), lambda n: (jnp.maximum(n - 1, 0), k_blk))
    kc = pl.BlockSpec((ATTN_BLOCK, d_kv), lambda n: (n, k_blk))
    vp = pl.BlockSpec((ATTN_BLOCK, d_kv), lambda n: (jnp.maximum(n - 1, 0), v_blk))
    vc = pl.BlockSpec((ATTN_BLOCK, d_kv), lambda n: (n, v_blk))
    return bq, kp, kc, vp, vc


def _attn_fwd(proj, sinks, nq, cols, rider=None):
    T = proj.shape[0]
    nkv = nq // GROUP
    d_attn, d_kv = nq * HEAD_DIM, nkv * HEAD_DIM
    q_off, k_off, v_off = cols
    bq, kp, kc, vp, vc = _attn_specs(T, d_attn, d_kv, q_off // d_attn, k_off // d_kv, v_off // d_kv)

    def body(sink_ref, q_ref, kp_ref, kc_ref, vp_ref, vc_ref, o_ref):
        n = pl.program_id(0)
        masks = _attn_masks(n)
        for g in range(nkv):
            ks = slice(g * HEAD_DIM, (g + 1) * HEAD_DIM)
            k_p, k_c = kp_ref[:, ks].astype(BF16), kc_ref[:, ks].astype(BF16)
            v_p, v_c = vp_ref[:, ks].astype(BF16), vc_ref[:, ks].astype(BF16)
            for hh in range(GROUP):
                h = g * GROUP + hh
                hs = slice(h * HEAD_DIM, (h + 1) * HEAD_DIM)
                slope = 2.0 ** (-8.0 * (h + 1) / nq)
                q = q_ref[:, hs].astype(BF16)
                pp, pc, _ = _attn_scores(q, k_p, k_c, slope, sink_ref[0, h], *masks)
                o = (jnp.dot(pp.astype(BF16), v_p, preferred_element_type=F32)
                     + jnp.dot(pc.astype(BF16), v_c, preferred_element_type=F32))
                o_ref[:, hs] = o.astype(BF16)

    (out,), carried = _call(
        body, name="attn_fwd", out_shape=[jax.ShapeDtypeStruct((T, d_attn), BF16)], grid=(T // ATTN_BLOCK,),
        in_specs=[pl.BlockSpec(memory_space=pltpu.SMEM), bq, kp, kc, vp, vc],
        out_specs=[pl.BlockSpec((ATTN_BLOCK, d_attn), lambda n: (n, 0))], scratch_shapes=[],
        args=(sinks, proj, proj, proj, proj, proj), sem=("parallel",), rider=rider)
    return out, carried


def _attn_bwd(proj, d_attn_out, sinks, nq, cols, rider=None):
    T = proj.shape[0]
    nkv = nq // GROUP
    d_attn, d_kv = nq * HEAD_DIM, nkv * HEAD_DIM
    q_off, k_off, v_off = cols
    bq, kp, kc, vp, vc = _attn_specs(T, d_attn, d_kv, q_off // d_attn, k_off // d_kv, v_off // d_kv)
    scale = HEAD_DIM ** -0.5
    dn_t = (((1,), (1,)), ((), ()))
    dn_r = (((0,), (0,)), ((), ()))

    def body(sink_ref, q_ref, kp_ref, kc_ref, vp_ref, vc_ref, do_ref, dq_ref, dk_ref, dv_ref, ds_ref):
        n = pl.program_id(0)

        @pl.when(n == 0)
        def _():
            dk_ref[...] = jnp.zeros_like(dk_ref)
            dv_ref[...] = jnp.zeros_like(dv_ref)
            ds_ref[...] = jnp.zeros_like(ds_ref)

        masks = _attn_masks(n)
        rows_c = pl.ds(pl.multiple_of(n * ATTN_BLOCK, ATTN_BLOCK), ATTN_BLOCK)
        rows_p = pl.ds(pl.multiple_of(jnp.maximum(n - 1, 0) * ATTN_BLOCK, ATTN_BLOCK), ATTN_BLOCK)
        lane = lax.broadcasted_iota(jnp.int32, ds_ref.shape, 1)
        srow = lax.broadcasted_iota(jnp.int32, ds_ref.shape, 0)
        ds_acc = jnp.zeros(ds_ref.shape, F32)
        for g in range(nkv):
            ks = slice(g * HEAD_DIM, (g + 1) * HEAD_DIM)
            k_p, k_c = kp_ref[:, ks].astype(BF16), kc_ref[:, ks].astype(BF16)
            v_p, v_c = vp_ref[:, ks].astype(BF16), vc_ref[:, ks].astype(BF16)
            dkp = jnp.zeros((ATTN_BLOCK, HEAD_DIM), F32)
            dkc = jnp.zeros((ATTN_BLOCK, HEAD_DIM), F32)
            dvp = jnp.zeros((ATTN_BLOCK, HEAD_DIM), F32)
            dvc = jnp.zeros((ATTN_BLOCK, HEAD_DIM), F32)
            for hh in range(GROUP):
                h = g * GROUP + hh
                hs = slice(h * HEAD_DIM, (h + 1) * HEAD_DIM)
                slope = 2.0 ** (-8.0 * (h + 1) / nq)
                q = q_ref[:, hs].astype(BF16)
                do = do_ref[:, hs].astype(BF16)
                pp, pc, ps = _attn_scores(q, k_p, k_c, slope, sink_ref[0, h], *masks)
                dpp = lax.dot_general(do, v_p, dn_t, preferred_element_type=F32)
                dpc = lax.dot_general(do, v_c, dn_t, preferred_element_type=F32)
                delta = jnp.sum(pp * dpp, axis=1, keepdims=True) + jnp.sum(pc * dpc, axis=1, keepdims=True)
                dsp = (pp * (dpp - delta)).astype(BF16)
                dsc = (pc * (dpc - delta)).astype(BF16)
                dsink = -jnp.sum(ps * delta)
                ds_acc = ds_acc + jnp.where(jnp.logical_and(lane == h, srow == 0), dsink, 0.0)
                dq = (jnp.dot(dsp, k_p, preferred_element_type=F32)
                      + jnp.dot(dsc, k_c, preferred_element_type=F32)) * scale
                dq_ref[:, hs] = dq.astype(BF16)
                dkp = dkp + lax.dot_general(dsp, q, dn_r, preferred_element_type=F32) * scale
                dkc = dkc + lax.dot_general(dsc, q, dn_r, preferred_element_type=F32) * scale
                dvp = dvp + lax.dot_general(pp.astype(BF16), do, dn_r, preferred_element_type=F32)
                dvc = dvc + lax.dot_general(pc.astype(BF16), do, dn_r, preferred_element_type=F32)
            dk_ref[rows_p, ks] += dkp
            dv_ref[rows_p, ks] += dvp
            dk_ref[rows_c, ks] += dkc
            dv_ref[rows_c, ks] += dvc
        ds_ref[...] += ds_acc

    out_shape = (jax.ShapeDtypeStruct((T, d_attn), BF16), jax.ShapeDtypeStruct((T, d_kv), F32),
                 jax.ShapeDtypeStruct((T, d_kv), F32), jax.ShapeDtypeStruct((8, LANES), F32))
    return _call(
        body, name="attn_bwd", out_shape=out_shape, grid=(T // ATTN_BLOCK,),
        in_specs=[pl.BlockSpec(memory_space=pltpu.SMEM), bq, kp, kc, vp, vc,
                  pl.BlockSpec((ATTN_BLOCK, d_attn), lambda n: (n, 0))],
        out_specs=(pl.BlockSpec((ATTN_BLOCK, d_attn), lambda n: (n, 0)),
                   pl.BlockSpec((T, d_kv), lambda n: (0, 0)), pl.BlockSpec((T, d_kv), lambda n: (0, 0)),
                   pl.BlockSpec((8, LANES), lambda n: (0, 0))),
        scratch_shapes=[], args=(sinks, proj, proj, proj, proj, proj, d_attn_out), sem=("arbitrary",), rider=rider)


def _rnn_tile(T):
    return _pick(T, (256, 128))


def _rnn_gates(x_ext, cw_ref, cb_ref, wa_ref, wi_ref, ba_ref, bi_ref, lam_ref, tt):
    xs = [pltpu.roll(x_ext, 3 - k, 0)[8:, :] if k < 3 else x_ext[8:, :] for k in range(4)]
    cx = cb_ref[...] + xs[0] * cw_ref[0:1, :]
    for k in range(1, 4):
        cx = cx + xs[k] * cw_ref[k:k + 1, :]
    cxb = cx.astype(BF16)
    r = jax.nn.sigmoid(jnp.dot(cxb, wa_ref[...], preferred_element_type=F32) + ba_ref[...])
    i = jax.nn.sigmoid(jnp.dot(cxb, wi_ref[...], preferred_element_type=F32) + bi_ref[...])
    lam = lam_ref[...]
    sp = jnp.maximum(-lam, 0.0) + jnp.log1p(jnp.exp(-jnp.abs(lam)))
    log_a = -LRU_C * r * sp
    a = jnp.exp(log_a)
    z = 2.0 * log_a
    em1 = jnp.where(z > -1e-2, z * (1.0 + z * (0.5 + z * (1.0 / 6.0 + z * (1.0 / 24.0)))), jnp.exp(z) - 1.0)
    s = jnp.sqrt(-em1)
    return xs, cx, r, i, sp, a, s


def _rnn_specs(T, gw, tt, rx_blk, ry_blk, rev):
    nT = T // tt
    hb = tt // 8

    def tile(t):
        return (nT - 1 - t) if rev else t

    rx = pl.BlockSpec((tt, gw), lambda g, t: (tile(t), rx_blk + g))
    rx_halo = pl.BlockSpec((8, gw), lambda g, t: (jnp.maximum(tile(t) * hb - 1, 0), rx_blk + g))
    ry = pl.BlockSpec((tt, gw), lambda g, t: (tile(t), ry_blk + g))
    cw = pl.BlockSpec((4, gw), lambda g, t: (0, g))
    vec = pl.BlockSpec((1, gw), lambda g, t: (0, g))
    wg = pl.BlockSpec((None, gw, gw), lambda g, t: (g, 0, 0))
    act = pl.BlockSpec((tt, gw), lambda g, t: (tile(t), g))
    act_halo = pl.BlockSpec((8, gw), lambda g, t: (jnp.maximum(tile(t) * hb - 1, 0), g))
    return rx, rx_halo, ry, cw, vec, wg, act, act_halo, tile


def _rnn_fwd(proj, cols, conv_w, conv_b, wa_g, wi_g, ba, bi, lam, rider=None):
    T = proj.shape[0]
    G, gw, _ = wa_g.shape
    d_rnn = G * gw
    tt = _rnn_tile(T)
    rx_off, ry_off = cols
    rx, rx_halo, ry, cw, vec, wg, act, _, _ = _rnn_specs(T, gw, tt, rx_off // gw, ry_off // gw, False)

    def body(rx_ref, rxh_ref, ry_ref, cw_ref, cb_ref, wa_ref, wi_ref, ba_ref, bi_ref, lam_ref,
             b_ref, h_ref, carry):
        t = pl.program_id(1)

        @pl.when(t == 0)
        def _():
            carry[...] = jnp.zeros_like(carry)

        halo = jnp.where(t > 0, rxh_ref[...], 0.0)
        x_ext = jnp.concatenate([halo, rx_ref[...]], axis=0)
        _, cx, _, i, _, a, s = _rnn_gates(x_ext, cw_ref, cb_ref, wa_ref, wi_ref, ba_ref, bi_ref, lam_ref, tt)
        acc_a, acc_b = a, s * (i * cx)
        d = 1
        while d < tt:
            acc_b = acc_a * _shift_down(acc_b, d, 0.0) + acc_b
            acc_a = acc_a * _shift_down(acc_a, d, 1.0)
            d *= 2
        h = acc_b + acc_a * carry[7:8, :]
        carry[...] = h[tt - 8:, :]
        h_ref[...] = h
        b_ref[...] = (h * _gelu(ry_ref[...])).astype(BF16)

    return _call(
        body, name="rnn_fwd",
        out_shape=(jax.ShapeDtypeStruct((T, d_rnn), BF16), jax.ShapeDtypeStruct((T, d_rnn), F32)),
        grid=(G, T // tt),
        in_specs=[rx, rx_halo, ry, cw, vec, wg, wg, vec, vec, vec], out_specs=(act, act),
        scratch_shapes=[pltpu.VMEM((8, gw), F32)],
        args=(proj, proj, proj, conv_w, conv_b, wa_g, wi_g, ba, bi, lam), sem=("parallel", "arbitrary"), rider=rider)


def _rnn_bwd(proj, cols, h_all, d_b, conv_w, conv_b, wa_g, wi_g, ba, bi, lam, rider=None):
    T = proj.shape[0]
    G, gw, _ = wa_g.shape
    d_rnn = G * gw
    tt = _rnn_tile(T)
    nT = T // tt
    rx_off, ry_off = cols
    rx, rx_halo, ry, cw, vec, wg, act, act_halo, _ = _rnn_specs(T, gw, tt, rx_off // gw, ry_off // gw, True)
    dn_t = (((1,), (1,)), ((), ()))
    dn_r = (((0,), (0,)), ((), ()))

    def body(rx_ref, rxh_ref, ry_ref, h_ref, hh_ref, db_ref, cw_ref, cb_ref, wa_ref, wi_ref, ba_ref, bi_ref, lam_ref,
             drx_ref, dry_ref, dcw_ref, dcb_ref, dba_ref, dbi_ref, dlam_ref, dwa_ref, dwi_ref,
             lam_carry, dcx_carry):
        t = pl.program_id(1)
        first_tile = t == nT - 1

        @pl.when(t == 0)
        def _():
            lam_carry[...] = jnp.zeros_like(lam_carry)
            dcx_carry[...] = jnp.zeros_like(dcx_carry)
            dcw_ref[...] = jnp.zeros_like(dcw_ref)
            dcb_ref[...] = jnp.zeros_like(dcb_ref)
            dba_ref[...] = jnp.zeros_like(dba_ref)
            dbi_ref[...] = jnp.zeros_like(dbi_ref)
            dlam_ref[...] = jnp.zeros_like(dlam_ref)
            dwa_ref[...] = jnp.zeros_like(dwa_ref)
            dwi_ref[...] = jnp.zeros_like(dwi_ref)

        halo = jnp.where(first_tile, 0.0, rxh_ref[...])
        x_ext = jnp.concatenate([halo, rx_ref[...]], axis=0)
        xs, cx, r, i, sp, a, s = _rnn_gates(x_ext, cw_ref, cb_ref, wa_ref, wi_ref, ba_ref, bi_ref, lam_ref, tt)
        h = h_ref[...]
        h_halo = jnp.where(first_tile, 0.0, hh_ref[...])
        h_prev = pltpu.roll(jnp.concatenate([h_halo, h], axis=0), 1, 0)[8:, :]
        gel, dgel = _gelu_and_grad(ry_ref[...])
        d_b_t = db_ref[...]
        dry_ref[...] = (d_b_t * h * dgel).astype(BF16)
        dh = d_b_t * gel

        acc_c = _shift_up(a, 1, 1.0)
        acc_l = dh
        d = 1
        while d < tt:
            acc_l = acc_c * _shift_up(acc_l, d, 0.0) + acc_l
            acc_c = acc_c * _shift_up(acc_c, d, 1.0)
            d *= 2
        lam_t = acc_l + acc_c * lam_carry[0:1, :]
        lam_carry[...] = (a * lam_t)[0:8, :]

        icx = i * cx
        d_s = lam_t * icx
        d_i = lam_t * s * cx
        dcx = lam_t * s * i
        d_a = lam_t * h_prev - d_s * (a / s)
        dlog_a = d_a * a
        d_r = dlog_a * (-LRU_C * sp)
        lam = lam_ref[...]
        dlam_ref[...] += jnp.sum(dlog_a * r, axis=0, keepdims=True) * (LRU_C * jax.nn.sigmoid(-lam))
        dpr = d_r * r * (1.0 - r)
        dpi = d_i * i * (1.0 - i)
        dba_ref[...] += jnp.sum(dpr, axis=0, keepdims=True)
        dbi_ref[...] += jnp.sum(dpi, axis=0, keepdims=True)
        cxb = cx.astype(BF16)
        dprb, dpib = dpr.astype(BF16), dpi.astype(BF16)
        dwa_ref[...] += lax.dot_general(cxb, dprb, dn_r, preferred_element_type=F32)
        dwi_ref[...] += lax.dot_general(cxb, dpib, dn_r, preferred_element_type=F32)
        dcx = (dcx + lax.dot_general(dprb, wa_ref[...], dn_t, preferred_element_type=F32)
               + lax.dot_general(dpib, wi_ref[...], dn_t, preferred_element_type=F32))

        dcb_ref[...] += jnp.sum(dcx, axis=0, keepdims=True)
        for k in range(4):
            dcw_ref[k:k + 1, :] += jnp.sum(dcx * xs[k], axis=0, keepdims=True)
        d_ext = jnp.concatenate([dcx, dcx_carry[...]], axis=0)
        drx = dcx * cw_ref[3:4, :]
        for k in range(3):
            drx = drx + pltpu.roll(d_ext, tt + 8 - (3 - k), 0)[:tt, :] * cw_ref[k:k + 1, :]
        drx_ref[...] = drx.astype(BF16)
        dcx_carry[...] = dcx[0:8, :]

    out_shape = (jax.ShapeDtypeStruct((T, d_rnn), BF16), jax.ShapeDtypeStruct((T, d_rnn), BF16),
                 jax.ShapeDtypeStruct((4, d_rnn), F32), jax.ShapeDtypeStruct((1, d_rnn), F32),
                 jax.ShapeDtypeStruct((1, d_rnn), F32), jax.ShapeDtypeStruct((1, d_rnn), F32),
                 jax.ShapeDtypeStruct((1, d_rnn), F32), jax.ShapeDtypeStruct((G, gw, gw), F32),
                 jax.ShapeDtypeStruct((G, gw, gw), F32))
    return _call(
        body, name="rnn_bwd", out_shape=out_shape, grid=(G, nT),
        in_specs=[rx, rx_halo, ry, act, act_halo, act, cw, vec, wg, wg, vec, vec, vec],
        out_specs=(act, act, cw, vec, vec, vec, vec, wg, wg),
        scratch_shapes=[pltpu.VMEM((8, gw), F32), pltpu.VMEM((8, gw), F32)],
        args=(proj, proj, proj, h_all, h_all, d_b, conv_w, conv_b, wa_g, wi_g, ba, bi, lam),
        sem=("parallel", "arbitrary"), rider=rider)


def _merge_fwd(proj, gl_off, b_gate, y_attn, y_rnn):
    T, D = y_attn.shape
    tm = _pick(T, (256, 128))
    ct = _pick(math.gcd(gl_off, D), (512, 256, 128))
    oa, orr, nd = gl_off // ct, (gl_off + D) // ct, D // ct

    def body(ga_ref, gr_ref, ba_ref, br_ref, ya_ref, yr_ref, m_ref):
        ga = jax.nn.sigmoid(ga_ref[...] + ba_ref[...])
        gr = jax.nn.sigmoid(gr_ref[...] + br_ref[...])
        m_ref[...] = (ga * ya_ref[...] + gr * yr_ref[...]).astype(BF16)

    blk = pl.BlockSpec((tm, ct), lambda i, j: (i, j))
    return pl.pallas_call(
        body, name="merge_fwd", out_shape=jax.ShapeDtypeStruct((T, D), BF16), grid=(T // tm, nd),
        in_specs=[pl.BlockSpec((tm, ct), lambda i, j: (i, oa + j)), pl.BlockSpec((tm, ct), lambda i, j: (i, orr + j)),
                  pl.BlockSpec((1, ct), lambda i, j: (0, j)), pl.BlockSpec((1, ct), lambda i, j: (0, nd + j)),
                  blk, blk],
        out_specs=blk, compiler_params=_cparams(("parallel", "parallel")),
    )(proj, proj, b_gate, b_gate, y_attn, y_rnn)


def _merge_bwd(proj, gl_off, b_gate, y_attn, y_rnn, d_m):
    T, D = y_attn.shape
    tm = _pick(T, (256, 128))
    ct = _pick(math.gcd(gl_off, D), (512, 256, 128))
    oa, orr, nd = gl_off // ct, (gl_off + D) // ct, D // ct

    def body(ga_ref, gr_ref, ba_ref, br_ref, ya_ref, yr_ref, dm_ref,
             dya_ref, dyr_ref, dga_ref, dgr_ref, dba_ref, dbr_ref):
        i = pl.program_id(1)

        @pl.when(i == 0)
        def _():
            dba_ref[...] = jnp.zeros_like(dba_ref)
            dbr_ref[...] = jnp.zeros_like(dbr_ref)

        ga = jax.nn.sigmoid(ga_ref[...] + ba_ref[...])
        gr = jax.nn.sigmoid(gr_ref[...] + br_ref[...])
        dm = dm_ref[...]
        dya_ref[...] = (dm * ga).astype(BF16)
        dyr_ref[...] = (dm * gr).astype(BF16)
        dga = dm * ya_ref[...] * ga * (1.0 - ga)
        dgr = dm * yr_ref[...] * gr * (1.0 - gr)
        dga_ref[...] = dga.astype(BF16)
        dgr_ref[...] = dgr.astype(BF16)
        dba_ref[...] += jnp.sum(dga, axis=0, keepdims=True)
        dbr_ref[...] += jnp.sum(dgr, axis=0, keepdims=True)

    blk = pl.BlockSpec((tm, ct), lambda j, i: (i, j))
    vec = pl.BlockSpec((1, ct), lambda j, i: (0, j))
    act = jax.ShapeDtypeStruct((T, D), BF16)
    v1 = jax.ShapeDtypeStruct((1, D), F32)
    return pl.pallas_call(
        body, name="merge_bwd", out_shape=(act, act, act, act, v1, v1), grid=(nd, T // tm),
        in_specs=[pl.BlockSpec((tm, ct), lambda j, i: (i, oa + j)), pl.BlockSpec((tm, ct), lambda j, i: (i, orr + j)),
                  vec, pl.BlockSpec((1, ct), lambda j, i: (0, nd + j)), blk, blk, blk],
        out_specs=(blk, blk, blk, blk, vec, vec),
        compiler_params=_cparams(("parallel", "arbitrary")),
    )(proj, proj, b_gate, b_gate, y_attn, y_rnn, d_m)


def _ln_fwd(x_res, delta, g, b, name):
    T, D = x_res.shape
    tm = _pick(T, (256, 128))

    def body(x_ref, d_ref, g_ref, b_ref, y_ref, yb_ref, xh_ref, rs_ref):
        z = ALPHA * x_ref[...] + d_ref[...]
        mu = jnp.mean(z, axis=1, keepdims=True)
        zc = z - mu
        var = jnp.mean(zc * zc, axis=1, keepdims=True)
        rstd = lax.rsqrt(var + LN_EPS)
        xh = zc * rstd
        xh_ref[...] = xh
        rs_ref[...] = rstd
        y = xh * g_ref[...] + b_ref[...]
        y_ref[...] = y
        yb_ref[...] = y.astype(BF16)

    row = pl.BlockSpec((tm, D), lambda i: (i, 0))
    vec = pl.BlockSpec((1, D), lambda i: (0, 0))
    return pl.pallas_call(
        body, name=name,
        out_shape=(jax.ShapeDtypeStruct((T, D), F32), jax.ShapeDtypeStruct((T, D), BF16),
                   jax.ShapeDtypeStruct((T, D), F32), jax.ShapeDtypeStruct((T, 1), F32)),
        grid=(T // tm,), in_specs=[row, row, vec, vec],
        out_specs=(row, row, row, pl.BlockSpec((tm, 1), lambda i: (i, 0))),
        compiler_params=_cparams(("parallel",)),
    )(x_res, delta, g, b)


def _ln_bwd_rows(dy, xh, rstd, g):
    dxh = dy * g
    m1 = jnp.mean(dxh, axis=1, keepdims=True)
    m2 = jnp.mean(dxh * xh, axis=1, keepdims=True)
    return rstd * (dxh - m1 - xh * m2)


def _ln_loss_bwd(x_res, delta, g, b, target):
    T, D = x_res.shape
    tm = _pick(T, (256, 128))

    def body(x_ref, d_ref, g_ref, b_ref, t_ref, dz_ref, dzb_ref, loss_ref, dg_ref, db_ref):
        i = pl.program_id(0)

        @pl.when(i == 0)
        def _():
            loss_ref[...] = jnp.zeros_like(loss_ref)
            dg_ref[...] = jnp.zeros_like(dg_ref)
            db_ref[...] = jnp.zeros_like(db_ref)

        z = ALPHA * x_ref[...] + d_ref[...]
        mu = jnp.mean(z, axis=1, keepdims=True)
        zc = z - mu
        var = jnp.mean(zc * zc, axis=1, keepdims=True)
        rstd = lax.rsqrt(var + LN_EPS)
        xh = zc * rstd
        gv = g_ref[...]
        err = xh * gv + b_ref[...] - t_ref[...]
        loss_ref[...] += 0.5 * jnp.sum(jnp.mean(err * err, axis=1, keepdims=True))
        dy = err * (1.0 / D)
        dg_ref[...] += jnp.sum(dy * xh, axis=0, keepdims=True)
        db_ref[...] += jnp.sum(dy, axis=0, keepdims=True)
        dz = _ln_bwd_rows(dy, xh, rstd, gv)
        dz_ref[...] = dz
        dzb_ref[...] = dz.astype(BF16)

    row = pl.BlockSpec((tm, D), lambda i: (i, 0))
    vec = pl.BlockSpec((1, D), lambda i: (0, 0))
    return pl.pallas_call(
        body, name="ln2_loss_bwd",
        out_shape=(jax.ShapeDtypeStruct((T, D), F32), jax.ShapeDtypeStruct((T, D), BF16),
                   jax.ShapeDtypeStruct((8, LANES), F32),
                   jax.ShapeDtypeStruct((1, D), F32), jax.ShapeDtypeStruct((1, D), F32)),
        grid=(T // tm,), in_specs=[row, row, vec, vec, row],
        out_specs=(row, row, pl.BlockSpec((8, LANES), lambda i: (0, 0)), vec, vec),
        compiler_params=_cparams(("arbitrary",)),
    )(x_res, delta, g, b, target)


def _ln_bwd(dy, xh, rstd, g):
    T, D = dy.shape
    tm = _pick(T, (256, 128))

    def body(dy_ref, xh_ref, rs_ref, g_ref, dz_ref, dzb_ref, dg_ref, db_ref):
        i = pl.program_id(0)

        @pl.when(i == 0)
        def _():
            dg_ref[...] = jnp.zeros_like(dg_ref)
            db_ref[...] = jnp.zeros_like(db_ref)

        dyv, xhv = dy_ref[...], xh_ref[...]
        dg_ref[...] += jnp.sum(dyv * xhv, axis=0, keepdims=True)
        db_ref[...] += jnp.sum(dyv, axis=0, keepdims=True)
        dz = _ln_bwd_rows(dyv, xhv, rs_ref[...], g_ref[...])
        dz_ref[...] = dz
        dzb_ref[...] = dz.astype(BF16)

    row = pl.BlockSpec((tm, D), lambda i: (i, 0))
    vec = pl.BlockSpec((1, D), lambda i: (0, 0))
    return pl.pallas_call(
        body, name="ln1_bwd",
        out_shape=(jax.ShapeDtypeStruct((T, D), F32), jax.ShapeDtypeStruct((T, D), BF16),
                   jax.ShapeDtypeStruct((1, D), F32), jax.ShapeDtypeStruct((1, D), F32)),
        grid=(T // tm,), in_specs=[row, row, pl.BlockSpec((tm, 1), lambda i: (i, 0)), vec],
        out_specs=(row, row, vec, vec), compiler_params=_cparams(("arbitrary",)),
    )(dy, xh, rstd, g)


def _ffn_col_tile(T, d_ff):
    return _pick(d_ff, (256, 128)) if T >= 1024 else _pick(d_ff, (512, 256, 128))


def _ffn_gate(gp, cw_ref, cb_ref):
    return (cb_ref[...] + gp * cw_ref[2:3, :] + _shift_down(gp, 1) * cw_ref[1:2, :]
            + _shift_down(gp, 2) * cw_ref[0:1, :])


def _ffn_fwd(up, gpre, conv_w, conv_b):
    T, d_ff = up.shape
    ct = _ffn_col_tile(T, d_ff)

    def body(up_ref, gp_ref, cw_ref, cb_ref, f_ref):
        gate = _ffn_gate(gp_ref[...], cw_ref, cb_ref)
        f_ref[...] = (_gelu(gate) * up_ref[...]).astype(BF16)

    col = pl.BlockSpec((T, ct), lambda j: (0, j))
    return pl.pallas_call(
        body, name="ffn_act_fwd", out_shape=jax.ShapeDtypeStruct((T, d_ff), BF16), grid=(d_ff // ct,),
        in_specs=[col, col, pl.BlockSpec((3, ct), lambda j: (0, j)), pl.BlockSpec((1, ct), lambda j: (0, j))],
        out_specs=col, compiler_params=_cparams(("parallel",)),
    )(up, gpre, conv_w, conv_b)


def _ffn_bwd(up, gpre, conv_w, conv_b, d_f, rider=None):
    T, d_ff = up.shape
    ct = _ffn_col_tile(T, d_ff)

    def body(up_ref, gp_ref, cw_ref, cb_ref, df_ref, dup_ref, dgp_ref, dcw_ref, dcb_ref):
        gp = gp_ref[...]
        gate = _ffn_gate(gp, cw_ref, cb_ref)
        gel, dgel = _gelu_and_grad(gate)
        df = df_ref[...]
        dup_ref[...] = (df * gel).astype(BF16)
        dgate = df * up_ref[...] * dgel
        dcb_ref[...] = jnp.sum(dgate, axis=0, keepdims=True)
        dcw_ref[2:3, :] = jnp.sum(dgate * gp, axis=0, keepdims=True)
        dcw_ref[1:2, :] = jnp.sum(dgate * _shift_down(gp, 1), axis=0, keepdims=True)
        dcw_ref[0:1, :] = jnp.sum(dgate * _shift_down(gp, 2), axis=0, keepdims=True)
        dgp = (dgate * cw_ref[2:3, :] + _shift_up(dgate, 1) * cw_ref[1:2, :]
               + _shift_up(dgate, 2) * cw_ref[0:1, :])
        dgp_ref[...] = dgp.astype(BF16)

    col = pl.BlockSpec((T, ct), lambda j: (0, j))
    w3 = pl.BlockSpec((3, ct), lambda j: (0, j))
    v1 = pl.BlockSpec((1, ct), lambda j: (0, j))
    return _call(
        body, name="ffn_act_bwd",
        out_shape=(jax.ShapeDtypeStruct((T, d_ff), BF16), jax.ShapeDtypeStruct((T, d_ff), BF16),
                   jax.ShapeDtypeStruct((3, d_ff), F32), jax.ShapeDtypeStruct((1, d_ff), F32)),
        grid=(d_ff // ct,), in_specs=[col, col, w3, v1, col], out_specs=(col, col, w3, v1),
        scratch_shapes=[], args=(up, gpre, conv_w, conv_b, d_f), sem=("parallel",), rider=rider)


def _adamw(w, g, m, v, name):
    R, C = w.shape
    tr = _row_tile(R, C * 4, 8, budget=1536 * 1024)
    c1 = 1.0 / (1.0 - ADAM_B1 ** ADAM_STEP)
    c2 = 1.0 / (1.0 - ADAM_B2 ** ADAM_STEP)

    def body(w_ref, g_ref, m_ref, v_ref, d_ref, nm_ref, nv_ref):
        gv = g_ref[...]
        nm = ADAM_B1 * m_ref[...] + (1.0 - ADAM_B1) * gv
        nv = ADAM_B2 * v_ref[...] + (1.0 - ADAM_B2) * (gv * gv)
        nm_ref[...] = nm
        nv_ref[...] = nv
        d_ref[...] = -ADAM_LR * ((nm * c1) / (jnp.sqrt(nv * c2) + ADAM_EPS) + ADAM_WD * w_ref[...])

    blk = pl.BlockSpec((tr, C), lambda r: (r, 0))
    sh = jax.ShapeDtypeStruct((R, C), F32)
    return pl.pallas_call(
        body, name=name, out_shape=(sh, sh, sh), grid=(R // tr,), in_specs=[blk] * 4, out_specs=(blk,) * 3,
        compiler_params=_cparams(("parallel",)),
    )(w, g, m, v)


def _group_blocks(w_blocks, per):
    nb, bw, _ = w_blocks.shape
    G = nb // per
    w4 = w_blocks.reshape(G, per, bw, bw)
    rows = []
    for p in range(per):
        parts = [w4[:, p] if q == p else jnp.zeros((G, bw, bw), w_blocks.dtype) for q in range(per)]
        rows.append(jnp.concatenate(parts, axis=2))
    return jnp.concatenate(rows, axis=1)


def _ungroup_blocks(w_groups, per):
    G, gw, _ = w_groups.shape
    bw = gw // per
    blocks = [w_groups[:, p * bw:(p + 1) * bw, p * bw:(p + 1) * bw] for p in range(per)]
    return jnp.stack(blocks, axis=1).reshape(G * per, bw, bw)


def _pack(parts):
    flat = jnp.concatenate([p.reshape(-1).astype(F32) for p in parts])
    n = flat.shape[0]
    rows = -(-n // LANES)
    rows = -(-rows // PACK_ROW_MULT) * PACK_ROW_MULT
    flat = jnp.pad(flat, (0, rows * LANES - n))
    return flat.reshape(rows, LANES)


def _unpack(packed, shapes):
    flat = packed.reshape(-1)
    out, off = [], 0
    for s in shapes:
        n = math.prod(s)
        out.append(flat[off:off + n].reshape(s))
        off += n
    return out


def kernel(x, w_in, b_gate, rnn_conv_w, rnn_conv_b, lru_wa, lru_ba, lru_wi, lru_bi, lru_lambda, attn_sinks, w_attn_proj, w_rnn_proj, w_out, ln1_g, ln1_b, ffn_w_up, ffn_w_gate, ffn_conv_w, ffn_conv_b, ffn_w_down, ln2_g, ln2_b, loss_target, m_w_in, m_b_gate, m_rnn_conv_w, m_rnn_conv_b, m_lru_wa, m_lru_ba, m_lru_wi, m_lru_bi, m_lru_lambda, m_attn_sinks, m_w_attn_proj, m_w_rnn_proj, m_w_out, m_ln1_g, m_ln1_b, m_ffn_w_up, m_ffn_w_gate, m_ffn_conv_w, m_ffn_conv_b, m_ffn_w_down, m_ln2_g, m_ln2_b, v_w_in, v_b_gate, v_rnn_conv_w, v_rnn_conv_b, v_lru_wa, v_lru_ba, v_lru_wi, v_lru_bi, v_lru_lambda, v_attn_sinks, v_w_attn_proj, v_w_rnn_proj, v_w_out, v_ln1_g, v_ln1_b, v_ffn_w_up, v_ffn_w_gate, v_ffn_conv_w, v_ffn_conv_b, v_ffn_w_down, v_ln2_g, v_ln2_b):
    weights = dict(w_in=w_in, b_gate=b_gate, rnn_conv_w=rnn_conv_w, rnn_conv_b=rnn_conv_b, lru_wa=lru_wa,
                   lru_ba=lru_ba, lru_wi=lru_wi, lru_bi=lru_bi, lru_lambda=lru_lambda, attn_sinks=attn_sinks,
                   w_attn_proj=w_attn_proj, w_rnn_proj=w_rnn_proj, w_out=w_out, ln1_g=ln1_g, ln1_b=ln1_b,
                   ffn_w_up=ffn_w_up, ffn_w_gate=ffn_w_gate, ffn_conv_w=ffn_conv_w, ffn_conv_b=ffn_conv_b,
                   ffn_w_down=ffn_w_down, ln2_g=ln2_g, ln2_b=ln2_b)
    m_in = dict(w_in=m_w_in, b_gate=m_b_gate, rnn_conv_w=m_rnn_conv_w, rnn_conv_b=m_rnn_conv_b, lru_wa=m_lru_wa,
                lru_ba=m_lru_ba, lru_wi=m_lru_wi, lru_bi=m_lru_bi, lru_lambda=m_lru_lambda, attn_sinks=m_attn_sinks,
                w_attn_proj=m_w_attn_proj, w_rnn_proj=m_w_rnn_proj, w_out=m_w_out, ln1_g=m_ln1_g, ln1_b=m_ln1_b,
                ffn_w_up=m_ffn_w_up, ffn_w_gate=m_ffn_w_gate, ffn_conv_w=m_ffn_conv_w, ffn_conv_b=m_ffn_conv_b,
                ffn_w_down=m_ffn_w_down, ln2_g=m_ln2_g, ln2_b=m_ln2_b)
    v_in = dict(w_in=v_w_in, b_gate=v_b_gate, rnn_conv_w=v_rnn_conv_w, rnn_conv_b=v_rnn_conv_b, lru_wa=v_lru_wa,
                lru_ba=v_lru_ba, lru_wi=v_lru_wi, lru_bi=v_lru_bi, lru_lambda=v_lru_lambda, attn_sinks=v_attn_sinks,
                w_attn_proj=v_w_attn_proj, w_rnn_proj=v_w_rnn_proj, w_out=v_w_out, ln1_g=v_ln1_g, ln1_b=v_ln1_b,
                ffn_w_up=v_ffn_w_up, ffn_w_gate=v_ffn_w_gate, ffn_conv_w=v_ffn_conv_w, ffn_conv_b=v_ffn_conv_b,
                ffn_w_down=v_ffn_w_down, ln2_g=v_ln2_g, ln2_b=v_ln2_b)
    order = list(weights)

    assert x.shape[0] == 1 and w_in.shape[0] == 1, "one sequence per device, depth 1"
    T, D = x.shape[1], x.shape[2]
    nq = attn_sinks.shape[-1]
    nkv = nq // GROUP
    d_attn, d_kv = nq * HEAD_DIM, nkv * HEAD_DIM
    d_rnn = rnn_conv_b.shape[-1]
    d_ff = ffn_conv_b.shape[-1]
    n_blocks, bw = lru_wa.shape[1], lru_wa.shape[2]
    per = (bw * LANES // math.gcd(bw, LANES)) // bw
    gw = per * bw
    assert n_blocks % per == 0 and d_rnn == n_blocks * bw
    q_off, k_off, v_off = 0, d_attn, d_attn + d_kv
    rx_off = d_attn + 2 * d_kv
    ry_off = rx_off + d_rnn
    gl_off = ry_off + d_rnn
    d_in = gl_off + 2 * D
    assert w_in.shape[-1] * N_SHARDS == d_in
    assert k_off % d_kv == 0 and rx_off % gw == 0 and T % ATTN_BLOCK == 0

    xi, yi, ci = lax.axis_index("x"), lax.axis_index("y"), lax.axis_index("c")
    j_me = 2 * xi + yi
    jc_arr = jnp.stack([j_me, ci]).astype(jnp.int32)

    x0 = x[0]
    x0b = _cast_bf16(x0, "cast_x")
    tgt = loss_target[0]
    big = ["w_in", "w_attn_proj", "w_rnn_proj", "w_out", "ffn_w_up", "ffn_w_gate", "ffn_w_down"]
    own = {n: _cast_bf16_into_slot(weights[n][0], jc_arr, "cast_" + n) for n in big}
    (w_in_s,) = _run_rider(_gather_rider([own["w_in"]]), "all_gather_w_in")

    rcw_s, fcw_s = _all_gather_small([rnn_conv_w[0], ffn_conv_w[0]])
    rcw = jnp.concatenate([rcw_s[j] for j in range(N_SHARDS)], axis=1)
    fcw = jnp.concatenate([fcw_s[j] for j in range(N_SHARDS)], axis=1)

    wa_g = _group_blocks(lru_wa[0], per).astype(BF16)
    wi_g = _group_blocks(lru_wi[0], per).astype(BF16)

    proj, (w_ap_s, w_rp_s, w_o_s) = _mm(
        x0b, w_in_s, name="mm_proj", b_shards=N_SHARDS,
        rider=_gather_rider([own["w_attn_proj"], own["w_rnn_proj"], own["w_out"]]))
    w_ap, w_rp, w_o = w_ap_s.reshape(d_attn, D), w_rp_s.reshape(d_rnn, D), w_o_s.reshape(D, D)
    a_out, (w_up_s,) = _attn_fwd(proj, attn_sinks, nq, (q_off, k_off, v_off),
                                 rider=_gather_rider([own["ffn_w_up"]]))
    (b_out, h_all), (w_gate_s,) = _rnn_fwd(proj, (rx_off, ry_off), rcw, rnn_conv_b, wa_g, wi_g, lru_ba, lru_bi,
                                           lru_lambda, rider=_gather_rider([own["ffn_w_gate"]]))
    y_attn = _mm(a_out, w_ap, name="mm_attn_proj")
    y_rnn = _mm(b_out, w_rp, name="mm_rnn_proj")
    merged = _merge_fwd(proj, gl_off, b_gate, y_attn, y_rnn)
    mix = _mm(merged, w_o, name="mm_out")
    x1, x1b, xh1, rstd1 = _ln_fwd(x0, mix, ln1_g, ln1_b, "ln1_fwd")
    up, (w_dn_s,) = _mm(x1b, w_up_s, name="mm_up", b_shards=N_SHARDS, rider=_gather_rider([own["ffn_w_down"]]))
    w_dn = w_dn_s.reshape(d_ff, D)
    gpre = _mm(x1b, w_gate_s, name="mm_gate", b_shards=N_SHARDS)
    f_act = _ffn_fwd(up, gpre, fcw, ffn_conv_b)
    f_out = _mm(f_act, w_dn, name="mm_down")
    dz2, dz2b, loss_acc, dg2, db2 = _ln_loss_bwd(x1, f_out, ln2_g, ln2_b, tgt)

    def pair_stage(arrs, names, tag):
        from_sibling = _pair_exchange(arrs, "pair_exchange_" + tag)
        return [_pair_sum(g, la, jc_arr, "pair_sum_" + n) for g, la, n in zip(arrs, from_sibling, names)]

    def shard_sums(parts, landed, names):
        return [_shard_sum(cp, lb, jc_arr, "shard_sum_" + n) for cp, lb, n in zip(parts, landed, names)]

    halves = {}
    g_down = _mm(f_act, dz2b, name="mm_d_w_down", ta=True, out_dtype=BF16)
    part1 = pair_stage([g_down.reshape(N_SHARDS, d_ff // N_SHARDS, D)], ["ffn_w_down"], "down")
    d_f = _mm(dz2b, w_dn, name="mm_d_f", tb=True)
    (dup, dgp, d_fcw, d_fcb), landed1 = _ffn_bwd(up, gpre, fcw, ffn_conv_b, d_f, rider=_shard_exchange_rider(part1))
    halves["ffn_w_down"], = shard_sums(part1, landed1, ["ffn_w_down"])
    g_up = _mm(x1b, dup, name="mm_d_w_up", ta=True, out_dtype=BF16, out_shards=N_SHARDS)
    g_gate = _mm(x1b, dgp, name="mm_d_w_gate", ta=True, out_dtype=BF16, out_shards=N_SHARDS)
    part2 = pair_stage([g_up, g_gate], ["ffn_w_up", "ffn_w_gate"], "up_gate")
    dx1_a = _mm(dup, w_up_s, name="mm_dx1_up", tb=True, b_shards=N_SHARDS, adds=((ALPHA, dz2),))
    dx1 = _mm(dgp, w_gate_s, name="mm_dx1_gate", tb=True, b_shards=N_SHARDS, adds=((1.0, dx1_a),))
    dz1, dz1b, dg1, db1 = _ln_bwd(dx1, xh1, rstd1, ln1_g)
    g_out = _mm(merged, dz1b, name="mm_d_w_out", ta=True, out_dtype=BF16)
    d_m = _mm(dz1b, w_o, name="mm_d_merged", tb=True)
    dya, dyr, dgl_a, dgl_r, dbg_a, dbg_r = _merge_bwd(proj, gl_off, b_gate, y_attn, y_rnn, d_m)
    g_ap = _mm(a_out, dya, name="mm_d_w_attn_proj", ta=True, out_dtype=BF16)
    g_rp = _mm(b_out, dyr, name="mm_d_w_rnn_proj", ta=True, out_dtype=BF16)
    names3 = ["w_out", "w_attn_proj", "w_rnn_proj"]
    part3 = pair_stage([g_out.reshape(N_SHARDS, D // N_SHARDS, D), g_ap.reshape(N_SHARDS, d_attn // N_SHARDS, D),
                        g_rp.reshape(N_SHARDS, d_rnn // N_SHARDS, D)], names3, "mixers")
    d_a = _mm(dya, w_ap, name="mm_d_attn", tb=True)
    d_b = _mm(dyr, w_rp, name="mm_d_rnn", tb=True)
    (dq, dk, dv, dsink), landed2 = _attn_bwd(proj, d_a, attn_sinks, nq, (q_off, k_off, v_off),
                                             rider=_shard_exchange_rider(part2))
    halves["ffn_w_up"], halves["ffn_w_gate"] = shard_sums(part2, landed2, ["ffn_w_up", "ffn_w_gate"])
    (drx, dry, d_rcw, d_rcb, d_ba, d_bi, d_lam, d_wa_g, d_wi_g), landed3 = _rnn_bwd(
        proj, (rx_off, ry_off), h_all, d_b, rcw, rnn_conv_b, wa_g, wi_g, lru_ba, lru_bi, lru_lambda,
        rider=_shard_exchange_rider(part3))
    halves["w_out"], halves["w_attn_proj"], halves["w_rnn_proj"] = shard_sums(part3, landed3, names3)
    d_proj = jnp.concatenate([dq, dk.astype(BF16), dv.astype(BF16), drx, dry, dgl_a, dgl_r], axis=1)
    g_in = _mm(x0b, d_proj, name="mm_d_w_in", ta=True, out_dtype=BF16, out_shards=N_SHARDS)

    small_parts = [
        ("loss", loss_acc[0:1, 0:1]),
        ("b_gate", jnp.concatenate([dbg_a, dbg_r], axis=1)),
        ("rnn_conv_w", d_rcw), ("rnn_conv_b", d_rcb),
        ("lru_wa", _ungroup_blocks(d_wa_g, per)), ("lru_ba", d_ba),
        ("lru_wi", _ungroup_blocks(d_wi_g, per)), ("lru_bi", d_bi), ("lru_lambda", d_lam),
        ("attn_sinks", dsink[0:1, 0:nq]),
        ("ln1_g", dg1), ("ln1_b", db1),
        ("ffn_conv_w", d_fcw), ("ffn_conv_b", d_fcb),
        ("ln2_g", dg2), ("ln2_b", db2),
    ]
    packed = _pack([p for _, p in small_parts])
    rs = packed.shape[0]

    part4 = pair_stage([g_in, packed.reshape(N_SHARDS, rs // N_SHARDS, LANES)], ["w_in", "small"], "in_small")
    grad_x, landed4 = _mm(d_proj, w_in_s, name="mm_d_x", tb=True, b_shards=N_SHARDS, adds=((ALPHA, dz1),),
                          rider=_shard_exchange_rider(part4))
    halves["w_in"], = shard_sums(part4[:1], landed4[:1], ["w_in"])
    eighths = _shard_sum(part4[1], landed4[1], jc_arr, "shard_sum_small", all_slots=True)
    shared = _share_results([halves[n] for n in big], eighths)
    grads = {n: g.reshape(2 * g.shape[1], g.shape[2]) for n, g in zip(big, shared[:-1])}
    reduced = shared[-1].reshape(rs, LANES)
    small = dict(zip([n for n, _ in small_parts], _unpack(reduced, [p.shape for _, p in small_parts])))
    loss = small.pop("loss").reshape(())
    rcw_n = d_rnn // N_SHARDS
    fcw_n = d_ff // N_SHARDS
    small["rnn_conv_w"] = lax.dynamic_slice(small["rnn_conv_w"], (0, j_me * rcw_n), (4, rcw_n))
    small["ffn_conv_w"] = lax.dynamic_slice(small["ffn_conv_w"], (0, j_me * fcw_n), (3, fcw_n))
    for n, g in small.items():
        grads[n] = g

    out_g, out_d, out_m, out_v = {}, {}, {}, {}
    for n in order:
        w_full = weights[n]
        shape = w_full.shape
        two_d = (math.prod(shape[:-1]), shape[-1])
        g2 = grads[n].reshape(two_d)
        d2, m2, v2 = _adamw(w_full.reshape(two_d), g2, m_in[n].reshape(two_d), v_in[n].reshape(two_d), "adamw_" + n)
        out_g[n] = g2.reshape(shape)
        out_d[n], out_m[n], out_v[n] = d2.reshape(shape), m2.reshape(shape), v2.reshape(shape)

    return (loss, grad_x.reshape(x.shape), *[out_g[n] for n in order], *[out_d[n] for n in order],
            *[out_m[n] for n in order], *[out_v[n] for n in order])
```

```python
import functools
import math

import jax
import jax.numpy as jnp
from jax import lax
from jax.experimental import pallas as pl
from jax.experimental.pallas import tpu as pltpu

F32 = jnp.float32
BF16 = jnp.bfloat16
MESH = pl.DeviceIdType.MESH

HEAD_DIM = 64
GROUP = 8
ATTN_BLOCK = 128
LRU_C = 8.0
LN_EPS = 1e-5
ALPHA = 2.0 ** 0.25
LANES = 128
N_SHARDS = 4
N_DEV = 8
VMEM_LIMIT = 56 * 1024 * 1024
MM_VMEM_BUDGET = 44 * 1024 * 1024
MM_MAX_TILE = 3072
PACK_ROW_MULT = 8 * 64
NEG = -1e30

ADAM_LR, ADAM_B1, ADAM_B2, ADAM_EPS, ADAM_WD, ADAM_STEP = 0.001, 0.9, 0.999, 1e-08, 0.01, 10

GELU_C = math.sqrt(2.0 / math.pi)
GELU_A = 0.044715


def _cparams(sem=None):
    kw = dict(vmem_limit_bytes=VMEM_LIMIT)
    if sem is not None:
        kw["dimension_semantics"] = sem
    return pltpu.CompilerParams(**kw)


def _pick(n, prefs):
    for p in prefs:
        if n % p == 0:
            return p
    return n


def _row_tile(rows, row_bytes, mult, budget=2 * 1024 * 1024):
    best = None
    for d in range(mult, rows + 1, mult):
        if rows % d == 0 and d * row_bytes <= budget:
            best = d
    return best if best is not None else rows


def _gelu(x):
    return 0.5 * x * (1.0 + jnp.tanh(GELU_C * (x + GELU_A * x * x * x)))


def _gelu_and_grad(x):
    t = jnp.tanh(GELU_C * (x + GELU_A * x * x * x))
    g = 0.5 * x * (1.0 + t)
    dg = 0.5 * (1.0 + t) + 0.5 * x * (1.0 - t * t) * GELU_C * (1.0 + 3.0 * GELU_A * x * x)
    return g, dg


def _shift_down(x, s, fill=0.0):
    row = lax.broadcasted_iota(jnp.int32, x.shape, 0)
    return jnp.where(row >= s, pltpu.roll(x, s, 0), fill)


def _shift_up(x, s, fill=0.0):
    n = x.shape[0]
    row = lax.broadcasted_iota(jnp.int32, x.shape, 0)
    return jnp.where(row < n - s, pltpu.roll(x, n - s, 0), fill)


def _mm(a, b, *, name, ta=False, tb=False, out_dtype=F32, adds=(), b_shards=1, out_shards=1,
        tm=None, tn=None, tk=None, rider=None):
    if ta:
        K, M = a.shape
    else:
        M, K = a.shape
    if b_shards > 1:
        n_sh = b.shape[-1]
        if tb:
            N = b.shape[1]
            assert b_shards * n_sh == K
        else:
            N = b_shards * n_sh
            assert b.shape[1] == K
    else:
        n_sh = None
        if tb:
            N = b.shape[0]
            assert b.shape[1] == K
        else:
            N = b.shape[1]
            assert b.shape[0] == K
    wide = (1024, 1536, 1280, 768, 640, 512, 256, 128)
    if tn is None:
        if b_shards > 1 and not tb:
            tn = n_sh if n_sh <= MM_MAX_TILE else _pick(n_sh, wide)
        elif out_shards > 1:
            tn = N // out_shards if N // out_shards <= MM_MAX_TILE else _pick(N // out_shards, wide)
        else:
            tn = _pick(N, wide)
    if tk is None:
        if b_shards > 1 and tb:
            tk = n_sh if n_sh <= MM_MAX_TILE else _pick(n_sh, wide)
        else:
            tk = K if K <= MM_MAX_TILE else _pick(K, (2048,) + wide)
    assert N % tn == 0 and K % tk == 0, (name, M, N, K, tn, tk)
    nk = K // tk
    n_add = len(adds)
    sa, sb, so = a.dtype.itemsize, b.dtype.itemsize, jnp.dtype(out_dtype).itemsize

    def vmem_bytes(tm_):
        return (2 * (tm_ * tk * sa + tk * tn * sb + tm_ * tn * so + n_add * tm_ * tn * 4)
                + (tm_ * tn * 4 if nk > 1 else 0))

    if tm is None:
        tm = _pick(M, (512, 256, 128))
        while vmem_bytes(tm) > MM_VMEM_BUDGET and tm % 256 == 0:
            tm //= 2
    assert M % tm == 0, (name, M, tm)
    b_outer = b.size * sb >= a.size * sa

    def ij(g0, g1):
        return (g1, g0) if b_outer else (g0, g1)

    def amap(g0, g1, k):
        i, _ = ij(g0, g1)
        return (k, i) if ta else (i, k)

    def bmap(g0, g1, k):
        _, j = ij(g0, g1)
        if b_shards > 1 and not tb:
            per = n_sh // tn
            return (j // per, k, j % per)
        if b_shards > 1 and tb:
            per = n_sh // tk
            return (k // per, j, k % per)
        return (j, k) if tb else (k, j)

    def omap(g0, g1, k):
        i, j = ij(g0, g1)
        if out_shards > 1:
            per_o = (N // out_shards) // tn
            return (j // per_o, i, j % per_o)
        return (i, j)

    a_spec = pl.BlockSpec((tk, tm) if ta else (tm, tk), amap)
    if b_shards > 1:
        b_spec = pl.BlockSpec((None, tn, tk) if tb else (None, tk, tn), bmap)
    else:
        b_spec = pl.BlockSpec((tn, tk) if tb else (tk, tn), bmap)
    add_specs = [pl.BlockSpec((tm, tn), lambda g0, g1, k: ij(g0, g1)) for _ in adds]
    if out_shards > 1:
        out_spec = pl.BlockSpec((None, tm, tn), omap)
        out_shape = jax.ShapeDtypeStruct((out_shards, M, N // out_shards), out_dtype)
    else:
        out_spec = pl.BlockSpec((tm, tn), omap)
        out_shape = jax.ShapeDtypeStruct((M, N), out_dtype)

    if ta:
        dims = (((0,), (0,)), ((), ()))
    elif tb:
        dims = (((1,), (1,)), ((), ()))
    else:
        dims = (((1,), (0,)), ((), ()))
    scales = tuple(s for s, _ in adds)

    def finish(r, add_refs, o_ref):
        for s, ref in zip(scales, add_refs):
            r = r + s * ref[...].astype(F32)
        o_ref[...] = r.astype(out_dtype)

    def body(a_ref, b_ref, *rest):
        add_refs = rest[:n_add]
        o_ref = rest[n_add]
        part = lax.dot_general(a_ref[...].astype(BF16), b_ref[...].astype(BF16), dims, preferred_element_type=F32)
        if nk == 1:
            finish(part, add_refs, o_ref)
            return
        acc = rest[n_add + 1]
        k = pl.program_id(2)

        @pl.when(k == 0)
        def _():
            acc[...] = part

        @pl.when(k > 0)
        def _():
            acc[...] += part

        @pl.when(k == nk - 1)
        def _():
            finish(acc[...], add_refs, o_ref)

    grid = (N // tn, M // tm, nk) if b_outer else (M // tm, N // tn, nk)
    (res,), carried = _call(
        body, name=name, grid=grid, in_specs=[a_spec, b_spec] + add_specs, out_specs=[out_spec],
        out_shape=[out_shape], scratch_shapes=[pltpu.VMEM((tm, tn), F32)] if nk > 1 else [],
        args=(a, b, *[x for _, x in adds]), sem=("parallel", "parallel", "arbitrary"), rider=rider)
    return (res, carried) if rider is not None else res


def _cast_bf16(w, name):
    R, C = w.shape
    tr = _row_tile(R, C * 4, 16)

    def body(w_ref, o_ref):
        o_ref[...] = w_ref[...].astype(BF16)

    return pl.pallas_call(
        body, name=name, out_shape=jax.ShapeDtypeStruct((R, C), BF16), grid=(R // tr,),
        in_specs=[pl.BlockSpec((tr, C), lambda r: (r, 0))], out_specs=pl.BlockSpec((tr, C), lambda r: (r, 0)),
        compiler_params=_cparams(("parallel",)),
    )(w)


def _cast_bf16_into_slot(w, jc_arr, name):
    R, C = w.shape
    tr = _row_tile(R, C * 4, 16)

    def body(jc_ref, w_ref, o_ref):
        o_ref[...] = w_ref[...].astype(BF16)

    gs = pltpu.PrefetchScalarGridSpec(
        num_scalar_prefetch=1, grid=(R // tr,),
        in_specs=[pl.BlockSpec((tr, C), lambda r, jc: (r, 0))],
        out_specs=pl.BlockSpec((None, tr, C), lambda r, jc: (jc[0], r, 0)))
    return pl.pallas_call(body, name=name, out_shape=jax.ShapeDtypeStruct((N_SHARDS, R, C), BF16), grid_spec=gs,
                          compiler_params=_cparams(("parallel",)))(jc_arr, w)


def _pair_sum(g, la, jc_arr, name):
    S, R, C = g.shape
    half = R // 2
    tr = _row_tile(half, C * 4, 16)
    nrt = half // tr
    dt = g.dtype

    def body(jc_ref, g_ref, la_ref, o_ref):
        o_ref[...] = (g_ref[...].astype(F32) + la_ref[...].astype(F32)).astype(dt)

    gs = pltpu.PrefetchScalarGridSpec(
        num_scalar_prefetch=1, grid=(S, nrt),
        in_specs=[pl.BlockSpec((None, tr, C), lambda s, r, jc: (s, jc[1] * nrt + r, 0)),
                  pl.BlockSpec((None, tr, C), lambda s, r, jc: (s, r, 0))],
        out_specs=pl.BlockSpec((None, tr, C), lambda s, r, jc: (s, r, 0)))
    return pl.pallas_call(body, name=name, out_shape=jax.ShapeDtypeStruct((S, half, C), dt), grid_spec=gs,
                          compiler_params=_cparams(("parallel", "parallel")))(jc_arr, g, la)


def _shard_sum(cp, lb, jc_arr, name, all_slots=False):
    S, h, C = cp.shape
    tr = _row_tile(h, C * 4, 16)

    def body(jc_ref, cp_ref, l0, l1, l2, o_ref):
        o_ref[...] = ((cp_ref[...].astype(F32) + l0[...].astype(F32)) + l1[...].astype(F32)) + l2[...].astype(F32)

    def lspec(kk):
        return pl.BlockSpec((None, tr, C), lambda r, jc: (kk, r, 0))

    if all_slots:
        out_spec = pl.BlockSpec((None, None, tr, C), lambda r, jc: (jc[0], jc[1], r, 0))
        out_shape = jax.ShapeDtypeStruct((S, 2, h, C), F32)
    else:
        out_spec = pl.BlockSpec((None, tr, C), lambda r, jc: (jc[1], r, 0))
        out_shape = jax.ShapeDtypeStruct((2, h, C), F32)
    gs = pltpu.PrefetchScalarGridSpec(
        num_scalar_prefetch=1, grid=(h // tr,),
        in_specs=[pl.BlockSpec((None, tr, C), lambda r, jc: (jc[0], r, 0)), lspec(0), lspec(1), lspec(2)],
        out_specs=out_spec)
    return pl.pallas_call(body, name=name, out_shape=out_shape, grid_spec=gs,
                          compiler_params=_cparams(("parallel",)))(jc_arr, cp, lb, lb, lb)


ANY = pl.BlockSpec(memory_space=pl.ANY)


def _place():
    x, y, c = lax.axis_index("x"), lax.axis_index("y"), lax.axis_index("c")
    chips = [(1 - x, y), (x, 1 - y), (1 - x, 1 - y)]
    return x, y, c, chips


class _Rider:
    def __init__(self, inputs, out_shape, aliases, sems, start, finish):
        self.inputs, self.out_shape, self.aliases, self.sems = list(inputs), list(out_shape), dict(aliases), list(sems)
        self.start, self.finish = start, finish


def _join_riders(r1, r2):
    i1, o1, s1 = len(r1.inputs), len(r1.out_shape), len(r1.sems)
    aliases = dict(r1.aliases)
    aliases.update({i1 + i: o1 + o for i, o in r2.aliases.items()})

    def start(ins, outs, sems):
        r1.start(ins[:i1], outs[:o1], sems[:s1])
        r2.start(ins[i1:], outs[o1:], sems[s1:])

    def finish(ins, outs, sems):
        r1.finish(ins[:i1], outs[:o1], sems[:s1])
        r2.finish(ins[i1:], outs[o1:], sems[s1:])

    return _Rider(r1.inputs + r2.inputs, r1.out_shape + r2.out_shape, aliases, r1.sems + r2.sems, start, finish)


def _call(body, *, name, grid, in_specs, out_specs, out_shape, scratch_shapes, args, sem, rider=None):
    out_specs, out_shape = tuple(out_specs), tuple(out_shape)
    if rider is None:
        res = pl.pallas_call(body, name=name, out_shape=out_shape, grid=grid, in_specs=list(in_specs),
                             out_specs=out_specs, scratch_shapes=list(scratch_shapes),
                             compiler_params=_cparams(sem))(*args)
        return tuple(res), []
    n_in, n_out, n_sc = len(in_specs), len(out_specs), len(scratch_shapes)
    r_in, r_out = len(rider.inputs), len(rider.out_shape)

    def wrapped(*refs):
        p = 0
        host_in = refs[p:p + n_in]; p += n_in
        rid_in = refs[p:p + r_in]; p += r_in
        host_out = refs[p:p + n_out]; p += n_out
        rid_out = refs[p:p + r_out]; p += r_out
        host_sc = refs[p:p + n_sc]; p += n_sc
        rid_sem = refs[p:]
        first = functools.reduce(jnp.logical_and, [pl.program_id(a) == 0 for a in range(len(grid))])
        last = functools.reduce(jnp.logical_and, [pl.program_id(a) == grid[a] - 1 for a in range(len(grid))])

        @pl.when(first)
        def _():
            rider.start(rid_in, rid_out, rid_sem)

        body(*host_in, *host_out, *host_sc)

        @pl.when(last)
        def _():
            rider.finish(rid_in, rid_out, rid_sem)

    res = pl.pallas_call(
        wrapped, name=name, out_shape=out_shape + tuple(rider.out_shape), grid=grid,
        in_specs=list(in_specs) + [ANY] * r_in, out_specs=out_specs + (ANY,) * r_out,
        input_output_aliases={n_in + i: n_out + o for i, o in rider.aliases.items()},
        scratch_shapes=list(scratch_shapes) + rider.sems,
        compiler_params=_cparams(("arbitrary",) * len(grid)),
    )(*args, *rider.inputs)
    return tuple(res[:n_out]), list(res[n_out:])


def _run_rider(rider, name):
    def body(*refs):
        r_in, r_out = len(rider.inputs), len(rider.out_shape)
        ins, outs, sems = refs[:r_in], refs[r_in:r_in + r_out], refs[r_in + r_out:]
        rider.start(ins, outs, sems)
        rider.finish(ins, outs, sems)

    return pl.pallas_call(
        body, name=name, out_shape=rider.out_shape, in_specs=[ANY] * len(rider.inputs),
        out_specs=[ANY] * len(rider.out_shape), input_output_aliases=rider.aliases, scratch_shapes=rider.sems,
    )(*rider.inputs)


def _atoms(indices, kks=(0, 1, 2), q=0, nq=1):
    return [(i, kk, q, nq) for i in indices for kk in kks]


def _gather_rider(bufs, atoms=None):
    n = len(bufs)
    if atoms is None:
        atoms = _atoms(range(n))
    na = len(atoms)

    def rows(out, atom, core):
        i, _, q, nq = atom
        half = out[i].shape[1] // 2
        assert half % (16 * nq) == 0, (half, nq)
        return pl.ds(core * half + q * (half // nq), half // nq)

    def ici_copy(out, sems, a, slot, peer):
        c = lax.axis_index("c")
        blk = out[atoms[a][0]].at[slot, rows(out, atoms[a], c), :]
        return pltpu.make_async_remote_copy(
            src_ref=blk, dst_ref=blk, send_sem=sems[0].at[a], recv_sem=sems[1].at[a],
            device_id=(peer[0], peer[1], c), device_id_type=MESH)

    def d2d_copy(out, sems, a, slot, from_core):
        x, y, c, _ = _place()
        blk = out[atoms[a][0]].at[slot, rows(out, atoms[a], from_core), :]
        return pltpu.make_async_remote_copy(
            src_ref=blk, dst_ref=blk, send_sem=sems[2].at[a], recv_sem=sems[3].at[a],
            device_id=(x, y, 1 - c), device_id_type=MESH)

    def start(ins, out, sems):
        x, y, c, chips = _place()
        for a in range(na):
            ici_copy(out, sems, a, 2 * x + y, chips[atoms[a][1]]).start()

    def finish(ins, out, sems):
        x, y, c, chips = _place()
        src = [2 * chips[atoms[a][1]][0] + chips[atoms[a][1]][1] for a in range(na)]
        for a in range(na):
            ici_copy(out, sems, a, src[a], chips[atoms[a][1]]).wait_recv()
            d2d_copy(out, sems, a, src[a], c).start()
        for a in range(na):
            d2d_copy(out, sems, a, src[a], 1 - c).wait_recv()
        for a in range(na):
            ici_copy(out, sems, a, 2 * x + y, chips[atoms[a][1]]).wait_send()
            d2d_copy(out, sems, a, src[a], c).wait_send()

    return _Rider(bufs, [jax.ShapeDtypeStruct(s.shape, s.dtype) for s in bufs], {i: i for i in range(n)},
                  [pltpu.SemaphoreType.DMA((na,))] * 4, start, finish)


def _all_gather_small(shards):
    n = len(shards)

    def body(*refs):
        w = refs[:n]
        out = refs[n:2 * n]
        local_sem, s_sem, r_sem = refs[2 * n:]
        x, y, c, chips = _place()
        j_me = 2 * x + y
        cps = []
        for i in range(n):
            lc = pltpu.make_async_copy(w[i], out[i].at[j_me], local_sem.at[i])
            lc.start()
            cps.append(lc)
        sends = []
        for i in range(n):
            for kk, (px, py) in enumerate(chips):
                cp = pltpu.make_async_remote_copy(
                    src_ref=w[i], dst_ref=out[i].at[j_me], send_sem=s_sem.at[3 * i + kk],
                    recv_sem=r_sem.at[3 * i + kk], device_id=(px, py, c), device_id_type=MESH)
                cp.start()
                sends.append(cp)
        for i in range(n):
            for kk, (px, py) in enumerate(chips):
                sends[3 * i + kk].wait_send()
                pltpu.make_async_remote_copy(
                    src_ref=w[i], dst_ref=out[i].at[2 * px + py], send_sem=s_sem.at[3 * i + kk],
                    recv_sem=r_sem.at[3 * i + kk], device_id=(px, py, c), device_id_type=MESH).wait_recv()
        for lc in cps:
            lc.wait()

    out_shape = [jax.ShapeDtypeStruct((N_SHARDS,) + s.shape, s.dtype) for s in shards]
    return pl.pallas_call(
        body, name="all_gather_conv_weights", out_shape=out_shape, in_specs=[ANY] * n, out_specs=[ANY] * n,
        scratch_shapes=[pltpu.SemaphoreType.DMA((n,)), pltpu.SemaphoreType.DMA((3 * n,)),
                        pltpu.SemaphoreType.DMA((3 * n,))],
    )(*shards)


def _pair_rider(grads):
    n = len(grads)

    def copies(g, la, sems):
        x, y, c, _ = _place()
        return [pltpu.make_async_remote_copy(
            src_ref=g[i].at[:, pl.ds((1 - c) * (g[i].shape[1] // 2), g[i].shape[1] // 2), :], dst_ref=la[i],
            send_sem=sems[0].at[i], recv_sem=sems[1].at[i], device_id=(x, y, 1 - c), device_id_type=MESH)
            for i in range(n)]

    def start(g, la, sems):
        for cp in copies(g, la, sems):
            cp.start()

    def finish(g, la, sems):
        for cp in copies(g, la, sems):
            cp.wait()

    return _Rider(grads, [jax.ShapeDtypeStruct((s.shape[0], s.shape[1] // 2, s.shape[2]), s.dtype) for s in grads],
                  {}, [pltpu.SemaphoreType.DMA((n,)), pltpu.SemaphoreType.DMA((n,))], start, finish)


def _shard_exchange_rider(cps_in, atoms=None, landing=None):
    n = len(cps_in)
    if atoms is None:
        atoms = _atoms(range(n))

    def copies(ins, lb, sems):
        x, y, c, chips = _place()
        out = []
        for a, (i, kk, q, nq) in enumerate(atoms):
            h = ins[i].shape[1]
            assert h % (16 * nq) == 0, (h, nq)
            rows = pl.ds(q * (h // nq), h // nq)
            px, py = chips[kk]
            out.append(pltpu.make_async_remote_copy(
                src_ref=ins[i].at[2 * px + py, rows, :], dst_ref=lb[i].at[kk, rows, :],
                send_sem=sems[0].at[a], recv_sem=sems[1].at[a], device_id=(px, py, c), device_id_type=MESH))
        return out

    def start(ins, lb, sems):
        for cp in copies(ins, lb, sems):
            cp.start()

    def finish(ins, lb, sems):
        for cp in copies(ins, lb, sems):
            cp.wait()

    shapes = [jax.ShapeDtypeStruct((3,) + s.shape[1:], s.dtype) for s in cps_in]
    sems = [pltpu.SemaphoreType.DMA((len(atoms),)), pltpu.SemaphoreType.DMA((len(atoms),))]
    if landing is None:
        return _Rider(cps_in, shapes, {}, sems, start, finish)
    return _Rider(list(cps_in) + list(landing), shapes, {n + i: i for i in range(n)}, sems, start, finish)


def _share_rider(halves, eighths=None):
    n = len(halves)
    bufs = list(halves) + ([eighths] if eighths is not None else [])

    def half_copy(out, sems, i, core):
        x, y, c, _ = _place()
        blk = out[i].at[core]
        return pltpu.make_async_remote_copy(src_ref=blk, dst_ref=blk, send_sem=sems[0].at[i], recv_sem=sems[1].at[i],
                                            device_id=(x, y, 1 - c), device_id_type=MESH)

    def eighth_copy(out, sems, r, mine):
        x, y, c, _ = _place()
        px, py, pc = x ^ ((r >> 2) & 1), y ^ ((r >> 1) & 1), c ^ (r & 1)
        blk = out[n].at[2 * x + y, c] if mine else out[n].at[2 * px + py, pc]
        return pltpu.make_async_remote_copy(src_ref=blk, dst_ref=blk, send_sem=sems[2].at[r - 1],
                                            recv_sem=sems[3].at[r - 1], device_id=(px, py, pc), device_id_type=MESH)

    def start(ins, out, sems):
        c = lax.axis_index("c")
        for i in range(n):
            half_copy(out, sems, i, c).start()
        if eighths is not None:
            for r in range(1, N_DEV):
                eighth_copy(out, sems, r, True).start()

    def finish(ins, out, sems):
        c = lax.axis_index("c")
        for i in range(n):
            half_copy(out, sems, i, 1 - c).wait_recv()
        if eighths is not None:
            for r in range(1, N_DEV):
                eighth_copy(out, sems, r, False).wait_recv()
        for i in range(n):
            half_copy(out, sems, i, c).wait_send()
        if eighths is not None:
            for r in range(1, N_DEV):
                eighth_copy(out, sems, r, True).wait_send()

    return _Rider(bufs, [jax.ShapeDtypeStruct(s.shape, s.dtype) for s in bufs], {i: i for i in range(len(bufs))},
                  [pltpu.SemaphoreType.DMA((max(n, 1),)), pltpu.SemaphoreType.DMA((max(n, 1),)),
                   pltpu.SemaphoreType.DMA((N_DEV - 1,)), pltpu.SemaphoreType.DMA((N_DEV - 1,))], start, finish)


ATTN_ROWS = GROUP * ATTN_BLOCK
ATTN_KEYS = 2 * ATTN_BLOCK


def _attn_geometry(n):
    row = lax.broadcasted_iota(jnp.int32, (ATTN_ROWS, ATTN_KEYS), 0)
    col = lax.broadcasted_iota(jnp.int32, (ATTN_ROWS, ATTN_KEYS), 1)
    dist = ATTN_BLOCK + jnp.bitwise_and(row, ATTN_BLOCK - 1) - col
    valid = jnp.logical_and(jnp.logical_and(dist >= 0, dist < ATTN_BLOCK),
                            jnp.logical_or(col >= ATTN_BLOCK, n > 0))
    return dist.astype(F32), valid


def _per_head_column(values):
    head = lax.broadcasted_iota(jnp.int32, (ATTN_ROWS, 1), 0) // ATTN_BLOCK
    col = jnp.zeros((ATTN_ROWS, 1), F32)
    for hh, v in enumerate(values):
        col = jnp.where(head == hh, v, col)
    return col


def _stack_heads(ref, g):
    return jnp.concatenate(
        [ref[:, (g * GROUP + hh) * HEAD_DIM:(g * GROUP + hh + 1) * HEAD_DIM].astype(BF16) for hh in range(GROUP)],
        axis=0)


def _attn_probs(q_s, k2, slope_col, sink_col, dist, valid):
    s = lax.dot_general(q_s, k2, (((1,), (1,)), ((), ())), preferred_element_type=F32) * (HEAD_DIM ** -0.5)
    s = jnp.where(valid, s - slope_col * dist, NEG)
    m = jnp.maximum(jnp.max(s, axis=1, keepdims=True), sink_col)
    e = jnp.exp(s - m)
    es = jnp.exp(sink_col - m)
    inv = 1.0 / (jnp.sum(e, axis=1, keepdims=True) + es)
    return e * inv, es * inv


def _attn_specs(T, d_attn, d_kv, q_blk, k_blk, v_blk):
    bq = pl.BlockSpec((ATTN_BLOCK, d_attn), lambda n: (n, q_blk))
    kp = pl.BlockSpec((ATTN_BLOCK, d_kv), lambda n: (jnp.maximum(n - 1, 0), k_blk))
    kc = pl.BlockSpec((ATTN_BLOCK, d_kv), lambda n: (n, k_blk))
    vp = pl.BlockSpec((ATTN_BLOCK, d_kv), lambda n: (jnp.maximum(n - 1, 0), v_blk))
    vc = pl.BlockSpec((ATTN_BLOCK, d_kv), lambda n: (n, v_blk))
    return bq, kp, kc, vp, vc


def _attn_fwd(proj, sinks, nq, cols, rider=None):
    T = proj.shape[0]
    nkv = nq // GROUP
    d_attn, d_kv = nq * HEAD_DIM, nkv * HEAD_DIM
    q_off, k_off, v_off = cols
    bq, kp, kc, vp, vc = _attn_specs(T, d_attn, d_kv, q_off // d_attn, k_off // d_kv, v_off // d_kv)

    def body(sink_ref, q_ref, kp_ref, kc_ref, vp_ref, vc_ref, o_ref):
        n = pl.program_id(0)
        dist, valid = _attn_geometry(n)
        for g in range(nkv):
            ks = slice(g * HEAD_DIM, (g + 1) * HEAD_DIM)
            k2 = jnp.concatenate([kp_ref[:, ks], kc_ref[:, ks]], axis=0).astype(BF16)
            v2 = jnp.concatenate([vp_ref[:, ks], vc_ref[:, ks]], axis=0).astype(BF16)
            slope_col = _per_head_column([2.0 ** (-8.0 * (g * GROUP + hh + 1) / nq) for hh in range(GROUP)])
            sink_col = _per_head_column([sink_ref[0, g * GROUP + hh] for hh in range(GROUP)])
            p, _ = _attn_probs(_stack_heads(q_ref, g), k2, slope_col, sink_col, dist, valid)
            o = jnp.dot(p.astype(BF16), v2, preferred_element_type=F32).astype(BF16)
            for hh in range(GROUP):
                h = g * GROUP + hh
                o_ref[:, h * HEAD_DIM:(h + 1) * HEAD_DIM] = o[hh * ATTN_BLOCK:(hh + 1) * ATTN_BLOCK, :]

    (out,), carried = _call(
        body, name="attn_fwd", out_shape=[jax.ShapeDtypeStruct((T, d_attn), BF16)], grid=(T // ATTN_BLOCK,),
        in_specs=[pl.BlockSpec(memory_space=pltpu.SMEM), bq, kp, kc, vp, vc],
        out_specs=[pl.BlockSpec((ATTN_BLOCK, d_attn), lambda n: (n, 0))], scratch_shapes=[],
        args=(sinks, proj, proj, proj, proj, proj), sem=("parallel",), rider=rider)
    return out, carried


def _attn_bwd(proj, d_attn_out, sinks, nq, cols, rider=None):
    T = proj.shape[0]
    nkv = nq // GROUP
    d_attn, d_kv = nq * HEAD_DIM, nkv * HEAD_DIM
    q_off, k_off, v_off = cols
    bq, kp, kc, vp, vc = _attn_specs(T, d_attn, d_kv, q_off // d_attn, k_off // d_kv, v_off // d_kv)
    scale = HEAD_DIM ** -0.5
    dn_t = (((1,), (1,)), ((), ()))
    dn_r = (((0,), (0,)), ((), ()))

    def body(sink_ref, q_ref, kp_ref, kc_ref, vp_ref, vc_ref, do_ref, dq_ref, dk_ref, dv_ref, ds_ref):
        n = pl.program_id(0)

        @pl.when(n == 0)
        def _():
            dk_ref[...] = jnp.zeros_like(dk_ref)
            dv_ref[...] = jnp.zeros_like(dv_ref)
            ds_ref[...] = jnp.zeros_like(ds_ref)

        dist, valid = _attn_geometry(n)
        rows_c = pl.ds(pl.multiple_of(n * ATTN_BLOCK, ATTN_BLOCK), ATTN_BLOCK)
        rows_p = pl.ds(pl.multiple_of(jnp.maximum(n - 1, 0) * ATTN_BLOCK, ATTN_BLOCK), ATTN_BLOCK)
        lane = lax.broadcasted_iota(jnp.int32, ds_ref.shape, 1)
        srow = lax.broadcasted_iota(jnp.int32, ds_ref.shape, 0)
        ds_acc = jnp.zeros(ds_ref.shape, F32)
        for g in range(nkv):
            ks = slice(g * HEAD_DIM, (g + 1) * HEAD_DIM)
            k2 = jnp.concatenate([kp_ref[:, ks], kc_ref[:, ks]], axis=0).astype(BF16)
            v2 = jnp.concatenate([vp_ref[:, ks], vc_ref[:, ks]], axis=0).astype(BF16)
            slope_col = _per_head_column([2.0 ** (-8.0 * (g * GROUP + hh + 1) / nq) for hh in range(GROUP)])
            sink_col = _per_head_column([sink_ref[0, g * GROUP + hh] for hh in range(GROUP)])
            q_s = _stack_heads(q_ref, g)
            do_s = _stack_heads(do_ref, g)
            p, p_sink = _attn_probs(q_s, k2, slope_col, sink_col, dist, valid)
            dp = lax.dot_general(do_s, v2, dn_t, preferred_element_type=F32)
            delta = jnp.sum(p * dp, axis=1, keepdims=True)
            ds = (p * (dp - delta)).astype(BF16)
            sink_part = p_sink * delta
            dq = (jnp.dot(ds, k2, preferred_element_type=F32) * scale).astype(BF16)
            for hh in range(GROUP):
                h = g * GROUP + hh
                blk = slice(hh * ATTN_BLOCK, (hh + 1) * ATTN_BLOCK)
                dq_ref[:, h * HEAD_DIM:(h + 1) * HEAD_DIM] = dq[blk, :]
                ds_acc = ds_acc + jnp.where(jnp.logical_and(lane == h, srow == 0), -jnp.sum(sink_part[blk, :]), 0.0)
            dk2 = lax.dot_general(ds, q_s, dn_r, preferred_element_type=F32) * scale
            dv2 = lax.dot_general(p.astype(BF16), do_s, dn_r, preferred_element_type=F32)
            dk_ref[rows_p, ks] += dk2[:ATTN_BLOCK, :]
            dv_ref[rows_p, ks] += dv2[:ATTN_BLOCK, :]
            dk_ref[rows_c, ks] += dk2[ATTN_BLOCK:, :]
            dv_ref[rows_c, ks] += dv2[ATTN_BLOCK:, :]
        ds_ref[...] += ds_acc

    out_shape = (jax.ShapeDtypeStruct((T, d_attn), BF16), jax.ShapeDtypeStruct((T, d_kv), F32),
                 jax.ShapeDtypeStruct((T, d_kv), F32), jax.ShapeDtypeStruct((8, LANES), F32))
    return _call(
        body, name="attn_bwd", out_shape=out_shape, grid=(T // ATTN_BLOCK,),
        in_specs=[pl.BlockSpec(memory_space=pltpu.SMEM), bq, kp, kc, vp, vc,
                  pl.BlockSpec((ATTN_BLOCK, d_attn), lambda n: (n, 0))],
        out_specs=(pl.BlockSpec((ATTN_BLOCK, d_attn), lambda n: (n, 0)),
                   pl.BlockSpec((T, d_kv), lambda n: (0, 0)), pl.BlockSpec((T, d_kv), lambda n: (0, 0)),
                   pl.BlockSpec((8, LANES), lambda n: (0, 0))),
        scratch_shapes=[], args=(sinks, proj, proj, proj, proj, proj, d_attn_out), sem=("arbitrary",), rider=rider)


def _rnn_tile(T):
    return _pick(T, (256, 128))


def _rnn_gates(x_ext, cw_ref, cb_ref, wa_ref, wi_ref, ba_ref, bi_ref, lam_ref, tt):
    xs = [pltpu.roll(x_ext, 3 - k, 0)[8:, :] if k < 3 else x_ext[8:, :] for k in range(4)]
    cx = cb_ref[...] + xs[0] * cw_ref[0:1, :]
    for k in range(1, 4):
        cx = cx + xs[k] * cw_ref[k:k + 1, :]
    cxb = cx.astype(BF16)
    r = jax.nn.sigmoid(jnp.dot(cxb, wa_ref[...], preferred_element_type=F32) + ba_ref[...])
    i = jax.nn.sigmoid(jnp.dot(cxb, wi_ref[...], preferred_element_type=F32) + bi_ref[...])
    lam = lam_ref[...]
    sp = jnp.maximum(-lam, 0.0) + jnp.log1p(jnp.exp(-jnp.abs(lam)))
    log_a = -LRU_C * r * sp
    a = jnp.exp(log_a)
    z = 2.0 * log_a
    em1 = jnp.where(z > -1e-2, z * (1.0 + z * (0.5 + z * (1.0 / 6.0 + z * (1.0 / 24.0)))), jnp.exp(z) - 1.0)
    s = jnp.sqrt(-em1)
    return xs, cx, r, i, sp, a, s


def _rnn_specs(T, gw, tt, rx_blk, ry_blk, rev):
    nT = T // tt
    hb = tt // 8

    def tile(t):
        return (nT - 1 - t) if rev else t

    rx = pl.BlockSpec((tt, gw), lambda g, t: (tile(t), rx_blk + g))
    rx_halo = pl.BlockSpec((8, gw), lambda g, t: (jnp.maximum(tile(t) * hb - 1, 0), rx_blk + g))
    ry = pl.BlockSpec((tt, gw), lambda g, t: (tile(t), ry_blk + g))
    cw = pl.BlockSpec((4, gw), lambda g, t: (0, g))
    vec = pl.BlockSpec((1, gw), lambda g, t: (0, g))
    wg = pl.BlockSpec((None, gw, gw), lambda g, t: (g, 0, 0))
    act = pl.BlockSpec((tt, gw), lambda g, t: (tile(t), g))
    act_halo = pl.BlockSpec((8, gw), lambda g, t: (jnp.maximum(tile(t) * hb - 1, 0), g))
    return rx, rx_halo, ry, cw, vec, wg, act, act_halo, tile


def _rnn_fwd(proj, cols, conv_w, conv_b, wa_g, wi_g, ba, bi, lam, rider=None):
    T = proj.shape[0]
    G, gw, _ = wa_g.shape
    d_rnn = G * gw
    tt = _rnn_tile(T)
    rx_off, ry_off = cols
    rx, rx_halo, ry, cw, vec, wg, act, _, _ = _rnn_specs(T, gw, tt, rx_off // gw, ry_off // gw, False)

    def body(rx_ref, rxh_ref, ry_ref, cw_ref, cb_ref, wa_ref, wi_ref, ba_ref, bi_ref, lam_ref,
             b_ref, h_ref, carry):
        t = pl.program_id(1)

        @pl.when(t == 0)
        def _():
            carry[...] = jnp.zeros_like(carry)

        halo = jnp.where(t > 0, rxh_ref[...], 0.0)
        x_ext = jnp.concatenate([halo, rx_ref[...]], axis=0)
        _, cx, _, i, _, a, s = _rnn_gates(x_ext, cw_ref, cb_ref, wa_ref, wi_ref, ba_ref, bi_ref, lam_ref, tt)
        acc_a, acc_b = a, s * (i * cx)
        d = 1
        while d < tt:
            acc_b = acc_a * _shift_down(acc_b, d, 0.0) + acc_b
            acc_a = acc_a * _shift_down(acc_a, d, 1.0)
            d *= 2
        h = acc_b + acc_a * carry[7:8, :]
        carry[...] = h[tt - 8:, :]
        h_ref[...] = h
        b_ref[...] = (h * _gelu(ry_ref[...])).astype(BF16)

    return _call(
        body, name="rnn_fwd",
        out_shape=(jax.ShapeDtypeStruct((T, d_rnn), BF16), jax.ShapeDtypeStruct((T, d_rnn), F32)),
        grid=(G, T // tt),
        in_specs=[rx, rx_halo, ry, cw, vec, wg, wg, vec, vec, vec], out_specs=(act, act),
        scratch_shapes=[pltpu.VMEM((8, gw), F32)],
        args=(proj, proj, proj, conv_w, conv_b, wa_g, wi_g, ba, bi, lam), sem=("parallel", "arbitrary"), rider=rider)


def _rnn_bwd(proj, cols, h_all, d_b, conv_w, conv_b, wa_g, wi_g, ba, bi, lam, rider=None):
    T = proj.shape[0]
    G, gw, _ = wa_g.shape
    d_rnn = G * gw
    tt = _rnn_tile(T)
    nT = T // tt
    rx_off, ry_off = cols
    rx, rx_halo, ry, cw, vec, wg, act, act_halo, _ = _rnn_specs(T, gw, tt, rx_off // gw, ry_off // gw, True)
    dn_t = (((1,), (1,)), ((), ()))
    dn_r = (((0,), (0,)), ((), ()))

    def body(rx_ref, rxh_ref, ry_ref, h_ref, hh_ref, db_ref, cw_ref, cb_ref, wa_ref, wi_ref, ba_ref, bi_ref, lam_ref,
             drx_ref, dry_ref, dcw_ref, dcb_ref, dba_ref, dbi_ref, dlam_ref, dwa_ref, dwi_ref,
             lam_carry, dcx_carry):
        t = pl.program_id(1)
        first_tile = t == nT - 1

        @pl.when(t == 0)
        def _():
            lam_carry[...] = jnp.zeros_like(lam_carry)
            dcx_carry[...] = jnp.zeros_like(dcx_carry)
            dcw_ref[...] = jnp.zeros_like(dcw_ref)
            dcb_ref[...] = jnp.zeros_like(dcb_ref)
            dba_ref[...] = jnp.zeros_like(dba_ref)
            dbi_ref[...] = jnp.zeros_like(dbi_ref)
            dlam_ref[...] = jnp.zeros_like(dlam_ref)
            dwa_ref[...] = jnp.zeros_like(dwa_ref)
            dwi_ref[...] = jnp.zeros_like(dwi_ref)

        halo = jnp.where(first_tile, 0.0, rxh_ref[...])
        x_ext = jnp.concatenate([halo, rx_ref[...]], axis=0)
        xs, cx, r, i, sp, a, s = _rnn_gates(x_ext, cw_ref, cb_ref, wa_ref, wi_ref, ba_ref, bi_ref, lam_ref, tt)
        h = h_ref[...]
        h_halo = jnp.where(first_tile, 0.0, hh_ref[...])
        h_prev = pltpu.roll(jnp.concatenate([h_halo, h], axis=0), 1, 0)[8:, :]
        gel, dgel = _gelu_and_grad(ry_ref[...])
        d_b_t = db_ref[...]
        dry_ref[...] = (d_b_t * h * dgel).astype(BF16)
        dh = d_b_t * gel

        acc_c = _shift_up(a, 1, 1.0)
        acc_l = dh
        d = 1
        while d < tt:
            acc_l = acc_c * _shift_up(acc_l, d, 0.0) + acc_l
            acc_c = acc_c * _shift_up(acc_c, d, 1.0)
            d *= 2
        lam_t = acc_l + acc_c * lam_carry[0:1, :]
        lam_carry[...] = (a * lam_t)[0:8, :]

        icx = i * cx
        d_s = lam_t * icx
        d_i = lam_t * s * cx
        dcx = lam_t * s * i
        d_a = lam_t * h_prev - d_s * (a / s)
        dlog_a = d_a * a
        d_r = dlog_a * (-LRU_C * sp)
        lam = lam_ref[...]
        dlam_ref[...] += jnp.sum(dlog_a * r, axis=0, keepdims=True) * (LRU_C * jax.nn.sigmoid(-lam))
        dpr = d_r * r * (1.0 - r)
        dpi = d_i * i * (1.0 - i)
        dba_ref[...] += jnp.sum(dpr, axis=0, keepdims=True)
        dbi_ref[...] += jnp.sum(dpi, axis=0, keepdims=True)
        cxb = cx.astype(BF16)
        dprb, dpib = dpr.astype(BF16), dpi.astype(BF16)
        dwa_ref[...] += lax.dot_general(cxb, dprb, dn_r, preferred_element_type=F32)
        dwi_ref[...] += lax.dot_general(cxb, dpib, dn_r, preferred_element_type=F32)
        dcx = (dcx + lax.dot_general(dprb, wa_ref[...], dn_t, preferred_element_type=F32)
               + lax.dot_general(dpib, wi_ref[...], dn_t, preferred_element_type=F32))

        dcb_ref[...] += jnp.sum(dcx, axis=0, keepdims=True)
        for k in range(4):
            dcw_ref[k:k + 1, :] += jnp.sum(dcx * xs[k], axis=0, keepdims=True)
        d_ext = jnp.concatenate([dcx, dcx_carry[...]], axis=0)
        drx = dcx * cw_ref[3:4, :]
        for k in range(3):
            drx = drx + pltpu.roll(d_ext, tt + 8 - (3 - k), 0)[:tt, :] * cw_ref[k:k + 1, :]
        drx_ref[...] = drx.astype(BF16)
        dcx_carry[...] = dcx[0:8, :]

    out_shape = (jax.ShapeDtypeStruct((T, d_rnn), BF16), jax.ShapeDtypeStruct((T, d_rnn), BF16),
                 jax.ShapeDtypeStruct((4, d_rnn), F32), jax.ShapeDtypeStruct((1, d_rnn), F32),
                 jax.ShapeDtypeStruct((1, d_rnn), F32), jax.ShapeDtypeStruct((1, d_rnn), F32),
                 jax.ShapeDtypeStruct((1, d_rnn), F32), jax.ShapeDtypeStruct((G, gw, gw), F32),
                 jax.ShapeDtypeStruct((G, gw, gw), F32))
    return _call(
        body, name="rnn_bwd", out_shape=out_shape, grid=(G, nT),
        in_specs=[rx, rx_halo, ry, act, act_halo, act, cw, vec, wg, wg, vec, vec, vec],
        out_specs=(act, act, cw, vec, vec, vec, vec, wg, wg),
        scratch_shapes=[pltpu.VMEM((8, gw), F32), pltpu.VMEM((8, gw), F32)],
        args=(proj, proj, proj, h_all, h_all, d_b, conv_w, conv_b, wa_g, wi_g, ba, bi, lam),
        sem=("parallel", "arbitrary"), rider=rider)


def _merge_fwd(proj, gl_off, b_gate, y_attn, y_rnn):
    T, D = y_attn.shape
    tm = _pick(T, (256, 128))
    ct = _pick(math.gcd(gl_off, D), (512, 256, 128))
    oa, orr, nd = gl_off // ct, (gl_off + D) // ct, D // ct

    def body(ga_ref, gr_ref, ba_ref, br_ref, ya_ref, yr_ref, m_ref):
        ga = jax.nn.sigmoid(ga_ref[...] + ba_ref[...])
        gr = jax.nn.sigmoid(gr_ref[...] + br_ref[...])
        m_ref[...] = (ga * ya_ref[...] + gr * yr_ref[...]).astype(BF16)

    blk = pl.BlockSpec((tm, ct), lambda i, j: (i, j))
    return pl.pallas_call(
        body, name="merge_fwd", out_shape=jax.ShapeDtypeStruct((T, D), BF16), grid=(T // tm, nd),
        in_specs=[pl.BlockSpec((tm, ct), lambda i, j: (i, oa + j)), pl.BlockSpec((tm, ct), lambda i, j: (i, orr + j)),
                  pl.BlockSpec((1, ct), lambda i, j: (0, j)), pl.BlockSpec((1, ct), lambda i, j: (0, nd + j)),
                  blk, blk],
        out_specs=blk, compiler_params=_cparams(("parallel", "parallel")),
    )(proj, proj, b_gate, b_gate, y_attn, y_rnn)


def _merge_bwd(proj, gl_off, b_gate, y_attn, y_rnn, d_m):
    T, D = y_attn.shape
    tm = _pick(T, (256, 128))
    ct = _pick(math.gcd(gl_off, D), (512, 256, 128))
    oa, orr, nd = gl_off // ct, (gl_off + D) // ct, D // ct

    def body(ga_ref, gr_ref, ba_ref, br_ref, ya_ref, yr_ref, dm_ref,
             dya_ref, dyr_ref, dga_ref, dgr_ref, dba_ref, dbr_ref):
        i = pl.program_id(1)

        @pl.when(i == 0)
        def _():
            dba_ref[...] = jnp.zeros_like(dba_ref)
            dbr_ref[...] = jnp.zeros_like(dbr_ref)

        ga = jax.nn.sigmoid(ga_ref[...] + ba_ref[...])
        gr = jax.nn.sigmoid(gr_ref[...] + br_ref[...])
        dm = dm_ref[...]
        dya_ref[...] = (dm * ga).astype(BF16)
        dyr_ref[...] = (dm * gr).astype(BF16)
        dga = dm * ya_ref[...] * ga * (1.0 - ga)
        dgr = dm * yr_ref[...] * gr * (1.0 - gr)
        dga_ref[...] = dga.astype(BF16)
        dgr_ref[...] = dgr.astype(BF16)
        dba_ref[...] += jnp.sum(dga, axis=0, keepdims=True)
        dbr_ref[...] += jnp.sum(dgr, axis=0, keepdims=True)

    blk = pl.BlockSpec((tm, ct), lambda j, i: (i, j))
    vec = pl.BlockSpec((1, ct), lambda j, i: (0, j))
    act = jax.ShapeDtypeStruct((T, D), BF16)
    v1 = jax.ShapeDtypeStruct((1, D), F32)
    return pl.pallas_call(
        body, name="merge_bwd", out_shape=(act, act, act, act, v1, v1), grid=(nd, T // tm),
        in_specs=[pl.BlockSpec((tm, ct), lambda j, i: (i, oa + j)), pl.BlockSpec((tm, ct), lambda j, i: (i, orr + j)),
                  vec, pl.BlockSpec((1, ct), lambda j, i: (0, nd + j)), blk, blk, blk],
        out_specs=(blk, blk, blk, blk, vec, vec),
        compiler_params=_cparams(("parallel", "arbitrary")),
    )(proj, proj, b_gate, b_gate, y_attn, y_rnn, d_m)


def _ln_fwd(x_res, delta, g, b, name):
    T, D = x_res.shape
    tm = _pick(T, (256, 128))

    def body(x_ref, d_ref, g_ref, b_ref, y_ref, yb_ref, xh_ref, rs_ref):
        z = ALPHA * x_ref[...] + d_ref[...]
        mu = jnp.mean(z, axis=1, keepdims=True)
        zc = z - mu
        var = jnp.mean(zc * zc, axis=1, keepdims=True)
        rstd = lax.rsqrt(var + LN_EPS)
        xh = zc * rstd
        xh_ref[...] = xh
        rs_ref[...] = rstd
        y = xh * g_ref[...] + b_ref[...]
        y_ref[...] = y
        yb_ref[...] = y.astype(BF16)

    row = pl.BlockSpec((tm, D), lambda i: (i, 0))
    vec = pl.BlockSpec((1, D), lambda i: (0, 0))
    return pl.pallas_call(
        body, name=name,
        out_shape=(jax.ShapeDtypeStruct((T, D), F32), jax.ShapeDtypeStruct((T, D), BF16),
                   jax.ShapeDtypeStruct((T, D), F32), jax.ShapeDtypeStruct((T, 1), F32)),
        grid=(T // tm,), in_specs=[row, row, vec, vec],
        out_specs=(row, row, row, pl.BlockSpec((tm, 1), lambda i: (i, 0))),
        compiler_params=_cparams(("parallel",)),
    )(x_res, delta, g, b)


def _ln_bwd_rows(dy, xh, rstd, g):
    dxh = dy * g
    m1 = jnp.mean(dxh, axis=1, keepdims=True)
    m2 = jnp.mean(dxh * xh, axis=1, keepdims=True)
    return rstd * (dxh - m1 - xh * m2)


def _ln_loss_bwd(x_res, delta, g, b, target):
    T, D = x_res.shape
    tm = _pick(T, (256, 128))

    def body(x_ref, d_ref, g_ref, b_ref, t_ref, dz_ref, dzb_ref, loss_ref, dg_ref, db_ref):
        i = pl.program_id(0)

        @pl.when(i == 0)
        def _():
            loss_ref[...] = jnp.zeros_like(loss_ref)
            dg_ref[...] = jnp.zeros_like(dg_ref)
            db_ref[...] = jnp.zeros_like(db_ref)

        z = ALPHA * x_ref[...] + d_ref[...]
        mu = jnp.mean(z, axis=1, keepdims=True)
        zc = z - mu
        var = jnp.mean(zc * zc, axis=1, keepdims=True)
        rstd = lax.rsqrt(var + LN_EPS)
        xh = zc * rstd
        gv = g_ref[...]
        err = xh * gv + b_ref[...] - t_ref[...]
        loss_ref[...] += 0.5 * jnp.sum(jnp.mean(err * err, axis=1, keepdims=True))
        dy = err * (1.0 / D)
        dg_ref[...] += jnp.sum(dy * xh, axis=0, keepdims=True)
        db_ref[...] += jnp.sum(dy, axis=0, keepdims=True)
        dz = _ln_bwd_rows(dy, xh, rstd, gv)
        dz_ref[...] = dz
        dzb_ref[...] = dz.astype(BF16)

    row = pl.BlockSpec((tm, D), lambda i: (i, 0))
    vec = pl.BlockSpec((1, D), lambda i: (0, 0))
    return pl.pallas_call(
        body, name="ln2_loss_bwd",
        out_shape=(jax.ShapeDtypeStruct((T, D), F32), jax.ShapeDtypeStruct((T, D), BF16),
                   jax.ShapeDtypeStruct((8, LANES), F32),
                   jax.ShapeDtypeStruct((1, D), F32), jax.ShapeDtypeStruct((1, D), F32)),
        grid=(T // tm,), in_specs=[row, row, vec, vec, row],
        out_specs=(row, row, pl.BlockSpec((8, LANES), lambda i: (0, 0)), vec, vec),
        compiler_params=_cparams(("arbitrary",)),
    )(x_res, delta, g, b, target)


def _ln_bwd(dy, xh, rstd, g):
    T, D = dy.shape
    tm = _pick(T, (256, 128))

    def body(dy_ref, xh_ref, rs_ref, g_ref, dz_ref, dzb_ref, dg_ref, db_ref):
        i = pl.program_id(0)

        @pl.when(i == 0)
        def _():
            dg_ref[...] = jnp.zeros_like(dg_ref)
            db_ref[...] = jnp.zeros_like(db_ref)

        dyv, xhv = dy_ref[...], xh_ref[...]
        dg_ref[...] += jnp.sum(dyv * xhv, axis=0, keepdims=True)
        db_ref[...] += jnp.sum(dyv, axis=0, keepdims=True)
        dz = _ln_bwd_rows(dyv, xhv, rs_ref[...], g_ref[...])
        dz_ref[...] = dz
        dzb_ref[...] = dz.astype(BF16)

    row = pl.BlockSpec((tm, D), lambda i: (i, 0))
    vec = pl.BlockSpec((1, D), lambda i: (0, 0))
    return pl.pallas_call(
        body, name="ln1_bwd",
        out_shape=(jax.ShapeDtypeStruct((T, D), F32), jax.ShapeDtypeStruct((T, D), BF16),
                   jax.ShapeDtypeStruct((1, D), F32), jax.ShapeDtypeStruct((1, D), F32)),
        grid=(T // tm,), in_specs=[row, row, pl.BlockSpec((tm, 1), lambda i: (i, 0)), vec],
        out_specs=(row, row, vec, vec), compiler_params=_cparams(("arbitrary",)),
    )(dy, xh, rstd, g)


def _ffn_col_tile(T, d_ff):
    return _pick(d_ff, (256, 128)) if T >= 1024 else _pick(d_ff, (512, 256, 128))


def _ffn_gate(gp, cw_ref, cb_ref):
    return (cb_ref[...] + gp * cw_ref[2:3, :] + _shift_down(gp, 1) * cw_ref[1:2, :]
            + _shift_down(gp, 2) * cw_ref[0:1, :])


def _ffn_fwd(up, gpre, conv_w, conv_b, rider=None):
    T, d_ff = up.shape
    ct = _ffn_col_tile(T, d_ff)

    def body(up_ref, gp_ref, cw_ref, cb_ref, f_ref):
        gate = _ffn_gate(gp_ref[...], cw_ref, cb_ref)
        f_ref[...] = (_gelu(gate) * up_ref[...]).astype(BF16)

    col = pl.BlockSpec((T, ct), lambda j: (0, j))
    (f,), carried = _call(
        body, name="ffn_act_fwd", out_shape=[jax.ShapeDtypeStruct((T, d_ff), BF16)], grid=(d_ff // ct,),
        in_specs=[col, col, pl.BlockSpec((3, ct), lambda j: (0, j)), pl.BlockSpec((1, ct), lambda j: (0, j))],
        out_specs=[col], scratch_shapes=[], args=(up, gpre, conv_w, conv_b), sem=("parallel",), rider=rider)
    return f, carried


def _ffn_bwd(up, gpre, conv_w, conv_b, d_f, rider=None):
    T, d_ff = up.shape
    ct = _ffn_col_tile(T, d_ff)

    def body(up_ref, gp_ref, cw_ref, cb_ref, df_ref, dup_ref, dgp_ref, dcw_ref, dcb_ref):
        gp = gp_ref[...]
        gate = _ffn_gate(gp, cw_ref, cb_ref)
        gel, dgel = _gelu_and_grad(gate)
        df = df_ref[...]
        dup_ref[...] = (df * gel).astype(BF16)
        dgate = df * up_ref[...] * dgel
        dcb_ref[...] = jnp.sum(dgate, axis=0, keepdims=True)
        dcw_ref[2:3, :] = jnp.sum(dgate * gp, axis=0, keepdims=True)
        dcw_ref[1:2, :] = jnp.sum(dgate * _shift_down(gp, 1), axis=0, keepdims=True)
        dcw_ref[0:1, :] = jnp.sum(dgate * _shift_down(gp, 2), axis=0, keepdims=True)
        dgp = (dgate * cw_ref[2:3, :] + _shift_up(dgate, 1) * cw_ref[1:2, :]
               + _shift_up(dgate, 2) * cw_ref[0:1, :])
        dgp_ref[...] = dgp.astype(BF16)

    col = pl.BlockSpec((T, ct), lambda j: (0, j))
    w3 = pl.BlockSpec((3, ct), lambda j: (0, j))
    v1 = pl.BlockSpec((1, ct), lambda j: (0, j))
    return _call(
        body, name="ffn_act_bwd",
        out_shape=(jax.ShapeDtypeStruct((T, d_ff), BF16), jax.ShapeDtypeStruct((T, d_ff), BF16),
                   jax.ShapeDtypeStruct((3, d_ff), F32), jax.ShapeDtypeStruct((1, d_ff), F32)),
        grid=(d_ff // ct,), in_specs=[col, col, w3, v1, col], out_specs=(col, col, w3, v1),
        scratch_shapes=[], args=(up, gpre, conv_w, conv_b, d_f), sem=("parallel",), rider=rider)


def _adamw(w, g, m, v, name, rider=None):
    R, C = w.shape
    tr = _row_tile(R, C * 4, 8, budget=1536 * 1024)
    c1 = 1.0 / (1.0 - ADAM_B1 ** ADAM_STEP)
    c2 = 1.0 / (1.0 - ADAM_B2 ** ADAM_STEP)

    def body(w_ref, g_ref, m_ref, v_ref, d_ref, nm_ref, nv_ref):
        gv = g_ref[...]
        nm = ADAM_B1 * m_ref[...] + (1.0 - ADAM_B1) * gv
        nv = ADAM_B2 * v_ref[...] + (1.0 - ADAM_B2) * (gv * gv)
        nm_ref[...] = nm
        nv_ref[...] = nv
        d_ref[...] = -ADAM_LR * ((nm * c1) / (jnp.sqrt(nv * c2) + ADAM_EPS) + ADAM_WD * w_ref[...])

    blk = pl.BlockSpec((tr, C), lambda r: (r, 0))
    sh = jax.ShapeDtypeStruct((R, C), F32)
    return _call(body, name=name, out_shape=(sh, sh, sh), grid=(R // tr,), in_specs=[blk] * 4, out_specs=(blk,) * 3,
                 scratch_shapes=[], args=(w, g, m, v), sem=("parallel",), rider=rider)


def _group_blocks(w_blocks, per):
    nb, bw, _ = w_blocks.shape
    G = nb // per
    w4 = w_blocks.reshape(G, per, bw, bw)
    rows = []
    for p in range(per):
        parts = [w4[:, p] if q == p else jnp.zeros((G, bw, bw), w_blocks.dtype) for q in range(per)]
        rows.append(jnp.concatenate(parts, axis=2))
    return jnp.concatenate(rows, axis=1)


def _ungroup_blocks(w_groups, per):
    G, gw, _ = w_groups.shape
    bw = gw // per
    blocks = [w_groups[:, p * bw:(p + 1) * bw, p * bw:(p + 1) * bw] for p in range(per)]
    return jnp.stack(blocks, axis=1).reshape(G * per, bw, bw)


def _pack(parts):
    flat = jnp.concatenate([p.reshape(-1).astype(F32) for p in parts])
    n = flat.shape[0]
    rows = -(-n // LANES)
    rows = -(-rows // PACK_ROW_MULT) * PACK_ROW_MULT
    flat = jnp.pad(flat, (0, rows * LANES - n))
    return flat.reshape(rows, LANES)


def _unpack(packed, shapes):
    flat = packed.reshape(-1)
    out, off = [], 0
    for s in shapes:
        n = math.prod(s)
        out.append(flat[off:off + n].reshape(s))
        off += n
    return out


def kernel(x, w_in, b_gate, rnn_conv_w, rnn_conv_b, lru_wa, lru_ba, lru_wi, lru_bi, lru_lambda, attn_sinks, w_attn_proj, w_rnn_proj, w_out, ln1_g, ln1_b, ffn_w_up, ffn_w_gate, ffn_conv_w, ffn_conv_b, ffn_w_down, ln2_g, ln2_b, loss_target, m_w_in, m_b_gate, m_rnn_conv_w, m_rnn_conv_b, m_lru_wa, m_lru_ba, m_lru_wi, m_lru_bi, m_lru_lambda, m_attn_sinks, m_w_attn_proj, m_w_rnn_proj, m_w_out, m_ln1_g, m_ln1_b, m_ffn_w_up, m_ffn_w_gate, m_ffn_conv_w, m_ffn_conv_b, m_ffn_w_down, m_ln2_g, m_ln2_b, v_w_in, v_b_gate, v_rnn_conv_w, v_rnn_conv_b, v_lru_wa, v_lru_ba, v_lru_wi, v_lru_bi, v_lru_lambda, v_attn_sinks, v_w_attn_proj, v_w_rnn_proj, v_w_out, v_ln1_g, v_ln1_b, v_ffn_w_up, v_ffn_w_gate, v_ffn_conv_w, v_ffn_conv_b, v_ffn_w_down, v_ln2_g, v_ln2_b):
    weights = dict(w_in=w_in, b_gate=b_gate, rnn_conv_w=rnn_conv_w, rnn_conv_b=rnn_conv_b, lru_wa=lru_wa,
                   lru_ba=lru_ba, lru_wi=lru_wi, lru_bi=lru_bi, lru_lambda=lru_lambda, attn_sinks=attn_sinks,
                   w_attn_proj=w_attn_proj, w_rnn_proj=w_rnn_proj, w_out=w_out, ln1_g=ln1_g, ln1_b=ln1_b,
                   ffn_w_up=ffn_w_up, ffn_w_gate=ffn_w_gate, ffn_conv_w=ffn_conv_w, ffn_conv_b=ffn_conv_b,
                   ffn_w_down=ffn_w_down, ln2_g=ln2_g, ln2_b=ln2_b)
    m_in = dict(w_in=m_w_in, b_gate=m_b_gate, rnn_conv_w=m_rnn_conv_w, rnn_conv_b=m_rnn_conv_b, lru_wa=m_lru_wa,
                lru_ba=m_lru_ba, lru_wi=m_lru_wi, lru_bi=m_lru_bi, lru_lambda=m_lru_lambda, attn_sinks=m_attn_sinks,
                w_attn_proj=m_w_attn_proj, w_rnn_proj=m_w_rnn_proj, w_out=m_w_out, ln1_g=m_ln1_g, ln1_b=m_ln1_b,
                ffn_w_up=m_ffn_w_up, ffn_w_gate=m_ffn_w_gate, ffn_conv_w=m_ffn_conv_w, ffn_conv_b=m_ffn_conv_b,
                ffn_w_down=m_ffn_w_down, ln2_g=m_ln2_g, ln2_b=m_ln2_b)
    v_in = dict(w_in=v_w_in, b_gate=v_b_gate, rnn_conv_w=v_rnn_conv_w, rnn_conv_b=v_rnn_conv_b, lru_wa=v_lru_wa,
                lru_ba=v_lru_ba, lru_wi=v_lru_wi, lru_bi=v_lru_bi, lru_lambda=v_lru_lambda, attn_sinks=v_attn_sinks,
                w_attn_proj=v_w_attn_proj, w_rnn_proj=v_w_rnn_proj, w_out=v_w_out, ln1_g=v_ln1_g, ln1_b=v_ln1_b,
                ffn_w_up=v_ffn_w_up, ffn_w_gate=v_ffn_w_gate, ffn_conv_w=v_ffn_conv_w, ffn_conv_b=v_ffn_conv_b,
                ffn_w_down=v_ffn_w_down, ln2_g=v_ln2_g, ln2_b=v_ln2_b)
    order = list(weights)

    assert x.shape[0] == 1 and w_in.shape[0] == 1, "one sequence per device, depth 1"
    T, D = x.shape[1], x.shape[2]
    nq = attn_sinks.shape[-1]
    nkv = nq // GROUP
    d_attn, d_kv = nq * HEAD_DIM, nkv * HEAD_DIM
    d_rnn = rnn_conv_b.shape[-1]
    d_ff = ffn_conv_b.shape[-1]
    n_blocks, bw = lru_wa.shape[1], lru_wa.shape[2]
    per = (bw * LANES // math.gcd(bw, LANES)) // bw
    gw = per * bw
    assert n_blocks % per == 0 and d_rnn == n_blocks * bw
    q_off, k_off, v_off = 0, d_attn, d_attn + d_kv
    rx_off = d_attn + 2 * d_kv
    ry_off = rx_off + d_rnn
    gl_off = ry_off + d_rnn
    d_in = gl_off + 2 * D
    assert w_in.shape[-1] * N_SHARDS == d_in
    assert k_off % d_kv == 0 and rx_off % gw == 0 and T % ATTN_BLOCK == 0

    xi, yi, ci = lax.axis_index("x"), lax.axis_index("y"), lax.axis_index("c")
    j_me = 2 * xi + yi
    jc_arr = jnp.stack([j_me, ci]).astype(jnp.int32)

    x0 = x[0]
    x0b = _cast_bf16(x0, "cast_x")
    tgt = loss_target[0]
    big = ["w_in", "w_attn_proj", "w_rnn_proj", "w_out", "ffn_w_up", "ffn_w_gate", "ffn_w_down"]
    own = {n: _cast_bf16_into_slot(weights[n][0], jc_arr, "cast_" + n) for n in big}
    (w_in_s,) = _run_rider(_gather_rider([own["w_in"]]), "all_gather_w_in")

    rcw_s, fcw_s = _all_gather_small([rnn_conv_w[0], ffn_conv_w[0]])
    rcw = jnp.concatenate([rcw_s[j] for j in range(N_SHARDS)], axis=1)
    fcw = jnp.concatenate([fcw_s[j] for j in range(N_SHARDS)], axis=1)

    wa_g = _group_blocks(lru_wa[0], per).astype(BF16)
    wi_g = _group_blocks(lru_wi[0], per).astype(BF16)

    near, diag = (0, 1), (2,)
    proj, (w_ap_s, w_rp_s, w_o_s) = _mm(
        x0b, w_in_s, name="mm_proj", b_shards=N_SHARDS,
        rider=_gather_rider([own["w_attn_proj"], own["w_rnn_proj"], own["w_out"]],
                            _atoms([0, 1]) + _atoms([2], near)))
    a_out, (w_o_s, w_up_s) = _attn_fwd(
        proj, attn_sinks, nq, (q_off, k_off, v_off),
        rider=_gather_rider([w_o_s, own["ffn_w_up"]], _atoms([0], diag) + _atoms([1], near, 0, 2)))
    (b_out, h_all), (w_up_s,) = _rnn_fwd(
        proj, (rx_off, ry_off), rcw, rnn_conv_b, wa_g, wi_g, lru_ba, lru_bi, lru_lambda,
        rider=_gather_rider([w_up_s], _atoms([0], near, 1, 2) + _atoms([0], diag)))
    w_ap, w_rp, w_o = w_ap_s.reshape(d_attn, D), w_rp_s.reshape(d_rnn, D), w_o_s.reshape(D, D)
    y_attn, (w_gate_s,) = _mm(a_out, w_ap, name="mm_attn_proj",
                              rider=_gather_rider([own["ffn_w_gate"]], _atoms([0], near, 0, 2)))
    y_rnn, (w_gate_s,) = _mm(b_out, w_rp, name="mm_rnn_proj", rider=_gather_rider([w_gate_s], _atoms([0], near, 1, 2)))
    merged = _merge_fwd(proj, gl_off, b_gate, y_attn, y_rnn)
    mix, (w_gate_s,) = _mm(merged, w_o, name="mm_out", rider=_gather_rider([w_gate_s], _atoms([0], diag, 0, 2)))
    x1, x1b, xh1, rstd1 = _ln_fwd(x0, mix, ln1_g, ln1_b, "ln1_fwd")
    up, (w_gate_s, w_dn_s) = _mm(
        x1b, w_up_s, name="mm_up", b_shards=N_SHARDS,
        rider=_gather_rider([w_gate_s, own["ffn_w_down"]], _atoms([0], diag, 1, 2) + _atoms([1], near, 0, 2)))
    gpre, (w_dn_s,) = _mm(x1b, w_gate_s, name="mm_gate", b_shards=N_SHARDS,
                          rider=_gather_rider([w_dn_s], _atoms([0], near, 1, 2) + _atoms([0], diag, 0, 2)))
    f_act, (w_dn_s,) = _ffn_fwd(up, gpre, fcw, ffn_conv_b,
                                rider=_gather_rider([w_dn_s], _atoms([0], diag, 1, 2)))
    w_dn = w_dn_s.reshape(d_ff, D)
    f_out = _mm(f_act, w_dn, name="mm_down")
    dz2, dz2b, loss_acc, dg2, db2 = _ln_loss_bwd(x1, f_out, ln2_g, ln2_b, tgt)

    def pair_sums(arrs, from_sibling, names):
        return [_pair_sum(g, la, jc_arr, "pair_sum_" + n) for g, la, n in zip(arrs, from_sibling, names)]

    def shard_sums(parts, landed, names):
        return [_shard_sum(cp, lb, jc_arr, "shard_sum_" + n) for cp, lb, n in zip(parts, landed, names)]

    halves = {}
    g_down = _mm(f_act, dz2b, name="mm_d_w_down", ta=True, out_dtype=BF16)
    g1 = [g_down.reshape(N_SHARDS, d_ff // N_SHARDS, D)]
    d_f, sib1 = _mm(dz2b, w_dn, name="mm_d_f", tb=True, rider=_pair_rider(g1))
    part1 = pair_sums(g1, sib1, ["ffn_w_down"])
    (dup, dgp, d_fcw, d_fcb), landed1 = _ffn_bwd(up, gpre, fcw, ffn_conv_b, d_f,
                                                 rider=_shard_exchange_rider(part1, _atoms([0], near)))
    g_up, landed1 = _mm(x1b, dup, name="mm_d_w_up", ta=True, out_dtype=BF16, out_shards=N_SHARDS,
                        rider=_shard_exchange_rider(part1, _atoms([0], diag), landed1))
    halves["ffn_w_down"], = shard_sums(part1, landed1, ["ffn_w_down"])
    g_gate = _mm(x1b, dgp, name="mm_d_w_gate", ta=True, out_dtype=BF16, out_shards=N_SHARDS)
    g2 = [g_up, g_gate]
    dx1_a, sib2 = _mm(dup, w_up_s, name="mm_dx1_up", tb=True, b_shards=N_SHARDS, adds=((ALPHA, dz2),),
                      rider=_pair_rider(g2))
    part2 = pair_sums(g2, sib2, ["ffn_w_up", "ffn_w_gate"])
    dx1, landed2 = _mm(dgp, w_gate_s, name="mm_dx1_gate", tb=True, b_shards=N_SHARDS, adds=((1.0, dx1_a),),
                       rider=_shard_exchange_rider(part2, _atoms([0], near)))
    dz1, dz1b, dg1, db1 = _ln_bwd(dx1, xh1, rstd1, ln1_g)
    g_out = _mm(merged, dz1b, name="mm_d_w_out", ta=True, out_dtype=BF16)
    d_m = _mm(dz1b, w_o, name="mm_d_merged", tb=True)
    dya, dyr, dgl_a, dgl_r, dbg_a, dbg_r = _merge_bwd(proj, gl_off, b_gate, y_attn, y_rnn, d_m)
    g_ap = _mm(a_out, dya, name="mm_d_w_attn_proj", ta=True, out_dtype=BF16)
    g_rp = _mm(b_out, dyr, name="mm_d_w_rnn_proj", ta=True, out_dtype=BF16)
    names3 = ["w_out", "w_attn_proj", "w_rnn_proj"]
    g3 = [g_out.reshape(N_SHARDS, D // N_SHARDS, D), g_ap.reshape(N_SHARDS, d_attn // N_SHARDS, D),
          g_rp.reshape(N_SHARDS, d_rnn // N_SHARDS, D)]
    d_a = _mm(dya, w_ap, name="mm_d_attn", tb=True)
    d_b, sib3 = _mm(dyr, w_rp, name="mm_d_rnn", tb=True, rider=_pair_rider(g3))
    part3 = pair_sums(g3, sib3, names3)
    (dq, dk, dv, dsink), landed2 = _attn_bwd(
        proj, d_a, attn_sinks, nq, (q_off, k_off, v_off),
        rider=_shard_exchange_rider(part2, _atoms([0], diag) + _atoms([1], near), landed2))
    (drx, dry, d_rcw, d_rcb, d_ba, d_bi, d_lam, d_wa_g, d_wi_g), (landed2_gate, *landed3) = _rnn_bwd(
        proj, (rx_off, ry_off), h_all, d_b, rcw, rnn_conv_b, wa_g, wi_g, lru_ba, lru_bi, lru_lambda,
        rider=_join_riders(_shard_exchange_rider(part2[1:], _atoms([0], diag), landed2[1:]),
                           _shard_exchange_rider(part3)))
    halves["ffn_w_up"], halves["ffn_w_gate"] = shard_sums(part2, [landed2[0], landed2_gate],
                                                          ["ffn_w_up", "ffn_w_gate"])
    halves["w_out"], halves["w_attn_proj"], halves["w_rnn_proj"] = shard_sums(part3, landed3, names3)
    d_proj = jnp.concatenate([dq, dk.astype(BF16), dv.astype(BF16), drx, dry, dgl_a, dgl_r], axis=1)
    first3 = ["ffn_w_down", "ffn_w_up", "ffn_w_gate", "w_out", "w_attn_proj", "w_rnn_proj"]
    g_in, shared3 = _mm(x0b, d_proj, name="mm_d_w_in", ta=True, out_dtype=BF16, out_shards=N_SHARDS,
                        rider=_share_rider([halves[n] for n in first3]))

    small_parts = [
        ("loss", loss_acc[0:1, 0:1]),
        ("b_gate", jnp.concatenate([dbg_a, dbg_r], axis=1)),
        ("rnn_conv_w", d_rcw), ("rnn_conv_b", d_rcb),
        ("lru_wa", _ungroup_blocks(d_wa_g, per)), ("lru_ba", d_ba),
        ("lru_wi", _ungroup_blocks(d_wi_g, per)), ("lru_bi", d_bi), ("lru_lambda", d_lam),
        ("attn_sinks", dsink[0:1, 0:nq]),
        ("ln1_g", dg1), ("ln1_b", db1),
        ("ffn_conv_w", d_fcw), ("ffn_conv_b", d_fcb),
        ("ln2_g", dg2), ("ln2_b", db2),
    ]
    packed = _pack([p for _, p in small_parts])
    rs = packed.shape[0]

    grads = {n: g.reshape(2 * g.shape[1], g.shape[2]) for n, g in zip(first3, shared3)}
    out_g, out_d, out_m, out_v = {}, {}, {}, {}

    def adamw(n, rider=None):
        shape = weights[n].shape
        two_d = (math.prod(shape[:-1]), shape[-1])
        g2 = grads[n].reshape(two_d)
        (d2, m2, v2), carried = _adamw(weights[n].reshape(two_d), g2, m_in[n].reshape(two_d), v_in[n].reshape(two_d),
                                       "adamw_" + n, rider=rider)
        out_g[n] = g2.reshape(shape)
        out_d[n], out_m[n], out_v[n] = d2.reshape(shape), m2.reshape(shape), v2.reshape(shape)
        return carried

    g4 = [g_in, packed.reshape(N_SHARDS, rs // N_SHARDS, LANES)]
    sib4 = _run_rider(_pair_rider(g4), "pair_exchange_in_small")
    part4 = pair_sums(g4, sib4, ["w_in", "small"])
    grad_x, landed4 = _mm(d_proj, w_in_s, name="mm_d_x", tb=True, b_shards=N_SHARDS, adds=((ALPHA, dz1),),
                          rider=_shard_exchange_rider(part4, _atoms([0], near) + _atoms([1])))
    for q, n in enumerate(["ffn_w_up", "ffn_w_gate", "ffn_w_down", "w_rnn_proj"]):
        landed4 = adamw(n, rider=_shard_exchange_rider(part4, _atoms([0], diag, q, 4), landed4))
    halves["w_in"], = shard_sums(part4[:1], landed4[:1], ["w_in"])
    eighths = _shard_sum(part4[1], landed4[1], jc_arr, "shard_sum_small", all_slots=True)
    shared_in, reduced = _run_rider(_share_rider([halves["w_in"]], eighths), "share_in_small")
    grads["w_in"] = shared_in.reshape(2 * shared_in.shape[1], shared_in.shape[2])
    reduced = reduced.reshape(rs, LANES)
    small = dict(zip([n for n, _ in small_parts], _unpack(reduced, [p.shape for _, p in small_parts])))
    loss = small.pop("loss").reshape(())
    rcw_n = d_rnn // N_SHARDS
    fcw_n = d_ff // N_SHARDS
    small["rnn_conv_w"] = lax.dynamic_slice(small["rnn_conv_w"], (0, j_me * rcw_n), (4, rcw_n))
    small["ffn_conv_w"] = lax.dynamic_slice(small["ffn_conv_w"], (0, j_me * fcw_n), (3, fcw_n))
    for n, g in small.items():
        grads[n] = g

    for n in order:
        if n not in out_g:
            adamw(n)

    return (loss, grad_x.reshape(x.shape), *[out_g[n] for n in order], *[out_d[n] for n in order],
            *[out_m[n] for n in order], *[out_v[n] for n in order])
```

```python
import functools
import math

import jax
import jax.numpy as jnp
from jax import lax
from jax.experimental import pallas as pl
from jax.experimental.pallas import tpu as pltpu

F32 = jnp.float32
BF16 = jnp.bfloat16
MESH = pl.DeviceIdType.MESH

HEAD_DIM = 64
GROUP = 8
ATTN_BLOCK = 128
LRU_C = 8.0
LN_EPS = 1e-5
ALPHA = 2.0 ** 0.25
LANES = 128
N_SHARDS = 4
N_DEV = 8
VMEM_LIMIT = 56 * 1024 * 1024
MM_VMEM_BUDGET = 40 * 1024 * 1024
MM_MAX_TILE = 3072
PACK_ROW_MULT = 8 * 64
NEG = -1e30

ADAM_LR, ADAM_B1, ADAM_B2, ADAM_EPS, ADAM_WD, ADAM_STEP = 0.001, 0.9, 0.999, 1e-08, 0.01, 10

GELU_C = math.sqrt(2.0 / math.pi)
GELU_A = 0.044715


def _cparams(sem=None):
    kw = dict(vmem_limit_bytes=VMEM_LIMIT)
    if sem is not None:
        kw["dimension_semantics"] = sem
    return pltpu.CompilerParams(**kw)


def _pick(n, prefs):
    for p in prefs:
        if n % p == 0:
            return p
    return n


def _row_tile(rows, row_bytes, mult, budget=2 * 1024 * 1024):
    best = None
    for d in range(mult, rows + 1, mult):
        if rows % d == 0 and d * row_bytes <= budget:
            best = d
    return best if best is not None else rows


def _gelu(x):
    return 0.5 * x * (1.0 + jnp.tanh(GELU_C * (x + GELU_A * x * x * x)))


def _gelu_and_grad(x):
    t = jnp.tanh(GELU_C * (x + GELU_A * x * x * x))
    g = 0.5 * x * (1.0 + t)
    dg = 0.5 * (1.0 + t) + 0.5 * x * (1.0 - t * t) * GELU_C * (1.0 + 3.0 * GELU_A * x * x)
    return g, dg


def _shift_down(x, s, fill=0.0):
    row = lax.broadcasted_iota(jnp.int32, x.shape, 0)
    return jnp.where(row >= s, pltpu.roll(x, s, 0), fill)


def _shift_up(x, s, fill=0.0):
    n = x.shape[0]
    row = lax.broadcasted_iota(jnp.int32, x.shape, 0)
    return jnp.where(row < n - s, pltpu.roll(x, n - s, 0), fill)


def _mm(a, b, *, name, ta=False, tb=False, out_dtype=F32, adds=(), b_shards=1, out_shards=1,
        tm=None, tn=None, tk=None, rider=None):
    if ta:
        K, M = a.shape
    else:
        M, K = a.shape
    if b_shards > 1:
        n_sh = b.shape[-1]
        if tb:
            N = b.shape[1]
            assert b_shards * n_sh == K
        else:
            N = b_shards * n_sh
            assert b.shape[1] == K
    else:
        n_sh = None
        if tb:
            N = b.shape[0]
            assert b.shape[1] == K
        else:
            N = b.shape[1]
            assert b.shape[0] == K
    wide = (1024, 1536, 1280, 768, 640, 512, 256, 128)
    if tn is None:
        if b_shards > 1 and not tb:
            tn = n_sh if n_sh <= MM_MAX_TILE else _pick(n_sh, wide)
        elif out_shards > 1:
            tn = N // out_shards if N // out_shards <= MM_MAX_TILE else _pick(N // out_shards, wide)
        else:
            tn = _pick(N, wide)
    if tk is None:
        if b_shards > 1 and tb:
            tk = n_sh if n_sh <= MM_MAX_TILE else _pick(n_sh, wide)
        else:
            tk = K if K <= MM_MAX_TILE else _pick(K, (2048,) + wide)
    assert N % tn == 0 and K % tk == 0, (name, M, N, K, tn, tk)
    nk = K // tk
    n_add = len(adds)
    sa, sb, so = a.dtype.itemsize, b.dtype.itemsize, jnp.dtype(out_dtype).itemsize

    def vmem_bytes(tm_):
        return (2 * (tm_ * tk * sa + tk * tn * sb + tm_ * tn * so + n_add * tm_ * tn * 4)
                + (tm_ * tn * 4 if nk > 1 else 0))

    if tm is None:
        tm = _pick(M, (1024, 512, 256, 128)) if nk > 1 else _pick(M, (512, 256, 128))
        while vmem_bytes(tm) > MM_VMEM_BUDGET and tm % 256 == 0:
            tm //= 2
    assert M % tm == 0, (name, M, tm)
    b_outer = b.size * sb >= a.size * sa

    def ij(g0, g1):
        return (g1, g0) if b_outer else (g0, g1)

    def amap(g0, g1, k):
        i, _ = ij(g0, g1)
        return (k, i) if ta else (i, k)

    def bmap(g0, g1, k):
        _, j = ij(g0, g1)
        if b_shards > 1 and not tb:
            per = n_sh // tn
            return (j // per, k, j % per)
        if b_shards > 1 and tb:
            per = n_sh // tk
            return (k // per, j, k % per)
        return (j, k) if tb else (k, j)

    def omap(g0, g1, k):
        i, j = ij(g0, g1)
        if out_shards > 1:
            per_o = (N // out_shards) // tn
            return (j // per_o, i, j % per_o)
        return (i, j)

    a_spec = pl.BlockSpec((tk, tm) if ta else (tm, tk), amap)
    if b_shards > 1:
        b_spec = pl.BlockSpec((None, tn, tk) if tb else (None, tk, tn), bmap)
    else:
        b_spec = pl.BlockSpec((tn, tk) if tb else (tk, tn), bmap)
    add_specs = [pl.BlockSpec((tm, tn), lambda g0, g1, k: ij(g0, g1)) for _ in adds]
    if out_shards > 1:
        out_spec = pl.BlockSpec((None, tm, tn), omap)
        out_shape = jax.ShapeDtypeStruct((out_shards, M, N // out_shards), out_dtype)
    else:
        out_spec = pl.BlockSpec((tm, tn), omap)
        out_shape = jax.ShapeDtypeStruct((M, N), out_dtype)

    if ta:
        dims = (((0,), (0,)), ((), ()))
    elif tb:
        dims = (((1,), (1,)), ((), ()))
    else:
        dims = (((1,), (0,)), ((), ()))
    scales = tuple(s for s, _ in adds)

    def finish(r, add_refs, o_ref):
        for s, ref in zip(scales, add_refs):
            r = r + s * ref[...].astype(F32)
        o_ref[...] = r.astype(out_dtype)

    def body(a_ref, b_ref, *rest):
        add_refs = rest[:n_add]
        o_ref = rest[n_add]
        part = lax.dot_general(a_ref[...].astype(BF16), b_ref[...].astype(BF16), dims, preferred_element_type=F32)
        if nk == 1:
            finish(part, add_refs, o_ref)
            return
        acc = rest[n_add + 1]
        k = pl.program_id(2)

        @pl.when(k == 0)
        def _():
            acc[...] = part

        @pl.when(k > 0)
        def _():
            acc[...] += part

        @pl.when(k == nk - 1)
        def _():
            finish(acc[...], add_refs, o_ref)

    grid = (N // tn, M // tm, nk) if b_outer else (M // tm, N // tn, nk)
    (res,), carried = _call(
        body, name=name, grid=grid, in_specs=[a_spec, b_spec] + add_specs, out_specs=[out_spec],
        out_shape=[out_shape], scratch_shapes=[pltpu.VMEM((tm, tn), F32)] if nk > 1 else [],
        args=(a, b, *[x for _, x in adds]), sem=("parallel", "parallel", "arbitrary"), rider=rider)
    return (res, carried) if rider is not None else res


def _cast_bf16(w, name):
    R, C = w.shape
    tr = _row_tile(R, C * 4, 16)

    def body(w_ref, o_ref):
        o_ref[...] = w_ref[...].astype(BF16)

    return pl.pallas_call(
        body, name=name, out_shape=jax.ShapeDtypeStruct((R, C), BF16), grid=(R // tr,),
        in_specs=[pl.BlockSpec((tr, C), lambda r: (r, 0))], out_specs=pl.BlockSpec((tr, C), lambda r: (r, 0)),
        compiler_params=_cparams(("parallel",)),
    )(w)


def _cast_bf16_into_slot(w, jc_arr, name):
    R, C = w.shape
    tr = _row_tile(R, C * 4, 16)

    def body(jc_ref, w_ref, o_ref):
        o_ref[...] = w_ref[...].astype(BF16)

    gs = pltpu.PrefetchScalarGridSpec(
        num_scalar_prefetch=1, grid=(R // tr,),
        in_specs=[pl.BlockSpec((tr, C), lambda r, jc: (r, 0))],
        out_specs=pl.BlockSpec((None, tr, C), lambda r, jc: (jc[0], r, 0)))
    return pl.pallas_call(body, name=name, out_shape=jax.ShapeDtypeStruct((N_SHARDS, R, C), BF16), grid_spec=gs,
                          compiler_params=_cparams(("parallel",)))(jc_arr, w)


def _pair_sum(g, la, jc_arr, name):
    S, R, C = g.shape
    half = R // 2
    tr = _row_tile(half, C * 4, 16)
    nrt = half // tr
    dt = g.dtype

    def body(jc_ref, g_ref, la_ref, o_ref):
        o_ref[...] = (g_ref[...].astype(F32) + la_ref[...].astype(F32)).astype(dt)

    gs = pltpu.PrefetchScalarGridSpec(
        num_scalar_prefetch=1, grid=(S, nrt),
        in_specs=[pl.BlockSpec((None, tr, C), lambda s, r, jc: (s, jc[1] * nrt + r, 0)),
                  pl.BlockSpec((None, tr, C), lambda s, r, jc: (s, r, 0))],
        out_specs=pl.BlockSpec((None, tr, C), lambda s, r, jc: (s, r, 0)))
    return pl.pallas_call(body, name=name, out_shape=jax.ShapeDtypeStruct((S, half, C), dt), grid_spec=gs,
                          compiler_params=_cparams(("parallel", "parallel")))(jc_arr, g, la)


def _shard_sum(cp, lb, jc_arr, name, all_slots=False):
    S, h, C = cp.shape
    tr = _row_tile(h, C * 4, 16)

    def body(jc_ref, cp_ref, l0, l1, l2, o_ref):
        o_ref[...] = ((cp_ref[...].astype(F32) + l0[...].astype(F32)) + l1[...].astype(F32)) + l2[...].astype(F32)

    def lspec(kk):
        return pl.BlockSpec((None, tr, C), lambda r, jc: (kk, r, 0))

    if all_slots:
        out_spec = pl.BlockSpec((None, None, tr, C), lambda r, jc: (jc[0], jc[1], r, 0))
        out_shape = jax.ShapeDtypeStruct((S, 2, h, C), F32)
    else:
        out_spec = pl.BlockSpec((None, tr, C), lambda r, jc: (jc[1], r, 0))
        out_shape = jax.ShapeDtypeStruct((2, h, C), F32)
    gs = pltpu.PrefetchScalarGridSpec(
        num_scalar_prefetch=1, grid=(h // tr,),
        in_specs=[pl.BlockSpec((None, tr, C), lambda r, jc: (jc[0], r, 0)), lspec(0), lspec(1), lspec(2)],
        out_specs=out_spec)
    return pl.pallas_call(body, name=name, out_shape=out_shape, grid_spec=gs,
                          compiler_params=_cparams(("parallel",)))(jc_arr, cp, lb, lb, lb)


ANY = pl.BlockSpec(memory_space=pl.ANY)


def _place():
    x, y, c = lax.axis_index("x"), lax.axis_index("y"), lax.axis_index("c")
    chips = [(1 - x, y), (x, 1 - y), (1 - x, 1 - y)]
    return x, y, c, chips


class _Rider:
    def __init__(self, inputs, out_shape, aliases, sems, start, finish):
        self.inputs, self.out_shape, self.aliases, self.sems = list(inputs), list(out_shape), dict(aliases), list(sems)
        self.start, self.finish = start, finish


def _join_riders(r1, r2):
    i1, o1, s1 = len(r1.inputs), len(r1.out_shape), len(r1.sems)
    aliases = dict(r1.aliases)
    aliases.update({i1 + i: o1 + o for i, o in r2.aliases.items()})

    def start(ins, outs, sems):
        r1.start(ins[:i1], outs[:o1], sems[:s1])
        r2.start(ins[i1:], outs[o1:], sems[s1:])

    def finish(ins, outs, sems):
        r1.finish(ins[:i1], outs[:o1], sems[:s1])
        r2.finish(ins[i1:], outs[o1:], sems[s1:])

    return _Rider(r1.inputs + r2.inputs, r1.out_shape + r2.out_shape, aliases, r1.sems + r2.sems, start, finish)


def _call(body, *, name, grid, in_specs, out_specs, out_shape, scratch_shapes, args, sem, rider=None):
    out_specs, out_shape = tuple(out_specs), tuple(out_shape)
    if rider is None:
        res = pl.pallas_call(body, name=name, out_shape=out_shape, grid=grid, in_specs=list(in_specs),
                             out_specs=out_specs, scratch_shapes=list(scratch_shapes),
                             compiler_params=_cparams(sem))(*args)
        return tuple(res), []
    n_in, n_out, n_sc = len(in_specs), len(out_specs), len(scratch_shapes)
    r_in, r_out = len(rider.inputs), len(rider.out_shape)

    def wrapped(*refs):
        p = 0
        host_in = refs[p:p + n_in]; p += n_in
        rid_in = refs[p:p + r_in]; p += r_in
        host_out = refs[p:p + n_out]; p += n_out
        rid_out = refs[p:p + r_out]; p += r_out
        host_sc = refs[p:p + n_sc]; p += n_sc
        rid_sem = refs[p:]
        first = functools.reduce(jnp.logical_and, [pl.program_id(a) == 0 for a in range(len(grid))])
        last = functools.reduce(jnp.logical_and, [pl.program_id(a) == grid[a] - 1 for a in range(len(grid))])

        @pl.when(first)
        def _():
            rider.start(rid_in, rid_out, rid_sem)

        body(*host_in, *host_out, *host_sc)

        @pl.when(last)
        def _():
            rider.finish(rid_in, rid_out, rid_sem)

    res = pl.pallas_call(
        wrapped, name=name, out_shape=out_shape + tuple(rider.out_shape), grid=grid,
        in_specs=list(in_specs) + [ANY] * r_in, out_specs=out_specs + (ANY,) * r_out,
        input_output_aliases={n_in + i: n_out + o for i, o in rider.aliases.items()},
        scratch_shapes=list(scratch_shapes) + rider.sems,
        compiler_params=_cparams(("arbitrary",) * len(grid)),
    )(*args, *rider.inputs)
    return tuple(res[:n_out]), list(res[n_out:])


def _run_rider(rider, name):
    def body(*refs):
        r_in, r_out = len(rider.inputs), len(rider.out_shape)
        ins, outs, sems = refs[:r_in], refs[r_in:r_in + r_out], refs[r_in + r_out:]
        rider.start(ins, outs, sems)
        rider.finish(ins, outs, sems)

    return pl.pallas_call(
        body, name=name, out_shape=rider.out_shape, in_specs=[ANY] * len(rider.inputs),
        out_specs=[ANY] * len(rider.out_shape), input_output_aliases=rider.aliases, scratch_shapes=rider.sems,
    )(*rider.inputs)


def _atoms(indices, kks=(0, 1, 2), q=0, nq=1):
    return [(i, kk, q, nq) for i in indices for kk in kks]


def _gather_rider(bufs, atoms=None):
    n = len(bufs)
    if atoms is None:
        atoms = _atoms(range(n))
    na = len(atoms)

    def rows(out, atom, core):
        i, _, q, nq = atom
        half = out[i].shape[1] // 2
        assert half % (16 * nq) == 0, (half, nq)
        return pl.ds(core * half + q * (half // nq), half // nq)

    def ici_copy(out, sems, a, slot, peer):
        c = lax.axis_index("c")
        blk = out[atoms[a][0]].at[slot, rows(out, atoms[a], c), :]
        return pltpu.make_async_remote_copy(
            src_ref=blk, dst_ref=blk, send_sem=sems[0].at[a], recv_sem=sems[1].at[a],
            device_id=(peer[0], peer[1], c), device_id_type=MESH)

    def d2d_copy(out, sems, a, slot, from_core):
        x, y, c, _ = _place()
        blk = out[atoms[a][0]].at[slot, rows(out, atoms[a], from_core), :]
        return pltpu.make_async_remote_copy(
            src_ref=blk, dst_ref=blk, send_sem=sems[2].at[a], recv_sem=sems[3].at[a],
            device_id=(x, y, 1 - c), device_id_type=MESH)

    def start(ins, out, sems):
        x, y, c, chips = _place()
        for a in range(na):
            ici_copy(out, sems, a, 2 * x + y, chips[atoms[a][1]]).start()

    def finish(ins, out, sems):
        x, y, c, chips = _place()
        src = [2 * chips[atoms[a][1]][0] + chips[atoms[a][1]][1] for a in range(na)]
        for a in range(na):
            ici_copy(out, sems, a, src[a], chips[atoms[a][1]]).wait_recv()
            d2d_copy(out, sems, a, src[a], c).start()
        for a in range(na):
            d2d_copy(out, sems, a, src[a], 1 - c).wait_recv()
        for a in range(na):
            ici_copy(out, sems, a, 2 * x + y, chips[atoms[a][1]]).wait_send()
            d2d_copy(out, sems, a, src[a], c).wait_send()

    return _Rider(bufs, [jax.ShapeDtypeStruct(s.shape, s.dtype) for s in bufs], {i: i for i in range(n)},
                  [pltpu.SemaphoreType.DMA((na,))] * 4, start, finish)


def _all_gather_small(shards):
    n = len(shards)

    def body(*refs):
        w = refs[:n]
        out = refs[n:2 * n]
        local_sem, s_sem, r_sem = refs[2 * n:]
        x, y, c, chips = _place()
        j_me = 2 * x + y
        cps = []
        for i in range(n):
            lc = pltpu.make_async_copy(w[i], out[i].at[j_me], local_sem.at[i])
            lc.start()
            cps.append(lc)
        sends = []
        for i in range(n):
            for kk, (px, py) in enumerate(chips):
                cp = pltpu.make_async_remote_copy(
                    src_ref=w[i], dst_ref=out[i].at[j_me], send_sem=s_sem.at[3 * i + kk],
                    recv_sem=r_sem.at[3 * i + kk], device_id=(px, py, c), device_id_type=MESH)
                cp.start()
                sends.append(cp)
        for i in range(n):
            for kk, (px, py) in enumerate(chips):
                sends[3 * i + kk].wait_send()
                pltpu.make_async_remote_copy(
                    src_ref=w[i], dst_ref=out[i].at[2 * px + py], send_sem=s_sem.at[3 * i + kk],
                    recv_sem=r_sem.at[3 * i + kk], device_id=(px, py, c), device_id_type=MESH).wait_recv()
        for lc in cps:
            lc.wait()

    out_shape = [jax.ShapeDtypeStruct((N_SHARDS,) + s.shape, s.dtype) for s in shards]
    return pl.pallas_call(
        body, name="all_gather_conv_weights", out_shape=out_shape, in_specs=[ANY] * n, out_specs=[ANY] * n,
        scratch_shapes=[pltpu.SemaphoreType.DMA((n,)), pltpu.SemaphoreType.DMA((3 * n,)),
                        pltpu.SemaphoreType.DMA((3 * n,))],
    )(*shards)


def _pair_rider(grads):
    n = len(grads)

    def copies(g, la, sems):
        x, y, c, _ = _place()
        return [pltpu.make_async_remote_copy(
            src_ref=g[i].at[:, pl.ds((1 - c) * (g[i].shape[1] // 2), g[i].shape[1] // 2), :], dst_ref=la[i],
            send_sem=sems[0].at[i], recv_sem=sems[1].at[i], device_id=(x, y, 1 - c), device_id_type=MESH)
            for i in range(n)]

    def start(g, la, sems):
        for cp in copies(g, la, sems):
            cp.start()

    def finish(g, la, sems):
        for cp in copies(g, la, sems):
            cp.wait()

    return _Rider(grads, [jax.ShapeDtypeStruct((s.shape[0], s.shape[1] // 2, s.shape[2]), s.dtype) for s in grads],
                  {}, [pltpu.SemaphoreType.DMA((n,)), pltpu.SemaphoreType.DMA((n,))], start, finish)


def _shard_exchange_rider(cps_in, atoms=None, landing=None):
    n = len(cps_in)
    if atoms is None:
        atoms = _atoms(range(n))

    def copies(ins, lb, sems):
        x, y, c, chips = _place()
        out = []
        for a, (i, kk, q, nq) in enumerate(atoms):
            h = ins[i].shape[1]
            assert h % (16 * nq) == 0, (h, nq)
            rows = pl.ds(q * (h // nq), h // nq)
            px, py = chips[kk]
            out.append(pltpu.make_async_remote_copy(
                src_ref=ins[i].at[2 * px + py, rows, :], dst_ref=lb[i].at[kk, rows, :],
                send_sem=sems[0].at[a], recv_sem=sems[1].at[a], device_id=(px, py, c), device_id_type=MESH))
        return out

    def start(ins, lb, sems):
        for cp in copies(ins, lb, sems):
            cp.start()

    def finish(ins, lb, sems):
        for cp in copies(ins, lb, sems):
            cp.wait()

    shapes = [jax.ShapeDtypeStruct((3,) + s.shape[1:], s.dtype) for s in cps_in]
    sems = [pltpu.SemaphoreType.DMA((len(atoms),)), pltpu.SemaphoreType.DMA((len(atoms),))]
    if landing is None:
        return _Rider(cps_in, shapes, {}, sems, start, finish)
    return _Rider(list(cps_in) + list(landing), shapes, {n + i: i for i in range(n)}, sems, start, finish)


def _share_rider(halves, eighths=None):
    n = len(halves)
    bufs = list(halves) + ([eighths] if eighths is not None else [])

    def half_copy(out, sems, i, core):
        x, y, c, _ = _place()
        blk = out[i].at[core]
        return pltpu.make_async_remote_copy(src_ref=blk, dst_ref=blk, send_sem=sems[0].at[i], recv_sem=sems[1].at[i],
                                            device_id=(x, y, 1 - c), device_id_type=MESH)

    def eighth_copy(out, sems, r, mine):
        x, y, c, _ = _place()
        px, py, pc = x ^ ((r >> 2) & 1), y ^ ((r >> 1) & 1), c ^ (r & 1)
        blk = out[n].at[2 * x + y, c] if mine else out[n].at[2 * px + py, pc]
        return pltpu.make_async_remote_copy(src_ref=blk, dst_ref=blk, send_sem=sems[2].at[r - 1],
                                            recv_sem=sems[3].at[r - 1], device_id=(px, py, pc), device_id_type=MESH)

    def start(ins, out, sems):
        c = lax.axis_index("c")
        for i in range(n):
            half_copy(out, sems, i, c).start()
        if eighths is not None:
            for r in range(1, N_DEV):
                eighth_copy(out, sems, r, True).start()

    def finish(ins, out, sems):
        c = lax.axis_index("c")
        for i in range(n):
            half_copy(out, sems, i, 1 - c).wait_recv()
        if eighths is not None:
            for r in range(1, N_DEV):
                eighth_copy(out, sems, r, False).wait_recv()
        for i in range(n):
            half_copy(out, sems, i, c).wait_send()
        if eighths is not None:
            for r in range(1, N_DEV):
                eighth_copy(out, sems, r, True).wait_send()

    return _Rider(bufs, [jax.ShapeDtypeStruct(s.shape, s.dtype) for s in bufs], {i: i for i in range(len(bufs))},
                  [pltpu.SemaphoreType.DMA((max(n, 1),)), pltpu.SemaphoreType.DMA((max(n, 1),)),
                   pltpu.SemaphoreType.DMA((N_DEV - 1,)), pltpu.SemaphoreType.DMA((N_DEV - 1,))], start, finish)


ATTN_ROWS = GROUP * ATTN_BLOCK
ATTN_KEYS = 2 * ATTN_BLOCK


def _attn_geometry(n):
    row = lax.broadcasted_iota(jnp.int32, (ATTN_ROWS, ATTN_KEYS), 0)
    col = lax.broadcasted_iota(jnp.int32, (ATTN_ROWS, ATTN_KEYS), 1)
    dist = ATTN_BLOCK + jnp.bitwise_and(row, ATTN_BLOCK - 1) - col
    valid = jnp.logical_and(jnp.logical_and(dist >= 0, dist < ATTN_BLOCK),
                            jnp.logical_or(col >= ATTN_BLOCK, n > 0))
    return dist.astype(F32), valid


def _per_head_column(values):
    head = lax.broadcasted_iota(jnp.int32, (ATTN_ROWS, 1), 0) // ATTN_BLOCK
    col = jnp.zeros((ATTN_ROWS, 1), F32)
    for hh, v in enumerate(values):
        col = jnp.where(head == hh, v, col)
    return col


def _stack_heads(ref, g):
    return jnp.concatenate(
        [ref[:, (g * GROUP + hh) * HEAD_DIM:(g * GROUP + hh + 1) * HEAD_DIM].astype(BF16) for hh in range(GROUP)],
        axis=0)


def _attn_probs(q_s, k2, slope_col, sink_col, dist, valid):
    s = lax.dot_general(q_s, k2, (((1,), (1,)), ((), ())), preferred_element_type=F32) * (HEAD_DIM ** -0.5)
    s = jnp.where(valid, s - slope_col * dist, NEG)
    m = jnp.maximum(jnp.max(s, axis=1, keepdims=True), sink_col)
    e = jnp.exp(s - m)
    es = jnp.exp(sink_col - m)
    inv = 1.0 / (jnp.sum(e, axis=1, keepdims=True) + es)
    return e * inv, es * inv


def _attn_specs(T, d_attn, d_kv, q_blk, k_blk, v_blk):
    bq = pl.BlockSpec((ATTN_BLOCK, d_attn), lambda n: (n, q_blk))
    kp = pl.BlockSpec((ATTN_BLOCK, d_kv), lambda n: (jnp.maximum(n - 1, 0), k_blk))
    kc = pl.BlockSpec((ATTN_BLOCK, d_kv), lambda n: (n, k_blk))
    vp = pl.BlockSpec((ATTN_BLOCK, d_kv), lambda n: (jnp.maximum(n - 1, 0), v_blk))
    vc = pl.BlockSpec((ATTN_BLOCK, d_kv), lambda n: (n, v_blk))
    return bq, kp, kc, vp, vc


def _attn_fwd(proj, sinks, nq, cols, rider=None):
    T = proj.shape[0]
    nkv = nq // GROUP
    d_attn, d_kv = nq * HEAD_DIM, nkv * HEAD_DIM
    q_off, k_off, v_off = cols
    bq, kp, kc, vp, vc = _attn_specs(T, d_attn, d_kv, q_off // d_attn, k_off // d_kv, v_off // d_kv)

    def body(sink_ref, q_ref, kp_ref, kc_ref, vp_ref, vc_ref, o_ref):
        n = pl.program_id(0)
        dist, valid = _attn_geometry(n)
        for g in range(nkv):
            ks = slice(g * HEAD_DIM, (g + 1) * HEAD_DIM)
            k2 = jnp.concatenate([kp_ref[:, ks], kc_ref[:, ks]], axis=0).astype(BF16)
            v2 = jnp.concatenate([vp_ref[:, ks], vc_ref[:, ks]], axis=0).astype(BF16)
            slope_col = _per_head_column([2.0 ** (-8.0 * (g * GROUP + hh + 1) / nq) for hh in range(GROUP)])
            sink_col = _per_head_column([sink_ref[0, g * GROUP + hh] for hh in range(GROUP)])
            p, _ = _attn_probs(_stack_heads(q_ref, g), k2, slope_col, sink_col, dist, valid)
            o = jnp.dot(p.astype(BF16), v2, preferred_element_type=F32).astype(BF16)
            for hh in range(GROUP):
                h = g * GROUP + hh
                o_ref[:, h * HEAD_DIM:(h + 1) * HEAD_DIM] = o[hh * ATTN_BLOCK:(hh + 1) * ATTN_BLOCK, :]

    (out,), carried = _call(
        body, name="attn_fwd", out_shape=[jax.ShapeDtypeStruct((T, d_attn), BF16)], grid=(T // ATTN_BLOCK,),
        in_specs=[pl.BlockSpec(memory_space=pltpu.SMEM), bq, kp, kc, vp, vc],
        out_specs=[pl.BlockSpec((ATTN_BLOCK, d_attn), lambda n: (n, 0))], scratch_shapes=[],
        args=(sinks, proj, proj, proj, proj, proj), sem=("parallel",), rider=rider)
    return out, carried


def _attn_bwd(proj, d_attn_out, sinks, nq, cols, rider=None):
    T = proj.shape[0]
    nkv = nq // GROUP
    d_attn, d_kv = nq * HEAD_DIM, nkv * HEAD_DIM
    q_off, k_off, v_off = cols
    bq, kp, kc, vp, vc = _attn_specs(T, d_attn, d_kv, q_off // d_attn, k_off // d_kv, v_off // d_kv)
    scale = HEAD_DIM ** -0.5
    dn_t = (((1,), (1,)), ((), ()))
    dn_r = (((0,), (0,)), ((), ()))

    def body(sink_ref, q_ref, kp_ref, kc_ref, vp_ref, vc_ref, do_ref, dq_ref, dk_ref, dv_ref, ds_ref):
        n = pl.program_id(0)

        @pl.when(n == 0)
        def _():
            dk_ref[...] = jnp.zeros_like(dk_ref)
            dv_ref[...] = jnp.zeros_like(dv_ref)
            ds_ref[...] = jnp.zeros_like(ds_ref)

        dist, valid = _attn_geometry(n)
        rows_c = pl.ds(pl.multiple_of(n * ATTN_BLOCK, ATTN_BLOCK), ATTN_BLOCK)
        rows_p = pl.ds(pl.multiple_of(jnp.maximum(n - 1, 0) * ATTN_BLOCK, ATTN_BLOCK), ATTN_BLOCK)
        lane = lax.broadcasted_iota(jnp.int32, ds_ref.shape, 1)
        srow = lax.broadcasted_iota(jnp.int32, ds_ref.shape, 0)
        ds_acc = jnp.zeros(ds_ref.shape, F32)
        for g in range(nkv):
            ks = slice(g * HEAD_DIM, (g + 1) * HEAD_DIM)
            k2 = jnp.concatenate([kp_ref[:, ks], kc_ref[:, ks]], axis=0).astype(BF16)
            v2 = jnp.concatenate([vp_ref[:, ks], vc_ref[:, ks]], axis=0).astype(BF16)
            slope_col = _per_head_column([2.0 ** (-8.0 * (g * GROUP + hh + 1) / nq) for hh in range(GROUP)])
            sink_col = _per_head_column([sink_ref[0, g * GROUP + hh] for hh in range(GROUP)])
            q_s = _stack_heads(q_ref, g)
            do_s = _stack_heads(do_ref, g)
            p, p_sink = _attn_probs(q_s, k2, slope_col, sink_col, dist, valid)
            dp = lax.dot_general(do_s, v2, dn_t, preferred_element_type=F32)
            delta = jnp.sum(p * dp, axis=1, keepdims=True)
            ds = (p * (dp - delta)).astype(BF16)
            sink_part = p_sink * delta
            dq = (jnp.dot(ds, k2, preferred_element_type=F32) * scale).astype(BF16)
            for hh in range(GROUP):
                h = g * GROUP + hh
                blk = slice(hh * ATTN_BLOCK, (hh + 1) * ATTN_BLOCK)
                dq_ref[:, h * HEAD_DIM:(h + 1) * HEAD_DIM] = dq[blk, :]
                ds_acc = ds_acc + jnp.where(jnp.logical_and(lane == h, srow == 0), -jnp.sum(sink_part[blk, :]), 0.0)
            dk2 = lax.dot_general(ds, q_s, dn_r, preferred_element_type=F32) * scale
            dv2 = lax.dot_general(p.astype(BF16), do_s, dn_r, preferred_element_type=F32)
            dk_ref[rows_p, ks] += dk2[:ATTN_BLOCK, :]
            dv_ref[rows_p, ks] += dv2[:ATTN_BLOCK, :]
            dk_ref[rows_c, ks] += dk2[ATTN_BLOCK:, :]
            dv_ref[rows_c, ks] += dv2[ATTN_BLOCK:, :]
        ds_ref[...] += ds_acc

    out_shape = (jax.ShapeDtypeStruct((T, d_attn), BF16), jax.ShapeDtypeStruct((T, d_kv), F32),
                 jax.ShapeDtypeStruct((T, d_kv), F32), jax.ShapeDtypeStruct((8, LANES), F32))
    return _call(
        body, name="attn_bwd", out_shape=out_shape, grid=(T // ATTN_BLOCK,),
        in_specs=[pl.BlockSpec(memory_space=pltpu.SMEM), bq, kp, kc, vp, vc,
                  pl.BlockSpec((ATTN_BLOCK, d_attn), lambda n: (n, 0))],
        out_specs=(pl.BlockSpec((ATTN_BLOCK, d_attn), lambda n: (n, 0)),
                   pl.BlockSpec((T, d_kv), lambda n: (0, 0)), pl.BlockSpec((T, d_kv), lambda n: (0, 0)),
                   pl.BlockSpec((8, LANES), lambda n: (0, 0))),
        scratch_shapes=[], args=(sinks, proj, proj, proj, proj, proj, d_attn_out), sem=("arbitrary",), rider=rider)


def _rnn_tile(T):
    return _pick(T, (256, 128))


def _rnn_gates(x_ext, cw_ref, cb_ref, wa_ref, wi_ref, ba_ref, bi_ref, lam_ref, tt):
    xs = [pltpu.roll(x_ext, 3 - k, 0)[8:, :] if k < 3 else x_ext[8:, :] for k in range(4)]
    cx = cb_ref[...] + xs[0] * cw_ref[0:1, :]
    for k in range(1, 4):
        cx = cx + xs[k] * cw_ref[k:k + 1, :]
    cxb = cx.astype(BF16)
    r = jax.nn.sigmoid(jnp.dot(cxb, wa_ref[...], preferred_element_type=F32) + ba_ref[...])
    i = jax.nn.sigmoid(jnp.dot(cxb, wi_ref[...], preferred_element_type=F32) + bi_ref[...])
    lam = lam_ref[...]
    sp = jnp.maximum(-lam, 0.0) + jnp.log1p(jnp.exp(-jnp.abs(lam)))
    log_a = -LRU_C * r * sp
    a = jnp.exp(log_a)
    z = 2.0 * log_a
    em1 = jnp.where(z > -1e-2, z * (1.0 + z * (0.5 + z * (1.0 / 6.0 + z * (1.0 / 24.0)))), jnp.exp(z) - 1.0)
    s = jnp.sqrt(-em1)
    return xs, cx, r, i, sp, a, s


def _rnn_specs(T, gw, tt, rx_blk, ry_blk, rev):
    nT = T // tt
    hb = tt // 8

    def tile(t):
        return (nT - 1 - t) if rev else t

    rx = pl.BlockSpec((tt, gw), lambda g, t: (tile(t), rx_blk + g))
    rx_halo = pl.BlockSpec((8, gw), lambda g, t: (jnp.maximum(tile(t) * hb - 1, 0), rx_blk + g))
    ry = pl.BlockSpec((tt, gw), lambda g, t: (tile(t), ry_blk + g))
    cw = pl.BlockSpec((4, gw), lambda g, t: (0, g))
    vec = pl.BlockSpec((1, gw), lambda g, t: (0, g))
    wg = pl.BlockSpec((None, gw, gw), lambda g, t: (g, 0, 0))
    act = pl.BlockSpec((tt, gw), lambda g, t: (tile(t), g))
    act_halo = pl.BlockSpec((8, gw), lambda g, t: (jnp.maximum(tile(t) * hb - 1, 0), g))
    return rx, rx_halo, ry, cw, vec, wg, act, act_halo, tile


def _rnn_fwd(proj, cols, conv_w, conv_b, wa_g, wi_g, ba, bi, lam, rider=None):
    T = proj.shape[0]
    G, gw, _ = wa_g.shape
    d_rnn = G * gw
    tt = _rnn_tile(T)
    rx_off, ry_off = cols
    rx, rx_halo, ry, cw, vec, wg, act, _, _ = _rnn_specs(T, gw, tt, rx_off // gw, ry_off // gw, False)

    def body(rx_ref, rxh_ref, ry_ref, cw_ref, cb_ref, wa_ref, wi_ref, ba_ref, bi_ref, lam_ref,
             b_ref, h_ref, carry):
        t = pl.program_id(1)

        @pl.when(t == 0)
        def _():
            carry[...] = jnp.zeros_like(carry)

        halo = jnp.where(t > 0, rxh_ref[...], 0.0)
        x_ext = jnp.concatenate([halo, rx_ref[...]], axis=0)
        _, cx, _, i, _, a, s = _rnn_gates(x_ext, cw_ref, cb_ref, wa_ref, wi_ref, ba_ref, bi_ref, lam_ref, tt)
        acc_a, acc_b = a, s * (i * cx)
        d = 1
        while d < tt:
            acc_b = acc_a * _shift_down(acc_b, d, 0.0) + acc_b
            acc_a = acc_a * _shift_down(acc_a, d, 1.0)
            d *= 2
        h = acc_b + acc_a * carry[7:8, :]
        carry[...] = h[tt - 8:, :]
        h_ref[...] = h
        b_ref[...] = (h * _gelu(ry_ref[...])).astype(BF16)

    return _call(
        body, name="rnn_fwd",
        out_shape=(jax.ShapeDtypeStruct((T, d_rnn), BF16), jax.ShapeDtypeStruct((T, d_rnn), F32)),
        grid=(G, T // tt),
        in_specs=[rx, rx_halo, ry, cw, vec, wg, wg, vec, vec, vec], out_specs=(act, act),
        scratch_shapes=[pltpu.VMEM((8, gw), F32)],
        args=(proj, proj, proj, conv_w, conv_b, wa_g, wi_g, ba, bi, lam), sem=("parallel", "arbitrary"), rider=rider)


def _rnn_bwd(proj, cols, h_all, d_b, conv_w, conv_b, wa_g, wi_g, ba, bi, lam, rider=None):
    T = proj.shape[0]
    G, gw, _ = wa_g.shape
    d_rnn = G * gw
    tt = _rnn_tile(T)
    nT = T // tt
    rx_off, ry_off = cols
    rx, rx_halo, ry, cw, vec, wg, act, act_halo, _ = _rnn_specs(T, gw, tt, rx_off // gw, ry_off // gw, True)
    dn_t = (((1,), (1,)), ((), ()))
    dn_r = (((0,), (0,)), ((), ()))

    def body(rx_ref, rxh_ref, ry_ref, h_ref, hh_ref, db_ref, cw_ref, cb_ref, wa_ref, wi_ref, ba_ref, bi_ref, lam_ref,
             drx_ref, dry_ref, dcw_ref, dcb_ref, dba_ref, dbi_ref, dlam_ref, dwa_ref, dwi_ref,
             lam_carry, dcx_carry):
        t = pl.program_id(1)
        first_tile = t == nT - 1

        @pl.when(t == 0)
        def _():
            lam_carry[...] = jnp.zeros_like(lam_carry)
            dcx_carry[...] = jnp.zeros_like(dcx_carry)
            dcw_ref[...] = jnp.zeros_like(dcw_ref)
            dcb_ref[...] = jnp.zeros_like(dcb_ref)
            dba_ref[...] = jnp.zeros_like(dba_ref)
            dbi_ref[...] = jnp.zeros_like(dbi_ref)
            dlam_ref[...] = jnp.zeros_like(dlam_ref)
            dwa_ref[...] = jnp.zeros_like(dwa_ref)
            dwi_ref[...] = jnp.zeros_like(dwi_ref)

        halo = jnp.where(first_tile, 0.0, rxh_ref[...])
        x_ext = jnp.concatenate([halo, rx_ref[...]], axis=0)
        xs, cx, r, i, sp, a, s = _rnn_gates(x_ext, cw_ref, cb_ref, wa_ref, wi_ref, ba_ref, bi_ref, lam_ref, tt)
        h = h_ref[...]
        h_halo = jnp.where(first_tile, 0.0, hh_ref[...])
        h_prev = pltpu.roll(jnp.concatenate([h_halo, h], axis=0), 1, 0)[8:, :]
        gel, dgel = _gelu_and_grad(ry_ref[...])
        d_b_t = db_ref[...]
        dry_ref[...] = (d_b_t * h * dgel).astype(BF16)
        dh = d_b_t * gel

        acc_c = _shift_up(a, 1, 1.0)
        acc_l = dh
        d = 1
        while d < tt:
            acc_l = acc_c * _shift_up(acc_l, d, 0.0) + acc_l
            acc_c = acc_c * _shift_up(acc_c, d, 1.0)
            d *= 2
        lam_t = acc_l + acc_c * lam_carry[0:1, :]
        lam_carry[...] = (a * lam_t)[0:8, :]

        icx = i * cx
        d_s = lam_t * icx
        d_i = lam_t * s * cx
        dcx = lam_t * s * i
        d_a = lam_t * h_prev - d_s * (a / s)
        dlog_a = d_a * a
        d_r = dlog_a * (-LRU_C * sp)
        lam = lam_ref[...]
        dlam_ref[...] += jnp.sum(dlog_a * r, axis=0, keepdims=True) * (LRU_C * jax.nn.sigmoid(-lam))
        dpr = d_r * r * (1.0 - r)
        dpi = d_i * i * (1.0 - i)
        dba_ref[...] += jnp.sum(dpr, axis=0, keepdims=True)
        dbi_ref[...] += jnp.sum(dpi, axis=0, keepdims=True)
        cxb = cx.astype(BF16)
        dprb, dpib = dpr.astype(BF16), dpi.astype(BF16)
        dwa_ref[...] += lax.dot_general(cxb, dprb, dn_r, preferred_element_type=F32)
        dwi_ref[...] += lax.dot_general(cxb, dpib, dn_r, preferred_element_type=F32)
        dcx = (dcx + lax.dot_general(dprb, wa_ref[...], dn_t, preferred_element_type=F32)
               + lax.dot_general(dpib, wi_ref[...], dn_t, preferred_element_type=F32))

        dcb_ref[...] += jnp.sum(dcx, axis=0, keepdims=True)
        for k in range(4):
            dcw_ref[k:k + 1, :] += jnp.sum(dcx * xs[k], axis=0, keepdims=True)
        d_ext = jnp.concatenate([dcx, dcx_carry[...]], axis=0)
        drx = dcx * cw_ref[3:4, :]
        for k in range(3):
            drx = drx + pltpu.roll(d_ext, tt + 8 - (3 - k), 0)[:tt, :] * cw_ref[k:k + 1, :]
        drx_ref[...] = drx.astype(BF16)
        dcx_carry[...] = dcx[0:8, :]

    out_shape = (jax.ShapeDtypeStruct((T, d_rnn), BF16), jax.ShapeDtypeStruct((T, d_rnn), BF16),
                 jax.ShapeDtypeStruct((4, d_rnn), F32), jax.ShapeDtypeStruct((1, d_rnn), F32),
                 jax.ShapeDtypeStruct((1, d_rnn), F32), jax.ShapeDtypeStruct((1, d_rnn), F32),
                 jax.ShapeDtypeStruct((1, d_rnn), F32), jax.ShapeDtypeStruct((G, gw, gw), F32),
                 jax.ShapeDtypeStruct((G, gw, gw), F32))
    return _call(
        body, name="rnn_bwd", out_shape=out_shape, grid=(G, nT),
        in_specs=[rx, rx_halo, ry, act, act_halo, act, cw, vec, wg, wg, vec, vec, vec],
        out_specs=(act, act, cw, vec, vec, vec, vec, wg, wg),
        scratch_shapes=[pltpu.VMEM((8, gw), F32), pltpu.VMEM((8, gw), F32)],
        args=(proj, proj, proj, h_all, h_all, d_b, conv_w, conv_b, wa_g, wi_g, ba, bi, lam),
        sem=("parallel", "arbitrary"), rider=rider)


def _merge_fwd(proj, gl_off, b_gate, y_attn, y_rnn):
    T, D = y_attn.shape
    tm = _pick(T, (256, 128))
    ct = _pick(math.gcd(gl_off, D), (512, 256, 128))
    oa, orr, nd = gl_off // ct, (gl_off + D) // ct, D // ct

    def body(ga_ref, gr_ref, ba_ref, br_ref, ya_ref, yr_ref, m_ref):
        ga = jax.nn.sigmoid(ga_ref[...] + ba_ref[...])
        gr = jax.nn.sigmoid(gr_ref[...] + br_ref[...])
        m_ref[...] = (ga * ya_ref[...] + gr * yr_ref[...]).astype(BF16)

    blk = pl.BlockSpec((tm, ct), lambda i, j: (i, j))
    return pl.pallas_call(
        body, name="merge_fwd", out_shape=jax.ShapeDtypeStruct((T, D), BF16), grid=(T // tm, nd),
        in_specs=[pl.BlockSpec((tm, ct), lambda i, j: (i, oa + j)), pl.BlockSpec((tm, ct), lambda i, j: (i, orr + j)),
                  pl.BlockSpec((1, ct), lambda i, j: (0, j)), pl.BlockSpec((1, ct), lambda i, j: (0, nd + j)),
                  blk, blk],
        out_specs=blk, compiler_params=_cparams(("parallel", "parallel")),
    )(proj, proj, b_gate, b_gate, y_attn, y_rnn)


def _merge_bwd(proj, gl_off, b_gate, y_attn, y_rnn, d_m):
    T, D = y_attn.shape
    tm = _pick(T, (256, 128))
    ct = _pick(math.gcd(gl_off, D), (512, 256, 128))
    oa, orr, nd = gl_off // ct, (gl_off + D) // ct, D // ct

    def body(ga_ref, gr_ref, ba_ref, br_ref, ya_ref, yr_ref, dm_ref,
             dya_ref, dyr_ref, dga_ref, dgr_ref, dba_ref, dbr_ref):
        i = pl.program_id(1)

        @pl.when(i == 0)
        def _():
            dba_ref[...] = jnp.zeros_like(dba_ref)
            dbr_ref[...] = jnp.zeros_like(dbr_ref)

        ga = jax.nn.sigmoid(ga_ref[...] + ba_ref[...])
        gr = jax.nn.sigmoid(gr_ref[...] + br_ref[...])
        dm = dm_ref[...]
        dya_ref[...] = (dm * ga).astype(BF16)
        dyr_ref[...] = (dm * gr).astype(BF16)
        dga = dm * ya_ref[...] * ga * (1.0 - ga)
        dgr = dm * yr_ref[...] * gr * (1.0 - gr)
        dga_ref[...] = dga.astype(BF16)
        dgr_ref[...] = dgr.astype(BF16)
        dba_ref[...] += jnp.sum(dga, axis=0, keepdims=True)
        dbr_ref[...] += jnp.sum(dgr, axis=0, keepdims=True)

    blk = pl.BlockSpec((tm, ct), lambda j, i: (i, j))
    vec = pl.BlockSpec((1, ct), lambda j, i: (0, j))
    act = jax.ShapeDtypeStruct((T, D), BF16)
    v1 = jax.ShapeDtypeStruct((1, D), F32)
    return pl.pallas_call(
        body, name="merge_bwd", out_shape=(act, act, act, act, v1, v1), grid=(nd, T // tm),
        in_specs=[pl.BlockSpec((tm, ct), lambda j, i: (i, oa + j)), pl.BlockSpec((tm, ct), lambda j, i: (i, orr + j)),
                  vec, pl.BlockSpec((1, ct), lambda j, i: (0, nd + j)), blk, blk, blk],
        out_specs=(blk, blk, blk, blk, vec, vec),
        compiler_params=_cparams(("parallel", "arbitrary")),
    )(proj, proj, b_gate, b_gate, y_attn, y_rnn, d_m)


def _ln_fwd(x_res, delta, g, b, name):
    T, D = x_res.shape
    tm = _pick(T, (256, 128))

    def body(x_ref, d_ref, g_ref, b_ref, y_ref, yb_ref, xh_ref, rs_ref):
        z = ALPHA * x_ref[...] + d_ref[...]
        mu = jnp.mean(z, axis=1, keepdims=True)
        zc = z - mu
        var = jnp.mean(zc * zc, axis=1, keepdims=True)
        rstd = lax.rsqrt(var + LN_EPS)
        xh = zc * rstd
        xh_ref[...] = xh
        rs_ref[...] = rstd
        y = xh * g_ref[...] + b_ref[...]
        y_ref[...] = y
        yb_ref[...] = y.astype(BF16)

    row = pl.BlockSpec((tm, D), lambda i: (i, 0))
    vec = pl.BlockSpec((1, D), lambda i: (0, 0))
    return pl.pallas_call(
        body, name=name,
        out_shape=(jax.ShapeDtypeStruct((T, D), F32), jax.ShapeDtypeStruct((T, D), BF16),
                   jax.ShapeDtypeStruct((T, D), F32), jax.ShapeDtypeStruct((T, 1), F32)),
        grid=(T // tm,), in_specs=[row, row, vec, vec],
        out_specs=(row, row, row, pl.BlockSpec((tm, 1), lambda i: (i, 0))),
        compiler_params=_cparams(("parallel",)),
    )(x_res, delta, g, b)


def _ln_bwd_rows(dy, xh, rstd, g):
    dxh = dy * g
    m1 = jnp.mean(dxh, axis=1, keepdims=True)
    m2 = jnp.mean(dxh * xh, axis=1, keepdims=True)
    return rstd * (dxh - m1 - xh * m2)


def _ln_loss_bwd(x_res, delta, g, b, target):
    T, D = x_res.shape
    tm = _pick(T, (256, 128))

    def body(x_ref, d_ref, g_ref, b_ref, t_ref, dz_ref, dzb_ref, loss_ref, dg_ref, db_ref):
        i = pl.program_id(0)

        @pl.when(i == 0)
        def _():
            loss_ref[...] = jnp.zeros_like(loss_ref)
            dg_ref[...] = jnp.zeros_like(dg_ref)
            db_ref[...] = jnp.zeros_like(db_ref)

        z = ALPHA * x_ref[...] + d_ref[...]
        mu = jnp.mean(z, axis=1, keepdims=True)
        zc = z - mu
        var = jnp.mean(zc * zc, axis=1, keepdims=True)
        rstd = lax.rsqrt(var + LN_EPS)
        xh = zc * rstd
        gv = g_ref[...]
        err = xh * gv + b_ref[...] - t_ref[...]
        loss_ref[...] += 0.5 * jnp.sum(jnp.mean(err * err, axis=1, keepdims=True))
        dy = err * (1.0 / D)
        dg_ref[...] += jnp.sum(dy * xh, axis=0, keepdims=True)
        db_ref[...] += jnp.sum(dy, axis=0, keepdims=True)
        dz = _ln_bwd_rows(dy, xh, rstd, gv)
        dz_ref[...] = dz
        dzb_ref[...] = dz.astype(BF16)

    row = pl.BlockSpec((tm, D), lambda i: (i, 0))
    vec = pl.BlockSpec((1, D), lambda i: (0, 0))
    return pl.pallas_call(
        body, name="ln2_loss_bwd",
        out_shape=(jax.ShapeDtypeStruct((T, D), F32), jax.ShapeDtypeStruct((T, D), BF16),
                   jax.ShapeDtypeStruct((8, LANES), F32),
                   jax.ShapeDtypeStruct((1, D), F32), jax.ShapeDtypeStruct((1, D), F32)),
        grid=(T // tm,), in_specs=[row, row, vec, vec, row],
        out_specs=(row, row, pl.BlockSpec((8, LANES), lambda i: (0, 0)), vec, vec),
        compiler_params=_cparams(("arbitrary",)),
    )(x_res, delta, g, b, target)


def _ln_bwd(dy, xh, rstd, g):
    T, D = dy.shape
    tm = _pick(T, (256, 128))

    def body(dy_ref, xh_ref, rs_ref, g_ref, dz_ref, dzb_ref, dg_ref, db_ref):
        i = pl.program_id(0)

        @pl.when(i == 0)
        def _():
            dg_ref[...] = jnp.zeros_like(dg_ref)
            db_ref[...] = jnp.zeros_like(db_ref)

        dyv, xhv = dy_ref[...], xh_ref[...]
        dg_ref[...] += jnp.sum(dyv * xhv, axis=0, keepdims=True)
        db_ref[...] += jnp.sum(dyv, axis=0, keepdims=True)
        dz = _ln_bwd_rows(dyv, xhv, rs_ref[...], g_ref[...])
        dz_ref[...] = dz
        dzb_ref[...] = dz.astype(BF16)

    row = pl.BlockSpec((tm, D), lambda i: (i, 0))
    vec = pl.BlockSpec((1, D), lambda i: (0, 0))
    return pl.pallas_call(
        body, name="ln1_bwd",
        out_shape=(jax.ShapeDtypeStruct((T, D), F32), jax.ShapeDtypeStruct((T, D), BF16),
                   jax.ShapeDtypeStruct((1, D), F32), jax.ShapeDtypeStruct((1, D), F32)),
        grid=(T // tm,), in_specs=[row, row, pl.BlockSpec((tm, 1), lambda i: (i, 0)), vec],
        out_specs=(row, row, vec, vec), compiler_params=_cparams(("arbitrary",)),
    )(dy, xh, rstd, g)


def _ffn_col_tile(T, d_ff):
    return _pick(d_ff, (256, 128)) if T >= 1024 else _pick(d_ff, (512, 256, 128))


def _ffn_gate(gp, cw_ref, cb_ref):
    return (cb_ref[...] + gp * cw_ref[2:3, :] + _shift_down(gp, 1) * cw_ref[1:2, :]
            + _shift_down(gp, 2) * cw_ref[0:1, :])


def _ffn_fwd(up, gpre, conv_w, conv_b, rider=None):
    T, d_ff = up.shape
    ct = _ffn_col_tile(T, d_ff)

    def body(up_ref, gp_ref, cw_ref, cb_ref, f_ref):
        gate = _ffn_gate(gp_ref[...], cw_ref, cb_ref)
        f_ref[...] = (_gelu(gate) * up_ref[...]).astype(BF16)

    col = pl.BlockSpec((T, ct), lambda j: (0, j))
    (f,), carried = _call(
        body, name="ffn_act_fwd", out_shape=[jax.ShapeDtypeStruct((T, d_ff), BF16)], grid=(d_ff // ct,),
        in_specs=[col, col, pl.BlockSpec((3, ct), lambda j: (0, j)), pl.BlockSpec((1, ct), lambda j: (0, j))],
        out_specs=[col], scratch_shapes=[], args=(up, gpre, conv_w, conv_b), sem=("parallel",), rider=rider)
    return f, carried


def _ffn_bwd(up, gpre, conv_w, conv_b, d_f, rider=None):
    T, d_ff = up.shape
    ct = _ffn_col_tile(T, d_ff)

    def body(up_ref, gp_ref, cw_ref, cb_ref, df_ref, dup_ref, dgp_ref, dcw_ref, dcb_ref):
        gp = gp_ref[...]
        gate = _ffn_gate(gp, cw_ref, cb_ref)
        gel, dgel = _gelu_and_grad(gate)
        df = df_ref[...]
        dup_ref[...] = (df * gel).astype(BF16)
        dgate = df * up_ref[...] * dgel
        dcb_ref[...] = jnp.sum(dgate, axis=0, keepdims=True)
        dcw_ref[2:3, :] = jnp.sum(dgate * gp, axis=0, keepdims=True)
        dcw_ref[1:2, :] = jnp.sum(dgate * _shift_down(gp, 1), axis=0, keepdims=True)
        dcw_ref[0:1, :] = jnp.sum(dgate * _shift_down(gp, 2), axis=0, keepdims=True)
        dgp = (dgate * cw_ref[2:3, :] + _shift_up(dgate, 1) * cw_ref[1:2, :]
               + _shift_up(dgate, 2) * cw_ref[0:1, :])
        dgp_ref[...] = dgp.astype(BF16)

    col = pl.BlockSpec((T, ct), lambda j: (0, j))
    w3 = pl.BlockSpec((3, ct), lambda j: (0, j))
    v1 = pl.BlockSpec((1, ct), lambda j: (0, j))
    return _call(
        body, name="ffn_act_bwd",
        out_shape=(jax.ShapeDtypeStruct((T, d_ff), BF16), jax.ShapeDtypeStruct((T, d_ff), BF16),
                   jax.ShapeDtypeStruct((3, d_ff), F32), jax.ShapeDtypeStruct((1, d_ff), F32)),
        grid=(d_ff // ct,), in_specs=[col, col, w3, v1, col], out_specs=(col, col, w3, v1),
        scratch_shapes=[], args=(up, gpre, conv_w, conv_b, d_f), sem=("parallel",), rider=rider)


def _adamw(w, g, m, v, name):
    R, C = w.shape
    tr = _row_tile(R, C * 4, 8, budget=1536 * 1024)
    c1 = 1.0 / (1.0 - ADAM_B1 ** ADAM_STEP)
    c2 = 1.0 / (1.0 - ADAM_B2 ** ADAM_STEP)

    def body(w_ref, g_ref, m_ref, v_ref, d_ref, nm_ref, nv_ref):
        gv = g_ref[...]
        nm = ADAM_B1 * m_ref[...] + (1.0 - ADAM_B1) * gv
        nv = ADAM_B2 * v_ref[...] + (1.0 - ADAM_B2) * (gv * gv)
        nm_ref[...] = nm
        nv_ref[...] = nv
        d_ref[...] = -ADAM_LR * ((nm * c1) / (jnp.sqrt(nv * c2) + ADAM_EPS) + ADAM_WD * w_ref[...])

    blk = pl.BlockSpec((tr, C), lambda r: (r, 0))
    sh = jax.ShapeDtypeStruct((R, C), F32)
    return _call(body, name=name, out_shape=(sh, sh, sh), grid=(R // tr,), in_specs=[blk] * 4, out_specs=(blk,) * 3,
                 scratch_shapes=[], args=(w, g, m, v), sem=("parallel",))[0]


def _group_blocks(w_blocks, per):
    nb, bw, _ = w_blocks.shape
    G = nb // per
    w4 = w_blocks.reshape(G, per, bw, bw)
    rows = []
    for p in range(per):
        parts = [w4[:, p] if q == p else jnp.zeros((G, bw, bw), w_blocks.dtype) for q in range(per)]
        rows.append(jnp.concatenate(parts, axis=2))
    return jnp.concatenate(rows, axis=1)


def _ungroup_blocks(w_groups, per):
    G, gw, _ = w_groups.shape
    bw = gw // per
    blocks = [w_groups[:, p * bw:(p + 1) * bw, p * bw:(p + 1) * bw] for p in range(per)]
    return jnp.stack(blocks, axis=1).reshape(G * per, bw, bw)


def _pack(parts):
    flat = jnp.concatenate([p.reshape(-1).astype(F32) for p in parts])
    n = flat.shape[0]
    rows = -(-n // LANES)
    rows = -(-rows // PACK_ROW_MULT) * PACK_ROW_MULT
    flat = jnp.pad(flat, (0, rows * LANES - n))
    return flat.reshape(rows, LANES)


def _unpack(packed, shapes):
    flat = packed.reshape(-1)
    out, off = [], 0
    for s in shapes:
        n = math.prod(s)
        out.append(flat[off:off + n].reshape(s))
        off += n
    return out


def kernel(x, w_in, b_gate, rnn_conv_w, rnn_conv_b, lru_wa, lru_ba, lru_wi, lru_bi, lru_lambda, attn_sinks, w_attn_proj, w_rnn_proj, w_out, ln1_g, ln1_b, ffn_w_up, ffn_w_gate, ffn_conv_w, ffn_conv_b, ffn_w_down, ln2_g, ln2_b, loss_target, m_w_in, m_b_gate, m_rnn_conv_w, m_rnn_conv_b, m_lru_wa, m_lru_ba, m_lru_wi, m_lru_bi, m_lru_lambda, m_attn_sinks, m_w_attn_proj, m_w_rnn_proj, m_w_out, m_ln1_g, m_ln1_b, m_ffn_w_up, m_ffn_w_gate, m_ffn_conv_w, m_ffn_conv_b, m_ffn_w_down, m_ln2_g, m_ln2_b, v_w_in, v_b_gate, v_rnn_conv_w, v_rnn_conv_b, v_lru_wa, v_lru_ba, v_lru_wi, v_lru_bi, v_lru_lambda, v_attn_sinks, v_w_attn_proj, v_w_rnn_proj, v_w_out, v_ln1_g, v_ln1_b, v_ffn_w_up, v_ffn_w_gate, v_ffn_conv_w, v_ffn_conv_b, v_ffn_w_down, v_ln2_g, v_ln2_b):
    weights = dict(w_in=w_in, b_gate=b_gate, rnn_conv_w=rnn_conv_w, rnn_conv_b=rnn_conv_b, lru_wa=lru_wa,
                   lru_ba=lru_ba, lru_wi=lru_wi, lru_bi=lru_bi, lru_lambda=lru_lambda, attn_sinks=attn_sinks,
                   w_attn_proj=w_attn_proj, w_rnn_proj=w_rnn_proj, w_out=w_out, ln1_g=ln1_g, ln1_b=ln1_b,
                   ffn_w_up=ffn_w_up, ffn_w_gate=ffn_w_gate, ffn_conv_w=ffn_conv_w, ffn_conv_b=ffn_conv_b,
                   ffn_w_down=ffn_w_down, ln2_g=ln2_g, ln2_b=ln2_b)
    m_in = dict(w_in=m_w_in, b_gate=m_b_gate, rnn_conv_w=m_rnn_conv_w, rnn_conv_b=m_rnn_conv_b, lru_wa=m_lru_wa,
                lru_ba=m_lru_ba, lru_wi=m_lru_wi, lru_bi=m_lru_bi, lru_lambda=m_lru_lambda, attn_sinks=m_attn_sinks,
                w_attn_proj=m_w_attn_proj, w_rnn_proj=m_w_rnn_proj, w_out=m_w_out, ln1_g=m_ln1_g, ln1_b=m_ln1_b,
                ffn_w_up=m_ffn_w_up, ffn_w_gate=m_ffn_w_gate, ffn_conv_w=m_ffn_conv_w, ffn_conv_b=m_ffn_conv_b,
                ffn_w_down=m_ffn_w_down, ln2_g=m_ln2_g, ln2_b=m_ln2_b)
    v_in = dict(w_in=v_w_in, b_gate=v_b_gate, rnn_conv_w=v_rnn_conv_w, rnn_conv_b=v_rnn_conv_b, lru_wa=v_lru_wa,
                lru_ba=v_lru_ba, lru_wi=v_lru_wi, lru_bi=v_lru_bi, lru_lambda=v_lru_lambda, attn_sinks=v_attn_sinks,
                w_attn_proj=v_w_attn_proj, w_rnn_proj=v_w_rnn_proj, w_out=v_w_out, ln1_g=v_ln1_g, ln1_b=v_ln1_b,
                ffn_w_up=v_ffn_w_up, ffn_w_gate=v_ffn_w_gate, ffn_conv_w=v_ffn_conv_w, ffn_conv_b=v_ffn_conv_b,
                ffn_w_down=v_ffn_w_down, ln2_g=v_ln2_g, ln2_b=v_ln2_b)
    order = list(weights)

    assert x.shape[0] == 1 and w_in.shape[0] == 1, "one sequence per device, depth 1"
    T, D = x.shape[1], x.shape[2]
    nq = attn_sinks.shape[-1]
    nkv = nq // GROUP
    d_attn, d_kv = nq * HEAD_DIM, nkv * HEAD_DIM
    d_rnn = rnn_conv_b.shape[-1]
    d_ff = ffn_conv_b.shape[-1]
    n_blocks, bw = lru_wa.shape[1], lru_wa.shape[2]
    per = (bw * LANES // math.gcd(bw, LANES)) // bw
    gw = per * bw
    assert n_blocks % per == 0 and d_rnn == n_blocks * bw
    q_off, k_off, v_off = 0, d_attn, d_attn + d_kv
    rx_off = d_attn + 2 * d_kv
    ry_off = rx_off + d_rnn
    gl_off = ry_off + d_rnn
    d_in = gl_off + 2 * D
    assert w_in.shape[-1] * N_SHARDS == d_in
    assert k_off % d_kv == 0 and rx_off % gw == 0 and T % ATTN_BLOCK == 0

    xi, yi, ci = lax.axis_index("x"), lax.axis_index("y"), lax.axis_index("c")
    j_me = 2 * xi + yi
    jc_arr = jnp.stack([j_me, ci]).astype(jnp.int32)

    x0 = x[0]
    x0b = _cast_bf16(x0, "cast_x")
    tgt = loss_target[0]
    big = ["w_in", "w_attn_proj", "w_rnn_proj", "w_out", "ffn_w_up", "ffn_w_gate", "ffn_w_down"]
    own = {n: _cast_bf16_into_slot(weights[n][0], jc_arr, "cast_" + n) for n in big}
    (w_in_s,) = _run_rider(_gather_rider([own["w_in"]]), "all_gather_w_in")

    rcw_s, fcw_s = _all_gather_small([rnn_conv_w[0], ffn_conv_w[0]])
    rcw = jnp.concatenate([rcw_s[j] for j in range(N_SHARDS)], axis=1)
    fcw = jnp.concatenate([fcw_s[j] for j in range(N_SHARDS)], axis=1)

    wa_g = _group_blocks(lru_wa[0], per).astype(BF16)
    wi_g = _group_blocks(lru_wi[0], per).astype(BF16)

    near, diag = (0, 1), (2,)
    proj, (w_ap_s, w_rp_s, w_o_s) = _mm(
        x0b, w_in_s, name="mm_proj", b_shards=N_SHARDS,
        rider=_gather_rider([own["w_attn_proj"], own["w_rnn_proj"], own["w_out"]],
                            _atoms([0, 1]) + _atoms([2], near)))
    a_out, (w_o_s, w_up_s) = _attn_fwd(
        proj, attn_sinks, nq, (q_off, k_off, v_off),
        rider=_gather_rider([w_o_s, own["ffn_w_up"]], _atoms([0], diag) + _atoms([1], near, 0, 2)))
    (b_out, h_all), (w_up_s,) = _rnn_fwd(
        proj, (rx_off, ry_off), rcw, rnn_conv_b, wa_g, wi_g, lru_ba, lru_bi, lru_lambda,
        rider=_gather_rider([w_up_s], _atoms([0], near, 1, 2) + _atoms([0], diag)))
    w_ap, w_rp, w_o = w_ap_s.reshape(d_attn, D), w_rp_s.reshape(d_rnn, D), w_o_s.reshape(D, D)
    y_attn, (w_gate_s,) = _mm(a_out, w_ap, name="mm_attn_proj",
                              rider=_gather_rider([own["ffn_w_gate"]], _atoms([0], near, 0, 2)))
    y_rnn, (w_gate_s,) = _mm(b_out, w_rp, name="mm_rnn_proj", rider=_gather_rider([w_gate_s], _atoms([0], near, 1, 2)))
    merged = _merge_fwd(proj, gl_off, b_gate, y_attn, y_rnn)
    mix, (w_gate_s,) = _mm(merged, w_o, name="mm_out", rider=_gather_rider([w_gate_s], _atoms([0], diag, 0, 2)))
    x1, x1b, xh1, rstd1 = _ln_fwd(x0, mix, ln1_g, ln1_b, "ln1_fwd")
    up, (w_gate_s, w_dn_s) = _mm(
        x1b, w_up_s, name="mm_up", b_shards=N_SHARDS,
        rider=_gather_rider([w_gate_s, own["ffn_w_down"]], _atoms([0], diag, 1, 2) + _atoms([1], near, 0, 2)))
    gpre, (w_dn_s,) = _mm(x1b, w_gate_s, name="mm_gate", b_shards=N_SHARDS,
                          rider=_gather_rider([w_dn_s], _atoms([0], near, 1, 2) + _atoms([0], diag, 0, 2)))
    f_act, (w_dn_s,) = _ffn_fwd(up, gpre, fcw, ffn_conv_b,
                                rider=_gather_rider([w_dn_s], _atoms([0], diag, 1, 2)))
    w_dn = w_dn_s.reshape(d_ff, D)
    f_out = _mm(f_act, w_dn, name="mm_down")
    dz2, dz2b, loss_acc, dg2, db2 = _ln_loss_bwd(x1, f_out, ln2_g, ln2_b, tgt)

    def pair_sums(arrs, from_sibling, names):
        return [_pair_sum(g, la, jc_arr, "pair_sum_" + n) for g, la, n in zip(arrs, from_sibling, names)]

    def shard_sums(parts, landed, names):
        return [_shard_sum(cp, lb, jc_arr, "shard_sum_" + n) for cp, lb, n in zip(parts, landed, names)]

    halves = {}
    g_down = _mm(f_act, dz2b, name="mm_d_w_down", ta=True, out_dtype=BF16)
    g1 = [g_down.reshape(N_SHARDS, d_ff // N_SHARDS, D)]
    d_f, sib1 = _mm(dz2b, w_dn, name="mm_d_f", tb=True, rider=_pair_rider(g1))
    part1 = pair_sums(g1, sib1, ["ffn_w_down"])
    (dup, dgp, d_fcw, d_fcb), landed1 = _ffn_bwd(up, gpre, fcw, ffn_conv_b, d_f,
                                                 rider=_shard_exchange_rider(part1, _atoms([0], near)))
    g_up, landed1 = _mm(x1b, dup, name="mm_d_w_up", ta=True, out_dtype=BF16, out_shards=N_SHARDS,
                        rider=_shard_exchange_rider(part1, _atoms([0], diag), landed1))
    halves["ffn_w_down"], = shard_sums(part1, landed1, ["ffn_w_down"])
    g_gate = _mm(x1b, dgp, name="mm_d_w_gate", ta=True, out_dtype=BF16, out_shards=N_SHARDS)
    g2 = [g_up, g_gate]
    dx1_a, sib2 = _mm(dup, w_up_s, name="mm_dx1_up", tb=True, b_shards=N_SHARDS, adds=((ALPHA, dz2),),
                      rider=_pair_rider(g2))
    part2 = pair_sums(g2, sib2, ["ffn_w_up", "ffn_w_gate"])
    dx1, landed2 = _mm(dgp, w_gate_s, name="mm_dx1_gate", tb=True, b_shards=N_SHARDS, adds=((1.0, dx1_a),),
                       rider=_shard_exchange_rider(part2, _atoms([0], near)))
    dz1, dz1b, dg1, db1 = _ln_bwd(dx1, xh1, rstd1, ln1_g)
    g_out = _mm(merged, dz1b, name="mm_d_w_out", ta=True, out_dtype=BF16)
    d_m = _mm(dz1b, w_o, name="mm_d_merged", tb=True)
    dya, dyr, dgl_a, dgl_r, dbg_a, dbg_r = _merge_bwd(proj, gl_off, b_gate, y_attn, y_rnn, d_m)
    g_ap = _mm(a_out, dya, name="mm_d_w_attn_proj", ta=True, out_dtype=BF16)
    g_rp = _mm(b_out, dyr, name="mm_d_w_rnn_proj", ta=True, out_dtype=BF16)
    names3 = ["w_out", "w_attn_proj", "w_rnn_proj"]
    g3 = [g_out.reshape(N_SHARDS, D // N_SHARDS, D), g_ap.reshape(N_SHARDS, d_attn // N_SHARDS, D),
          g_rp.reshape(N_SHARDS, d_rnn // N_SHARDS, D)]
    d_a = _mm(dya, w_ap, name="mm_d_attn", tb=True)
    d_b, sib3 = _mm(dyr, w_rp, name="mm_d_rnn", tb=True, rider=_pair_rider(g3))
    part3 = pair_sums(g3, sib3, names3)
    (dq, dk, dv, dsink), landed2 = _attn_bwd(
        proj, d_a, attn_sinks, nq, (q_off, k_off, v_off),
        rider=_shard_exchange_rider(part2, _atoms([0], diag) + _atoms([1], near), landed2))
    (drx, dry, d_rcw, d_rcb, d_ba, d_bi, d_lam, d_wa_g, d_wi_g), (landed2_gate, *landed3) = _rnn_bwd(
        proj, (rx_off, ry_off), h_all, d_b, rcw, rnn_conv_b, wa_g, wi_g, lru_ba, lru_bi, lru_lambda,
        rider=_join_riders(_shard_exchange_rider(part2[1:], _atoms([0], diag), landed2[1:]),
                           _shard_exchange_rider(part3, _atoms([0, 1, 2], near))))
    halves["ffn_w_up"], halves["ffn_w_gate"] = shard_sums(part2, [landed2[0], landed2_gate],
                                                          ["ffn_w_up", "ffn_w_gate"])
    d_proj = jnp.concatenate([dq, dk.astype(BF16), dv.astype(BF16), drx, dry, dgl_a, dgl_r], axis=1)
    ffn_names = ["ffn_w_down", "ffn_w_up", "ffn_w_gate"]
    g_in, (*shared_ffn, lb_o, lb_a, lb_r) = _mm(
        x0b, d_proj, name="mm_d_w_in", ta=True, out_dtype=BF16, out_shards=N_SHARDS,
        rider=_join_riders(_share_rider([halves[n] for n in ffn_names]),
                           _shard_exchange_rider(part3, _atoms([0, 1, 2], diag), landed3)))
    halves["w_out"], halves["w_attn_proj"], halves["w_rnn_proj"] = shard_sums(part3, [lb_o, lb_a, lb_r], names3)

    small_parts = [
        ("loss", loss_acc[0:1, 0:1]),
        ("b_gate", jnp.concatenate([dbg_a, dbg_r], axis=1)),
        ("rnn_conv_w", d_rcw), ("rnn_conv_b", d_rcb),
        ("lru_wa", _ungroup_blocks(d_wa_g, per)), ("lru_ba", d_ba),
        ("lru_wi", _ungroup_blocks(d_wi_g, per)), ("lru_bi", d_bi), ("lru_lambda", d_lam),
        ("attn_sinks", dsink[0:1, 0:nq]),
        ("ln1_g", dg1), ("ln1_b", db1),
        ("ffn_conv_w", d_fcw), ("ffn_conv_b", d_fcb),
        ("ln2_g", dg2), ("ln2_b", db2),
    ]
    packed = _pack([p for _, p in small_parts])
    rs = packed.shape[0]

    def whole(g):
        return g.reshape(2 * g.shape[1], g.shape[2])

    grads = {n: whole(g) for n, g in zip(ffn_names, shared_ffn)}
    out_g, out_d, out_m, out_v = {}, {}, {}, {}

    def adamw(n):
        shape = weights[n].shape
        two_d = (math.prod(shape[:-1]), shape[-1])
        g2 = grads[n].reshape(two_d)
        d2, m2, v2 = _adamw(weights[n].reshape(two_d), g2, m_in[n].reshape(two_d), v_in[n].reshape(two_d),
                            "adamw_" + n)
        out_g[n] = g2.reshape(shape)
        out_d[n], out_m[n], out_v[n] = d2.reshape(shape), m2.reshape(shape), v2.reshape(shape)

    g4 = [g_in, packed.reshape(N_SHARDS, rs // N_SHARDS, LANES)]
    sib4 = _run_rider(_pair_rider(g4), "pair_exchange_in_small")
    part4 = pair_sums(g4, sib4, ["w_in", "small"])
    grad_x, (lb_in, lb_small, *shared_mix) = _mm(
        d_proj, w_in_s, name="mm_d_x", tb=True, b_shards=N_SHARDS, adds=((ALPHA, dz1),),
        rider=_join_riders(_shard_exchange_rider(part4), _share_rider([halves[n] for n in names3])))
    grads.update({n: whole(g) for n, g in zip(names3, shared_mix)})
    halves["w_in"], = shard_sums(part4[:1], [lb_in], ["w_in"])
    eighths = _shard_sum(part4[1], lb_small, jc_arr, "shard_sum_small", all_slots=True)
    shared_in, reduced = _run_rider(_share_rider([halves["w_in"]], eighths), "share_in_small")
    grads["w_in"] = whole(shared_in)
    reduced = reduced.reshape(rs, LANES)
    small = dict(zip([n for n, _ in small_parts], _unpack(reduced, [p.shape for _, p in small_parts])))
    loss = small.pop("loss").reshape(())
    rcw_n = d_rnn // N_SHARDS
    fcw_n = d_ff // N_SHARDS
    small["rnn_conv_w"] = lax.dynamic_slice(small["rnn_conv_w"], (0, j_me * rcw_n), (4, rcw_n))
    small["ffn_conv_w"] = lax.dynamic_slice(small["ffn_conv_w"], (0, j_me * fcw_n), (3, fcw_n))
    for n, g in small.items():
        grads[n] = g

    for n in order:
        if n not in out_g:
            adamw(n)

    return (loss, grad_x.reshape(x.shape), *[out_g[n] for n in order], *[out_d[n] for n in order],
            *[out_m[n] for n in order], *[out_v[n] for n in order])
```

```python
import functools
import math

import jax
import jax.numpy as jnp
from jax import lax
from jax.experimental import pallas as pl
from jax.experimental.pallas import tpu as pltpu

F32 = jnp.float32
BF16 = jnp.bfloat16
MESH = pl.DeviceIdType.MESH

HEAD_DIM = 64
GROUP = 8
ATTN_BLOCK = 128
LRU_C = 8.0
LN_EPS = 1e-5
ALPHA = 2.0 ** 0.25
LANES = 128
N_SHARDS = 4
N_DEV = 8
VMEM_LIMIT = 56 * 1024 * 1024
MM_VMEM_BUDGET = 40 * 1024 * 1024
MM_MAX_TILE = 3072
PACK_ROW_MULT = 8 * 64
NEG = -1e30

ADAM_LR, ADAM_B1, ADAM_B2, ADAM_EPS, ADAM_WD, ADAM_STEP = 0.001, 0.9, 0.999, 1e-08, 0.01, 10

GELU_C = math.sqrt(2.0 / math.pi)
GELU_A = 0.044715


def _cparams(sem=None):
    kw = dict(vmem_limit_bytes=VMEM_LIMIT)
    if sem is not None:
        kw["dimension_semantics"] = sem
    return pltpu.CompilerParams(**kw)


def _pick(n, prefs):
    for p in prefs:
        if n % p == 0:
            return p
    return n


def _row_tile(rows, row_bytes, mult, budget=2 * 1024 * 1024):
    best = None
    for d in range(mult, rows + 1, mult):
        if rows % d == 0 and d * row_bytes <= budget:
            best = d
    return best if best is not None else rows


def _gelu(x):
    return 0.5 * x * (1.0 + jnp.tanh(GELU_C * (x + GELU_A * x * x * x)))


def _gelu_and_grad(x):
    t = jnp.tanh(GELU_C * (x + GELU_A * x * x * x))
    g = 0.5 * x * (1.0 + t)
    dg = 0.5 * (1.0 + t) + 0.5 * x * (1.0 - t * t) * GELU_C * (1.0 + 3.0 * GELU_A * x * x)
    return g, dg


def _shift_down(x, s, fill=0.0):
    row = lax.broadcasted_iota(jnp.int32, x.shape, 0)
    return jnp.where(row >= s, pltpu.roll(x, s, 0), fill)


def _shift_up(x, s, fill=0.0):
    n = x.shape[0]
    row = lax.broadcasted_iota(jnp.int32, x.shape, 0)
    return jnp.where(row < n - s, pltpu.roll(x, n - s, 0), fill)


def _mm(a, b, *, name, ta=False, tb=False, out_dtype=F32, adds=(), b_shards=1, out_shards=1,
        tm=None, tn=None, tk=None, rider=None):
    if ta:
        K, M = a.shape
    else:
        M, K = a.shape
    if b_shards > 1:
        n_sh = b.shape[-1]
        if tb:
            N = b.shape[1]
            assert b_shards * n_sh == K
        else:
            N = b_shards * n_sh
            assert b.shape[1] == K
    else:
        n_sh = None
        if tb:
            N = b.shape[0]
            assert b.shape[1] == K
        else:
            N = b.shape[1]
            assert b.shape[0] == K
    wide = (1024, 1536, 1280, 768, 640, 512, 256, 128)
    if tn is None:
        if b_shards > 1 and not tb:
            tn = n_sh if n_sh <= MM_MAX_TILE else _pick(n_sh, wide)
        elif out_shards > 1:
            tn = N // out_shards if N // out_shards <= MM_MAX_TILE else _pick(N // out_shards, wide)
        else:
            tn = _pick(N, wide)
    if tk is None:
        if b_shards > 1 and tb:
            tk = n_sh if n_sh <= MM_MAX_TILE else _pick(n_sh, wide)
        else:
            tk = K if K <= MM_MAX_TILE else _pick(K, (2048,) + wide)
    assert N % tn == 0 and K % tk == 0, (name, M, N, K, tn, tk)
    nk = K // tk
    n_add = len(adds)
    sa, sb, so = a.dtype.itemsize, b.dtype.itemsize, jnp.dtype(out_dtype).itemsize

    def vmem_bytes(tm_):
        return (2 * (tm_ * tk * sa + tk * tn * sb + tm_ * tn * so + n_add * tm_ * tn * 4)
                + (tm_ * tn * 4 if nk > 1 else 0))

    if tm is None:
        tm = _pick(M, (1024, 512, 256, 128)) if nk > 1 else _pick(M, (512, 256, 128))
        while vmem_bytes(tm) > MM_VMEM_BUDGET and tm % 256 == 0:
            tm //= 2
    assert M % tm == 0, (name, M, tm)
    b_outer = b.size * sb >= a.size * sa

    def ij(g0, g1):
        return (g1, g0) if b_outer else (g0, g1)

    def amap(g0, g1, k):
        i, _ = ij(g0, g1)
        return (k, i) if ta else (i, k)

    def bmap(g0, g1, k):
        _, j = ij(g0, g1)
        if b_shards > 1 and not tb:
            per = n_sh // tn
            return (j // per, k, j % per)
        if b_shards > 1 and tb:
            per = n_sh // tk
            return (k // per, j, k % per)
        return (j, k) if tb else (k, j)

    def omap(g0, g1, k):
        i, j = ij(g0, g1)
        if out_shards > 1:
            per_o = (N // out_shards) // tn
            return (j // per_o, i, j % per_o)
        return (i, j)

    a_spec = pl.BlockSpec((tk, tm) if ta else (tm, tk), amap)
    if b_shards > 1:
        b_spec = pl.BlockSpec((None, tn, tk) if tb else (None, tk, tn), bmap)
    else:
        b_spec = pl.BlockSpec((tn, tk) if tb else (tk, tn), bmap)
    add_specs = [pl.BlockSpec((tm, tn), lambda g0, g1, k: ij(g0, g1)) for _ in adds]
    if out_shards > 1:
        out_spec = pl.BlockSpec((None, tm, tn), omap)
        out_shape = jax.ShapeDtypeStruct((out_shards, M, N // out_shards), out_dtype)
    else:
        out_spec = pl.BlockSpec((tm, tn), omap)
        out_shape = jax.ShapeDtypeStruct((M, N), out_dtype)

    if ta:
        dims = (((0,), (0,)), ((), ()))
    elif tb:
        dims = (((1,), (1,)), ((), ()))
    else:
        dims = (((1,), (0,)), ((), ()))
    scales = tuple(s for s, _ in adds)

    def finish(r, add_refs, o_ref):
        for s, ref in zip(scales, add_refs):
            r = r + s * ref[...].astype(F32)
        o_ref[...] = r.astype(out_dtype)

    def body(a_ref, b_ref, *rest):
        add_refs = rest[:n_add]
        o_ref = rest[n_add]
        part = lax.dot_general(a_ref[...].astype(BF16), b_ref[...].astype(BF16), dims, preferred_element_type=F32)
        if nk == 1:
            finish(part, add_refs, o_ref)
            return
        acc = rest[n_add + 1]
        k = pl.program_id(2)

        @pl.when(k == 0)
        def _():
            acc[...] = part

        @pl.when(k > 0)
        def _():
            acc[...] += part

        @pl.when(k == nk - 1)
        def _():
            finish(acc[...], add_refs, o_ref)

    grid = (N // tn, M // tm, nk) if b_outer else (M // tm, N // tn, nk)
    (res,), carried = _call(
        body, name=name, grid=grid, in_specs=[a_spec, b_spec] + add_specs, out_specs=[out_spec],
        out_shape=[out_shape], scratch_shapes=[pltpu.VMEM((tm, tn), F32)] if nk > 1 else [],
        args=(a, b, *[x for _, x in adds]), sem=("parallel", "parallel", "arbitrary"), rider=rider)
    return (res, carried) if rider is not None else res


def _cast_bf16(w, name):
    R, C = w.shape
    tr = _row_tile(R, C * 4, 16)

    def body(w_ref, o_ref):
        o_ref[...] = w_ref[...].astype(BF16)

    return pl.pallas_call(
        body, name=name, out_shape=jax.ShapeDtypeStruct((R, C), BF16), grid=(R // tr,),
        in_specs=[pl.BlockSpec((tr, C), lambda r: (r, 0))], out_specs=pl.BlockSpec((tr, C), lambda r: (r, 0)),
        compiler_params=_cparams(("parallel",)),
    )(w)


def _cast_bf16_into_slot(w, jc_arr, name):
    R, C = w.shape
    tr = _row_tile(R, C * 4, 16)

    def body(jc_ref, w_ref, o_ref):
        o_ref[...] = w_ref[...].astype(BF16)

    gs = pltpu.PrefetchScalarGridSpec(
        num_scalar_prefetch=1, grid=(R // tr,),
        in_specs=[pl.BlockSpec((tr, C), lambda r, jc: (r, 0))],
        out_specs=pl.BlockSpec((None, tr, C), lambda r, jc: (jc[0], r, 0)))
    return pl.pallas_call(body, name=name, out_shape=jax.ShapeDtypeStruct((N_SHARDS, R, C), BF16), grid_spec=gs,
                          compiler_params=_cparams(("parallel",)))(jc_arr, w)


def _pair_sum(g, la, jc_arr, name):
    S, R, C = g.shape
    half = R // 2
    tr = _row_tile(half, C * 4, 16)
    nrt = half // tr
    dt = g.dtype

    def body(jc_ref, g_ref, la_ref, o_ref):
        o_ref[...] = (g_ref[...].astype(F32) + la_ref[...].astype(F32)).astype(dt)

    gs = pltpu.PrefetchScalarGridSpec(
        num_scalar_prefetch=1, grid=(S, nrt),
        in_specs=[pl.BlockSpec((None, tr, C), lambda s, r, jc: (s, jc[1] * nrt + r, 0)),
                  pl.BlockSpec((None, tr, C), lambda s, r, jc: (s, r, 0))],
        out_specs=pl.BlockSpec((None, tr, C), lambda s, r, jc: (s, r, 0)))
    return pl.pallas_call(body, name=name, out_shape=jax.ShapeDtypeStruct((S, half, C), dt), grid_spec=gs,
                          compiler_params=_cparams(("parallel", "parallel")))(jc_arr, g, la)


def _shard_sum(cp, lb, jc_arr, name, all_slots=False):
    S, h, C = cp.shape
    tr = _row_tile(h, C * 4, 16)

    def body(jc_ref, cp_ref, l0, l1, l2, o_ref):
        o_ref[...] = ((cp_ref[...].astype(F32) + l0[...].astype(F32)) + l1[...].astype(F32)) + l2[...].astype(F32)

    def lspec(kk):
        return pl.BlockSpec((None, tr, C), lambda r, jc: (kk, r, 0))

    if all_slots:
        out_spec = pl.BlockSpec((None, None, tr, C), lambda r, jc: (jc[0], jc[1], r, 0))
        out_shape = jax.ShapeDtypeStruct((S, 2, h, C), F32)
    else:
        out_spec = pl.BlockSpec((None, tr, C), lambda r, jc: (jc[1], r, 0))
        out_shape = jax.ShapeDtypeStruct((2, h, C), F32)
    gs = pltpu.PrefetchScalarGridSpec(
        num_scalar_prefetch=1, grid=(h // tr,),
        in_specs=[pl.BlockSpec((None, tr, C), lambda r, jc: (jc[0], r, 0)), lspec(0), lspec(1), lspec(2)],
        out_specs=out_spec)
    return pl.pallas_call(body, name=name, out_shape=out_shape, grid_spec=gs,
                          compiler_params=_cparams(("parallel",)))(jc_arr, cp, lb, lb, lb)


ANY = pl.BlockSpec(memory_space=pl.ANY)


def _place():
    x, y, c = lax.axis_index("x"), lax.axis_index("y"), lax.axis_index("c")
    chips = [(1 - x, y), (x, 1 - y), (1 - x, 1 - y)]
    return x, y, c, chips


class _Rider:
    def __init__(self, inputs, out_shape, aliases, sems, start, finish):
        self.inputs, self.out_shape, self.aliases, self.sems = list(inputs), list(out_shape), dict(aliases), list(sems)
        self.start, self.finish = start, finish


def _join_riders(r1, r2):
    i1, o1, s1 = len(r1.inputs), len(r1.out_shape), len(r1.sems)
    aliases = dict(r1.aliases)
    aliases.update({i1 + i: o1 + o for i, o in r2.aliases.items()})

    def start(ins, outs, sems):
        r1.start(ins[:i1], outs[:o1], sems[:s1])
        r2.start(ins[i1:], outs[o1:], sems[s1:])

    def finish(ins, outs, sems):
        r1.finish(ins[:i1], outs[:o1], sems[:s1])
        r2.finish(ins[i1:], outs[o1:], sems[s1:])

    return _Rider(r1.inputs + r2.inputs, r1.out_shape + r2.out_shape, aliases, r1.sems + r2.sems, start, finish)


def _call(body, *, name, grid, in_specs, out_specs, out_shape, scratch_shapes, args, sem, rider=None):
    out_specs, out_shape = tuple(out_specs), tuple(out_shape)
    if rider is None:
        res = pl.pallas_call(body, name=name, out_shape=out_shape, grid=grid, in_specs=list(in_specs),
                             out_specs=out_specs, scratch_shapes=list(scratch_shapes),
                             compiler_params=_cparams(sem))(*args)
        return tuple(res), []
    n_in, n_out, n_sc = len(in_specs), len(out_specs), len(scratch_shapes)
    r_in, r_out = len(rider.inputs), len(rider.out_shape)

    def wrapped(*refs):
        p = 0
        host_in = refs[p:p + n_in]; p += n_in
        rid_in = refs[p:p + r_in]; p += r_in
        host_out = refs[p:p + n_out]; p += n_out
        rid_out = refs[p:p + r_out]; p += r_out
        host_sc = refs[p:p + n_sc]; p += n_sc
        rid_sem = refs[p:]
        first = functools.reduce(jnp.logical_and, [pl.program_id(a) == 0 for a in range(len(grid))])
        last = functools.reduce(jnp.logical_and, [pl.program_id(a) == grid[a] - 1 for a in range(len(grid))])

        @pl.when(first)
        def _():
            rider.start(rid_in, rid_out, rid_sem)

        body(*host_in, *host_out, *host_sc)

        @pl.when(last)
        def _():
            rider.finish(rid_in, rid_out, rid_sem)

    res = pl.pallas_call(
        wrapped, name=name, out_shape=out_shape + tuple(rider.out_shape), grid=grid,
        in_specs=list(in_specs) + [ANY] * r_in, out_specs=out_specs + (ANY,) * r_out,
        input_output_aliases={n_in + i: n_out + o for i, o in rider.aliases.items()},
        scratch_shapes=list(scratch_shapes) + rider.sems,
        compiler_params=_cparams(("arbitrary",) * len(grid)),
    )(*args, *rider.inputs)
    return tuple(res[:n_out]), list(res[n_out:])


def _run_rider(rider, name):
    def body(*refs):
        r_in, r_out = len(rider.inputs), len(rider.out_shape)
        ins, outs, sems = refs[:r_in], refs[r_in:r_in + r_out], refs[r_in + r_out:]
        rider.start(ins, outs, sems)
        rider.finish(ins, outs, sems)

    return pl.pallas_call(
        body, name=name, out_shape=rider.out_shape, in_specs=[ANY] * len(rider.inputs),
        out_specs=[ANY] * len(rider.out_shape), input_output_aliases=rider.aliases, scratch_shapes=rider.sems,
    )(*rider.inputs)


def _atoms(indices, kks=(0, 1, 2), q=0, nq=1):
    return [(i, kk, q, nq) for i in indices for kk in kks]


def _gather_rider(bufs, atoms=None):
    n = len(bufs)
    if atoms is None:
        atoms = _atoms(range(n))
    na = len(atoms)

    def rows(out, atom, core):
        i, _, q, nq = atom
        half = out[i].shape[1] // 2
        assert half % (16 * nq) == 0, (half, nq)
        return pl.ds(core * half + q * (half // nq), half // nq)

    def ici_copy(out, sems, a, slot, peer):
        c = lax.axis_index("c")
        blk = out[atoms[a][0]].at[slot, rows(out, atoms[a], c), :]
        return pltpu.make_async_remote_copy(
            src_ref=blk, dst_ref=blk, send_sem=sems[0].at[a], recv_sem=sems[1].at[a],
            device_id=(peer[0], peer[1], c), device_id_type=MESH)

    def d2d_copy(out, sems, a, slot, from_core):
        x, y, c, _ = _place()
        blk = out[atoms[a][0]].at[slot, rows(out, atoms[a], from_core), :]
        return pltpu.make_async_remote_copy(
            src_ref=blk, dst_ref=blk, send_sem=sems[2].at[a], recv_sem=sems[3].at[a],
            device_id=(x, y, 1 - c), device_id_type=MESH)

    def start(ins, out, sems):
        x, y, c, chips = _place()
        for a in range(na):
            ici_copy(out, sems, a, 2 * x + y, chips[atoms[a][1]]).start()

    def finish(ins, out, sems):
        x, y, c, chips = _place()
        src = [2 * chips[atoms[a][1]][0] + chips[atoms[a][1]][1] for a in range(na)]
        for a in range(na):
            ici_copy(out, sems, a, src[a], chips[atoms[a][1]]).wait_recv()
            d2d_copy(out, sems, a, src[a], c).start()
        for a in range(na):
            d2d_copy(out, sems, a, src[a], 1 - c).wait_recv()
        for a in range(na):
            ici_copy(out, sems, a, 2 * x + y, chips[atoms[a][1]]).wait_send()
            d2d_copy(out, sems, a, src[a], c).wait_send()

    return _Rider(bufs, [jax.ShapeDtypeStruct(s.shape, s.dtype) for s in bufs], {i: i for i in range(n)},
                  [pltpu.SemaphoreType.DMA((na,))] * 4, start, finish)


def _mm_gathering(a, buf, order_arr, name):
    M, K = a.shape
    S, _, n = buf.shape
    tm = _pick(M, (512, 256, 128))
    n_i = M // tm
    half = K // 2

    def body(order_ref, a_ref, w_in_ref, o_ref, w_ref, b_vmem, load_sem, s_ici, r_ici, s_d2d, r_d2d):
        s, i = pl.program_id(0), pl.program_id(1)
        x, y, c, chips = _place()
        j_me = 2 * x + y
        slots = [2 * px + py for px, py in chips]

        def ici(kk, slot):
            blk = w_ref.at[slot, pl.ds(c * half, half), :]
            return pltpu.make_async_remote_copy(
                src_ref=blk, dst_ref=blk, send_sem=s_ici.at[kk], recv_sem=r_ici.at[kk],
                device_id=(chips[kk][0], chips[kk][1], c), device_id_type=MESH)

        def d2d(kk, from_core):
            blk = w_ref.at[slots[kk], pl.ds(from_core * half, half), :]
            return pltpu.make_async_remote_copy(
                src_ref=blk, dst_ref=blk, send_sem=s_d2d.at[kk], recv_sem=r_d2d.at[kk],
                device_id=(x, y, 1 - c), device_id_type=MESH)

        def load(slot, b):
            return pltpu.make_async_copy(w_ref.at[slot], b_vmem.at[b], load_sem.at[b])

        @pl.when(jnp.logical_and(s == 0, i == 0))
        def _():
            for kk in range(3):
                ici(kk, j_me).start()
            load(j_me, 0).start()

        @pl.when(i == 0)
        def _():
            load(order_ref[s], s % 2).wait()

        o_ref[...] = jnp.dot(a_ref[...], b_vmem[s % 2], preferred_element_type=F32)

        last = i == n_i - 1

        @pl.when(jnp.logical_and(last, s == 0))
        def _():
            ici(0, slots[0]).wait_recv()
            d2d(0, c).start()
            ici(1, slots[1]).wait_recv()
            d2d(1, c).start()
            d2d(0, 1 - c).wait_recv()
            load(slots[0], 1).start()

        @pl.when(jnp.logical_and(last, s == 1))
        def _():
            d2d(1, 1 - c).wait_recv()
            load(slots[1], 0).start()

        @pl.when(jnp.logical_and(last, s == 2))
        def _():
            ici(2, slots[2]).wait_recv()
            d2d(2, c).start()
            d2d(2, 1 - c).wait_recv()
            load(slots[2], 1).start()

        @pl.when(jnp.logical_and(last, s == 3))
        def _():
            for kk in range(3):
                ici(kk, j_me).wait_send()
                d2d(kk, c).wait_send()

    gs = pltpu.PrefetchScalarGridSpec(
        num_scalar_prefetch=1, grid=(S, n_i),
        in_specs=[pl.BlockSpec((tm, K), lambda s, i, order: (i, 0)), ANY],
        out_specs=[pl.BlockSpec((tm, n), lambda s, i, order: (i, order[s])), ANY],
        scratch_shapes=[pltpu.VMEM((2, K, n), BF16), pltpu.SemaphoreType.DMA((2,))]
        + [pltpu.SemaphoreType.DMA((3,))] * 4)
    return pl.pallas_call(
        body, name=name, grid_spec=gs,
        out_shape=[jax.ShapeDtypeStruct((M, S * n), F32), jax.ShapeDtypeStruct(buf.shape, buf.dtype)],
        input_output_aliases={2: 1}, compiler_params=_cparams(("arbitrary", "arbitrary")),
    )(order_arr, a, buf)


def _all_gather_small(shards):
    n = len(shards)

    def body(*refs):
        w = refs[:n]
        out = refs[n:2 * n]
        local_sem, s_sem, r_sem = refs[2 * n:]
        x, y, c, chips = _place()
        j_me = 2 * x + y
        cps = []
        for i in range(n):
            lc = pltpu.make_async_copy(w[i], out[i].at[j_me], local_sem.at[i])
            lc.start()
            cps.append(lc)
        sends = []
        for i in range(n):
            for kk, (px, py) in enumerate(chips):
                cp = pltpu.make_async_remote_copy(
                    src_ref=w[i], dst_ref=out[i].at[j_me], send_sem=s_sem.at[3 * i + kk],
                    recv_sem=r_sem.at[3 * i + kk], device_id=(px, py, c), device_id_type=MESH)
                cp.start()
                sends.append(cp)
        for i in range(n):
            for kk, (px, py) in enumerate(chips):
                sends[3 * i + kk].wait_send()
                pltpu.make_async_remote_copy(
                    src_ref=w[i], dst_ref=out[i].at[2 * px + py], send_sem=s_sem.at[3 * i + kk],
                    recv_sem=r_sem.at[3 * i + kk], device_id=(px, py, c), device_id_type=MESH).wait_recv()
        for lc in cps:
            lc.wait()

    out_shape = [jax.ShapeDtypeStruct((N_SHARDS,) + s.shape, s.dtype) for s in shards]
    return pl.pallas_call(
        body, name="all_gather_conv_weights", out_shape=out_shape, in_specs=[ANY] * n, out_specs=[ANY] * n,
        scratch_shapes=[pltpu.SemaphoreType.DMA((n,)), pltpu.SemaphoreType.DMA((3 * n,)),
                        pltpu.SemaphoreType.DMA((3 * n,))],
    )(*shards)


def _pair_rider(grads):
    n = len(grads)

    def copies(g, la, sems):
        x, y, c, _ = _place()
        return [pltpu.make_async_remote_copy(
            src_ref=g[i].at[:, pl.ds((1 - c) * (g[i].shape[1] // 2), g[i].shape[1] // 2), :], dst_ref=la[i],
            send_sem=sems[0].at[i], recv_sem=sems[1].at[i], device_id=(x, y, 1 - c), device_id_type=MESH)
            for i in range(n)]

    def start(g, la, sems):
        for cp in copies(g, la, sems):
            cp.start()

    def finish(g, la, sems):
        for cp in copies(g, la, sems):
            cp.wait()

    return _Rider(grads, [jax.ShapeDtypeStruct((s.shape[0], s.shape[1] // 2, s.shape[2]), s.dtype) for s in grads],
                  {}, [pltpu.SemaphoreType.DMA((n,)), pltpu.SemaphoreType.DMA((n,))], start, finish)


def _shard_exchange_rider(cps_in, atoms=None, landing=None):
    n = len(cps_in)
    if atoms is None:
        atoms = _atoms(range(n))

    def copies(ins, lb, sems):
        x, y, c, chips = _place()
        out = []
        for a, (i, kk, q, nq) in enumerate(atoms):
            h = ins[i].shape[1]
            assert h % (16 * nq) == 0, (h, nq)
            rows = pl.ds(q * (h // nq), h // nq)
            px, py = chips[kk]
            out.append(pltpu.make_async_remote_copy(
                src_ref=ins[i].at[2 * px + py, rows, :], dst_ref=lb[i].at[kk, rows, :],
                send_sem=sems[0].at[a], recv_sem=sems[1].at[a], device_id=(px, py, c), device_id_type=MESH))
        return out

    def start(ins, lb, sems):
        for cp in copies(ins, lb, sems):
            cp.start()

    def finish(ins, lb, sems):
        for cp in copies(ins, lb, sems):
            cp.wait()

    shapes = [jax.ShapeDtypeStruct((3,) + s.shape[1:], s.dtype) for s in cps_in]
    sems = [pltpu.SemaphoreType.DMA((len(atoms),)), pltpu.SemaphoreType.DMA((len(atoms),))]
    if landing is None:
        return _Rider(cps_in, shapes, {}, sems, start, finish)
    return _Rider(list(cps_in) + list(landing), shapes, {n + i: i for i in range(n)}, sems, start, finish)


def _share_rider(halves, eighths=None):
    n = len(halves)
    bufs = list(halves) + ([eighths] if eighths is not None else [])

    def half_copy(out, sems, i, core):
        x, y, c, _ = _place()
        blk = out[i].at[core]
        return pltpu.make_async_remote_copy(src_ref=blk, dst_ref=blk, send_sem=sems[0].at[i], recv_sem=sems[1].at[i],
                                            device_id=(x, y, 1 - c), device_id_type=MESH)

    def eighth_copy(out, sems, r, mine):
        x, y, c, _ = _place()
        px, py, pc = x ^ ((r >> 2) & 1), y ^ ((r >> 1) & 1), c ^ (r & 1)
        blk = out[n].at[2 * x + y, c] if mine else out[n].at[2 * px + py, pc]
        return pltpu.make_async_remote_copy(src_ref=blk, dst_ref=blk, send_sem=sems[2].at[r - 1],
                                            recv_sem=sems[3].at[r - 1], device_id=(px, py, pc), device_id_type=MESH)

    def start(ins, out, sems):
        c = lax.axis_index("c")
        for i in range(n):
            half_copy(out, sems, i, c).start()
        if eighths is not None:
            for r in range(1, N_DEV):
                eighth_copy(out, sems, r, True).start()

    def finish(ins, out, sems):
        c = lax.axis_index("c")
        for i in range(n):
            half_copy(out, sems, i, 1 - c).wait_recv()
        if eighths is not None:
            for r in range(1, N_DEV):
                eighth_copy(out, sems, r, False).wait_recv()
        for i in range(n):
            half_copy(out, sems, i, c).wait_send()
        if eighths is not None:
            for r in range(1, N_DEV):
                eighth_copy(out, sems, r, True).wait_send()

    return _Rider(bufs, [jax.ShapeDtypeStruct(s.shape, s.dtype) for s in bufs], {i: i for i in range(len(bufs))},
                  [pltpu.SemaphoreType.DMA((max(n, 1),)), pltpu.SemaphoreType.DMA((max(n, 1),)),
                   pltpu.SemaphoreType.DMA((N_DEV - 1,)), pltpu.SemaphoreType.DMA((N_DEV - 1,))], start, finish)


ATTN_ROWS = GROUP * ATTN_BLOCK
ATTN_KEYS = 2 * ATTN_BLOCK


def _attn_geometry(n):
    row = lax.broadcasted_iota(jnp.int32, (ATTN_ROWS, ATTN_KEYS), 0)
    col = lax.broadcasted_iota(jnp.int32, (ATTN_ROWS, ATTN_KEYS), 1)
    dist = ATTN_BLOCK + jnp.bitwise_and(row, ATTN_BLOCK - 1) - col
    valid = jnp.logical_and(jnp.logical_and(dist >= 0, dist < ATTN_BLOCK),
                            jnp.logical_or(col >= ATTN_BLOCK, n > 0))
    return dist.astype(F32), valid


def _per_head_column(values):
    head = lax.broadcasted_iota(jnp.int32, (ATTN_ROWS, 1), 0) // ATTN_BLOCK
    col = jnp.zeros((ATTN_ROWS, 1), F32)
    for hh, v in enumerate(values):
        col = jnp.where(head == hh, v, col)
    return col


def _stack_heads(ref, g):
    return jnp.concatenate(
        [ref[:, (g * GROUP + hh) * HEAD_DIM:(g * GROUP + hh + 1) * HEAD_DIM].astype(BF16) for hh in range(GROUP)],
        axis=0)


def _attn_probs(q_s, k2, slope_col, sink_col, dist, valid):
    s = lax.dot_general(q_s, k2, (((1,), (1,)), ((), ())), preferred_element_type=F32) * (HEAD_DIM ** -0.5)
    s = jnp.where(valid, s - slope_col * dist, NEG)
    m = jnp.maximum(jnp.max(s, axis=1, keepdims=True), sink_col)
    e = jnp.exp(s - m)
    es = jnp.exp(sink_col - m)
    inv = 1.0 / (jnp.sum(e, axis=1, keepdims=True) + es)
    return e * inv, es * inv


def _attn_specs(T, d_attn, d_kv, q_blk, k_blk, v_blk):
    bq = pl.BlockSpec((ATTN_BLOCK, d_attn), lambda n: (n, q_blk))
    kp = pl.BlockSpec((ATTN_BLOCK, d_kv), lambda n: (jnp.maximum(n - 1, 0), k_blk))
    kc = pl.BlockSpec((ATTN_BLOCK, d_kv), lambda n: (n, k_blk))
    vp = pl.BlockSpec((ATTN_BLOCK, d_kv), lambda n: (jnp.maximum(n - 1, 0), v_blk))
    vc = pl.BlockSpec((ATTN_BLOCK, d_kv), lambda n: (n, v_blk))
    return bq, kp, kc, vp, vc


def _attn_fwd(proj, sinks, nq, cols, rider=None):
    T = proj.shape[0]
    nkv = nq // GROUP
    d_attn, d_kv = nq * HEAD_DIM, nkv * HEAD_DIM
    q_off, k_off, v_off = cols
    bq, kp, kc, vp, vc = _attn_specs(T, d_attn, d_kv, q_off // d_attn, k_off // d_kv, v_off // d_kv)

    def body(sink_ref, q_ref, kp_ref, kc_ref, vp_ref, vc_ref, o_ref):
        n = pl.program_id(0)
        dist, valid = _attn_geometry(n)
        for g in range(nkv):
            ks = slice(g * HEAD_DIM, (g + 1) * HEAD_DIM)
            k2 = jnp.concatenate([kp_ref[:, ks], kc_ref[:, ks]], axis=0).astype(BF16)
            v2 = jnp.concatenate([vp_ref[:, ks], vc_ref[:, ks]], axis=0).astype(BF16)
            slope_col = _per_head_column([2.0 ** (-8.0 * (g * GROUP + hh + 1) / nq) for hh in range(GROUP)])
            sink_col = _per_head_column([sink_ref[0, g * GROUP + hh] for hh in range(GROUP)])
            p, _ = _attn_probs(_stack_heads(q_ref, g), k2, slope_col, sink_col, dist, valid)
            o = jnp.dot(p.astype(BF16), v2, preferred_element_type=F32).astype(BF16)
            for hh in range(GROUP):
                h = g * GROUP + hh
                o_ref[:, h * HEAD_DIM:(h + 1) * HEAD_DIM] = o[hh * ATTN_BLOCK:(hh + 1) * ATTN_BLOCK, :]

    (out,), carried = _call(
        body, name="attn_fwd", out_shape=[jax.ShapeDtypeStruct((T, d_attn), BF16)], grid=(T // ATTN_BLOCK,),
        in_specs=[pl.BlockSpec(memory_space=pltpu.SMEM), bq, kp, kc, vp, vc],
        out_specs=[pl.BlockSpec((ATTN_BLOCK, d_attn), lambda n: (n, 0))], scratch_shapes=[],
        args=(sinks, proj, proj, proj, proj, proj), sem=("parallel",), rider=rider)
    return out, carried


def _attn_bwd(proj, d_attn_out, sinks, nq, cols, rider=None):
    T = proj.shape[0]
    nkv = nq // GROUP
    d_attn, d_kv = nq * HEAD_DIM, nkv * HEAD_DIM
    q_off, k_off, v_off = cols
    bq, kp, kc, vp, vc = _attn_specs(T, d_attn, d_kv, q_off // d_attn, k_off // d_kv, v_off // d_kv)
    scale = HEAD_DIM ** -0.5
    dn_t = (((1,), (1,)), ((), ()))
    dn_r = (((0,), (0,)), ((), ()))

    def body(sink_ref, q_ref, kp_ref, kc_ref, vp_ref, vc_ref, do_ref, dq_ref, dk_ref, dv_ref, ds_ref):
        n = pl.program_id(0)

        @pl.when(n == 0)
        def _():
            dk_ref[...] = jnp.zeros_like(dk_ref)
            dv_ref[...] = jnp.zeros_like(dv_ref)
            ds_ref[...] = jnp.zeros_like(ds_ref)

        dist, valid = _attn_geometry(n)
        rows_c = pl.ds(pl.multiple_of(n * ATTN_BLOCK, ATTN_BLOCK), ATTN_BLOCK)
        rows_p = pl.ds(pl.multiple_of(jnp.maximum(n - 1, 0) * ATTN_BLOCK, ATTN_BLOCK), ATTN_BLOCK)
        lane = lax.broadcasted_iota(jnp.int32, ds_ref.shape, 1)
        srow = lax.broadcasted_iota(jnp.int32, ds_ref.shape, 0)
        ds_acc = jnp.zeros(ds_ref.shape, F32)
        for g in range(nkv):
            ks = slice(g * HEAD_DIM, (g + 1) * HEAD_DIM)
            k2 = jnp.concatenate([kp_ref[:, ks], kc_ref[:, ks]], axis=0).astype(BF16)
            v2 = jnp.concatenate([vp_ref[:, ks], vc_ref[:, ks]], axis=0).astype(BF16)
            slope_col = _per_head_column([2.0 ** (-8.0 * (g * GROUP + hh + 1) / nq) for hh in range(GROUP)])
            sink_col = _per_head_column([sink_ref[0, g * GROUP + hh] for hh in range(GROUP)])
            q_s = _stack_heads(q_ref, g)
            do_s = _stack_heads(do_ref, g)
            p, p_sink = _attn_probs(q_s, k2, slope_col, sink_col, dist, valid)
            dp = lax.dot_general(do_s, v2, dn_t, preferred_element_type=F32)
            delta = jnp.sum(p * dp, axis=1, keepdims=True)
            ds = (p * (dp - delta)).astype(BF16)
            sink_part = p_sink * delta
            dq = (jnp.dot(ds, k2, preferred_element_type=F32) * scale).astype(BF16)
            for hh in range(GROUP):
                h = g * GROUP + hh
                blk = slice(hh * ATTN_BLOCK, (hh + 1) * ATTN_BLOCK)
                dq_ref[:, h * HEAD_DIM:(h + 1) * HEAD_DIM] = dq[blk, :]
                ds_acc = ds_acc + jnp.where(jnp.logical_and(lane == h, srow == 0), -jnp.sum(sink_part[blk, :]), 0.0)
            dk2 = lax.dot_general(ds, q_s, dn_r, preferred_element_type=F32) * scale
            dv2 = lax.dot_general(p.astype(BF16), do_s, dn_r, preferred_element_type=F32)
            dk_ref[rows_p, ks] += dk2[:ATTN_BLOCK, :]
            dv_ref[rows_p, ks] += dv2[:ATTN_BLOCK, :]
            dk_ref[rows_c, ks] += dk2[ATTN_BLOCK:, :]
            dv_ref[rows_c, ks] += dv2[ATTN_BLOCK:, :]
        ds_ref[...] += ds_acc

    out_shape = (jax.ShapeDtypeStruct((T, d_attn), BF16), jax.ShapeDtypeStruct((T, d_kv), F32),
                 jax.ShapeDtypeStruct((T, d_kv), F32), jax.ShapeDtypeStruct((8, LANES), F32))
    return _call(
        body, name="attn_bwd", out_shape=out_shape, grid=(T // ATTN_BLOCK,),
        in_specs=[pl.BlockSpec(memory_space=pltpu.SMEM), bq, kp, kc, vp, vc,
                  pl.BlockSpec((ATTN_BLOCK, d_attn), lambda n: (n, 0))],
        out_specs=(pl.BlockSpec((ATTN_BLOCK, d_attn), lambda n: (n, 0)),
                   pl.BlockSpec((T, d_kv), lambda n: (0, 0)), pl.BlockSpec((T, d_kv), lambda n: (0, 0)),
                   pl.BlockSpec((8, LANES), lambda n: (0, 0))),
        scratch_shapes=[], args=(sinks, proj, proj, proj, proj, proj, d_attn_out), sem=("arbitrary",), rider=rider)


def _rnn_tile(T):
    return _pick(T, (256, 128))


def _rnn_gates(x_ext, cw_ref, cb_ref, wa_ref, wi_ref, ba_ref, bi_ref, lam_ref, tt):
    xs = [pltpu.roll(x_ext, 3 - k, 0)[8:, :] if k < 3 else x_ext[8:, :] for k in range(4)]
    cx = cb_ref[...] + xs[0] * cw_ref[0:1, :]
    for k in range(1, 4):
        cx = cx + xs[k] * cw_ref[k:k + 1, :]
    cxb = cx.astype(BF16)
    r = jax.nn.sigmoid(jnp.dot(cxb, wa_ref[...], preferred_element_type=F32) + ba_ref[...])
    i = jax.nn.sigmoid(jnp.dot(cxb, wi_ref[...], preferred_element_type=F32) + bi_ref[...])
    lam = lam_ref[...]
    sp = jnp.maximum(-lam, 0.0) + jnp.log1p(jnp.exp(-jnp.abs(lam)))
    log_a = -LRU_C * r * sp
    a = jnp.exp(log_a)
    z = 2.0 * log_a
    em1 = jnp.where(z > -1e-2, z * (1.0 + z * (0.5 + z * (1.0 / 6.0 + z * (1.0 / 24.0)))), jnp.exp(z) - 1.0)
    s = jnp.sqrt(-em1)
    return xs, cx, r, i, sp, a, s


def _rnn_specs(T, gw, tt, rx_blk, ry_blk, rev):
    nT = T // tt
    hb = tt // 8

    def tile(t):
        return (nT - 1 - t) if rev else t

    rx = pl.BlockSpec((tt, gw), lambda g, t: (tile(t), rx_blk + g))
    rx_halo = pl.BlockSpec((8, gw), lambda g, t: (jnp.maximum(tile(t) * hb - 1, 0), rx_blk + g))
    ry = pl.BlockSpec((tt, gw), lambda g, t: (tile(t), ry_blk + g))
    cw = pl.BlockSpec((4, gw), lambda g, t: (0, g))
    vec = pl.BlockSpec((1, gw), lambda g, t: (0, g))
    wg = pl.BlockSpec((None, gw, gw), lambda g, t: (g, 0, 0))
    act = pl.BlockSpec((tt, gw), lambda g, t: (tile(t), g))
    act_halo = pl.BlockSpec((8, gw), lambda g, t: (jnp.maximum(tile(t) * hb - 1, 0), g))
    return rx, rx_halo, ry, cw, vec, wg, act, act_halo, tile


def _rnn_fwd(proj, cols, conv_w, conv_b, wa_g, wi_g, ba, bi, lam, rider=None):
    T = proj.shape[0]
    G, gw, _ = wa_g.shape
    d_rnn = G * gw
    tt = _rnn_tile(T)
    rx_off, ry_off = cols
    rx, rx_halo, ry, cw, vec, wg, act, _, _ = _rnn_specs(T, gw, tt, rx_off // gw, ry_off // gw, False)

    def body(rx_ref, rxh_ref, ry_ref, cw_ref, cb_ref, wa_ref, wi_ref, ba_ref, bi_ref, lam_ref,
             b_ref, h_ref, carry):
        t = pl.program_id(1)

        @pl.when(t == 0)
        def _():
            carry[...] = jnp.zeros_like(carry)

        halo = jnp.where(t > 0, rxh_ref[...], 0.0)
        x_ext = jnp.concatenate([halo, rx_ref[...]], axis=0)
        _, cx, _, i, _, a, s = _rnn_gates(x_ext, cw_ref, cb_ref, wa_ref, wi_ref, ba_ref, bi_ref, lam_ref, tt)
        acc_a, acc_b = a, s * (i * cx)
        d = 1
        while d < tt:
            acc_b = acc_a * _shift_down(acc_b, d, 0.0) + acc_b
            acc_a = acc_a * _shift_down(acc_a, d, 1.0)
            d *= 2
        h = acc_b + acc_a * carry[7:8, :]
        carry[...] = h[tt - 8:, :]
        h_ref[...] = h
        b_ref[...] = (h * _gelu(ry_ref[...])).astype(BF16)

    return _call(
        body, name="rnn_fwd",
        out_shape=(jax.ShapeDtypeStruct((T, d_rnn), BF16), jax.ShapeDtypeStruct((T, d_rnn), F32)),
        grid=(G, T // tt),
        in_specs=[rx, rx_halo, ry, cw, vec, wg, wg, vec, vec, vec], out_specs=(act, act),
        scratch_shapes=[pltpu.VMEM((8, gw), F32)],
        args=(proj, proj, proj, conv_w, conv_b, wa_g, wi_g, ba, bi, lam), sem=("parallel", "arbitrary"), rider=rider)


def _rnn_bwd(proj, cols, h_all, d_b, conv_w, conv_b, wa_g, wi_g, ba, bi, lam, rider=None):
    T = proj.shape[0]
    G, gw, _ = wa_g.shape
    d_rnn = G * gw
    tt = _rnn_tile(T)
    nT = T // tt
    rx_off, ry_off = cols
    rx, rx_halo, ry, cw, vec, wg, act, act_halo, _ = _rnn_specs(T, gw, tt, rx_off // gw, ry_off // gw, True)
    dn_t = (((1,), (1,)), ((), ()))
    dn_r = (((0,), (0,)), ((), ()))

    def body(rx_ref, rxh_ref, ry_ref, h_ref, hh_ref, db_ref, cw_ref, cb_ref, wa_ref, wi_ref, ba_ref, bi_ref, lam_ref,
             drx_ref, dry_ref, dcw_ref, dcb_ref, dba_ref, dbi_ref, dlam_ref, dwa_ref, dwi_ref,
             lam_carry, dcx_carry):
        t = pl.program_id(1)
        first_tile = t == nT - 1

        @pl.when(t == 0)
        def _():
            lam_carry[...] = jnp.zeros_like(lam_carry)
            dcx_carry[...] = jnp.zeros_like(dcx_carry)
            dcw_ref[...] = jnp.zeros_like(dcw_ref)
            dcb_ref[...] = jnp.zeros_like(dcb_ref)
            dba_ref[...] = jnp.zeros_like(dba_ref)
            dbi_ref[...] = jnp.zeros_like(dbi_ref)
            dlam_ref[...] = jnp.zeros_like(dlam_ref)
            dwa_ref[...] = jnp.zeros_like(dwa_ref)
            dwi_ref[...] = jnp.zeros_like(dwi_ref)

        halo = jnp.where(first_tile, 0.0, rxh_ref[...])
        x_ext = jnp.concatenate([halo, rx_ref[...]], axis=0)
        xs, cx, r, i, sp, a, s = _rnn_gates(x_ext, cw_ref, cb_ref, wa_ref, wi_ref, ba_ref, bi_ref, lam_ref, tt)
        h = h_ref[...]
        h_halo = jnp.where(first_tile, 0.0, hh_ref[...])
        h_prev = pltpu.roll(jnp.concatenate([h_halo, h], axis=0), 1, 0)[8:, :]
        gel, dgel = _gelu_and_grad(ry_ref[...])
        d_b_t = db_ref[...]
        dry_ref[...] = (d_b_t * h * dgel).astype(BF16)
        dh = d_b_t * gel

        acc_c = _shift_up(a, 1, 1.0)
        acc_l = dh
        d = 1
        while d < tt:
            acc_l = acc_c * _shift_up(acc_l, d, 0.0) + acc_l
            acc_c = acc_c * _shift_up(acc_c, d, 1.0)
            d *= 2
        lam_t = acc_l + acc_c * lam_carry[0:1, :]
        lam_carry[...] = (a * lam_t)[0:8, :]

        icx = i * cx
        d_s = lam_t * icx
        d_i = lam_t * s * cx
        dcx = lam_t * s * i
        d_a = lam_t * h_prev - d_s * (a / s)
        dlog_a = d_a * a
        d_r = dlog_a * (-LRU_C * sp)
        lam = lam_ref[...]
        dlam_ref[...] += jnp.sum(dlog_a * r, axis=0, keepdims=True) * (LRU_C * jax.nn.sigmoid(-lam))
        dpr = d_r * r * (1.0 - r)
        dpi = d_i * i * (1.0 - i)
        dba_ref[...] += jnp.sum(dpr, axis=0, keepdims=True)
        dbi_ref[...] += jnp.sum(dpi, axis=0, keepdims=True)
        cxb = cx.astype(BF16)
        dprb, dpib = dpr.astype(BF16), dpi.astype(BF16)
        dwa_ref[...] += lax.dot_general(cxb, dprb, dn_r, preferred_element_type=F32)
        dwi_ref[...] += lax.dot_general(cxb, dpib, dn_r, preferred_element_type=F32)
        dcx = (dcx + lax.dot_general(dprb, wa_ref[...], dn_t, preferred_element_type=F32)
               + lax.dot_general(dpib, wi_ref[...], dn_t, preferred_element_type=F32))

        dcb_ref[...] += jnp.sum(dcx, axis=0, keepdims=True)
        for k in range(4):
            dcw_ref[k:k + 1, :] += jnp.sum(dcx * xs[k], axis=0, keepdims=True)
        d_ext = jnp.concatenate([dcx, dcx_carry[...]], axis=0)
        drx = dcx * cw_ref[3:4, :]
        for k in range(3):
            drx = drx + pltpu.roll(d_ext, tt + 8 - (3 - k), 0)[:tt, :] * cw_ref[k:k + 1, :]
        drx_ref[...] = drx.astype(BF16)
        dcx_carry[...] = dcx[0:8, :]

    out_shape = (jax.ShapeDtypeStruct((T, d_rnn), BF16), jax.ShapeDtypeStruct((T, d_rnn), BF16),
                 jax.ShapeDtypeStruct((4, d_rnn), F32), jax.ShapeDtypeStruct((1, d_rnn), F32),
                 jax.ShapeDtypeStruct((1, d_rnn), F32), jax.ShapeDtypeStruct((1, d_rnn), F32),
                 jax.ShapeDtypeStruct((1, d_rnn), F32), jax.ShapeDtypeStruct((G, gw, gw), F32),
                 jax.ShapeDtypeStruct((G, gw, gw), F32))
    return _call(
        body, name="rnn_bwd", out_shape=out_shape, grid=(G, nT),
        in_specs=[rx, rx_halo, ry, act, act_halo, act, cw, vec, wg, wg, vec, vec, vec],
        out_specs=(act, act, cw, vec, vec, vec, vec, wg, wg),
        scratch_shapes=[pltpu.VMEM((8, gw), F32), pltpu.VMEM((8, gw), F32)],
        args=(proj, proj, proj, h_all, h_all, d_b, conv_w, conv_b, wa_g, wi_g, ba, bi, lam),
        sem=("parallel", "arbitrary"), rider=rider)


def _merge_fwd(proj, gl_off, b_gate, y_attn, y_rnn, rider=None):
    T, D = y_attn.shape
    tm = _pick(T, (256, 128))
    ct = _pick(math.gcd(gl_off, D), (512, 256, 128))
    oa, orr, nd = gl_off // ct, (gl_off + D) // ct, D // ct

    def body(ga_ref, gr_ref, ba_ref, br_ref, ya_ref, yr_ref, m_ref):
        ga = jax.nn.sigmoid(ga_ref[...] + ba_ref[...])
        gr = jax.nn.sigmoid(gr_ref[...] + br_ref[...])
        m_ref[...] = (ga * ya_ref[...] + gr * yr_ref[...]).astype(BF16)

    blk = pl.BlockSpec((tm, ct), lambda i, j: (i, j))
    (merged,), carried = _call(
        body, name="merge_fwd", out_shape=[jax.ShapeDtypeStruct((T, D), BF16)], grid=(T // tm, nd),
        in_specs=[pl.BlockSpec((tm, ct), lambda i, j: (i, oa + j)), pl.BlockSpec((tm, ct), lambda i, j: (i, orr + j)),
                  pl.BlockSpec((1, ct), lambda i, j: (0, j)), pl.BlockSpec((1, ct), lambda i, j: (0, nd + j)),
                  blk, blk],
        out_specs=[blk], scratch_shapes=[], args=(proj, proj, b_gate, b_gate, y_attn, y_rnn),
        sem=("parallel", "parallel"), rider=rider)
    return merged, carried


def _merge_bwd(proj, gl_off, b_gate, y_attn, y_rnn, d_m):
    T, D = y_attn.shape
    tm = _pick(T, (256, 128))
    ct = _pick(math.gcd(gl_off, D), (512, 256, 128))
    oa, orr, nd = gl_off // ct, (gl_off + D) // ct, D // ct

    def body(ga_ref, gr_ref, ba_ref, br_ref, ya_ref, yr_ref, dm_ref,
             dya_ref, dyr_ref, dga_ref, dgr_ref, dba_ref, dbr_ref):
        i = pl.program_id(1)

        @pl.when(i == 0)
        def _():
            dba_ref[...] = jnp.zeros_like(dba_ref)
            dbr_ref[...] = jnp.zeros_like(dbr_ref)

        ga = jax.nn.sigmoid(ga_ref[...] + ba_ref[...])
        gr = jax.nn.sigmoid(gr_ref[...] + br_ref[...])
        dm = dm_ref[...]
        dya_ref[...] = (dm * ga).astype(BF16)
        dyr_ref[...] = (dm * gr).astype(BF16)
        dga = dm * ya_ref[...] * ga * (1.0 - ga)
        dgr = dm * yr_ref[...] * gr * (1.0 - gr)
        dga_ref[...] = dga.astype(BF16)
        dgr_ref[...] = dgr.astype(BF16)
        dba_ref[...] += jnp.sum(dga, axis=0, keepdims=True)
        dbr_ref[...] += jnp.sum(dgr, axis=0, keepdims=True)

    blk = pl.BlockSpec((tm, ct), lambda j, i: (i, j))
    vec = pl.BlockSpec((1, ct), lambda j, i: (0, j))
    act = jax.ShapeDtypeStruct((T, D), BF16)
    v1 = jax.ShapeDtypeStruct((1, D), F32)
    return pl.pallas_call(
        body, name="merge_bwd", out_shape=(act, act, act, act, v1, v1), grid=(nd, T // tm),
        in_specs=[pl.BlockSpec((tm, ct), lambda j, i: (i, oa + j)), pl.BlockSpec((tm, ct), lambda j, i: (i, orr + j)),
                  vec, pl.BlockSpec((1, ct), lambda j, i: (0, nd + j)), blk, blk, blk],
        out_specs=(blk, blk, blk, blk, vec, vec),
        compiler_params=_cparams(("parallel", "arbitrary")),
    )(proj, proj, b_gate, b_gate, y_attn, y_rnn, d_m)


def _ln_fwd(x_res, delta, g, b, name, rider=None):
    T, D = x_res.shape
    tm = _pick(T, (256, 128))

    def body(x_ref, d_ref, g_ref, b_ref, y_ref, yb_ref, xh_ref, rs_ref):
        z = ALPHA * x_ref[...] + d_ref[...]
        mu = jnp.mean(z, axis=1, keepdims=True)
        zc = z - mu
        var = jnp.mean(zc * zc, axis=1, keepdims=True)
        rstd = lax.rsqrt(var + LN_EPS)
        xh = zc * rstd
        xh_ref[...] = xh
        rs_ref[...] = rstd
        y = xh * g_ref[...] + b_ref[...]
        y_ref[...] = y
        yb_ref[...] = y.astype(BF16)

    row = pl.BlockSpec((tm, D), lambda i: (i, 0))
    vec = pl.BlockSpec((1, D), lambda i: (0, 0))
    return _call(
        body, name=name,
        out_shape=(jax.ShapeDtypeStruct((T, D), F32), jax.ShapeDtypeStruct((T, D), BF16),
                   jax.ShapeDtypeStruct((T, D), F32), jax.ShapeDtypeStruct((T, 1), F32)),
        grid=(T // tm,), in_specs=[row, row, vec, vec],
        out_specs=(row, row, row, pl.BlockSpec((tm, 1), lambda i: (i, 0))),
        scratch_shapes=[], args=(x_res, delta, g, b), sem=("parallel",), rider=rider)


def _ln_bwd_rows(dy, xh, rstd, g):
    dxh = dy * g
    m1 = jnp.mean(dxh, axis=1, keepdims=True)
    m2 = jnp.mean(dxh * xh, axis=1, keepdims=True)
    return rstd * (dxh - m1 - xh * m2)


def _ln_loss_bwd(x_res, delta, g, b, target):
    T, D = x_res.shape
    tm = _pick(T, (256, 128))

    def body(x_ref, d_ref, g_ref, b_ref, t_ref, dz_ref, dzb_ref, loss_ref, dg_ref, db_ref):
        i = pl.program_id(0)

        @pl.when(i == 0)
        def _():
            loss_ref[...] = jnp.zeros_like(loss_ref)
            dg_ref[...] = jnp.zeros_like(dg_ref)
            db_ref[...] = jnp.zeros_like(db_ref)

        z = ALPHA * x_ref[...] + d_ref[...]
        mu = jnp.mean(z, axis=1, keepdims=True)
        zc = z - mu
        var = jnp.mean(zc * zc, axis=1, keepdims=True)
        rstd = lax.rsqrt(var + LN_EPS)
        xh = zc * rstd
        gv = g_ref[...]
        err = xh * gv + b_ref[...] - t_ref[...]
        loss_ref[...] += 0.5 * jnp.sum(jnp.mean(err * err, axis=1, keepdims=True))
        dy = err * (1.0 / D)
        dg_ref[...] += jnp.sum(dy * xh, axis=0, keepdims=True)
        db_ref[...] += jnp.sum(dy, axis=0, keepdims=True)
        dz = _ln_bwd_rows(dy, xh, rstd, gv)
        dz_ref[...] = dz
        dzb_ref[...] = dz.astype(BF16)

    row = pl.BlockSpec((tm, D), lambda i: (i, 0))
    vec = pl.BlockSpec((1, D), lambda i: (0, 0))
    return pl.pallas_call(
        body, name="ln2_loss_bwd",
        out_shape=(jax.ShapeDtypeStruct((T, D), F32), jax.ShapeDtypeStruct((T, D), BF16),
                   jax.ShapeDtypeStruct((8, LANES), F32),
                   jax.ShapeDtypeStruct((1, D), F32), jax.ShapeDtypeStruct((1, D), F32)),
        grid=(T // tm,), in_specs=[row, row, vec, vec, row],
        out_specs=(row, row, pl.BlockSpec((8, LANES), lambda i: (0, 0)), vec, vec),
        compiler_params=_cparams(("arbitrary",)),
    )(x_res, delta, g, b, target)


def _ln_bwd(dy, xh, rstd, g):
    T, D = dy.shape
    tm = _pick(T, (256, 128))

    def body(dy_ref, xh_ref, rs_ref, g_ref, dz_ref, dzb_ref, dg_ref, db_ref):
        i = pl.program_id(0)

        @pl.when(i == 0)
        def _():
            dg_ref[...] = jnp.zeros_like(dg_ref)
            db_ref[...] = jnp.zeros_like(db_ref)

        dyv, xhv = dy_ref[...], xh_ref[...]
        dg_ref[...] += jnp.sum(dyv * xhv, axis=0, keepdims=True)
        db_ref[...] += jnp.sum(dyv, axis=0, keepdims=True)
        dz = _ln_bwd_rows(dyv, xhv, rs_ref[...], g_ref[...])
        dz_ref[...] = dz
        dzb_ref[...] = dz.astype(BF16)

    row = pl.BlockSpec((tm, D), lambda i: (i, 0))
    vec = pl.BlockSpec((1, D), lambda i: (0, 0))
    return pl.pallas_call(
        body, name="ln1_bwd",
        out_shape=(jax.ShapeDtypeStruct((T, D), F32), jax.ShapeDtypeStruct((T, D), BF16),
                   jax.ShapeDtypeStruct((1, D), F32), jax.ShapeDtypeStruct((1, D), F32)),
        grid=(T // tm,), in_specs=[row, row, pl.BlockSpec((tm, 1), lambda i: (i, 0)), vec],
        out_specs=(row, row, vec, vec), compiler_params=_cparams(("arbitrary",)),
    )(dy, xh, rstd, g)


def _ffn_col_tile(T, d_ff):
    return _pick(d_ff, (256, 128)) if T >= 1024 else _pick(d_ff, (512, 256, 128))


def _ffn_gate(gp, cw_ref, cb_ref):
    return (cb_ref[...] + gp * cw_ref[2:3, :] + _shift_down(gp, 1) * cw_ref[1:2, :]
            + _shift_down(gp, 2) * cw_ref[0:1, :])


def _ffn_fwd(up, gpre, conv_w, conv_b, rider=None):
    T, d_ff = up.shape
    ct = _ffn_col_tile(T, d_ff)

    def body(up_ref, gp_ref, cw_ref, cb_ref, f_ref):
        gate = _ffn_gate(gp_ref[...], cw_ref, cb_ref)
        f_ref[...] = (_gelu(gate) * up_ref[...]).astype(BF16)

    col = pl.BlockSpec((T, ct), lambda j: (0, j))
    (f,), carried = _call(
        body, name="ffn_act_fwd", out_shape=[jax.ShapeDtypeStruct((T, d_ff), BF16)], grid=(d_ff // ct,),
        in_specs=[col, col, pl.BlockSpec((3, ct), lambda j: (0, j)), pl.BlockSpec((1, ct), lambda j: (0, j))],
        out_specs=[col], scratch_shapes=[], args=(up, gpre, conv_w, conv_b), sem=("parallel",), rider=rider)
    return f, carried


def _ffn_bwd(up, gpre, conv_w, conv_b, d_f, rider=None):
    T, d_ff = up.shape
    ct = _ffn_col_tile(T, d_ff)

    def body(up_ref, gp_ref, cw_ref, cb_ref, df_ref, dup_ref, dgp_ref, dcw_ref, dcb_ref):
        gp = gp_ref[...]
        gate = _ffn_gate(gp, cw_ref, cb_ref)
        gel, dgel = _gelu_and_grad(gate)
        df = df_ref[...]
        dup_ref[...] = (df * gel).astype(BF16)
        dgate = df * up_ref[...] * dgel
        dcb_ref[...] = jnp.sum(dgate, axis=0, keepdims=True)
        dcw_ref[2:3, :] = jnp.sum(dgate * gp, axis=0, keepdims=True)
        dcw_ref[1:2, :] = jnp.sum(dgate * _shift_down(gp, 1), axis=0, keepdims=True)
        dcw_ref[0:1, :] = jnp.sum(dgate * _shift_down(gp, 2), axis=0, keepdims=True)
        dgp = (dgate * cw_ref[2:3, :] + _shift_up(dgate, 1) * cw_ref[1:2, :]
               + _shift_up(dgate, 2) * cw_ref[0:1, :])
        dgp_ref[...] = dgp.astype(BF16)

    col = pl.BlockSpec((T, ct), lambda j: (0, j))
    w3 = pl.BlockSpec((3, ct), lambda j: (0, j))
    v1 = pl.BlockSpec((1, ct), lambda j: (0, j))
    return _call(
        body, name="ffn_act_bwd",
        out_shape=(jax.ShapeDtypeStruct((T, d_ff), BF16), jax.ShapeDtypeStruct((T, d_ff), BF16),
                   jax.ShapeDtypeStruct((3, d_ff), F32), jax.ShapeDtypeStruct((1, d_ff), F32)),
        grid=(d_ff // ct,), in_specs=[col, col, w3, v1, col], out_specs=(col, col, w3, v1),
        scratch_shapes=[], args=(up, gpre, conv_w, conv_b, d_f), sem=("parallel",), rider=rider)


def _adamw(w, g, m, v, name):
    R, C = w.shape
    tr = _row_tile(R, C * 4, 8, budget=1536 * 1024)
    c1 = 1.0 / (1.0 - ADAM_B1 ** ADAM_STEP)
    c2 = 1.0 / (1.0 - ADAM_B2 ** ADAM_STEP)

    def body(w_ref, g_ref, m_ref, v_ref, d_ref, nm_ref, nv_ref):
        gv = g_ref[...]
        nm = ADAM_B1 * m_ref[...] + (1.0 - ADAM_B1) * gv
        nv = ADAM_B2 * v_ref[...] + (1.0 - ADAM_B2) * (gv * gv)
        nm_ref[...] = nm
        nv_ref[...] = nv
        d_ref[...] = -ADAM_LR * ((nm * c1) / (jnp.sqrt(nv * c2) + ADAM_EPS) + ADAM_WD * w_ref[...])

    blk = pl.BlockSpec((tr, C), lambda r: (r, 0))
    sh = jax.ShapeDtypeStruct((R, C), F32)
    return _call(body, name=name, out_shape=(sh, sh, sh), grid=(R // tr,), in_specs=[blk] * 4, out_specs=(blk,) * 3,
                 scratch_shapes=[], args=(w, g, m, v), sem=("parallel",))[0]


def _group_blocks(w_blocks, per):
    nb, bw, _ = w_blocks.shape
    G = nb // per
    w4 = w_blocks.reshape(G, per, bw, bw)
    rows = []
    for p in range(per):
        parts = [w4[:, p] if q == p else jnp.zeros((G, bw, bw), w_blocks.dtype) for q in range(per)]
        rows.append(jnp.concatenate(parts, axis=2))
    return jnp.concatenate(rows, axis=1)


def _ungroup_blocks(w_groups, per):
    G, gw, _ = w_groups.shape
    bw = gw // per
    blocks = [w_groups[:, p * bw:(p + 1) * bw, p * bw:(p + 1) * bw] for p in range(per)]
    return jnp.stack(blocks, axis=1).reshape(G * per, bw, bw)


def _pack(parts):
    flat = jnp.concatenate([p.reshape(-1).astype(F32) for p in parts])
    n = flat.shape[0]
    rows = -(-n // LANES)
    rows = -(-rows // PACK_ROW_MULT) * PACK_ROW_MULT
    flat = jnp.pad(flat, (0, rows * LANES - n))
    return flat.reshape(rows, LANES)


def _unpack(packed, shapes):
    flat = packed.reshape(-1)
    out, off = [], 0
    for s in shapes:
        n = math.prod(s)
        out.append(flat[off:off + n].reshape(s))
        off += n
    return out


def kernel(x, w_in, b_gate, rnn_conv_w, rnn_conv_b, lru_wa, lru_ba, lru_wi, lru_bi, lru_lambda, attn_sinks, w_attn_proj, w_rnn_proj, w_out, ln1_g, ln1_b, ffn_w_up, ffn_w_gate, ffn_conv_w, ffn_conv_b, ffn_w_down, ln2_g, ln2_b, loss_target, m_w_in, m_b_gate, m_rnn_conv_w, m_rnn_conv_b, m_lru_wa, m_lru_ba, m_lru_wi, m_lru_bi, m_lru_lambda, m_attn_sinks, m_w_attn_proj, m_w_rnn_proj, m_w_out, m_ln1_g, m_ln1_b, m_ffn_w_up, m_ffn_w_gate, m_ffn_conv_w, m_ffn_conv_b, m_ffn_w_down, m_ln2_g, m_ln2_b, v_w_in, v_b_gate, v_rnn_conv_w, v_rnn_conv_b, v_lru_wa, v_lru_ba, v_lru_wi, v_lru_bi, v_lru_lambda, v_attn_sinks, v_w_attn_proj, v_w_rnn_proj, v_w_out, v_ln1_g, v_ln1_b, v_ffn_w_up, v_ffn_w_gate, v_ffn_conv_w, v_ffn_conv_b, v_ffn_w_down, v_ln2_g, v_ln2_b):
    weights = dict(w_in=w_in, b_gate=b_gate, rnn_conv_w=rnn_conv_w, rnn_conv_b=rnn_conv_b, lru_wa=lru_wa,
                   lru_ba=lru_ba, lru_wi=lru_wi, lru_bi=lru_bi, lru_lambda=lru_lambda, attn_sinks=attn_sinks,
                   w_attn_proj=w_attn_proj, w_rnn_proj=w_rnn_proj, w_out=w_out, ln1_g=ln1_g, ln1_b=ln1_b,
                   ffn_w_up=ffn_w_up, ffn_w_gate=ffn_w_gate, ffn_conv_w=ffn_conv_w, ffn_conv_b=ffn_conv_b,
                   ffn_w_down=ffn_w_down, ln2_g=ln2_g, ln2_b=ln2_b)
    m_in = dict(w_in=m_w_in, b_gate=m_b_gate, rnn_conv_w=m_rnn_conv_w, rnn_conv_b=m_rnn_conv_b, lru_wa=m_lru_wa,
                lru_ba=m_lru_ba, lru_wi=m_lru_wi, lru_bi=m_lru_bi, lru_lambda=m_lru_lambda, attn_sinks=m_attn_sinks,
                w_attn_proj=m_w_attn_proj, w_rnn_proj=m_w_rnn_proj, w_out=m_w_out, ln1_g=m_ln1_g, ln1_b=m_ln1_b,
                ffn_w_up=m_ffn_w_up, ffn_w_gate=m_ffn_w_gate, ffn_conv_w=m_ffn_conv_w, ffn_conv_b=m_ffn_conv_b,
                ffn_w_down=m_ffn_w_down, ln2_g=m_ln2_g, ln2_b=m_ln2_b)
    v_in = dict(w_in=v_w_in, b_gate=v_b_gate, rnn_conv_w=v_rnn_conv_w, rnn_conv_b=v_rnn_conv_b, lru_wa=v_lru_wa,
                lru_ba=v_lru_ba, lru_wi=v_lru_wi, lru_bi=v_lru_bi, lru_lambda=v_lru_lambda, attn_sinks=v_attn_sinks,
                w_attn_proj=v_w_attn_proj, w_rnn_proj=v_w_rnn_proj, w_out=v_w_out, ln1_g=v_ln1_g, ln1_b=v_ln1_b,
                ffn_w_up=v_ffn_w_up, ffn_w_gate=v_ffn_w_gate, ffn_conv_w=v_ffn_conv_w, ffn_conv_b=v_ffn_conv_b,
                ffn_w_down=v_ffn_w_down, ln2_g=v_ln2_g, ln2_b=v_ln2_b)
    order = list(weights)

    assert x.shape[0] == 1 and w_in.shape[0] == 1, "one sequence per device, depth 1"
    T, D = x.shape[1], x.shape[2]
    nq = attn_sinks.shape[-1]
    nkv = nq // GROUP
    d_attn, d_kv = nq * HEAD_DIM, nkv * HEAD_DIM
    d_rnn = rnn_conv_b.shape[-1]
    d_ff = ffn_conv_b.shape[-1]
    n_blocks, bw = lru_wa.shape[1], lru_wa.shape[2]
    per = (bw * LANES // math.gcd(bw, LANES)) // bw
    gw = per * bw
    assert n_blocks % per == 0 and d_rnn == n_blocks * bw
    q_off, k_off, v_off = 0, d_attn, d_attn + d_kv
    rx_off = d_attn + 2 * d_kv
    ry_off = rx_off + d_rnn
    gl_off = ry_off + d_rnn
    d_in = gl_off + 2 * D
    assert w_in.shape[-1] * N_SHARDS == d_in
    assert k_off % d_kv == 0 and rx_off % gw == 0 and T % ATTN_BLOCK == 0

    xi, yi, ci = lax.axis_index("x"), lax.axis_index("y"), lax.axis_index("c")
    j_me = 2 * xi + yi
    jc_arr = jnp.stack([j_me, ci]).astype(jnp.int32)

    x0 = x[0]
    x0b = _cast_bf16(x0, "cast_x")
    tgt = loss_target[0]
    big = ["w_in", "w_attn_proj", "w_rnn_proj", "w_out", "ffn_w_up", "ffn_w_gate", "ffn_w_down"]
    own = {n: _cast_bf16_into_slot(weights[n][0], jc_arr, "cast_" + n) for n in big}
    order_arr = jnp.stack([j_me, j_me ^ 2, j_me ^ 1, j_me ^ 3]).astype(jnp.int32)

    rcw_s, fcw_s = _all_gather_small([rnn_conv_w[0], ffn_conv_w[0]])
    rcw = jnp.concatenate([rcw_s[j] for j in range(N_SHARDS)], axis=1)
    fcw = jnp.concatenate([fcw_s[j] for j in range(N_SHARDS)], axis=1)

    wa_g = _group_blocks(lru_wa[0], per).astype(BF16)
    wi_g = _group_blocks(lru_wi[0], per).astype(BF16)

    near, diag = (0, 1), (2,)
    proj, w_in_s = _mm_gathering(x0b, own["w_in"], order_arr, "mm_proj")
    a_out, (w_ap_s, w_rp_s) = _attn_fwd(
        proj, attn_sinks, nq, (q_off, k_off, v_off),
        rider=_gather_rider([own["w_attn_proj"], own["w_rnn_proj"]], _atoms([0]) + _atoms([1], near)))
    (b_out, h_all), (w_rp_s, w_o_s, w_up_s) = _rnn_fwd(
        proj, (rx_off, ry_off), rcw, rnn_conv_b, wa_g, wi_g, lru_ba, lru_bi, lru_lambda,
        rider=_gather_rider([w_rp_s, own["w_out"], own["ffn_w_up"]],
                            _atoms([0], diag) + _atoms([1]) + _atoms([2], near, 0, 2)))
    w_ap, w_rp, w_o = w_ap_s.reshape(d_attn, D), w_rp_s.reshape(d_rnn, D), w_o_s.reshape(D, D)
    y_attn, (w_up_s,) = _mm(a_out, w_ap, name="mm_attn_proj", rider=_gather_rider([w_up_s], _atoms([0], near, 1, 2)))
    y_rnn, (w_up_s,) = _mm(b_out, w_rp, name="mm_rnn_proj", rider=_gather_rider([w_up_s], _atoms([0], diag, 0, 2)))
    merged, (w_gate_s,) = _merge_fwd(proj, gl_off, b_gate, y_attn, y_rnn,
                                     rider=_gather_rider([own["ffn_w_gate"]], _atoms([0], near, 0, 2)))
    mix, (w_up_s,) = _mm(merged, w_o, name="mm_out", rider=_gather_rider([w_up_s], _atoms([0], diag, 1, 2)))
    (x1, x1b, xh1, rstd1), (w_gate_s,) = _ln_fwd(x0, mix, ln1_g, ln1_b, "ln1_fwd",
                                                 rider=_gather_rider([w_gate_s], _atoms([0], near, 1, 2)))
    up, (w_gate_s,) = _mm(x1b, w_up_s, name="mm_up", b_shards=N_SHARDS,
                          rider=_gather_rider([w_gate_s], _atoms([0], diag)))
    gpre, (w_dn_s,) = _mm(x1b, w_gate_s, name="mm_gate", b_shards=N_SHARDS,
                          rider=_gather_rider([own["ffn_w_down"]], _atoms([0], near)))
    f_act, (w_dn_s,) = _ffn_fwd(up, gpre, fcw, ffn_conv_b,
                                rider=_gather_rider([w_dn_s], _atoms([0], diag)))
    w_dn = w_dn_s.reshape(d_ff, D)
    f_out = _mm(f_act, w_dn, name="mm_down")
    dz2, dz2b, loss_acc, dg2, db2 = _ln_loss_bwd(x1, f_out, ln2_g, ln2_b, tgt)

    def pair_sums(arrs, from_sibling, names):
        return [_pair_sum(g, la, jc_arr, "pair_sum_" + n) for g, la, n in zip(arrs, from_sibling, names)]

    def shard_sums(parts, landed, names):
        return [_shard_sum(cp, lb, jc_arr, "shard_sum_" + n) for cp, lb, n in zip(parts, landed, names)]

    halves = {}
    g_down = _mm(f_act, dz2b, name="mm_d_w_down", ta=True, out_dtype=BF16)
    g1 = [g_down.reshape(N_SHARDS, d_ff // N_SHARDS, D)]
    d_f, sib1 = _mm(dz2b, w_dn, name="mm_d_f", tb=True, rider=_pair_rider(g1))
    part1 = pair_sums(g1, sib1, ["ffn_w_down"])
    (dup, dgp, d_fcw, d_fcb), landed1 = _ffn_bwd(up, gpre, fcw, ffn_conv_b, d_f,
                                                 rider=_shard_exchange_rider(part1, _atoms([0], near)))
    g_up, landed1 = _mm(x1b, dup, name="mm_d_w_up", ta=True, out_dtype=BF16, out_shards=N_SHARDS,
                        rider=_shard_exchange_rider(part1, _atoms([0], diag), landed1))
    halves["ffn_w_down"], = shard_sums(part1, landed1, ["ffn_w_down"])
    g_gate = _mm(x1b, dgp, name="mm_d_w_gate", ta=True, out_dtype=BF16, out_shards=N_SHARDS)
    g2 = [g_up, g_gate]
    dx1_a, sib2 = _mm(dup, w_up_s, name="mm_dx1_up", tb=True, b_shards=N_SHARDS, adds=((ALPHA, dz2),),
                      rider=_pair_rider(g2))
    part2 = pair_sums(g2, sib2, ["ffn_w_up", "ffn_w_gate"])
    dx1, landed2 = _mm(dgp, w_gate_s, name="mm_dx1_gate", tb=True, b_shards=N_SHARDS, adds=((1.0, dx1_a),),
                       rider=_shard_exchange_rider(part2, _atoms([0], near)))
    dz1, dz1b, dg1, db1 = _ln_bwd(dx1, xh1, rstd1, ln1_g)
    g_out = _mm(merged, dz1b, name="mm_d_w_out", ta=True, out_dtype=BF16)
    d_m = _mm(dz1b, w_o, name="mm_d_merged", tb=True)
    dya, dyr, dgl_a, dgl_r, dbg_a, dbg_r = _merge_bwd(proj, gl_off, b_gate, y_attn, y_rnn, d_m)
    g_ap = _mm(a_out, dya, name="mm_d_w_attn_proj", ta=True, out_dtype=BF16)
    g_rp = _mm(b_out, dyr, name="mm_d_w_rnn_proj", ta=True, out_dtype=BF16)
    names3 = ["w_out", "w_attn_proj", "w_rnn_proj"]
    g3 = [g_out.reshape(N_SHARDS, D // N_SHARDS, D), g_ap.reshape(N_SHARDS, d_attn // N_SHARDS, D),
          g_rp.reshape(N_SHARDS, d_rnn // N_SHARDS, D)]
    d_a = _mm(dya, w_ap, name="mm_d_attn", tb=True)
    d_b, sib3 = _mm(dyr, w_rp, name="mm_d_rnn", tb=True, rider=_pair_rider(g3))
    part3 = pair_sums(g3, sib3, names3)
    (dq, dk, dv, dsink), landed2 = _attn_bwd(
        proj, d_a, attn_sinks, nq, (q_off, k_off, v_off),
        rider=_shard_exchange_rider(part2, _atoms([0], diag) + _atoms([1], near), landed2))
    (drx, dry, d_rcw, d_rcb, d_ba, d_bi, d_lam, d_wa_g, d_wi_g), (landed2_gate, *landed3) = _rnn_bwd(
        proj, (rx_off, ry_off), h_all, d_b, rcw, rnn_conv_b, wa_g, wi_g, lru_ba, lru_bi, lru_lambda,
        rider=_join_riders(_shard_exchange_rider(part2[1:], _atoms([0], diag), landed2[1:]),
                           _shard_exchange_rider(part3, _atoms([0, 1, 2], near))))
    halves["ffn_w_up"], halves["ffn_w_gate"] = shard_sums(part2, [landed2[0], landed2_gate],
                                                          ["ffn_w_up", "ffn_w_gate"])
    d_proj = jnp.concatenate([dq, dk.astype(BF16), dv.astype(BF16), drx, dry, dgl_a, dgl_r], axis=1)
    ffn_names = ["ffn_w_down", "ffn_w_up", "ffn_w_gate"]
    g_in, (*shared_ffn, lb_o, lb_a, lb_r) = _mm(
        x0b, d_proj, name="mm_d_w_in", ta=True, out_dtype=BF16, out_shards=N_SHARDS,
        rider=_join_riders(_share_rider([halves[n] for n in ffn_names]),
                           _shard_exchange_rider(part3, _atoms([0, 1, 2], diag), landed3)))
    halves["w_out"], halves["w_attn_proj"], halves["w_rnn_proj"] = shard_sums(part3, [lb_o, lb_a, lb_r], names3)

    small_parts = [
        ("loss", loss_acc[0:1, 0:1]),
        ("b_gate", jnp.concatenate([dbg_a, dbg_r], axis=1)),
        ("rnn_conv_w", d_rcw), ("rnn_conv_b", d_rcb),
        ("lru_wa", _ungroup_blocks(d_wa_g, per)), ("lru_ba", d_ba),
        ("lru_wi", _ungroup_blocks(d_wi_g, per)), ("lru_bi", d_bi), ("lru_lambda", d_lam),
        ("attn_sinks", dsink[0:1, 0:nq]),
        ("ln1_g", dg1), ("ln1_b", db1),
        ("ffn_conv_w", d_fcw), ("ffn_conv_b", d_fcb),
        ("ln2_g", dg2), ("ln2_b", db2),
    ]
    packed = _pack([p for _, p in small_parts])
    rs = packed.shape[0]

    def whole(g):
        return g.reshape(2 * g.shape[1], g.shape[2])

    grads = {n: whole(g) for n, g in zip(ffn_names, shared_ffn)}
    out_g, out_d, out_m, out_v = {}, {}, {}, {}

    def adamw(n):
        shape = weights[n].shape
        two_d = (math.prod(shape[:-1]), shape[-1])
        g2 = grads[n].reshape(two_d)
        d2, m2, v2 = _adamw(weights[n].reshape(two_d), g2, m_in[n].reshape(two_d), v_in[n].reshape(two_d),
                            "adamw_" + n)
        out_g[n] = g2.reshape(shape)
        out_d[n], out_m[n], out_v[n] = d2.reshape(shape), m2.reshape(shape), v2.reshape(shape)

    g4 = [g_in, packed.reshape(N_SHARDS, rs // N_SHARDS, LANES)]
    sib4 = _run_rider(_pair_rider(g4), "pair_exchange_in_small")
    part4 = pair_sums(g4, sib4, ["w_in", "small"])
    grad_x, (lb_in, lb_small, *shared_mix) = _mm(
        d_proj, w_in_s, name="mm_d_x", tb=True, b_shards=N_SHARDS, adds=((ALPHA, dz1),),
        rider=_join_riders(_shard_exchange_rider(part4), _share_rider([halves[n] for n in names3])))
    grads.update({n: whole(g) for n, g in zip(names3, shared_mix)})
    halves["w_in"], = shard_sums(part4[:1], [lb_in], ["w_in"])
    eighths = _shard_sum(part4[1], lb_small, jc_arr, "shard_sum_small", all_slots=True)
    shared_in, reduced = _run_rider(_share_rider([halves["w_in"]], eighths), "share_in_small")
    grads["w_in"] = whole(shared_in)
    reduced = reduced.reshape(rs, LANES)
    small = dict(zip([n for n, _ in small_parts], _unpack(reduced, [p.shape for _, p in small_parts])))
    loss = small.pop("loss").reshape(())
    rcw_n = d_rnn // N_SHARDS
    fcw_n = d_ff // N_SHARDS
    small["rnn_conv_w"] = lax.dynamic_slice(small["rnn_conv_w"], (0, j_me * rcw_n), (4, rcw_n))
    small["ffn_conv_w"] = lax.dynamic_slice(small["ffn_conv_w"], (0, j_me * fcw_n), (3, fcw_n))
    for n, g in small.items():
        grads[n] = g

    for n in order:
        if n not in out_g:
            adamw(n)

    return (loss, grad_x.reshape(x.shape), *[out_g[n] for n in order], *[out_d[n] for n in order],
            *[out_m[n] for n in order], *[out_v[n] for n in order])
```

```python
import functools
import math

import jax
import jax.numpy as jnp
from jax import lax
from jax.experimental import pallas as pl
from jax.experimental.pallas import tpu as pltpu

F32 = jnp.float32
BF16 = jnp.bfloat16
MESH = pl.DeviceIdType.MESH

HEAD_DIM = 64
GROUP = 8
ATTN_BLOCK = 128
LRU_C = 8.0
LN_EPS = 1e-5
ALPHA = 2.0 ** 0.25
LANES = 128
N_SHARDS = 4
N_DEV = 8
VMEM_LIMIT = 56 * 1024 * 1024
MM_VMEM_BUDGET = 40 * 1024 * 1024
MM_MAX_TILE = 3072
PACK_ROW_MULT = 8 * 64
NEG = -1e30

ADAM_LR, ADAM_B1, ADAM_B2, ADAM_EPS, ADAM_WD, ADAM_STEP = 0.001, 0.9, 0.999, 1e-08, 0.01, 10

GELU_C = math.sqrt(2.0 / math.pi)
GELU_A = 0.044715


def _cparams(sem=None):
    kw = dict(vmem_limit_bytes=VMEM_LIMIT)
    if sem is not None:
        kw["dimension_semantics"] = sem
    return pltpu.CompilerParams(**kw)


def _pick(n, prefs):
    for p in prefs:
        if n % p == 0:
            return p
    return n


def _row_tile(rows, row_bytes, mult, budget=2 * 1024 * 1024):
    best = None
    for d in range(mult, rows + 1, mult):
        if rows % d == 0 and d * row_bytes <= budget:
            best = d
    return best if best is not None else rows


def _gelu(x):
    return 0.5 * x * (1.0 + jnp.tanh(GELU_C * (x + GELU_A * x * x * x)))


def _gelu_and_grad(x):
    t = jnp.tanh(GELU_C * (x + GELU_A * x * x * x))
    g = 0.5 * x * (1.0 + t)
    dg = 0.5 * (1.0 + t) + 0.5 * x * (1.0 - t * t) * GELU_C * (1.0 + 3.0 * GELU_A * x * x)
    return g, dg


def _shift_down(x, s, fill=0.0):
    row = lax.broadcasted_iota(jnp.int32, x.shape, 0)
    return jnp.where(row >= s, pltpu.roll(x, s, 0), fill)


def _shift_up(x, s, fill=0.0):
    n = x.shape[0]
    row = lax.broadcasted_iota(jnp.int32, x.shape, 0)
    return jnp.where(row < n - s, pltpu.roll(x, n - s, 0), fill)


def _mm(a, b, *, name, ta=False, tb=False, out_dtype=F32, adds=(), b_shards=1, out_shards=1,
        tm=None, tn=None, tk=None, rider=None):
    if ta:
        K, M = a.shape
    else:
        M, K = a.shape
    if b_shards > 1:
        n_sh = b.shape[-1]
        if tb:
            N = b.shape[1]
            assert b_shards * n_sh == K
        else:
            N = b_shards * n_sh
            assert b.shape[1] == K
    else:
        n_sh = None
        if tb:
            N = b.shape[0]
            assert b.shape[1] == K
        else:
            N = b.shape[1]
            assert b.shape[0] == K
    wide = (1024, 1536, 1280, 768, 640, 512, 256, 128)
    if tn is None:
        if b_shards > 1 and not tb:
            tn = n_sh if n_sh <= MM_MAX_TILE else _pick(n_sh, wide)
        elif out_shards > 1:
            tn = N // out_shards if N // out_shards <= MM_MAX_TILE else _pick(N // out_shards, wide)
        else:
            tn = _pick(N, wide)
    if tk is None:
        if b_shards > 1 and tb:
            tk = n_sh if n_sh <= MM_MAX_TILE else _pick(n_sh, wide)
        else:
            tk = K if K <= MM_MAX_TILE else _pick(K, (2048,) + wide)
    assert N % tn == 0 and K % tk == 0, (name, M, N, K, tn, tk)
    nk = K // tk
    n_add = len(adds)
    sa, sb, so = a.dtype.itemsize, b.dtype.itemsize, jnp.dtype(out_dtype).itemsize

    def vmem_bytes(tm_):
        return (2 * (tm_ * tk * sa + tk * tn * sb + tm_ * tn * so + n_add * tm_ * tn * 4)
                + (tm_ * tn * 4 if nk > 1 else 0))

    if tm is None:
        tm = _pick(M, (1024, 512, 256, 128)) if nk > 1 else _pick(M, (512, 256, 128))
        while vmem_bytes(tm) > MM_VMEM_BUDGET and tm % 256 == 0:
            tm //= 2
    assert M % tm == 0, (name, M, tm)
    b_outer = b.size * sb >= a.size * sa

    def ij(g0, g1):
        return (g1, g0) if b_outer else (g0, g1)

    def amap(g0, g1, k):
        i, _ = ij(g0, g1)
        return (k, i) if ta else (i, k)

    def bmap(g0, g1, k):
        _, j = ij(g0, g1)
        if b_shards > 1 and not tb:
            per = n_sh // tn
            return (j // per, k, j % per)
        if b_shards > 1 and tb:
            per = n_sh // tk
            return (k // per, j, k % per)
        return (j, k) if tb else (k, j)

    def omap(g0, g1, k):
        i, j = ij(g0, g1)
        if out_shards > 1:
            per_o = (N // out_shards) // tn
            return (j // per_o, i, j % per_o)
        return (i, j)

    a_spec = pl.BlockSpec((tk, tm) if ta else (tm, tk), amap)
    if b_shards > 1:
        b_spec = pl.BlockSpec((None, tn, tk) if tb else (None, tk, tn), bmap)
    else:
        b_spec = pl.BlockSpec((tn, tk) if tb else (tk, tn), bmap)
    add_specs = [pl.BlockSpec((tm, tn), lambda g0, g1, k: ij(g0, g1)) for _ in adds]
    if out_shards > 1:
        out_spec = pl.BlockSpec((None, tm, tn), omap)
        out_shape = jax.ShapeDtypeStruct((out_shards, M, N // out_shards), out_dtype)
    else:
        out_spec = pl.BlockSpec((tm, tn), omap)
        out_shape = jax.ShapeDtypeStruct((M, N), out_dtype)

    if ta:
        dims = (((0,), (0,)), ((), ()))
    elif tb:
        dims = (((1,), (1,)), ((), ()))
    else:
        dims = (((1,), (0,)), ((), ()))
    scales = tuple(s for s, _ in adds)

    def finish(r, add_refs, o_ref):
        for s, ref in zip(scales, add_refs):
            r = r + s * ref[...].astype(F32)
        o_ref[...] = r.astype(out_dtype)

    def body(a_ref, b_ref, *rest):
        add_refs = rest[:n_add]
        o_ref = rest[n_add]
        part = lax.dot_general(a_ref[...].astype(BF16), b_ref[...].astype(BF16), dims, preferred_element_type=F32)
        if nk == 1:
            finish(part, add_refs, o_ref)
            return
        acc = rest[n_add + 1]
        k = pl.program_id(2)

        @pl.when(k == 0)
        def _():
            acc[...] = part

        @pl.when(k > 0)
        def _():
            acc[...] += part

        @pl.when(k == nk - 1)
        def _():
            finish(acc[...], add_refs, o_ref)

    grid = (N // tn, M // tm, nk) if b_outer else (M // tm, N // tn, nk)
    (res,), carried = _call(
        body, name=name, grid=grid, in_specs=[a_spec, b_spec] + add_specs, out_specs=[out_spec],
        out_shape=[out_shape], scratch_shapes=[pltpu.VMEM((tm, tn), F32)] if nk > 1 else [],
        args=(a, b, *[x for _, x in adds]), sem=("parallel", "parallel", "arbitrary"), rider=rider)
    return (res, carried) if rider is not None else res


def _cast_bf16(w, name):
    R, C = w.shape
    tr = _row_tile(R, C * 4, 16)

    def body(w_ref, o_ref):
        o_ref[...] = w_ref[...].astype(BF16)

    return pl.pallas_call(
        body, name=name, out_shape=jax.ShapeDtypeStruct((R, C), BF16), grid=(R // tr,),
        in_specs=[pl.BlockSpec((tr, C), lambda r: (r, 0))], out_specs=pl.BlockSpec((tr, C), lambda r: (r, 0)),
        compiler_params=_cparams(("parallel",)),
    )(w)


def _cast_bf16_into_slot(w, jc_arr, name):
    R, C = w.shape
    tr = _row_tile(R, C * 4, 16)

    def body(jc_ref, w_ref, o_ref):
        o_ref[...] = w_ref[...].astype(BF16)

    gs = pltpu.PrefetchScalarGridSpec(
        num_scalar_prefetch=1, grid=(R // tr,),
        in_specs=[pl.BlockSpec((tr, C), lambda r, jc: (r, 0))],
        out_specs=pl.BlockSpec((None, tr, C), lambda r, jc: (jc[0], r, 0)))
    return pl.pallas_call(body, name=name, out_shape=jax.ShapeDtypeStruct((N_SHARDS, R, C), BF16), grid_spec=gs,
                          compiler_params=_cparams(("parallel",)))(jc_arr, w)


def _pair_sum(g, la, jc_arr, name):
    S, R, C = g.shape
    half = R // 2
    tr = _row_tile(half, C * 4, 16)
    nrt = half // tr
    dt = g.dtype

    def body(jc_ref, g_ref, la_ref, o_ref):
        o_ref[...] = (g_ref[...].astype(F32) + la_ref[...].astype(F32)).astype(dt)

    gs = pltpu.PrefetchScalarGridSpec(
        num_scalar_prefetch=1, grid=(S, nrt),
        in_specs=[pl.BlockSpec((None, tr, C), lambda s, r, jc: (s, jc[1] * nrt + r, 0)),
                  pl.BlockSpec((None, tr, C), lambda s, r, jc: (s, r, 0))],
        out_specs=pl.BlockSpec((None, tr, C), lambda s, r, jc: (s, r, 0)))
    return pl.pallas_call(body, name=name, out_shape=jax.ShapeDtypeStruct((S, half, C), dt), grid_spec=gs,
                          compiler_params=_cparams(("parallel", "parallel")))(jc_arr, g, la)


def _shard_sum(cp, lb, jc_arr, name, all_slots=False):
    S, h, C = cp.shape
    tr = _row_tile(h, C * 4, 16)

    def body(jc_ref, cp_ref, l0, l1, l2, o_ref):
        o_ref[...] = ((cp_ref[...].astype(F32) + l0[...].astype(F32)) + l1[...].astype(F32)) + l2[...].astype(F32)

    def lspec(kk):
        return pl.BlockSpec((None, tr, C), lambda r, jc: (kk, r, 0))

    if all_slots:
        out_spec = pl.BlockSpec((None, None, tr, C), lambda r, jc: (jc[0], jc[1], r, 0))
        out_shape = jax.ShapeDtypeStruct((S, 2, h, C), F32)
    else:
        out_spec = pl.BlockSpec((None, tr, C), lambda r, jc: (jc[1], r, 0))
        out_shape = jax.ShapeDtypeStruct((2, h, C), F32)
    gs = pltpu.PrefetchScalarGridSpec(
        num_scalar_prefetch=1, grid=(h // tr,),
        in_specs=[pl.BlockSpec((None, tr, C), lambda r, jc: (jc[0], r, 0)), lspec(0), lspec(1), lspec(2)],
        out_specs=out_spec)
    return pl.pallas_call(body, name=name, out_shape=out_shape, grid_spec=gs,
                          compiler_params=_cparams(("parallel",)))(jc_arr, cp, lb, lb, lb)


ANY = pl.BlockSpec(memory_space=pl.ANY)


def _place():
    x, y, c = lax.axis_index("x"), lax.axis_index("y"), lax.axis_index("c")
    chips = [(1 - x, y), (x, 1 - y), (1 - x, 1 - y)]
    return x, y, c, chips


class _Rider:
    def __init__(self, inputs, out_shape, aliases, sems, start, finish):
        self.inputs, self.out_shape, self.aliases, self.sems = list(inputs), list(out_shape), dict(aliases), list(sems)
        self.start, self.finish = start, finish


def _join_riders(r1, r2):
    i1, o1, s1 = len(r1.inputs), len(r1.out_shape), len(r1.sems)
    aliases = dict(r1.aliases)
    aliases.update({i1 + i: o1 + o for i, o in r2.aliases.items()})

    def start(ins, outs, sems):
        r1.start(ins[:i1], outs[:o1], sems[:s1])
        r2.start(ins[i1:], outs[o1:], sems[s1:])

    def finish(ins, outs, sems):
        r1.finish(ins[:i1], outs[:o1], sems[:s1])
        r2.finish(ins[i1:], outs[o1:], sems[s1:])

    return _Rider(r1.inputs + r2.inputs, r1.out_shape + r2.out_shape, aliases, r1.sems + r2.sems, start, finish)


def _call(body, *, name, grid, in_specs, out_specs, out_shape, scratch_shapes, args, sem, rider=None):
    out_specs, out_shape = tuple(out_specs), tuple(out_shape)
    if rider is None:
        res = pl.pallas_call(body, name=name, out_shape=out_shape, grid=grid, in_specs=list(in_specs),
                             out_specs=out_specs, scratch_shapes=list(scratch_shapes),
                             compiler_params=_cparams(sem))(*args)
        return tuple(res), []
    n_in, n_out, n_sc = len(in_specs), len(out_specs), len(scratch_shapes)
    r_in, r_out = len(rider.inputs), len(rider.out_shape)

    def wrapped(*refs):
        p = 0
        host_in = refs[p:p + n_in]; p += n_in
        rid_in = refs[p:p + r_in]; p += r_in
        host_out = refs[p:p + n_out]; p += n_out
        rid_out = refs[p:p + r_out]; p += r_out
        host_sc = refs[p:p + n_sc]; p += n_sc
        rid_sem = refs[p:]
        first = functools.reduce(jnp.logical_and, [pl.program_id(a) == 0 for a in range(len(grid))])
        last = functools.reduce(jnp.logical_and, [pl.program_id(a) == grid[a] - 1 for a in range(len(grid))])

        @pl.when(first)
        def _():
            rider.start(rid_in, rid_out, rid_sem)

        body(*host_in, *host_out, *host_sc)

        @pl.when(last)
        def _():
            rider.finish(rid_in, rid_out, rid_sem)

    res = pl.pallas_call(
        wrapped, name=name, out_shape=out_shape + tuple(rider.out_shape), grid=grid,
        in_specs=list(in_specs) + [ANY] * r_in, out_specs=out_specs + (ANY,) * r_out,
        input_output_aliases={n_in + i: n_out + o for i, o in rider.aliases.items()},
        scratch_shapes=list(scratch_shapes) + rider.sems,
        compiler_params=_cparams(("arbitrary",) * len(grid)),
    )(*args, *rider.inputs)
    return tuple(res[:n_out]), list(res[n_out:])


def _run_rider(rider, name):
    def body(*refs):
        r_in, r_out = len(rider.inputs), len(rider.out_shape)
        ins, outs, sems = refs[:r_in], refs[r_in:r_in + r_out], refs[r_in + r_out:]
        rider.start(ins, outs, sems)
        rider.finish(ins, outs, sems)

    return pl.pallas_call(
        body, name=name, out_shape=rider.out_shape, in_specs=[ANY] * len(rider.inputs),
        out_specs=[ANY] * len(rider.out_shape), input_output_aliases=rider.aliases, scratch_shapes=rider.sems,
    )(*rider.inputs)


def _atoms(indices, kks=(0, 1, 2), q=0, nq=1):
    return [(i, kk, q, nq) for i in indices for kk in kks]


def _gather_rider(bufs, atoms=None):
    n = len(bufs)
    if atoms is None:
        atoms = _atoms(range(n))
    na = len(atoms)

    def rows(out, atom, core):
        i, _, q, nq = atom
        half = out[i].shape[1] // 2
        assert half % (16 * nq) == 0, (half, nq)
        return pl.ds(core * half + q * (half // nq), half // nq)

    def ici_copy(out, sems, a, slot, peer):
        c = lax.axis_index("c")
        blk = out[atoms[a][0]].at[slot, rows(out, atoms[a], c), :]
        return pltpu.make_async_remote_copy(
            src_ref=blk, dst_ref=blk, send_sem=sems[0].at[a], recv_sem=sems[1].at[a],
            device_id=(peer[0], peer[1], c), device_id_type=MESH)

    def d2d_copy(out, sems, a, slot, from_core):
        x, y, c, _ = _place()
        blk = out[atoms[a][0]].at[slot, rows(out, atoms[a], from_core), :]
        return pltpu.make_async_remote_copy(
            src_ref=blk, dst_ref=blk, send_sem=sems[2].at[a], recv_sem=sems[3].at[a],
            device_id=(x, y, 1 - c), device_id_type=MESH)

    def start(ins, out, sems):
        x, y, c, chips = _place()
        for a in range(na):
            ici_copy(out, sems, a, 2 * x + y, chips[atoms[a][1]]).start()

    def finish(ins, out, sems):
        x, y, c, chips = _place()
        src = [2 * chips[atoms[a][1]][0] + chips[atoms[a][1]][1] for a in range(na)]
        for a in range(na):
            ici_copy(out, sems, a, src[a], chips[atoms[a][1]]).wait_recv()
            d2d_copy(out, sems, a, src[a], c).start()
        for a in range(na):
            d2d_copy(out, sems, a, src[a], 1 - c).wait_recv()
        for a in range(na):
            ici_copy(out, sems, a, 2 * x + y, chips[atoms[a][1]]).wait_send()
            d2d_copy(out, sems, a, src[a], c).wait_send()

    return _Rider(bufs, [jax.ShapeDtypeStruct(s.shape, s.dtype) for s in bufs], {i: i for i in range(n)},
                  [pltpu.SemaphoreType.DMA((na,))] * 4, start, finish)


def _mm_gathering(a, buf, order_arr, name):
    M, K = a.shape
    S, _, n = buf.shape
    tm = _pick(M, (512, 256, 128))
    n_i = M // tm
    half = K // 2

    def body(order_ref, a_ref, w_in_ref, o_ref, w_ref, b_vmem, load_sem, s_ici, r_ici, s_d2d, r_d2d):
        s, i = pl.program_id(0), pl.program_id(1)
        x, y, c, chips = _place()
        j_me = 2 * x + y
        slots = [2 * px + py for px, py in chips]

        def ici(kk, slot):
            blk = w_ref.at[slot, pl.ds(c * half, half), :]
            return pltpu.make_async_remote_copy(
                src_ref=blk, dst_ref=blk, send_sem=s_ici.at[kk], recv_sem=r_ici.at[kk],
                device_id=(chips[kk][0], chips[kk][1], c), device_id_type=MESH)

        def d2d(kk, from_core):
            blk = w_ref.at[slots[kk], pl.ds(from_core * half, half), :]
            return pltpu.make_async_remote_copy(
                src_ref=blk, dst_ref=blk, send_sem=s_d2d.at[kk], recv_sem=r_d2d.at[kk],
                device_id=(x, y, 1 - c), device_id_type=MESH)

        def load(slot, b):
            return pltpu.make_async_copy(w_ref.at[slot], b_vmem.at[b], load_sem.at[b])

        @pl.when(jnp.logical_and(s == 0, i == 0))
        def _():
            for kk in range(3):
                ici(kk, j_me).start()
            load(j_me, 0).start()

        @pl.when(i == 0)
        def _():
            load(order_ref[s], s % 2).wait()

        o_ref[...] = jnp.dot(a_ref[...], b_vmem[s % 2], preferred_element_type=F32)

        last = i == n_i - 1

        @pl.when(jnp.logical_and(last, s == 0))
        def _():
            ici(0, slots[0]).wait_recv()
            d2d(0, c).start()
            ici(1, slots[1]).wait_recv()
            d2d(1, c).start()
            d2d(0, 1 - c).wait_recv()
            load(slots[0], 1).start()

        @pl.when(jnp.logical_and(last, s == 1))
        def _():
            d2d(1, 1 - c).wait_recv()
            load(slots[1], 0).start()

        @pl.when(jnp.logical_and(last, s == 2))
        def _():
            ici(2, slots[2]).wait_recv()
            d2d(2, c).start()
            d2d(2, 1 - c).wait_recv()
            load(slots[2], 1).start()

        @pl.when(jnp.logical_and(last, s == 3))
        def _():
            for kk in range(3):
                ici(kk, j_me).wait_send()
                d2d(kk, c).wait_send()

    gs = pltpu.PrefetchScalarGridSpec(
        num_scalar_prefetch=1, grid=(S, n_i),
        in_specs=[pl.BlockSpec((tm, K), lambda s, i, order: (i, 0)), ANY],
        out_specs=[pl.BlockSpec((tm, n), lambda s, i, order: (i, order[s])), ANY],
        scratch_shapes=[pltpu.VMEM((2, K, n), BF16), pltpu.SemaphoreType.DMA((2,))]
        + [pltpu.SemaphoreType.DMA((3,))] * 4)
    return pl.pallas_call(
        body, name=name, grid_spec=gs,
        out_shape=[jax.ShapeDtypeStruct((M, S * n), F32), jax.ShapeDtypeStruct(buf.shape, buf.dtype)],
        input_output_aliases={2: 1}, compiler_params=_cparams(("arbitrary", "arbitrary")),
    )(order_arr, a, buf)


def _all_gather_small(shards):
    n = len(shards)

    def body(*refs):
        w = refs[:n]
        out = refs[n:2 * n]
        local_sem, s_sem, r_sem = refs[2 * n:]
        x, y, c, chips = _place()
        j_me = 2 * x + y
        cps = []
        for i in range(n):
            lc = pltpu.make_async_copy(w[i], out[i].at[j_me], local_sem.at[i])
            lc.start()
            cps.append(lc)
        sends = []
        for i in range(n):
            for kk, (px, py) in enumerate(chips):
                cp = pltpu.make_async_remote_copy(
                    src_ref=w[i], dst_ref=out[i].at[j_me], send_sem=s_sem.at[3 * i + kk],
                    recv_sem=r_sem.at[3 * i + kk], device_id=(px, py, c), device_id_type=MESH)
                cp.start()
                sends.append(cp)
        for i in range(n):
            for kk, (px, py) in enumerate(chips):
                sends[3 * i + kk].wait_send()
                pltpu.make_async_remote_copy(
                    src_ref=w[i], dst_ref=out[i].at[2 * px + py], send_sem=s_sem.at[3 * i + kk],
                    recv_sem=r_sem.at[3 * i + kk], device_id=(px, py, c), device_id_type=MESH).wait_recv()
        for lc in cps:
            lc.wait()

    out_shape = [jax.ShapeDtypeStruct((N_SHARDS,) + s.shape, s.dtype) for s in shards]
    return pl.pallas_call(
        body, name="all_gather_conv_weights", out_shape=out_shape, in_specs=[ANY] * n, out_specs=[ANY] * n,
        scratch_shapes=[pltpu.SemaphoreType.DMA((n,)), pltpu.SemaphoreType.DMA((3 * n,)),
                        pltpu.SemaphoreType.DMA((3 * n,))],
    )(*shards)


def _pair_rider(grads):
    n = len(grads)

    def copies(g, la, sems):
        x, y, c, _ = _place()
        return [pltpu.make_async_remote_copy(
            src_ref=g[i].at[:, pl.ds((1 - c) * (g[i].shape[1] // 2), g[i].shape[1] // 2), :], dst_ref=la[i],
            send_sem=sems[0].at[i], recv_sem=sems[1].at[i], device_id=(x, y, 1 - c), device_id_type=MESH)
            for i in range(n)]

    def start(g, la, sems):
        for cp in copies(g, la, sems):
            cp.start()

    def finish(g, la, sems):
        for cp in copies(g, la, sems):
            cp.wait()

    return _Rider(grads, [jax.ShapeDtypeStruct((s.shape[0], s.shape[1] // 2, s.shape[2]), s.dtype) for s in grads],
                  {}, [pltpu.SemaphoreType.DMA((n,)), pltpu.SemaphoreType.DMA((n,))], start, finish)


def _shard_exchange_rider(cps_in, atoms=None, landing=None):
    n = len(cps_in)
    if atoms is None:
        atoms = _atoms(range(n))

    def copies(ins, lb, sems):
        x, y, c, chips = _place()
        out = []
        for a, (i, kk, q, nq) in enumerate(atoms):
            h = ins[i].shape[1]
            assert h % (16 * nq) == 0, (h, nq)
            rows = pl.ds(q * (h // nq), h // nq)
            px, py = chips[kk]
            out.append(pltpu.make_async_remote_copy(
                src_ref=ins[i].at[2 * px + py, rows, :], dst_ref=lb[i].at[kk, rows, :],
                send_sem=sems[0].at[a], recv_sem=sems[1].at[a], device_id=(px, py, c), device_id_type=MESH))
        return out

    def start(ins, lb, sems):
        for cp in copies(ins, lb, sems):
            cp.start()

    def finish(ins, lb, sems):
        for cp in copies(ins, lb, sems):
            cp.wait()

    shapes = [jax.ShapeDtypeStruct((3,) + s.shape[1:], s.dtype) for s in cps_in]
    sems = [pltpu.SemaphoreType.DMA((len(atoms),)), pltpu.SemaphoreType.DMA((len(atoms),))]
    if landing is None:
        return _Rider(cps_in, shapes, {}, sems, start, finish)
    return _Rider(list(cps_in) + list(landing), shapes, {n + i: i for i in range(n)}, sems, start, finish)


HBM = pl.BlockSpec(memory_space=pltpu.HBM)
SEM = pl.BlockSpec(memory_space=pltpu.SEMAPHORE)


def _diag_copy(part_ref, land_ref, send_sem, recv_sem):
    x, y, c = lax.axis_index("x"), lax.axis_index("y"), lax.axis_index("c")
    return pltpu.make_async_remote_copy(
        src_ref=part_ref.at[2 * (1 - x) + (1 - y)], dst_ref=land_ref.at[2], send_sem=send_sem, recv_sem=recv_sem,
        device_id=(1 - x, 1 - y, c), device_id_type=MESH)


def _diag_exchange_start(part, landing):
    def body(part_ref, land_ref, send_sem, recv_sem, part_thru, land_thru, token):
        _diag_copy(part_ref, land_ref, send_sem, recv_sem).start()
        token[...] = jnp.zeros_like(token)

    return pl.pallas_call(
        body, name="diag_exchange_start",
        out_shape=(pltpu.SemaphoreType.DMA(()), pltpu.SemaphoreType.DMA(()), pltpu.HBM(part.shape, part.dtype),
                   pltpu.HBM(landing.shape, landing.dtype), jax.ShapeDtypeStruct((8, LANES), F32)),
        in_specs=(HBM, HBM), out_specs=(SEM, SEM, HBM, HBM, pl.BlockSpec(memory_space=pltpu.VMEM)),
        input_output_aliases={0: 2, 1: 3},
        compiler_params=pltpu.CompilerParams(has_side_effects=pltpu.SideEffectType.DATAFLOW_SIDE_EFFECTING),
    )(pltpu.with_memory_space_constraint(part, pltpu.HBM), pltpu.with_memory_space_constraint(landing, pltpu.HBM))


def _diag_exchange_wait(send_sem, recv_sem, part_thru, land_thru, after):
    def body(part_ref, land_ref, send_sem, recv_sem, after_ref, part_dead, land_out):
        cp = _diag_copy(part_ref, land_ref, send_sem, recv_sem)
        cp.wait_send()
        cp.wait_recv()

    return pl.pallas_call(
        body, name="diag_exchange_wait",
        out_shape=(pltpu.HBM(part_thru.shape, part_thru.dtype), pltpu.HBM(land_thru.shape, land_thru.dtype)),
        in_specs=(HBM, HBM, SEM, SEM, ANY), out_specs=(HBM, HBM), input_output_aliases={0: 0, 1: 1},
        compiler_params=pltpu.CompilerParams(has_side_effects=pltpu.SideEffectType.DATAFLOW_SIDE_EFFECTING),
    )(part_thru, land_thru, send_sem, recv_sem, after)


def _share_rider(halves, eighths=None):
    n = len(halves)
    bufs = list(halves) + ([eighths] if eighths is not None else [])

    def half_copy(out, sems, i, core):
        x, y, c, _ = _place()
        blk = out[i].at[core]
        return pltpu.make_async_remote_copy(src_ref=blk, dst_ref=blk, send_sem=sems[0].at[i], recv_sem=sems[1].at[i],
                                            device_id=(x, y, 1 - c), device_id_type=MESH)

    def eighth_copy(out, sems, r, mine):
        x, y, c, _ = _place()
        px, py, pc = x ^ ((r >> 2) & 1), y ^ ((r >> 1) & 1), c ^ (r & 1)
        blk = out[n].at[2 * x + y, c] if mine else out[n].at[2 * px + py, pc]
        return pltpu.make_async_remote_copy(src_ref=blk, dst_ref=blk, send_sem=sems[2].at[r - 1],
                                            recv_sem=sems[3].at[r - 1], device_id=(px, py, pc), device_id_type=MESH)

    def start(ins, out, sems):
        c = lax.axis_index("c")
        for i in range(n):
            half_copy(out, sems, i, c).start()
        if eighths is not None:
            for r in range(1, N_DEV):
                eighth_copy(out, sems, r, True).start()

    def finish(ins, out, sems):
        c = lax.axis_index("c")
        for i in range(n):
            half_copy(out, sems, i, 1 - c).wait_recv()
        if eighths is not None:
            for r in range(1, N_DEV):
                eighth_copy(out, sems, r, False).wait_recv()
        for i in range(n):
            half_copy(out, sems, i, c).wait_send()
        if eighths is not None:
            for r in range(1, N_DEV):
                eighth_copy(out, sems, r, True).wait_send()

    return _Rider(bufs, [jax.ShapeDtypeStruct(s.shape, s.dtype) for s in bufs], {i: i for i in range(len(bufs))},
                  [pltpu.SemaphoreType.DMA((max(n, 1),)), pltpu.SemaphoreType.DMA((max(n, 1),)),
                   pltpu.SemaphoreType.DMA((N_DEV - 1,)), pltpu.SemaphoreType.DMA((N_DEV - 1,))], start, finish)


ATTN_ROWS = GROUP * ATTN_BLOCK
ATTN_KEYS = 2 * ATTN_BLOCK


def _attn_geometry(n):
    row = lax.broadcasted_iota(jnp.int32, (ATTN_ROWS, ATTN_KEYS), 0)
    col = lax.broadcasted_iota(jnp.int32, (ATTN_ROWS, ATTN_KEYS), 1)
    dist = ATTN_BLOCK + jnp.bitwise_and(row, ATTN_BLOCK - 1) - col
    valid = jnp.logical_and(jnp.logical_and(dist >= 0, dist < ATTN_BLOCK),
                            jnp.logical_or(col >= ATTN_BLOCK, n > 0))
    return dist.astype(F32), valid


def _per_head_column(values):
    head = lax.broadcasted_iota(jnp.int32, (ATTN_ROWS, 1), 0) // ATTN_BLOCK
    col = jnp.zeros((ATTN_ROWS, 1), F32)
    for hh, v in enumerate(values):
        col = jnp.where(head == hh, v, col)
    return col


def _stack_heads(ref, g):
    return jnp.concatenate(
        [ref[:, (g * GROUP + hh) * HEAD_DIM:(g * GROUP + hh + 1) * HEAD_DIM].astype(BF16) for hh in range(GROUP)],
        axis=0)


def _attn_probs(q_s, k2, slope_col, sink_col, dist, valid):
    s = lax.dot_general(q_s, k2, (((1,), (1,)), ((), ())), preferred_element_type=F32) * (HEAD_DIM ** -0.5)
    s = jnp.where(valid, s - slope_col * dist, NEG)
    m = jnp.maximum(jnp.max(s, axis=1, keepdims=True), sink_col)
    e = jnp.exp(s - m)
    es = jnp.exp(sink_col - m)
    inv = 1.0 / (jnp.sum(e, axis=1, keepdims=True) + es)
    return e * inv, es * inv


def _attn_specs(T, d_attn, d_kv, q_blk, k_blk, v_blk):
    bq = pl.BlockSpec((ATTN_BLOCK, d_attn), lambda n: (n, q_blk))
    kp = pl.BlockSpec((ATTN_BLOCK, d_kv), lambda n: (jnp.maximum(n - 1, 0), k_blk))
    kc = pl.BlockSpec((ATTN_BLOCK, d_kv), lambda n: (n, k_blk))
    vp = pl.BlockSpec((ATTN_BLOCK, d_kv), lambda n: (jnp.maximum(n - 1, 0), v_blk))
    vc = pl.BlockSpec((ATTN_BLOCK, d_kv), lambda n: (n, v_blk))
    return bq, kp, kc, vp, vc


def _attn_fwd(proj, sinks, nq, cols, rider=None):
    T = proj.shape[0]
    nkv = nq // GROUP
    d_attn, d_kv = nq * HEAD_DIM, nkv * HEAD_DIM
    q_off, k_off, v_off = cols
    bq, kp, kc, vp, vc = _attn_specs(T, d_attn, d_kv, q_off // d_attn, k_off // d_kv, v_off // d_kv)

    def body(sink_ref, q_ref, kp_ref, kc_ref, vp_ref, vc_ref, o_ref):
        n = pl.program_id(0)
        dist, valid = _attn_geometry(n)
        for g in range(nkv):
            ks = slice(g * HEAD_DIM, (g + 1) * HEAD_DIM)
            k2 = jnp.concatenate([kp_ref[:, ks], kc_ref[:, ks]], axis=0).astype(BF16)
            v2 = jnp.concatenate([vp_ref[:, ks], vc_ref[:, ks]], axis=0).astype(BF16)
            slope_col = _per_head_column([2.0 ** (-8.0 * (g * GROUP + hh + 1) / nq) for hh in range(GROUP)])
            sink_col = _per_head_column([sink_ref[0, g * GROUP + hh] for hh in range(GROUP)])
            p, _ = _attn_probs(_stack_heads(q_ref, g), k2, slope_col, sink_col, dist, valid)
            o = jnp.dot(p.astype(BF16), v2, preferred_element_type=F32).astype(BF16)
            for hh in range(GROUP):
                h = g * GROUP + hh
                o_ref[:, h * HEAD_DIM:(h + 1) * HEAD_DIM] = o[hh * ATTN_BLOCK:(hh + 1) * ATTN_BLOCK, :]

    (out,), carried = _call(
        body, name="attn_fwd", out_shape=[jax.ShapeDtypeStruct((T, d_attn), BF16)], grid=(T // ATTN_BLOCK,),
        in_specs=[pl.BlockSpec(memory_space=pltpu.SMEM), bq, kp, kc, vp, vc],
        out_specs=[pl.BlockSpec((ATTN_BLOCK, d_attn), lambda n: (n, 0))], scratch_shapes=[],
        args=(sinks, proj, proj, proj, proj, proj), sem=("parallel",), rider=rider)
    return out, carried


def _attn_bwd(proj, d_attn_out, sinks, nq, cols, rider=None):
    T = proj.shape[0]
    nkv = nq // GROUP
    d_attn, d_kv = nq * HEAD_DIM, nkv * HEAD_DIM
    q_off, k_off, v_off = cols
    bq, kp, kc, vp, vc = _attn_specs(T, d_attn, d_kv, q_off // d_attn, k_off // d_kv, v_off // d_kv)
    scale = HEAD_DIM ** -0.5
    dn_t = (((1,), (1,)), ((), ()))
    dn_r = (((0,), (0,)), ((), ()))

    def body(sink_ref, q_ref, kp_ref, kc_ref, vp_ref, vc_ref, do_ref, dq_ref, dk_ref, dv_ref, ds_ref):
        n = pl.program_id(0)

        @pl.when(n == 0)
        def _():
            dk_ref[...] = jnp.zeros_like(dk_ref)
            dv_ref[...] = jnp.zeros_like(dv_ref)
            ds_ref[...] = jnp.zeros_like(ds_ref)

        dist, valid = _attn_geometry(n)
        rows_c = pl.ds(pl.multiple_of(n * ATTN_BLOCK, ATTN_BLOCK), ATTN_BLOCK)
        rows_p = pl.ds(pl.multiple_of(jnp.maximum(n - 1, 0) * ATTN_BLOCK, ATTN_BLOCK), ATTN_BLOCK)
        lane = lax.broadcasted_iota(jnp.int32, ds_ref.shape, 1)
        srow = lax.broadcasted_iota(jnp.int32, ds_ref.shape, 0)
        ds_acc = jnp.zeros(ds_ref.shape, F32)
        for g in range(nkv):
            ks = slice(g * HEAD_DIM, (g + 1) * HEAD_DIM)
            k2 = jnp.concatenate([kp_ref[:, ks], kc_ref[:, ks]], axis=0).astype(BF16)
            v2 = jnp.concatenate([vp_ref[:, ks], vc_ref[:, ks]], axis=0).astype(BF16)
            slope_col = _per_head_column([2.0 ** (-8.0 * (g * GROUP + hh + 1) / nq) for hh in range(GROUP)])
            sink_col = _per_head_column([sink_ref[0, g * GROUP + hh] for hh in range(GROUP)])
            q_s = _stack_heads(q_ref, g)
            do_s = _stack_heads(do_ref, g)
            p, p_sink = _attn_probs(q_s, k2, slope_col, sink_col, dist, valid)
            dp = lax.dot_general(do_s, v2, dn_t, preferred_element_type=F32)
            delta = jnp.sum(p * dp, axis=1, keepdims=True)
            ds = (p * (dp - delta)).astype(BF16)
            sink_part = p_sink * delta
            dq = (jnp.dot(ds, k2, preferred_element_type=F32) * scale).astype(BF16)
            for hh in range(GROUP):
                h = g * GROUP + hh
                blk = slice(hh * ATTN_BLOCK, (hh + 1) * ATTN_BLOCK)
                dq_ref[:, h * HEAD_DIM:(h + 1) * HEAD_DIM] = dq[blk, :]
                ds_acc = ds_acc + jnp.where(jnp.logical_and(lane == h, srow == 0), -jnp.sum(sink_part[blk, :]), 0.0)
            dk2 = lax.dot_general(ds, q_s, dn_r, preferred_element_type=F32) * scale
            dv2 = lax.dot_general(p.astype(BF16), do_s, dn_r, preferred_element_type=F32)
            dk_ref[rows_p, ks] += dk2[:ATTN_BLOCK, :]
            dv_ref[rows_p, ks] += dv2[:ATTN_BLOCK, :]
            dk_ref[rows_c, ks] += dk2[ATTN_BLOCK:, :]
            dv_ref[rows_c, ks] += dv2[ATTN_BLOCK:, :]
        ds_ref[...] += ds_acc

    out_shape = (jax.ShapeDtypeStruct((T, d_attn), BF16), jax.ShapeDtypeStruct((T, d_kv), F32),
                 jax.ShapeDtypeStruct((T, d_kv), F32), jax.ShapeDtypeStruct((8, LANES), F32))
    return _call(
        body, name="attn_bwd", out_shape=out_shape, grid=(T // ATTN_BLOCK,),
        in_specs=[pl.BlockSpec(memory_space=pltpu.SMEM), bq, kp, kc, vp, vc,
                  pl.BlockSpec((ATTN_BLOCK, d_attn), lambda n: (n, 0))],
        out_specs=(pl.BlockSpec((ATTN_BLOCK, d_attn), lambda n: (n, 0)),
                   pl.BlockSpec((T, d_kv), lambda n: (0, 0)), pl.BlockSpec((T, d_kv), lambda n: (0, 0)),
                   pl.BlockSpec((8, LANES), lambda n: (0, 0))),
        scratch_shapes=[], args=(sinks, proj, proj, proj, proj, proj, d_attn_out), sem=("arbitrary",), rider=rider)


def _rnn_tile(T):
    return _pick(T, (256, 128))


def _rnn_gates(x_ext, cw_ref, cb_ref, wa_ref, wi_ref, ba_ref, bi_ref, lam_ref, tt):
    xs = [pltpu.roll(x_ext, 3 - k, 0)[8:, :] if k < 3 else x_ext[8:, :] for k in range(4)]
    cx = cb_ref[...] + xs[0] * cw_ref[0:1, :]
    for k in range(1, 4):
        cx = cx + xs[k] * cw_ref[k:k + 1, :]
    cxb = cx.astype(BF16)
    r = jax.nn.sigmoid(jnp.dot(cxb, wa_ref[...], preferred_element_type=F32) + ba_ref[...])
    i = jax.nn.sigmoid(jnp.dot(cxb, wi_ref[...], preferred_element_type=F32) + bi_ref[...])
    lam = lam_ref[...]
    sp = jnp.maximum(-lam, 0.0) + jnp.log1p(jnp.exp(-jnp.abs(lam)))
    log_a = -LRU_C * r * sp
    a = jnp.exp(log_a)
    z = 2.0 * log_a
    em1 = jnp.where(z > -1e-2, z * (1.0 + z * (0.5 + z * (1.0 / 6.0 + z * (1.0 / 24.0)))), jnp.exp(z) - 1.0)
    s = jnp.sqrt(-em1)
    return xs, cx, r, i, sp, a, s


def _rnn_specs(T, gw, tt, rx_blk, ry_blk, rev):
    nT = T // tt
    hb = tt // 8

    def tile(t):
        return (nT - 1 - t) if rev else t

    rx = pl.BlockSpec((tt, gw), lambda g, t: (tile(t), rx_blk + g))
    rx_halo = pl.BlockSpec((8, gw), lambda g, t: (jnp.maximum(tile(t) * hb - 1, 0), rx_blk + g))
    ry = pl.BlockSpec((tt, gw), lambda g, t: (tile(t), ry_blk + g))
    cw = pl.BlockSpec((4, gw), lambda g, t: (0, g))
    vec = pl.BlockSpec((1, gw), lambda g, t: (0, g))
    wg = pl.BlockSpec((None, gw, gw), lambda g, t: (g, 0, 0))
    act = pl.BlockSpec((tt, gw), lambda g, t: (tile(t), g))
    act_halo = pl.BlockSpec((8, gw), lambda g, t: (jnp.maximum(tile(t) * hb - 1, 0), g))
    return rx, rx_halo, ry, cw, vec, wg, act, act_halo, tile


def _rnn_fwd(proj, cols, conv_w, conv_b, wa_g, wi_g, ba, bi, lam, rider=None):
    T = proj.shape[0]
    G, gw, _ = wa_g.shape
    d_rnn = G * gw
    tt = _rnn_tile(T)
    rx_off, ry_off = cols
    rx, rx_halo, ry, cw, vec, wg, act, _, _ = _rnn_specs(T, gw, tt, rx_off // gw, ry_off // gw, False)

    def body(rx_ref, rxh_ref, ry_ref, cw_ref, cb_ref, wa_ref, wi_ref, ba_ref, bi_ref, lam_ref,
             b_ref, h_ref, carry):
        t = pl.program_id(1)

        @pl.when(t == 0)
        def _():
            carry[...] = jnp.zeros_like(carry)

        halo = jnp.where(t > 0, rxh_ref[...], 0.0)
        x_ext = jnp.concatenate([halo, rx_ref[...]], axis=0)
        _, cx, _, i, _, a, s = _rnn_gates(x_ext, cw_ref, cb_ref, wa_ref, wi_ref, ba_ref, bi_ref, lam_ref, tt)
        acc_a, acc_b = a, s * (i * cx)
        d = 1
        while d < tt:
            acc_b = acc_a * _shift_down(acc_b, d, 0.0) + acc_b
            acc_a = acc_a * _shift_down(acc_a, d, 1.0)
            d *= 2
        h = acc_b + acc_a * carry[7:8, :]
        carry[...] = h[tt - 8:, :]
        h_ref[...] = h
        b_ref[...] = (h * _gelu(ry_ref[...])).astype(BF16)

    return _call(
        body, name="rnn_fwd",
        out_shape=(jax.ShapeDtypeStruct((T, d_rnn), BF16), jax.ShapeDtypeStruct((T, d_rnn), F32)),
        grid=(G, T // tt),
        in_specs=[rx, rx_halo, ry, cw, vec, wg, wg, vec, vec, vec], out_specs=(act, act),
        scratch_shapes=[pltpu.VMEM((8, gw), F32)],
        args=(proj, proj, proj, conv_w, conv_b, wa_g, wi_g, ba, bi, lam), sem=("parallel", "arbitrary"), rider=rider)


def _rnn_bwd(proj, cols, h_all, d_b, conv_w, conv_b, wa_g, wi_g, ba, bi, lam, rider=None):
    T = proj.shape[0]
    G, gw, _ = wa_g.shape
    d_rnn = G * gw
    tt = _rnn_tile(T)
    nT = T // tt
    rx_off, ry_off = cols
    rx, rx_halo, ry, cw, vec, wg, act, act_halo, _ = _rnn_specs(T, gw, tt, rx_off // gw, ry_off // gw, True)
    dn_t = (((1,), (1,)), ((), ()))
    dn_r = (((0,), (0,)), ((), ()))

    def body(rx_ref, rxh_ref, ry_ref, h_ref, hh_ref, db_ref, cw_ref, cb_ref, wa_ref, wi_ref, ba_ref, bi_ref, lam_ref,
             drx_ref, dry_ref, dcw_ref, dcb_ref, dba_ref, dbi_ref, dlam_ref, dwa_ref, dwi_ref,
             lam_carry, dcx_carry):
        t = pl.program_id(1)
        first_tile = t == nT - 1

        @pl.when(t == 0)
        def _():
            lam_carry[...] = jnp.zeros_like(lam_carry)
            dcx_carry[...] = jnp.zeros_like(dcx_carry)
            dcw_ref[...] = jnp.zeros_like(dcw_ref)
            dcb_ref[...] = jnp.zeros_like(dcb_ref)
            dba_ref[...] = jnp.zeros_like(dba_ref)
            dbi_ref[...] = jnp.zeros_like(dbi_ref)
            dlam_ref[...] = jnp.zeros_like(dlam_ref)
            dwa_ref[...] = jnp.zeros_like(dwa_ref)
            dwi_ref[...] = jnp.zeros_like(dwi_ref)

        halo = jnp.where(first_tile, 0.0, rxh_ref[...])
        x_ext = jnp.concatenate([halo, rx_ref[...]], axis=0)
        xs, cx, r, i, sp, a, s = _rnn_gates(x_ext, cw_ref, cb_ref, wa_ref, wi_ref, ba_ref, bi_ref, lam_ref, tt)
        h = h_ref[...]
        h_halo = jnp.where(first_tile, 0.0, hh_ref[...])
        h_prev = pltpu.roll(jnp.concatenate([h_halo, h], axis=0), 1, 0)[8:, :]
        gel, dgel = _gelu_and_grad(ry_ref[...])
        d_b_t = db_ref[...]
        dry_ref[...] = (d_b_t * h * dgel).astype(BF16)
        dh = d_b_t * gel

        acc_c = _shift_up(a, 1, 1.0)
        acc_l = dh
        d = 1
        while d < tt:
            acc_l = acc_c * _shift_up(acc_l, d, 0.0) + acc_l
            acc_c = acc_c * _shift_up(acc_c, d, 1.0)
            d *= 2
        lam_t = acc_l + acc_c * lam_carry[0:1, :]
        lam_carry[...] = (a * lam_t)[0:8, :]

        icx = i * cx
        d_s = lam_t * icx
        d_i = lam_t * s * cx
        dcx = lam_t * s * i
        d_a = lam_t * h_prev - d_s * (a / s)
        dlog_a = d_a * a
        d_r = dlog_a * (-LRU_C * sp)
        lam = lam_ref[...]
        dlam_ref[...] += jnp.sum(dlog_a * r, axis=0, keepdims=True) * (LRU_C * jax.nn.sigmoid(-lam))
        dpr = d_r * r * (1.0 - r)
        dpi = d_i * i * (1.0 - i)
        dba_ref[...] += jnp.sum(dpr, axis=0, keepdims=True)
        dbi_ref[...] += jnp.sum(dpi, axis=0, keepdims=True)
        cxb = cx.astype(BF16)
        dprb, dpib = dpr.astype(BF16), dpi.astype(BF16)
        dwa_ref[...] += lax.dot_general(cxb, dprb, dn_r, preferred_element_type=F32)
        dwi_ref[...] += lax.dot_general(cxb, dpib, dn_r, preferred_element_type=F32)
        dcx = (dcx + lax.dot_general(dprb, wa_ref[...], dn_t, preferred_element_type=F32)
               + lax.dot_general(dpib, wi_ref[...], dn_t, preferred_element_type=F32))

        dcb_ref[...] += jnp.sum(dcx, axis=0, keepdims=True)
        for k in range(4):
            dcw_ref[k:k + 1, :] += jnp.sum(dcx * xs[k], axis=0, keepdims=True)
        d_ext = jnp.concatenate([dcx, dcx_carry[...]], axis=0)
        drx = dcx * cw_ref[3:4, :]
        for k in range(3):
            drx = drx + pltpu.roll(d_ext, tt + 8 - (3 - k), 0)[:tt, :] * cw_ref[k:k + 1, :]
        drx_ref[...] = drx.astype(BF16)
        dcx_carry[...] = dcx[0:8, :]

    out_shape = (jax.ShapeDtypeStruct((T, d_rnn), BF16), jax.ShapeDtypeStruct((T, d_rnn), BF16),
                 jax.ShapeDtypeStruct((4, d_rnn), F32), jax.ShapeDtypeStruct((1, d_rnn), F32),
                 jax.ShapeDtypeStruct((1, d_rnn), F32), jax.ShapeDtypeStruct((1, d_rnn), F32),
                 jax.ShapeDtypeStruct((1, d_rnn), F32), jax.ShapeDtypeStruct((G, gw, gw), F32),
                 jax.ShapeDtypeStruct((G, gw, gw), F32))
    return _call(
        body, name="rnn_bwd", out_shape=out_shape, grid=(G, nT),
        in_specs=[rx, rx_halo, ry, act, act_halo, act, cw, vec, wg, wg, vec, vec, vec],
        out_specs=(act, act, cw, vec, vec, vec, vec, wg, wg),
        scratch_shapes=[pltpu.VMEM((8, gw), F32), pltpu.VMEM((8, gw), F32)],
        args=(proj, proj, proj, h_all, h_all, d_b, conv_w, conv_b, wa_g, wi_g, ba, bi, lam),
        sem=("parallel", "arbitrary"), rider=rider)


def _merge_fwd(proj, gl_off, b_gate, y_attn, y_rnn, rider=None):
    T, D = y_attn.shape
    tm = _pick(T, (256, 128))
    ct = _pick(math.gcd(gl_off, D), (512, 256, 128))
    oa, orr, nd = gl_off // ct, (gl_off + D) // ct, D // ct

    def body(ga_ref, gr_ref, ba_ref, br_ref, ya_ref, yr_ref, m_ref):
        ga = jax.nn.sigmoid(ga_ref[...] + ba_ref[...])
        gr = jax.nn.sigmoid(gr_ref[...] + br_ref[...])
        m_ref[...] = (ga * ya_ref[...] + gr * yr_ref[...]).astype(BF16)

    blk = pl.BlockSpec((tm, ct), lambda i, j: (i, j))
    (merged,), carried = _call(
        body, name="merge_fwd", out_shape=[jax.ShapeDtypeStruct((T, D), BF16)], grid=(T // tm, nd),
        in_specs=[pl.BlockSpec((tm, ct), lambda i, j: (i, oa + j)), pl.BlockSpec((tm, ct), lambda i, j: (i, orr + j)),
                  pl.BlockSpec((1, ct), lambda i, j: (0, j)), pl.BlockSpec((1, ct), lambda i, j: (0, nd + j)),
                  blk, blk],
        out_specs=[blk], scratch_shapes=[], args=(proj, proj, b_gate, b_gate, y_attn, y_rnn),
        sem=("parallel", "parallel"), rider=rider)
    return merged, carried


def _merge_bwd(proj, gl_off, b_gate, y_attn, y_rnn, d_m):
    T, D = y_attn.shape
    tm = _pick(T, (256, 128))
    ct = _pick(math.gcd(gl_off, D), (512, 256, 128))
    oa, orr, nd = gl_off // ct, (gl_off + D) // ct, D // ct

    def body(ga_ref, gr_ref, ba_ref, br_ref, ya_ref, yr_ref, dm_ref,
             dya_ref, dyr_ref, dga_ref, dgr_ref, dba_ref, dbr_ref):
        i = pl.program_id(1)

        @pl.when(i == 0)
        def _():
            dba_ref[...] = jnp.zeros_like(dba_ref)
            dbr_ref[...] = jnp.zeros_like(dbr_ref)

        ga = jax.nn.sigmoid(ga_ref[...] + ba_ref[...])
        gr = jax.nn.sigmoid(gr_ref[...] + br_ref[...])
        dm = dm_ref[...]
        dya_ref[...] = (dm * ga).astype(BF16)
        dyr_ref[...] = (dm * gr).astype(BF16)
        dga = dm * ya_ref[...] * ga * (1.0 - ga)
        dgr = dm * yr_ref[...] * gr * (1.0 - gr)
        dga_ref[...] = dga.astype(BF16)
        dgr_ref[...] = dgr.astype(BF16)
        dba_ref[...] += jnp.sum(dga, axis=0, keepdims=True)
        dbr_ref[...] += jnp.sum(dgr, axis=0, keepdims=True)

    blk = pl.BlockSpec((tm, ct), lambda j, i: (i, j))
    vec = pl.BlockSpec((1, ct), lambda j, i: (0, j))
    act = jax.ShapeDtypeStruct((T, D), BF16)
    v1 = jax.ShapeDtypeStruct((1, D), F32)
    return pl.pallas_call(
        body, name="merge_bwd", out_shape=(act, act, act, act, v1, v1), grid=(nd, T // tm),
        in_specs=[pl.BlockSpec((tm, ct), lambda j, i: (i, oa + j)), pl.BlockSpec((tm, ct), lambda j, i: (i, orr + j)),
                  vec, pl.BlockSpec((1, ct), lambda j, i: (0, nd + j)), blk, blk, blk],
        out_specs=(blk, blk, blk, blk, vec, vec),
        compiler_params=_cparams(("parallel", "arbitrary")),
    )(proj, proj, b_gate, b_gate, y_attn, y_rnn, d_m)


def _ln_fwd(x_res, delta, g, b, name, rider=None):
    T, D = x_res.shape
    tm = _pick(T, (256, 128))

    def body(x_ref, d_ref, g_ref, b_ref, y_ref, yb_ref, xh_ref, rs_ref):
        z = ALPHA * x_ref[...] + d_ref[...]
        mu = jnp.mean(z, axis=1, keepdims=True)
        zc = z - mu
        var = jnp.mean(zc * zc, axis=1, keepdims=True)
        rstd = lax.rsqrt(var + LN_EPS)
        xh = zc * rstd
        xh_ref[...] = xh
        rs_ref[...] = rstd
        y = xh * g_ref[...] + b_ref[...]
        y_ref[...] = y
        yb_ref[...] = y.astype(BF16)

    row = pl.BlockSpec((tm, D), lambda i: (i, 0))
    vec = pl.BlockSpec((1, D), lambda i: (0, 0))
    return _call(
        body, name=name,
        out_shape=(jax.ShapeDtypeStruct((T, D), F32), jax.ShapeDtypeStruct((T, D), BF16),
                   jax.ShapeDtypeStruct((T, D), F32), jax.ShapeDtypeStruct((T, 1), F32)),
        grid=(T // tm,), in_specs=[row, row, vec, vec],
        out_specs=(row, row, row, pl.BlockSpec((tm, 1), lambda i: (i, 0))),
        scratch_shapes=[], args=(x_res, delta, g, b), sem=("parallel",), rider=rider)


def _ln_bwd_rows(dy, xh, rstd, g):
    dxh = dy * g
    m1 = jnp.mean(dxh, axis=1, keepdims=True)
    m2 = jnp.mean(dxh * xh, axis=1, keepdims=True)
    return rstd * (dxh - m1 - xh * m2)


def _ln_loss_bwd(x_res, delta, g, b, target):
    T, D = x_res.shape
    tm = _pick(T, (256, 128))

    def body(x_ref, d_ref, g_ref, b_ref, t_ref, dz_ref, dzb_ref, loss_ref, dg_ref, db_ref):
        i = pl.program_id(0)

        @pl.when(i == 0)
        def _():
            loss_ref[...] = jnp.zeros_like(loss_ref)
            dg_ref[...] = jnp.zeros_like(dg_ref)
            db_ref[...] = jnp.zeros_like(db_ref)

        z = ALPHA * x_ref[...] + d_ref[...]
        mu = jnp.mean(z, axis=1, keepdims=True)
        zc = z - mu
        var = jnp.mean(zc * zc, axis=1, keepdims=True)
        rstd = lax.rsqrt(var + LN_EPS)
        xh = zc * rstd
        gv = g_ref[...]
        err = xh * gv + b_ref[...] - t_ref[...]
        loss_ref[...] += 0.5 * jnp.sum(jnp.mean(err * err, axis=1, keepdims=True))
        dy = err * (1.0 / D)
        dg_ref[...] += jnp.sum(dy * xh, axis=0, keepdims=True)
        db_ref[...] += jnp.sum(dy, axis=0, keepdims=True)
        dz = _ln_bwd_rows(dy, xh, rstd, gv)
        dz_ref[...] = dz
        dzb_ref[...] = dz.astype(BF16)

    row = pl.BlockSpec((tm, D), lambda i: (i, 0))
    vec = pl.BlockSpec((1, D), lambda i: (0, 0))
    return pl.pallas_call(
        body, name="ln2_loss_bwd",
        out_shape=(jax.ShapeDtypeStruct((T, D), F32), jax.ShapeDtypeStruct((T, D), BF16),
                   jax.ShapeDtypeStruct((8, LANES), F32),
                   jax.ShapeDtypeStruct((1, D), F32), jax.ShapeDtypeStruct((1, D), F32)),
        grid=(T // tm,), in_specs=[row, row, vec, vec, row],
        out_specs=(row, row, pl.BlockSpec((8, LANES), lambda i: (0, 0)), vec, vec),
        compiler_params=_cparams(("arbitrary",)),
    )(x_res, delta, g, b, target)


def _ln_bwd(dy, xh, rstd, g):
    T, D = dy.shape
    tm = _pick(T, (256, 128))

    def body(dy_ref, xh_ref, rs_ref, g_ref, dz_ref, dzb_ref, dg_ref, db_ref):
        i = pl.program_id(0)

        @pl.when(i == 0)
        def _():
            dg_ref[...] = jnp.zeros_like(dg_ref)
            db_ref[...] = jnp.zeros_like(db_ref)

        dyv, xhv = dy_ref[...], xh_ref[...]
        dg_ref[...] += jnp.sum(dyv * xhv, axis=0, keepdims=True)
        db_ref[...] += jnp.sum(dyv, axis=0, keepdims=True)
        dz = _ln_bwd_rows(dyv, xhv, rs_ref[...], g_ref[...])
        dz_ref[...] = dz
        dzb_ref[...] = dz.astype(BF16)

    row = pl.BlockSpec((tm, D), lambda i: (i, 0))
    vec = pl.BlockSpec((1, D), lambda i: (0, 0))
    return pl.pallas_call(
        body, name="ln1_bwd",
        out_shape=(jax.ShapeDtypeStruct((T, D), F32), jax.ShapeDtypeStruct((T, D), BF16),
                   jax.ShapeDtypeStruct((1, D), F32), jax.ShapeDtypeStruct((1, D), F32)),
        grid=(T // tm,), in_specs=[row, row, pl.BlockSpec((tm, 1), lambda i: (i, 0)), vec],
        out_specs=(row, row, vec, vec), compiler_params=_cparams(("arbitrary",)),
    )(dy, xh, rstd, g)


def _ffn_col_tile(T, d_ff):
    return _pick(d_ff, (256, 128)) if T >= 1024 else _pick(d_ff, (512, 256, 128))


def _ffn_gate(gp, cw_ref, cb_ref):
    return (cb_ref[...] + gp * cw_ref[2:3, :] + _shift_down(gp, 1) * cw_ref[1:2, :]
            + _shift_down(gp, 2) * cw_ref[0:1, :])


def _ffn_fwd(up, gpre, conv_w, conv_b, rider=None):
    T, d_ff = up.shape
    ct = _ffn_col_tile(T, d_ff)

    def body(up_ref, gp_ref, cw_ref, cb_ref, f_ref):
        gate = _ffn_gate(gp_ref[...], cw_ref, cb_ref)
        f_ref[...] = (_gelu(gate) * up_ref[...]).astype(BF16)

    col = pl.BlockSpec((T, ct), lambda j: (0, j))
    (f,), carried = _call(
        body, name="ffn_act_fwd", out_shape=[jax.ShapeDtypeStruct((T, d_ff), BF16)], grid=(d_ff // ct,),
        in_specs=[col, col, pl.BlockSpec((3, ct), lambda j: (0, j)), pl.BlockSpec((1, ct), lambda j: (0, j))],
        out_specs=[col], scratch_shapes=[], args=(up, gpre, conv_w, conv_b), sem=("parallel",), rider=rider)
    return f, carried


def _ffn_bwd(up, gpre, conv_w, conv_b, d_f, rider=None):
    T, d_ff = up.shape
    ct = _ffn_col_tile(T, d_ff)

    def body(up_ref, gp_ref, cw_ref, cb_ref, df_ref, dup_ref, dgp_ref, dcw_ref, dcb_ref):
        gp = gp_ref[...]
        gate = _ffn_gate(gp, cw_ref, cb_ref)
        gel, dgel = _gelu_and_grad(gate)
        df = df_ref[...]
        dup_ref[...] = (df * gel).astype(BF16)
        dgate = df * up_ref[...] * dgel
        dcb_ref[...] = jnp.sum(dgate, axis=0, keepdims=True)
        dcw_ref[2:3, :] = jnp.sum(dgate * gp, axis=0, keepdims=True)
        dcw_ref[1:2, :] = jnp.sum(dgate * _shift_down(gp, 1), axis=0, keepdims=True)
        dcw_ref[0:1, :] = jnp.sum(dgate * _shift_down(gp, 2), axis=0, keepdims=True)
        dgp = (dgate * cw_ref[2:3, :] + _shift_up(dgate, 1) * cw_ref[1:2, :]
               + _shift_up(dgate, 2) * cw_ref[0:1, :])
        dgp_ref[...] = dgp.astype(BF16)

    col = pl.BlockSpec((T, ct), lambda j: (0, j))
    w3 = pl.BlockSpec((3, ct), lambda j: (0, j))
    v1 = pl.BlockSpec((1, ct), lambda j: (0, j))
    return _call(
        body, name="ffn_act_bwd",
        out_shape=(jax.ShapeDtypeStruct((T, d_ff), BF16), jax.ShapeDtypeStruct((T, d_ff), BF16),
                   jax.ShapeDtypeStruct((3, d_ff), F32), jax.ShapeDtypeStruct((1, d_ff), F32)),
        grid=(d_ff // ct,), in_specs=[col, col, w3, v1, col], out_specs=(col, col, w3, v1),
        scratch_shapes=[], args=(up, gpre, conv_w, conv_b, d_f), sem=("parallel",), rider=rider)


def _adamw(w, g, m, v, name, after=None):
    R, C = w.shape
    tr = _row_tile(R, C * 4, 8, budget=1280 * 1024)
    c1 = 1.0 / (1.0 - ADAM_B1 ** ADAM_STEP)
    c2 = 1.0 / (1.0 - ADAM_B2 ** ADAM_STEP)
    n_in = 4 if after is None else 5

    def body(*refs):
        w_ref, g_ref, m_ref, v_ref = refs[:4]
        go_ref, d_ref, nm_ref, nv_ref = refs[n_in:]
        gv = g_ref[...]
        go_ref[...] = gv
        nm = ADAM_B1 * m_ref[...] + (1.0 - ADAM_B1) * gv
        nv = ADAM_B2 * v_ref[...] + (1.0 - ADAM_B2) * (gv * gv)
        nm_ref[...] = nm
        nv_ref[...] = nv
        d_ref[...] = -ADAM_LR * ((nm * c1) / (jnp.sqrt(nv * c2) + ADAM_EPS) + ADAM_WD * w_ref[...])

    blk = pl.BlockSpec((tr, C), lambda r: (r, 0))
    sh = jax.ShapeDtypeStruct((R, C), F32)
    args = (w, g, m, v) if after is None else (w, g, m, v, after)
    return _call(body, name=name, out_shape=(sh,) * 4, grid=(R // tr,), in_specs=[blk] * 4 + [ANY] * (n_in - 4),
                 out_specs=(blk,) * 4, scratch_shapes=[], args=args, sem=("parallel",))[0]


def _group_blocks(w_blocks, per):
    nb, bw, _ = w_blocks.shape
    G = nb // per
    w4 = w_blocks.reshape(G, per, bw, bw)
    rows = []
    for p in range(per):
        parts = [w4[:, p] if q == p else jnp.zeros((G, bw, bw), w_blocks.dtype) for q in range(per)]
        rows.append(jnp.concatenate(parts, axis=2))
    return jnp.concatenate(rows, axis=1)


def _ungroup_blocks(w_groups, per):
    G, gw, _ = w_groups.shape
    bw = gw // per
    blocks = [w_groups[:, p * bw:(p + 1) * bw, p * bw:(p + 1) * bw] for p in range(per)]
    return jnp.stack(blocks, axis=1).reshape(G * per, bw, bw)


def _pack(parts):
    flat = jnp.concatenate([p.reshape(-1).astype(F32) for p in parts])
    n = flat.shape[0]
    rows = -(-n // LANES)
    rows = -(-rows // PACK_ROW_MULT) * PACK_ROW_MULT
    flat = jnp.pad(flat, (0, rows * LANES - n))
    return flat.reshape(rows, LANES)


def _unpack(packed, shapes):
    flat = packed.reshape(-1)
    out, off = [], 0
    for s in shapes:
        n = math.prod(s)
        out.append(flat[off:off + n].reshape(s))
        off += n
    return out


def kernel(x, w_in, b_gate, rnn_conv_w, rnn_conv_b, lru_wa, lru_ba, lru_wi, lru_bi, lru_lambda, attn_sinks, w_attn_proj, w_rnn_proj, w_out, ln1_g, ln1_b, ffn_w_up, ffn_w_gate, ffn_conv_w, ffn_conv_b, ffn_w_down, ln2_g, ln2_b, loss_target, m_w_in, m_b_gate, m_rnn_conv_w, m_rnn_conv_b, m_lru_wa, m_lru_ba, m_lru_wi, m_lru_bi, m_lru_lambda, m_attn_sinks, m_w_attn_proj, m_w_rnn_proj, m_w_out, m_ln1_g, m_ln1_b, m_ffn_w_up, m_ffn_w_gate, m_ffn_conv_w, m_ffn_conv_b, m_ffn_w_down, m_ln2_g, m_ln2_b, v_w_in, v_b_gate, v_rnn_conv_w, v_rnn_conv_b, v_lru_wa, v_lru_ba, v_lru_wi, v_lru_bi, v_lru_lambda, v_attn_sinks, v_w_attn_proj, v_w_rnn_proj, v_w_out, v_ln1_g, v_ln1_b, v_ffn_w_up, v_ffn_w_gate, v_ffn_conv_w, v_ffn_conv_b, v_ffn_w_down, v_ln2_g, v_ln2_b):
    weights = dict(w_in=w_in, b_gate=b_gate, rnn_conv_w=rnn_conv_w, rnn_conv_b=rnn_conv_b, lru_wa=lru_wa,
                   lru_ba=lru_ba, lru_wi=lru_wi, lru_bi=lru_bi, lru_lambda=lru_lambda, attn_sinks=attn_sinks,
                   w_attn_proj=w_attn_proj, w_rnn_proj=w_rnn_proj, w_out=w_out, ln1_g=ln1_g, ln1_b=ln1_b,
                   ffn_w_up=ffn_w_up, ffn_w_gate=ffn_w_gate, ffn_conv_w=ffn_conv_w, ffn_conv_b=ffn_conv_b,
                   ffn_w_down=ffn_w_down, ln2_g=ln2_g, ln2_b=ln2_b)
    m_in = dict(w_in=m_w_in, b_gate=m_b_gate, rnn_conv_w=m_rnn_conv_w, rnn_conv_b=m_rnn_conv_b, lru_wa=m_lru_wa,
                lru_ba=m_lru_ba, lru_wi=m_lru_wi, lru_bi=m_lru_bi, lru_lambda=m_lru_lambda, attn_sinks=m_attn_sinks,
                w_attn_proj=m_w_attn_proj, w_rnn_proj=m_w_rnn_proj, w_out=m_w_out, ln1_g=m_ln1_g, ln1_b=m_ln1_b,
                ffn_w_up=m_ffn_w_up, ffn_w_gate=m_ffn_w_gate, ffn_conv_w=m_ffn_conv_w, ffn_conv_b=m_ffn_conv_b,
                ffn_w_down=m_ffn_w_down, ln2_g=m_ln2_g, ln2_b=m_ln2_b)
    v_in = dict(w_in=v_w_in, b_gate=v_b_gate, rnn_conv_w=v_rnn_conv_w, rnn_conv_b=v_rnn_conv_b, lru_wa=v_lru_wa,
                lru_ba=v_lru_ba, lru_wi=v_lru_wi, lru_bi=v_lru_bi, lru_lambda=v_lru_lambda, attn_sinks=v_attn_sinks,
                w_attn_proj=v_w_attn_proj, w_rnn_proj=v_w_rnn_proj, w_out=v_w_out, ln1_g=v_ln1_g, ln1_b=v_ln1_b,
                ffn_w_up=v_ffn_w_up, ffn_w_gate=v_ffn_w_gate, ffn_conv_w=v_ffn_conv_w, ffn_conv_b=v_ffn_conv_b,
                ffn_w_down=v_ffn_w_down, ln2_g=v_ln2_g, ln2_b=v_ln2_b)
    order = list(weights)

    assert x.shape[0] == 1 and w_in.shape[0] == 1, "one sequence per device, depth 1"
    T, D = x.shape[1], x.shape[2]
    nq = attn_sinks.shape[-1]
    nkv = nq // GROUP
    d_attn, d_kv = nq * HEAD_DIM, nkv * HEAD_DIM
    d_rnn = rnn_conv_b.shape[-1]
    d_ff = ffn_conv_b.shape[-1]
    n_blocks, bw = lru_wa.shape[1], lru_wa.shape[2]
    per = (bw * LANES // math.gcd(bw, LANES)) // bw
    gw = per * bw
    assert n_blocks % per == 0 and d_rnn == n_blocks * bw
    q_off, k_off, v_off = 0, d_attn, d_attn + d_kv
    rx_off = d_attn + 2 * d_kv
    ry_off = rx_off + d_rnn
    gl_off = ry_off + d_rnn
    d_in = gl_off + 2 * D
    assert w_in.shape[-1] * N_SHARDS == d_in
    assert k_off % d_kv == 0 and rx_off % gw == 0 and T % ATTN_BLOCK == 0

    xi, yi, ci = lax.axis_index("x"), lax.axis_index("y"), lax.axis_index("c")
    j_me = 2 * xi + yi
    jc_arr = jnp.stack([j_me, ci]).astype(jnp.int32)

    x0 = x[0]
    x0b = _cast_bf16(x0, "cast_x")
    tgt = loss_target[0]
    big = ["w_in", "w_attn_proj", "w_rnn_proj", "w_out", "ffn_w_up", "ffn_w_gate", "ffn_w_down"]
    own = {n: _cast_bf16_into_slot(weights[n][0], jc_arr, "cast_" + n) for n in big}
    order_arr = jnp.stack([j_me, j_me ^ 2, j_me ^ 1, j_me ^ 3]).astype(jnp.int32)

    rcw_s, fcw_s = _all_gather_small([rnn_conv_w[0], ffn_conv_w[0]])
    rcw = jnp.concatenate([rcw_s[j] for j in range(N_SHARDS)], axis=1)
    fcw = jnp.concatenate([fcw_s[j] for j in range(N_SHARDS)], axis=1)

    wa_g = _group_blocks(lru_wa[0], per).astype(BF16)
    wi_g = _group_blocks(lru_wi[0], per).astype(BF16)

    near, diag = (0, 1), (2,)
    proj, w_in_s = _mm_gathering(x0b, own["w_in"], order_arr, "mm_proj")
    a_out, (w_ap_s, w_rp_s) = _attn_fwd(
        proj, attn_sinks, nq, (q_off, k_off, v_off),
        rider=_gather_rider([own["w_attn_proj"], own["w_rnn_proj"]], _atoms([0]) + _atoms([1], near)))
    (b_out, h_all), (w_rp_s, w_o_s, w_up_s) = _rnn_fwd(
        proj, (rx_off, ry_off), rcw, rnn_conv_b, wa_g, wi_g, lru_ba, lru_bi, lru_lambda,
        rider=_gather_rider([w_rp_s, own["w_out"], own["ffn_w_up"]],
                            _atoms([0], diag) + _atoms([1]) + _atoms([2], near, 0, 2)))
    w_ap, w_rp, w_o = w_ap_s.reshape(d_attn, D), w_rp_s.reshape(d_rnn, D), w_o_s.reshape(D, D)
    y_attn, (w_up_s,) = _mm(a_out, w_ap, name="mm_attn_proj", rider=_gather_rider([w_up_s], _atoms([0], near, 1, 2)))
    y_rnn, (w_up_s,) = _mm(b_out, w_rp, name="mm_rnn_proj", rider=_gather_rider([w_up_s], _atoms([0], diag, 0, 2)))
    merged, (w_gate_s,) = _merge_fwd(proj, gl_off, b_gate, y_attn, y_rnn,
                                     rider=_gather_rider([own["ffn_w_gate"]], _atoms([0], near, 0, 2)))
    mix, (w_up_s,) = _mm(merged, w_o, name="mm_out", rider=_gather_rider([w_up_s], _atoms([0], diag, 1, 2)))
    (x1, x1b, xh1, rstd1), (w_gate_s,) = _ln_fwd(x0, mix, ln1_g, ln1_b, "ln1_fwd",
                                                 rider=_gather_rider([w_gate_s], _atoms([0], near, 1, 2)))
    up, (w_gate_s,) = _mm(x1b, w_up_s, name="mm_up", b_shards=N_SHARDS,
                          rider=_gather_rider([w_gate_s], _atoms([0], diag)))
    gpre, (w_dn_s,) = _mm(x1b, w_gate_s, name="mm_gate", b_shards=N_SHARDS,
                          rider=_gather_rider([own["ffn_w_down"]], _atoms([0], near)))
    f_act, (w_dn_s,) = _ffn_fwd(up, gpre, fcw, ffn_conv_b,
                                rider=_gather_rider([w_dn_s], _atoms([0], diag)))
    w_dn = w_dn_s.reshape(d_ff, D)
    f_out = _mm(f_act, w_dn, name="mm_down")
    dz2, dz2b, loss_acc, dg2, db2 = _ln_loss_bwd(x1, f_out, ln2_g, ln2_b, tgt)

    def pair_sums(arrs, from_sibling, names):
        return [_pair_sum(g, la, jc_arr, "pair_sum_" + n) for g, la, n in zip(arrs, from_sibling, names)]

    def shard_sums(parts, landed, names):
        return [_shard_sum(cp, lb, jc_arr, "shard_sum_" + n) for cp, lb, n in zip(parts, landed, names)]

    halves = {}
    g_down = _mm(f_act, dz2b, name="mm_d_w_down", ta=True, out_dtype=BF16)
    g1 = [g_down.reshape(N_SHARDS, d_ff // N_SHARDS, D)]
    d_f, sib1 = _mm(dz2b, w_dn, name="mm_d_f", tb=True, rider=_pair_rider(g1))
    part1 = pair_sums(g1, sib1, ["ffn_w_down"])
    (dup, dgp, d_fcw, d_fcb), landed1 = _ffn_bwd(up, gpre, fcw, ffn_conv_b, d_f,
                                                 rider=_shard_exchange_rider(part1, _atoms([0], near)))
    g_up, landed1 = _mm(x1b, dup, name="mm_d_w_up", ta=True, out_dtype=BF16, out_shards=N_SHARDS,
                        rider=_shard_exchange_rider(part1, _atoms([0], diag), landed1))
    halves["ffn_w_down"], = shard_sums(part1, landed1, ["ffn_w_down"])
    g_gate = _mm(x1b, dgp, name="mm_d_w_gate", ta=True, out_dtype=BF16, out_shards=N_SHARDS)
    g2 = [g_up, g_gate]
    dx1_a, sib2 = _mm(dup, w_up_s, name="mm_dx1_up", tb=True, b_shards=N_SHARDS, adds=((ALPHA, dz2),),
                      rider=_pair_rider(g2))
    part2 = pair_sums(g2, sib2, ["ffn_w_up", "ffn_w_gate"])
    dx1, landed2 = _mm(dgp, w_gate_s, name="mm_dx1_gate", tb=True, b_shards=N_SHARDS, adds=((1.0, dx1_a),),
                       rider=_shard_exchange_rider(part2, _atoms([0], near)))
    dz1, dz1b, dg1, db1 = _ln_bwd(dx1, xh1, rstd1, ln1_g)
    g_out = _mm(merged, dz1b, name="mm_d_w_out", ta=True, out_dtype=BF16)
    d_m = _mm(dz1b, w_o, name="mm_d_merged", tb=True)
    dya, dyr, dgl_a, dgl_r, dbg_a, dbg_r = _merge_bwd(proj, gl_off, b_gate, y_attn, y_rnn, d_m)
    g_ap = _mm(a_out, dya, name="mm_d_w_attn_proj", ta=True, out_dtype=BF16)
    g_rp = _mm(b_out, dyr, name="mm_d_w_rnn_proj", ta=True, out_dtype=BF16)
    names3 = ["w_out", "w_attn_proj", "w_rnn_proj"]
    g3 = [g_out.reshape(N_SHARDS, D // N_SHARDS, D), g_ap.reshape(N_SHARDS, d_attn // N_SHARDS, D),
          g_rp.reshape(N_SHARDS, d_rnn // N_SHARDS, D)]
    d_a = _mm(dya, w_ap, name="mm_d_attn", tb=True)
    d_b, sib3 = _mm(dyr, w_rp, name="mm_d_rnn", tb=True, rider=_pair_rider(g3))
    part3 = pair_sums(g3, sib3, names3)
    (dq, dk, dv, dsink), landed2 = _attn_bwd(
        proj, d_a, attn_sinks, nq, (q_off, k_off, v_off),
        rider=_shard_exchange_rider(part2, _atoms([0], diag) + _atoms([1], near), landed2))
    (drx, dry, d_rcw, d_rcb, d_ba, d_bi, d_lam, d_wa_g, d_wi_g), (landed2_gate, *landed3) = _rnn_bwd(
        proj, (rx_off, ry_off), h_all, d_b, rcw, rnn_conv_b, wa_g, wi_g, lru_ba, lru_bi, lru_lambda,
        rider=_join_riders(_shard_exchange_rider(part2[1:], _atoms([0], diag), landed2[1:]),
                           _shard_exchange_rider(part3, _atoms([0, 1, 2], near))))
    halves["ffn_w_up"], halves["ffn_w_gate"] = shard_sums(part2, [landed2[0], landed2_gate],
                                                          ["ffn_w_up", "ffn_w_gate"])
    d_proj = jnp.concatenate([dq, dk.astype(BF16), dv.astype(BF16), drx, dry, dgl_a, dgl_r], axis=1)
    ffn_names = ["ffn_w_down", "ffn_w_up", "ffn_w_gate"]
    g_in, (*shared_ffn, lb_o, lb_a, lb_r) = _mm(
        x0b, d_proj, name="mm_d_w_in", ta=True, out_dtype=BF16, out_shards=N_SHARDS,
        rider=_join_riders(_share_rider([halves[n] for n in ffn_names]),
                           _shard_exchange_rider(part3, _atoms([0, 1, 2], diag), landed3)))
    halves["w_out"], halves["w_attn_proj"], halves["w_rnn_proj"] = shard_sums(part3, [lb_o, lb_a, lb_r], names3)

    small_parts = [
        ("loss", loss_acc[0:1, 0:1]),
        ("b_gate", jnp.concatenate([dbg_a, dbg_r], axis=1)),
        ("rnn_conv_w", d_rcw), ("rnn_conv_b", d_rcb),
        ("lru_wa", _ungroup_blocks(d_wa_g, per)), ("lru_ba", d_ba),
        ("lru_wi", _ungroup_blocks(d_wi_g, per)), ("lru_bi", d_bi), ("lru_lambda", d_lam),
        ("attn_sinks", dsink[0:1, 0:nq]),
        ("ln1_g", dg1), ("ln1_b", db1),
        ("ffn_conv_w", d_fcw), ("ffn_conv_b", d_fcb),
        ("ln2_g", dg2), ("ln2_b", db2),
    ]
    packed = _pack([p for _, p in small_parts])
    rs = packed.shape[0]

    def whole(g):
        return g.reshape(2 * g.shape[1], g.shape[2])

    grads = {n: whole(g) for n, g in zip(ffn_names, shared_ffn)}
    out_g, out_d, out_m, out_v = {}, {}, {}, {}

    def adamw(n, after=None):
        shape = weights[n].shape
        two_d = (math.prod(shape[:-1]), shape[-1])
        g2, d2, m2, v2 = _adamw(weights[n].reshape(two_d), grads[n].reshape(two_d), m_in[n].reshape(two_d),
                                v_in[n].reshape(two_d), "adamw_" + n, after=after)
        out_g[n], out_d[n] = g2.reshape(shape), d2.reshape(shape)
        out_m[n], out_v[n] = m2.reshape(shape), v2.reshape(shape)

    g4 = [g_in, packed.reshape(N_SHARDS, rs // N_SHARDS, LANES)]
    sib4 = _run_rider(_pair_rider(g4), "pair_exchange_in_small")
    part4 = pair_sums(g4, sib4, ["w_in", "small"])
    grad_x, (lb_in, lb_small, *shared_mix) = _mm(
        d_proj, w_in_s, name="mm_d_x", tb=True, b_shards=N_SHARDS, adds=((ALPHA, dz1),),
        rider=_join_riders(_shard_exchange_rider(part4, _atoms([0], near) + _atoms([1])),
                           _share_rider([halves[n] for n in names3])))
    grads.update({n: whole(g) for n, g in zip(names3, shared_mix)})
    send_sem, recv_sem, part_thru, lb_thru, token = _diag_exchange_start(part4[0], lb_in)
    for n in ffn_names + names3:
        adamw(n, after=token)
    part_in, lb_in = _diag_exchange_wait(send_sem, recv_sem, part_thru, lb_thru, out_d[names3[-1]])
    halves["w_in"], = shard_sums([part_in], [lb_in], ["w_in"])
    eighths = _shard_sum(part4[1], lb_small, jc_arr, "shard_sum_small", all_slots=True)
    shared_in, reduced = _run_rider(_share_rider([halves["w_in"]], eighths), "share_in_small")
    grads["w_in"] = whole(shared_in)
    reduced = reduced.reshape(rs, LANES)
    small = dict(zip([n for n, _ in small_parts], _unpack(reduced, [p.shape for _, p in small_parts])))
    loss = small.pop("loss").reshape(())
    rcw_n = d_rnn // N_SHARDS
    fcw_n = d_ff // N_SHARDS
    small["rnn_conv_w"] = lax.dynamic_slice(small["rnn_conv_w"], (0, j_me * rcw_n), (4, rcw_n))
    small["ffn_conv_w"] = lax.dynamic_slice(small["ffn_conv_w"], (0, j_me * fcw_n), (3, fcw_n))
    for n, g in small.items():
        grads[n] = g

    for n in order:
        if n not in out_g:
            adamw(n)

    return (loss, grad_x.reshape(x.shape), *[out_g[n] for n in order], *[out_d[n] for n in order],
            *[out_m[n] for n in order], *[out_v[n] for n in order])
```

```python
import functools
import math

import jax
import jax.numpy as jnp
from jax import lax
from jax.experimental import pallas as pl
from jax.experimental.pallas import tpu as pltpu

F32 = jnp.float32
BF16 = jnp.bfloat16
MESH = pl.DeviceIdType.MESH

HEAD_DIM = 64
GROUP = 8
ATTN_BLOCK = 128
LRU_C = 8.0
LN_EPS = 1e-5
ALPHA = 2.0 ** 0.25
LANES = 128
N_SHARDS = 4
N_DEV = 8
VMEM_LIMIT = 56 * 1024 * 1024
MM_VMEM_BUDGET = 40 * 1024 * 1024
MM_MAX_TILE = 3072
PACK_ROW_MULT = 8 * 64
NEG = -1e30

ADAM_LR, ADAM_B1, ADAM_B2, ADAM_EPS, ADAM_WD, ADAM_STEP = 0.001, 0.9, 0.999, 1e-08, 0.01, 10

GELU_C = math.sqrt(2.0 / math.pi)
GELU_A = 0.044715


def _cparams(sem=None):
    kw = dict(vmem_limit_bytes=VMEM_LIMIT)
    if sem is not None:
        kw["dimension_semantics"] = sem
    return pltpu.CompilerParams(**kw)


def _pick(n, prefs):
    for p in prefs:
        if n % p == 0:
            return p
    return n


def _row_tile(rows, row_bytes, mult, budget=2 * 1024 * 1024):
    best = None
    for d in range(mult, rows + 1, mult):
        if rows % d == 0 and d * row_bytes <= budget:
            best = d
    return best if best is not None else rows


def _gelu(x):
    return 0.5 * x * (1.0 + jnp.tanh(GELU_C * (x + GELU_A * x * x * x)))


def _gelu_and_grad(x):
    t = jnp.tanh(GELU_C * (x + GELU_A * x * x * x))
    g = 0.5 * x * (1.0 + t)
    dg = 0.5 * (1.0 + t) + 0.5 * x * (1.0 - t * t) * GELU_C * (1.0 + 3.0 * GELU_A * x * x)
    return g, dg


def _shift_down(x, s, fill=0.0):
    row = lax.broadcasted_iota(jnp.int32, x.shape, 0)
    return jnp.where(row >= s, pltpu.roll(x, s, 0), fill)


def _shift_up(x, s, fill=0.0):
    n = x.shape[0]
    row = lax.broadcasted_iota(jnp.int32, x.shape, 0)
    return jnp.where(row < n - s, pltpu.roll(x, n - s, 0), fill)


def _mm(a, b, *, name, ta=False, tb=False, out_dtype=F32, adds=(), b_shards=1, out_shards=1,
        tm=None, tn=None, tk=None, rider=None, after=None):
    if ta:
        K, M = a.shape
    else:
        M, K = a.shape
    if b_shards > 1:
        n_sh = b.shape[-1]
        if tb:
            N = b.shape[1]
            assert b_shards * n_sh == K
        else:
            N = b_shards * n_sh
            assert b.shape[1] == K
    else:
        n_sh = None
        if tb:
            N = b.shape[0]
            assert b.shape[1] == K
        else:
            N = b.shape[1]
            assert b.shape[0] == K
    wide = (1024, 1536, 1280, 768, 640, 512, 256, 128)
    if tn is None:
        if b_shards > 1 and not tb:
            tn = n_sh if n_sh <= MM_MAX_TILE else _pick(n_sh, wide)
        elif out_shards > 1:
            tn = N // out_shards if N // out_shards <= MM_MAX_TILE else _pick(N // out_shards, wide)
        else:
            tn = _pick(N, wide)
    if tk is None:
        if b_shards > 1 and tb:
            tk = n_sh if n_sh <= MM_MAX_TILE else _pick(n_sh, wide)
        else:
            tk = K if K <= MM_MAX_TILE else _pick(K, (2048,) + wide)
    assert N % tn == 0 and K % tk == 0, (name, M, N, K, tn, tk)
    nk = K // tk
    n_add = len(adds)
    sa, sb, so = a.dtype.itemsize, b.dtype.itemsize, jnp.dtype(out_dtype).itemsize

    def vmem_bytes(tm_):
        return (2 * (tm_ * tk * sa + tk * tn * sb + tm_ * tn * so + n_add * tm_ * tn * 4)
                + (tm_ * tn * 4 if nk > 1 else 0))

    if tm is None:
        tm = _pick(M, (1024, 512, 256, 128)) if nk > 1 else _pick(M, (512, 256, 128))
        while vmem_bytes(tm) > MM_VMEM_BUDGET and tm % 256 == 0:
            tm //= 2
    assert M % tm == 0, (name, M, tm)
    b_outer = b.size * sb >= a.size * sa

    def ij(g0, g1):
        return (g1, g0) if b_outer else (g0, g1)

    def amap(g0, g1, k):
        i, _ = ij(g0, g1)
        return (k, i) if ta else (i, k)

    def bmap(g0, g1, k):
        _, j = ij(g0, g1)
        if b_shards > 1 and not tb:
            per = n_sh // tn
            return (j // per, k, j % per)
        if b_shards > 1 and tb:
            per = n_sh // tk
            return (k // per, j, k % per)
        return (j, k) if tb else (k, j)

    def omap(g0, g1, k):
        i, j = ij(g0, g1)
        if out_shards > 1:
            per_o = (N // out_shards) // tn
            return (j // per_o, i, j % per_o)
        return (i, j)

    a_spec = pl.BlockSpec((tk, tm) if ta else (tm, tk), amap)
    if b_shards > 1:
        b_spec = pl.BlockSpec((None, tn, tk) if tb else (None, tk, tn), bmap)
    else:
        b_spec = pl.BlockSpec((tn, tk) if tb else (tk, tn), bmap)
    add_specs = [pl.BlockSpec((tm, tn), lambda g0, g1, k: ij(g0, g1)) for _ in adds]
    if out_shards > 1:
        out_spec = pl.BlockSpec((None, tm, tn), omap)
        out_shape = jax.ShapeDtypeStruct((out_shards, M, N // out_shards), out_dtype)
    else:
        out_spec = pl.BlockSpec((tm, tn), omap)
        out_shape = jax.ShapeDtypeStruct((M, N), out_dtype)

    if ta:
        dims = (((0,), (0,)), ((), ()))
    elif tb:
        dims = (((1,), (1,)), ((), ()))
    else:
        dims = (((1,), (0,)), ((), ()))
    scales = tuple(s for s, _ in adds)

    def finish(r, add_refs, o_ref):
        for s, ref in zip(scales, add_refs):
            r = r + s * ref[...].astype(F32)
        o_ref[...] = r.astype(out_dtype)

    def body(a_ref, b_ref, *rest):
        add_refs = rest[:n_add]
        o_ref = rest[n_add]
        part = lax.dot_general(a_ref[...].astype(BF16), b_ref[...].astype(BF16), dims, preferred_element_type=F32)
        if nk == 1:
            finish(part, add_refs, o_ref)
            return
        acc = rest[n_add + 1]
        k = pl.program_id(2)

        @pl.when(k == 0)
        def _():
            acc[...] = part

        @pl.when(k > 0)
        def _():
            acc[...] += part

        @pl.when(k == nk - 1)
        def _():
            finish(acc[...], add_refs, o_ref)

    grid = (N // tn, M // tm, nk) if b_outer else (M // tm, N // tn, nk)
    (res,), carried = _call(
        body, name=name, grid=grid, in_specs=[a_spec, b_spec] + add_specs, out_specs=[out_spec],
        out_shape=[out_shape], scratch_shapes=[pltpu.VMEM((tm, tn), F32)] if nk > 1 else [],
        args=(a, b, *[x for _, x in adds]), sem=("parallel", "parallel", "arbitrary"), rider=rider, after=after)
    return (res, carried) if rider is not None else res


def _cast_bf16(w, name):
    R, C = w.shape
    tr = _row_tile(R, C * 4, 16)

    def body(w_ref, o_ref):
        o_ref[...] = w_ref[...].astype(BF16)

    return pl.pallas_call(
        body, name=name, out_shape=jax.ShapeDtypeStruct((R, C), BF16), grid=(R // tr,),
        in_specs=[pl.BlockSpec((tr, C), lambda r: (r, 0))], out_specs=pl.BlockSpec((tr, C), lambda r: (r, 0)),
        compiler_params=_cparams(("parallel",)),
    )(w)


def _cast_bf16_into_slot(w, jc_arr, name):
    R, C = w.shape
    tr = _row_tile(R, C * 4, 16)

    def body(jc_ref, w_ref, o_ref):
        o_ref[...] = w_ref[...].astype(BF16)

    gs = pltpu.PrefetchScalarGridSpec(
        num_scalar_prefetch=1, grid=(R // tr,),
        in_specs=[pl.BlockSpec((tr, C), lambda r, jc: (r, 0))],
        out_specs=pl.BlockSpec((None, tr, C), lambda r, jc: (jc[0], r, 0)))
    return pl.pallas_call(body, name=name, out_shape=jax.ShapeDtypeStruct((N_SHARDS, R, C), BF16), grid_spec=gs,
                          compiler_params=_cparams(("parallel",)))(jc_arr, w)


def _pair_sum(g, la, jc_arr, name):
    S, R, C = g.shape
    half = R // 2
    tr = _row_tile(half, C * 4, 16)
    nrt = half // tr
    dt = g.dtype

    def body(jc_ref, g_ref, la_ref, o_ref):
        o_ref[...] = (g_ref[...].astype(F32) + la_ref[...].astype(F32)).astype(dt)

    gs = pltpu.PrefetchScalarGridSpec(
        num_scalar_prefetch=1, grid=(S, nrt),
        in_specs=[pl.BlockSpec((None, tr, C), lambda s, r, jc: (s, jc[1] * nrt + r, 0)),
                  pl.BlockSpec((None, tr, C), lambda s, r, jc: (s, r, 0))],
        out_specs=pl.BlockSpec((None, tr, C), lambda s, r, jc: (s, r, 0)))
    return pl.pallas_call(body, name=name, out_shape=jax.ShapeDtypeStruct((S, half, C), dt), grid_spec=gs,
                          compiler_params=_cparams(("parallel", "parallel")))(jc_arr, g, la)


def _shard_sum(cp, lb, jc_arr, name, all_slots=False):
    S, h, C = cp.shape
    tr = _row_tile(h, C * 4, 16)

    def body(jc_ref, cp_ref, l0, l1, l2, o_ref):
        o_ref[...] = ((cp_ref[...].astype(F32) + l0[...].astype(F32)) + l1[...].astype(F32)) + l2[...].astype(F32)

    def lspec(kk):
        return pl.BlockSpec((None, tr, C), lambda r, jc: (kk, r, 0))

    if all_slots:
        out_spec = pl.BlockSpec((None, None, tr, C), lambda r, jc: (jc[0], jc[1], r, 0))
        out_shape = jax.ShapeDtypeStruct((S, 2, h, C), F32)
    else:
        out_spec = pl.BlockSpec((None, tr, C), lambda r, jc: (jc[1], r, 0))
        out_shape = jax.ShapeDtypeStruct((2, h, C), F32)
    gs = pltpu.PrefetchScalarGridSpec(
        num_scalar_prefetch=1, grid=(h // tr,),
        in_specs=[pl.BlockSpec((None, tr, C), lambda r, jc: (jc[0], r, 0)), lspec(0), lspec(1), lspec(2)],
        out_specs=out_spec)
    return pl.pallas_call(body, name=name, out_shape=out_shape, grid_spec=gs,
                          compiler_params=_cparams(("parallel",)))(jc_arr, cp, lb, lb, lb)


ANY = pl.BlockSpec(memory_space=pl.ANY)


def _place():
    x, y, c = lax.axis_index("x"), lax.axis_index("y"), lax.axis_index("c")
    chips = [(1 - x, y), (x, 1 - y), (1 - x, 1 - y)]
    return x, y, c, chips


class _Rider:
    def __init__(self, inputs, out_shape, aliases, sems, start, finish):
        self.inputs, self.out_shape, self.aliases, self.sems = list(inputs), list(out_shape), dict(aliases), list(sems)
        self.start, self.finish = start, finish


def _join_riders(r1, r2):
    i1, o1, s1 = len(r1.inputs), len(r1.out_shape), len(r1.sems)
    aliases = dict(r1.aliases)
    aliases.update({i1 + i: o1 + o for i, o in r2.aliases.items()})

    def start(ins, outs, sems):
        r1.start(ins[:i1], outs[:o1], sems[:s1])
        r2.start(ins[i1:], outs[o1:], sems[s1:])

    def finish(ins, outs, sems):
        r1.finish(ins[:i1], outs[:o1], sems[:s1])
        r2.finish(ins[i1:], outs[o1:], sems[s1:])

    return _Rider(r1.inputs + r2.inputs, r1.out_shape + r2.out_shape, aliases, r1.sems + r2.sems, start, finish)


def _after_rider(x):
    return _Rider([x], [], {}, [], lambda *a: None, lambda *a: None)


def _call(body, *, name, grid, in_specs, out_specs, out_shape, scratch_shapes, args, sem, rider=None, after=None):
    out_specs, out_shape = tuple(out_specs), tuple(out_shape)
    if after is not None:
        rider = _after_rider(after) if rider is None else _join_riders(_after_rider(after), rider)
    if rider is None:
        res = pl.pallas_call(body, name=name, out_shape=out_shape, grid=grid, in_specs=list(in_specs),
                             out_specs=out_specs, scratch_shapes=list(scratch_shapes),
                             compiler_params=_cparams(sem))(*args)
        return tuple(res), []
    n_in, n_out, n_sc = len(in_specs), len(out_specs), len(scratch_shapes)
    r_in, r_out = len(rider.inputs), len(rider.out_shape)

    def wrapped(*refs):
        p = 0
        host_in = refs[p:p + n_in]; p += n_in
        rid_in = refs[p:p + r_in]; p += r_in
        host_out = refs[p:p + n_out]; p += n_out
        rid_out = refs[p:p + r_out]; p += r_out
        host_sc = refs[p:p + n_sc]; p += n_sc
        rid_sem = refs[p:]
        first = functools.reduce(jnp.logical_and, [pl.program_id(a) == 0 for a in range(len(grid))])
        last = functools.reduce(jnp.logical_and, [pl.program_id(a) == grid[a] - 1 for a in range(len(grid))])

        @pl.when(first)
        def _():
            rider.start(rid_in, rid_out, rid_sem)

        body(*host_in, *host_out, *host_sc)

        @pl.when(last)
        def _():
            rider.finish(rid_in, rid_out, rid_sem)

    res = pl.pallas_call(
        wrapped, name=name, out_shape=out_shape + tuple(rider.out_shape), grid=grid,
        in_specs=list(in_specs) + [ANY] * r_in, out_specs=out_specs + (ANY,) * r_out,
        input_output_aliases={n_in + i: n_out + o for i, o in rider.aliases.items()},
        scratch_shapes=list(scratch_shapes) + rider.sems,
        compiler_params=_cparams(("arbitrary",) * len(grid)),
    )(*args, *rider.inputs)
    return tuple(res[:n_out]), list(res[n_out:])


def _run_rider(rider, name):
    def body(*refs):
        r_in, r_out = len(rider.inputs), len(rider.out_shape)
        ins, outs, sems = refs[:r_in], refs[r_in:r_in + r_out], refs[r_in + r_out:]
        rider.start(ins, outs, sems)
        rider.finish(ins, outs, sems)

    return pl.pallas_call(
        body, name=name, out_shape=rider.out_shape, in_specs=[ANY] * len(rider.inputs),
        out_specs=[ANY] * len(rider.out_shape), input_output_aliases=rider.aliases, scratch_shapes=rider.sems,
    )(*rider.inputs)


def _atoms(indices, kks=(0, 1, 2), q=0, nq=1):
    return [(i, kk, q, nq) for i in indices for kk in kks]


def _gather_rider(bufs, atoms=None):
    n = len(bufs)
    if atoms is None:
        atoms = _atoms(range(n))
    na = len(atoms)

    def rows(out, atom, core):
        i, _, q, nq = atom
        half = out[i].shape[1] // 2
        assert half % (16 * nq) == 0, (half, nq)
        return pl.ds(core * half + q * (half // nq), half // nq)

    def ici_copy(out, sems, a, slot, peer):
        c = lax.axis_index("c")
        blk = out[atoms[a][0]].at[slot, rows(out, atoms[a], c), :]
        return pltpu.make_async_remote_copy(
            src_ref=blk, dst_ref=blk, send_sem=sems[0].at[a], recv_sem=sems[1].at[a],
            device_id=(peer[0], peer[1], c), device_id_type=MESH)

    def d2d_copy(out, sems, a, slot, from_core):
        x, y, c, _ = _place()
        blk = out[atoms[a][0]].at[slot, rows(out, atoms[a], from_core), :]
        return pltpu.make_async_remote_copy(
            src_ref=blk, dst_ref=blk, send_sem=sems[2].at[a], recv_sem=sems[3].at[a],
            device_id=(x, y, 1 - c), device_id_type=MESH)

    def start(ins, out, sems):
        x, y, c, chips = _place()
        for a in range(na):
            ici_copy(out, sems, a, 2 * x + y, chips[atoms[a][1]]).start()

    def finish(ins, out, sems):
        x, y, c, chips = _place()
        src = [2 * chips[atoms[a][1]][0] + chips[atoms[a][1]][1] for a in range(na)]
        for a in range(na):
            ici_copy(out, sems, a, src[a], chips[atoms[a][1]]).wait_recv()
            d2d_copy(out, sems, a, src[a], c).start()
        for a in range(na):
            d2d_copy(out, sems, a, src[a], 1 - c).wait_recv()
        for a in range(na):
            ici_copy(out, sems, a, 2 * x + y, chips[atoms[a][1]]).wait_send()
            d2d_copy(out, sems, a, src[a], c).wait_send()

    return _Rider(bufs, [jax.ShapeDtypeStruct(s.shape, s.dtype) for s in bufs], {i: i for i in range(n)},
                  [pltpu.SemaphoreType.DMA((na,))] * 4, start, finish)


def _mm_gathering(a, buf, order_arr, name):
    M, K = a.shape
    S, _, n = buf.shape
    tm = _pick(M, (512, 256, 128))
    n_i = M // tm
    half = K // 2

    def body(order_ref, a_ref, w_in_ref, o_ref, w_ref, b_vmem, load_sem, s_ici, r_ici, s_d2d, r_d2d):
        s, i = pl.program_id(0), pl.program_id(1)
        x, y, c, chips = _place()
        j_me = 2 * x + y
        slots = [2 * px + py for px, py in chips]

        def ici(kk, slot):
            blk = w_ref.at[slot, pl.ds(c * half, half), :]
            return pltpu.make_async_remote_copy(
                src_ref=blk, dst_ref=blk, send_sem=s_ici.at[kk], recv_sem=r_ici.at[kk],
                device_id=(chips[kk][0], chips[kk][1], c), device_id_type=MESH)

        def d2d(kk, from_core):
            blk = w_ref.at[slots[kk], pl.ds(from_core * half, half), :]
            return pltpu.make_async_remote_copy(
                src_ref=blk, dst_ref=blk, send_sem=s_d2d.at[kk], recv_sem=r_d2d.at[kk],
                device_id=(x, y, 1 - c), device_id_type=MESH)

        def load(slot, b):
            return pltpu.make_async_copy(w_ref.at[slot], b_vmem.at[b], load_sem.at[b])

        @pl.when(jnp.logical_and(s == 0, i == 0))
        def _():
            for kk in range(3):
                ici(kk, j_me).start()
            load(j_me, 0).start()

        @pl.when(i == 0)
        def _():
            load(order_ref[s], s % 2).wait()

        o_ref[...] = jnp.dot(a_ref[...], b_vmem[s % 2], preferred_element_type=F32)

        last = i == n_i - 1

        @pl.when(jnp.logical_and(last, s == 0))
        def _():
            ici(0, slots[0]).wait_recv()
            d2d(0, c).start()
            ici(1, slots[1]).wait_recv()
            d2d(1, c).start()
            d2d(0, 1 - c).wait_recv()
            load(slots[0], 1).start()

        @pl.when(jnp.logical_and(last, s == 1))
        def _():
            d2d(1, 1 - c).wait_recv()
            load(slots[1], 0).start()

        @pl.when(jnp.logical_and(last, s == 2))
        def _():
            ici(2, slots[2]).wait_recv()
            d2d(2, c).start()
            d2d(2, 1 - c).wait_recv()
            load(slots[2], 1).start()

        @pl.when(jnp.logical_and(last, s == 3))
        def _():
            for kk in range(3):
                ici(kk, j_me).wait_send()
                d2d(kk, c).wait_send()

    gs = pltpu.PrefetchScalarGridSpec(
        num_scalar_prefetch=1, grid=(S, n_i),
        in_specs=[pl.BlockSpec((tm, K), lambda s, i, order: (i, 0)), ANY],
        out_specs=[pl.BlockSpec((tm, n), lambda s, i, order: (i, order[s])), ANY],
        scratch_shapes=[pltpu.VMEM((2, K, n), BF16), pltpu.SemaphoreType.DMA((2,))]
        + [pltpu.SemaphoreType.DMA((3,))] * 4)
    return pl.pallas_call(
        body, name=name, grid_spec=gs,
        out_shape=[jax.ShapeDtypeStruct((M, S * n), F32), jax.ShapeDtypeStruct(buf.shape, buf.dtype)],
        input_output_aliases={2: 1}, compiler_params=_cparams(("arbitrary", "arbitrary")),
    )(order_arr, a, buf)


def _all_gather_small(shards):
    n = len(shards)

    def body(*refs):
        w = refs[:n]
        out = refs[n:2 * n]
        local_sem, s_sem, r_sem = refs[2 * n:]
        x, y, c, chips = _place()
        j_me = 2 * x + y
        cps = []
        for i in range(n):
            lc = pltpu.make_async_copy(w[i], out[i].at[j_me], local_sem.at[i])
            lc.start()
            cps.append(lc)
        sends = []
        for i in range(n):
            for kk, (px, py) in enumerate(chips):
                cp = pltpu.make_async_remote_copy(
                    src_ref=w[i], dst_ref=out[i].at[j_me], send_sem=s_sem.at[3 * i + kk],
                    recv_sem=r_sem.at[3 * i + kk], device_id=(px, py, c), device_id_type=MESH)
                cp.start()
                sends.append(cp)
        for i in range(n):
            for kk, (px, py) in enumerate(chips):
                sends[3 * i + kk].wait_send()
                pltpu.make_async_remote_copy(
                    src_ref=w[i], dst_ref=out[i].at[2 * px + py], send_sem=s_sem.at[3 * i + kk],
                    recv_sem=r_sem.at[3 * i + kk], device_id=(px, py, c), device_id_type=MESH).wait_recv()
        for lc in cps:
            lc.wait()

    out_shape = [jax.ShapeDtypeStruct((N_SHARDS,) + s.shape, s.dtype) for s in shards]
    return pl.pallas_call(
        body, name="all_gather_conv_weights", out_shape=out_shape, in_specs=[ANY] * n, out_specs=[ANY] * n,
        scratch_shapes=[pltpu.SemaphoreType.DMA((n,)), pltpu.SemaphoreType.DMA((3 * n,)),
                        pltpu.SemaphoreType.DMA((3 * n,))],
    )(*shards)


def _pair_rider(grads):
    n = len(grads)

    def copies(g, la, sems):
        x, y, c, _ = _place()
        return [pltpu.make_async_remote_copy(
            src_ref=g[i].at[:, pl.ds((1 - c) * (g[i].shape[1] // 2), g[i].shape[1] // 2), :], dst_ref=la[i],
            send_sem=sems[0].at[i], recv_sem=sems[1].at[i], device_id=(x, y, 1 - c), device_id_type=MESH)
            for i in range(n)]

    def start(g, la, sems):
        for cp in copies(g, la, sems):
            cp.start()

    def finish(g, la, sems):
        for cp in copies(g, la, sems):
            cp.wait()

    return _Rider(grads, [jax.ShapeDtypeStruct((s.shape[0], s.shape[1] // 2, s.shape[2]), s.dtype) for s in grads],
                  {}, [pltpu.SemaphoreType.DMA((n,)), pltpu.SemaphoreType.DMA((n,))], start, finish)


HBM = pl.BlockSpec(memory_space=pltpu.HBM)
SEM = pl.BlockSpec(memory_space=pltpu.SEMAPHORE)


def _shard_copies(part_refs, land_refs, send_sems, recv_sems):
    x, y, c, chips = _place()
    return [pltpu.make_async_remote_copy(
        src_ref=part_refs[i].at[2 * px + py], dst_ref=land_refs[i].at[kk],
        send_sem=send_sems.at[3 * i + kk], recv_sem=recv_sems.at[3 * i + kk],
        device_id=(px, py, c), device_id_type=MESH)
        for i in range(len(part_refs)) for kk, (px, py) in enumerate(chips)]


SIDE_EFFECT = pltpu.SideEffectType.DATAFLOW_SIDE_EFFECTING


def _shard_exchange_start(parts, name):
    n = len(parts)

    def body(*refs):
        part_refs, land_refs = refs[:n], refs[n:2 * n]
        send_sems, recv_sems = refs[2 * n], refs[2 * n + 1]
        token = refs[4 * n + 2]
        for cp in _shard_copies(part_refs, land_refs, send_sems, recv_sems):
            cp.start()
        token[...] = jnp.zeros_like(token)

    lands = [lax.empty((3,) + p.shape[1:], p.dtype) for p in parts]
    bufs = list(parts) + lands
    res = pl.pallas_call(
        body, name=name,
        out_shape=(pltpu.SemaphoreType.DMA((3 * n,)), pltpu.SemaphoreType.DMA((3 * n,)),
                   *[pltpu.HBM(b.shape, b.dtype) for b in bufs], jax.ShapeDtypeStruct((8, LANES), F32)),
        in_specs=(HBM,) * (2 * n), out_specs=(SEM, SEM) + (HBM,) * (2 * n) + (pl.BlockSpec(memory_space=pltpu.VMEM),),
        input_output_aliases={i: 2 + i for i in range(2 * n)},
        compiler_params=pltpu.CompilerParams(has_side_effects=SIDE_EFFECT),
    )(*[pltpu.with_memory_space_constraint(b, pltpu.HBM) for b in bufs])
    return res[0], res[1], list(res[2:2 + n]), list(res[2 + n:2 + 2 * n]), res[2 + 2 * n]


def _shard_exchange_wait(started, after, name):
    send_sems, recv_sems, parts, lands, _ = started
    n = len(parts)

    def body(*refs):
        part_refs, land_refs = refs[:n], refs[n:2 * n]
        send_sems_ref, recv_sems_ref = refs[2 * n], refs[2 * n + 1]
        for cp in _shard_copies(part_refs, land_refs, send_sems_ref, recv_sems_ref):
            cp.wait_send()
            cp.wait_recv()

    bufs = parts + lands
    res = pl.pallas_call(
        body, name=name, out_shape=tuple(pltpu.HBM(b.shape, b.dtype) for b in bufs),
        in_specs=(HBM,) * (2 * n) + (SEM, SEM, ANY), out_specs=(HBM,) * (2 * n),
        input_output_aliases={i: i for i in range(2 * n)},
        compiler_params=pltpu.CompilerParams(has_side_effects=SIDE_EFFECT),
    )(*bufs, send_sems, recv_sems, after)
    return list(res[:n]), list(res[n:])


def _share_rider(halves, eighths=None):
    n = len(halves)
    bufs = list(halves) + ([eighths] if eighths is not None else [])

    def half_copy(out, sems, i, core):
        x, y, c, _ = _place()
        blk = out[i].at[core]
        return pltpu.make_async_remote_copy(src_ref=blk, dst_ref=blk, send_sem=sems[0].at[i], recv_sem=sems[1].at[i],
                                            device_id=(x, y, 1 - c), device_id_type=MESH)

    def eighth_copy(out, sems, r, mine):
        x, y, c, _ = _place()
        px, py, pc = x ^ ((r >> 2) & 1), y ^ ((r >> 1) & 1), c ^ (r & 1)
        blk = out[n].at[2 * x + y, c] if mine else out[n].at[2 * px + py, pc]
        return pltpu.make_async_remote_copy(src_ref=blk, dst_ref=blk, send_sem=sems[2].at[r - 1],
                                            recv_sem=sems[3].at[r - 1], device_id=(px, py, pc), device_id_type=MESH)

    def start(ins, out, sems):
        c = lax.axis_index("c")
        for i in range(n):
            half_copy(out, sems, i, c).start()
        if eighths is not None:
            for r in range(1, N_DEV):
                eighth_copy(out, sems, r, True).start()

    def finish(ins, out, sems):
        c = lax.axis_index("c")
        for i in range(n):
            half_copy(out, sems, i, 1 - c).wait_recv()
        if eighths is not None:
            for r in range(1, N_DEV):
                eighth_copy(out, sems, r, False).wait_recv()
        for i in range(n):
            half_copy(out, sems, i, c).wait_send()
        if eighths is not None:
            for r in range(1, N_DEV):
                eighth_copy(out, sems, r, True).wait_send()

    return _Rider(bufs, [jax.ShapeDtypeStruct(s.shape, s.dtype) for s in bufs], {i: i for i in range(len(bufs))},
                  [pltpu.SemaphoreType.DMA((max(n, 1),)), pltpu.SemaphoreType.DMA((max(n, 1),)),
                   pltpu.SemaphoreType.DMA((N_DEV - 1,)), pltpu.SemaphoreType.DMA((N_DEV - 1,))], start, finish)


ATTN_ROWS = GROUP * ATTN_BLOCK
ATTN_KEYS = 2 * ATTN_BLOCK


def _attn_geometry(n):
    row = lax.broadcasted_iota(jnp.int32, (ATTN_ROWS, ATTN_KEYS), 0)
    col = lax.broadcasted_iota(jnp.int32, (ATTN_ROWS, ATTN_KEYS), 1)
    dist = ATTN_BLOCK + jnp.bitwise_and(row, ATTN_BLOCK - 1) - col
    valid = jnp.logical_and(jnp.logical_and(dist >= 0, dist < ATTN_BLOCK),
                            jnp.logical_or(col >= ATTN_BLOCK, n > 0))
    return dist.astype(F32), valid


def _per_head_column(values):
    head = lax.broadcasted_iota(jnp.int32, (ATTN_ROWS, 1), 0) // ATTN_BLOCK
    col = jnp.zeros((ATTN_ROWS, 1), F32)
    for hh, v in enumerate(values):
        col = jnp.where(head == hh, v, col)
    return col


def _stack_heads(ref, g):
    return jnp.concatenate(
        [ref[:, (g * GROUP + hh) * HEAD_DIM:(g * GROUP + hh + 1) * HEAD_DIM].astype(BF16) for hh in range(GROUP)],
        axis=0)


def _attn_probs(q_s, k2, slope_col, sink_col, dist, valid):
    s = lax.dot_general(q_s, k2, (((1,), (1,)), ((), ())), preferred_element_type=F32) * (HEAD_DIM ** -0.5)
    s = jnp.where(valid, s - slope_col * dist, NEG)
    m = jnp.maximum(jnp.max(s, axis=1, keepdims=True), sink_col)
    e = jnp.exp(s - m)
    es = jnp.exp(sink_col - m)
    inv = 1.0 / (jnp.sum(e, axis=1, keepdims=True) + es)
    return e * inv, es * inv


def _attn_specs(T, d_attn, d_kv, q_blk, k_blk, v_blk):
    bq = pl.BlockSpec((ATTN_BLOCK, d_attn), lambda n: (n, q_blk))
    kp = pl.BlockSpec((ATTN_BLOCK, d_kv), lambda n: (jnp.maximum(n - 1, 0), k_blk))
    kc = pl.BlockSpec((ATTN_BLOCK, d_kv), lambda n: (n, k_blk))
    vp = pl.BlockSpec((ATTN_BLOCK, d_kv), lambda n: (jnp.maximum(n - 1, 0), v_blk))
    vc = pl.BlockSpec((ATTN_BLOCK, d_kv), lambda n: (n, v_blk))
    return bq, kp, kc, vp, vc


def _attn_fwd(proj, sinks, nq, cols, rider=None):
    T = proj.shape[0]
    nkv = nq // GROUP
    d_attn, d_kv = nq * HEAD_DIM, nkv * HEAD_DIM
    q_off, k_off, v_off = cols
    bq, kp, kc, vp, vc = _attn_specs(T, d_attn, d_kv, q_off // d_attn, k_off // d_kv, v_off // d_kv)

    def body(sink_ref, q_ref, kp_ref, kc_ref, vp_ref, vc_ref, o_ref):
        n = pl.program_id(0)
        dist, valid = _attn_geometry(n)
        for g in range(nkv):
            ks = slice(g * HEAD_DIM, (g + 1) * HEAD_DIM)
            k2 = jnp.concatenate([kp_ref[:, ks], kc_ref[:, ks]], axis=0).astype(BF16)
            v2 = jnp.concatenate([vp_ref[:, ks], vc_ref[:, ks]], axis=0).astype(BF16)
            slope_col = _per_head_column([2.0 ** (-8.0 * (g * GROUP + hh + 1) / nq) for hh in range(GROUP)])
            sink_col = _per_head_column([sink_ref[0, g * GROUP + hh] for hh in range(GROUP)])
            p, _ = _attn_probs(_stack_heads(q_ref, g), k2, slope_col, sink_col, dist, valid)
            o = jnp.dot(p.astype(BF16), v2, preferred_element_type=F32).astype(BF16)
            for hh in range(GROUP):
                h = g * GROUP + hh
                o_ref[:, h * HEAD_DIM:(h + 1) * HEAD_DIM] = o[hh * ATTN_BLOCK:(hh + 1) * ATTN_BLOCK, :]

    (out,), carried = _call(
        body, name="attn_fwd", out_shape=[jax.ShapeDtypeStruct((T, d_attn), BF16)], grid=(T // ATTN_BLOCK,),
        in_specs=[pl.BlockSpec(memory_space=pltpu.SMEM), bq, kp, kc, vp, vc],
        out_specs=[pl.BlockSpec((ATTN_BLOCK, d_attn), lambda n: (n, 0))], scratch_shapes=[],
        args=(sinks, proj, proj, proj, proj, proj), sem=("parallel",), rider=rider)
    return out, carried


def _attn_bwd(proj, d_attn_out, sinks, nq, cols, after=None):
    T = proj.shape[0]
    nkv = nq // GROUP
    d_attn, d_kv = nq * HEAD_DIM, nkv * HEAD_DIM
    q_off, k_off, v_off = cols
    bq, kp, kc, vp, vc = _attn_specs(T, d_attn, d_kv, q_off // d_attn, k_off // d_kv, v_off // d_kv)
    scale = HEAD_DIM ** -0.5
    dn_t = (((1,), (1,)), ((), ()))
    dn_r = (((0,), (0,)), ((), ()))

    def body(sink_ref, q_ref, kp_ref, kc_ref, vp_ref, vc_ref, do_ref, dq_ref, dk_ref, dv_ref, ds_ref):
        n = pl.program_id(0)

        @pl.when(n == 0)
        def _():
            dk_ref[...] = jnp.zeros_like(dk_ref)
            dv_ref[...] = jnp.zeros_like(dv_ref)
            ds_ref[...] = jnp.zeros_like(ds_ref)

        dist, valid = _attn_geometry(n)
        rows_c = pl.ds(pl.multiple_of(n * ATTN_BLOCK, ATTN_BLOCK), ATTN_BLOCK)
        rows_p = pl.ds(pl.multiple_of(jnp.maximum(n - 1, 0) * ATTN_BLOCK, ATTN_BLOCK), ATTN_BLOCK)
        lane = lax.broadcasted_iota(jnp.int32, ds_ref.shape, 1)
        srow = lax.broadcasted_iota(jnp.int32, ds_ref.shape, 0)
        ds_acc = jnp.zeros(ds_ref.shape, F32)
        for g in range(nkv):
            ks = slice(g * HEAD_DIM, (g + 1) * HEAD_DIM)
            k2 = jnp.concatenate([kp_ref[:, ks], kc_ref[:, ks]], axis=0).astype(BF16)
            v2 = jnp.concatenate([vp_ref[:, ks], vc_ref[:, ks]], axis=0).astype(BF16)
            slope_col = _per_head_column([2.0 ** (-8.0 * (g * GROUP + hh + 1) / nq) for hh in range(GROUP)])
            sink_col = _per_head_column([sink_ref[0, g * GROUP + hh] for hh in range(GROUP)])
            q_s = _stack_heads(q_ref, g)
            do_s = _stack_heads(do_ref, g)
            p, p_sink = _attn_probs(q_s, k2, slope_col, sink_col, dist, valid)
            dp = lax.dot_general(do_s, v2, dn_t, preferred_element_type=F32)
            delta = jnp.sum(p * dp, axis=1, keepdims=True)
            ds = (p * (dp - delta)).astype(BF16)
            sink_part = p_sink * delta
            dq = (jnp.dot(ds, k2, preferred_element_type=F32) * scale).astype(BF16)
            for hh in range(GROUP):
                h = g * GROUP + hh
                blk = slice(hh * ATTN_BLOCK, (hh + 1) * ATTN_BLOCK)
                dq_ref[:, h * HEAD_DIM:(h + 1) * HEAD_DIM] = dq[blk, :]
                ds_acc = ds_acc + jnp.where(jnp.logical_and(lane == h, srow == 0), -jnp.sum(sink_part[blk, :]), 0.0)
            dk2 = lax.dot_general(ds, q_s, dn_r, preferred_element_type=F32) * scale
            dv2 = lax.dot_general(p.astype(BF16), do_s, dn_r, preferred_element_type=F32)
            dk_ref[rows_p, ks] += dk2[:ATTN_BLOCK, :]
            dv_ref[rows_p, ks] += dv2[:ATTN_BLOCK, :]
            dk_ref[rows_c, ks] += dk2[ATTN_BLOCK:, :]
            dv_ref[rows_c, ks] += dv2[ATTN_BLOCK:, :]
        ds_ref[...] += ds_acc

    out_shape = (jax.ShapeDtypeStruct((T, d_attn), BF16), jax.ShapeDtypeStruct((T, d_kv), F32),
                 jax.ShapeDtypeStruct((T, d_kv), F32), jax.ShapeDtypeStruct((8, LANES), F32))
    return _call(
        body, name="attn_bwd", out_shape=out_shape, grid=(T // ATTN_BLOCK,),
        in_specs=[pl.BlockSpec(memory_space=pltpu.SMEM), bq, kp, kc, vp, vc,
                  pl.BlockSpec((ATTN_BLOCK, d_attn), lambda n: (n, 0))],
        out_specs=(pl.BlockSpec((ATTN_BLOCK, d_attn), lambda n: (n, 0)),
                   pl.BlockSpec((T, d_kv), lambda n: (0, 0)), pl.BlockSpec((T, d_kv), lambda n: (0, 0)),
                   pl.BlockSpec((8, LANES), lambda n: (0, 0))),
        scratch_shapes=[], args=(sinks, proj, proj, proj, proj, proj, d_attn_out), sem=("arbitrary",), after=after)[0]


def _rnn_tile(T):
    return _pick(T, (256, 128))


def _rnn_gates(x_ext, cw_ref, cb_ref, wa_ref, wi_ref, ba_ref, bi_ref, lam_ref, tt):
    xs = [pltpu.roll(x_ext, 3 - k, 0)[8:, :] if k < 3 else x_ext[8:, :] for k in range(4)]
    cx = cb_ref[...] + xs[0] * cw_ref[0:1, :]
    for k in range(1, 4):
        cx = cx + xs[k] * cw_ref[k:k + 1, :]
    cxb = cx.astype(BF16)
    r = jax.nn.sigmoid(jnp.dot(cxb, wa_ref[...], preferred_element_type=F32) + ba_ref[...])
    i = jax.nn.sigmoid(jnp.dot(cxb, wi_ref[...], preferred_element_type=F32) + bi_ref[...])
    lam = lam_ref[...]
    sp = jnp.maximum(-lam, 0.0) + jnp.log1p(jnp.exp(-jnp.abs(lam)))
    log_a = -LRU_C * r * sp
    a = jnp.exp(log_a)
    z = 2.0 * log_a
    em1 = jnp.where(z > -1e-2, z * (1.0 + z * (0.5 + z * (1.0 / 6.0 + z * (1.0 / 24.0)))), jnp.exp(z) - 1.0)
    s = jnp.sqrt(-em1)
    return xs, cx, r, i, sp, a, s


def _rnn_specs(T, gw, tt, rx_blk, ry_blk, rev):
    nT = T // tt
    hb = tt // 8

    def tile(t):
        return (nT - 1 - t) if rev else t

    rx = pl.BlockSpec((tt, gw), lambda g, t: (tile(t), rx_blk + g))
    rx_halo = pl.BlockSpec((8, gw), lambda g, t: (jnp.maximum(tile(t) * hb - 1, 0), rx_blk + g))
    ry = pl.BlockSpec((tt, gw), lambda g, t: (tile(t), ry_blk + g))
    cw = pl.BlockSpec((4, gw), lambda g, t: (0, g))
    vec = pl.BlockSpec((1, gw), lambda g, t: (0, g))
    wg = pl.BlockSpec((None, gw, gw), lambda g, t: (g, 0, 0))
    act = pl.BlockSpec((tt, gw), lambda g, t: (tile(t), g))
    act_halo = pl.BlockSpec((8, gw), lambda g, t: (jnp.maximum(tile(t) * hb - 1, 0), g))
    return rx, rx_halo, ry, cw, vec, wg, act, act_halo, tile


def _rnn_fwd(proj, cols, conv_w, conv_b, wa_g, wi_g, ba, bi, lam, rider=None):
    T = proj.shape[0]
    G, gw, _ = wa_g.shape
    d_rnn = G * gw
    tt = _rnn_tile(T)
    rx_off, ry_off = cols
    rx, rx_halo, ry, cw, vec, wg, act, _, _ = _rnn_specs(T, gw, tt, rx_off // gw, ry_off // gw, False)

    def body(rx_ref, rxh_ref, ry_ref, cw_ref, cb_ref, wa_ref, wi_ref, ba_ref, bi_ref, lam_ref,
             b_ref, h_ref, carry):
        t = pl.program_id(1)

        @pl.when(t == 0)
        def _():
            carry[...] = jnp.zeros_like(carry)

        halo = jnp.where(t > 0, rxh_ref[...], 0.0)
        x_ext = jnp.concatenate([halo, rx_ref[...]], axis=0)
        _, cx, _, i, _, a, s = _rnn_gates(x_ext, cw_ref, cb_ref, wa_ref, wi_ref, ba_ref, bi_ref, lam_ref, tt)
        acc_a, acc_b = a, s * (i * cx)
        d = 1
        while d < tt:
            acc_b = acc_a * _shift_down(acc_b, d, 0.0) + acc_b
            acc_a = acc_a * _shift_down(acc_a, d, 1.0)
            d *= 2
        h = acc_b + acc_a * carry[7:8, :]
        carry[...] = h[tt - 8:, :]
        h_ref[...] = h
        b_ref[...] = (h * _gelu(ry_ref[...])).astype(BF16)

    return _call(
        body, name="rnn_fwd",
        out_shape=(jax.ShapeDtypeStruct((T, d_rnn), BF16), jax.ShapeDtypeStruct((T, d_rnn), F32)),
        grid=(G, T // tt),
        in_specs=[rx, rx_halo, ry, cw, vec, wg, wg, vec, vec, vec], out_specs=(act, act),
        scratch_shapes=[pltpu.VMEM((8, gw), F32)],
        args=(proj, proj, proj, conv_w, conv_b, wa_g, wi_g, ba, bi, lam), sem=("parallel", "arbitrary"), rider=rider)


def _rnn_bwd(proj, cols, h_all, d_b, conv_w, conv_b, wa_g, wi_g, ba, bi, lam, rider=None):
    T = proj.shape[0]
    G, gw, _ = wa_g.shape
    d_rnn = G * gw
    tt = _rnn_tile(T)
    nT = T // tt
    rx_off, ry_off = cols
    rx, rx_halo, ry, cw, vec, wg, act, act_halo, _ = _rnn_specs(T, gw, tt, rx_off // gw, ry_off // gw, True)
    dn_t = (((1,), (1,)), ((), ()))
    dn_r = (((0,), (0,)), ((), ()))

    def body(rx_ref, rxh_ref, ry_ref, h_ref, hh_ref, db_ref, cw_ref, cb_ref, wa_ref, wi_ref, ba_ref, bi_ref, lam_ref,
             drx_ref, dry_ref, dcw_ref, dcb_ref, dba_ref, dbi_ref, dlam_ref, dwa_ref, dwi_ref,
             lam_carry, dcx_carry):
        t = pl.program_id(1)
        first_tile = t == nT - 1

        @pl.when(t == 0)
        def _():
            lam_carry[...] = jnp.zeros_like(lam_carry)
            dcx_carry[...] = jnp.zeros_like(dcx_carry)
            dcw_ref[...] = jnp.zeros_like(dcw_ref)
            dcb_ref[...] = jnp.zeros_like(dcb_ref)
            dba_ref[...] = jnp.zeros_like(dba_ref)
            dbi_ref[...] = jnp.zeros_like(dbi_ref)
            dlam_ref[...] = jnp.zeros_like(dlam_ref)
            dwa_ref[...] = jnp.zeros_like(dwa_ref)
            dwi_ref[...] = jnp.zeros_like(dwi_ref)

        halo = jnp.where(first_tile, 0.0, rxh_ref[...])
        x_ext = jnp.concatenate([halo, rx_ref[...]], axis=0)
        xs, cx, r, i, sp, a, s = _rnn_gates(x_ext, cw_ref, cb_ref, wa_ref, wi_ref, ba_ref, bi_ref, lam_ref, tt)
        h = h_ref[...]
        h_halo = jnp.where(first_tile, 0.0, hh_ref[...])
        h_prev = pltpu.roll(jnp.concatenate([h_halo, h], axis=0), 1, 0)[8:, :]
        gel, dgel = _gelu_and_grad(ry_ref[...])
        d_b_t = db_ref[...]
        dry_ref[...] = (d_b_t * h * dgel).astype(BF16)
        dh = d_b_t * gel

        acc_c = _shift_up(a, 1, 1.0)
        acc_l = dh
        d = 1
        while d < tt:
            acc_l = acc_c * _shift_up(acc_l, d, 0.0) + acc_l
            acc_c = acc_c * _shift_up(acc_c, d, 1.0)
            d *= 2
        lam_t = acc_l + acc_c * lam_carry[0:1, :]
        lam_carry[...] = (a * lam_t)[0:8, :]

        icx = i * cx
        d_s = lam_t * icx
        d_i = lam_t * s * cx
        dcx = lam_t * s * i
        d_a = lam_t * h_prev - d_s * (a / s)
        dlog_a = d_a * a
        d_r = dlog_a * (-LRU_C * sp)
        lam = lam_ref[...]
        dlam_ref[...] += jnp.sum(dlog_a * r, axis=0, keepdims=True) * (LRU_C * jax.nn.sigmoid(-lam))
        dpr = d_r * r * (1.0 - r)
        dpi = d_i * i * (1.0 - i)
        dba_ref[...] += jnp.sum(dpr, axis=0, keepdims=True)
        dbi_ref[...] += jnp.sum(dpi, axis=0, keepdims=True)
        cxb = cx.astype(BF16)
        dprb, dpib = dpr.astype(BF16), dpi.astype(BF16)
        dwa_ref[...] += lax.dot_general(cxb, dprb, dn_r, preferred_element_type=F32)
        dwi_ref[...] += lax.dot_general(cxb, dpib, dn_r, preferred_element_type=F32)
        dcx = (dcx + lax.dot_general(dprb, wa_ref[...], dn_t, preferred_element_type=F32)
               + lax.dot_general(dpib, wi_ref[...], dn_t, preferred_element_type=F32))

        dcb_ref[...] += jnp.sum(dcx, axis=0, keepdims=True)
        for k in range(4):
            dcw_ref[k:k + 1, :] += jnp.sum(dcx * xs[k], axis=0, keepdims=True)
        d_ext = jnp.concatenate([dcx, dcx_carry[...]], axis=0)
        drx = dcx * cw_ref[3:4, :]
        for k in range(3):
            drx = drx + pltpu.roll(d_ext, tt + 8 - (3 - k), 0)[:tt, :] * cw_ref[k:k + 1, :]
        drx_ref[...] = drx.astype(BF16)
        dcx_carry[...] = dcx[0:8, :]

    out_shape = (jax.ShapeDtypeStruct((T, d_rnn), BF16), jax.ShapeDtypeStruct((T, d_rnn), BF16),
                 jax.ShapeDtypeStruct((4, d_rnn), F32), jax.ShapeDtypeStruct((1, d_rnn), F32),
                 jax.ShapeDtypeStruct((1, d_rnn), F32), jax.ShapeDtypeStruct((1, d_rnn), F32),
                 jax.ShapeDtypeStruct((1, d_rnn), F32), jax.ShapeDtypeStruct((G, gw, gw), F32),
                 jax.ShapeDtypeStruct((G, gw, gw), F32))
    return _call(
        body, name="rnn_bwd", out_shape=out_shape, grid=(G, nT),
        in_specs=[rx, rx_halo, ry, act, act_halo, act, cw, vec, wg, wg, vec, vec, vec],
        out_specs=(act, act, cw, vec, vec, vec, vec, wg, wg),
        scratch_shapes=[pltpu.VMEM((8, gw), F32), pltpu.VMEM((8, gw), F32)],
        args=(proj, proj, proj, h_all, h_all, d_b, conv_w, conv_b, wa_g, wi_g, ba, bi, lam),
        sem=("parallel", "arbitrary"), rider=rider)


def _merge_fwd(proj, gl_off, b_gate, y_attn, y_rnn, rider=None):
    T, D = y_attn.shape
    tm = _pick(T, (256, 128))
    ct = _pick(math.gcd(gl_off, D), (512, 256, 128))
    oa, orr, nd = gl_off // ct, (gl_off + D) // ct, D // ct

    def body(ga_ref, gr_ref, ba_ref, br_ref, ya_ref, yr_ref, m_ref):
        ga = jax.nn.sigmoid(ga_ref[...] + ba_ref[...])
        gr = jax.nn.sigmoid(gr_ref[...] + br_ref[...])
        m_ref[...] = (ga * ya_ref[...] + gr * yr_ref[...]).astype(BF16)

    blk = pl.BlockSpec((tm, ct), lambda i, j: (i, j))
    (merged,), carried = _call(
        body, name="merge_fwd", out_shape=[jax.ShapeDtypeStruct((T, D), BF16)], grid=(T // tm, nd),
        in_specs=[pl.BlockSpec((tm, ct), lambda i, j: (i, oa + j)), pl.BlockSpec((tm, ct), lambda i, j: (i, orr + j)),
                  pl.BlockSpec((1, ct), lambda i, j: (0, j)), pl.BlockSpec((1, ct), lambda i, j: (0, nd + j)),
                  blk, blk],
        out_specs=[blk], scratch_shapes=[], args=(proj, proj, b_gate, b_gate, y_attn, y_rnn),
        sem=("parallel", "parallel"), rider=rider)
    return merged, carried


def _merge_bwd(proj, gl_off, b_gate, y_attn, y_rnn, d_m):
    T, D = y_attn.shape
    tm = _pick(T, (256, 128))
    ct = _pick(math.gcd(gl_off, D), (512, 256, 128))
    oa, orr, nd = gl_off // ct, (gl_off + D) // ct, D // ct

    def body(ga_ref, gr_ref, ba_ref, br_ref, ya_ref, yr_ref, dm_ref,
             dya_ref, dyr_ref, dga_ref, dgr_ref, dba_ref, dbr_ref):
        i = pl.program_id(1)

        @pl.when(i == 0)
        def _():
            dba_ref[...] = jnp.zeros_like(dba_ref)
            dbr_ref[...] = jnp.zeros_like(dbr_ref)

        ga = jax.nn.sigmoid(ga_ref[...] + ba_ref[...])
        gr = jax.nn.sigmoid(gr_ref[...] + br_ref[...])
        dm = dm_ref[...]
        dya_ref[...] = (dm * ga).astype(BF16)
        dyr_ref[...] = (dm * gr).astype(BF16)
        dga = dm * ya_ref[...] * ga * (1.0 - ga)
        dgr = dm * yr_ref[...] * gr * (1.0 - gr)
        dga_ref[...] = dga.astype(BF16)
        dgr_ref[...] = dgr.astype(BF16)
        dba_ref[...] += jnp.sum(dga, axis=0, keepdims=True)
        dbr_ref[...] += jnp.sum(dgr, axis=0, keepdims=True)

    blk = pl.BlockSpec((tm, ct), lambda j, i: (i, j))
    vec = pl.BlockSpec((1, ct), lambda j, i: (0, j))
    act = jax.ShapeDtypeStruct((T, D), BF16)
    v1 = jax.ShapeDtypeStruct((1, D), F32)
    return pl.pallas_call(
        body, name="merge_bwd", out_shape=(act, act, act, act, v1, v1), grid=(nd, T // tm),
        in_specs=[pl.BlockSpec((tm, ct), lambda j, i: (i, oa + j)), pl.BlockSpec((tm, ct), lambda j, i: (i, orr + j)),
                  vec, pl.BlockSpec((1, ct), lambda j, i: (0, nd + j)), blk, blk, blk],
        out_specs=(blk, blk, blk, blk, vec, vec),
        compiler_params=_cparams(("parallel", "arbitrary")),
    )(proj, proj, b_gate, b_gate, y_attn, y_rnn, d_m)


def _ln_fwd(x_res, delta, g, b, name, rider=None):
    T, D = x_res.shape
    tm = _pick(T, (256, 128))

    def body(x_ref, d_ref, g_ref, b_ref, y_ref, yb_ref, xh_ref, rs_ref):
        z = ALPHA * x_ref[...] + d_ref[...]
        mu = jnp.mean(z, axis=1, keepdims=True)
        zc = z - mu
        var = jnp.mean(zc * zc, axis=1, keepdims=True)
        rstd = lax.rsqrt(var + LN_EPS)
        xh = zc * rstd
        xh_ref[...] = xh
        rs_ref[...] = rstd
        y = xh * g_ref[...] + b_ref[...]
        y_ref[...] = y
        yb_ref[...] = y.astype(BF16)

    row = pl.BlockSpec((tm, D), lambda i: (i, 0))
    vec = pl.BlockSpec((1, D), lambda i: (0, 0))
    return _call(
        body, name=name,
        out_shape=(jax.ShapeDtypeStruct((T, D), F32), jax.ShapeDtypeStruct((T, D), BF16),
                   jax.ShapeDtypeStruct((T, D), F32), jax.ShapeDtypeStruct((T, 1), F32)),
        grid=(T // tm,), in_specs=[row, row, vec, vec],
        out_specs=(row, row, row, pl.BlockSpec((tm, 1), lambda i: (i, 0))),
        scratch_shapes=[], args=(x_res, delta, g, b), sem=("parallel",), rider=rider)


def _ln_bwd_rows(dy, xh, rstd, g):
    dxh = dy * g
    m1 = jnp.mean(dxh, axis=1, keepdims=True)
    m2 = jnp.mean(dxh * xh, axis=1, keepdims=True)
    return rstd * (dxh - m1 - xh * m2)


def _ln_loss_bwd(x_res, delta, g, b, target):
    T, D = x_res.shape
    tm = _pick(T, (256, 128))

    def body(x_ref, d_ref, g_ref, b_ref, t_ref, dz_ref, dzb_ref, loss_ref, dg_ref, db_ref):
        i = pl.program_id(0)

        @pl.when(i == 0)
        def _():
            loss_ref[...] = jnp.zeros_like(loss_ref)
            dg_ref[...] = jnp.zeros_like(dg_ref)
            db_ref[...] = jnp.zeros_like(db_ref)

        z = ALPHA * x_ref[...] + d_ref[...]
        mu = jnp.mean(z, axis=1, keepdims=True)
        zc = z - mu
        var = jnp.mean(zc * zc, axis=1, keepdims=True)
        rstd = lax.rsqrt(var + LN_EPS)
        xh = zc * rstd
        gv = g_ref[...]
        err = xh * gv + b_ref[...] - t_ref[...]
        loss_ref[...] += 0.5 * jnp.sum(jnp.mean(err * err, axis=1, keepdims=True))
        dy = err * (1.0 / D)
        dg_ref[...] += jnp.sum(dy * xh, axis=0, keepdims=True)
        db_ref[...] += jnp.sum(dy, axis=0, keepdims=True)
        dz = _ln_bwd_rows(dy, xh, rstd, gv)
        dz_ref[...] = dz
        dzb_ref[...] = dz.astype(BF16)

    row = pl.BlockSpec((tm, D), lambda i: (i, 0))
    vec = pl.BlockSpec((1, D), lambda i: (0, 0))
    return pl.pallas_call(
        body, name="ln2_loss_bwd",
        out_shape=(jax.ShapeDtypeStruct((T, D), F32), jax.ShapeDtypeStruct((T, D), BF16),
                   jax.ShapeDtypeStruct((8, LANES), F32),
                   jax.ShapeDtypeStruct((1, D), F32), jax.ShapeDtypeStruct((1, D), F32)),
        grid=(T // tm,), in_specs=[row, row, vec, vec, row],
        out_specs=(row, row, pl.BlockSpec((8, LANES), lambda i: (0, 0)), vec, vec),
        compiler_params=_cparams(("arbitrary",)),
    )(x_res, delta, g, b, target)


def _ln_bwd(dy, xh, rstd, g):
    T, D = dy.shape
    tm = _pick(T, (256, 128))

    def body(dy_ref, xh_ref, rs_ref, g_ref, dz_ref, dzb_ref, dg_ref, db_ref):
        i = pl.program_id(0)

        @pl.when(i == 0)
        def _():
            dg_ref[...] = jnp.zeros_like(dg_ref)
            db_ref[...] = jnp.zeros_like(db_ref)

        dyv, xhv = dy_ref[...], xh_ref[...]
        dg_ref[...] += jnp.sum(dyv * xhv, axis=0, keepdims=True)
        db_ref[...] += jnp.sum(dyv, axis=0, keepdims=True)
        dz = _ln_bwd_rows(dyv, xhv, rs_ref[...], g_ref[...])
        dz_ref[...] = dz
        dzb_ref[...] = dz.astype(BF16)

    row = pl.BlockSpec((tm, D), lambda i: (i, 0))
    vec = pl.BlockSpec((1, D), lambda i: (0, 0))
    return pl.pallas_call(
        body, name="ln1_bwd",
        out_shape=(jax.ShapeDtypeStruct((T, D), F32), jax.ShapeDtypeStruct((T, D), BF16),
                   jax.ShapeDtypeStruct((1, D), F32), jax.ShapeDtypeStruct((1, D), F32)),
        grid=(T // tm,), in_specs=[row, row, pl.BlockSpec((tm, 1), lambda i: (i, 0)), vec],
        out_specs=(row, row, vec, vec), compiler_params=_cparams(("arbitrary",)),
    )(dy, xh, rstd, g)


def _ffn_col_tile(T, d_ff):
    return _pick(d_ff, (256, 128)) if T >= 1024 else _pick(d_ff, (512, 256, 128))


def _ffn_gate(gp, cw_ref, cb_ref):
    return (cb_ref[...] + gp * cw_ref[2:3, :] + _shift_down(gp, 1) * cw_ref[1:2, :]
            + _shift_down(gp, 2) * cw_ref[0:1, :])


def _ffn_fwd(up, gpre, conv_w, conv_b, rider=None):
    T, d_ff = up.shape
    ct = _ffn_col_tile(T, d_ff)

    def body(up_ref, gp_ref, cw_ref, cb_ref, f_ref):
        gate = _ffn_gate(gp_ref[...], cw_ref, cb_ref)
        f_ref[...] = (_gelu(gate) * up_ref[...]).astype(BF16)

    col = pl.BlockSpec((T, ct), lambda j: (0, j))
    (f,), carried = _call(
        body, name="ffn_act_fwd", out_shape=[jax.ShapeDtypeStruct((T, d_ff), BF16)], grid=(d_ff // ct,),
        in_specs=[col, col, pl.BlockSpec((3, ct), lambda j: (0, j)), pl.BlockSpec((1, ct), lambda j: (0, j))],
        out_specs=[col], scratch_shapes=[], args=(up, gpre, conv_w, conv_b), sem=("parallel",), rider=rider)
    return f, carried


def _ffn_bwd(up, gpre, conv_w, conv_b, d_f, after=None):
    T, d_ff = up.shape
    ct = _ffn_col_tile(T, d_ff)

    def body(up_ref, gp_ref, cw_ref, cb_ref, df_ref, dup_ref, dgp_ref, dcw_ref, dcb_ref):
        gp = gp_ref[...]
        gate = _ffn_gate(gp, cw_ref, cb_ref)
        gel, dgel = _gelu_and_grad(gate)
        df = df_ref[...]
        dup_ref[...] = (df * gel).astype(BF16)
        dgate = df * up_ref[...] * dgel
        dcb_ref[...] = jnp.sum(dgate, axis=0, keepdims=True)
        dcw_ref[2:3, :] = jnp.sum(dgate * gp, axis=0, keepdims=True)
        dcw_ref[1:2, :] = jnp.sum(dgate * _shift_down(gp, 1), axis=0, keepdims=True)
        dcw_ref[0:1, :] = jnp.sum(dgate * _shift_down(gp, 2), axis=0, keepdims=True)
        dgp = (dgate * cw_ref[2:3, :] + _shift_up(dgate, 1) * cw_ref[1:2, :]
               + _shift_up(dgate, 2) * cw_ref[0:1, :])
        dgp_ref[...] = dgp.astype(BF16)

    col = pl.BlockSpec((T, ct), lambda j: (0, j))
    w3 = pl.BlockSpec((3, ct), lambda j: (0, j))
    v1 = pl.BlockSpec((1, ct), lambda j: (0, j))
    return _call(
        body, name="ffn_act_bwd",
        out_shape=(jax.ShapeDtypeStruct((T, d_ff), BF16), jax.ShapeDtypeStruct((T, d_ff), BF16),
                   jax.ShapeDtypeStruct((3, d_ff), F32), jax.ShapeDtypeStruct((1, d_ff), F32)),
        grid=(d_ff // ct,), in_specs=[col, col, w3, v1, col], out_specs=(col, col, w3, v1),
        scratch_shapes=[], args=(up, gpre, conv_w, conv_b, d_f), sem=("parallel",), after=after)[0]


def _adamw(w, g, m, v, name):
    R, C = w.shape
    tr = _row_tile(R, C * 4, 8, budget=1280 * 1024)
    c1 = 1.0 / (1.0 - ADAM_B1 ** ADAM_STEP)
    c2 = 1.0 / (1.0 - ADAM_B2 ** ADAM_STEP)

    def body(w_ref, g_ref, m_ref, v_ref, go_ref, d_ref, nm_ref, nv_ref):
        gv = g_ref[...]
        go_ref[...] = gv
        nm = ADAM_B1 * m_ref[...] + (1.0 - ADAM_B1) * gv
        nv = ADAM_B2 * v_ref[...] + (1.0 - ADAM_B2) * (gv * gv)
        nm_ref[...] = nm
        nv_ref[...] = nv
        d_ref[...] = -ADAM_LR * ((nm * c1) / (jnp.sqrt(nv * c2) + ADAM_EPS) + ADAM_WD * w_ref[...])

    blk = pl.BlockSpec((tr, C), lambda r: (r, 0))
    sh = jax.ShapeDtypeStruct((R, C), F32)
    return _call(body, name=name, out_shape=(sh,) * 4, grid=(R // tr,), in_specs=[blk] * 4, out_specs=(blk,) * 4,
                 scratch_shapes=[], args=(w, g, m, v), sem=("parallel",))[0]


def _group_blocks(w_blocks, per):
    nb, bw, _ = w_blocks.shape
    G = nb // per
    w4 = w_blocks.reshape(G, per, bw, bw)
    rows = []
    for p in range(per):
        parts = [w4[:, p] if q == p else jnp.zeros((G, bw, bw), w_blocks.dtype) for q in range(per)]
        rows.append(jnp.concatenate(parts, axis=2))
    return jnp.concatenate(rows, axis=1)


def _ungroup_blocks(w_groups, per):
    G, gw, _ = w_groups.shape
    bw = gw // per
    blocks = [w_groups[:, p * bw:(p + 1) * bw, p * bw:(p + 1) * bw] for p in range(per)]
    return jnp.stack(blocks, axis=1).reshape(G * per, bw, bw)


def _pack(parts):
    flat = jnp.concatenate([p.reshape(-1).astype(F32) for p in parts])
    n = flat.shape[0]
    rows = -(-n // LANES)
    rows = -(-rows // PACK_ROW_MULT) * PACK_ROW_MULT
    flat = jnp.pad(flat, (0, rows * LANES - n))
    return flat.reshape(rows, LANES)


def _unpack(packed, shapes):
    flat = packed.reshape(-1)
    out, off = [], 0
    for s in shapes:
        n = math.prod(s)
        out.append(flat[off:off + n].reshape(s))
        off += n
    return out


def kernel(x, w_in, b_gate, rnn_conv_w, rnn_conv_b, lru_wa, lru_ba, lru_wi, lru_bi, lru_lambda, attn_sinks, w_attn_proj, w_rnn_proj, w_out, ln1_g, ln1_b, ffn_w_up, ffn_w_gate, ffn_conv_w, ffn_conv_b, ffn_w_down, ln2_g, ln2_b, loss_target, m_w_in, m_b_gate, m_rnn_conv_w, m_rnn_conv_b, m_lru_wa, m_lru_ba, m_lru_wi, m_lru_bi, m_lru_lambda, m_attn_sinks, m_w_attn_proj, m_w_rnn_proj, m_w_out, m_ln1_g, m_ln1_b, m_ffn_w_up, m_ffn_w_gate, m_ffn_conv_w, m_ffn_conv_b, m_ffn_w_down, m_ln2_g, m_ln2_b, v_w_in, v_b_gate, v_rnn_conv_w, v_rnn_conv_b, v_lru_wa, v_lru_ba, v_lru_wi, v_lru_bi, v_lru_lambda, v_attn_sinks, v_w_attn_proj, v_w_rnn_proj, v_w_out, v_ln1_g, v_ln1_b, v_ffn_w_up, v_ffn_w_gate, v_ffn_conv_w, v_ffn_conv_b, v_ffn_w_down, v_ln2_g, v_ln2_b):
    weights = dict(w_in=w_in, b_gate=b_gate, rnn_conv_w=rnn_conv_w, rnn_conv_b=rnn_conv_b, lru_wa=lru_wa,
                   lru_ba=lru_ba, lru_wi=lru_wi, lru_bi=lru_bi, lru_lambda=lru_lambda, attn_sinks=attn_sinks,
                   w_attn_proj=w_attn_proj, w_rnn_proj=w_rnn_proj, w_out=w_out, ln1_g=ln1_g, ln1_b=ln1_b,
                   ffn_w_up=ffn_w_up, ffn_w_gate=ffn_w_gate, ffn_conv_w=ffn_conv_w, ffn_conv_b=ffn_conv_b,
                   ffn_w_down=ffn_w_down, ln2_g=ln2_g, ln2_b=ln2_b)
    m_in = dict(w_in=m_w_in, b_gate=m_b_gate, rnn_conv_w=m_rnn_conv_w, rnn_conv_b=m_rnn_conv_b, lru_wa=m_lru_wa,
                lru_ba=m_lru_ba, lru_wi=m_lru_wi, lru_bi=m_lru_bi, lru_lambda=m_lru_lambda, attn_sinks=m_attn_sinks,
                w_attn_proj=m_w_attn_proj, w_rnn_proj=m_w_rnn_proj, w_out=m_w_out, ln1_g=m_ln1_g, ln1_b=m_ln1_b,
                ffn_w_up=m_ffn_w_up, ffn_w_gate=m_ffn_w_gate, ffn_conv_w=m_ffn_conv_w, ffn_conv_b=m_ffn_conv_b,
                ffn_w_down=m_ffn_w_down, ln2_g=m_ln2_g, ln2_b=m_ln2_b)
    v_in = dict(w_in=v_w_in, b_gate=v_b_gate, rnn_conv_w=v_rnn_conv_w, rnn_conv_b=v_rnn_conv_b, lru_wa=v_lru_wa,
                lru_ba=v_lru_ba, lru_wi=v_lru_wi, lru_bi=v_lru_bi, lru_lambda=v_lru_lambda, attn_sinks=v_attn_sinks,
                w_attn_proj=v_w_attn_proj, w_rnn_proj=v_w_rnn_proj, w_out=v_w_out, ln1_g=v_ln1_g, ln1_b=v_ln1_b,
                ffn_w_up=v_ffn_w_up, ffn_w_gate=v_ffn_w_gate, ffn_conv_w=v_ffn_conv_w, ffn_conv_b=v_ffn_conv_b,
                ffn_w_down=v_ffn_w_down, ln2_g=v_ln2_g, ln2_b=v_ln2_b)
    order = list(weights)

    assert x.shape[0] == 1 and w_in.shape[0] == 1, "one sequence per device, depth 1"
    T, D = x.shape[1], x.shape[2]
    nq = attn_sinks.shape[-1]
    nkv = nq // GROUP
    d_attn, d_kv = nq * HEAD_DIM, nkv * HEAD_DIM
    d_rnn = rnn_conv_b.shape[-1]
    d_ff = ffn_conv_b.shape[-1]
    n_blocks, bw = lru_wa.shape[1], lru_wa.shape[2]
    per = (bw * LANES // math.gcd(bw, LANES)) // bw
    gw = per * bw
    assert n_blocks % per == 0 and d_rnn == n_blocks * bw
    q_off, k_off, v_off = 0, d_attn, d_attn + d_kv
    rx_off = d_attn + 2 * d_kv
    ry_off = rx_off + d_rnn
    gl_off = ry_off + d_rnn
    d_in = gl_off + 2 * D
    assert w_in.shape[-1] * N_SHARDS == d_in
    assert k_off % d_kv == 0 and rx_off % gw == 0 and T % ATTN_BLOCK == 0

    xi, yi, ci = lax.axis_index("x"), lax.axis_index("y"), lax.axis_index("c")
    j_me = 2 * xi + yi
    jc_arr = jnp.stack([j_me, ci]).astype(jnp.int32)

    x0 = x[0]
    x0b = _cast_bf16(x0, "cast_x")
    tgt = loss_target[0]
    big = ["w_in", "w_attn_proj", "w_rnn_proj", "w_out", "ffn_w_up", "ffn_w_gate", "ffn_w_down"]
    own = {n: _cast_bf16_into_slot(weights[n][0], jc_arr, "cast_" + n) for n in big}
    order_arr = jnp.stack([j_me, j_me ^ 2, j_me ^ 1, j_me ^ 3]).astype(jnp.int32)

    rcw_s, fcw_s = _all_gather_small([rnn_conv_w[0], ffn_conv_w[0]])
    rcw = jnp.concatenate([rcw_s[j] for j in range(N_SHARDS)], axis=1)
    fcw = jnp.concatenate([fcw_s[j] for j in range(N_SHARDS)], axis=1)

    wa_g = _group_blocks(lru_wa[0], per).astype(BF16)
    wi_g = _group_blocks(lru_wi[0], per).astype(BF16)

    near, diag = (0, 1), (2,)
    proj, w_in_s = _mm_gathering(x0b, own["w_in"], order_arr, "mm_proj")
    a_out, (w_ap_s, w_rp_s) = _attn_fwd(
        proj, attn_sinks, nq, (q_off, k_off, v_off),
        rider=_gather_rider([own["w_attn_proj"], own["w_rnn_proj"]], _atoms([0]) + _atoms([1], near)))
    (b_out, h_all), (w_rp_s, w_o_s, w_up_s) = _rnn_fwd(
        proj, (rx_off, ry_off), rcw, rnn_conv_b, wa_g, wi_g, lru_ba, lru_bi, lru_lambda,
        rider=_gather_rider([w_rp_s, own["w_out"], own["ffn_w_up"]],
                            _atoms([0], diag) + _atoms([1]) + _atoms([2], near, 0, 2)))
    w_ap, w_rp, w_o = w_ap_s.reshape(d_attn, D), w_rp_s.reshape(d_rnn, D), w_o_s.reshape(D, D)
    y_attn, (w_up_s,) = _mm(a_out, w_ap, name="mm_attn_proj", rider=_gather_rider([w_up_s], _atoms([0], near, 1, 2)))
    y_rnn, (w_up_s,) = _mm(b_out, w_rp, name="mm_rnn_proj", rider=_gather_rider([w_up_s], _atoms([0], diag, 0, 2)))
    merged, (w_gate_s,) = _merge_fwd(proj, gl_off, b_gate, y_attn, y_rnn,
                                     rider=_gather_rider([own["ffn_w_gate"]], _atoms([0], near, 0, 2)))
    mix, (w_up_s,) = _mm(merged, w_o, name="mm_out", rider=_gather_rider([w_up_s], _atoms([0], diag, 1, 2)))
    (x1, x1b, xh1, rstd1), (w_gate_s,) = _ln_fwd(x0, mix, ln1_g, ln1_b, "ln1_fwd",
                                                 rider=_gather_rider([w_gate_s], _atoms([0], near, 1, 2)))
    up, (w_gate_s,) = _mm(x1b, w_up_s, name="mm_up", b_shards=N_SHARDS,
                          rider=_gather_rider([w_gate_s], _atoms([0], diag)))
    gpre, (w_dn_s,) = _mm(x1b, w_gate_s, name="mm_gate", b_shards=N_SHARDS,
                          rider=_gather_rider([own["ffn_w_down"]], _atoms([0], near)))
    f_act, (w_dn_s,) = _ffn_fwd(up, gpre, fcw, ffn_conv_b,
                                rider=_gather_rider([w_dn_s], _atoms([0], diag)))
    w_dn = w_dn_s.reshape(d_ff, D)
    f_out = _mm(f_act, w_dn, name="mm_down")
    dz2, dz2b, loss_acc, dg2, db2 = _ln_loss_bwd(x1, f_out, ln2_g, ln2_b, tgt)

    def pair_sums(arrs, from_sibling, names):
        return [_pair_sum(g, la, jc_arr, "pair_sum_" + n) for g, la, n in zip(arrs, from_sibling, names)]

    def shard_sums(parts, landed, names):
        return [_shard_sum(cp, lb, jc_arr, "shard_sum_" + n) for cp, lb, n in zip(parts, landed, names)]

    halves = {}
    g_down = _mm(f_act, dz2b, name="mm_d_w_down", ta=True, out_dtype=BF16)
    g1 = [g_down.reshape(N_SHARDS, d_ff // N_SHARDS, D)]
    d_f, sib1 = _mm(dz2b, w_dn, name="mm_d_f", tb=True, rider=_pair_rider(g1))
    sent1 = _shard_exchange_start(pair_sums(g1, sib1, ["ffn_w_down"]), "shard_exchange_start_down")
    dup, dgp, d_fcw, d_fcb = _ffn_bwd(up, gpre, fcw, ffn_conv_b, d_f, after=sent1[-1])
    g_up = _mm(x1b, dup, name="mm_d_w_up", ta=True, out_dtype=BF16, out_shards=N_SHARDS)
    g_gate = _mm(x1b, dgp, name="mm_d_w_gate", ta=True, out_dtype=BF16, out_shards=N_SHARDS)
    g2 = [g_up, g_gate]
    dx1_a, sib2 = _mm(dup, w_up_s, name="mm_dx1_up", tb=True, b_shards=N_SHARDS, adds=((ALPHA, dz2),),
                      rider=_pair_rider(g2))
    halves["ffn_w_down"], = shard_sums(*_shard_exchange_wait(sent1, dx1_a, "shard_exchange_wait_down"),
                                       ["ffn_w_down"])
    sent2 = _shard_exchange_start(pair_sums(g2, sib2, ["ffn_w_up", "ffn_w_gate"]), "shard_exchange_start_up_gate")
    dx1 = _mm(dgp, w_gate_s, name="mm_dx1_gate", tb=True, b_shards=N_SHARDS, adds=((1.0, dx1_a),), after=sent2[-1])
    dz1, dz1b, dg1, db1 = _ln_bwd(dx1, xh1, rstd1, ln1_g)
    g_out = _mm(merged, dz1b, name="mm_d_w_out", ta=True, out_dtype=BF16)
    d_m = _mm(dz1b, w_o, name="mm_d_merged", tb=True)
    dya, dyr, dgl_a, dgl_r, dbg_a, dbg_r = _merge_bwd(proj, gl_off, b_gate, y_attn, y_rnn, d_m)
    g_ap = _mm(a_out, dya, name="mm_d_w_attn_proj", ta=True, out_dtype=BF16)
    g_rp = _mm(b_out, dyr, name="mm_d_w_rnn_proj", ta=True, out_dtype=BF16)
    names3 = ["w_out", "w_attn_proj", "w_rnn_proj"]
    g3 = [g_out.reshape(N_SHARDS, D // N_SHARDS, D), g_ap.reshape(N_SHARDS, d_attn // N_SHARDS, D),
          g_rp.reshape(N_SHARDS, d_rnn // N_SHARDS, D)]
    d_a = _mm(dya, w_ap, name="mm_d_attn", tb=True)
    d_b, sib3 = _mm(dyr, w_rp, name="mm_d_rnn", tb=True, rider=_pair_rider(g3))
    sent3 = _shard_exchange_start(pair_sums(g3, sib3, names3), "shard_exchange_start_mixers")
    dq, dk, dv, dsink = _attn_bwd(proj, d_a, attn_sinks, nq, (q_off, k_off, v_off), after=sent3[-1])
    (drx, dry, d_rcw, d_rcb, d_ba, d_bi, d_lam, d_wa_g, d_wi_g), _ = _rnn_bwd(
        proj, (rx_off, ry_off), h_all, d_b, rcw, rnn_conv_b, wa_g, wi_g, lru_ba, lru_bi, lru_lambda)
    halves["ffn_w_up"], halves["ffn_w_gate"] = shard_sums(
        *_shard_exchange_wait(sent2, drx, "shard_exchange_wait_up_gate"), ["ffn_w_up", "ffn_w_gate"])
    d_proj = jnp.concatenate([dq, dk.astype(BF16), dv.astype(BF16), drx, dry, dgl_a, dgl_r], axis=1)
    ffn_names = ["ffn_w_down", "ffn_w_up", "ffn_w_gate"]
    g_in, shared_ffn = _mm(x0b, d_proj, name="mm_d_w_in", ta=True, out_dtype=BF16, out_shards=N_SHARDS,
                           rider=_share_rider([halves[n] for n in ffn_names]))
    halves["w_out"], halves["w_attn_proj"], halves["w_rnn_proj"] = shard_sums(
        *_shard_exchange_wait(sent3, g_in, "shard_exchange_wait_mixers"), names3)

    small_parts = [
        ("loss", loss_acc[0:1, 0:1]),
        ("b_gate", jnp.concatenate([dbg_a, dbg_r], axis=1)),
        ("rnn_conv_w", d_rcw), ("rnn_conv_b", d_rcb),
        ("lru_wa", _ungroup_blocks(d_wa_g, per)), ("lru_ba", d_ba),
        ("lru_wi", _ungroup_blocks(d_wi_g, per)), ("lru_bi", d_bi), ("lru_lambda", d_lam),
        ("attn_sinks", dsink[0:1, 0:nq]),
        ("ln1_g", dg1), ("ln1_b", db1),
        ("ffn_conv_w", d_fcw), ("ffn_conv_b", d_fcb),
        ("ln2_g", dg2), ("ln2_b", db2),
    ]
    packed = _pack([p for _, p in small_parts])
    rs = packed.shape[0]

    def whole(g):
        return g.reshape(2 * g.shape[1], g.shape[2])

    grads = {n: whole(g) for n, g in zip(ffn_names, shared_ffn)}
    out_g, out_d, out_m, out_v = {}, {}, {}, {}

    def adamw(n):
        shape = weights[n].shape
        two_d = (math.prod(shape[:-1]), shape[-1])
        g2, d2, m2, v2 = _adamw(weights[n].reshape(two_d), grads[n].reshape(two_d), m_in[n].reshape(two_d),
                                v_in[n].reshape(two_d), "adamw_" + n)
        out_g[n], out_d[n] = g2.reshape(shape), d2.reshape(shape)
        out_m[n], out_v[n] = m2.reshape(shape), v2.reshape(shape)

    g4 = [g_in, packed.reshape(N_SHARDS, rs // N_SHARDS, LANES)]
    sib4 = _run_rider(_pair_rider(g4), "pair_exchange_in_small")
    sent4 = _shard_exchange_start(pair_sums(g4, sib4, ["w_in", "small"]), "shard_exchange_start_in_small")
    grad_x, shared_mix = _mm(d_proj, w_in_s, name="mm_d_x", tb=True, b_shards=N_SHARDS, adds=((ALPHA, dz1),),
                             rider=_share_rider([halves[n] for n in names3]), after=sent4[-1])
    grads.update({n: whole(g) for n, g in zip(names3, shared_mix)})
    for n in ffn_names + names3:
        adamw(n)
    (part_in, part_small), (lb_in, lb_small) = _shard_exchange_wait(sent4, out_d[names3[-1]],
                                                                    "shard_exchange_wait_in_small")
    halves["w_in"], = shard_sums([part_in], [lb_in], ["w_in"])
    eighths = _shard_sum(part_small, lb_small, jc_arr, "shard_sum_small", all_slots=True)
    shared_in, reduced = _run_rider(_share_rider([halves["w_in"]], eighths), "share_in_small")
    grads["w_in"] = whole(shared_in)
    reduced = reduced.reshape(rs, LANES)
    small = dict(zip([n for n, _ in small_parts], _unpack(reduced, [p.shape for _, p in small_parts])))
    loss = small.pop("loss").reshape(())
    rcw_n = d_rnn // N_SHARDS
    fcw_n = d_ff // N_SHARDS
    small["rnn_conv_w"] = lax.dynamic_slice(small["rnn_conv_w"], (0, j_me * rcw_n), (4, rcw_n))
    small["ffn_conv_w"] = lax.dynamic_slice(small["ffn_conv_w"], (0, j_me * fcw_n), (3, fcw_n))
    for n, g in small.items():
        grads[n] = g

    for n in order:
        if n not in out_g:
            adamw(n)

    return (loss, grad_x.reshape(x.shape), *[out_g[n] for n in order], *[out_d[n] for n in order],
            *[out_m[n] for n in order], *[out_v[n] for n in order])
```

```python
import functools
import math

import jax
import jax.numpy as jnp
from jax import lax
from jax.experimental import pallas as pl
from jax.experimental.pallas import tpu as pltpu

F32 = jnp.float32
BF16 = jnp.bfloat16
MESH = pl.DeviceIdType.MESH

HEAD_DIM = 64
GROUP = 8
ATTN_BLOCK = 128
LRU_C = 8.0
LN_EPS = 1e-5
ALPHA = 2.0 ** 0.25
LANES = 128
N_SHARDS = 4
N_DEV = 8
VMEM_LIMIT = 56 * 1024 * 1024
MM_VMEM_BUDGET = 40 * 1024 * 1024
MM_MAX_TILE = 3072
PACK_ROW_MULT = 8 * 64
NEG = -1e30

ADAM_LR, ADAM_B1, ADAM_B2, ADAM_EPS, ADAM_WD, ADAM_STEP = 0.001, 0.9, 0.999, 1e-08, 0.01, 10

GELU_C = math.sqrt(2.0 / math.pi)
GELU_A = 0.044715


def _cparams(sem=None):
    kw = dict(vmem_limit_bytes=VMEM_LIMIT)
    if sem is not None:
        kw["dimension_semantics"] = sem
    return pltpu.CompilerParams(**kw)


def _pick(n, prefs):
    for p in prefs:
        if n % p == 0:
            return p
    return n


def _row_tile(rows, row_bytes, mult, budget=2 * 1024 * 1024):
    best = None
    for d in range(mult, rows + 1, mult):
        if rows % d == 0 and d * row_bytes <= budget:
            best = d
    return best if best is not None else rows


def _gelu(x):
    return 0.5 * x * (1.0 + jnp.tanh(GELU_C * (x + GELU_A * x * x * x)))


def _gelu_and_grad(x):
    t = jnp.tanh(GELU_C * (x + GELU_A * x * x * x))
    g = 0.5 * x * (1.0 + t)
    dg = 0.5 * (1.0 + t) + 0.5 * x * (1.0 - t * t) * GELU_C * (1.0 + 3.0 * GELU_A * x * x)
    return g, dg


def _shift_down(x, s, fill=0.0):
    row = lax.broadcasted_iota(jnp.int32, x.shape, 0)
    return jnp.where(row >= s, pltpu.roll(x, s, 0), fill)


def _shift_up(x, s, fill=0.0):
    n = x.shape[0]
    row = lax.broadcasted_iota(jnp.int32, x.shape, 0)
    return jnp.where(row < n - s, pltpu.roll(x, n - s, 0), fill)


def _mm(a, b, *, name, ta=False, tb=False, out_dtype=F32, adds=(), b_shards=1, out_shards=1,
        tm=None, tn=None, tk=None, rider=None, after=None):
    if ta:
        K, M = a.shape
    else:
        M, K = a.shape
    if b_shards > 1:
        n_sh = b.shape[-1]
        if tb:
            N = b.shape[1]
            assert b_shards * n_sh == K
        else:
            N = b_shards * n_sh
            assert b.shape[1] == K
    else:
        n_sh = None
        if tb:
            N = b.shape[0]
            assert b.shape[1] == K
        else:
            N = b.shape[1]
            assert b.shape[0] == K
    wide = (1024, 1536, 1280, 768, 640, 512, 256, 128)
    if tn is None:
        if b_shards > 1 and not tb:
            tn = n_sh if n_sh <= MM_MAX_TILE else _pick(n_sh, wide)
        elif out_shards > 1:
            tn = N // out_shards if N // out_shards <= MM_MAX_TILE else _pick(N // out_shards, wide)
        else:
            tn = _pick(N, wide)
    if tk is None:
        if b_shards > 1 and tb:
            tk = n_sh if n_sh <= MM_MAX_TILE else _pick(n_sh, wide)
        else:
            tk = K if K <= MM_MAX_TILE else _pick(K, (2048,) + wide)
    assert N % tn == 0 and K % tk == 0, (name, M, N, K, tn, tk)
    nk = K // tk
    n_add = len(adds)
    sa, sb, so = a.dtype.itemsize, b.dtype.itemsize, jnp.dtype(out_dtype).itemsize

    def vmem_bytes(tm_):
        return (2 * (tm_ * tk * sa + tk * tn * sb + tm_ * tn * so + n_add * tm_ * tn * 4)
                + (tm_ * tn * 4 if nk > 1 else 0))

    if tm is None:
        tm = _pick(M, (1024, 512, 256, 128)) if nk > 1 else _pick(M, (512, 256, 128))
        while vmem_bytes(tm) > MM_VMEM_BUDGET and tm % 256 == 0:
            tm //= 2
    assert M % tm == 0, (name, M, tm)
    b_outer = b.size * sb >= a.size * sa

    def ij(g0, g1):
        return (g1, g0) if b_outer else (g0, g1)

    def amap(g0, g1, k):
        i, _ = ij(g0, g1)
        return (k, i) if ta else (i, k)

    def bmap(g0, g1, k):
        _, j = ij(g0, g1)
        if b_shards > 1 and not tb:
            per = n_sh // tn
            return (j // per, k, j % per)
        if b_shards > 1 and tb:
            per = n_sh // tk
            return (k // per, j, k % per)
        return (j, k) if tb else (k, j)

    def omap(g0, g1, k):
        i, j = ij(g0, g1)
        if out_shards > 1:
            per_o = (N // out_shards) // tn
            return (j // per_o, i, j % per_o)
        return (i, j)

    a_spec = pl.BlockSpec((tk, tm) if ta else (tm, tk), amap)
    if b_shards > 1:
        b_spec = pl.BlockSpec((None, tn, tk) if tb else (None, tk, tn), bmap)
    else:
        b_spec = pl.BlockSpec((tn, tk) if tb else (tk, tn), bmap)
    add_specs = [pl.BlockSpec((tm, tn), lambda g0, g1, k: ij(g0, g1)) for _ in adds]
    if out_shards > 1:
        out_spec = pl.BlockSpec((None, tm, tn), omap)
        out_shape = jax.ShapeDtypeStruct((out_shards, M, N // out_shards), out_dtype)
    else:
        out_spec = pl.BlockSpec((tm, tn), omap)
        out_shape = jax.ShapeDtypeStruct((M, N), out_dtype)

    if ta:
        dims = (((0,), (0,)), ((), ()))
    elif tb:
        dims = (((1,), (1,)), ((), ()))
    else:
        dims = (((1,), (0,)), ((), ()))
    scales = tuple(s for s, _ in adds)

    def finish(r, add_refs, o_ref):
        for s, ref in zip(scales, add_refs):
            r = r + s * ref[...].astype(F32)
        o_ref[...] = r.astype(out_dtype)

    def body(a_ref, b_ref, *rest):
        add_refs = rest[:n_add]
        o_ref = rest[n_add]
        part = lax.dot_general(a_ref[...].astype(BF16), b_ref[...].astype(BF16), dims, preferred_element_type=F32)
        if nk == 1:
            finish(part, add_refs, o_ref)
            return
        acc = rest[n_add + 1]
        k = pl.program_id(2)

        @pl.when(k == 0)
        def _():
            acc[...] = part

        @pl.when(k > 0)
        def _():
            acc[...] += part

        @pl.when(k == nk - 1)
        def _():
            finish(acc[...], add_refs, o_ref)

    grid = (N // tn, M // tm, nk) if b_outer else (M // tm, N // tn, nk)
    (res,), carried = _call(
        body, name=name, grid=grid, in_specs=[a_spec, b_spec] + add_specs, out_specs=[out_spec],
        out_shape=[out_shape], scratch_shapes=[pltpu.VMEM((tm, tn), F32)] if nk > 1 else [],
        args=(a, b, *[x for _, x in adds]), sem=("parallel", "parallel", "arbitrary"), rider=rider, after=after)
    return (res, carried) if rider is not None else res


def _cast_bf16(w, name):
    R, C = w.shape
    tr = _row_tile(R, C * 4, 16)

    def body(w_ref, o_ref):
        o_ref[...] = w_ref[...].astype(BF16)

    return pl.pallas_call(
        body, name=name, out_shape=jax.ShapeDtypeStruct((R, C), BF16), grid=(R // tr,),
        in_specs=[pl.BlockSpec((tr, C), lambda r: (r, 0))], out_specs=pl.BlockSpec((tr, C), lambda r: (r, 0)),
        compiler_params=_cparams(("parallel",)),
    )(w)


def _cast_bf16_into_slot(w, jc_arr, name):
    R, C = w.shape
    tr = _row_tile(R, C * 4, 16)

    def body(jc_ref, w_ref, o_ref):
        o_ref[...] = w_ref[...].astype(BF16)

    gs = pltpu.PrefetchScalarGridSpec(
        num_scalar_prefetch=1, grid=(R // tr,),
        in_specs=[pl.BlockSpec((tr, C), lambda r, jc: (r, 0))],
        out_specs=pl.BlockSpec((None, tr, C), lambda r, jc: (jc[0], r, 0)))
    return pl.pallas_call(body, name=name, out_shape=jax.ShapeDtypeStruct((N_SHARDS, R, C), BF16), grid_spec=gs,
                          compiler_params=_cparams(("parallel",)))(jc_arr, w)


def _pair_sum(g, la, jc_arr, name):
    S, R, C = g.shape
    half = R // 2
    tr = _row_tile(half, C * 4, 16)
    nrt = half // tr
    dt = g.dtype

    def body(jc_ref, g_ref, la_ref, o_ref):
        o_ref[...] = (g_ref[...].astype(F32) + la_ref[...].astype(F32)).astype(dt)

    gs = pltpu.PrefetchScalarGridSpec(
        num_scalar_prefetch=1, grid=(S, nrt),
        in_specs=[pl.BlockSpec((None, tr, C), lambda s, r, jc: (s, jc[1] * nrt + r, 0)),
                  pl.BlockSpec((None, tr, C), lambda s, r, jc: (s, r, 0))],
        out_specs=pl.BlockSpec((None, tr, C), lambda s, r, jc: (s, r, 0)))
    return pl.pallas_call(body, name=name, out_shape=jax.ShapeDtypeStruct((S, half, C), dt), grid_spec=gs,
                          compiler_params=_cparams(("parallel", "parallel")))(jc_arr, g, la)


def _shard_sum(cp, lb, jc_arr, name, all_slots=False):
    S, h, C = cp.shape
    tr = _row_tile(h, C * 4, 16)

    def body(jc_ref, cp_ref, l0, l1, l2, o_ref):
        o_ref[...] = ((cp_ref[...].astype(F32) + l0[...].astype(F32)) + l1[...].astype(F32)) + l2[...].astype(F32)

    def lspec(kk):
        return pl.BlockSpec((None, tr, C), lambda r, jc: (kk, r, 0))

    if all_slots:
        out_spec = pl.BlockSpec((None, None, tr, C), lambda r, jc: (jc[0], jc[1], r, 0))
        out_shape = jax.ShapeDtypeStruct((S, 2, h, C), F32)
    else:
        out_spec = pl.BlockSpec((None, tr, C), lambda r, jc: (jc[1], r, 0))
        out_shape = jax.ShapeDtypeStruct((2, h, C), F32)
    gs = pltpu.PrefetchScalarGridSpec(
        num_scalar_prefetch=1, grid=(h // tr,),
        in_specs=[pl.BlockSpec((None, tr, C), lambda r, jc: (jc[0], r, 0)), lspec(0), lspec(1), lspec(2)],
        out_specs=out_spec)
    return pl.pallas_call(body, name=name, out_shape=out_shape, grid_spec=gs,
                          compiler_params=_cparams(("parallel",)))(jc_arr, cp, lb, lb, lb)


ANY = pl.BlockSpec(memory_space=pl.ANY)


def _place():
    x, y, c = lax.axis_index("x"), lax.axis_index("y"), lax.axis_index("c")
    chips = [(1 - x, y), (x, 1 - y), (1 - x, 1 - y)]
    return x, y, c, chips


class _Rider:
    def __init__(self, inputs, out_shape, aliases, sems, start, finish):
        self.inputs, self.out_shape, self.aliases, self.sems = list(inputs), list(out_shape), dict(aliases), list(sems)
        self.start, self.finish = start, finish


def _join_riders(r1, r2):
    i1, o1, s1 = len(r1.inputs), len(r1.out_shape), len(r1.sems)
    aliases = dict(r1.aliases)
    aliases.update({i1 + i: o1 + o for i, o in r2.aliases.items()})

    def start(ins, outs, sems):
        r1.start(ins[:i1], outs[:o1], sems[:s1])
        r2.start(ins[i1:], outs[o1:], sems[s1:])

    def finish(ins, outs, sems):
        r1.finish(ins[:i1], outs[:o1], sems[:s1])
        r2.finish(ins[i1:], outs[o1:], sems[s1:])

    return _Rider(r1.inputs + r2.inputs, r1.out_shape + r2.out_shape, aliases, r1.sems + r2.sems, start, finish)


def _after_rider(x):
    return _Rider([x], [], {}, [], lambda *a: None, lambda *a: None)


def _call(body, *, name, grid, in_specs, out_specs, out_shape, scratch_shapes, args, sem, rider=None, after=None):
    out_specs, out_shape = tuple(out_specs), tuple(out_shape)
    if after is not None:
        rider = _after_rider(after) if rider is None else _join_riders(_after_rider(after), rider)
    if rider is None:
        res = pl.pallas_call(body, name=name, out_shape=out_shape, grid=grid, in_specs=list(in_specs),
                             out_specs=out_specs, scratch_shapes=list(scratch_shapes),
                             compiler_params=_cparams(sem))(*args)
        return tuple(res), []
    n_in, n_out, n_sc = len(in_specs), len(out_specs), len(scratch_shapes)
    r_in, r_out = len(rider.inputs), len(rider.out_shape)

    def wrapped(*refs):
        p = 0
        host_in = refs[p:p + n_in]; p += n_in
        rid_in = refs[p:p + r_in]; p += r_in
        host_out = refs[p:p + n_out]; p += n_out
        rid_out = refs[p:p + r_out]; p += r_out
        host_sc = refs[p:p + n_sc]; p += n_sc
        rid_sem = refs[p:]
        first = functools.reduce(jnp.logical_and, [pl.program_id(a) == 0 for a in range(len(grid))])
        last = functools.reduce(jnp.logical_and, [pl.program_id(a) == grid[a] - 1 for a in range(len(grid))])

        @pl.when(first)
        def _():
            rider.start(rid_in, rid_out, rid_sem)

        body(*host_in, *host_out, *host_sc)

        @pl.when(last)
        def _():
            rider.finish(rid_in, rid_out, rid_sem)

    res = pl.pallas_call(
        wrapped, name=name, out_shape=out_shape + tuple(rider.out_shape), grid=grid,
        in_specs=list(in_specs) + [ANY] * r_in, out_specs=out_specs + (ANY,) * r_out,
        input_output_aliases={n_in + i: n_out + o for i, o in rider.aliases.items()},
        scratch_shapes=list(scratch_shapes) + rider.sems,
        compiler_params=_cparams(("arbitrary",) * len(grid)),
    )(*args, *rider.inputs)
    return tuple(res[:n_out]), list(res[n_out:])


def _run_rider(rider, name):
    def body(*refs):
        r_in, r_out = len(rider.inputs), len(rider.out_shape)
        ins, outs, sems = refs[:r_in], refs[r_in:r_in + r_out], refs[r_in + r_out:]
        rider.start(ins, outs, sems)
        rider.finish(ins, outs, sems)

    return pl.pallas_call(
        body, name=name, out_shape=rider.out_shape, in_specs=[ANY] * len(rider.inputs),
        out_specs=[ANY] * len(rider.out_shape), input_output_aliases=rider.aliases, scratch_shapes=rider.sems,
    )(*rider.inputs)


def _atoms(indices, kks=(0, 1, 2), q=0, nq=1):
    return [(i, kk, q, nq) for i in indices for kk in kks]


def _gather_rider(bufs, atoms=None):
    n = len(bufs)
    if atoms is None:
        atoms = _atoms(range(n))
    na = len(atoms)

    def rows(out, atom, core):
        i, _, q, nq = atom
        half = out[i].shape[1] // 2
        assert half % (16 * nq) == 0, (half, nq)
        return pl.ds(core * half + q * (half // nq), half // nq)

    def ici_copy(out, sems, a, slot, peer):
        c = lax.axis_index("c")
        blk = out[atoms[a][0]].at[slot, rows(out, atoms[a], c), :]
        return pltpu.make_async_remote_copy(
            src_ref=blk, dst_ref=blk, send_sem=sems[0].at[a], recv_sem=sems[1].at[a],
            device_id=(peer[0], peer[1], c), device_id_type=MESH)

    def d2d_copy(out, sems, a, slot, from_core):
        x, y, c, _ = _place()
        blk = out[atoms[a][0]].at[slot, rows(out, atoms[a], from_core), :]
        return pltpu.make_async_remote_copy(
            src_ref=blk, dst_ref=blk, send_sem=sems[2].at[a], recv_sem=sems[3].at[a],
            device_id=(x, y, 1 - c), device_id_type=MESH)

    def start(ins, out, sems):
        x, y, c, chips = _place()
        for a in range(na):
            ici_copy(out, sems, a, 2 * x + y, chips[atoms[a][1]]).start()

    def finish(ins, out, sems):
        x, y, c, chips = _place()
        src = [2 * chips[atoms[a][1]][0] + chips[atoms[a][1]][1] for a in range(na)]
        for a in range(na):
            ici_copy(out, sems, a, src[a], chips[atoms[a][1]]).wait_recv()
            d2d_copy(out, sems, a, src[a], c).start()
        for a in range(na):
            d2d_copy(out, sems, a, src[a], 1 - c).wait_recv()
        for a in range(na):
            ici_copy(out, sems, a, 2 * x + y, chips[atoms[a][1]]).wait_send()
            d2d_copy(out, sems, a, src[a], c).wait_send()

    return _Rider(bufs, [jax.ShapeDtypeStruct(s.shape, s.dtype) for s in bufs], {i: i for i in range(n)},
                  [pltpu.SemaphoreType.DMA((na,))] * 4, start, finish)


def _mm_gathering(a, buf, order_arr, name):
    M, K = a.shape
    S, _, n = buf.shape
    tm = _pick(M, (512, 256, 128))
    n_i = M // tm
    half = K // 2

    def body(order_ref, a_ref, w_in_ref, o_ref, w_ref, b_vmem, load_sem, s_ici, r_ici, s_d2d, r_d2d):
        s, i = pl.program_id(0), pl.program_id(1)
        x, y, c, chips = _place()
        j_me = 2 * x + y
        slots = [2 * px + py for px, py in chips]

        def ici(kk, slot):
            blk = w_ref.at[slot, pl.ds(c * half, half), :]
            return pltpu.make_async_remote_copy(
                src_ref=blk, dst_ref=blk, send_sem=s_ici.at[kk], recv_sem=r_ici.at[kk],
                device_id=(chips[kk][0], chips[kk][1], c), device_id_type=MESH)

        def d2d(kk, from_core):
            blk = w_ref.at[slots[kk], pl.ds(from_core * half, half), :]
            return pltpu.make_async_remote_copy(
                src_ref=blk, dst_ref=blk, send_sem=s_d2d.at[kk], recv_sem=r_d2d.at[kk],
                device_id=(x, y, 1 - c), device_id_type=MESH)

        def load(slot, b):
            return pltpu.make_async_copy(w_ref.at[slot], b_vmem.at[b], load_sem.at[b])

        @pl.when(jnp.logical_and(s == 0, i == 0))
        def _():
            for kk in range(3):
                ici(kk, j_me).start()
            load(j_me, 0).start()

        @pl.when(i == 0)
        def _():
            load(order_ref[s], s % 2).wait()

        o_ref[...] = jnp.dot(a_ref[...], b_vmem[s % 2], preferred_element_type=F32)

        last = i == n_i - 1

        @pl.when(jnp.logical_and(last, s == 0))
        def _():
            ici(0, slots[0]).wait_recv()
            d2d(0, c).start()
            ici(1, slots[1]).wait_recv()
            d2d(1, c).start()
            d2d(0, 1 - c).wait_recv()
            load(slots[0], 1).start()

        @pl.when(jnp.logical_and(last, s == 1))
        def _():
            d2d(1, 1 - c).wait_recv()
            load(slots[1], 0).start()

        @pl.when(jnp.logical_and(last, s == 2))
        def _():
            ici(2, slots[2]).wait_recv()
            d2d(2, c).start()
            d2d(2, 1 - c).wait_recv()
            load(slots[2], 1).start()

        @pl.when(jnp.logical_and(last, s == 3))
        def _():
            for kk in range(3):
                ici(kk, j_me).wait_send()
                d2d(kk, c).wait_send()

    gs = pltpu.PrefetchScalarGridSpec(
        num_scalar_prefetch=1, grid=(S, n_i),
        in_specs=[pl.BlockSpec((tm, K), lambda s, i, order: (i, 0)), ANY],
        out_specs=[pl.BlockSpec((tm, n), lambda s, i, order: (i, order[s])), ANY],
        scratch_shapes=[pltpu.VMEM((2, K, n), BF16), pltpu.SemaphoreType.DMA((2,))]
        + [pltpu.SemaphoreType.DMA((3,))] * 4)
    return pl.pallas_call(
        body, name=name, grid_spec=gs,
        out_shape=[jax.ShapeDtypeStruct((M, S * n), F32), jax.ShapeDtypeStruct(buf.shape, buf.dtype)],
        input_output_aliases={2: 1}, compiler_params=_cparams(("arbitrary", "arbitrary")),
    )(order_arr, a, buf)


def _all_gather_small(shards):
    n = len(shards)

    def body(*refs):
        w = refs[:n]
        out = refs[n:2 * n]
        local_sem, s_sem, r_sem = refs[2 * n:]
        x, y, c, chips = _place()
        j_me = 2 * x + y
        cps = []
        for i in range(n):
            lc = pltpu.make_async_copy(w[i], out[i].at[j_me], local_sem.at[i])
            lc.start()
            cps.append(lc)
        sends = []
        for i in range(n):
            for kk, (px, py) in enumerate(chips):
                cp = pltpu.make_async_remote_copy(
                    src_ref=w[i], dst_ref=out[i].at[j_me], send_sem=s_sem.at[3 * i + kk],
                    recv_sem=r_sem.at[3 * i + kk], device_id=(px, py, c), device_id_type=MESH)
                cp.start()
                sends.append(cp)
        for i in range(n):
            for kk, (px, py) in enumerate(chips):
                sends[3 * i + kk].wait_send()
                pltpu.make_async_remote_copy(
                    src_ref=w[i], dst_ref=out[i].at[2 * px + py], send_sem=s_sem.at[3 * i + kk],
                    recv_sem=r_sem.at[3 * i + kk], device_id=(px, py, c), device_id_type=MESH).wait_recv()
        for lc in cps:
            lc.wait()

    out_shape = [jax.ShapeDtypeStruct((N_SHARDS,) + s.shape, s.dtype) for s in shards]
    return pl.pallas_call(
        body, name="all_gather_conv_weights", out_shape=out_shape, in_specs=[ANY] * n, out_specs=[ANY] * n,
        scratch_shapes=[pltpu.SemaphoreType.DMA((n,)), pltpu.SemaphoreType.DMA((3 * n,)),
                        pltpu.SemaphoreType.DMA((3 * n,))],
    )(*shards)


def _pair_rider(grads):
    n = len(grads)

    def copies(g, la, sems):
        x, y, c, _ = _place()
        return [pltpu.make_async_remote_copy(
            src_ref=g[i].at[:, pl.ds((1 - c) * (g[i].shape[1] // 2), g[i].shape[1] // 2), :], dst_ref=la[i],
            send_sem=sems[0].at[i], recv_sem=sems[1].at[i], device_id=(x, y, 1 - c), device_id_type=MESH)
            for i in range(n)]

    def start(g, la, sems):
        for cp in copies(g, la, sems):
            cp.start()

    def finish(g, la, sems):
        for cp in copies(g, la, sems):
            cp.wait()

    return _Rider(grads, [jax.ShapeDtypeStruct((s.shape[0], s.shape[1] // 2, s.shape[2]), s.dtype) for s in grads],
                  {}, [pltpu.SemaphoreType.DMA((n,)), pltpu.SemaphoreType.DMA((n,))], start, finish)


HBM = pl.BlockSpec(memory_space=pltpu.HBM)
SEM = pl.BlockSpec(memory_space=pltpu.SEMAPHORE)


def _shard_copies(part_refs, land_refs, send_sems, recv_sems, relations):
    x, y, c, chips = _place()
    nr = len(relations)
    return [pltpu.make_async_remote_copy(
        src_ref=part_refs[i].at[2 * chips[kk][0] + chips[kk][1]], dst_ref=land_refs[i].at[kk],
        send_sem=send_sems.at[nr * i + r], recv_sem=recv_sems.at[nr * i + r],
        device_id=(chips[kk][0], chips[kk][1], c), device_id_type=MESH)
        for i in range(len(part_refs)) for r, kk in enumerate(relations)]


SIDE_EFFECT = pltpu.SideEffectType.DATAFLOW_SIDE_EFFECTING


def _shard_exchange_start(parts, name, relations=(0, 1, 2), lands=None):
    n = len(parts)
    ns = n * len(relations)

    def body(*refs):
        part_refs, land_refs = refs[:n], refs[n:2 * n]
        send_sems, recv_sems = refs[2 * n], refs[2 * n + 1]
        token = refs[4 * n + 2]
        for cp in _shard_copies(part_refs, land_refs, send_sems, recv_sems, relations):
            cp.start()
        token[...] = jnp.zeros_like(token)

    if lands is None:
        lands = [lax.empty((3,) + p.shape[1:], p.dtype) for p in parts]
    bufs = list(parts) + list(lands)
    res = pl.pallas_call(
        body, name=name,
        out_shape=(pltpu.SemaphoreType.DMA((ns,)), pltpu.SemaphoreType.DMA((ns,)),
                   *[pltpu.HBM(b.shape, b.dtype) for b in bufs], jax.ShapeDtypeStruct((8, LANES), F32)),
        in_specs=(HBM,) * (2 * n), out_specs=(SEM, SEM) + (HBM,) * (2 * n) + (pl.BlockSpec(memory_space=pltpu.VMEM),),
        input_output_aliases={i: 2 + i for i in range(2 * n)},
        compiler_params=pltpu.CompilerParams(has_side_effects=SIDE_EFFECT),
    )(*[pltpu.with_memory_space_constraint(b, pltpu.HBM) for b in bufs])
    return res[0], res[1], list(res[2:2 + n]), list(res[2 + n:2 + 2 * n]), res[2 + 2 * n], relations


def _shard_exchange_wait(started, after, name):
    send_sems, recv_sems, parts, lands, _, relations = started
    n = len(parts)

    def body(*refs):
        part_refs, land_refs = refs[:n], refs[n:2 * n]
        send_sems_ref, recv_sems_ref = refs[2 * n], refs[2 * n + 1]
        for cp in _shard_copies(part_refs, land_refs, send_sems_ref, recv_sems_ref, relations):
            cp.wait_send()
            cp.wait_recv()

    bufs = parts + lands
    res = pl.pallas_call(
        body, name=name, out_shape=tuple(pltpu.HBM(b.shape, b.dtype) for b in bufs),
        in_specs=(HBM,) * (2 * n) + (SEM, SEM, ANY), out_specs=(HBM,) * (2 * n),
        input_output_aliases={i: i for i in range(2 * n)},
        compiler_params=pltpu.CompilerParams(has_side_effects=SIDE_EFFECT),
    )(*bufs, send_sems, recv_sems, after)
    return list(res[:n]), list(res[n:])


def _share_rider(halves, eighths=None):
    n = len(halves)
    bufs = list(halves) + ([eighths] if eighths is not None else [])

    def half_copy(out, sems, i, core):
        x, y, c, _ = _place()
        blk = out[i].at[core]
        return pltpu.make_async_remote_copy(src_ref=blk, dst_ref=blk, send_sem=sems[0].at[i], recv_sem=sems[1].at[i],
                                            device_id=(x, y, 1 - c), device_id_type=MESH)

    def eighth_copy(out, sems, r, mine):
        x, y, c, _ = _place()
        px, py, pc = x ^ ((r >> 2) & 1), y ^ ((r >> 1) & 1), c ^ (r & 1)
        blk = out[n].at[2 * x + y, c] if mine else out[n].at[2 * px + py, pc]
        return pltpu.make_async_remote_copy(src_ref=blk, dst_ref=blk, send_sem=sems[2].at[r - 1],
                                            recv_sem=sems[3].at[r - 1], device_id=(px, py, pc), device_id_type=MESH)

    def start(ins, out, sems):
        c = lax.axis_index("c")
        for i in range(n):
            half_copy(out, sems, i, c).start()
        if eighths is not None:
            for r in range(1, N_DEV):
                eighth_copy(out, sems, r, True).start()

    def finish(ins, out, sems):
        c = lax.axis_index("c")
        for i in range(n):
            half_copy(out, sems, i, 1 - c).wait_recv()
        if eighths is not None:
            for r in range(1, N_DEV):
                eighth_copy(out, sems, r, False).wait_recv()
        for i in range(n):
            half_copy(out, sems, i, c).wait_send()
        if eighths is not None:
            for r in range(1, N_DEV):
                eighth_copy(out, sems, r, True).wait_send()

    return _Rider(bufs, [jax.ShapeDtypeStruct(s.shape, s.dtype) for s in bufs], {i: i for i in range(len(bufs))},
                  [pltpu.SemaphoreType.DMA((max(n, 1),)), pltpu.SemaphoreType.DMA((max(n, 1),)),
                   pltpu.SemaphoreType.DMA((N_DEV - 1,)), pltpu.SemaphoreType.DMA((N_DEV - 1,))], start, finish)


ATTN_ROWS = GROUP * ATTN_BLOCK
ATTN_KEYS = 2 * ATTN_BLOCK


def _attn_geometry(n):
    row = lax.broadcasted_iota(jnp.int32, (ATTN_ROWS, ATTN_KEYS), 0)
    col = lax.broadcasted_iota(jnp.int32, (ATTN_ROWS, ATTN_KEYS), 1)
    dist = ATTN_BLOCK + jnp.bitwise_and(row, ATTN_BLOCK - 1) - col
    valid = jnp.logical_and(jnp.logical_and(dist >= 0, dist < ATTN_BLOCK),
                            jnp.logical_or(col >= ATTN_BLOCK, n > 0))
    return dist.astype(F32), valid


def _per_head_column(values):
    head = lax.broadcasted_iota(jnp.int32, (ATTN_ROWS, 1), 0) // ATTN_BLOCK
    col = jnp.zeros((ATTN_ROWS, 1), F32)
    for hh, v in enumerate(values):
        col = jnp.where(head == hh, v, col)
    return col


def _stack_heads(ref, g):
    return jnp.concatenate(
        [ref[:, (g * GROUP + hh) * HEAD_DIM:(g * GROUP + hh + 1) * HEAD_DIM].astype(BF16) for hh in range(GROUP)],
        axis=0)


def _attn_probs(q_s, k2, slope_col, sink_col, dist, valid):
    s = lax.dot_general(q_s, k2, (((1,), (1,)), ((), ())), preferred_element_type=F32) * (HEAD_DIM ** -0.5)
    s = jnp.where(valid, s - slope_col * dist, NEG)
    m = jnp.maximum(jnp.max(s, axis=1, keepdims=True), sink_col)
    e = jnp.exp(s - m)
    es = jnp.exp(sink_col - m)
    inv = 1.0 / (jnp.sum(e, axis=1, keepdims=True) + es)
    return e * inv, es * inv


def _attn_specs(T, d_attn, d_kv, q_blk, k_blk, v_blk):
    bq = pl.BlockSpec((ATTN_BLOCK, d_attn), lambda n: (n, q_blk))
    kp = pl.BlockSpec((ATTN_BLOCK, d_kv), lambda n: (jnp.maximum(n - 1, 0), k_blk))
    kc = pl.BlockSpec((ATTN_BLOCK, d_kv), lambda n: (n, k_blk))
    vp = pl.BlockSpec((ATTN_BLOCK, d_kv), lambda n: (jnp.maximum(n - 1, 0), v_blk))
    vc = pl.BlockSpec((ATTN_BLOCK, d_kv), lambda n: (n, v_blk))
    return bq, kp, kc, vp, vc


def _attn_fwd(proj, sinks, nq, cols, rider=None):
    T = proj.shape[0]
    nkv = nq // GROUP
    d_attn, d_kv = nq * HEAD_DIM, nkv * HEAD_DIM
    q_off, k_off, v_off = cols
    bq, kp, kc, vp, vc = _attn_specs(T, d_attn, d_kv, q_off // d_attn, k_off // d_kv, v_off // d_kv)

    def body(sink_ref, q_ref, kp_ref, kc_ref, vp_ref, vc_ref, o_ref):
        n = pl.program_id(0)
        dist, valid = _attn_geometry(n)
        for g in range(nkv):
            ks = slice(g * HEAD_DIM, (g + 1) * HEAD_DIM)
            k2 = jnp.concatenate([kp_ref[:, ks], kc_ref[:, ks]], axis=0).astype(BF16)
            v2 = jnp.concatenate([vp_ref[:, ks], vc_ref[:, ks]], axis=0).astype(BF16)
            slope_col = _per_head_column([2.0 ** (-8.0 * (g * GROUP + hh + 1) / nq) for hh in range(GROUP)])
            sink_col = _per_head_column([sink_ref[0, g * GROUP + hh] for hh in range(GROUP)])
            p, _ = _attn_probs(_stack_heads(q_ref, g), k2, slope_col, sink_col, dist, valid)
            o = jnp.dot(p.astype(BF16), v2, preferred_element_type=F32).astype(BF16)
            for hh in range(GROUP):
                h = g * GROUP + hh
                o_ref[:, h * HEAD_DIM:(h + 1) * HEAD_DIM] = o[hh * ATTN_BLOCK:(hh + 1) * ATTN_BLOCK, :]

    (out,), carried = _call(
        body, name="attn_fwd", out_shape=[jax.ShapeDtypeStruct((T, d_attn), BF16)], grid=(T // ATTN_BLOCK,),
        in_specs=[pl.BlockSpec(memory_space=pltpu.SMEM), bq, kp, kc, vp, vc],
        out_specs=[pl.BlockSpec((ATTN_BLOCK, d_attn), lambda n: (n, 0))], scratch_shapes=[],
        args=(sinks, proj, proj, proj, proj, proj), sem=("parallel",), rider=rider)
    return out, carried


def _attn_bwd(proj, d_attn_out, sinks, nq, cols, after=None):
    T = proj.shape[0]
    nkv = nq // GROUP
    d_attn, d_kv = nq * HEAD_DIM, nkv * HEAD_DIM
    q_off, k_off, v_off = cols
    bq, kp, kc, vp, vc = _attn_specs(T, d_attn, d_kv, q_off // d_attn, k_off // d_kv, v_off // d_kv)
    scale = HEAD_DIM ** -0.5
    dn_t = (((1,), (1,)), ((), ()))
    dn_r = (((0,), (0,)), ((), ()))

    def body(sink_ref, q_ref, kp_ref, kc_ref, vp_ref, vc_ref, do_ref, dq_ref, dk_ref, dv_ref, ds_ref):
        n = pl.program_id(0)

        @pl.when(n == 0)
        def _():
            dk_ref[...] = jnp.zeros_like(dk_ref)
            dv_ref[...] = jnp.zeros_like(dv_ref)
            ds_ref[...] = jnp.zeros_like(ds_ref)

        dist, valid = _attn_geometry(n)
        rows_c = pl.ds(pl.multiple_of(n * ATTN_BLOCK, ATTN_BLOCK), ATTN_BLOCK)
        rows_p = pl.ds(pl.multiple_of(jnp.maximum(n - 1, 0) * ATTN_BLOCK, ATTN_BLOCK), ATTN_BLOCK)
        lane = lax.broadcasted_iota(jnp.int32, ds_ref.shape, 1)
        srow = lax.broadcasted_iota(jnp.int32, ds_ref.shape, 0)
        ds_acc = jnp.zeros(ds_ref.shape, F32)
        for g in range(nkv):
            ks = slice(g * HEAD_DIM, (g + 1) * HEAD_DIM)
            k2 = jnp.concatenate([kp_ref[:, ks], kc_ref[:, ks]], axis=0).astype(BF16)
            v2 = jnp.concatenate([vp_ref[:, ks], vc_ref[:, ks]], axis=0).astype(BF16)
            slope_col = _per_head_column([2.0 ** (-8.0 * (g * GROUP + hh + 1) / nq) for hh in range(GROUP)])
            sink_col = _per_head_column([sink_ref[0, g * GROUP + hh] for hh in range(GROUP)])
            q_s = _stack_heads(q_ref, g)
            do_s = _stack_heads(do_ref, g)
            p, p_sink = _attn_probs(q_s, k2, slope_col, sink_col, dist, valid)
            dp = lax.dot_general(do_s, v2, dn_t, preferred_element_type=F32)
            delta = jnp.sum(p * dp, axis=1, keepdims=True)
            ds = (p * (dp - delta)).astype(BF16)
            sink_part = p_sink * delta
            dq = (jnp.dot(ds, k2, preferred_element_type=F32) * scale).astype(BF16)
            for hh in range(GROUP):
                h = g * GROUP + hh
                blk = slice(hh * ATTN_BLOCK, (hh + 1) * ATTN_BLOCK)
                dq_ref[:, h * HEAD_DIM:(h + 1) * HEAD_DIM] = dq[blk, :]
                ds_acc = ds_acc + jnp.where(jnp.logical_and(lane == h, srow == 0), -jnp.sum(sink_part[blk, :]), 0.0)
            dk2 = lax.dot_general(ds, q_s, dn_r, preferred_element_type=F32) * scale
            dv2 = lax.dot_general(p.astype(BF16), do_s, dn_r, preferred_element_type=F32)
            dk_ref[rows_p, ks] += dk2[:ATTN_BLOCK, :]
            dv_ref[rows_p, ks] += dv2[:ATTN_BLOCK, :]
            dk_ref[rows_c, ks] += dk2[ATTN_BLOCK:, :]
            dv_ref[rows_c, ks] += dv2[ATTN_BLOCK:, :]
        ds_ref[...] += ds_acc

    out_shape = (jax.ShapeDtypeStruct((T, d_attn), BF16), jax.ShapeDtypeStruct((T, d_kv), F32),
                 jax.ShapeDtypeStruct((T, d_kv), F32), jax.ShapeDtypeStruct((8, LANES), F32))
    return _call(
        body, name="attn_bwd", out_shape=out_shape, grid=(T // ATTN_BLOCK,),
        in_specs=[pl.BlockSpec(memory_space=pltpu.SMEM), bq, kp, kc, vp, vc,
                  pl.BlockSpec((ATTN_BLOCK, d_attn), lambda n: (n, 0))],
        out_specs=(pl.BlockSpec((ATTN_BLOCK, d_attn), lambda n: (n, 0)),
                   pl.BlockSpec((T, d_kv), lambda n: (0, 0)), pl.BlockSpec((T, d_kv), lambda n: (0, 0)),
                   pl.BlockSpec((8, LANES), lambda n: (0, 0))),
        scratch_shapes=[], args=(sinks, proj, proj, proj, proj, proj, d_attn_out), sem=("arbitrary",), after=after)[0]


def _rnn_tile(T):
    return _pick(T, (256, 128))


def _rnn_gates(x_ext, cw_ref, cb_ref, wa_ref, wi_ref, ba_ref, bi_ref, lam_ref, tt):
    xs = [pltpu.roll(x_ext, 3 - k, 0)[8:, :] if k < 3 else x_ext[8:, :] for k in range(4)]
    cx = cb_ref[...] + xs[0] * cw_ref[0:1, :]
    for k in range(1, 4):
        cx = cx + xs[k] * cw_ref[k:k + 1, :]
    cxb = cx.astype(BF16)
    r = jax.nn.sigmoid(jnp.dot(cxb, wa_ref[...], preferred_element_type=F32) + ba_ref[...])
    i = jax.nn.sigmoid(jnp.dot(cxb, wi_ref[...], preferred_element_type=F32) + bi_ref[...])
    lam = lam_ref[...]
    sp = jnp.maximum(-lam, 0.0) + jnp.log1p(jnp.exp(-jnp.abs(lam)))
    log_a = -LRU_C * r * sp
    a = jnp.exp(log_a)
    z = 2.0 * log_a
    em1 = jnp.where(z > -1e-2, z * (1.0 + z * (0.5 + z * (1.0 / 6.0 + z * (1.0 / 24.0)))), jnp.exp(z) - 1.0)
    s = jnp.sqrt(-em1)
    return xs, cx, r, i, sp, a, s


def _rnn_specs(T, gw, tt, rx_blk, ry_blk, rev):
    nT = T // tt
    hb = tt // 8

    def tile(t):
        return (nT - 1 - t) if rev else t

    rx = pl.BlockSpec((tt, gw), lambda g, t: (tile(t), rx_blk + g))
    rx_halo = pl.BlockSpec((8, gw), lambda g, t: (jnp.maximum(tile(t) * hb - 1, 0), rx_blk + g))
    ry = pl.BlockSpec((tt, gw), lambda g, t: (tile(t), ry_blk + g))
    cw = pl.BlockSpec((4, gw), lambda g, t: (0, g))
    vec = pl.BlockSpec((1, gw), lambda g, t: (0, g))
    wg = pl.BlockSpec((None, gw, gw), lambda g, t: (g, 0, 0))
    act = pl.BlockSpec((tt, gw), lambda g, t: (tile(t), g))
    act_halo = pl.BlockSpec((8, gw), lambda g, t: (jnp.maximum(tile(t) * hb - 1, 0), g))
    return rx, rx_halo, ry, cw, vec, wg, act, act_halo, tile


def _rnn_fwd(proj, cols, conv_w, conv_b, wa_g, wi_g, ba, bi, lam, rider=None):
    T = proj.shape[0]
    G, gw, _ = wa_g.shape
    d_rnn = G * gw
    tt = _rnn_tile(T)
    rx_off, ry_off = cols
    rx, rx_halo, ry, cw, vec, wg, act, _, _ = _rnn_specs(T, gw, tt, rx_off // gw, ry_off // gw, False)

    def body(rx_ref, rxh_ref, ry_ref, cw_ref, cb_ref, wa_ref, wi_ref, ba_ref, bi_ref, lam_ref,
             b_ref, h_ref, carry):
        t = pl.program_id(1)

        @pl.when(t == 0)
        def _():
            carry[...] = jnp.zeros_like(carry)

        halo = jnp.where(t > 0, rxh_ref[...], 0.0)
        x_ext = jnp.concatenate([halo, rx_ref[...]], axis=0)
        _, cx, _, i, _, a, s = _rnn_gates(x_ext, cw_ref, cb_ref, wa_ref, wi_ref, ba_ref, bi_ref, lam_ref, tt)
        acc_a, acc_b = a, s * (i * cx)
        d = 1
        while d < tt:
            acc_b = acc_a * _shift_down(acc_b, d, 0.0) + acc_b
            acc_a = acc_a * _shift_down(acc_a, d, 1.0)
            d *= 2
        h = acc_b + acc_a * carry[7:8, :]
        carry[...] = h[tt - 8:, :]
        h_ref[...] = h
        b_ref[...] = (h * _gelu(ry_ref[...])).astype(BF16)

    return _call(
        body, name="rnn_fwd",
        out_shape=(jax.ShapeDtypeStruct((T, d_rnn), BF16), jax.ShapeDtypeStruct((T, d_rnn), F32)),
        grid=(G, T // tt),
        in_specs=[rx, rx_halo, ry, cw, vec, wg, wg, vec, vec, vec], out_specs=(act, act),
        scratch_shapes=[pltpu.VMEM((8, gw), F32)],
        args=(proj, proj, proj, conv_w, conv_b, wa_g, wi_g, ba, bi, lam), sem=("parallel", "arbitrary"), rider=rider)


def _rnn_bwd(proj, cols, h_all, d_b, conv_w, conv_b, wa_g, wi_g, ba, bi, lam, rider=None):
    T = proj.shape[0]
    G, gw, _ = wa_g.shape
    d_rnn = G * gw
    tt = _rnn_tile(T)
    nT = T // tt
    rx_off, ry_off = cols
    rx, rx_halo, ry, cw, vec, wg, act, act_halo, _ = _rnn_specs(T, gw, tt, rx_off // gw, ry_off // gw, True)
    dn_t = (((1,), (1,)), ((), ()))
    dn_r = (((0,), (0,)), ((), ()))

    def body(rx_ref, rxh_ref, ry_ref, h_ref, hh_ref, db_ref, cw_ref, cb_ref, wa_ref, wi_ref, ba_ref, bi_ref, lam_ref,
             drx_ref, dry_ref, dcw_ref, dcb_ref, dba_ref, dbi_ref, dlam_ref, dwa_ref, dwi_ref,
             lam_carry, dcx_carry):
        t = pl.program_id(1)
        first_tile = t == nT - 1

        @pl.when(t == 0)
        def _():
            lam_carry[...] = jnp.zeros_like(lam_carry)
            dcx_carry[...] = jnp.zeros_like(dcx_carry)
            dcw_ref[...] = jnp.zeros_like(dcw_ref)
            dcb_ref[...] = jnp.zeros_like(dcb_ref)
            dba_ref[...] = jnp.zeros_like(dba_ref)
            dbi_ref[...] = jnp.zeros_like(dbi_ref)
            dlam_ref[...] = jnp.zeros_like(dlam_ref)
            dwa_ref[...] = jnp.zeros_like(dwa_ref)
            dwi_ref[...] = jnp.zeros_like(dwi_ref)

        halo = jnp.where(first_tile, 0.0, rxh_ref[...])
        x_ext = jnp.concatenate([halo, rx_ref[...]], axis=0)
        xs, cx, r, i, sp, a, s = _rnn_gates(x_ext, cw_ref, cb_ref, wa_ref, wi_ref, ba_ref, bi_ref, lam_ref, tt)
        h = h_ref[...]
        h_halo = jnp.where(first_tile, 0.0, hh_ref[...])
        h_prev = pltpu.roll(jnp.concatenate([h_halo, h], axis=0), 1, 0)[8:, :]
        gel, dgel = _gelu_and_grad(ry_ref[...])
        d_b_t = db_ref[...]
        dry_ref[...] = (d_b_t * h * dgel).astype(BF16)
        dh = d_b_t * gel

        acc_c = _shift_up(a, 1, 1.0)
        acc_l = dh
        d = 1
        while d < tt:
            acc_l = acc_c * _shift_up(acc_l, d, 0.0) + acc_l
            acc_c = acc_c * _shift_up(acc_c, d, 1.0)
            d *= 2
        lam_t = acc_l + acc_c * lam_carry[0:1, :]
        lam_carry[...] = (a * lam_t)[0:8, :]

        icx = i * cx
        d_s = lam_t * icx
        d_i = lam_t * s * cx
        dcx = lam_t * s * i
        d_a = lam_t * h_prev - d_s * (a / s)
        dlog_a = d_a * a
        d_r = dlog_a * (-LRU_C * sp)
        lam = lam_ref[...]
        dlam_ref[...] += jnp.sum(dlog_a * r, axis=0, keepdims=True) * (LRU_C * jax.nn.sigmoid(-lam))
        dpr = d_r * r * (1.0 - r)
        dpi = d_i * i * (1.0 - i)
        dba_ref[...] += jnp.sum(dpr, axis=0, keepdims=True)
        dbi_ref[...] += jnp.sum(dpi, axis=0, keepdims=True)
        cxb = cx.astype(BF16)
        dprb, dpib = dpr.astype(BF16), dpi.astype(BF16)
        dwa_ref[...] += lax.dot_general(cxb, dprb, dn_r, preferred_element_type=F32)
        dwi_ref[...] += lax.dot_general(cxb, dpib, dn_r, preferred_element_type=F32)
        dcx = (dcx + lax.dot_general(dprb, wa_ref[...], dn_t, preferred_element_type=F32)
               + lax.dot_general(dpib, wi_ref[...], dn_t, preferred_element_type=F32))

        dcb_ref[...] += jnp.sum(dcx, axis=0, keepdims=True)
        for k in range(4):
            dcw_ref[k:k + 1, :] += jnp.sum(dcx * xs[k], axis=0, keepdims=True)
        d_ext = jnp.concatenate([dcx, dcx_carry[...]], axis=0)
        drx = dcx * cw_ref[3:4, :]
        for k in range(3):
            drx = drx + pltpu.roll(d_ext, tt + 8 - (3 - k), 0)[:tt, :] * cw_ref[k:k + 1, :]
        drx_ref[...] = drx.astype(BF16)
        dcx_carry[...] = dcx[0:8, :]

    out_shape = (jax.ShapeDtypeStruct((T, d_rnn), BF16), jax.ShapeDtypeStruct((T, d_rnn), BF16),
                 jax.ShapeDtypeStruct((4, d_rnn), F32), jax.ShapeDtypeStruct((1, d_rnn), F32),
                 jax.ShapeDtypeStruct((1, d_rnn), F32), jax.ShapeDtypeStruct((1, d_rnn), F32),
                 jax.ShapeDtypeStruct((1, d_rnn), F32), jax.ShapeDtypeStruct((G, gw, gw), F32),
                 jax.ShapeDtypeStruct((G, gw, gw), F32))
    return _call(
        body, name="rnn_bwd", out_shape=out_shape, grid=(G, nT),
        in_specs=[rx, rx_halo, ry, act, act_halo, act, cw, vec, wg, wg, vec, vec, vec],
        out_specs=(act, act, cw, vec, vec, vec, vec, wg, wg),
        scratch_shapes=[pltpu.VMEM((8, gw), F32), pltpu.VMEM((8, gw), F32)],
        args=(proj, proj, proj, h_all, h_all, d_b, conv_w, conv_b, wa_g, wi_g, ba, bi, lam),
        sem=("parallel", "arbitrary"), rider=rider)


def _merge_fwd(proj, gl_off, b_gate, y_attn, y_rnn, rider=None):
    T, D = y_attn.shape
    tm = _pick(T, (256, 128))
    ct = _pick(math.gcd(gl_off, D), (512, 256, 128))
    oa, orr, nd = gl_off // ct, (gl_off + D) // ct, D // ct

    def body(ga_ref, gr_ref, ba_ref, br_ref, ya_ref, yr_ref, m_ref):
        ga = jax.nn.sigmoid(ga_ref[...] + ba_ref[...])
        gr = jax.nn.sigmoid(gr_ref[...] + br_ref[...])
        m_ref[...] = (ga * ya_ref[...] + gr * yr_ref[...]).astype(BF16)

    blk = pl.BlockSpec((tm, ct), lambda i, j: (i, j))
    (merged,), carried = _call(
        body, name="merge_fwd", out_shape=[jax.ShapeDtypeStruct((T, D), BF16)], grid=(T // tm, nd),
        in_specs=[pl.BlockSpec((tm, ct), lambda i, j: (i, oa + j)), pl.BlockSpec((tm, ct), lambda i, j: (i, orr + j)),
                  pl.BlockSpec((1, ct), lambda i, j: (0, j)), pl.BlockSpec((1, ct), lambda i, j: (0, nd + j)),
                  blk, blk],
        out_specs=[blk], scratch_shapes=[], args=(proj, proj, b_gate, b_gate, y_attn, y_rnn),
        sem=("parallel", "parallel"), rider=rider)
    return merged, carried


def _merge_bwd(proj, gl_off, b_gate, y_attn, y_rnn, d_m):
    T, D = y_attn.shape
    tm = _pick(T, (256, 128))
    ct = _pick(math.gcd(gl_off, D), (512, 256, 128))
    oa, orr, nd = gl_off // ct, (gl_off + D) // ct, D // ct

    def body(ga_ref, gr_ref, ba_ref, br_ref, ya_ref, yr_ref, dm_ref,
             dya_ref, dyr_ref, dga_ref, dgr_ref, dba_ref, dbr_ref):
        i = pl.program_id(1)

        @pl.when(i == 0)
        def _():
            dba_ref[...] = jnp.zeros_like(dba_ref)
            dbr_ref[...] = jnp.zeros_like(dbr_ref)

        ga = jax.nn.sigmoid(ga_ref[...] + ba_ref[...])
        gr = jax.nn.sigmoid(gr_ref[...] + br_ref[...])
        dm = dm_ref[...]
        dya_ref[...] = (dm * ga).astype(BF16)
        dyr_ref[...] = (dm * gr).astype(BF16)
        dga = dm * ya_ref[...] * ga * (1.0 - ga)
        dgr = dm * yr_ref[...] * gr * (1.0 - gr)
        dga_ref[...] = dga.astype(BF16)
        dgr_ref[...] = dgr.astype(BF16)
        dba_ref[...] += jnp.sum(dga, axis=0, keepdims=True)
        dbr_ref[...] += jnp.sum(dgr, axis=0, keepdims=True)

    blk = pl.BlockSpec((tm, ct), lambda j, i: (i, j))
    vec = pl.BlockSpec((1, ct), lambda j, i: (0, j))
    act = jax.ShapeDtypeStruct((T, D), BF16)
    v1 = jax.ShapeDtypeStruct((1, D), F32)
    return pl.pallas_call(
        body, name="merge_bwd", out_shape=(act, act, act, act, v1, v1), grid=(nd, T // tm),
        in_specs=[pl.BlockSpec((tm, ct), lambda j, i: (i, oa + j)), pl.BlockSpec((tm, ct), lambda j, i: (i, orr + j)),
                  vec, pl.BlockSpec((1, ct), lambda j, i: (0, nd + j)), blk, blk, blk],
        out_specs=(blk, blk, blk, blk, vec, vec),
        compiler_params=_cparams(("parallel", "arbitrary")),
    )(proj, proj, b_gate, b_gate, y_attn, y_rnn, d_m)


def _ln_fwd(x_res, delta, g, b, name, rider=None):
    T, D = x_res.shape
    tm = _pick(T, (256, 128))

    def body(x_ref, d_ref, g_ref, b_ref, y_ref, yb_ref, xh_ref, rs_ref):
        z = ALPHA * x_ref[...] + d_ref[...]
        mu = jnp.mean(z, axis=1, keepdims=True)
        zc = z - mu
        var = jnp.mean(zc * zc, axis=1, keepdims=True)
        rstd = lax.rsqrt(var + LN_EPS)
        xh = zc * rstd
        xh_ref[...] = xh
        rs_ref[...] = rstd
        y = xh * g_ref[...] + b_ref[...]
        y_ref[...] = y
        yb_ref[...] = y.astype(BF16)

    row = pl.BlockSpec((tm, D), lambda i: (i, 0))
    vec = pl.BlockSpec((1, D), lambda i: (0, 0))
    return _call(
        body, name=name,
        out_shape=(jax.ShapeDtypeStruct((T, D), F32), jax.ShapeDtypeStruct((T, D), BF16),
                   jax.ShapeDtypeStruct((T, D), F32), jax.ShapeDtypeStruct((T, 1), F32)),
        grid=(T // tm,), in_specs=[row, row, vec, vec],
        out_specs=(row, row, row, pl.BlockSpec((tm, 1), lambda i: (i, 0))),
        scratch_shapes=[], args=(x_res, delta, g, b), sem=("parallel",), rider=rider)


def _ln_bwd_rows(dy, xh, rstd, g):
    dxh = dy * g
    m1 = jnp.mean(dxh, axis=1, keepdims=True)
    m2 = jnp.mean(dxh * xh, axis=1, keepdims=True)
    return rstd * (dxh - m1 - xh * m2)


def _ln_loss_bwd(x_res, delta, g, b, target):
    T, D = x_res.shape
    tm = _pick(T, (256, 128))

    def body(x_ref, d_ref, g_ref, b_ref, t_ref, dz_ref, dzb_ref, loss_ref, dg_ref, db_ref):
        i = pl.program_id(0)

        @pl.when(i == 0)
        def _():
            loss_ref[...] = jnp.zeros_like(loss_ref)
            dg_ref[...] = jnp.zeros_like(dg_ref)
            db_ref[...] = jnp.zeros_like(db_ref)

        z = ALPHA * x_ref[...] + d_ref[...]
        mu = jnp.mean(z, axis=1, keepdims=True)
        zc = z - mu
        var = jnp.mean(zc * zc, axis=1, keepdims=True)
        rstd = lax.rsqrt(var + LN_EPS)
        xh = zc * rstd
        gv = g_ref[...]
        err = xh * gv + b_ref[...] - t_ref[...]
        loss_ref[...] += 0.5 * jnp.sum(jnp.mean(err * err, axis=1, keepdims=True))
        dy = err * (1.0 / D)
        dg_ref[...] += jnp.sum(dy * xh, axis=0, keepdims=True)
        db_ref[...] += jnp.sum(dy, axis=0, keepdims=True)
        dz = _ln_bwd_rows(dy, xh, rstd, gv)
        dz_ref[...] = dz
        dzb_ref[...] = dz.astype(BF16)

    row = pl.BlockSpec((tm, D), lambda i: (i, 0))
    vec = pl.BlockSpec((1, D), lambda i: (0, 0))
    return pl.pallas_call(
        body, name="ln2_loss_bwd",
        out_shape=(jax.ShapeDtypeStruct((T, D), F32), jax.ShapeDtypeStruct((T, D), BF16),
                   jax.ShapeDtypeStruct((8, LANES), F32),
                   jax.ShapeDtypeStruct((1, D), F32), jax.ShapeDtypeStruct((1, D), F32)),
        grid=(T // tm,), in_specs=[row, row, vec, vec, row],
        out_specs=(row, row, pl.BlockSpec((8, LANES), lambda i: (0, 0)), vec, vec),
        compiler_params=_cparams(("arbitrary",)),
    )(x_res, delta, g, b, target)


def _ln_bwd(dy, xh, rstd, g):
    T, D = dy.shape
    tm = _pick(T, (256, 128))

    def body(dy_ref, xh_ref, rs_ref, g_ref, dz_ref, dzb_ref, dg_ref, db_ref):
        i = pl.program_id(0)

        @pl.when(i == 0)
        def _():
            dg_ref[...] = jnp.zeros_like(dg_ref)
            db_ref[...] = jnp.zeros_like(db_ref)

        dyv, xhv = dy_ref[...], xh_ref[...]
        dg_ref[...] += jnp.sum(dyv * xhv, axis=0, keepdims=True)
        db_ref[...] += jnp.sum(dyv, axis=0, keepdims=True)
        dz = _ln_bwd_rows(dyv, xhv, rs_ref[...], g_ref[...])
        dz_ref[...] = dz
        dzb_ref[...] = dz.astype(BF16)

    row = pl.BlockSpec((tm, D), lambda i: (i, 0))
    vec = pl.BlockSpec((1, D), lambda i: (0, 0))
    return pl.pallas_call(
        body, name="ln1_bwd",
        out_shape=(jax.ShapeDtypeStruct((T, D), F32), jax.ShapeDtypeStruct((T, D), BF16),
                   jax.ShapeDtypeStruct((1, D), F32), jax.ShapeDtypeStruct((1, D), F32)),
        grid=(T // tm,), in_specs=[row, row, pl.BlockSpec((tm, 1), lambda i: (i, 0)), vec],
        out_specs=(row, row, vec, vec), compiler_params=_cparams(("arbitrary",)),
    )(dy, xh, rstd, g)


def _ffn_col_tile(T, d_ff):
    return _pick(d_ff, (256, 128)) if T >= 1024 else _pick(d_ff, (512, 256, 128))


def _ffn_gate(gp, cw_ref, cb_ref):
    return (cb_ref[...] + gp * cw_ref[2:3, :] + _shift_down(gp, 1) * cw_ref[1:2, :]
            + _shift_down(gp, 2) * cw_ref[0:1, :])


def _ffn_fwd(up, gpre, conv_w, conv_b, rider=None):
    T, d_ff = up.shape
    ct = _ffn_col_tile(T, d_ff)

    def body(up_ref, gp_ref, cw_ref, cb_ref, f_ref):
        gate = _ffn_gate(gp_ref[...], cw_ref, cb_ref)
        f_ref[...] = (_gelu(gate) * up_ref[...]).astype(BF16)

    col = pl.BlockSpec((T, ct), lambda j: (0, j))
    (f,), carried = _call(
        body, name="ffn_act_fwd", out_shape=[jax.ShapeDtypeStruct((T, d_ff), BF16)], grid=(d_ff // ct,),
        in_specs=[col, col, pl.BlockSpec((3, ct), lambda j: (0, j)), pl.BlockSpec((1, ct), lambda j: (0, j))],
        out_specs=[col], scratch_shapes=[], args=(up, gpre, conv_w, conv_b), sem=("parallel",), rider=rider)
    return f, carried


def _ffn_bwd(up, gpre, conv_w, conv_b, d_f, after=None):
    T, d_ff = up.shape
    ct = _ffn_col_tile(T, d_ff)

    def body(up_ref, gp_ref, cw_ref, cb_ref, df_ref, dup_ref, dgp_ref, dcw_ref, dcb_ref):
        gp = gp_ref[...]
        gate = _ffn_gate(gp, cw_ref, cb_ref)
        gel, dgel = _gelu_and_grad(gate)
        df = df_ref[...]
        dup_ref[...] = (df * gel).astype(BF16)
        dgate = df * up_ref[...] * dgel
        dcb_ref[...] = jnp.sum(dgate, axis=0, keepdims=True)
        dcw_ref[2:3, :] = jnp.sum(dgate * gp, axis=0, keepdims=True)
        dcw_ref[1:2, :] = jnp.sum(dgate * _shift_down(gp, 1), axis=0, keepdims=True)
        dcw_ref[0:1, :] = jnp.sum(dgate * _shift_down(gp, 2), axis=0, keepdims=True)
        dgp = (dgate * cw_ref[2:3, :] + _shift_up(dgate, 1) * cw_ref[1:2, :]
               + _shift_up(dgate, 2) * cw_ref[0:1, :])
        dgp_ref[...] = dgp.astype(BF16)

    col = pl.BlockSpec((T, ct), lambda j: (0, j))
    w3 = pl.BlockSpec((3, ct), lambda j: (0, j))
    v1 = pl.BlockSpec((1, ct), lambda j: (0, j))
    return _call(
        body, name="ffn_act_bwd",
        out_shape=(jax.ShapeDtypeStruct((T, d_ff), BF16), jax.ShapeDtypeStruct((T, d_ff), BF16),
                   jax.ShapeDtypeStruct((3, d_ff), F32), jax.ShapeDtypeStruct((1, d_ff), F32)),
        grid=(d_ff // ct,), in_specs=[col, col, w3, v1, col], out_specs=(col, col, w3, v1),
        scratch_shapes=[], args=(up, gpre, conv_w, conv_b, d_f), sem=("parallel",), after=after)[0]


def _adamw(w, g, m, v, name, after=None):
    R, C = w.shape
    tr = _row_tile(R, C * 4, 8, budget=1280 * 1024)
    c1 = 1.0 / (1.0 - ADAM_B1 ** ADAM_STEP)
    c2 = 1.0 / (1.0 - ADAM_B2 ** ADAM_STEP)

    def body(w_ref, g_ref, m_ref, v_ref, go_ref, d_ref, nm_ref, nv_ref):
        gv = g_ref[...]
        go_ref[...] = gv
        nm = ADAM_B1 * m_ref[...] + (1.0 - ADAM_B1) * gv
        nv = ADAM_B2 * v_ref[...] + (1.0 - ADAM_B2) * (gv * gv)
        nm_ref[...] = nm
        nv_ref[...] = nv
        d_ref[...] = -ADAM_LR * ((nm * c1) / (jnp.sqrt(nv * c2) + ADAM_EPS) + ADAM_WD * w_ref[...])

    blk = pl.BlockSpec((tr, C), lambda r: (r, 0))
    sh = jax.ShapeDtypeStruct((R, C), F32)
    return _call(body, name=name, out_shape=(sh,) * 4, grid=(R // tr,), in_specs=[blk] * 4, out_specs=(blk,) * 4,
                 scratch_shapes=[], args=(w, g, m, v), sem=("parallel",), after=after)[0]


def _group_blocks(w_blocks, per):
    nb, bw, _ = w_blocks.shape
    G = nb // per
    w4 = w_blocks.reshape(G, per, bw, bw)
    rows = []
    for p in range(per):
        parts = [w4[:, p] if q == p else jnp.zeros((G, bw, bw), w_blocks.dtype) for q in range(per)]
        rows.append(jnp.concatenate(parts, axis=2))
    return jnp.concatenate(rows, axis=1)


def _ungroup_blocks(w_groups, per):
    G, gw, _ = w_groups.shape
    bw = gw // per
    blocks = [w_groups[:, p * bw:(p + 1) * bw, p * bw:(p + 1) * bw] for p in range(per)]
    return jnp.stack(blocks, axis=1).reshape(G * per, bw, bw)


def _pack(parts):
    flat = jnp.concatenate([p.reshape(-1).astype(F32) for p in parts])
    n = flat.shape[0]
    rows = -(-n // LANES)
    rows = -(-rows // PACK_ROW_MULT) * PACK_ROW_MULT
    flat = jnp.pad(flat, (0, rows * LANES - n))
    return flat.reshape(rows, LANES)


def _unpack(packed, shapes):
    flat = packed.reshape(-1)
    out, off = [], 0
    for s in shapes:
        n = math.prod(s)
        out.append(flat[off:off + n].reshape(s))
        off += n
    return out


def kernel(x, w_in, b_gate, rnn_conv_w, rnn_conv_b, lru_wa, lru_ba, lru_wi, lru_bi, lru_lambda, attn_sinks, w_attn_proj, w_rnn_proj, w_out, ln1_g, ln1_b, ffn_w_up, ffn_w_gate, ffn_conv_w, ffn_conv_b, ffn_w_down, ln2_g, ln2_b, loss_target, m_w_in, m_b_gate, m_rnn_conv_w, m_rnn_conv_b, m_lru_wa, m_lru_ba, m_lru_wi, m_lru_bi, m_lru_lambda, m_attn_sinks, m_w_attn_proj, m_w_rnn_proj, m_w_out, m_ln1_g, m_ln1_b, m_ffn_w_up, m_ffn_w_gate, m_ffn_conv_w, m_ffn_conv_b, m_ffn_w_down, m_ln2_g, m_ln2_b, v_w_in, v_b_gate, v_rnn_conv_w, v_rnn_conv_b, v_lru_wa, v_lru_ba, v_lru_wi, v_lru_bi, v_lru_lambda, v_attn_sinks, v_w_attn_proj, v_w_rnn_proj, v_w_out, v_ln1_g, v_ln1_b, v_ffn_w_up, v_ffn_w_gate, v_ffn_conv_w, v_ffn_conv_b, v_ffn_w_down, v_ln2_g, v_ln2_b):
    weights = dict(w_in=w_in, b_gate=b_gate, rnn_conv_w=rnn_conv_w, rnn_conv_b=rnn_conv_b, lru_wa=lru_wa,
                   lru_ba=lru_ba, lru_wi=lru_wi, lru_bi=lru_bi, lru_lambda=lru_lambda, attn_sinks=attn_sinks,
                   w_attn_proj=w_attn_proj, w_rnn_proj=w_rnn_proj, w_out=w_out, ln1_g=ln1_g, ln1_b=ln1_b,
                   ffn_w_up=ffn_w_up, ffn_w_gate=ffn_w_gate, ffn_conv_w=ffn_conv_w, ffn_conv_b=ffn_conv_b,
                   ffn_w_down=ffn_w_down, ln2_g=ln2_g, ln2_b=ln2_b)
    m_in = dict(w_in=m_w_in, b_gate=m_b_gate, rnn_conv_w=m_rnn_conv_w, rnn_conv_b=m_rnn_conv_b, lru_wa=m_lru_wa,
                lru_ba=m_lru_ba, lru_wi=m_lru_wi, lru_bi=m_lru_bi, lru_lambda=m_lru_lambda, attn_sinks=m_attn_sinks,
                w_attn_proj=m_w_attn_proj, w_rnn_proj=m_w_rnn_proj, w_out=m_w_out, ln1_g=m_ln1_g, ln1_b=m_ln1_b,
                ffn_w_up=m_ffn_w_up, ffn_w_gate=m_ffn_w_gate, ffn_conv_w=m_ffn_conv_w, ffn_conv_b=m_ffn_conv_b,
                ffn_w_down=m_ffn_w_down, ln2_g=m_ln2_g, ln2_b=m_ln2_b)
    v_in = dict(w_in=v_w_in, b_gate=v_b_gate, rnn_conv_w=v_rnn_conv_w, rnn_conv_b=v_rnn_conv_b, lru_wa=v_lru_wa,
                lru_ba=v_lru_ba, lru_wi=v_lru_wi, lru_bi=v_lru_bi, lru_lambda=v_lru_lambda, attn_sinks=v_attn_sinks,
                w_attn_proj=v_w_attn_proj, w_rnn_proj=v_w_rnn_proj, w_out=v_w_out, ln1_g=v_ln1_g, ln1_b=v_ln1_b,
                ffn_w_up=v_ffn_w_up, ffn_w_gate=v_ffn_w_gate, ffn_conv_w=v_ffn_conv_w, ffn_conv_b=v_ffn_conv_b,
                ffn_w_down=v_ffn_w_down, ln2_g=v_ln2_g, ln2_b=v_ln2_b)
    order = list(weights)

    assert x.shape[0] == 1 and w_in.shape[0] == 1, "one sequence per device, depth 1"
    T, D = x.shape[1], x.shape[2]
    nq = attn_sinks.shape[-1]
    nkv = nq // GROUP
    d_attn, d_kv = nq * HEAD_DIM, nkv * HEAD_DIM
    d_rnn = rnn_conv_b.shape[-1]
    d_ff = ffn_conv_b.shape[-1]
    n_blocks, bw = lru_wa.shape[1], lru_wa.shape[2]
    per = (bw * LANES // math.gcd(bw, LANES)) // bw
    gw = per * bw
    assert n_blocks % per == 0 and d_rnn == n_blocks * bw
    q_off, k_off, v_off = 0, d_attn, d_attn + d_kv
    rx_off = d_attn + 2 * d_kv
    ry_off = rx_off + d_rnn
    gl_off = ry_off + d_rnn
    d_in = gl_off + 2 * D
    assert w_in.shape[-1] * N_SHARDS == d_in
    assert k_off % d_kv == 0 and rx_off % gw == 0 and T % ATTN_BLOCK == 0

    xi, yi, ci = lax.axis_index("x"), lax.axis_index("y"), lax.axis_index("c")
    j_me = 2 * xi + yi
    jc_arr = jnp.stack([j_me, ci]).astype(jnp.int32)

    x0 = x[0]
    x0b = _cast_bf16(x0, "cast_x")
    tgt = loss_target[0]
    big = ["w_in", "w_attn_proj", "w_rnn_proj", "w_out", "ffn_w_up", "ffn_w_gate", "ffn_w_down"]
    own = {n: _cast_bf16_into_slot(weights[n][0], jc_arr, "cast_" + n) for n in big}
    order_arr = jnp.stack([j_me, j_me ^ 2, j_me ^ 1, j_me ^ 3]).astype(jnp.int32)

    rcw_s, fcw_s = _all_gather_small([rnn_conv_w[0], ffn_conv_w[0]])
    rcw = jnp.concatenate([rcw_s[j] for j in range(N_SHARDS)], axis=1)
    fcw = jnp.concatenate([fcw_s[j] for j in range(N_SHARDS)], axis=1)

    wa_g = _group_blocks(lru_wa[0], per).astype(BF16)
    wi_g = _group_blocks(lru_wi[0], per).astype(BF16)

    near, diag = (0, 1), (2,)
    proj, w_in_s = _mm_gathering(x0b, own["w_in"], order_arr, "mm_proj")
    a_out, (w_ap_s, w_rp_s) = _attn_fwd(
        proj, attn_sinks, nq, (q_off, k_off, v_off),
        rider=_gather_rider([own["w_attn_proj"], own["w_rnn_proj"]], _atoms([0]) + _atoms([1], near)))
    (b_out, h_all), (w_rp_s, w_o_s, w_up_s) = _rnn_fwd(
        proj, (rx_off, ry_off), rcw, rnn_conv_b, wa_g, wi_g, lru_ba, lru_bi, lru_lambda,
        rider=_gather_rider([w_rp_s, own["w_out"], own["ffn_w_up"]],
                            _atoms([0], diag) + _atoms([1]) + _atoms([2], near, 0, 2)))
    w_ap, w_rp, w_o = w_ap_s.reshape(d_attn, D), w_rp_s.reshape(d_rnn, D), w_o_s.reshape(D, D)
    y_attn, (w_up_s,) = _mm(a_out, w_ap, name="mm_attn_proj", rider=_gather_rider([w_up_s], _atoms([0], near, 1, 2)))
    y_rnn, (w_up_s,) = _mm(b_out, w_rp, name="mm_rnn_proj", rider=_gather_rider([w_up_s], _atoms([0], diag, 0, 2)))
    merged, (w_gate_s,) = _merge_fwd(proj, gl_off, b_gate, y_attn, y_rnn,
                                     rider=_gather_rider([own["ffn_w_gate"]], _atoms([0], near, 0, 2)))
    mix, (w_up_s,) = _mm(merged, w_o, name="mm_out", rider=_gather_rider([w_up_s], _atoms([0], diag, 1, 2)))
    (x1, x1b, xh1, rstd1), (w_gate_s,) = _ln_fwd(x0, mix, ln1_g, ln1_b, "ln1_fwd",
                                                 rider=_gather_rider([w_gate_s], _atoms([0], near, 1, 2)))
    up, (w_gate_s,) = _mm(x1b, w_up_s, name="mm_up", b_shards=N_SHARDS,
                          rider=_gather_rider([w_gate_s], _atoms([0], diag)))
    gpre, (w_dn_s,) = _mm(x1b, w_gate_s, name="mm_gate", b_shards=N_SHARDS,
                          rider=_gather_rider([own["ffn_w_down"]], _atoms([0], near)))
    f_act, (w_dn_s,) = _ffn_fwd(up, gpre, fcw, ffn_conv_b,
                                rider=_gather_rider([w_dn_s], _atoms([0], diag)))
    w_dn = w_dn_s.reshape(d_ff, D)
    f_out = _mm(f_act, w_dn, name="mm_down")
    dz2, dz2b, loss_acc, dg2, db2 = _ln_loss_bwd(x1, f_out, ln2_g, ln2_b, tgt)

    def pair_sums(arrs, from_sibling, names):
        return [_pair_sum(g, la, jc_arr, "pair_sum_" + n) for g, la, n in zip(arrs, from_sibling, names)]

    def shard_sums(parts, landed, names):
        return [_shard_sum(cp, lb, jc_arr, "shard_sum_" + n) for cp, lb, n in zip(parts, landed, names)]

    halves = {}
    g_down = _mm(f_act, dz2b, name="mm_d_w_down", ta=True, out_dtype=BF16)
    g1 = [g_down.reshape(N_SHARDS, d_ff // N_SHARDS, D)]
    d_f, sib1 = _mm(dz2b, w_dn, name="mm_d_f", tb=True, rider=_pair_rider(g1))
    sent1 = _shard_exchange_start(pair_sums(g1, sib1, ["ffn_w_down"]), "shard_exchange_start_down")
    dup, dgp, d_fcw, d_fcb = _ffn_bwd(up, gpre, fcw, ffn_conv_b, d_f, after=sent1[4])
    g_up = _mm(x1b, dup, name="mm_d_w_up", ta=True, out_dtype=BF16, out_shards=N_SHARDS)
    g_gate = _mm(x1b, dgp, name="mm_d_w_gate", ta=True, out_dtype=BF16, out_shards=N_SHARDS)
    g2 = [g_up, g_gate]
    dx1_a, sib2 = _mm(dup, w_up_s, name="mm_dx1_up", tb=True, b_shards=N_SHARDS, adds=((ALPHA, dz2),),
                      rider=_pair_rider(g2))
    halves["ffn_w_down"], = shard_sums(*_shard_exchange_wait(sent1, dx1_a, "shard_exchange_wait_down"),
                                       ["ffn_w_down"])
    sent2 = _shard_exchange_start(pair_sums(g2, sib2, ["ffn_w_up", "ffn_w_gate"]), "shard_exchange_start_up_gate")
    dx1 = _mm(dgp, w_gate_s, name="mm_dx1_gate", tb=True, b_shards=N_SHARDS, adds=((1.0, dx1_a),), after=sent2[4])
    dz1, dz1b, dg1, db1 = _ln_bwd(dx1, xh1, rstd1, ln1_g)
    g_out = _mm(merged, dz1b, name="mm_d_w_out", ta=True, out_dtype=BF16)
    d_m = _mm(dz1b, w_o, name="mm_d_merged", tb=True)
    dya, dyr, dgl_a, dgl_r, dbg_a, dbg_r = _merge_bwd(proj, gl_off, b_gate, y_attn, y_rnn, d_m)
    g_ap = _mm(a_out, dya, name="mm_d_w_attn_proj", ta=True, out_dtype=BF16)
    g_rp = _mm(b_out, dyr, name="mm_d_w_rnn_proj", ta=True, out_dtype=BF16)
    names3 = ["w_out", "w_attn_proj", "w_rnn_proj"]
    g3 = [g_out.reshape(N_SHARDS, D // N_SHARDS, D), g_ap.reshape(N_SHARDS, d_attn // N_SHARDS, D),
          g_rp.reshape(N_SHARDS, d_rnn // N_SHARDS, D)]
    d_a = _mm(dya, w_ap, name="mm_d_attn", tb=True)
    d_b, sib3 = _mm(dyr, w_rp, name="mm_d_rnn", tb=True, rider=_pair_rider(g3))
    sent3 = _shard_exchange_start(pair_sums(g3, sib3, names3), "shard_exchange_start_mixers")
    dq, dk, dv, dsink = _attn_bwd(proj, d_a, attn_sinks, nq, (q_off, k_off, v_off), after=sent3[4])
    (drx, dry, d_rcw, d_rcb, d_ba, d_bi, d_lam, d_wa_g, d_wi_g), _ = _rnn_bwd(
        proj, (rx_off, ry_off), h_all, d_b, rcw, rnn_conv_b, wa_g, wi_g, lru_ba, lru_bi, lru_lambda)
    halves["ffn_w_up"], halves["ffn_w_gate"] = shard_sums(
        *_shard_exchange_wait(sent2, drx, "shard_exchange_wait_up_gate"), ["ffn_w_up", "ffn_w_gate"])
    d_proj = jnp.concatenate([dq, dk.astype(BF16), dv.astype(BF16), drx, dry, dgl_a, dgl_r], axis=1)
    ffn_names = ["ffn_w_down", "ffn_w_up", "ffn_w_gate"]
    g_in, shared_ffn = _mm(x0b, d_proj, name="mm_d_w_in", ta=True, out_dtype=BF16, out_shards=N_SHARDS,
                           rider=_share_rider([halves[n] for n in ffn_names]))
    halves["w_out"], halves["w_attn_proj"], halves["w_rnn_proj"] = shard_sums(
        *_shard_exchange_wait(sent3, g_in, "shard_exchange_wait_mixers"), names3)

    small_parts = [
        ("loss", loss_acc[0:1, 0:1]),
        ("b_gate", jnp.concatenate([dbg_a, dbg_r], axis=1)),
        ("rnn_conv_w", d_rcw), ("rnn_conv_b", d_rcb),
        ("lru_wa", _ungroup_blocks(d_wa_g, per)), ("lru_ba", d_ba),
        ("lru_wi", _ungroup_blocks(d_wi_g, per)), ("lru_bi", d_bi), ("lru_lambda", d_lam),
        ("attn_sinks", dsink[0:1, 0:nq]),
        ("ln1_g", dg1), ("ln1_b", db1),
        ("ffn_conv_w", d_fcw), ("ffn_conv_b", d_fcb),
        ("ln2_g", dg2), ("ln2_b", db2),
    ]
    packed = _pack([p for _, p in small_parts])
    rs = packed.shape[0]

    def whole(g):
        return g.reshape(2 * g.shape[1], g.shape[2])

    grads = {n: whole(g) for n, g in zip(ffn_names, shared_ffn)}
    out_g, out_d, out_m, out_v = {}, {}, {}, {}

    def adamw(n, after=None):
        shape = weights[n].shape
        two_d = (math.prod(shape[:-1]), shape[-1])
        g2, d2, m2, v2 = _adamw(weights[n].reshape(two_d), grads[n].reshape(two_d), m_in[n].reshape(two_d),
                                v_in[n].reshape(two_d), "adamw_" + n, after=after)
        out_g[n], out_d[n] = g2.reshape(shape), d2.reshape(shape)
        out_m[n], out_v[n] = m2.reshape(shape), v2.reshape(shape)

    g4 = [g_in, packed.reshape(N_SHARDS, rs // N_SHARDS, LANES)]
    sib4 = _run_rider(_pair_rider(g4), "pair_exchange_in_small")
    sent4 = _shard_exchange_start(pair_sums(g4, sib4, ["w_in", "small"]), "shard_exchange_start_in_small",
                                  relations=near)
    grad_x, shared_mix = _mm(d_proj, w_in_s, name="mm_d_x", tb=True, b_shards=N_SHARDS, adds=((ALPHA, dz1),),
                             rider=_share_rider([halves[n] for n in names3]), after=sent4[4])
    grads.update({n: whole(g) for n, g in zip(names3, shared_mix)})
    part4, lands4 = _shard_exchange_wait(sent4, grad_x, "shard_exchange_wait_in_small")
    sent5 = _shard_exchange_start(part4, "shard_exchange_start_in_small_diag", relations=diag, lands=lands4)
    adamw(ffn_names[0], after=sent5[4])
    for n in ffn_names[1:] + names3:
        adamw(n)
    (part_in, part_small), (lb_in, lb_small) = _shard_exchange_wait(sent5, out_d[names3[-1]],
                                                                    "shard_exchange_wait_in_small_diag")
    halves["w_in"], = shard_sums([part_in], [lb_in], ["w_in"])
    eighths = _shard_sum(part_small, lb_small, jc_arr, "shard_sum_small", all_slots=True)
    shared_in, reduced = _run_rider(_share_rider([halves["w_in"]], eighths), "share_in_small")
    grads["w_in"] = whole(shared_in)
    reduced = reduced.reshape(rs, LANES)
    small = dict(zip([n for n, _ in small_parts], _unpack(reduced, [p.shape for _, p in small_parts])))
    loss = small.pop("loss").reshape(())
    rcw_n = d_rnn // N_SHARDS
    fcw_n = d_ff // N_SHARDS
    small["rnn_conv_w"] = lax.dynamic_slice(small["rnn_conv_w"], (0, j_me * rcw_n), (4, rcw_n))
    small["ffn_conv_w"] = lax.dynamic_slice(small["ffn_conv_w"], (0, j_me * fcw_n), (3, fcw_n))
    for n, g in small.items():
        grads[n] = g

    for n in order:
        if n not in out_g:
            adamw(n)

    return (loss, grad_x.reshape(x.shape), *[out_g[n] for n in order], *[out_d[n] for n in order],
            *[out_m[n] for n in order], *[out_v[n] for n in order])
```

```python
import functools
import math

import jax
import jax.numpy as jnp
from jax import lax
from jax.experimental import pallas as pl
from jax.experimental.pallas import tpu as pltpu

F32 = jnp.float32
BF16 = jnp.bfloat16
MESH = pl.DeviceIdType.MESH

HEAD_DIM = 64
GROUP = 8
ATTN_BLOCK = 128
LRU_C = 8.0
LN_EPS = 1e-5
ALPHA = 2.0 ** 0.25
LANES = 128
N_SHARDS = 4
N_DEV = 8
VMEM_LIMIT = 56 * 1024 * 1024
MM_VMEM_BUDGET = 40 * 1024 * 1024
MM_MAX_TILE = 3072
PACK_ROW_MULT = 8 * 64
NEG = -1e30

ADAM_LR, ADAM_B1, ADAM_B2, ADAM_EPS, ADAM_WD, ADAM_STEP = 0.001, 0.9, 0.999, 1e-08, 0.01, 10

GELU_C = math.sqrt(2.0 / math.pi)
GELU_A = 0.044715


def _cparams(sem=None):
    kw = dict(vmem_limit_bytes=VMEM_LIMIT)
    if sem is not None:
        kw["dimension_semantics"] = sem
    return pltpu.CompilerParams(**kw)


def _pick(n, prefs):
    for p in prefs:
        if n % p == 0:
            return p
    return n


def _row_tile(rows, row_bytes, mult, budget=2 * 1024 * 1024):
    best = None
    for d in range(mult, rows + 1, mult):
        if rows % d == 0 and d * row_bytes <= budget:
            best = d
    return best if best is not None else rows


def _gelu(x):
    return 0.5 * x * (1.0 + jnp.tanh(GELU_C * (x + GELU_A * x * x * x)))


def _gelu_and_grad(x):
    t = jnp.tanh(GELU_C * (x + GELU_A * x * x * x))
    g = 0.5 * x * (1.0 + t)
    dg = 0.5 * (1.0 + t) + 0.5 * x * (1.0 - t * t) * GELU_C * (1.0 + 3.0 * GELU_A * x * x)
    return g, dg


def _shift_down(x, s, fill=0.0):
    row = lax.broadcasted_iota(jnp.int32, x.shape, 0)
    return jnp.where(row >= s, pltpu.roll(x, s, 0), fill)


def _shift_up(x, s, fill=0.0):
    n = x.shape[0]
    row = lax.broadcasted_iota(jnp.int32, x.shape, 0)
    return jnp.where(row < n - s, pltpu.roll(x, n - s, 0), fill)


def _mm(a, b, *, name, ta=False, tb=False, out_dtype=F32, adds=(), b_shards=1, out_shards=1,
        tm=None, tn=None, tk=None, rider=None, after=None):
    if ta:
        K, M = a.shape
    else:
        M, K = a.shape
    if b_shards > 1:
        n_sh = b.shape[-1]
        if tb:
            N = b.shape[1]
            assert b_shards * n_sh == K
        else:
            N = b_shards * n_sh
            assert b.shape[1] == K
    else:
        n_sh = None
        if tb:
            N = b.shape[0]
            assert b.shape[1] == K
        else:
            N = b.shape[1]
            assert b.shape[0] == K
    wide = (1024, 1536, 1280, 768, 640, 512, 256, 128)
    if tn is None:
        if b_shards > 1 and not tb:
            tn = n_sh if n_sh <= MM_MAX_TILE else _pick(n_sh, wide)
        elif out_shards > 1:
            tn = N // out_shards if N // out_shards <= MM_MAX_TILE else _pick(N // out_shards, wide)
        else:
            tn = _pick(N, wide)
    if tk is None:
        if b_shards > 1 and tb:
            tk = n_sh if n_sh <= MM_MAX_TILE else _pick(n_sh, wide)
        else:
            tk = K if K <= MM_MAX_TILE else _pick(K, (2048,) + wide)
    assert N % tn == 0 and K % tk == 0, (name, M, N, K, tn, tk)
    nk = K // tk
    n_add = len(adds)
    sa, sb, so = a.dtype.itemsize, b.dtype.itemsize, jnp.dtype(out_dtype).itemsize

    def vmem_bytes(tm_):
        return (2 * (tm_ * tk * sa + tk * tn * sb + tm_ * tn * so + n_add * tm_ * tn * 4)
                + (tm_ * tn * 4 if nk > 1 else 0))

    if tm is None:
        tm = _pick(M, (1024, 512, 256, 128)) if nk > 1 else _pick(M, (512, 256, 128))
        while vmem_bytes(tm) > MM_VMEM_BUDGET and tm % 256 == 0:
            tm //= 2
    assert M % tm == 0, (name, M, tm)
    b_outer = b.size * sb >= a.size * sa

    def ij(g0, g1):
        return (g1, g0) if b_outer else (g0, g1)

    def amap(g0, g1, k):
        i, _ = ij(g0, g1)
        return (k, i) if ta else (i, k)

    def bmap(g0, g1, k):
        _, j = ij(g0, g1)
        if b_shards > 1 and not tb:
            per = n_sh // tn
            return (j // per, k, j % per)
        if b_shards > 1 and tb:
            per = n_sh // tk
            return (k // per, j, k % per)
        return (j, k) if tb else (k, j)

    def omap(g0, g1, k):
        i, j = ij(g0, g1)
        if out_shards > 1:
            per_o = (N // out_shards) // tn
            return (j // per_o, i, j % per_o)
        return (i, j)

    a_spec = pl.BlockSpec((tk, tm) if ta else (tm, tk), amap)
    if b_shards > 1:
        b_spec = pl.BlockSpec((None, tn, tk) if tb else (None, tk, tn), bmap)
    else:
        b_spec = pl.BlockSpec((tn, tk) if tb else (tk, tn), bmap)
    add_specs = [pl.BlockSpec((tm, tn), lambda g0, g1, k: ij(g0, g1)) for _ in adds]
    if out_shards > 1:
        out_spec = pl.BlockSpec((None, tm, tn), omap)
        out_shape = jax.ShapeDtypeStruct((out_shards, M, N // out_shards), out_dtype)
    else:
        out_spec = pl.BlockSpec((tm, tn), omap)
        out_shape = jax.ShapeDtypeStruct((M, N), out_dtype)

    if ta:
        dims = (((0,), (0,)), ((), ()))
    elif tb:
        dims = (((1,), (1,)), ((), ()))
    else:
        dims = (((1,), (0,)), ((), ()))
    scales = tuple(s for s, _ in adds)

    def finish(r, add_refs, o_ref):
        for s, ref in zip(scales, add_refs):
            r = r + s * ref[...].astype(F32)
        o_ref[...] = r.astype(out_dtype)

    def body(a_ref, b_ref, *rest):
        add_refs = rest[:n_add]
        o_ref = rest[n_add]
        part = lax.dot_general(a_ref[...].astype(BF16), b_ref[...].astype(BF16), dims, preferred_element_type=F32)
        if nk == 1:
            finish(part, add_refs, o_ref)
            return
        acc = rest[n_add + 1]
        k = pl.program_id(2)

        @pl.when(k == 0)
        def _():
            acc[...] = part

        @pl.when(k > 0)
        def _():
            acc[...] += part

        @pl.when(k == nk - 1)
        def _():
            finish(acc[...], add_refs, o_ref)

    grid = (N // tn, M // tm, nk) if b_outer else (M // tm, N // tn, nk)
    (res,), carried = _call(
        body, name=name, grid=grid, in_specs=[a_spec, b_spec] + add_specs, out_specs=[out_spec],
        out_shape=[out_shape], scratch_shapes=[pltpu.VMEM((tm, tn), F32)] if nk > 1 else [],
        args=(a, b, *[x for _, x in adds]), sem=("parallel", "parallel", "arbitrary"), rider=rider, after=after)
    return (res, carried) if rider is not None else res


def _cast_bf16(w, name):
    R, C = w.shape
    tr = _row_tile(R, C * 4, 16)

    def body(w_ref, o_ref):
        o_ref[...] = w_ref[...].astype(BF16)

    return pl.pallas_call(
        body, name=name, out_shape=jax.ShapeDtypeStruct((R, C), BF16), grid=(R // tr,),
        in_specs=[pl.BlockSpec((tr, C), lambda r: (r, 0))], out_specs=pl.BlockSpec((tr, C), lambda r: (r, 0)),
        compiler_params=_cparams(("parallel",)),
    )(w)


def _cast_bf16_into_slot(w, jc_arr, name):
    R, C = w.shape
    tr = _row_tile(R, C * 4, 16)

    def body(jc_ref, w_ref, o_ref):
        o_ref[...] = w_ref[...].astype(BF16)

    gs = pltpu.PrefetchScalarGridSpec(
        num_scalar_prefetch=1, grid=(R // tr,),
        in_specs=[pl.BlockSpec((tr, C), lambda r, jc: (r, 0))],
        out_specs=pl.BlockSpec((None, tr, C), lambda r, jc: (jc[0], r, 0)))
    return pl.pallas_call(body, name=name, out_shape=jax.ShapeDtypeStruct((N_SHARDS, R, C), BF16), grid_spec=gs,
                          compiler_params=_cparams(("parallel",)))(jc_arr, w)


def _pair_sum(g, la, jc_arr, name):
    S, R, C = g.shape
    half = R // 2
    tr = _row_tile(half, C * 4, 16)
    nrt = half // tr
    dt = g.dtype

    def body(jc_ref, g_ref, la_ref, o_ref):
        o_ref[...] = (g_ref[...].astype(F32) + la_ref[...].astype(F32)).astype(dt)

    gs = pltpu.PrefetchScalarGridSpec(
        num_scalar_prefetch=1, grid=(S, nrt),
        in_specs=[pl.BlockSpec((None, tr, C), lambda s, r, jc: (s, jc[1] * nrt + r, 0)),
                  pl.BlockSpec((None, tr, C), lambda s, r, jc: (s, r, 0))],
        out_specs=pl.BlockSpec((None, tr, C), lambda s, r, jc: (s, r, 0)))
    return pl.pallas_call(body, name=name, out_shape=jax.ShapeDtypeStruct((S, half, C), dt), grid_spec=gs,
                          compiler_params=_cparams(("parallel", "parallel")))(jc_arr, g, la)


def _shard_sum(cp, lb, jc_arr, name, all_slots=False):
    S, h, C = cp.shape
    tr = _row_tile(h, C * 4, 16)

    def body(jc_ref, cp_ref, l0, l1, l2, o_ref):
        o_ref[...] = ((cp_ref[...].astype(F32) + l0[...].astype(F32)) + l1[...].astype(F32)) + l2[...].astype(F32)

    def lspec(kk):
        return pl.BlockSpec((None, tr, C), lambda r, jc: (kk, r, 0))

    if all_slots:
        out_spec = pl.BlockSpec((None, None, tr, C), lambda r, jc: (jc[0], jc[1], r, 0))
        out_shape = jax.ShapeDtypeStruct((S, 2, h, C), F32)
    else:
        out_spec = pl.BlockSpec((None, tr, C), lambda r, jc: (jc[1], r, 0))
        out_shape = jax.ShapeDtypeStruct((2, h, C), F32)
    gs = pltpu.PrefetchScalarGridSpec(
        num_scalar_prefetch=1, grid=(h // tr,),
        in_specs=[pl.BlockSpec((None, tr, C), lambda r, jc: (jc[0], r, 0)), lspec(0), lspec(1), lspec(2)],
        out_specs=out_spec)
    return pl.pallas_call(body, name=name, out_shape=out_shape, grid_spec=gs,
                          compiler_params=_cparams(("parallel",)))(jc_arr, cp, lb, lb, lb)


ANY = pl.BlockSpec(memory_space=pl.ANY)


def _place():
    x, y, c = lax.axis_index("x"), lax.axis_index("y"), lax.axis_index("c")
    chips = [(1 - x, y), (x, 1 - y), (1 - x, 1 - y)]
    return x, y, c, chips


class _Rider:
    def __init__(self, inputs, out_shape, aliases, sems, start, finish):
        self.inputs, self.out_shape, self.aliases, self.sems = list(inputs), list(out_shape), dict(aliases), list(sems)
        self.start, self.finish = start, finish


def _join_riders(r1, r2):
    i1, o1, s1 = len(r1.inputs), len(r1.out_shape), len(r1.sems)
    aliases = dict(r1.aliases)
    aliases.update({i1 + i: o1 + o for i, o in r2.aliases.items()})

    def start(ins, outs, sems):
        r1.start(ins[:i1], outs[:o1], sems[:s1])
        r2.start(ins[i1:], outs[o1:], sems[s1:])

    def finish(ins, outs, sems):
        r1.finish(ins[:i1], outs[:o1], sems[:s1])
        r2.finish(ins[i1:], outs[o1:], sems[s1:])

    return _Rider(r1.inputs + r2.inputs, r1.out_shape + r2.out_shape, aliases, r1.sems + r2.sems, start, finish)


def _after_rider(x):
    return _Rider([x], [], {}, [], lambda *a: None, lambda *a: None)


def _call(body, *, name, grid, in_specs, out_specs, out_shape, scratch_shapes, args, sem, rider=None, after=None):
    out_specs, out_shape = tuple(out_specs), tuple(out_shape)
    if after is not None:
        rider = _after_rider(after) if rider is None else _join_riders(_after_rider(after), rider)
    if rider is None:
        res = pl.pallas_call(body, name=name, out_shape=out_shape, grid=grid, in_specs=list(in_specs),
                             out_specs=out_specs, scratch_shapes=list(scratch_shapes),
                             compiler_params=_cparams(sem))(*args)
        return tuple(res), []
    n_in, n_out, n_sc = len(in_specs), len(out_specs), len(scratch_shapes)
    r_in, r_out = len(rider.inputs), len(rider.out_shape)

    def wrapped(*refs):
        p = 0
        host_in = refs[p:p + n_in]; p += n_in
        rid_in = refs[p:p + r_in]; p += r_in
        host_out = refs[p:p + n_out]; p += n_out
        rid_out = refs[p:p + r_out]; p += r_out
        host_sc = refs[p:p + n_sc]; p += n_sc
        rid_sem = refs[p:]
        first = functools.reduce(jnp.logical_and, [pl.program_id(a) == 0 for a in range(len(grid))])
        last = functools.reduce(jnp.logical_and, [pl.program_id(a) == grid[a] - 1 for a in range(len(grid))])

        @pl.when(first)
        def _():
            rider.start(rid_in, rid_out, rid_sem)

        body(*host_in, *host_out, *host_sc)

        @pl.when(last)
        def _():
            rider.finish(rid_in, rid_out, rid_sem)

    res = pl.pallas_call(
        wrapped, name=name, out_shape=out_shape + tuple(rider.out_shape), grid=grid,
        in_specs=list(in_specs) + [ANY] * r_in, out_specs=out_specs + (ANY,) * r_out,
        input_output_aliases={n_in + i: n_out + o for i, o in rider.aliases.items()},
        scratch_shapes=list(scratch_shapes) + rider.sems,
        compiler_params=_cparams(("arbitrary",) * len(grid)),
    )(*args, *rider.inputs)
    return tuple(res[:n_out]), list(res[n_out:])


def _run_rider(rider, name):
    def body(*refs):
        r_in, r_out = len(rider.inputs), len(rider.out_shape)
        ins, outs, sems = refs[:r_in], refs[r_in:r_in + r_out], refs[r_in + r_out:]
        rider.start(ins, outs, sems)
        rider.finish(ins, outs, sems)

    return pl.pallas_call(
        body, name=name, out_shape=rider.out_shape, in_specs=[ANY] * len(rider.inputs),
        out_specs=[ANY] * len(rider.out_shape), input_output_aliases=rider.aliases, scratch_shapes=rider.sems,
    )(*rider.inputs)


def _atoms(indices, kks=(0, 1, 2), q=0, nq=1):
    return [(i, kk, q, nq) for i in indices for kk in kks]


def _gather_rider(bufs, atoms=None):
    n = len(bufs)
    if atoms is None:
        atoms = _atoms(range(n))
    na = len(atoms)

    def rows(out, atom, core):
        i, _, q, nq = atom
        half = out[i].shape[1] // 2
        assert half % (16 * nq) == 0, (half, nq)
        return pl.ds(core * half + q * (half // nq), half // nq)

    def ici_copy(out, sems, a, slot, peer):
        c = lax.axis_index("c")
        blk = out[atoms[a][0]].at[slot, rows(out, atoms[a], c), :]
        return pltpu.make_async_remote_copy(
            src_ref=blk, dst_ref=blk, send_sem=sems[0].at[a], recv_sem=sems[1].at[a],
            device_id=(peer[0], peer[1], c), device_id_type=MESH)

    def d2d_copy(out, sems, a, slot, from_core):
        x, y, c, _ = _place()
        blk = out[atoms[a][0]].at[slot, rows(out, atoms[a], from_core), :]
        return pltpu.make_async_remote_copy(
            src_ref=blk, dst_ref=blk, send_sem=sems[2].at[a], recv_sem=sems[3].at[a],
            device_id=(x, y, 1 - c), device_id_type=MESH)

    def start(ins, out, sems):
        x, y, c, chips = _place()
        for a in range(na):
            ici_copy(out, sems, a, 2 * x + y, chips[atoms[a][1]]).start()

    def finish(ins, out, sems):
        x, y, c, chips = _place()
        src = [2 * chips[atoms[a][1]][0] + chips[atoms[a][1]][1] for a in range(na)]
        for a in range(na):
            ici_copy(out, sems, a, src[a], chips[atoms[a][1]]).wait_recv()
            d2d_copy(out, sems, a, src[a], c).start()
        for a in range(na):
            d2d_copy(out, sems, a, src[a], 1 - c).wait_recv()
        for a in range(na):
            ici_copy(out, sems, a, 2 * x + y, chips[atoms[a][1]]).wait_send()
            d2d_copy(out, sems, a, src[a], c).wait_send()

    return _Rider(bufs, [jax.ShapeDtypeStruct(s.shape, s.dtype) for s in bufs], {i: i for i in range(n)},
                  [pltpu.SemaphoreType.DMA((na,))] * 4, start, finish)


def _mm_gathering(a, buf, order_arr, name):
    M, K = a.shape
    S, _, n = buf.shape
    tm = _pick(M, (512, 256, 128))
    n_i = M // tm
    half = K // 2

    def body(order_ref, a_ref, w_in_ref, o_ref, w_ref, b_vmem, load_sem, s_ici, r_ici, s_d2d, r_d2d):
        s, i = pl.program_id(0), pl.program_id(1)
        x, y, c, chips = _place()
        j_me = 2 * x + y
        slots = [2 * px + py for px, py in chips]

        def ici(kk, slot):
            blk = w_ref.at[slot, pl.ds(c * half, half), :]
            return pltpu.make_async_remote_copy(
                src_ref=blk, dst_ref=blk, send_sem=s_ici.at[kk], recv_sem=r_ici.at[kk],
                device_id=(chips[kk][0], chips[kk][1], c), device_id_type=MESH)

        def d2d(kk, from_core):
            blk = w_ref.at[slots[kk], pl.ds(from_core * half, half), :]
            return pltpu.make_async_remote_copy(
                src_ref=blk, dst_ref=blk, send_sem=s_d2d.at[kk], recv_sem=r_d2d.at[kk],
                device_id=(x, y, 1 - c), device_id_type=MESH)

        def load(slot, b):
            return pltpu.make_async_copy(w_ref.at[slot], b_vmem.at[b], load_sem.at[b])

        @pl.when(jnp.logical_and(s == 0, i == 0))
        def _():
            for kk in range(3):
                ici(kk, j_me).start()
            load(j_me, 0).start()

        @pl.when(i == 0)
        def _():
            load(order_ref[s], s % 2).wait()

        o_ref[...] = jnp.dot(a_ref[...], b_vmem[s % 2], preferred_element_type=F32)

        last = i == n_i - 1

        @pl.when(jnp.logical_and(last, s == 0))
        def _():
            ici(0, slots[0]).wait_recv()
            d2d(0, c).start()
            ici(1, slots[1]).wait_recv()
            d2d(1, c).start()
            d2d(0, 1 - c).wait_recv()
            load(slots[0], 1).start()

        @pl.when(jnp.logical_and(last, s == 1))
        def _():
            d2d(1, 1 - c).wait_recv()
            load(slots[1], 0).start()

        @pl.when(jnp.logical_and(last, s == 2))
        def _():
            ici(2, slots[2]).wait_recv()
            d2d(2, c).start()
            d2d(2, 1 - c).wait_recv()
            load(slots[2], 1).start()

        @pl.when(jnp.logical_and(last, s == 3))
        def _():
            for kk in range(3):
                ici(kk, j_me).wait_send()
                d2d(kk, c).wait_send()

    gs = pltpu.PrefetchScalarGridSpec(
        num_scalar_prefetch=1, grid=(S, n_i),
        in_specs=[pl.BlockSpec((tm, K), lambda s, i, order: (i, 0)), ANY],
        out_specs=[pl.BlockSpec((tm, n), lambda s, i, order: (i, order[s])), ANY],
        scratch_shapes=[pltpu.VMEM((2, K, n), BF16), pltpu.SemaphoreType.DMA((2,))]
        + [pltpu.SemaphoreType.DMA((3,))] * 4)
    return pl.pallas_call(
        body, name=name, grid_spec=gs,
        out_shape=[jax.ShapeDtypeStruct((M, S * n), F32), jax.ShapeDtypeStruct(buf.shape, buf.dtype)],
        input_output_aliases={2: 1}, compiler_params=_cparams(("arbitrary", "arbitrary")),
    )(order_arr, a, buf)


def _all_gather_small(shards):
    n = len(shards)

    def body(*refs):
        w = refs[:n]
        out = refs[n:2 * n]
        local_sem, s_sem, r_sem = refs[2 * n:]
        x, y, c, chips = _place()
        j_me = 2 * x + y
        cps = []
        for i in range(n):
            lc = pltpu.make_async_copy(w[i], out[i].at[j_me], local_sem.at[i])
            lc.start()
            cps.append(lc)
        sends = []
        for i in range(n):
            for kk, (px, py) in enumerate(chips):
                cp = pltpu.make_async_remote_copy(
                    src_ref=w[i], dst_ref=out[i].at[j_me], send_sem=s_sem.at[3 * i + kk],
                    recv_sem=r_sem.at[3 * i + kk], device_id=(px, py, c), device_id_type=MESH)
                cp.start()
                sends.append(cp)
        for i in range(n):
            for kk, (px, py) in enumerate(chips):
                sends[3 * i + kk].wait_send()
                pltpu.make_async_remote_copy(
                    src_ref=w[i], dst_ref=out[i].at[2 * px + py], send_sem=s_sem.at[3 * i + kk],
                    recv_sem=r_sem.at[3 * i + kk], device_id=(px, py, c), device_id_type=MESH).wait_recv()
        for lc in cps:
            lc.wait()

    out_shape = [jax.ShapeDtypeStruct((N_SHARDS,) + s.shape, s.dtype) for s in shards]
    return pl.pallas_call(
        body, name="all_gather_conv_weights", out_shape=out_shape, in_specs=[ANY] * n, out_specs=[ANY] * n,
        scratch_shapes=[pltpu.SemaphoreType.DMA((n,)), pltpu.SemaphoreType.DMA((3 * n,)),
                        pltpu.SemaphoreType.DMA((3 * n,))],
    )(*shards)


def _pair_rider(grads):
    n = len(grads)

    def copies(g, la, sems):
        x, y, c, _ = _place()
        return [pltpu.make_async_remote_copy(
            src_ref=g[i].at[:, pl.ds((1 - c) * (g[i].shape[1] // 2), g[i].shape[1] // 2), :], dst_ref=la[i],
            send_sem=sems[0].at[i], recv_sem=sems[1].at[i], device_id=(x, y, 1 - c), device_id_type=MESH)
            for i in range(n)]

    def start(g, la, sems):
        for cp in copies(g, la, sems):
            cp.start()

    def finish(g, la, sems):
        for cp in copies(g, la, sems):
            cp.wait()

    return _Rider(grads, [jax.ShapeDtypeStruct((s.shape[0], s.shape[1] // 2, s.shape[2]), s.dtype) for s in grads],
                  {}, [pltpu.SemaphoreType.DMA((n,)), pltpu.SemaphoreType.DMA((n,))], start, finish)


def _shard_exchange_rider(cps_in, atoms=None):
    n = len(cps_in)
    if atoms is None:
        atoms = _atoms(range(n))

    def copies(ins, lb, sems):
        x, y, c, chips = _place()
        out = []
        for a, (i, kk, q, nq) in enumerate(atoms):
            h = ins[i].shape[1]
            assert h % (16 * nq) == 0, (h, nq)
            rows = pl.ds(q * (h // nq), h // nq)
            px, py = chips[kk]
            out.append(pltpu.make_async_remote_copy(
                src_ref=ins[i].at[2 * px + py, rows, :], dst_ref=lb[i].at[kk, rows, :],
                send_sem=sems[0].at[a], recv_sem=sems[1].at[a], device_id=(px, py, c), device_id_type=MESH))
        return out

    def start(ins, lb, sems):
        for cp in copies(ins, lb, sems):
            cp.start()

    def finish(ins, lb, sems):
        for cp in copies(ins, lb, sems):
            cp.wait()

    return _Rider(cps_in, [jax.ShapeDtypeStruct((3,) + s.shape[1:], s.dtype) for s in cps_in], {},
                  [pltpu.SemaphoreType.DMA((len(atoms),)), pltpu.SemaphoreType.DMA((len(atoms),))], start, finish)


HBM = pl.BlockSpec(memory_space=pltpu.HBM)
SEM = pl.BlockSpec(memory_space=pltpu.SEMAPHORE)


def _shard_copies(part_refs, land_refs, send_sems, recv_sems, relations):
    x, y, c, chips = _place()
    nr = len(relations)
    return [pltpu.make_async_remote_copy(
        src_ref=part_refs[i].at[2 * chips[kk][0] + chips[kk][1]], dst_ref=land_refs[i].at[kk],
        send_sem=send_sems.at[nr * i + r], recv_sem=recv_sems.at[nr * i + r],
        device_id=(chips[kk][0], chips[kk][1], c), device_id_type=MESH)
        for i in range(len(part_refs)) for r, kk in enumerate(relations)]


SIDE_EFFECT = pltpu.SideEffectType.DATAFLOW_SIDE_EFFECTING


def _shard_exchange_start(parts, name, relations=(0, 1, 2), lands=None):
    n = len(parts)
    ns = n * len(relations)

    def body(*refs):
        part_refs, land_refs = refs[:n], refs[n:2 * n]
        send_sems, recv_sems = refs[2 * n], refs[2 * n + 1]
        token = refs[4 * n + 2]
        for cp in _shard_copies(part_refs, land_refs, send_sems, recv_sems, relations):
            cp.start()
        token[...] = jnp.zeros_like(token)

    if lands is None:
        lands = [lax.empty((3,) + p.shape[1:], p.dtype) for p in parts]
    bufs = list(parts) + list(lands)
    res = pl.pallas_call(
        body, name=name,
        out_shape=(pltpu.SemaphoreType.DMA((ns,)), pltpu.SemaphoreType.DMA((ns,)),
                   *[pltpu.HBM(b.shape, b.dtype) for b in bufs], jax.ShapeDtypeStruct((8, LANES), F32)),
        in_specs=(HBM,) * (2 * n), out_specs=(SEM, SEM) + (HBM,) * (2 * n) + (pl.BlockSpec(memory_space=pltpu.VMEM),),
        input_output_aliases={i: 2 + i for i in range(2 * n)},
        compiler_params=pltpu.CompilerParams(has_side_effects=SIDE_EFFECT),
    )(*[pltpu.with_memory_space_constraint(b, pltpu.HBM) for b in bufs])
    return res[0], res[1], list(res[2:2 + n]), list(res[2 + n:2 + 2 * n]), res[2 + 2 * n], relations


def _shard_exchange_wait(started, after, name):
    send_sems, recv_sems, parts, lands, _, relations = started
    n = len(parts)

    def body(*refs):
        part_refs, land_refs = refs[:n], refs[n:2 * n]
        send_sems_ref, recv_sems_ref = refs[2 * n], refs[2 * n + 1]
        for cp in _shard_copies(part_refs, land_refs, send_sems_ref, recv_sems_ref, relations):
            cp.wait_send()
            cp.wait_recv()

    bufs = parts + lands
    res = pl.pallas_call(
        body, name=name, out_shape=tuple(pltpu.HBM(b.shape, b.dtype) for b in bufs),
        in_specs=(HBM,) * (2 * n) + (SEM, SEM, ANY), out_specs=(HBM,) * (2 * n),
        input_output_aliases={i: i for i in range(2 * n)},
        compiler_params=pltpu.CompilerParams(has_side_effects=SIDE_EFFECT),
    )(*bufs, send_sems, recv_sems, after)
    return list(res[:n]), list(res[n:])


def _share_rider(halves, eighths=None):
    n = len(halves)
    bufs = list(halves) + ([eighths] if eighths is not None else [])

    def half_copy(out, sems, i, core):
        x, y, c, _ = _place()
        blk = out[i].at[core]
        return pltpu.make_async_remote_copy(src_ref=blk, dst_ref=blk, send_sem=sems[0].at[i], recv_sem=sems[1].at[i],
                                            device_id=(x, y, 1 - c), device_id_type=MESH)

    def eighth_copy(out, sems, r, mine):
        x, y, c, _ = _place()
        px, py, pc = x ^ ((r >> 2) & 1), y ^ ((r >> 1) & 1), c ^ (r & 1)
        blk = out[n].at[2 * x + y, c] if mine else out[n].at[2 * px + py, pc]
        return pltpu.make_async_remote_copy(src_ref=blk, dst_ref=blk, send_sem=sems[2].at[r - 1],
                                            recv_sem=sems[3].at[r - 1], device_id=(px, py, pc), device_id_type=MESH)

    def start(ins, out, sems):
        c = lax.axis_index("c")
        for i in range(n):
            half_copy(out, sems, i, c).start()
        if eighths is not None:
            for r in range(1, N_DEV):
                eighth_copy(out, sems, r, True).start()

    def finish(ins, out, sems):
        c = lax.axis_index("c")
        for i in range(n):
            half_copy(out, sems, i, 1 - c).wait_recv()
        if eighths is not None:
            for r in range(1, N_DEV):
                eighth_copy(out, sems, r, False).wait_recv()
        for i in range(n):
            half_copy(out, sems, i, c).wait_send()
        if eighths is not None:
            for r in range(1, N_DEV):
                eighth_copy(out, sems, r, True).wait_send()

    return _Rider(bufs, [jax.ShapeDtypeStruct(s.shape, s.dtype) for s in bufs], {i: i for i in range(len(bufs))},
                  [pltpu.SemaphoreType.DMA((max(n, 1),)), pltpu.SemaphoreType.DMA((max(n, 1),)),
                   pltpu.SemaphoreType.DMA((N_DEV - 1,)), pltpu.SemaphoreType.DMA((N_DEV - 1,))], start, finish)


ATTN_ROWS = GROUP * ATTN_BLOCK
ATTN_KEYS = 2 * ATTN_BLOCK


def _attn_geometry(n):
    row = lax.broadcasted_iota(jnp.int32, (ATTN_ROWS, ATTN_KEYS), 0)
    col = lax.broadcasted_iota(jnp.int32, (ATTN_ROWS, ATTN_KEYS), 1)
    dist = ATTN_BLOCK + jnp.bitwise_and(row, ATTN_BLOCK - 1) - col
    valid = jnp.logical_and(jnp.logical_and(dist >= 0, dist < ATTN_BLOCK),
                            jnp.logical_or(col >= ATTN_BLOCK, n > 0))
    return dist.astype(F32), valid


def _per_head_column(values):
    head = lax.broadcasted_iota(jnp.int32, (ATTN_ROWS, 1), 0) // ATTN_BLOCK
    col = jnp.zeros((ATTN_ROWS, 1), F32)
    for hh, v in enumerate(values):
        col = jnp.where(head == hh, v, col)
    return col


def _stack_heads(ref, g):
    return jnp.concatenate(
        [ref[:, (g * GROUP + hh) * HEAD_DIM:(g * GROUP + hh + 1) * HEAD_DIM].astype(BF16) for hh in range(GROUP)],
        axis=0)


def _attn_probs(q_s, k2, slope_col, sink_col, dist, valid):
    s = lax.dot_general(q_s, k2, (((1,), (1,)), ((), ())), preferred_element_type=F32) * (HEAD_DIM ** -0.5)
    s = jnp.where(valid, s - slope_col * dist, NEG)
    m = jnp.maximum(jnp.max(s, axis=1, keepdims=True), sink_col)
    e = jnp.exp(s - m)
    es = jnp.exp(sink_col - m)
    inv = 1.0 / (jnp.sum(e, axis=1, keepdims=True) + es)
    return e * inv, es * inv


def _attn_specs(T, d_attn, d_kv, q_blk, k_blk, v_blk):
    bq = pl.BlockSpec((ATTN_BLOCK, d_attn), lambda n: (n, q_blk))
    kp = pl.BlockSpec((ATTN_BLOCK, d_kv), lambda n: (jnp.maximum(n - 1, 0), k_blk))
    kc = pl.BlockSpec((ATTN_BLOCK, d_kv), lambda n: (n, k_blk))
    vp = pl.BlockSpec((ATTN_BLOCK, d_kv), lambda n: (jnp.maximum(n - 1, 0), v_blk))
    vc = pl.BlockSpec((ATTN_BLOCK, d_kv), lambda n: (n, v_blk))
    return bq, kp, kc, vp, vc


def _attn_fwd(proj, sinks, nq, cols, rider=None):
    T = proj.shape[0]
    nkv = nq // GROUP
    d_attn, d_kv = nq * HEAD_DIM, nkv * HEAD_DIM
    q_off, k_off, v_off = cols
    bq, kp, kc, vp, vc = _attn_specs(T, d_attn, d_kv, q_off // d_attn, k_off // d_kv, v_off // d_kv)

    def body(sink_ref, q_ref, kp_ref, kc_ref, vp_ref, vc_ref, o_ref):
        n = pl.program_id(0)
        dist, valid = _attn_geometry(n)
        for g in range(nkv):
            ks = slice(g * HEAD_DIM, (g + 1) * HEAD_DIM)
            k2 = jnp.concatenate([kp_ref[:, ks], kc_ref[:, ks]], axis=0).astype(BF16)
            v2 = jnp.concatenate([vp_ref[:, ks], vc_ref[:, ks]], axis=0).astype(BF16)
            slope_col = _per_head_column([2.0 ** (-8.0 * (g * GROUP + hh + 1) / nq) for hh in range(GROUP)])
            sink_col = _per_head_column([sink_ref[0, g * GROUP + hh] for hh in range(GROUP)])
            p, _ = _attn_probs(_stack_heads(q_ref, g), k2, slope_col, sink_col, dist, valid)
            o = jnp.dot(p.astype(BF16), v2, preferred_element_type=F32).astype(BF16)
            for hh in range(GROUP):
                h = g * GROUP + hh
                o_ref[:, h * HEAD_DIM:(h + 1) * HEAD_DIM] = o[hh * ATTN_BLOCK:(hh + 1) * ATTN_BLOCK, :]

    (out,), carried = _call(
        body, name="attn_fwd", out_shape=[jax.ShapeDtypeStruct((T, d_attn), BF16)], grid=(T // ATTN_BLOCK,),
        in_specs=[pl.BlockSpec(memory_space=pltpu.SMEM), bq, kp, kc, vp, vc],
        out_specs=[pl.BlockSpec((ATTN_BLOCK, d_attn), lambda n: (n, 0))], scratch_shapes=[],
        args=(sinks, proj, proj, proj, proj, proj), sem=("parallel",), rider=rider)
    return out, carried


def _attn_bwd(proj, d_attn_out, sinks, nq, cols, after=None):
    T = proj.shape[0]
    nkv = nq // GROUP
    d_attn, d_kv = nq * HEAD_DIM, nkv * HEAD_DIM
    q_off, k_off, v_off = cols
    bq, kp, kc, vp, vc = _attn_specs(T, d_attn, d_kv, q_off // d_attn, k_off // d_kv, v_off // d_kv)
    scale = HEAD_DIM ** -0.5
    dn_t = (((1,), (1,)), ((), ()))
    dn_r = (((0,), (0,)), ((), ()))

    def body(sink_ref, q_ref, kp_ref, kc_ref, vp_ref, vc_ref, do_ref, dq_ref, dk_ref, dv_ref, ds_ref):
        n = pl.program_id(0)

        @pl.when(n == 0)
        def _():
            dk_ref[...] = jnp.zeros_like(dk_ref)
            dv_ref[...] = jnp.zeros_like(dv_ref)
            ds_ref[...] = jnp.zeros_like(ds_ref)

        dist, valid = _attn_geometry(n)
        rows_c = pl.ds(pl.multiple_of(n * ATTN_BLOCK, ATTN_BLOCK), ATTN_BLOCK)
        rows_p = pl.ds(pl.multiple_of(jnp.maximum(n - 1, 0) * ATTN_BLOCK, ATTN_BLOCK), ATTN_BLOCK)
        lane = lax.broadcasted_iota(jnp.int32, ds_ref.shape, 1)
        srow = lax.broadcasted_iota(jnp.int32, ds_ref.shape, 0)
        ds_acc = jnp.zeros(ds_ref.shape, F32)
        for g in range(nkv):
            ks = slice(g * HEAD_DIM, (g + 1) * HEAD_DIM)
            k2 = jnp.concatenate([kp_ref[:, ks], kc_ref[:, ks]], axis=0).astype(BF16)
            v2 = jnp.concatenate([vp_ref[:, ks], vc_ref[:, ks]], axis=0).astype(BF16)
            slope_col = _per_head_column([2.0 ** (-8.0 * (g * GROUP + hh + 1) / nq) for hh in range(GROUP)])
            sink_col = _per_head_column([sink_ref[0, g * GROUP + hh] for hh in range(GROUP)])
            q_s = _stack_heads(q_ref, g)
            do_s = _stack_heads(do_ref, g)
            p, p_sink = _attn_probs(q_s, k2, slope_col, sink_col, dist, valid)
            dp = lax.dot_general(do_s, v2, dn_t, preferred_element_type=F32)
            delta = jnp.sum(p * dp, axis=1, keepdims=True)
            ds = (p * (dp - delta)).astype(BF16)
            sink_part = p_sink * delta
            dq = (jnp.dot(ds, k2, preferred_element_type=F32) * scale).astype(BF16)
            for hh in range(GROUP):
                h = g * GROUP + hh
                blk = slice(hh * ATTN_BLOCK, (hh + 1) * ATTN_BLOCK)
                dq_ref[:, h * HEAD_DIM:(h + 1) * HEAD_DIM] = dq[blk, :]
                ds_acc = ds_acc + jnp.where(jnp.logical_and(lane == h, srow == 0), -jnp.sum(sink_part[blk, :]), 0.0)
            dk2 = lax.dot_general(ds, q_s, dn_r, preferred_element_type=F32) * scale
            dv2 = lax.dot_general(p.astype(BF16), do_s, dn_r, preferred_element_type=F32)
            dk_ref[rows_p, ks] += dk2[:ATTN_BLOCK, :]
            dv_ref[rows_p, ks] += dv2[:ATTN_BLOCK, :]
            dk_ref[rows_c, ks] += dk2[ATTN_BLOCK:, :]
            dv_ref[rows_c, ks] += dv2[ATTN_BLOCK:, :]
        ds_ref[...] += ds_acc

    out_shape = (jax.ShapeDtypeStruct((T, d_attn), BF16), jax.ShapeDtypeStruct((T, d_kv), F32),
                 jax.ShapeDtypeStruct((T, d_kv), F32), jax.ShapeDtypeStruct((8, LANES), F32))
    return _call(
        body, name="attn_bwd", out_shape=out_shape, grid=(T // ATTN_BLOCK,),
        in_specs=[pl.BlockSpec(memory_space=pltpu.SMEM), bq, kp, kc, vp, vc,
                  pl.BlockSpec((ATTN_BLOCK, d_attn), lambda n: (n, 0))],
        out_specs=(pl.BlockSpec((ATTN_BLOCK, d_attn), lambda n: (n, 0)),
                   pl.BlockSpec((T, d_kv), lambda n: (0, 0)), pl.BlockSpec((T, d_kv), lambda n: (0, 0)),
                   pl.BlockSpec((8, LANES), lambda n: (0, 0))),
        scratch_shapes=[], args=(sinks, proj, proj, proj, proj, proj, d_attn_out), sem=("arbitrary",), after=after)[0]


def _rnn_tile(T):
    return _pick(T, (256, 128))


def _rnn_gates(x_ext, cw_ref, cb_ref, wa_ref, wi_ref, ba_ref, bi_ref, lam_ref, tt):
    xs = [pltpu.roll(x_ext, 3 - k, 0)[8:, :] if k < 3 else x_ext[8:, :] for k in range(4)]
    cx = cb_ref[...] + xs[0] * cw_ref[0:1, :]
    for k in range(1, 4):
        cx = cx + xs[k] * cw_ref[k:k + 1, :]
    cxb = cx.astype(BF16)
    r = jax.nn.sigmoid(jnp.dot(cxb, wa_ref[...], preferred_element_type=F32) + ba_ref[...])
    i = jax.nn.sigmoid(jnp.dot(cxb, wi_ref[...], preferred_element_type=F32) + bi_ref[...])
    lam = lam_ref[...]
    sp = jnp.maximum(-lam, 0.0) + jnp.log1p(jnp.exp(-jnp.abs(lam)))
    log_a = -LRU_C * r * sp
    a = jnp.exp(log_a)
    z = 2.0 * log_a
    em1 = jnp.where(z > -1e-2, z * (1.0 + z * (0.5 + z * (1.0 / 6.0 + z * (1.0 / 24.0)))), jnp.exp(z) - 1.0)
    s = jnp.sqrt(-em1)
    return xs, cx, r, i, sp, a, s


def _rnn_specs(T, gw, tt, rx_blk, ry_blk, rev):
    nT = T // tt
    hb = tt // 8

    def tile(t):
        return (nT - 1 - t) if rev else t

    rx = pl.BlockSpec((tt, gw), lambda g, t: (tile(t), rx_blk + g))
    rx_halo = pl.BlockSpec((8, gw), lambda g, t: (jnp.maximum(tile(t) * hb - 1, 0), rx_blk + g))
    ry = pl.BlockSpec((tt, gw), lambda g, t: (tile(t), ry_blk + g))
    cw = pl.BlockSpec((4, gw), lambda g, t: (0, g))
    vec = pl.BlockSpec((1, gw), lambda g, t: (0, g))
    wg = pl.BlockSpec((None, gw, gw), lambda g, t: (g, 0, 0))
    act = pl.BlockSpec((tt, gw), lambda g, t: (tile(t), g))
    act_halo = pl.BlockSpec((8, gw), lambda g, t: (jnp.maximum(tile(t) * hb - 1, 0), g))
    return rx, rx_halo, ry, cw, vec, wg, act, act_halo, tile


def _rnn_fwd(proj, cols, conv_w, conv_b, wa_g, wi_g, ba, bi, lam, rider=None):
    T = proj.shape[0]
    G, gw, _ = wa_g.shape
    d_rnn = G * gw
    tt = _rnn_tile(T)
    rx_off, ry_off = cols
    rx, rx_halo, ry, cw, vec, wg, act, _, _ = _rnn_specs(T, gw, tt, rx_off // gw, ry_off // gw, False)

    def body(rx_ref, rxh_ref, ry_ref, cw_ref, cb_ref, wa_ref, wi_ref, ba_ref, bi_ref, lam_ref,
             b_ref, h_ref, carry):
        t = pl.program_id(1)

        @pl.when(t == 0)
        def _():
            carry[...] = jnp.zeros_like(carry)

        halo = jnp.where(t > 0, rxh_ref[...], 0.0)
        x_ext = jnp.concatenate([halo, rx_ref[...]], axis=0)
        _, cx, _, i, _, a, s = _rnn_gates(x_ext, cw_ref, cb_ref, wa_ref, wi_ref, ba_ref, bi_ref, lam_ref, tt)
        acc_a, acc_b = a, s * (i * cx)
        d = 1
        while d < tt:
            acc_b = acc_a * _shift_down(acc_b, d, 0.0) + acc_b
            acc_a = acc_a * _shift_down(acc_a, d, 1.0)
            d *= 2
        h = acc_b + acc_a * carry[7:8, :]
        carry[...] = h[tt - 8:, :]
        h_ref[...] = h
        b_ref[...] = (h * _gelu(ry_ref[...])).astype(BF16)

    return _call(
        body, name="rnn_fwd",
        out_shape=(jax.ShapeDtypeStruct((T, d_rnn), BF16), jax.ShapeDtypeStruct((T, d_rnn), F32)),
        grid=(G, T // tt),
        in_specs=[rx, rx_halo, ry, cw, vec, wg, wg, vec, vec, vec], out_specs=(act, act),
        scratch_shapes=[pltpu.VMEM((8, gw), F32)],
        args=(proj, proj, proj, conv_w, conv_b, wa_g, wi_g, ba, bi, lam), sem=("parallel", "arbitrary"), rider=rider)


def _rnn_bwd(proj, cols, h_all, d_b, conv_w, conv_b, wa_g, wi_g, ba, bi, lam, rider=None):
    T = proj.shape[0]
    G, gw, _ = wa_g.shape
    d_rnn = G * gw
    tt = _rnn_tile(T)
    nT = T // tt
    rx_off, ry_off = cols
    rx, rx_halo, ry, cw, vec, wg, act, act_halo, _ = _rnn_specs(T, gw, tt, rx_off // gw, ry_off // gw, True)
    dn_t = (((1,), (1,)), ((), ()))
    dn_r = (((0,), (0,)), ((), ()))

    def body(rx_ref, rxh_ref, ry_ref, h_ref, hh_ref, db_ref, cw_ref, cb_ref, wa_ref, wi_ref, ba_ref, bi_ref, lam_ref,
             drx_ref, dry_ref, dcw_ref, dcb_ref, dba_ref, dbi_ref, dlam_ref, dwa_ref, dwi_ref,
             lam_carry, dcx_carry):
        t = pl.program_id(1)
        first_tile = t == nT - 1

        @pl.when(t == 0)
        def _():
            lam_carry[...] = jnp.zeros_like(lam_carry)
            dcx_carry[...] = jnp.zeros_like(dcx_carry)
            dcw_ref[...] = jnp.zeros_like(dcw_ref)
            dcb_ref[...] = jnp.zeros_like(dcb_ref)
            dba_ref[...] = jnp.zeros_like(dba_ref)
            dbi_ref[...] = jnp.zeros_like(dbi_ref)
            dlam_ref[...] = jnp.zeros_like(dlam_ref)
            dwa_ref[...] = jnp.zeros_like(dwa_ref)
            dwi_ref[...] = jnp.zeros_like(dwi_ref)

        halo = jnp.where(first_tile, 0.0, rxh_ref[...])
        x_ext = jnp.concatenate([halo, rx_ref[...]], axis=0)
        xs, cx, r, i, sp, a, s = _rnn_gates(x_ext, cw_ref, cb_ref, wa_ref, wi_ref, ba_ref, bi_ref, lam_ref, tt)
        h = h_ref[...]
        h_halo = jnp.where(first_tile, 0.0, hh_ref[...])
        h_prev = pltpu.roll(jnp.concatenate([h_halo, h], axis=0), 1, 0)[8:, :]
        gel, dgel = _gelu_and_grad(ry_ref[...])
        d_b_t = db_ref[...]
        dry_ref[...] = (d_b_t * h * dgel).astype(BF16)
        dh = d_b_t * gel

        acc_c = _shift_up(a, 1, 1.0)
        acc_l = dh
        d = 1
        while d < tt:
            acc_l = acc_c * _shift_up(acc_l, d, 0.0) + acc_l
            acc_c = acc_c * _shift_up(acc_c, d, 1.0)
            d *= 2
        lam_t = acc_l + acc_c * lam_carry[0:1, :]
        lam_carry[...] = (a * lam_t)[0:8, :]

        icx = i * cx
        d_s = lam_t * icx
        d_i = lam_t * s * cx
        dcx = lam_t * s * i
        d_a = lam_t * h_prev - d_s * (a / s)
        dlog_a = d_a * a
        d_r = dlog_a * (-LRU_C * sp)
        lam = lam_ref[...]
        dlam_ref[...] += jnp.sum(dlog_a * r, axis=0, keepdims=True) * (LRU_C * jax.nn.sigmoid(-lam))
        dpr = d_r * r * (1.0 - r)
        dpi = d_i * i * (1.0 - i)
        dba_ref[...] += jnp.sum(dpr, axis=0, keepdims=True)
        dbi_ref[...] += jnp.sum(dpi, axis=0, keepdims=True)
        cxb = cx.astype(BF16)
        dprb, dpib = dpr.astype(BF16), dpi.astype(BF16)
        dwa_ref[...] += lax.dot_general(cxb, dprb, dn_r, preferred_element_type=F32)
        dwi_ref[...] += lax.dot_general(cxb, dpib, dn_r, preferred_element_type=F32)
        dcx = (dcx + lax.dot_general(dprb, wa_ref[...], dn_t, preferred_element_type=F32)
               + lax.dot_general(dpib, wi_ref[...], dn_t, preferred_element_type=F32))

        dcb_ref[...] += jnp.sum(dcx, axis=0, keepdims=True)
        for k in range(4):
            dcw_ref[k:k + 1, :] += jnp.sum(dcx * xs[k], axis=0, keepdims=True)
        d_ext = jnp.concatenate([dcx, dcx_carry[...]], axis=0)
        drx = dcx * cw_ref[3:4, :]
        for k in range(3):
            drx = drx + pltpu.roll(d_ext, tt + 8 - (3 - k), 0)[:tt, :] * cw_ref[k:k + 1, :]
        drx_ref[...] = drx.astype(BF16)
        dcx_carry[...] = dcx[0:8, :]

    out_shape = (jax.ShapeDtypeStruct((T, d_rnn), BF16), jax.ShapeDtypeStruct((T, d_rnn), BF16),
                 jax.ShapeDtypeStruct((4, d_rnn), F32), jax.ShapeDtypeStruct((1, d_rnn), F32),
                 jax.ShapeDtypeStruct((1, d_rnn), F32), jax.ShapeDtypeStruct((1, d_rnn), F32),
                 jax.ShapeDtypeStruct((1, d_rnn), F32), jax.ShapeDtypeStruct((G, gw, gw), F32),
                 jax.ShapeDtypeStruct((G, gw, gw), F32))
    return _call(
        body, name="rnn_bwd", out_shape=out_shape, grid=(G, nT),
        in_specs=[rx, rx_halo, ry, act, act_halo, act, cw, vec, wg, wg, vec, vec, vec],
        out_specs=(act, act, cw, vec, vec, vec, vec, wg, wg),
        scratch_shapes=[pltpu.VMEM((8, gw), F32), pltpu.VMEM((8, gw), F32)],
        args=(proj, proj, proj, h_all, h_all, d_b, conv_w, conv_b, wa_g, wi_g, ba, bi, lam),
        sem=("parallel", "arbitrary"), rider=rider)


def _merge_fwd(proj, gl_off, b_gate, y_attn, y_rnn, rider=None):
    T, D = y_attn.shape
    tm = _pick(T, (256, 128))
    ct = _pick(math.gcd(gl_off, D), (512, 256, 128))
    oa, orr, nd = gl_off // ct, (gl_off + D) // ct, D // ct

    def body(ga_ref, gr_ref, ba_ref, br_ref, ya_ref, yr_ref, m_ref):
        ga = jax.nn.sigmoid(ga_ref[...] + ba_ref[...])
        gr = jax.nn.sigmoid(gr_ref[...] + br_ref[...])
        m_ref[...] = (ga * ya_ref[...] + gr * yr_ref[...]).astype(BF16)

    blk = pl.BlockSpec((tm, ct), lambda i, j: (i, j))
    (merged,), carried = _call(
        body, name="merge_fwd", out_shape=[jax.ShapeDtypeStruct((T, D), BF16)], grid=(T // tm, nd),
        in_specs=[pl.BlockSpec((tm, ct), lambda i, j: (i, oa + j)), pl.BlockSpec((tm, ct), lambda i, j: (i, orr + j)),
                  pl.BlockSpec((1, ct), lambda i, j: (0, j)), pl.BlockSpec((1, ct), lambda i, j: (0, nd + j)),
                  blk, blk],
        out_specs=[blk], scratch_shapes=[], args=(proj, proj, b_gate, b_gate, y_attn, y_rnn),
        sem=("parallel", "parallel"), rider=rider)
    return merged, carried


def _merge_bwd(proj, gl_off, b_gate, y_attn, y_rnn, d_m):
    T, D = y_attn.shape
    tm = _pick(T, (256, 128))
    ct = _pick(math.gcd(gl_off, D), (512, 256, 128))
    oa, orr, nd = gl_off // ct, (gl_off + D) // ct, D // ct

    def body(ga_ref, gr_ref, ba_ref, br_ref, ya_ref, yr_ref, dm_ref,
             dya_ref, dyr_ref, dga_ref, dgr_ref, dba_ref, dbr_ref):
        i = pl.program_id(1)

        @pl.when(i == 0)
        def _():
            dba_ref[...] = jnp.zeros_like(dba_ref)
            dbr_ref[...] = jnp.zeros_like(dbr_ref)

        ga = jax.nn.sigmoid(ga_ref[...] + ba_ref[...])
        gr = jax.nn.sigmoid(gr_ref[...] + br_ref[...])
        dm = dm_ref[...]
        dya_ref[...] = (dm * ga).astype(BF16)
        dyr_ref[...] = (dm * gr).astype(BF16)
        dga = dm * ya_ref[...] * ga * (1.0 - ga)
        dgr = dm * yr_ref[...] * gr * (1.0 - gr)
        dga_ref[...] = dga.astype(BF16)
        dgr_ref[...] = dgr.astype(BF16)
        dba_ref[...] += jnp.sum(dga, axis=0, keepdims=True)
        dbr_ref[...] += jnp.sum(dgr, axis=0, keepdims=True)

    blk = pl.BlockSpec((tm, ct), lambda j, i: (i, j))
    vec = pl.BlockSpec((1, ct), lambda j, i: (0, j))
    act = jax.ShapeDtypeStruct((T, D), BF16)
    v1 = jax.ShapeDtypeStruct((1, D), F32)
    return pl.pallas_call(
        body, name="merge_bwd", out_shape=(act, act, act, act, v1, v1), grid=(nd, T // tm),
        in_specs=[pl.BlockSpec((tm, ct), lambda j, i: (i, oa + j)), pl.BlockSpec((tm, ct), lambda j, i: (i, orr + j)),
                  vec, pl.BlockSpec((1, ct), lambda j, i: (0, nd + j)), blk, blk, blk],
        out_specs=(blk, blk, blk, blk, vec, vec),
        compiler_params=_cparams(("parallel", "arbitrary")),
    )(proj, proj, b_gate, b_gate, y_attn, y_rnn, d_m)


def _ln_fwd(x_res, delta, g, b, name, rider=None):
    T, D = x_res.shape
    tm = _pick(T, (256, 128))

    def body(x_ref, d_ref, g_ref, b_ref, y_ref, yb_ref, xh_ref, rs_ref):
        z = ALPHA * x_ref[...] + d_ref[...]
        mu = jnp.mean(z, axis=1, keepdims=True)
        zc = z - mu
        var = jnp.mean(zc * zc, axis=1, keepdims=True)
        rstd = lax.rsqrt(var + LN_EPS)
        xh = zc * rstd
        xh_ref[...] = xh
        rs_ref[...] = rstd
        y = xh * g_ref[...] + b_ref[...]
        y_ref[...] = y
        yb_ref[...] = y.astype(BF16)

    row = pl.BlockSpec((tm, D), lambda i: (i, 0))
    vec = pl.BlockSpec((1, D), lambda i: (0, 0))
    return _call(
        body, name=name,
        out_shape=(jax.ShapeDtypeStruct((T, D), F32), jax.ShapeDtypeStruct((T, D), BF16),
                   jax.ShapeDtypeStruct((T, D), F32), jax.ShapeDtypeStruct((T, 1), F32)),
        grid=(T // tm,), in_specs=[row, row, vec, vec],
        out_specs=(row, row, row, pl.BlockSpec((tm, 1), lambda i: (i, 0))),
        scratch_shapes=[], args=(x_res, delta, g, b), sem=("parallel",), rider=rider)


def _ln_bwd_rows(dy, xh, rstd, g):
    dxh = dy * g
    m1 = jnp.mean(dxh, axis=1, keepdims=True)
    m2 = jnp.mean(dxh * xh, axis=1, keepdims=True)
    return rstd * (dxh - m1 - xh * m2)


def _ln_loss_bwd(x_res, delta, g, b, target):
    T, D = x_res.shape
    tm = _pick(T, (256, 128))

    def body(x_ref, d_ref, g_ref, b_ref, t_ref, dz_ref, dzb_ref, loss_ref, dg_ref, db_ref):
        i = pl.program_id(0)

        @pl.when(i == 0)
        def _():
            loss_ref[...] = jnp.zeros_like(loss_ref)
            dg_ref[...] = jnp.zeros_like(dg_ref)
            db_ref[...] = jnp.zeros_like(db_ref)

        z = ALPHA * x_ref[...] + d_ref[...]
        mu = jnp.mean(z, axis=1, keepdims=True)
        zc = z - mu
        var = jnp.mean(zc * zc, axis=1, keepdims=True)
        rstd = lax.rsqrt(var + LN_EPS)
        xh = zc * rstd
        gv = g_ref[...]
        err = xh * gv + b_ref[...] - t_ref[...]
        loss_ref[...] += 0.5 * jnp.sum(jnp.mean(err * err, axis=1, keepdims=True))
        dy = err * (1.0 / D)
        dg_ref[...] += jnp.sum(dy * xh, axis=0, keepdims=True)
        db_ref[...] += jnp.sum(dy, axis=0, keepdims=True)
        dz = _ln_bwd_rows(dy, xh, rstd, gv)
        dz_ref[...] = dz
        dzb_ref[...] = dz.astype(BF16)

    row = pl.BlockSpec((tm, D), lambda i: (i, 0))
    vec = pl.BlockSpec((1, D), lambda i: (0, 0))
    return pl.pallas_call(
        body, name="ln2_loss_bwd",
        out_shape=(jax.ShapeDtypeStruct((T, D), F32), jax.ShapeDtypeStruct((T, D), BF16),
                   jax.ShapeDtypeStruct((8, LANES), F32),
                   jax.ShapeDtypeStruct((1, D), F32), jax.ShapeDtypeStruct((1, D), F32)),
        grid=(T // tm,), in_specs=[row, row, vec, vec, row],
        out_specs=(row, row, pl.BlockSpec((8, LANES), lambda i: (0, 0)), vec, vec),
        compiler_params=_cparams(("arbitrary",)),
    )(x_res, delta, g, b, target)


def _ln_bwd(dy, xh, rstd, g):
    T, D = dy.shape
    tm = _pick(T, (256, 128))

    def body(dy_ref, xh_ref, rs_ref, g_ref, dz_ref, dzb_ref, dg_ref, db_ref):
        i = pl.program_id(0)

        @pl.when(i == 0)
        def _():
            dg_ref[...] = jnp.zeros_like(dg_ref)
            db_ref[...] = jnp.zeros_like(db_ref)

        dyv, xhv = dy_ref[...], xh_ref[...]
        dg_ref[...] += jnp.sum(dyv * xhv, axis=0, keepdims=True)
        db_ref[...] += jnp.sum(dyv, axis=0, keepdims=True)
        dz = _ln_bwd_rows(dyv, xhv, rs_ref[...], g_ref[...])
        dz_ref[...] = dz
        dzb_ref[...] = dz.astype(BF16)

    row = pl.BlockSpec((tm, D), lambda i: (i, 0))
    vec = pl.BlockSpec((1, D), lambda i: (0, 0))
    return pl.pallas_call(
        body, name="ln1_bwd",
        out_shape=(jax.ShapeDtypeStruct((T, D), F32), jax.ShapeDtypeStruct((T, D), BF16),
                   jax.ShapeDtypeStruct((1, D), F32), jax.ShapeDtypeStruct((1, D), F32)),
        grid=(T // tm,), in_specs=[row, row, pl.BlockSpec((tm, 1), lambda i: (i, 0)), vec],
        out_specs=(row, row, vec, vec), compiler_params=_cparams(("arbitrary",)),
    )(dy, xh, rstd, g)


def _ffn_col_tile(T, d_ff):
    return _pick(d_ff, (256, 128)) if T >= 1024 else _pick(d_ff, (512, 256, 128))


def _ffn_gate(gp, cw_ref, cb_ref):
    return (cb_ref[...] + gp * cw_ref[2:3, :] + _shift_down(gp, 1) * cw_ref[1:2, :]
            + _shift_down(gp, 2) * cw_ref[0:1, :])


def _ffn_fwd(up, gpre, conv_w, conv_b, rider=None):
    T, d_ff = up.shape
    ct = _ffn_col_tile(T, d_ff)

    def body(up_ref, gp_ref, cw_ref, cb_ref, f_ref):
        gate = _ffn_gate(gp_ref[...], cw_ref, cb_ref)
        f_ref[...] = (_gelu(gate) * up_ref[...]).astype(BF16)

    col = pl.BlockSpec((T, ct), lambda j: (0, j))
    (f,), carried = _call(
        body, name="ffn_act_fwd", out_shape=[jax.ShapeDtypeStruct((T, d_ff), BF16)], grid=(d_ff // ct,),
        in_specs=[col, col, pl.BlockSpec((3, ct), lambda j: (0, j)), pl.BlockSpec((1, ct), lambda j: (0, j))],
        out_specs=[col], scratch_shapes=[], args=(up, gpre, conv_w, conv_b), sem=("parallel",), rider=rider)
    return f, carried


def _ffn_bwd(up, gpre, conv_w, conv_b, d_f, after=None):
    T, d_ff = up.shape
    ct = _ffn_col_tile(T, d_ff)

    def body(up_ref, gp_ref, cw_ref, cb_ref, df_ref, dup_ref, dgp_ref, dcw_ref, dcb_ref):
        gp = gp_ref[...]
        gate = _ffn_gate(gp, cw_ref, cb_ref)
        gel, dgel = _gelu_and_grad(gate)
        df = df_ref[...]
        dup_ref[...] = (df * gel).astype(BF16)
        dgate = df * up_ref[...] * dgel
        dcb_ref[...] = jnp.sum(dgate, axis=0, keepdims=True)
        dcw_ref[2:3, :] = jnp.sum(dgate * gp, axis=0, keepdims=True)
        dcw_ref[1:2, :] = jnp.sum(dgate * _shift_down(gp, 1), axis=0, keepdims=True)
        dcw_ref[0:1, :] = jnp.sum(dgate * _shift_down(gp, 2), axis=0, keepdims=True)
        dgp = (dgate * cw_ref[2:3, :] + _shift_up(dgate, 1) * cw_ref[1:2, :]
               + _shift_up(dgate, 2) * cw_ref[0:1, :])
        dgp_ref[...] = dgp.astype(BF16)

    col = pl.BlockSpec((T, ct), lambda j: (0, j))
    w3 = pl.BlockSpec((3, ct), lambda j: (0, j))
    v1 = pl.BlockSpec((1, ct), lambda j: (0, j))
    return _call(
        body, name="ffn_act_bwd",
        out_shape=(jax.ShapeDtypeStruct((T, d_ff), BF16), jax.ShapeDtypeStruct((T, d_ff), BF16),
                   jax.ShapeDtypeStruct((3, d_ff), F32), jax.ShapeDtypeStruct((1, d_ff), F32)),
        grid=(d_ff // ct,), in_specs=[col, col, w3, v1, col], out_specs=(col, col, w3, v1),
        scratch_shapes=[], args=(up, gpre, conv_w, conv_b, d_f), sem=("parallel",), after=after)[0]


def _adamw(w, g, m, v, name, after=None):
    R, C = w.shape
    tr = _row_tile(R, C * 4, 8, budget=1280 * 1024)
    c1 = 1.0 / (1.0 - ADAM_B1 ** ADAM_STEP)
    c2 = 1.0 / (1.0 - ADAM_B2 ** ADAM_STEP)

    def body(w_ref, g_ref, m_ref, v_ref, go_ref, d_ref, nm_ref, nv_ref):
        gv = g_ref[...]
        go_ref[...] = gv
        nm = ADAM_B1 * m_ref[...] + (1.0 - ADAM_B1) * gv
        nv = ADAM_B2 * v_ref[...] + (1.0 - ADAM_B2) * (gv * gv)
        nm_ref[...] = nm
        nv_ref[...] = nv
        d_ref[...] = -ADAM_LR * ((nm * c1) / (jnp.sqrt(nv * c2) + ADAM_EPS) + ADAM_WD * w_ref[...])

    blk = pl.BlockSpec((tr, C), lambda r: (r, 0))
    sh = jax.ShapeDtypeStruct((R, C), F32)
    return _call(body, name=name, out_shape=(sh,) * 4, grid=(R // tr,), in_specs=[blk] * 4, out_specs=(blk,) * 4,
                 scratch_shapes=[], args=(w, g, m, v), sem=("parallel",), after=after)[0]


def _group_blocks(w_blocks, per):
    nb, bw, _ = w_blocks.shape
    G = nb // per
    w4 = w_blocks.reshape(G, per, bw, bw)
    rows = []
    for p in range(per):
        parts = [w4[:, p] if q == p else jnp.zeros((G, bw, bw), w_blocks.dtype) for q in range(per)]
        rows.append(jnp.concatenate(parts, axis=2))
    return jnp.concatenate(rows, axis=1)


def _ungroup_blocks(w_groups, per):
    G, gw, _ = w_groups.shape
    bw = gw // per
    blocks = [w_groups[:, p * bw:(p + 1) * bw, p * bw:(p + 1) * bw] for p in range(per)]
    return jnp.stack(blocks, axis=1).reshape(G * per, bw, bw)


def _pack(parts):
    flat = jnp.concatenate([p.reshape(-1).astype(F32) for p in parts])
    n = flat.shape[0]
    rows = -(-n // LANES)
    rows = -(-rows // PACK_ROW_MULT) * PACK_ROW_MULT
    flat = jnp.pad(flat, (0, rows * LANES - n))
    return flat.reshape(rows, LANES)


def _unpack(packed, shapes):
    flat = packed.reshape(-1)
    out, off = [], 0
    for s in shapes:
        n = math.prod(s)
        out.append(flat[off:off + n].reshape(s))
        off += n
    return out


def kernel(x, w_in, b_gate, rnn_conv_w, rnn_conv_b, lru_wa, lru_ba, lru_wi, lru_bi, lru_lambda, attn_sinks, w_attn_proj, w_rnn_proj, w_out, ln1_g, ln1_b, ffn_w_up, ffn_w_gate, ffn_conv_w, ffn_conv_b, ffn_w_down, ln2_g, ln2_b, loss_target, m_w_in, m_b_gate, m_rnn_conv_w, m_rnn_conv_b, m_lru_wa, m_lru_ba, m_lru_wi, m_lru_bi, m_lru_lambda, m_attn_sinks, m_w_attn_proj, m_w_rnn_proj, m_w_out, m_ln1_g, m_ln1_b, m_ffn_w_up, m_ffn_w_gate, m_ffn_conv_w, m_ffn_conv_b, m_ffn_w_down, m_ln2_g, m_ln2_b, v_w_in, v_b_gate, v_rnn_conv_w, v_rnn_conv_b, v_lru_wa, v_lru_ba, v_lru_wi, v_lru_bi, v_lru_lambda, v_attn_sinks, v_w_attn_proj, v_w_rnn_proj, v_w_out, v_ln1_g, v_ln1_b, v_ffn_w_up, v_ffn_w_gate, v_ffn_conv_w, v_ffn_conv_b, v_ffn_w_down, v_ln2_g, v_ln2_b):
    weights = dict(w_in=w_in, b_gate=b_gate, rnn_conv_w=rnn_conv_w, rnn_conv_b=rnn_conv_b, lru_wa=lru_wa,
                   lru_ba=lru_ba, lru_wi=lru_wi, lru_bi=lru_bi, lru_lambda=lru_lambda, attn_sinks=attn_sinks,
                   w_attn_proj=w_attn_proj, w_rnn_proj=w_rnn_proj, w_out=w_out, ln1_g=ln1_g, ln1_b=ln1_b,
                   ffn_w_up=ffn_w_up, ffn_w_gate=ffn_w_gate, ffn_conv_w=ffn_conv_w, ffn_conv_b=ffn_conv_b,
                   ffn_w_down=ffn_w_down, ln2_g=ln2_g, ln2_b=ln2_b)
    m_in = dict(w_in=m_w_in, b_gate=m_b_gate, rnn_conv_w=m_rnn_conv_w, rnn_conv_b=m_rnn_conv_b, lru_wa=m_lru_wa,
                lru_ba=m_lru_ba, lru_wi=m_lru_wi, lru_bi=m_lru_bi, lru_lambda=m_lru_lambda, attn_sinks=m_attn_sinks,
                w_attn_proj=m_w_attn_proj, w_rnn_proj=m_w_rnn_proj, w_out=m_w_out, ln1_g=m_ln1_g, ln1_b=m_ln1_b,
                ffn_w_up=m_ffn_w_up, ffn_w_gate=m_ffn_w_gate, ffn_conv_w=m_ffn_conv_w, ffn_conv_b=m_ffn_conv_b,
                ffn_w_down=m_ffn_w_down, ln2_g=m_ln2_g, ln2_b=m_ln2_b)
    v_in = dict(w_in=v_w_in, b_gate=v_b_gate, rnn_conv_w=v_rnn_conv_w, rnn_conv_b=v_rnn_conv_b, lru_wa=v_lru_wa,
                lru_ba=v_lru_ba, lru_wi=v_lru_wi, lru_bi=v_lru_bi, lru_lambda=v_lru_lambda, attn_sinks=v_attn_sinks,
                w_attn_proj=v_w_attn_proj, w_rnn_proj=v_w_rnn_proj, w_out=v_w_out, ln1_g=v_ln1_g, ln1_b=v_ln1_b,
                ffn_w_up=v_ffn_w_up, ffn_w_gate=v_ffn_w_gate, ffn_conv_w=v_ffn_conv_w, ffn_conv_b=v_ffn_conv_b,
                ffn_w_down=v_ffn_w_down, ln2_g=v_ln2_g, ln2_b=v_ln2_b)
    order = list(weights)

    assert x.shape[0] == 1 and w_in.shape[0] == 1, "one sequence per device, depth 1"
    T, D = x.shape[1], x.shape[2]
    nq = attn_sinks.shape[-1]
    nkv = nq // GROUP
    d_attn, d_kv = nq * HEAD_DIM, nkv * HEAD_DIM
    d_rnn = rnn_conv_b.shape[-1]
    d_ff = ffn_conv_b.shape[-1]
    n_blocks, bw = lru_wa.shape[1], lru_wa.shape[2]
    per = (bw * LANES // math.gcd(bw, LANES)) // bw
    gw = per * bw
    assert n_blocks % per == 0 and d_rnn == n_blocks * bw
    q_off, k_off, v_off = 0, d_attn, d_attn + d_kv
    rx_off = d_attn + 2 * d_kv
    ry_off = rx_off + d_rnn
    gl_off = ry_off + d_rnn
    d_in = gl_off + 2 * D
    assert w_in.shape[-1] * N_SHARDS == d_in
    assert k_off % d_kv == 0 and rx_off % gw == 0 and T % ATTN_BLOCK == 0

    xi, yi, ci = lax.axis_index("x"), lax.axis_index("y"), lax.axis_index("c")
    j_me = 2 * xi + yi
    jc_arr = jnp.stack([j_me, ci]).astype(jnp.int32)

    x0 = x[0]
    x0b = _cast_bf16(x0, "cast_x")
    tgt = loss_target[0]
    big = ["w_in", "w_attn_proj", "w_rnn_proj", "w_out", "ffn_w_up", "ffn_w_gate", "ffn_w_down"]
    own = {n: _cast_bf16_into_slot(weights[n][0], jc_arr, "cast_" + n) for n in big}
    order_arr = jnp.stack([j_me, j_me ^ 2, j_me ^ 1, j_me ^ 3]).astype(jnp.int32)

    rcw_s, fcw_s = _all_gather_small([rnn_conv_w[0], ffn_conv_w[0]])
    rcw = jnp.concatenate([rcw_s[j] for j in range(N_SHARDS)], axis=1)
    fcw = jnp.concatenate([fcw_s[j] for j in range(N_SHARDS)], axis=1)

    wa_g = _group_blocks(lru_wa[0], per).astype(BF16)
    wi_g = _group_blocks(lru_wi[0], per).astype(BF16)

    near, diag = (0, 1), (2,)
    proj, w_in_s = _mm_gathering(x0b, own["w_in"], order_arr, "mm_proj")
    a_out, (w_ap_s, w_rp_s) = _attn_fwd(
        proj, attn_sinks, nq, (q_off, k_off, v_off),
        rider=_gather_rider([own["w_attn_proj"], own["w_rnn_proj"]], _atoms([0]) + _atoms([1], near)))
    (b_out, h_all), (w_rp_s, w_o_s, w_up_s) = _rnn_fwd(
        proj, (rx_off, ry_off), rcw, rnn_conv_b, wa_g, wi_g, lru_ba, lru_bi, lru_lambda,
        rider=_gather_rider([w_rp_s, own["w_out"], own["ffn_w_up"]],
                            _atoms([0], diag) + _atoms([1]) + _atoms([2], near, 0, 2)))
    w_ap, w_rp, w_o = w_ap_s.reshape(d_attn, D), w_rp_s.reshape(d_rnn, D), w_o_s.reshape(D, D)
    y_attn, (w_up_s,) = _mm(a_out, w_ap, name="mm_attn_proj", rider=_gather_rider([w_up_s], _atoms([0], near, 1, 2)))
    y_rnn, (w_up_s,) = _mm(b_out, w_rp, name="mm_rnn_proj", rider=_gather_rider([w_up_s], _atoms([0], diag, 0, 2)))
    merged, (w_gate_s,) = _merge_fwd(proj, gl_off, b_gate, y_attn, y_rnn,
                                     rider=_gather_rider([own["ffn_w_gate"]], _atoms([0], near, 0, 2)))
    mix, (w_up_s,) = _mm(merged, w_o, name="mm_out", rider=_gather_rider([w_up_s], _atoms([0], diag, 1, 2)))
    (x1, x1b, xh1, rstd1), (w_gate_s,) = _ln_fwd(x0, mix, ln1_g, ln1_b, "ln1_fwd",
                                                 rider=_gather_rider([w_gate_s], _atoms([0], near, 1, 2)))
    up, (w_gate_s,) = _mm(x1b, w_up_s, name="mm_up", b_shards=N_SHARDS,
                          rider=_gather_rider([w_gate_s], _atoms([0], diag)))
    gpre, (w_dn_s,) = _mm(x1b, w_gate_s, name="mm_gate", b_shards=N_SHARDS,
                          rider=_gather_rider([own["ffn_w_down"]], _atoms([0], near)))
    f_act, (w_dn_s,) = _ffn_fwd(up, gpre, fcw, ffn_conv_b,
                                rider=_gather_rider([w_dn_s], _atoms([0], diag)))
    w_dn = w_dn_s.reshape(d_ff, D)
    f_out = _mm(f_act, w_dn, name="mm_down")
    dz2, dz2b, loss_acc, dg2, db2 = _ln_loss_bwd(x1, f_out, ln2_g, ln2_b, tgt)

    def pair_sums(arrs, from_sibling, names):
        return [_pair_sum(g, la, jc_arr, "pair_sum_" + n) for g, la, n in zip(arrs, from_sibling, names)]

    def shard_sums(parts, landed, names):
        return [_shard_sum(cp, lb, jc_arr, "shard_sum_" + n) for cp, lb, n in zip(parts, landed, names)]

    halves = {}
    g_down = _mm(f_act, dz2b, name="mm_d_w_down", ta=True, out_dtype=BF16)
    g1 = [g_down.reshape(N_SHARDS, d_ff // N_SHARDS, D)]
    d_f, sib1 = _mm(dz2b, w_dn, name="mm_d_f", tb=True, rider=_pair_rider(g1))
    sent1 = _shard_exchange_start(pair_sums(g1, sib1, ["ffn_w_down"]), "shard_exchange_start_down")
    dup, dgp, d_fcw, d_fcb = _ffn_bwd(up, gpre, fcw, ffn_conv_b, d_f, after=sent1[4])
    g_up = _mm(x1b, dup, name="mm_d_w_up", ta=True, out_dtype=BF16, out_shards=N_SHARDS)
    g_gate = _mm(x1b, dgp, name="mm_d_w_gate", ta=True, out_dtype=BF16, out_shards=N_SHARDS)
    g2 = [g_up, g_gate]
    dx1_a, sib2 = _mm(dup, w_up_s, name="mm_dx1_up", tb=True, b_shards=N_SHARDS, adds=((ALPHA, dz2),),
                      rider=_pair_rider(g2))
    halves["ffn_w_down"], = shard_sums(*_shard_exchange_wait(sent1, dx1_a, "shard_exchange_wait_down"),
                                       ["ffn_w_down"])
    sent2 = _shard_exchange_start(pair_sums(g2, sib2, ["ffn_w_up", "ffn_w_gate"]), "shard_exchange_start_up_gate")
    dx1 = _mm(dgp, w_gate_s, name="mm_dx1_gate", tb=True, b_shards=N_SHARDS, adds=((1.0, dx1_a),), after=sent2[4])
    dz1, dz1b, dg1, db1 = _ln_bwd(dx1, xh1, rstd1, ln1_g)
    g_out = _mm(merged, dz1b, name="mm_d_w_out", ta=True, out_dtype=BF16)
    d_m = _mm(dz1b, w_o, name="mm_d_merged", tb=True)
    dya, dyr, dgl_a, dgl_r, dbg_a, dbg_r = _merge_bwd(proj, gl_off, b_gate, y_attn, y_rnn, d_m)
    g_ap = _mm(a_out, dya, name="mm_d_w_attn_proj", ta=True, out_dtype=BF16)
    g_rp = _mm(b_out, dyr, name="mm_d_w_rnn_proj", ta=True, out_dtype=BF16)
    names3 = ["w_out", "w_attn_proj", "w_rnn_proj"]
    g3 = [g_out.reshape(N_SHARDS, D // N_SHARDS, D), g_ap.reshape(N_SHARDS, d_attn // N_SHARDS, D),
          g_rp.reshape(N_SHARDS, d_rnn // N_SHARDS, D)]
    d_a = _mm(dya, w_ap, name="mm_d_attn", tb=True)
    d_b, sib3 = _mm(dyr, w_rp, name="mm_d_rnn", tb=True, rider=_pair_rider(g3))
    sent3 = _shard_exchange_start(pair_sums(g3, sib3, names3), "shard_exchange_start_mixers")
    dq, dk, dv, dsink = _attn_bwd(proj, d_a, attn_sinks, nq, (q_off, k_off, v_off), after=sent3[4])
    (drx, dry, d_rcw, d_rcb, d_ba, d_bi, d_lam, d_wa_g, d_wi_g), _ = _rnn_bwd(
        proj, (rx_off, ry_off), h_all, d_b, rcw, rnn_conv_b, wa_g, wi_g, lru_ba, lru_bi, lru_lambda)
    halves["ffn_w_up"], halves["ffn_w_gate"] = shard_sums(
        *_shard_exchange_wait(sent2, drx, "shard_exchange_wait_up_gate"), ["ffn_w_up", "ffn_w_gate"])
    d_proj = jnp.concatenate([dq, dk.astype(BF16), dv.astype(BF16), drx, dry, dgl_a, dgl_r], axis=1)
    ffn_names = ["ffn_w_down", "ffn_w_up", "ffn_w_gate"]
    g_in, shared_ffn = _mm(x0b, d_proj, name="mm_d_w_in", ta=True, out_dtype=BF16, out_shards=N_SHARDS,
                           rider=_share_rider([halves[n] for n in ffn_names]))
    halves["w_out"], halves["w_attn_proj"], halves["w_rnn_proj"] = shard_sums(
        *_shard_exchange_wait(sent3, g_in, "shard_exchange_wait_mixers"), names3)

    small_parts = [
        ("loss", loss_acc[0:1, 0:1]),
        ("b_gate", jnp.concatenate([dbg_a, dbg_r], axis=1)),
        ("rnn_conv_w", d_rcw), ("rnn_conv_b", d_rcb),
        ("lru_wa", _ungroup_blocks(d_wa_g, per)), ("lru_ba", d_ba),
        ("lru_wi", _ungroup_blocks(d_wi_g, per)), ("lru_bi", d_bi), ("lru_lambda", d_lam),
        ("attn_sinks", dsink[0:1, 0:nq]),
        ("ln1_g", dg1), ("ln1_b", db1),
        ("ffn_conv_w", d_fcw), ("ffn_conv_b", d_fcb),
        ("ln2_g", dg2), ("ln2_b", db2),
    ]
    packed = _pack([p for _, p in small_parts])
    rs = packed.shape[0]

    def whole(g):
        return g.reshape(2 * g.shape[1], g.shape[2])

    grads = {n: whole(g) for n, g in zip(ffn_names, shared_ffn)}
    out_g, out_d, out_m, out_v = {}, {}, {}, {}

    def adamw(n, after=None):
        shape = weights[n].shape
        two_d = (math.prod(shape[:-1]), shape[-1])
        g2, d2, m2, v2 = _adamw(weights[n].reshape(two_d), grads[n].reshape(two_d), m_in[n].reshape(two_d),
                                v_in[n].reshape(two_d), "adamw_" + n, after=after)
        out_g[n], out_d[n] = g2.reshape(shape), d2.reshape(shape)
        out_m[n], out_v[n] = m2.reshape(shape), v2.reshape(shape)

    g4 = [g_in, packed.reshape(N_SHARDS, rs // N_SHARDS, LANES)]
    sib4 = _run_rider(_pair_rider(g4), "pair_exchange_in_small")
    part4 = pair_sums(g4, sib4, ["w_in", "small"])
    grad_x, (lb_in, lb_small, *shared_mix) = _mm(
        d_proj, w_in_s, name="mm_d_x", tb=True, b_shards=N_SHARDS, adds=((ALPHA, dz1),),
        rider=_join_riders(_shard_exchange_rider(part4, _atoms([0], near) + _atoms([1])),
                           _share_rider([halves[n] for n in names3])))
    grads.update({n: whole(g) for n, g in zip(names3, shared_mix)})
    sent5 = _shard_exchange_start(part4[:1], "shard_exchange_start_in_diag", relations=diag, lands=[lb_in])
    for n in ffn_names + names3:
        adamw(n, after=sent5[4])
    (part_in,), (lb_in,) = _shard_exchange_wait(sent5, out_d[names3[-1]], "shard_exchange_wait_in_diag")
    part_small = part4[1]
    halves["w_in"], = shard_sums([part_in], [lb_in], ["w_in"])
    eighths = _shard_sum(part_small, lb_small, jc_arr, "shard_sum_small", all_slots=True)
    shared_in, reduced = _run_rider(_share_rider([halves["w_in"]], eighths), "share_in_small")
    grads["w_in"] = whole(shared_in)
    reduced = reduced.reshape(rs, LANES)
    small = dict(zip([n for n, _ in small_parts], _unpack(reduced, [p.shape for _, p in small_parts])))
    loss = small.pop("loss").reshape(())
    rcw_n = d_rnn // N_SHARDS
    fcw_n = d_ff // N_SHARDS
    small["rnn_conv_w"] = lax.dynamic_slice(small["rnn_conv_w"], (0, j_me * rcw_n), (4, rcw_n))
    small["ffn_conv_w"] = lax.dynamic_slice(small["ffn_conv_w"], (0, j_me * fcw_n), (3, fcw_n))
    for n, g in small.items():
        grads[n] = g

    for n in order:
        if n not in out_g:
            adamw(n)

    return (loss, grad_x.reshape(x.shape), *[out_g[n] for n in order], *[out_d[n] for n in order],
            *[out_m[n] for n in order], *[out_v[n] for n in order])
```

```python
import functools
import math

import jax
import jax.numpy as jnp
from jax import lax
from jax.experimental import pallas as pl
from jax.experimental.pallas import tpu as pltpu

F32 = jnp.float32
BF16 = jnp.bfloat16
MESH = pl.DeviceIdType.MESH

HEAD_DIM = 64
GROUP = 8
ATTN_BLOCK = 128
LRU_C = 8.0
LN_EPS = 1e-5
ALPHA = 2.0 ** 0.25
LANES = 128
N_SHARDS = 4
N_DEV = 8
VMEM_LIMIT = 56 * 1024 * 1024
MM_VMEM_BUDGET = 40 * 1024 * 1024
MM_MAX_TILE = 3072
PACK_ROW_MULT = 8 * 64
NEG = -1e30

ADAM_LR, ADAM_B1, ADAM_B2, ADAM_EPS, ADAM_WD, ADAM_STEP = 0.001, 0.9, 0.999, 1e-08, 0.01, 10

GELU_C = math.sqrt(2.0 / math.pi)
GELU_A = 0.044715


def _cparams(sem=None):
    kw = dict(vmem_limit_bytes=VMEM_LIMIT)
    if sem is not None:
        kw["dimension_semantics"] = sem
    return pltpu.CompilerParams(**kw)


def _pick(n, prefs):
    for p in prefs:
        if n % p == 0:
            return p
    return n


def _row_tile(rows, row_bytes, mult, budget=2 * 1024 * 1024):
    best = None
    for d in range(mult, rows + 1, mult):
        if rows % d == 0 and d * row_bytes <= budget:
            best = d
    return best if best is not None else rows


def _gelu(x):
    return 0.5 * x * (1.0 + jnp.tanh(GELU_C * (x + GELU_A * x * x * x)))


def _gelu_and_grad(x):
    t = jnp.tanh(GELU_C * (x + GELU_A * x * x * x))
    g = 0.5 * x * (1.0 + t)
    dg = 0.5 * (1.0 + t) + 0.5 * x * (1.0 - t * t) * GELU_C * (1.0 + 3.0 * GELU_A * x * x)
    return g, dg


def _shift_down(x, s, fill=0.0):
    row = lax.broadcasted_iota(jnp.int32, x.shape, 0)
    return jnp.where(row >= s, pltpu.roll(x, s, 0), fill)


def _shift_up(x, s, fill=0.0):
    n = x.shape[0]
    row = lax.broadcasted_iota(jnp.int32, x.shape, 0)
    return jnp.where(row < n - s, pltpu.roll(x, n - s, 0), fill)


def _mm(a, b, *, name, ta=False, tb=False, out_dtype=F32, adds=(), b_shards=1, out_shards=1,
        tm=None, tn=None, tk=None, rider=None, after=None):
    if ta:
        K, M = a.shape
    else:
        M, K = a.shape
    if b_shards > 1:
        n_sh = b.shape[-1]
        if tb:
            N = b.shape[1]
            assert b_shards * n_sh == K
        else:
            N = b_shards * n_sh
            assert b.shape[1] == K
    else:
        n_sh = None
        if tb:
            N = b.shape[0]
            assert b.shape[1] == K
        else:
            N = b.shape[1]
            assert b.shape[0] == K
    wide = (1024, 1536, 1280, 768, 640, 512, 256, 128)
    if tn is None:
        if b_shards > 1 and not tb:
            tn = n_sh if n_sh <= MM_MAX_TILE else _pick(n_sh, wide)
        elif out_shards > 1:
            tn = N // out_shards if N // out_shards <= MM_MAX_TILE else _pick(N // out_shards, wide)
        else:
            tn = _pick(N, wide)
    if tk is None:
        if b_shards > 1 and tb:
            tk = n_sh if n_sh <= MM_MAX_TILE else _pick(n_sh, wide)
        else:
            tk = K if K <= MM_MAX_TILE else _pick(K, (2048,) + wide)
    assert N % tn == 0 and K % tk == 0, (name, M, N, K, tn, tk)
    nk = K // tk
    n_add = len(adds)
    sa, sb, so = a.dtype.itemsize, b.dtype.itemsize, jnp.dtype(out_dtype).itemsize

    def vmem_bytes(tm_):
        return (2 * (tm_ * tk * sa + tk * tn * sb + tm_ * tn * so + n_add * tm_ * tn * 4)
                + (tm_ * tn * 4 if nk > 1 else 0))

    if tm is None:
        tm = _pick(M, (1024, 512, 256, 128)) if nk > 1 else _pick(M, (512, 256, 128))
        while vmem_bytes(tm) > MM_VMEM_BUDGET and tm % 256 == 0:
            tm //= 2
    assert M % tm == 0, (name, M, tm)
    b_outer = b.size * sb >= a.size * sa

    def ij(g0, g1):
        return (g1, g0) if b_outer else (g0, g1)

    def amap(g0, g1, k):
        i, _ = ij(g0, g1)
        return (k, i) if ta else (i, k)

    def bmap(g0, g1, k):
        _, j = ij(g0, g1)
        if b_shards > 1 and not tb:
            per = n_sh // tn
            return (j // per, k, j % per)
        if b_shards > 1 and tb:
            per = n_sh // tk
            return (k // per, j, k % per)
        return (j, k) if tb else (k, j)

    def omap(g0, g1, k):
        i, j = ij(g0, g1)
        if out_shards > 1:
            per_o = (N // out_shards) // tn
            return (j // per_o, i, j % per_o)
        return (i, j)

    a_spec = pl.BlockSpec((tk, tm) if ta else (tm, tk), amap)
    if b_shards > 1:
        b_spec = pl.BlockSpec((None, tn, tk) if tb else (None, tk, tn), bmap)
    else:
        b_spec = pl.BlockSpec((tn, tk) if tb else (tk, tn), bmap)
    add_specs = [pl.BlockSpec((tm, tn), lambda g0, g1, k: ij(g0, g1)) for _ in adds]
    if out_shards > 1:
        out_spec = pl.BlockSpec((None, tm, tn), omap)
        out_shape = jax.ShapeDtypeStruct((out_shards, M, N // out_shards), out_dtype)
    else:
        out_spec = pl.BlockSpec((tm, tn), omap)
        out_shape = jax.ShapeDtypeStruct((M, N), out_dtype)

    if ta:
        dims = (((0,), (0,)), ((), ()))
    elif tb:
        dims = (((1,), (1,)), ((), ()))
    else:
        dims = (((1,), (0,)), ((), ()))
    scales = tuple(s for s, _ in adds)

    def finish(r, add_refs, o_ref):
        for s, ref in zip(scales, add_refs):
            r = r + s * ref[...].astype(F32)
        o_ref[...] = r.astype(out_dtype)

    def body(a_ref, b_ref, *rest):
        add_refs = rest[:n_add]
        o_ref = rest[n_add]
        part = lax.dot_general(a_ref[...].astype(BF16), b_ref[...].astype(BF16), dims, preferred_element_type=F32)
        if nk == 1:
            finish(part, add_refs, o_ref)
            return
        acc = rest[n_add + 1]
        k = pl.program_id(2)

        @pl.when(k == 0)
        def _():
            acc[...] = part

        @pl.when(k > 0)
        def _():
            acc[...] += part

        @pl.when(k == nk - 1)
        def _():
            finish(acc[...], add_refs, o_ref)

    grid = (N // tn, M // tm, nk) if b_outer else (M // tm, N // tn, nk)
    (res,), carried = _call(
        body, name=name, grid=grid, in_specs=[a_spec, b_spec] + add_specs, out_specs=[out_spec],
        out_shape=[out_shape], scratch_shapes=[pltpu.VMEM((tm, tn), F32)] if nk > 1 else [],
        args=(a, b, *[x for _, x in adds]), sem=("parallel", "parallel", "arbitrary"), rider=rider, after=after)
    return (res, carried) if rider is not None else res


def _cast_bf16(w, name):
    R, C = w.shape
    tr = _row_tile(R, C * 4, 16)

    def body(w_ref, o_ref):
        o_ref[...] = w_ref[...].astype(BF16)

    return pl.pallas_call(
        body, name=name, out_shape=jax.ShapeDtypeStruct((R, C), BF16), grid=(R // tr,),
        in_specs=[pl.BlockSpec((tr, C), lambda r: (r, 0))], out_specs=pl.BlockSpec((tr, C), lambda r: (r, 0)),
        compiler_params=_cparams(("parallel",)),
    )(w)


def _cast_bf16_into_slot(w, jc_arr, name):
    R, C = w.shape
    tr = _row_tile(R, C * 4, 16)

    def body(jc_ref, w_ref, o_ref):
        o_ref[...] = w_ref[...].astype(BF16)

    gs = pltpu.PrefetchScalarGridSpec(
        num_scalar_prefetch=1, grid=(R // tr,),
        in_specs=[pl.BlockSpec((tr, C), lambda r, jc: (r, 0))],
        out_specs=pl.BlockSpec((None, tr, C), lambda r, jc: (jc[0], r, 0)))
    return pl.pallas_call(body, name=name, out_shape=jax.ShapeDtypeStruct((N_SHARDS, R, C), BF16), grid_spec=gs,
                          compiler_params=_cparams(("parallel",)))(jc_arr, w)


def _pair_sum(g, la, jc_arr, name):
    S, R, C = g.shape
    half = R // 2
    tr = _row_tile(half, C * 4, 16)
    nrt = half // tr
    dt = g.dtype

    def body(jc_ref, g_ref, la_ref, o_ref):
        o_ref[...] = (g_ref[...].astype(F32) + la_ref[...].astype(F32)).astype(dt)

    gs = pltpu.PrefetchScalarGridSpec(
        num_scalar_prefetch=1, grid=(S, nrt),
        in_specs=[pl.BlockSpec((None, tr, C), lambda s, r, jc: (s, jc[1] * nrt + r, 0)),
                  pl.BlockSpec((None, tr, C), lambda s, r, jc: (s, r, 0))],
        out_specs=pl.BlockSpec((None, tr, C), lambda s, r, jc: (s, r, 0)))
    return pl.pallas_call(body, name=name, out_shape=jax.ShapeDtypeStruct((S, half, C), dt), grid_spec=gs,
                          compiler_params=_cparams(("parallel", "parallel")))(jc_arr, g, la)


def _shard_sum(cp, lb, jc_arr, name, all_slots=False):
    S, h, C = cp.shape
    tr = _row_tile(h, C * 4, 16)

    def body(jc_ref, cp_ref, l0, l1, l2, o_ref):
        o_ref[...] = ((cp_ref[...].astype(F32) + l0[...].astype(F32)) + l1[...].astype(F32)) + l2[...].astype(F32)

    def lspec(kk):
        return pl.BlockSpec((None, tr, C), lambda r, jc: (kk, r, 0))

    if all_slots:
        out_spec = pl.BlockSpec((None, None, tr, C), lambda r, jc: (jc[0], jc[1], r, 0))
        out_shape = jax.ShapeDtypeStruct((S, 2, h, C), F32)
    else:
        out_spec = pl.BlockSpec((None, tr, C), lambda r, jc: (jc[1], r, 0))
        out_shape = jax.ShapeDtypeStruct((2, h, C), F32)
    gs = pltpu.PrefetchScalarGridSpec(
        num_scalar_prefetch=1, grid=(h // tr,),
        in_specs=[pl.BlockSpec((None, tr, C), lambda r, jc: (jc[0], r, 0)), lspec(0), lspec(1), lspec(2)],
        out_specs=out_spec)
    return pl.pallas_call(body, name=name, out_shape=out_shape, grid_spec=gs,
                          compiler_params=_cparams(("parallel",)))(jc_arr, cp, lb, lb, lb)


ANY = pl.BlockSpec(memory_space=pl.ANY)


def _place():
    x, y, c = lax.axis_index("x"), lax.axis_index("y"), lax.axis_index("c")
    chips = [(1 - x, y), (x, 1 - y), (1 - x, 1 - y)]
    return x, y, c, chips


class _Rider:
    def __init__(self, inputs, out_shape, aliases, sems, start, finish):
        self.inputs, self.out_shape, self.aliases, self.sems = list(inputs), list(out_shape), dict(aliases), list(sems)
        self.start, self.finish = start, finish


def _join_riders(r1, r2):
    i1, o1, s1 = len(r1.inputs), len(r1.out_shape), len(r1.sems)
    aliases = dict(r1.aliases)
    aliases.update({i1 + i: o1 + o for i, o in r2.aliases.items()})

    def start(ins, outs, sems):
        r1.start(ins[:i1], outs[:o1], sems[:s1])
        r2.start(ins[i1:], outs[o1:], sems[s1:])

    def finish(ins, outs, sems):
        r1.finish(ins[:i1], outs[:o1], sems[:s1])
        r2.finish(ins[i1:], outs[o1:], sems[s1:])

    return _Rider(r1.inputs + r2.inputs, r1.out_shape + r2.out_shape, aliases, r1.sems + r2.sems, start, finish)


def _after_rider(x):
    return _Rider([x], [], {}, [], lambda *a: None, lambda *a: None)


def _call(body, *, name, grid, in_specs, out_specs, out_shape, scratch_shapes, args, sem, rider=None, after=None):
    out_specs, out_shape = tuple(out_specs), tuple(out_shape)
    if after is not None:
        rider = _after_rider(after) if rider is None else _join_riders(_after_rider(after), rider)
    if rider is None:
        res = pl.pallas_call(body, name=name, out_shape=out_shape, grid=grid, in_specs=list(in_specs),
                             out_specs=out_specs, scratch_shapes=list(scratch_shapes),
                             compiler_params=_cparams(sem))(*args)
        return tuple(res), []
    n_in, n_out, n_sc = len(in_specs), len(out_specs), len(scratch_shapes)
    r_in, r_out = len(rider.inputs), len(rider.out_shape)

    def wrapped(*refs):
        p = 0
        host_in = refs[p:p + n_in]; p += n_in
        rid_in = refs[p:p + r_in]; p += r_in
        host_out = refs[p:p + n_out]; p += n_out
        rid_out = refs[p:p + r_out]; p += r_out
        host_sc = refs[p:p + n_sc]; p += n_sc
        rid_sem = refs[p:]
        first = functools.reduce(jnp.logical_and, [pl.program_id(a) == 0 for a in range(len(grid))])
        last = functools.reduce(jnp.logical_and, [pl.program_id(a) == grid[a] - 1 for a in range(len(grid))])

        @pl.when(first)
        def _():
            rider.start(rid_in, rid_out, rid_sem)

        body(*host_in, *host_out, *host_sc)

        @pl.when(last)
        def _():
            rider.finish(rid_in, rid_out, rid_sem)

    res = pl.pallas_call(
        wrapped, name=name, out_shape=out_shape + tuple(rider.out_shape), grid=grid,
        in_specs=list(in_specs) + [ANY] * r_in, out_specs=out_specs + (ANY,) * r_out,
        input_output_aliases={n_in + i: n_out + o for i, o in rider.aliases.items()},
        scratch_shapes=list(scratch_shapes) + rider.sems,
        compiler_params=_cparams(("arbitrary",) * len(grid)),
    )(*args, *rider.inputs)
    return tuple(res[:n_out]), list(res[n_out:])


def _run_rider(rider, name):
    def body(*refs):
        r_in, r_out = len(rider.inputs), len(rider.out_shape)
        ins, outs, sems = refs[:r_in], refs[r_in:r_in + r_out], refs[r_in + r_out:]
        rider.start(ins, outs, sems)
        rider.finish(ins, outs, sems)

    return pl.pallas_call(
        body, name=name, out_shape=rider.out_shape, in_specs=[ANY] * len(rider.inputs),
        out_specs=[ANY] * len(rider.out_shape), input_output_aliases=rider.aliases, scratch_shapes=rider.sems,
    )(*rider.inputs)


def _atoms(indices, kks=(0, 1, 2), q=0, nq=1):
    return [(i, kk, q, nq) for i in indices for kk in kks]


def _gather_rider(bufs, atoms=None):
    n = len(bufs)
    if atoms is None:
        atoms = _atoms(range(n))
    na = len(atoms)

    def rows(out, atom, core):
        i, _, q, nq = atom
        half = out[i].shape[1] // 2
        assert half % (16 * nq) == 0, (half, nq)
        return pl.ds(core * half + q * (half // nq), half // nq)

    def ici_copy(out, sems, a, slot, peer):
        c = lax.axis_index("c")
        blk = out[atoms[a][0]].at[slot, rows(out, atoms[a], c), :]
        return pltpu.make_async_remote_copy(
            src_ref=blk, dst_ref=blk, send_sem=sems[0].at[a], recv_sem=sems[1].at[a],
            device_id=(peer[0], peer[1], c), device_id_type=MESH)

    def d2d_copy(out, sems, a, slot, from_core):
        x, y, c, _ = _place()
        blk = out[atoms[a][0]].at[slot, rows(out, atoms[a], from_core), :]
        return pltpu.make_async_remote_copy(
            src_ref=blk, dst_ref=blk, send_sem=sems[2].at[a], recv_sem=sems[3].at[a],
            device_id=(x, y, 1 - c), device_id_type=MESH)

    def start(ins, out, sems):
        x, y, c, chips = _place()
        for a in range(na):
            ici_copy(out, sems, a, 2 * x + y, chips[atoms[a][1]]).start()

    def finish(ins, out, sems):
        x, y, c, chips = _place()
        src = [2 * chips[atoms[a][1]][0] + chips[atoms[a][1]][1] for a in range(na)]
        for a in range(na):
            ici_copy(out, sems, a, src[a], chips[atoms[a][1]]).wait_recv()
            d2d_copy(out, sems, a, src[a], c).start()
        for a in range(na):
            d2d_copy(out, sems, a, src[a], 1 - c).wait_recv()
        for a in range(na):
            ici_copy(out, sems, a, 2 * x + y, chips[atoms[a][1]]).wait_send()
            d2d_copy(out, sems, a, src[a], c).wait_send()

    return _Rider(bufs, [jax.ShapeDtypeStruct(s.shape, s.dtype) for s in bufs], {i: i for i in range(n)},
                  [pltpu.SemaphoreType.DMA((na,))] * 4, start, finish)


def _mm_gathering(a, buf, order_arr, name):
    M, K = a.shape
    S, _, n = buf.shape
    tm = _pick(M, (512, 256, 128))
    n_i = M // tm
    half = K // 2

    def body(order_ref, a_ref, w_in_ref, o_ref, w_ref, b_vmem, load_sem, s_ici, r_ici, s_d2d, r_d2d):
        s, i = pl.program_id(0), pl.program_id(1)
        x, y, c, chips = _place()
        j_me = 2 * x + y
        slots = [2 * px + py for px, py in chips]

        def ici(kk, slot):
            blk = w_ref.at[slot, pl.ds(c * half, half), :]
            return pltpu.make_async_remote_copy(
                src_ref=blk, dst_ref=blk, send_sem=s_ici.at[kk], recv_sem=r_ici.at[kk],
                device_id=(chips[kk][0], chips[kk][1], c), device_id_type=MESH)

        def d2d(kk, from_core):
            blk = w_ref.at[slots[kk], pl.ds(from_core * half, half), :]
            return pltpu.make_async_remote_copy(
                src_ref=blk, dst_ref=blk, send_sem=s_d2d.at[kk], recv_sem=r_d2d.at[kk],
                device_id=(x, y, 1 - c), device_id_type=MESH)

        def load(slot, b):
            return pltpu.make_async_copy(w_ref.at[slot], b_vmem.at[b], load_sem.at[b])

        @pl.when(jnp.logical_and(s == 0, i == 0))
        def _():
            for kk in range(3):
                ici(kk, j_me).start()
            load(j_me, 0).start()

        @pl.when(i == 0)
        def _():
            load(order_ref[s], s % 2).wait()

        o_ref[...] = jnp.dot(a_ref[...], b_vmem[s % 2], preferred_element_type=F32)

        last = i == n_i - 1

        @pl.when(jnp.logical_and(last, s == 0))
        def _():
            ici(0, slots[0]).wait_recv()
            d2d(0, c).start()
            ici(1, slots[1]).wait_recv()
            d2d(1, c).start()
            d2d(0, 1 - c).wait_recv()
            load(slots[0], 1).start()

        @pl.when(jnp.logical_and(last, s == 1))
        def _():
            d2d(1, 1 - c).wait_recv()
            load(slots[1], 0).start()

        @pl.when(jnp.logical_and(last, s == 2))
        def _():
            ici(2, slots[2]).wait_recv()
            d2d(2, c).start()
            d2d(2, 1 - c).wait_recv()
            load(slots[2], 1).start()

        @pl.when(jnp.logical_and(last, s == 3))
        def _():
            for kk in range(3):
                ici(kk, j_me).wait_send()
                d2d(kk, c).wait_send()

    gs = pltpu.PrefetchScalarGridSpec(
        num_scalar_prefetch=1, grid=(S, n_i),
        in_specs=[pl.BlockSpec((tm, K), lambda s, i, order: (i, 0)), ANY],
        out_specs=[pl.BlockSpec((tm, n), lambda s, i, order: (i, order[s])), ANY],
        scratch_shapes=[pltpu.VMEM((2, K, n), BF16), pltpu.SemaphoreType.DMA((2,))]
        + [pltpu.SemaphoreType.DMA((3,))] * 4)
    return pl.pallas_call(
        body, name=name, grid_spec=gs,
        out_shape=[jax.ShapeDtypeStruct((M, S * n), F32), jax.ShapeDtypeStruct(buf.shape, buf.dtype)],
        input_output_aliases={2: 1}, compiler_params=_cparams(("arbitrary", "arbitrary")),
    )(order_arr, a, buf)


def _all_gather_small(shards):
    n = len(shards)

    def body(*refs):
        w = refs[:n]
        out = refs[n:2 * n]
        local_sem, s_sem, r_sem = refs[2 * n:]
        x, y, c, chips = _place()
        j_me = 2 * x + y
        cps = []
        for i in range(n):
            lc = pltpu.make_async_copy(w[i], out[i].at[j_me], local_sem.at[i])
            lc.start()
            cps.append(lc)
        sends = []
        for i in range(n):
            for kk, (px, py) in enumerate(chips):
                cp = pltpu.make_async_remote_copy(
                    src_ref=w[i], dst_ref=out[i].at[j_me], send_sem=s_sem.at[3 * i + kk],
                    recv_sem=r_sem.at[3 * i + kk], device_id=(px, py, c), device_id_type=MESH)
                cp.start()
                sends.append(cp)
        for i in range(n):
            for kk, (px, py) in enumerate(chips):
                sends[3 * i + kk].wait_send()
                pltpu.make_async_remote_copy(
                    src_ref=w[i], dst_ref=out[i].at[2 * px + py], send_sem=s_sem.at[3 * i + kk],
                    recv_sem=r_sem.at[3 * i + kk], device_id=(px, py, c), device_id_type=MESH).wait_recv()
        for lc in cps:
            lc.wait()

    out_shape = [jax.ShapeDtypeStruct((N_SHARDS,) + s.shape, s.dtype) for s in shards]
    return pl.pallas_call(
        body, name="all_gather_conv_weights", out_shape=out_shape, in_specs=[ANY] * n, out_specs=[ANY] * n,
        scratch_shapes=[pltpu.SemaphoreType.DMA((n,)), pltpu.SemaphoreType.DMA((3 * n,)),
                        pltpu.SemaphoreType.DMA((3 * n,))],
    )(*shards)


def _pair_rider(grads):
    n = len(grads)

    def copies(g, la, sems):
        x, y, c, _ = _place()
        return [pltpu.make_async_remote_copy(
            src_ref=g[i].at[:, pl.ds((1 - c) * (g[i].shape[1] // 2), g[i].shape[1] // 2), :], dst_ref=la[i],
            send_sem=sems[0].at[i], recv_sem=sems[1].at[i], device_id=(x, y, 1 - c), device_id_type=MESH)
            for i in range(n)]

    def start(g, la, sems):
        for cp in copies(g, la, sems):
            cp.start()

    def finish(g, la, sems):
        for cp in copies(g, la, sems):
            cp.wait()

    return _Rider(grads, [jax.ShapeDtypeStruct((s.shape[0], s.shape[1] // 2, s.shape[2]), s.dtype) for s in grads],
                  {}, [pltpu.SemaphoreType.DMA((n,)), pltpu.SemaphoreType.DMA((n,))], start, finish)


def _shard_exchange_rider(cps_in, atoms=None):
    n = len(cps_in)
    if atoms is None:
        atoms = _atoms(range(n))

    def copies(ins, lb, sems):
        x, y, c, chips = _place()
        out = []
        for a, (i, kk, q, nq) in enumerate(atoms):
            h = ins[i].shape[1]
            assert h % (16 * nq) == 0, (h, nq)
            rows = pl.ds(q * (h // nq), h // nq)
            px, py = chips[kk]
            out.append(pltpu.make_async_remote_copy(
                src_ref=ins[i].at[2 * px + py, rows, :], dst_ref=lb[i].at[kk, rows, :],
                send_sem=sems[0].at[a], recv_sem=sems[1].at[a], device_id=(px, py, c), device_id_type=MESH))
        return out

    def start(ins, lb, sems):
        for cp in copies(ins, lb, sems):
            cp.start()

    def finish(ins, lb, sems):
        for cp in copies(ins, lb, sems):
            cp.wait()

    return _Rider(cps_in, [jax.ShapeDtypeStruct((3,) + s.shape[1:], s.dtype) for s in cps_in], {},
                  [pltpu.SemaphoreType.DMA((len(atoms),)), pltpu.SemaphoreType.DMA((len(atoms),))], start, finish)


HBM = pl.BlockSpec(memory_space=pltpu.HBM)
SEM = pl.BlockSpec(memory_space=pltpu.SEMAPHORE)


def _shard_copies(part_refs, land_refs, send_sems, recv_sems, relations):
    x, y, c, chips = _place()
    nr = len(relations)
    return [pltpu.make_async_remote_copy(
        src_ref=part_refs[i].at[2 * chips[kk][0] + chips[kk][1]], dst_ref=land_refs[i].at[kk],
        send_sem=send_sems.at[nr * i + r], recv_sem=recv_sems.at[nr * i + r],
        device_id=(chips[kk][0], chips[kk][1], c), device_id_type=MESH)
        for i in range(len(part_refs)) for r, kk in enumerate(relations)]


SIDE_EFFECT = pltpu.SideEffectType.DATAFLOW_SIDE_EFFECTING


def _shard_exchange_start(parts, name, relations=(0, 1, 2), lands=None):
    n = len(parts)
    ns = n * len(relations)

    def body(*refs):
        part_refs, land_refs = refs[:n], refs[n:2 * n]
        send_sems, recv_sems = refs[2 * n], refs[2 * n + 1]
        token = refs[4 * n + 2]
        for cp in _shard_copies(part_refs, land_refs, send_sems, recv_sems, relations):
            cp.start()
        token[...] = jnp.zeros_like(token)

    if lands is None:
        lands = [lax.empty((3,) + p.shape[1:], p.dtype) for p in parts]
    bufs = list(parts) + list(lands)
    res = pl.pallas_call(
        body, name=name,
        out_shape=(pltpu.SemaphoreType.DMA((ns,)), pltpu.SemaphoreType.DMA((ns,)),
                   *[pltpu.HBM(b.shape, b.dtype) for b in bufs], jax.ShapeDtypeStruct((8, LANES), F32)),
        in_specs=(HBM,) * (2 * n), out_specs=(SEM, SEM) + (HBM,) * (2 * n) + (pl.BlockSpec(memory_space=pltpu.VMEM),),
        input_output_aliases={i: 2 + i for i in range(2 * n)},
        compiler_params=pltpu.CompilerParams(has_side_effects=SIDE_EFFECT),
    )(*[pltpu.with_memory_space_constraint(b, pltpu.HBM) for b in bufs])
    return res[0], res[1], list(res[2:2 + n]), list(res[2 + n:2 + 2 * n]), res[2 + 2 * n], relations


def _shard_exchange_wait(started, after, name):
    send_sems, recv_sems, parts, lands, _, relations = started
    n = len(parts)

    def body(*refs):
        part_refs, land_refs = refs[:n], refs[n:2 * n]
        send_sems_ref, recv_sems_ref = refs[2 * n], refs[2 * n + 1]
        for cp in _shard_copies(part_refs, land_refs, send_sems_ref, recv_sems_ref, relations):
            cp.wait_send()
            cp.wait_recv()

    bufs = parts + lands
    res = pl.pallas_call(
        body, name=name, out_shape=tuple(pltpu.HBM(b.shape, b.dtype) for b in bufs),
        in_specs=(HBM,) * (2 * n) + (SEM, SEM, ANY), out_specs=(HBM,) * (2 * n),
        input_output_aliases={i: i for i in range(2 * n)},
        compiler_params=pltpu.CompilerParams(has_side_effects=SIDE_EFFECT),
    )(*bufs, send_sems, recv_sems, after)
    return list(res[:n]), list(res[n:])


def _gather_copies(buf_ref, send_sems, recv_sems, over_d2d, arriving):
    x, y, c, chips = _place()
    half = buf_ref.shape[1] // 2
    out = []
    for kk, (px, py) in enumerate(chips):
        if over_d2d:
            slot, core, peer = 2 * px + py, (1 - c) if arriving else c, (x, y, 1 - c)
        else:
            slot, core, peer = (2 * px + py) if arriving else (2 * x + y), c, (px, py, c)
        blk = buf_ref.at[slot, pl.ds(core * half, half), :]
        out.append(pltpu.make_async_remote_copy(src_ref=blk, dst_ref=blk, send_sem=send_sems.at[kk],
                                                recv_sem=recv_sems.at[kk], device_id=peer, device_id_type=MESH))
    return out


def _gather_step(buf, after, name, sems_in=None, start_d2d=None):
    n_sem = 0 if sems_in is None else 2

    def body(*refs):
        buf_ref = refs[0]
        ins = refs[1:1 + n_sem]
        outs = refs[2 + n_sem:]
        if sems_in is not None:
            waited_d2d = start_d2d is None
            for mine, theirs in zip(_gather_copies(buf_ref, ins[0], ins[1], waited_d2d, False),
                                    _gather_copies(buf_ref, ins[0], ins[1], waited_d2d, True)):
                theirs.wait_recv()
                mine.wait_send()
        if start_d2d is not None:
            for cp in _gather_copies(buf_ref, outs[0], outs[1], start_d2d, False):
                cp.start()
            outs[3][...] = jnp.zeros_like(outs[3])

    sem_out = () if start_d2d is None else (pltpu.SemaphoreType.DMA((3,)), pltpu.SemaphoreType.DMA((3,)))
    tok_out = () if start_d2d is None else (jax.ShapeDtypeStruct((8, LANES), F32),)
    res = pl.pallas_call(
        body, name=name,
        out_shape=sem_out + (pltpu.HBM(buf.shape, buf.dtype),) + tok_out,
        in_specs=(HBM,) + (SEM,) * n_sem + (ANY,),
        out_specs=(SEM,) * len(sem_out) + (HBM,) + (pl.BlockSpec(memory_space=pltpu.VMEM),) * len(tok_out),
        input_output_aliases={0: len(sem_out)},
        compiler_params=pltpu.CompilerParams(has_side_effects=SIDE_EFFECT),
    )(pltpu.with_memory_space_constraint(buf, pltpu.HBM), *(sems_in or ()), after)
    if start_d2d is None:
        return res[0]
    return (res[0], res[1]), res[2], res[3]


def _share_rider(halves, eighths=None):
    n = len(halves)
    bufs = list(halves) + ([eighths] if eighths is not None else [])

    def half_copy(out, sems, i, core):
        x, y, c, _ = _place()
        blk = out[i].at[core]
        return pltpu.make_async_remote_copy(src_ref=blk, dst_ref=blk, send_sem=sems[0].at[i], recv_sem=sems[1].at[i],
                                            device_id=(x, y, 1 - c), device_id_type=MESH)

    def eighth_copy(out, sems, r, mine):
        x, y, c, _ = _place()
        px, py, pc = x ^ ((r >> 2) & 1), y ^ ((r >> 1) & 1), c ^ (r & 1)
        blk = out[n].at[2 * x + y, c] if mine else out[n].at[2 * px + py, pc]
        return pltpu.make_async_remote_copy(src_ref=blk, dst_ref=blk, send_sem=sems[2].at[r - 1],
                                            recv_sem=sems[3].at[r - 1], device_id=(px, py, pc), device_id_type=MESH)

    def start(ins, out, sems):
        c = lax.axis_index("c")
        for i in range(n):
            half_copy(out, sems, i, c).start()
        if eighths is not None:
            for r in range(1, N_DEV):
                eighth_copy(out, sems, r, True).start()

    def finish(ins, out, sems):
        c = lax.axis_index("c")
        for i in range(n):
            half_copy(out, sems, i, 1 - c).wait_recv()
        if eighths is not None:
            for r in range(1, N_DEV):
                eighth_copy(out, sems, r, False).wait_recv()
        for i in range(n):
            half_copy(out, sems, i, c).wait_send()
        if eighths is not None:
            for r in range(1, N_DEV):
                eighth_copy(out, sems, r, True).wait_send()

    return _Rider(bufs, [jax.ShapeDtypeStruct(s.shape, s.dtype) for s in bufs], {i: i for i in range(len(bufs))},
                  [pltpu.SemaphoreType.DMA((max(n, 1),)), pltpu.SemaphoreType.DMA((max(n, 1),)),
                   pltpu.SemaphoreType.DMA((N_DEV - 1,)), pltpu.SemaphoreType.DMA((N_DEV - 1,))], start, finish)


ATTN_ROWS = GROUP * ATTN_BLOCK
ATTN_KEYS = 2 * ATTN_BLOCK


def _attn_geometry(n):
    row = lax.broadcasted_iota(jnp.int32, (ATTN_ROWS, ATTN_KEYS), 0)
    col = lax.broadcasted_iota(jnp.int32, (ATTN_ROWS, ATTN_KEYS), 1)
    dist = ATTN_BLOCK + jnp.bitwise_and(row, ATTN_BLOCK - 1) - col
    valid = jnp.logical_and(jnp.logical_and(dist >= 0, dist < ATTN_BLOCK),
                            jnp.logical_or(col >= ATTN_BLOCK, n > 0))
    return dist.astype(F32), valid


def _per_head_column(values):
    head = lax.broadcasted_iota(jnp.int32, (ATTN_ROWS, 1), 0) // ATTN_BLOCK
    col = jnp.zeros((ATTN_ROWS, 1), F32)
    for hh, v in enumerate(values):
        col = jnp.where(head == hh, v, col)
    return col


def _stack_heads(ref, g):
    return jnp.concatenate(
        [ref[:, (g * GROUP + hh) * HEAD_DIM:(g * GROUP + hh + 1) * HEAD_DIM].astype(BF16) for hh in range(GROUP)],
        axis=0)


def _attn_probs(q_s, k2, slope_col, sink_col, dist, valid):
    s = lax.dot_general(q_s, k2, (((1,), (1,)), ((), ())), preferred_element_type=F32) * (HEAD_DIM ** -0.5)
    s = jnp.where(valid, s - slope_col * dist, NEG)
    m = jnp.maximum(jnp.max(s, axis=1, keepdims=True), sink_col)
    e = jnp.exp(s - m)
    es = jnp.exp(sink_col - m)
    inv = 1.0 / (jnp.sum(e, axis=1, keepdims=True) + es)
    return e * inv, es * inv


def _attn_specs(T, d_attn, d_kv, q_blk, k_blk, v_blk):
    bq = pl.BlockSpec((ATTN_BLOCK, d_attn), lambda n: (n, q_blk))
    kp = pl.BlockSpec((ATTN_BLOCK, d_kv), lambda n: (jnp.maximum(n - 1, 0), k_blk))
    kc = pl.BlockSpec((ATTN_BLOCK, d_kv), lambda n: (n, k_blk))
    vp = pl.BlockSpec((ATTN_BLOCK, d_kv), lambda n: (jnp.maximum(n - 1, 0), v_blk))
    vc = pl.BlockSpec((ATTN_BLOCK, d_kv), lambda n: (n, v_blk))
    return bq, kp, kc, vp, vc


def _attn_fwd(proj, sinks, nq, cols, after=None):
    T = proj.shape[0]
    nkv = nq // GROUP
    d_attn, d_kv = nq * HEAD_DIM, nkv * HEAD_DIM
    q_off, k_off, v_off = cols
    bq, kp, kc, vp, vc = _attn_specs(T, d_attn, d_kv, q_off // d_attn, k_off // d_kv, v_off // d_kv)

    def body(sink_ref, q_ref, kp_ref, kc_ref, vp_ref, vc_ref, o_ref):
        n = pl.program_id(0)
        dist, valid = _attn_geometry(n)
        for g in range(nkv):
            ks = slice(g * HEAD_DIM, (g + 1) * HEAD_DIM)
            k2 = jnp.concatenate([kp_ref[:, ks], kc_ref[:, ks]], axis=0).astype(BF16)
            v2 = jnp.concatenate([vp_ref[:, ks], vc_ref[:, ks]], axis=0).astype(BF16)
            slope_col = _per_head_column([2.0 ** (-8.0 * (g * GROUP + hh + 1) / nq) for hh in range(GROUP)])
            sink_col = _per_head_column([sink_ref[0, g * GROUP + hh] for hh in range(GROUP)])
            p, _ = _attn_probs(_stack_heads(q_ref, g), k2, slope_col, sink_col, dist, valid)
            o = jnp.dot(p.astype(BF16), v2, preferred_element_type=F32).astype(BF16)
            for hh in range(GROUP):
                h = g * GROUP + hh
                o_ref[:, h * HEAD_DIM:(h + 1) * HEAD_DIM] = o[hh * ATTN_BLOCK:(hh + 1) * ATTN_BLOCK, :]

    (out,), carried = _call(
        body, name="attn_fwd", out_shape=[jax.ShapeDtypeStruct((T, d_attn), BF16)], grid=(T // ATTN_BLOCK,),
        in_specs=[pl.BlockSpec(memory_space=pltpu.SMEM), bq, kp, kc, vp, vc],
        out_specs=[pl.BlockSpec((ATTN_BLOCK, d_attn), lambda n: (n, 0))], scratch_shapes=[],
        args=(sinks, proj, proj, proj, proj, proj), sem=("parallel",), after=after)
    return out


def _attn_bwd(proj, d_attn_out, sinks, nq, cols, after=None):
    T = proj.shape[0]
    nkv = nq // GROUP
    d_attn, d_kv = nq * HEAD_DIM, nkv * HEAD_DIM
    q_off, k_off, v_off = cols
    bq, kp, kc, vp, vc = _attn_specs(T, d_attn, d_kv, q_off // d_attn, k_off // d_kv, v_off // d_kv)
    scale = HEAD_DIM ** -0.5
    dn_t = (((1,), (1,)), ((), ()))
    dn_r = (((0,), (0,)), ((), ()))

    def body(sink_ref, q_ref, kp_ref, kc_ref, vp_ref, vc_ref, do_ref, dq_ref, dk_ref, dv_ref, ds_ref):
        n = pl.program_id(0)

        @pl.when(n == 0)
        def _():
            dk_ref[...] = jnp.zeros_like(dk_ref)
            dv_ref[...] = jnp.zeros_like(dv_ref)
            ds_ref[...] = jnp.zeros_like(ds_ref)

        dist, valid = _attn_geometry(n)
        rows_c = pl.ds(pl.multiple_of(n * ATTN_BLOCK, ATTN_BLOCK), ATTN_BLOCK)
        rows_p = pl.ds(pl.multiple_of(jnp.maximum(n - 1, 0) * ATTN_BLOCK, ATTN_BLOCK), ATTN_BLOCK)
        lane = lax.broadcasted_iota(jnp.int32, ds_ref.shape, 1)
        srow = lax.broadcasted_iota(jnp.int32, ds_ref.shape, 0)
        ds_acc = jnp.zeros(ds_ref.shape, F32)
        for g in range(nkv):
            ks = slice(g * HEAD_DIM, (g + 1) * HEAD_DIM)
            k2 = jnp.concatenate([kp_ref[:, ks], kc_ref[:, ks]], axis=0).astype(BF16)
            v2 = jnp.concatenate([vp_ref[:, ks], vc_ref[:, ks]], axis=0).astype(BF16)
            slope_col = _per_head_column([2.0 ** (-8.0 * (g * GROUP + hh + 1) / nq) for hh in range(GROUP)])
            sink_col = _per_head_column([sink_ref[0, g * GROUP + hh] for hh in range(GROUP)])
            q_s = _stack_heads(q_ref, g)
            do_s = _stack_heads(do_ref, g)
            p, p_sink = _attn_probs(q_s, k2, slope_col, sink_col, dist, valid)
            dp = lax.dot_general(do_s, v2, dn_t, preferred_element_type=F32)
            delta = jnp.sum(p * dp, axis=1, keepdims=True)
            ds = (p * (dp - delta)).astype(BF16)
            sink_part = p_sink * delta
            dq = (jnp.dot(ds, k2, preferred_element_type=F32) * scale).astype(BF16)
            for hh in range(GROUP):
                h = g * GROUP + hh
                blk = slice(hh * ATTN_BLOCK, (hh + 1) * ATTN_BLOCK)
                dq_ref[:, h * HEAD_DIM:(h + 1) * HEAD_DIM] = dq[blk, :]
                ds_acc = ds_acc + jnp.where(jnp.logical_and(lane == h, srow == 0), -jnp.sum(sink_part[blk, :]), 0.0)
            dk2 = lax.dot_general(ds, q_s, dn_r, preferred_element_type=F32) * scale
            dv2 = lax.dot_general(p.astype(BF16), do_s, dn_r, preferred_element_type=F32)
            dk_ref[rows_p, ks] += dk2[:ATTN_BLOCK, :]
            dv_ref[rows_p, ks] += dv2[:ATTN_BLOCK, :]
            dk_ref[rows_c, ks] += dk2[ATTN_BLOCK:, :]
            dv_ref[rows_c, ks] += dv2[ATTN_BLOCK:, :]
        ds_ref[...] += ds_acc

    out_shape = (jax.ShapeDtypeStruct((T, d_attn), BF16), jax.ShapeDtypeStruct((T, d_kv), F32),
                 jax.ShapeDtypeStruct((T, d_kv), F32), jax.ShapeDtypeStruct((8, LANES), F32))
    return _call(
        body, name="attn_bwd", out_shape=out_shape, grid=(T // ATTN_BLOCK,),
        in_specs=[pl.BlockSpec(memory_space=pltpu.SMEM), bq, kp, kc, vp, vc,
                  pl.BlockSpec((ATTN_BLOCK, d_attn), lambda n: (n, 0))],
        out_specs=(pl.BlockSpec((ATTN_BLOCK, d_attn), lambda n: (n, 0)),
                   pl.BlockSpec((T, d_kv), lambda n: (0, 0)), pl.BlockSpec((T, d_kv), lambda n: (0, 0)),
                   pl.BlockSpec((8, LANES), lambda n: (0, 0))),
        scratch_shapes=[], args=(sinks, proj, proj, proj, proj, proj, d_attn_out), sem=("arbitrary",), after=after)[0]


def _rnn_tile(T):
    return _pick(T, (256, 128))


def _rnn_gates(x_ext, cw_ref, cb_ref, wa_ref, wi_ref, ba_ref, bi_ref, lam_ref, tt):
    xs = [pltpu.roll(x_ext, 3 - k, 0)[8:, :] if k < 3 else x_ext[8:, :] for k in range(4)]
    cx = cb_ref[...] + xs[0] * cw_ref[0:1, :]
    for k in range(1, 4):
        cx = cx + xs[k] * cw_ref[k:k + 1, :]
    cxb = cx.astype(BF16)
    r = jax.nn.sigmoid(jnp.dot(cxb, wa_ref[...], preferred_element_type=F32) + ba_ref[...])
    i = jax.nn.sigmoid(jnp.dot(cxb, wi_ref[...], preferred_element_type=F32) + bi_ref[...])
    lam = lam_ref[...]
    sp = jnp.maximum(-lam, 0.0) + jnp.log1p(jnp.exp(-jnp.abs(lam)))
    log_a = -LRU_C * r * sp
    a = jnp.exp(log_a)
    z = 2.0 * log_a
    em1 = jnp.where(z > -1e-2, z * (1.0 + z * (0.5 + z * (1.0 / 6.0 + z * (1.0 / 24.0)))), jnp.exp(z) - 1.0)
    s = jnp.sqrt(-em1)
    return xs, cx, r, i, sp, a, s


def _rnn_specs(T, gw, tt, rx_blk, ry_blk, rev):
    nT = T // tt
    hb = tt // 8

    def tile(t):
        return (nT - 1 - t) if rev else t

    rx = pl.BlockSpec((tt, gw), lambda g, t: (tile(t), rx_blk + g))
    rx_halo = pl.BlockSpec((8, gw), lambda g, t: (jnp.maximum(tile(t) * hb - 1, 0), rx_blk + g))
    ry = pl.BlockSpec((tt, gw), lambda g, t: (tile(t), ry_blk + g))
    cw = pl.BlockSpec((4, gw), lambda g, t: (0, g))
    vec = pl.BlockSpec((1, gw), lambda g, t: (0, g))
    wg = pl.BlockSpec((None, gw, gw), lambda g, t: (g, 0, 0))
    act = pl.BlockSpec((tt, gw), lambda g, t: (tile(t), g))
    act_halo = pl.BlockSpec((8, gw), lambda g, t: (jnp.maximum(tile(t) * hb - 1, 0), g))
    return rx, rx_halo, ry, cw, vec, wg, act, act_halo, tile


def _rnn_fwd(proj, cols, conv_w, conv_b, wa_g, wi_g, ba, bi, lam, rider=None):
    T = proj.shape[0]
    G, gw, _ = wa_g.shape
    d_rnn = G * gw
    tt = _rnn_tile(T)
    rx_off, ry_off = cols
    rx, rx_halo, ry, cw, vec, wg, act, _, _ = _rnn_specs(T, gw, tt, rx_off // gw, ry_off // gw, False)

    def body(rx_ref, rxh_ref, ry_ref, cw_ref, cb_ref, wa_ref, wi_ref, ba_ref, bi_ref, lam_ref,
             b_ref, h_ref, carry):
        t = pl.program_id(1)

        @pl.when(t == 0)
        def _():
            carry[...] = jnp.zeros_like(carry)

        halo = jnp.where(t > 0, rxh_ref[...], 0.0)
        x_ext = jnp.concatenate([halo, rx_ref[...]], axis=0)
        _, cx, _, i, _, a, s = _rnn_gates(x_ext, cw_ref, cb_ref, wa_ref, wi_ref, ba_ref, bi_ref, lam_ref, tt)
        acc_a, acc_b = a, s * (i * cx)
        d = 1
        while d < tt:
            acc_b = acc_a * _shift_down(acc_b, d, 0.0) + acc_b
            acc_a = acc_a * _shift_down(acc_a, d, 1.0)
            d *= 2
        h = acc_b + acc_a * carry[7:8, :]
        carry[...] = h[tt - 8:, :]
        h_ref[...] = h
        b_ref[...] = (h * _gelu(ry_ref[...])).astype(BF16)

    return _call(
        body, name="rnn_fwd",
        out_shape=(jax.ShapeDtypeStruct((T, d_rnn), BF16), jax.ShapeDtypeStruct((T, d_rnn), F32)),
        grid=(G, T // tt),
        in_specs=[rx, rx_halo, ry, cw, vec, wg, wg, vec, vec, vec], out_specs=(act, act),
        scratch_shapes=[pltpu.VMEM((8, gw), F32)],
        args=(proj, proj, proj, conv_w, conv_b, wa_g, wi_g, ba, bi, lam), sem=("parallel", "arbitrary"), rider=rider)


def _rnn_bwd(proj, cols, h_all, d_b, conv_w, conv_b, wa_g, wi_g, ba, bi, lam, rider=None):
    T = proj.shape[0]
    G, gw, _ = wa_g.shape
    d_rnn = G * gw
    tt = _rnn_tile(T)
    nT = T // tt
    rx_off, ry_off = cols
    rx, rx_halo, ry, cw, vec, wg, act, act_halo, _ = _rnn_specs(T, gw, tt, rx_off // gw, ry_off // gw, True)
    dn_t = (((1,), (1,)), ((), ()))
    dn_r = (((0,), (0,)), ((), ()))

    def body(rx_ref, rxh_ref, ry_ref, h_ref, hh_ref, db_ref, cw_ref, cb_ref, wa_ref, wi_ref, ba_ref, bi_ref, lam_ref,
             drx_ref, dry_ref, dcw_ref, dcb_ref, dba_ref, dbi_ref, dlam_ref, dwa_ref, dwi_ref,
             lam_carry, dcx_carry):
        t = pl.program_id(1)
        first_tile = t == nT - 1

        @pl.when(t == 0)
        def _():
            lam_carry[...] = jnp.zeros_like(lam_carry)
            dcx_carry[...] = jnp.zeros_like(dcx_carry)
            dcw_ref[...] = jnp.zeros_like(dcw_ref)
            dcb_ref[...] = jnp.zeros_like(dcb_ref)
            dba_ref[...] = jnp.zeros_like(dba_ref)
            dbi_ref[...] = jnp.zeros_like(dbi_ref)
            dlam_ref[...] = jnp.zeros_like(dlam_ref)
            dwa_ref[...] = jnp.zeros_like(dwa_ref)
            dwi_ref[...] = jnp.zeros_like(dwi_ref)

        halo = jnp.where(first_tile, 0.0, rxh_ref[...])
        x_ext = jnp.concatenate([halo, rx_ref[...]], axis=0)
        xs, cx, r, i, sp, a, s = _rnn_gates(x_ext, cw_ref, cb_ref, wa_ref, wi_ref, ba_ref, bi_ref, lam_ref, tt)
        h = h_ref[...]
        h_halo = jnp.where(first_tile, 0.0, hh_ref[...])
        h_prev = pltpu.roll(jnp.concatenate([h_halo, h], axis=0), 1, 0)[8:, :]
        gel, dgel = _gelu_and_grad(ry_ref[...])
        d_b_t = db_ref[...]
        dry_ref[...] = (d_b_t * h * dgel).astype(BF16)
        dh = d_b_t * gel

        acc_c = _shift_up(a, 1, 1.0)
        acc_l = dh
        d = 1
        while d < tt:
            acc_l = acc_c * _shift_up(acc_l, d, 0.0) + acc_l
            acc_c = acc_c * _shift_up(acc_c, d, 1.0)
            d *= 2
        lam_t = acc_l + acc_c * lam_carry[0:1, :]
        lam_carry[...] = (a * lam_t)[0:8, :]

        icx = i * cx
        d_s = lam_t * icx
        d_i = lam_t * s * cx
        dcx = lam_t * s * i
        d_a = lam_t * h_prev - d_s * (a / s)
        dlog_a = d_a * a
        d_r = dlog_a * (-LRU_C * sp)
        lam = lam_ref[...]
        dlam_ref[...] += jnp.sum(dlog_a * r, axis=0, keepdims=True) * (LRU_C * jax.nn.sigmoid(-lam))
        dpr = d_r * r * (1.0 - r)
        dpi = d_i * i * (1.0 - i)
        dba_ref[...] += jnp.sum(dpr, axis=0, keepdims=True)
        dbi_ref[...] += jnp.sum(dpi, axis=0, keepdims=True)
        cxb = cx.astype(BF16)
        dprb, dpib = dpr.astype(BF16), dpi.astype(BF16)
        dwa_ref[...] += lax.dot_general(cxb, dprb, dn_r, preferred_element_type=F32)
        dwi_ref[...] += lax.dot_general(cxb, dpib, dn_r, preferred_element_type=F32)
        dcx = (dcx + lax.dot_general(dprb, wa_ref[...], dn_t, preferred_element_type=F32)
               + lax.dot_general(dpib, wi_ref[...], dn_t, preferred_element_type=F32))

        dcb_ref[...] += jnp.sum(dcx, axis=0, keepdims=True)
        for k in range(4):
            dcw_ref[k:k + 1, :] += jnp.sum(dcx * xs[k], axis=0, keepdims=True)
        d_ext = jnp.concatenate([dcx, dcx_carry[...]], axis=0)
        drx = dcx * cw_ref[3:4, :]
        for k in range(3):
            drx = drx + pltpu.roll(d_ext, tt + 8 - (3 - k), 0)[:tt, :] * cw_ref[k:k + 1, :]
        drx_ref[...] = drx.astype(BF16)
        dcx_carry[...] = dcx[0:8, :]

    out_shape = (jax.ShapeDtypeStruct((T, d_rnn), BF16), jax.ShapeDtypeStruct((T, d_rnn), BF16),
                 jax.ShapeDtypeStruct((4, d_rnn), F32), jax.ShapeDtypeStruct((1, d_rnn), F32),
                 jax.ShapeDtypeStruct((1, d_rnn), F32), jax.ShapeDtypeStruct((1, d_rnn), F32),
                 jax.ShapeDtypeStruct((1, d_rnn), F32), jax.ShapeDtypeStruct((G, gw, gw), F32),
                 jax.ShapeDtypeStruct((G, gw, gw), F32))
    return _call(
        body, name="rnn_bwd", out_shape=out_shape, grid=(G, nT),
        in_specs=[rx, rx_halo, ry, act, act_halo, act, cw, vec, wg, wg, vec, vec, vec],
        out_specs=(act, act, cw, vec, vec, vec, vec, wg, wg),
        scratch_shapes=[pltpu.VMEM((8, gw), F32), pltpu.VMEM((8, gw), F32)],
        args=(proj, proj, proj, h_all, h_all, d_b, conv_w, conv_b, wa_g, wi_g, ba, bi, lam),
        sem=("parallel", "arbitrary"), rider=rider)


def _merge_fwd(proj, gl_off, b_gate, y_attn, y_rnn, rider=None):
    T, D = y_attn.shape
    tm = _pick(T, (256, 128))
    ct = _pick(math.gcd(gl_off, D), (512, 256, 128))
    oa, orr, nd = gl_off // ct, (gl_off + D) // ct, D // ct

    def body(ga_ref, gr_ref, ba_ref, br_ref, ya_ref, yr_ref, m_ref):
        ga = jax.nn.sigmoid(ga_ref[...] + ba_ref[...])
        gr = jax.nn.sigmoid(gr_ref[...] + br_ref[...])
        m_ref[...] = (ga * ya_ref[...] + gr * yr_ref[...]).astype(BF16)

    blk = pl.BlockSpec((tm, ct), lambda i, j: (i, j))
    (merged,), carried = _call(
        body, name="merge_fwd", out_shape=[jax.ShapeDtypeStruct((T, D), BF16)], grid=(T // tm, nd),
        in_specs=[pl.BlockSpec((tm, ct), lambda i, j: (i, oa + j)), pl.BlockSpec((tm, ct), lambda i, j: (i, orr + j)),
                  pl.BlockSpec((1, ct), lambda i, j: (0, j)), pl.BlockSpec((1, ct), lambda i, j: (0, nd + j)),
                  blk, blk],
        out_specs=[blk], scratch_shapes=[], args=(proj, proj, b_gate, b_gate, y_attn, y_rnn),
        sem=("parallel", "parallel"), rider=rider)
    return merged, carried


def _merge_bwd(proj, gl_off, b_gate, y_attn, y_rnn, d_m):
    T, D = y_attn.shape
    tm = _pick(T, (256, 128))
    ct = _pick(math.gcd(gl_off, D), (512, 256, 128))
    oa, orr, nd = gl_off // ct, (gl_off + D) // ct, D // ct

    def body(ga_ref, gr_ref, ba_ref, br_ref, ya_ref, yr_ref, dm_ref,
             dya_ref, dyr_ref, dga_ref, dgr_ref, dba_ref, dbr_ref):
        i = pl.program_id(1)

        @pl.when(i == 0)
        def _():
            dba_ref[...] = jnp.zeros_like(dba_ref)
            dbr_ref[...] = jnp.zeros_like(dbr_ref)

        ga = jax.nn.sigmoid(ga_ref[...] + ba_ref[...])
        gr = jax.nn.sigmoid(gr_ref[...] + br_ref[...])
        dm = dm_ref[...]
        dya_ref[...] = (dm * ga).astype(BF16)
        dyr_ref[...] = (dm * gr).astype(BF16)
        dga = dm * ya_ref[...] * ga * (1.0 - ga)
        dgr = dm * yr_ref[...] * gr * (1.0 - gr)
        dga_ref[...] = dga.astype(BF16)
        dgr_ref[...] = dgr.astype(BF16)
        dba_ref[...] += jnp.sum(dga, axis=0, keepdims=True)
        dbr_ref[...] += jnp.sum(dgr, axis=0, keepdims=True)

    blk = pl.BlockSpec((tm, ct), lambda j, i: (i, j))
    vec = pl.BlockSpec((1, ct), lambda j, i: (0, j))
    act = jax.ShapeDtypeStruct((T, D), BF16)
    v1 = jax.ShapeDtypeStruct((1, D), F32)
    return pl.pallas_call(
        body, name="merge_bwd", out_shape=(act, act, act, act, v1, v1), grid=(nd, T // tm),
        in_specs=[pl.BlockSpec((tm, ct), lambda j, i: (i, oa + j)), pl.BlockSpec((tm, ct), lambda j, i: (i, orr + j)),
                  vec, pl.BlockSpec((1, ct), lambda j, i: (0, nd + j)), blk, blk, blk],
        out_specs=(blk, blk, blk, blk, vec, vec),
        compiler_params=_cparams(("parallel", "arbitrary")),
    )(proj, proj, b_gate, b_gate, y_attn, y_rnn, d_m)


def _ln_fwd(x_res, delta, g, b, name, rider=None):
    T, D = x_res.shape
    tm = _pick(T, (256, 128))

    def body(x_ref, d_ref, g_ref, b_ref, y_ref, yb_ref, xh_ref, rs_ref):
        z = ALPHA * x_ref[...] + d_ref[...]
        mu = jnp.mean(z, axis=1, keepdims=True)
        zc = z - mu
        var = jnp.mean(zc * zc, axis=1, keepdims=True)
        rstd = lax.rsqrt(var + LN_EPS)
        xh = zc * rstd
        xh_ref[...] = xh
        rs_ref[...] = rstd
        y = xh * g_ref[...] + b_ref[...]
        y_ref[...] = y
        yb_ref[...] = y.astype(BF16)

    row = pl.BlockSpec((tm, D), lambda i: (i, 0))
    vec = pl.BlockSpec((1, D), lambda i: (0, 0))
    return _call(
        body, name=name,
        out_shape=(jax.ShapeDtypeStruct((T, D), F32), jax.ShapeDtypeStruct((T, D), BF16),
                   jax.ShapeDtypeStruct((T, D), F32), jax.ShapeDtypeStruct((T, 1), F32)),
        grid=(T // tm,), in_specs=[row, row, vec, vec],
        out_specs=(row, row, row, pl.BlockSpec((tm, 1), lambda i: (i, 0))),
        scratch_shapes=[], args=(x_res, delta, g, b), sem=("parallel",), rider=rider)


def _ln_bwd_rows(dy, xh, rstd, g):
    dxh = dy * g
    m1 = jnp.mean(dxh, axis=1, keepdims=True)
    m2 = jnp.mean(dxh * xh, axis=1, keepdims=True)
    return rstd * (dxh - m1 - xh * m2)


def _ln_loss_bwd(x_res, delta, g, b, target):
    T, D = x_res.shape
    tm = _pick(T, (256, 128))

    def body(x_ref, d_ref, g_ref, b_ref, t_ref, dz_ref, dzb_ref, loss_ref, dg_ref, db_ref):
        i = pl.program_id(0)

        @pl.when(i == 0)
        def _():
            loss_ref[...] = jnp.zeros_like(loss_ref)
            dg_ref[...] = jnp.zeros_like(dg_ref)
            db_ref[...] = jnp.zeros_like(db_ref)

        z = ALPHA * x_ref[...] + d_ref[...]
        mu = jnp.mean(z, axis=1, keepdims=True)
        zc = z - mu
        var = jnp.mean(zc * zc, axis=1, keepdims=True)
        rstd = lax.rsqrt(var + LN_EPS)
        xh = zc * rstd
        gv = g_ref[...]
        err = xh * gv + b_ref[...] - t_ref[...]
        loss_ref[...] += 0.5 * jnp.sum(jnp.mean(err * err, axis=1, keepdims=True))
        dy = err * (1.0 / D)
        dg_ref[...] += jnp.sum(dy * xh, axis=0, keepdims=True)
        db_ref[...] += jnp.sum(dy, axis=0, keepdims=True)
        dz = _ln_bwd_rows(dy, xh, rstd, gv)
        dz_ref[...] = dz
        dzb_ref[...] = dz.astype(BF16)

    row = pl.BlockSpec((tm, D), lambda i: (i, 0))
    vec = pl.BlockSpec((1, D), lambda i: (0, 0))
    return pl.pallas_call(
        body, name="ln2_loss_bwd",
        out_shape=(jax.ShapeDtypeStruct((T, D), F32), jax.ShapeDtypeStruct((T, D), BF16),
                   jax.ShapeDtypeStruct((8, LANES), F32),
                   jax.ShapeDtypeStruct((1, D), F32), jax.ShapeDtypeStruct((1, D), F32)),
        grid=(T // tm,), in_specs=[row, row, vec, vec, row],
        out_specs=(row, row, pl.BlockSpec((8, LANES), lambda i: (0, 0)), vec, vec),
        compiler_params=_cparams(("arbitrary",)),
    )(x_res, delta, g, b, target)


def _ln_bwd(dy, xh, rstd, g):
    T, D = dy.shape
    tm = _pick(T, (256, 128))

    def body(dy_ref, xh_ref, rs_ref, g_ref, dz_ref, dzb_ref, dg_ref, db_ref):
        i = pl.program_id(0)

        @pl.when(i == 0)
        def _():
            dg_ref[...] = jnp.zeros_like(dg_ref)
            db_ref[...] = jnp.zeros_like(db_ref)

        dyv, xhv = dy_ref[...], xh_ref[...]
        dg_ref[...] += jnp.sum(dyv * xhv, axis=0, keepdims=True)
        db_ref[...] += jnp.sum(dyv, axis=0, keepdims=True)
        dz = _ln_bwd_rows(dyv, xhv, rs_ref[...], g_ref[...])
        dz_ref[...] = dz
        dzb_ref[...] = dz.astype(BF16)

    row = pl.BlockSpec((tm, D), lambda i: (i, 0))
    vec = pl.BlockSpec((1, D), lambda i: (0, 0))
    return pl.pallas_call(
        body, name="ln1_bwd",
        out_shape=(jax.ShapeDtypeStruct((T, D), F32), jax.ShapeDtypeStruct((T, D), BF16),
                   jax.ShapeDtypeStruct((1, D), F32), jax.ShapeDtypeStruct((1, D), F32)),
        grid=(T // tm,), in_specs=[row, row, pl.BlockSpec((tm, 1), lambda i: (i, 0)), vec],
        out_specs=(row, row, vec, vec), compiler_params=_cparams(("arbitrary",)),
    )(dy, xh, rstd, g)


def _ffn_col_tile(T, d_ff):
    return _pick(d_ff, (256, 128)) if T >= 1024 else _pick(d_ff, (512, 256, 128))


def _ffn_gate(gp, cw_ref, cb_ref):
    return (cb_ref[...] + gp * cw_ref[2:3, :] + _shift_down(gp, 1) * cw_ref[1:2, :]
            + _shift_down(gp, 2) * cw_ref[0:1, :])


def _ffn_fwd(up, gpre, conv_w, conv_b, rider=None):
    T, d_ff = up.shape
    ct = _ffn_col_tile(T, d_ff)

    def body(up_ref, gp_ref, cw_ref, cb_ref, f_ref):
        gate = _ffn_gate(gp_ref[...], cw_ref, cb_ref)
        f_ref[...] = (_gelu(gate) * up_ref[...]).astype(BF16)

    col = pl.BlockSpec((T, ct), lambda j: (0, j))
    (f,), carried = _call(
        body, name="ffn_act_fwd", out_shape=[jax.ShapeDtypeStruct((T, d_ff), BF16)], grid=(d_ff // ct,),
        in_specs=[col, col, pl.BlockSpec((3, ct), lambda j: (0, j)), pl.BlockSpec((1, ct), lambda j: (0, j))],
        out_specs=[col], scratch_shapes=[], args=(up, gpre, conv_w, conv_b), sem=("parallel",), rider=rider)
    return f, carried


def _ffn_bwd(up, gpre, conv_w, conv_b, d_f, after=None):
    T, d_ff = up.shape
    ct = _ffn_col_tile(T, d_ff)

    def body(up_ref, gp_ref, cw_ref, cb_ref, df_ref, dup_ref, dgp_ref, dcw_ref, dcb_ref):
        gp = gp_ref[...]
        gate = _ffn_gate(gp, cw_ref, cb_ref)
        gel, dgel = _gelu_and_grad(gate)
        df = df_ref[...]
        dup_ref[...] = (df * gel).astype(BF16)
        dgate = df * up_ref[...] * dgel
        dcb_ref[...] = jnp.sum(dgate, axis=0, keepdims=True)
        dcw_ref[2:3, :] = jnp.sum(dgate * gp, axis=0, keepdims=True)
        dcw_ref[1:2, :] = jnp.sum(dgate * _shift_down(gp, 1), axis=0, keepdims=True)
        dcw_ref[0:1, :] = jnp.sum(dgate * _shift_down(gp, 2), axis=0, keepdims=True)
        dgp = (dgate * cw_ref[2:3, :] + _shift_up(dgate, 1) * cw_ref[1:2, :]
               + _shift_up(dgate, 2) * cw_ref[0:1, :])
        dgp_ref[...] = dgp.astype(BF16)

    col = pl.BlockSpec((T, ct), lambda j: (0, j))
    w3 = pl.BlockSpec((3, ct), lambda j: (0, j))
    v1 = pl.BlockSpec((1, ct), lambda j: (0, j))
    return _call(
        body, name="ffn_act_bwd",
        out_shape=(jax.ShapeDtypeStruct((T, d_ff), BF16), jax.ShapeDtypeStruct((T, d_ff), BF16),
                   jax.ShapeDtypeStruct((3, d_ff), F32), jax.ShapeDtypeStruct((1, d_ff), F32)),
        grid=(d_ff // ct,), in_specs=[col, col, w3, v1, col], out_specs=(col, col, w3, v1),
        scratch_shapes=[], args=(up, gpre, conv_w, conv_b, d_f), sem=("parallel",), after=after)[0]


def _adamw(w, g, m, v, name, after=None):
    R, C = w.shape
    tr = _row_tile(R, C * 4, 8, budget=1280 * 1024)
    c1 = 1.0 / (1.0 - ADAM_B1 ** ADAM_STEP)
    c2 = 1.0 / (1.0 - ADAM_B2 ** ADAM_STEP)

    def body(w_ref, g_ref, m_ref, v_ref, go_ref, d_ref, nm_ref, nv_ref):
        gv = g_ref[...]
        go_ref[...] = gv
        nm = ADAM_B1 * m_ref[...] + (1.0 - ADAM_B1) * gv
        nv = ADAM_B2 * v_ref[...] + (1.0 - ADAM_B2) * (gv * gv)
        nm_ref[...] = nm
        nv_ref[...] = nv
        d_ref[...] = -ADAM_LR * ((nm * c1) / (jnp.sqrt(nv * c2) + ADAM_EPS) + ADAM_WD * w_ref[...])

    blk = pl.BlockSpec((tr, C), lambda r: (r, 0))
    sh = jax.ShapeDtypeStruct((R, C), F32)
    return _call(body, name=name, out_shape=(sh,) * 4, grid=(R // tr,), in_specs=[blk] * 4, out_specs=(blk,) * 4,
                 scratch_shapes=[], args=(w, g, m, v), sem=("parallel",), after=after)[0]


def _group_blocks(w_blocks, per):
    nb, bw, _ = w_blocks.shape
    G = nb // per
    w4 = w_blocks.reshape(G, per, bw, bw)
    rows = []
    for p in range(per):
        parts = [w4[:, p] if q == p else jnp.zeros((G, bw, bw), w_blocks.dtype) for q in range(per)]
        rows.append(jnp.concatenate(parts, axis=2))
    return jnp.concatenate(rows, axis=1)


def _ungroup_blocks(w_groups, per):
    G, gw, _ = w_groups.shape
    bw = gw // per
    blocks = [w_groups[:, p * bw:(p + 1) * bw, p * bw:(p + 1) * bw] for p in range(per)]
    return jnp.stack(blocks, axis=1).reshape(G * per, bw, bw)


def _pack(parts):
    flat = jnp.concatenate([p.reshape(-1).astype(F32) for p in parts])
    n = flat.shape[0]
    rows = -(-n // LANES)
    rows = -(-rows // PACK_ROW_MULT) * PACK_ROW_MULT
    flat = jnp.pad(flat, (0, rows * LANES - n))
    return flat.reshape(rows, LANES)


def _unpack(packed, shapes):
    flat = packed.reshape(-1)
    out, off = [], 0
    for s in shapes:
        n = math.prod(s)
        out.append(flat[off:off + n].reshape(s))
        off += n
    return out


def kernel(x, w_in, b_gate, rnn_conv_w, rnn_conv_b, lru_wa, lru_ba, lru_wi, lru_bi, lru_lambda, attn_sinks, w_attn_proj, w_rnn_proj, w_out, ln1_g, ln1_b, ffn_w_up, ffn_w_gate, ffn_conv_w, ffn_conv_b, ffn_w_down, ln2_g, ln2_b, loss_target, m_w_in, m_b_gate, m_rnn_conv_w, m_rnn_conv_b, m_lru_wa, m_lru_ba, m_lru_wi, m_lru_bi, m_lru_lambda, m_attn_sinks, m_w_attn_proj, m_w_rnn_proj, m_w_out, m_ln1_g, m_ln1_b, m_ffn_w_up, m_ffn_w_gate, m_ffn_conv_w, m_ffn_conv_b, m_ffn_w_down, m_ln2_g, m_ln2_b, v_w_in, v_b_gate, v_rnn_conv_w, v_rnn_conv_b, v_lru_wa, v_lru_ba, v_lru_wi, v_lru_bi, v_lru_lambda, v_attn_sinks, v_w_attn_proj, v_w_rnn_proj, v_w_out, v_ln1_g, v_ln1_b, v_ffn_w_up, v_ffn_w_gate, v_ffn_conv_w, v_ffn_conv_b, v_ffn_w_down, v_ln2_g, v_ln2_b):
    weights = dict(w_in=w_in, b_gate=b_gate, rnn_conv_w=rnn_conv_w, rnn_conv_b=rnn_conv_b, lru_wa=lru_wa,
                   lru_ba=lru_ba, lru_wi=lru_wi, lru_bi=lru_bi, lru_lambda=lru_lambda, attn_sinks=attn_sinks,
                   w_attn_proj=w_attn_proj, w_rnn_proj=w_rnn_proj, w_out=w_out, ln1_g=ln1_g, ln1_b=ln1_b,
                   ffn_w_up=ffn_w_up, ffn_w_gate=ffn_w_gate, ffn_conv_w=ffn_conv_w, ffn_conv_b=ffn_conv_b,
                   ffn_w_down=ffn_w_down, ln2_g=ln2_g, ln2_b=ln2_b)
    m_in = dict(w_in=m_w_in, b_gate=m_b_gate, rnn_conv_w=m_rnn_conv_w, rnn_conv_b=m_rnn_conv_b, lru_wa=m_lru_wa,
                lru_ba=m_lru_ba, lru_wi=m_lru_wi, lru_bi=m_lru_bi, lru_lambda=m_lru_lambda, attn_sinks=m_attn_sinks,
                w_attn_proj=m_w_attn_proj, w_rnn_proj=m_w_rnn_proj, w_out=m_w_out, ln1_g=m_ln1_g, ln1_b=m_ln1_b,
                ffn_w_up=m_ffn_w_up, ffn_w_gate=m_ffn_w_gate, ffn_conv_w=m_ffn_conv_w, ffn_conv_b=m_ffn_conv_b,
                ffn_w_down=m_ffn_w_down, ln2_g=m_ln2_g, ln2_b=m_ln2_b)
    v_in = dict(w_in=v_w_in, b_gate=v_b_gate, rnn_conv_w=v_rnn_conv_w, rnn_conv_b=v_rnn_conv_b, lru_wa=v_lru_wa,
                lru_ba=v_lru_ba, lru_wi=v_lru_wi, lru_bi=v_lru_bi, lru_lambda=v_lru_lambda, attn_sinks=v_attn_sinks,
                w_attn_proj=v_w_attn_proj, w_rnn_proj=v_w_rnn_proj, w_out=v_w_out, ln1_g=v_ln1_g, ln1_b=v_ln1_b,
                ffn_w_up=v_ffn_w_up, ffn_w_gate=v_ffn_w_gate, ffn_conv_w=v_ffn_conv_w, ffn_conv_b=v_ffn_conv_b,
                ffn_w_down=v_ffn_w_down, ln2_g=v_ln2_g, ln2_b=v_ln2_b)
    order = list(weights)

    assert x.shape[0] == 1 and w_in.shape[0] == 1, "one sequence per device, depth 1"
    T, D = x.shape[1], x.shape[2]
    nq = attn_sinks.shape[-1]
    nkv = nq // GROUP
    d_attn, d_kv = nq * HEAD_DIM, nkv * HEAD_DIM
    d_rnn = rnn_conv_b.shape[-1]
    d_ff = ffn_conv_b.shape[-1]
    n_blocks, bw = lru_wa.shape[1], lru_wa.shape[2]
    per = (bw * LANES // math.gcd(bw, LANES)) // bw
    gw = per * bw
    assert n_blocks % per == 0 and d_rnn == n_blocks * bw
    q_off, k_off, v_off = 0, d_attn, d_attn + d_kv
    rx_off = d_attn + 2 * d_kv
    ry_off = rx_off + d_rnn
    gl_off = ry_off + d_rnn
    d_in = gl_off + 2 * D
    assert w_in.shape[-1] * N_SHARDS == d_in
    assert k_off % d_kv == 0 and rx_off % gw == 0 and T % ATTN_BLOCK == 0

    xi, yi, ci = lax.axis_index("x"), lax.axis_index("y"), lax.axis_index("c")
    j_me = 2 * xi + yi
    jc_arr = jnp.stack([j_me, ci]).astype(jnp.int32)

    x0 = x[0]
    x0b = _cast_bf16(x0, "cast_x")
    tgt = loss_target[0]
    big = ["w_in", "w_attn_proj", "w_rnn_proj", "w_out", "ffn_w_up", "ffn_w_gate", "ffn_w_down"]
    own = {n: _cast_bf16_into_slot(weights[n][0], jc_arr, "cast_" + n) for n in big}
    order_arr = jnp.stack([j_me, j_me ^ 2, j_me ^ 1, j_me ^ 3]).astype(jnp.int32)

    rcw_s, fcw_s = _all_gather_small([rnn_conv_w[0], ffn_conv_w[0]])
    rcw = jnp.concatenate([rcw_s[j] for j in range(N_SHARDS)], axis=1)
    fcw = jnp.concatenate([fcw_s[j] for j in range(N_SHARDS)], axis=1)

    wa_g = _group_blocks(lru_wa[0], per).astype(BF16)
    wi_g = _group_blocks(lru_wi[0], per).astype(BF16)

    near, diag = (0, 1), (2,)
    proj, w_in_s = _mm_gathering(x0b, own["w_in"], order_arr, "mm_proj")
    ici, last = {}, proj
    for n in big[1:]:
        ici[n] = _gather_step(own[n], last, "gather_start_" + n, start_d2d=False)
        last = ici[n][2]

    def forward_halves(n, after):
        sems, buf, _ = ici[n]
        return _gather_step(buf, after, "gather_forward_" + n, sems_in=sems, start_d2d=True)

    def gathered(d2d, after, n):
        sems, buf, _ = d2d
        return _gather_step(buf, after, "gather_finish_" + n, sems_in=sems)

    a_out = _attn_fwd(proj, attn_sinks, nq, (q_off, k_off, v_off), after=last)
    fw_ap = forward_halves("w_attn_proj", a_out)
    (b_out, h_all), _ = _rnn_fwd(proj, (rx_off, ry_off), rcw, rnn_conv_b, wa_g, wi_g, lru_ba, lru_bi, lru_lambda)
    fw_rp = forward_halves("w_rnn_proj", b_out)
    w_ap = gathered(fw_ap, b_out, "w_attn_proj").reshape(d_attn, D)
    y_attn = _mm(a_out, w_ap, name="mm_attn_proj")
    fw_o = forward_halves("w_out", y_attn)
    w_rp = gathered(fw_rp, y_attn, "w_rnn_proj").reshape(d_rnn, D)
    y_rnn = _mm(b_out, w_rp, name="mm_rnn_proj")
    merged, _ = _merge_fwd(proj, gl_off, b_gate, y_attn, y_rnn)
    w_o = gathered(fw_o, merged, "w_out").reshape(D, D)
    mix = _mm(merged, w_o, name="mm_out")
    fw_up = forward_halves("ffn_w_up", mix)
    (x1, x1b, xh1, rstd1), _ = _ln_fwd(x0, mix, ln1_g, ln1_b, "ln1_fwd")
    w_up_s = gathered(fw_up, x1b, "ffn_w_up")
    up = _mm(x1b, w_up_s, name="mm_up", b_shards=N_SHARDS)
    fw_gate = forward_halves("ffn_w_gate", up)
    w_gate_s = gathered(fw_gate, fw_gate[2], "ffn_w_gate")
    gpre = _mm(x1b, w_gate_s, name="mm_gate", b_shards=N_SHARDS)
    f_act, _ = _ffn_fwd(up, gpre, fcw, ffn_conv_b)
    fw_dn = forward_halves("ffn_w_down", f_act)
    w_dn = gathered(fw_dn, fw_dn[2], "ffn_w_down").reshape(d_ff, D)
    f_out = _mm(f_act, w_dn, name="mm_down")
    dz2, dz2b, loss_acc, dg2, db2 = _ln_loss_bwd(x1, f_out, ln2_g, ln2_b, tgt)

    def pair_sums(arrs, from_sibling, names):
        return [_pair_sum(g, la, jc_arr, "pair_sum_" + n) for g, la, n in zip(arrs, from_sibling, names)]

    def shard_sums(parts, landed, names):
        return [_shard_sum(cp, lb, jc_arr, "shard_sum_" + n) for cp, lb, n in zip(parts, landed, names)]

    halves = {}
    g_down = _mm(f_act, dz2b, name="mm_d_w_down", ta=True, out_dtype=BF16)
    g1 = [g_down.reshape(N_SHARDS, d_ff // N_SHARDS, D)]
    d_f, sib1 = _mm(dz2b, w_dn, name="mm_d_f", tb=True, rider=_pair_rider(g1))
    sent1 = _shard_exchange_start(pair_sums(g1, sib1, ["ffn_w_down"]), "shard_exchange_start_down")
    dup, dgp, d_fcw, d_fcb = _ffn_bwd(up, gpre, fcw, ffn_conv_b, d_f, after=sent1[4])
    g_up = _mm(x1b, dup, name="mm_d_w_up", ta=True, out_dtype=BF16, out_shards=N_SHARDS)
    g_gate = _mm(x1b, dgp, name="mm_d_w_gate", ta=True, out_dtype=BF16, out_shards=N_SHARDS)
    g2 = [g_up, g_gate]
    dx1_a, sib2 = _mm(dup, w_up_s, name="mm_dx1_up", tb=True, b_shards=N_SHARDS, adds=((ALPHA, dz2),),
                      rider=_pair_rider(g2))
    halves["ffn_w_down"], = shard_sums(*_shard_exchange_wait(sent1, dx1_a, "shard_exchange_wait_down"),
                                       ["ffn_w_down"])
    sent2 = _shard_exchange_start(pair_sums(g2, sib2, ["ffn_w_up", "ffn_w_gate"]), "shard_exchange_start_up_gate")
    dx1 = _mm(dgp, w_gate_s, name="mm_dx1_gate", tb=True, b_shards=N_SHARDS, adds=((1.0, dx1_a),), after=sent2[4])
    dz1, dz1b, dg1, db1 = _ln_bwd(dx1, xh1, rstd1, ln1_g)
    g_out = _mm(merged, dz1b, name="mm_d_w_out", ta=True, out_dtype=BF16)
    d_m = _mm(dz1b, w_o, name="mm_d_merged", tb=True)
    dya, dyr, dgl_a, dgl_r, dbg_a, dbg_r = _merge_bwd(proj, gl_off, b_gate, y_attn, y_rnn, d_m)
    g_ap = _mm(a_out, dya, name="mm_d_w_attn_proj", ta=True, out_dtype=BF16)
    g_rp = _mm(b_out, dyr, name="mm_d_w_rnn_proj", ta=True, out_dtype=BF16)
    names3 = ["w_out", "w_attn_proj", "w_rnn_proj"]
    g3 = [g_out.reshape(N_SHARDS, D // N_SHARDS, D), g_ap.reshape(N_SHARDS, d_attn // N_SHARDS, D),
          g_rp.reshape(N_SHARDS, d_rnn // N_SHARDS, D)]
    d_a = _mm(dya, w_ap, name="mm_d_attn", tb=True)
    d_b, sib3 = _mm(dyr, w_rp, name="mm_d_rnn", tb=True, rider=_pair_rider(g3))
    sent3 = _shard_exchange_start(pair_sums(g3, sib3, names3), "shard_exchange_start_mixers")
    dq, dk, dv, dsink = _attn_bwd(proj, d_a, attn_sinks, nq, (q_off, k_off, v_off), after=sent3[4])
    (drx, dry, d_rcw, d_rcb, d_ba, d_bi, d_lam, d_wa_g, d_wi_g), _ = _rnn_bwd(
        proj, (rx_off, ry_off), h_all, d_b, rcw, rnn_conv_b, wa_g, wi_g, lru_ba, lru_bi, lru_lambda)
    halves["ffn_w_up"], halves["ffn_w_gate"] = shard_sums(
        *_shard_exchange_wait(sent2, drx, "shard_exchange_wait_up_gate"), ["ffn_w_up", "ffn_w_gate"])
    d_proj = jnp.concatenate([dq, dk.astype(BF16), dv.astype(BF16), drx, dry, dgl_a, dgl_r], axis=1)
    ffn_names = ["ffn_w_down", "ffn_w_up", "ffn_w_gate"]
    g_in, shared_ffn = _mm(x0b, d_proj, name="mm_d_w_in", ta=True, out_dtype=BF16, out_shards=N_SHARDS,
                           rider=_share_rider([halves[n] for n in ffn_names]))
    halves["w_out"], halves["w_attn_proj"], halves["w_rnn_proj"] = shard_sums(
        *_shard_exchange_wait(sent3, g_in, "shard_exchange_wait_mixers"), names3)

    small_parts = [
        ("loss", loss_acc[0:1, 0:1]),
        ("b_gate", jnp.concatenate([dbg_a, dbg_r], axis=1)),
        ("rnn_conv_w", d_rcw), ("rnn_conv_b", d_rcb),
        ("lru_wa", _ungroup_blocks(d_wa_g, per)), ("lru_ba", d_ba),
        ("lru_wi", _ungroup_blocks(d_wi_g, per)), ("lru_bi", d_bi), ("lru_lambda", d_lam),
        ("attn_sinks", dsink[0:1, 0:nq]),
        ("ln1_g", dg1), ("ln1_b", db1),
        ("ffn_conv_w", d_fcw), ("ffn_conv_b", d_fcb),
        ("ln2_g", dg2), ("ln2_b", db2),
    ]
    packed = _pack([p for _, p in small_parts])
    rs = packed.shape[0]

    def whole(g):
        return g.reshape(2 * g.shape[1], g.shape[2])

    grads = {n: whole(g) for n, g in zip(ffn_names, shared_ffn)}
    out_g, out_d, out_m, out_v = {}, {}, {}, {}

    def adamw(n, after=None):
        shape = weights[n].shape
        two_d = (math.prod(shape[:-1]), shape[-1])
        g2, d2, m2, v2 = _adamw(weights[n].reshape(two_d), grads[n].reshape(two_d), m_in[n].reshape(two_d),
                                v_in[n].reshape(two_d), "adamw_" + n, after=after)
        out_g[n], out_d[n] = g2.reshape(shape), d2.reshape(shape)
        out_m[n], out_v[n] = m2.reshape(shape), v2.reshape(shape)

    g4 = [g_in, packed.reshape(N_SHARDS, rs // N_SHARDS, LANES)]
    sib4 = _run_rider(_pair_rider(g4), "pair_exchange_in_small")
    part4 = pair_sums(g4, sib4, ["w_in", "small"])
    grad_x, (lb_in, lb_small, *shared_mix) = _mm(
        d_proj, w_in_s, name="mm_d_x", tb=True, b_shards=N_SHARDS, adds=((ALPHA, dz1),),
        rider=_join_riders(_shard_exchange_rider(part4, _atoms([0], near) + _atoms([1])),
                           _share_rider([halves[n] for n in names3])))
    grads.update({n: whole(g) for n, g in zip(names3, shared_mix)})
    sent5 = _shard_exchange_start(part4[:1], "shard_exchange_start_in_diag", relations=diag, lands=[lb_in])
    for n in ffn_names + names3:
        adamw(n, after=sent5[4])
    (part_in,), (lb_in,) = _shard_exchange_wait(sent5, out_d[names3[-1]], "shard_exchange_wait_in_diag")
    part_small = part4[1]
    halves["w_in"], = shard_sums([part_in], [lb_in], ["w_in"])
    eighths = _shard_sum(part_small, lb_small, jc_arr, "shard_sum_small", all_slots=True)
    shared_in, reduced = _run_rider(_share_rider([halves["w_in"]], eighths), "share_in_small")
    grads["w_in"] = whole(shared_in)
    reduced = reduced.reshape(rs, LANES)
    small = dict(zip([n for n, _ in small_parts], _unpack(reduced, [p.shape for _, p in small_parts])))
    loss = small.pop("loss").reshape(())
    rcw_n = d_rnn // N_SHARDS
    fcw_n = d_ff // N_SHARDS
    small["rnn_conv_w"] = lax.dynamic_slice(small["rnn_conv_w"], (0, j_me * rcw_n), (4, rcw_n))
    small["ffn_conv_w"] = lax.dynamic_slice(small["ffn_conv_w"], (0, j_me * fcw_n), (3, fcw_n))
    for n, g in small.items():
        grads[n] = g

    for n in order:
        if n not in out_g:
            adamw(n)

    return (loss, grad_x.reshape(x.shape), *[out_g[n] for n in order], *[out_d[n] for n in order],
            *[out_m[n] for n in order], *[out_v[n] for n in order])
```

```python
import functools
import math

import jax
import jax.numpy as jnp
from jax import lax
from jax.experimental import pallas as pl
from jax.experimental.pallas import tpu as pltpu

F32 = jnp.float32
BF16 = jnp.bfloat16
MESH = pl.DeviceIdType.MESH

HEAD_DIM = 64
GROUP = 8
ATTN_BLOCK = 128
LRU_C = 8.0
LN_EPS = 1e-5
ALPHA = 2.0 ** 0.25
LANES = 128
N_SHARDS = 4
N_DEV = 8
VMEM_LIMIT = 56 * 1024 * 1024
MM_VMEM_BUDGET = 40 * 1024 * 1024
MM_MAX_TILE = 3072
PACK_ROW_MULT = 8 * 64
NEG = -1e30

ADAM_LR, ADAM_B1, ADAM_B2, ADAM_EPS, ADAM_WD, ADAM_STEP = 0.001, 0.9, 0.999, 1e-08, 0.01, 10

GELU_C = math.sqrt(2.0 / math.pi)
GELU_A = 0.044715


def _cparams(sem=None):
    kw = dict(vmem_limit_bytes=VMEM_LIMIT)
    if sem is not None:
        kw["dimension_semantics"] = sem
    return pltpu.CompilerParams(**kw)


def _pick(n, prefs):
    for p in prefs:
        if n % p == 0:
            return p
    return n


def _row_tile(rows, row_bytes, mult, budget=2 * 1024 * 1024):
    best = None
    for d in range(mult, rows + 1, mult):
        if rows % d == 0 and d * row_bytes <= budget:
            best = d
    return best if best is not None else rows


def _gelu(x):
    return 0.5 * x * (1.0 + jnp.tanh(GELU_C * (x + GELU_A * x * x * x)))


def _gelu_and_grad(x):
    t = jnp.tanh(GELU_C * (x + GELU_A * x * x * x))
    g = 0.5 * x * (1.0 + t)
    dg = 0.5 * (1.0 + t) + 0.5 * x * (1.0 - t * t) * GELU_C * (1.0 + 3.0 * GELU_A * x * x)
    return g, dg


def _shift_down(x, s, fill=0.0):
    row = lax.broadcasted_iota(jnp.int32, x.shape, 0)
    return jnp.where(row >= s, pltpu.roll(x, s, 0), fill)


def _shift_up(x, s, fill=0.0):
    n = x.shape[0]
    row = lax.broadcasted_iota(jnp.int32, x.shape, 0)
    return jnp.where(row < n - s, pltpu.roll(x, n - s, 0), fill)


def _mm(a, b, *, name, ta=False, tb=False, out_dtype=F32, adds=(), b_shards=1, out_shards=1,
        tm=None, tn=None, tk=None, rider=None, after=None):
    if ta:
        K, M = a.shape
    else:
        M, K = a.shape
    if b_shards > 1:
        n_sh = b.shape[-1]
        if tb:
            N = b.shape[1]
            assert b_shards * n_sh == K
        else:
            N = b_shards * n_sh
            assert b.shape[1] == K
    else:
        n_sh = None
        if tb:
            N = b.shape[0]
            assert b.shape[1] == K
        else:
            N = b.shape[1]
            assert b.shape[0] == K
    wide = (1024, 1536, 1280, 768, 640, 512, 256, 128)
    if tn is None:
        if b_shards > 1 and not tb:
            tn = n_sh if n_sh <= MM_MAX_TILE else _pick(n_sh, wide)
        elif out_shards > 1:
            tn = N // out_shards if N // out_shards <= MM_MAX_TILE else _pick(N // out_shards, wide)
        else:
            tn = _pick(N, wide)
    if tk is None:
        if b_shards > 1 and tb:
            tk = n_sh if n_sh <= MM_MAX_TILE else _pick(n_sh, wide)
        else:
            tk = K if K <= MM_MAX_TILE else _pick(K, (2048,) + wide)
    assert N % tn == 0 and K % tk == 0, (name, M, N, K, tn, tk)
    nk = K // tk
    n_add = len(adds)
    sa, sb, so = a.dtype.itemsize, b.dtype.itemsize, jnp.dtype(out_dtype).itemsize

    def vmem_bytes(tm_):
        return (2 * (tm_ * tk * sa + tk * tn * sb + tm_ * tn * so + n_add * tm_ * tn * 4)
                + (tm_ * tn * 4 if nk > 1 else 0))

    if tm is None:
        tm = _pick(M, (1024, 512, 256, 128)) if nk > 1 else _pick(M, (512, 256, 128))
        while vmem_bytes(tm) > MM_VMEM_BUDGET and tm % 256 == 0:
            tm //= 2
    assert M % tm == 0, (name, M, tm)
    b_outer = b.size * sb >= a.size * sa

    def ij(g0, g1):
        return (g1, g0) if b_outer else (g0, g1)

    def amap(g0, g1, k):
        i, _ = ij(g0, g1)
        return (k, i) if ta else (i, k)

    def bmap(g0, g1, k):
        _, j = ij(g0, g1)
        if b_shards > 1 and not tb:
            per = n_sh // tn
            return (j // per, k, j % per)
        if b_shards > 1 and tb:
            per = n_sh // tk
            return (k // per, j, k % per)
        return (j, k) if tb else (k, j)

    def omap(g0, g1, k):
        i, j = ij(g0, g1)
        if out_shards > 1:
            per_o = (N // out_shards) // tn
            return (j // per_o, i, j % per_o)
        return (i, j)

    a_spec = pl.BlockSpec((tk, tm) if ta else (tm, tk), amap)
    if b_shards > 1:
        b_spec = pl.BlockSpec((None, tn, tk) if tb else (None, tk, tn), bmap)
    else:
        b_spec = pl.BlockSpec((tn, tk) if tb else (tk, tn), bmap)
    add_specs = [pl.BlockSpec((tm, tn), lambda g0, g1, k: ij(g0, g1)) for _ in adds]
    if out_shards > 1:
        out_spec = pl.BlockSpec((None, tm, tn), omap)
        out_shape = jax.ShapeDtypeStruct((out_shards, M, N // out_shards), out_dtype)
    else:
        out_spec = pl.BlockSpec((tm, tn), omap)
        out_shape = jax.ShapeDtypeStruct((M, N), out_dtype)

    if ta:
        dims = (((0,), (0,)), ((), ()))
    elif tb:
        dims = (((1,), (1,)), ((), ()))
    else:
        dims = (((1,), (0,)), ((), ()))
    scales = tuple(s for s, _ in adds)

    def finish(r, add_refs, o_ref):
        for s, ref in zip(scales, add_refs):
            r = r + s * ref[...].astype(F32)
        o_ref[...] = r.astype(out_dtype)

    def body(a_ref, b_ref, *rest):
        add_refs = rest[:n_add]
        o_ref = rest[n_add]
        part = lax.dot_general(a_ref[...].astype(BF16), b_ref[...].astype(BF16), dims, preferred_element_type=F32)
        if nk == 1:
            finish(part, add_refs, o_ref)
            return
        acc = rest[n_add + 1]
        k = pl.program_id(2)

        @pl.when(k == 0)
        def _():
            acc[...] = part

        @pl.when(k > 0)
        def _():
            acc[...] += part

        @pl.when(k == nk - 1)
        def _():
            finish(acc[...], add_refs, o_ref)

    grid = (N // tn, M // tm, nk) if b_outer else (M // tm, N // tn, nk)
    (res,), carried = _call(
        body, name=name, grid=grid, in_specs=[a_spec, b_spec] + add_specs, out_specs=[out_spec],
        out_shape=[out_shape], scratch_shapes=[pltpu.VMEM((tm, tn), F32)] if nk > 1 else [],
        args=(a, b, *[x for _, x in adds]), sem=("parallel", "parallel", "arbitrary"), rider=rider, after=after)
    return (res, carried) if rider is not None else res


def _cast_bf16(w, name):
    R, C = w.shape
    tr = _row_tile(R, C * 4, 16)

    def body(w_ref, o_ref):
        o_ref[...] = w_ref[...].astype(BF16)

    return pl.pallas_call(
        body, name=name, out_shape=jax.ShapeDtypeStruct((R, C), BF16), grid=(R // tr,),
        in_specs=[pl.BlockSpec((tr, C), lambda r: (r, 0))], out_specs=pl.BlockSpec((tr, C), lambda r: (r, 0)),
        compiler_params=_cparams(("parallel",)),
    )(w)


def _cast_bf16_into_slot(w, jc_arr, name):
    R, C = w.shape
    tr = _row_tile(R, C * 4, 16)

    def body(jc_ref, w_ref, o_ref):
        o_ref[...] = w_ref[...].astype(BF16)

    gs = pltpu.PrefetchScalarGridSpec(
        num_scalar_prefetch=1, grid=(R // tr,),
        in_specs=[pl.BlockSpec((tr, C), lambda r, jc: (r, 0))],
        out_specs=pl.BlockSpec((None, tr, C), lambda r, jc: (jc[0], r, 0)))
    return pl.pallas_call(body, name=name, out_shape=jax.ShapeDtypeStruct((N_SHARDS, R, C), BF16), grid_spec=gs,
                          compiler_params=_cparams(("parallel",)))(jc_arr, w)


def _pair_sum(g, la, jc_arr, name):
    S, R, C = g.shape
    half = R // 2
    tr = _row_tile(half, C * 4, 16)
    nrt = half // tr
    dt = g.dtype

    def body(jc_ref, g_ref, la_ref, o_ref):
        o_ref[...] = (g_ref[...].astype(F32) + la_ref[...].astype(F32)).astype(dt)

    gs = pltpu.PrefetchScalarGridSpec(
        num_scalar_prefetch=1, grid=(S, nrt),
        in_specs=[pl.BlockSpec((None, tr, C), lambda s, r, jc: (s, jc[1] * nrt + r, 0)),
                  pl.BlockSpec((None, tr, C), lambda s, r, jc: (s, r, 0))],
        out_specs=pl.BlockSpec((None, tr, C), lambda s, r, jc: (s, r, 0)))
    return pl.pallas_call(body, name=name, out_shape=jax.ShapeDtypeStruct((S, half, C), dt), grid_spec=gs,
                          compiler_params=_cparams(("parallel", "parallel")))(jc_arr, g, la)


def _shard_sum(cp, lb, jc_arr, name, all_slots=False):
    S, h, C = cp.shape
    tr = _row_tile(h, C * 4, 16)

    def body(jc_ref, cp_ref, l0, l1, l2, o_ref):
        o_ref[...] = ((cp_ref[...].astype(F32) + l0[...].astype(F32)) + l1[...].astype(F32)) + l2[...].astype(F32)

    def lspec(kk):
        return pl.BlockSpec((None, tr, C), lambda r, jc: (kk, r, 0))

    if all_slots:
        out_spec = pl.BlockSpec((None, None, tr, C), lambda r, jc: (jc[0], jc[1], r, 0))
        out_shape = jax.ShapeDtypeStruct((S, 2, h, C), F32)
    else:
        out_spec = pl.BlockSpec((None, tr, C), lambda r, jc: (jc[1], r, 0))
        out_shape = jax.ShapeDtypeStruct((2, h, C), F32)
    gs = pltpu.PrefetchScalarGridSpec(
        num_scalar_prefetch=1, grid=(h // tr,),
        in_specs=[pl.BlockSpec((None, tr, C), lambda r, jc: (jc[0], r, 0)), lspec(0), lspec(1), lspec(2)],
        out_specs=out_spec)
    return pl.pallas_call(body, name=name, out_shape=out_shape, grid_spec=gs,
                          compiler_params=_cparams(("parallel",)))(jc_arr, cp, lb, lb, lb)


ANY = pl.BlockSpec(memory_space=pl.ANY)


def _place():
    x, y, c = lax.axis_index("x"), lax.axis_index("y"), lax.axis_index("c")
    chips = [(1 - x, y), (x, 1 - y), (1 - x, 1 - y)]
    return x, y, c, chips


class _Rider:
    def __init__(self, inputs, out_shape, aliases, sems, start, finish):
        self.inputs, self.out_shape, self.aliases, self.sems = list(inputs), list(out_shape), dict(aliases), list(sems)
        self.start, self.finish = start, finish


def _join_riders(r1, r2):
    i1, o1, s1 = len(r1.inputs), len(r1.out_shape), len(r1.sems)
    aliases = dict(r1.aliases)
    aliases.update({i1 + i: o1 + o for i, o in r2.aliases.items()})

    def start(ins, outs, sems):
        r1.start(ins[:i1], outs[:o1], sems[:s1])
        r2.start(ins[i1:], outs[o1:], sems[s1:])

    def finish(ins, outs, sems):
        r1.finish(ins[:i1], outs[:o1], sems[:s1])
        r2.finish(ins[i1:], outs[o1:], sems[s1:])

    return _Rider(r1.inputs + r2.inputs, r1.out_shape + r2.out_shape, aliases, r1.sems + r2.sems, start, finish)


def _after_rider(x):
    return _Rider([x], [], {}, [], lambda *a: None, lambda *a: None)


def _call(body, *, name, grid, in_specs, out_specs, out_shape, scratch_shapes, args, sem, rider=None, after=None):
    out_specs, out_shape = tuple(out_specs), tuple(out_shape)
    if after is not None:
        rider = _after_rider(after) if rider is None else _join_riders(_after_rider(after), rider)
    if rider is None:
        res = pl.pallas_call(body, name=name, out_shape=out_shape, grid=grid, in_specs=list(in_specs),
                             out_specs=out_specs, scratch_shapes=list(scratch_shapes),
                             compiler_params=_cparams(sem))(*args)
        return tuple(res), []
    n_in, n_out, n_sc = len(in_specs), len(out_specs), len(scratch_shapes)
    r_in, r_out = len(rider.inputs), len(rider.out_shape)

    def wrapped(*refs):
        p = 0
        host_in = refs[p:p + n_in]; p += n_in
        rid_in = refs[p:p + r_in]; p += r_in
        host_out = refs[p:p + n_out]; p += n_out
        rid_out = refs[p:p + r_out]; p += r_out
        host_sc = refs[p:p + n_sc]; p += n_sc
        rid_sem = refs[p:]
        first = functools.reduce(jnp.logical_and, [pl.program_id(a) == 0 for a in range(len(grid))])
        last = functools.reduce(jnp.logical_and, [pl.program_id(a) == grid[a] - 1 for a in range(len(grid))])

        @pl.when(first)
        def _():
            rider.start(rid_in, rid_out, rid_sem)

        body(*host_in, *host_out, *host_sc)

        @pl.when(last)
        def _():
            rider.finish(rid_in, rid_out, rid_sem)

    res = pl.pallas_call(
        wrapped, name=name, out_shape=out_shape + tuple(rider.out_shape), grid=grid,
        in_specs=list(in_specs) + [ANY] * r_in, out_specs=out_specs + (ANY,) * r_out,
        input_output_aliases={n_in + i: n_out + o for i, o in rider.aliases.items()},
        scratch_shapes=list(scratch_shapes) + rider.sems,
        compiler_params=_cparams(("arbitrary",) * len(grid)),
    )(*args, *rider.inputs)
    return tuple(res[:n_out]), list(res[n_out:])


def _run_rider(rider, name):
    def body(*refs):
        r_in, r_out = len(rider.inputs), len(rider.out_shape)
        ins, outs, sems = refs[:r_in], refs[r_in:r_in + r_out], refs[r_in + r_out:]
        rider.start(ins, outs, sems)
        rider.finish(ins, outs, sems)

    return pl.pallas_call(
        body, name=name, out_shape=rider.out_shape, in_specs=[ANY] * len(rider.inputs),
        out_specs=[ANY] * len(rider.out_shape), input_output_aliases=rider.aliases, scratch_shapes=rider.sems,
    )(*rider.inputs)


def _atoms(indices, kks=(0, 1, 2), q=0, nq=1):
    return [(i, kk, q, nq) for i in indices for kk in kks]


def _gather_rider(bufs, atoms=None):
    n = len(bufs)
    if atoms is None:
        atoms = _atoms(range(n))
    na = len(atoms)

    def rows(out, atom, core):
        i, _, q, nq = atom
        half = out[i].shape[1] // 2
        assert half % (16 * nq) == 0, (half, nq)
        return pl.ds(core * half + q * (half // nq), half // nq)

    def ici_copy(out, sems, a, slot, peer):
        c = lax.axis_index("c")
        blk = out[atoms[a][0]].at[slot, rows(out, atoms[a], c), :]
        return pltpu.make_async_remote_copy(
            src_ref=blk, dst_ref=blk, send_sem=sems[0].at[a], recv_sem=sems[1].at[a],
            device_id=(peer[0], peer[1], c), device_id_type=MESH)

    def d2d_copy(out, sems, a, slot, from_core):
        x, y, c, _ = _place()
        blk = out[atoms[a][0]].at[slot, rows(out, atoms[a], from_core), :]
        return pltpu.make_async_remote_copy(
            src_ref=blk, dst_ref=blk, send_sem=sems[2].at[a], recv_sem=sems[3].at[a],
            device_id=(x, y, 1 - c), device_id_type=MESH)

    def start(ins, out, sems):
        x, y, c, chips = _place()
        for a in range(na):
            ici_copy(out, sems, a, 2 * x + y, chips[atoms[a][1]]).start()

    def finish(ins, out, sems):
        x, y, c, chips = _place()
        src = [2 * chips[atoms[a][1]][0] + chips[atoms[a][1]][1] for a in range(na)]
        for a in range(na):
            ici_copy(out, sems, a, src[a], chips[atoms[a][1]]).wait_recv()
            d2d_copy(out, sems, a, src[a], c).start()
        for a in range(na):
            d2d_copy(out, sems, a, src[a], 1 - c).wait_recv()
        for a in range(na):
            ici_copy(out, sems, a, 2 * x + y, chips[atoms[a][1]]).wait_send()
            d2d_copy(out, sems, a, src[a], c).wait_send()

    return _Rider(bufs, [jax.ShapeDtypeStruct(s.shape, s.dtype) for s in bufs], {i: i for i in range(n)},
                  [pltpu.SemaphoreType.DMA((na,))] * 4, start, finish)


def _mm_gathering(a, buf, order_arr, name, after):
    M, K = a.shape
    S, _, n = buf.shape
    tm = _pick(M, (512, 256, 128))
    n_i = M // tm
    half = K // 2

    def body(order_ref, a_ref, w_in_ref, after_ref, o_ref, w_ref, b_vmem, load_sem, s_ici, r_ici, s_d2d, r_d2d):
        s, i = pl.program_id(0), pl.program_id(1)
        x, y, c, chips = _place()
        j_me = 2 * x + y
        slots = [2 * px + py for px, py in chips]

        def ici(kk, slot):
            blk = w_ref.at[slot, pl.ds(c * half, half), :]
            return pltpu.make_async_remote_copy(
                src_ref=blk, dst_ref=blk, send_sem=s_ici.at[kk], recv_sem=r_ici.at[kk],
                device_id=(chips[kk][0], chips[kk][1], c), device_id_type=MESH)

        def d2d(kk, from_core):
            blk = w_ref.at[slots[kk], pl.ds(from_core * half, half), :]
            return pltpu.make_async_remote_copy(
                src_ref=blk, dst_ref=blk, send_sem=s_d2d.at[kk], recv_sem=r_d2d.at[kk],
                device_id=(x, y, 1 - c), device_id_type=MESH)

        def load(slot, b):
            return pltpu.make_async_copy(w_ref.at[slot], b_vmem.at[b], load_sem.at[b])

        @pl.when(jnp.logical_and(s == 0, i == 0))
        def _():
            for kk in range(3):
                ici(kk, j_me).start()
            load(j_me, 0).start()

        @pl.when(i == 0)
        def _():
            load(order_ref[s], s % 2).wait()

        o_ref[...] = jnp.dot(a_ref[...], b_vmem[s % 2], preferred_element_type=F32)

        last = i == n_i - 1

        @pl.when(jnp.logical_and(last, s == 0))
        def _():
            ici(0, slots[0]).wait_recv()
            d2d(0, c).start()
            ici(1, slots[1]).wait_recv()
            d2d(1, c).start()
            d2d(0, 1 - c).wait_recv()
            load(slots[0], 1).start()

        @pl.when(jnp.logical_and(last, s == 1))
        def _():
            d2d(1, 1 - c).wait_recv()
            load(slots[1], 0).start()

        @pl.when(jnp.logical_and(last, s == 2))
        def _():
            ici(2, slots[2]).wait_recv()
            d2d(2, c).start()
            d2d(2, 1 - c).wait_recv()
            load(slots[2], 1).start()

        @pl.when(jnp.logical_and(last, s == 3))
        def _():
            for kk in range(3):
                ici(kk, j_me).wait_send()
                d2d(kk, c).wait_send()

    gs = pltpu.PrefetchScalarGridSpec(
        num_scalar_prefetch=1, grid=(S, n_i),
        in_specs=[pl.BlockSpec((tm, K), lambda s, i, order: (i, 0)), ANY, ANY],
        out_specs=[pl.BlockSpec((tm, n), lambda s, i, order: (i, order[s])), ANY],
        scratch_shapes=[pltpu.VMEM((2, K, n), BF16), pltpu.SemaphoreType.DMA((2,))]
        + [pltpu.SemaphoreType.DMA((3,))] * 4)
    return pl.pallas_call(
        body, name=name, grid_spec=gs,
        out_shape=[jax.ShapeDtypeStruct((M, S * n), F32), jax.ShapeDtypeStruct(buf.shape, buf.dtype)],
        input_output_aliases={2: 1}, compiler_params=_cparams(("arbitrary", "arbitrary")),
    )(order_arr, a, buf, after)


def _all_gather_small(shards):
    n = len(shards)

    def body(*refs):
        w = refs[:n]
        out = refs[n:2 * n]
        local_sem, s_sem, r_sem = refs[2 * n:]
        x, y, c, chips = _place()
        j_me = 2 * x + y
        cps = []
        for i in range(n):
            lc = pltpu.make_async_copy(w[i], out[i].at[j_me], local_sem.at[i])
            lc.start()
            cps.append(lc)
        sends = []
        for i in range(n):
            for kk, (px, py) in enumerate(chips):
                cp = pltpu.make_async_remote_copy(
                    src_ref=w[i], dst_ref=out[i].at[j_me], send_sem=s_sem.at[3 * i + kk],
                    recv_sem=r_sem.at[3 * i + kk], device_id=(px, py, c), device_id_type=MESH)
                cp.start()
                sends.append(cp)
        for i in range(n):
            for kk, (px, py) in enumerate(chips):
                sends[3 * i + kk].wait_send()
                pltpu.make_async_remote_copy(
                    src_ref=w[i], dst_ref=out[i].at[2 * px + py], send_sem=s_sem.at[3 * i + kk],
                    recv_sem=r_sem.at[3 * i + kk], device_id=(px, py, c), device_id_type=MESH).wait_recv()
        for lc in cps:
            lc.wait()

    out_shape = [jax.ShapeDtypeStruct((N_SHARDS,) + s.shape, s.dtype) for s in shards]
    return pl.pallas_call(
        body, name="all_gather_conv_weights", out_shape=out_shape, in_specs=[ANY] * n, out_specs=[ANY] * n,
        scratch_shapes=[pltpu.SemaphoreType.DMA((n,)), pltpu.SemaphoreType.DMA((3 * n,)),
                        pltpu.SemaphoreType.DMA((3 * n,))],
    )(*shards)


def _pair_rider(grads):
    n = len(grads)

    def copies(g, la, sems):
        x, y, c, _ = _place()
        return [pltpu.make_async_remote_copy(
            src_ref=g[i].at[:, pl.ds((1 - c) * (g[i].shape[1] // 2), g[i].shape[1] // 2), :], dst_ref=la[i],
            send_sem=sems[0].at[i], recv_sem=sems[1].at[i], device_id=(x, y, 1 - c), device_id_type=MESH)
            for i in range(n)]

    def start(g, la, sems):
        for cp in copies(g, la, sems):
            cp.start()

    def finish(g, la, sems):
        for cp in copies(g, la, sems):
            cp.wait()

    return _Rider(grads, [jax.ShapeDtypeStruct((s.shape[0], s.shape[1] // 2, s.shape[2]), s.dtype) for s in grads],
                  {}, [pltpu.SemaphoreType.DMA((n,)), pltpu.SemaphoreType.DMA((n,))], start, finish)


def _shard_exchange_rider(cps_in, atoms=None):
    n = len(cps_in)
    if atoms is None:
        atoms = _atoms(range(n))

    def copies(ins, lb, sems):
        x, y, c, chips = _place()
        out = []
        for a, (i, kk, q, nq) in enumerate(atoms):
            h = ins[i].shape[1]
            assert h % (16 * nq) == 0, (h, nq)
            rows = pl.ds(q * (h // nq), h // nq)
            px, py = chips[kk]
            out.append(pltpu.make_async_remote_copy(
                src_ref=ins[i].at[2 * px + py, rows, :], dst_ref=lb[i].at[kk, rows, :],
                send_sem=sems[0].at[a], recv_sem=sems[1].at[a], device_id=(px, py, c), device_id_type=MESH))
        return out

    def start(ins, lb, sems):
        for cp in copies(ins, lb, sems):
            cp.start()

    def finish(ins, lb, sems):
        for cp in copies(ins, lb, sems):
            cp.wait()

    return _Rider(cps_in, [jax.ShapeDtypeStruct((3,) + s.shape[1:], s.dtype) for s in cps_in], {},
                  [pltpu.SemaphoreType.DMA((len(atoms),)), pltpu.SemaphoreType.DMA((len(atoms),))], start, finish)


HBM = pl.BlockSpec(memory_space=pltpu.HBM)
SEM = pl.BlockSpec(memory_space=pltpu.SEMAPHORE)


def _shard_copies(part_refs, land_refs, send_sems, recv_sems, relations):
    x, y, c, chips = _place()
    nr = len(relations)
    return [pltpu.make_async_remote_copy(
        src_ref=part_refs[i].at[2 * chips[kk][0] + chips[kk][1]], dst_ref=land_refs[i].at[kk],
        send_sem=send_sems.at[nr * i + r], recv_sem=recv_sems.at[nr * i + r],
        device_id=(chips[kk][0], chips[kk][1], c), device_id_type=MESH)
        for i in range(len(part_refs)) for r, kk in enumerate(relations)]


SIDE_EFFECT = pltpu.SideEffectType.DATAFLOW_SIDE_EFFECTING


def _shard_exchange_start(parts, name, relations=(0, 1, 2), lands=None):
    n = len(parts)
    ns = n * len(relations)

    def body(*refs):
        part_refs, land_refs = refs[:n], refs[n:2 * n]
        send_sems, recv_sems = refs[2 * n], refs[2 * n + 1]
        token = refs[4 * n + 2]
        for cp in _shard_copies(part_refs, land_refs, send_sems, recv_sems, relations):
            cp.start()
        token[...] = jnp.zeros_like(token)

    if lands is None:
        lands = [lax.empty((3,) + p.shape[1:], p.dtype) for p in parts]
    bufs = list(parts) + list(lands)
    res = pl.pallas_call(
        body, name=name,
        out_shape=(pltpu.SemaphoreType.DMA((ns,)), pltpu.SemaphoreType.DMA((ns,)),
                   *[pltpu.HBM(b.shape, b.dtype) for b in bufs], jax.ShapeDtypeStruct((8, LANES), F32)),
        in_specs=(HBM,) * (2 * n), out_specs=(SEM, SEM) + (HBM,) * (2 * n) + (pl.BlockSpec(memory_space=pltpu.VMEM),),
        input_output_aliases={i: 2 + i for i in range(2 * n)},
        compiler_params=pltpu.CompilerParams(has_side_effects=SIDE_EFFECT),
    )(*[pltpu.with_memory_space_constraint(b, pltpu.HBM) for b in bufs])
    return res[0], res[1], list(res[2:2 + n]), list(res[2 + n:2 + 2 * n]), res[2 + 2 * n], relations


def _shard_exchange_wait(started, after, name):
    send_sems, recv_sems, parts, lands, _, relations = started
    n = len(parts)

    def body(*refs):
        part_refs, land_refs = refs[:n], refs[n:2 * n]
        send_sems_ref, recv_sems_ref = refs[2 * n], refs[2 * n + 1]
        for cp in _shard_copies(part_refs, land_refs, send_sems_ref, recv_sems_ref, relations):
            cp.wait_send()
            cp.wait_recv()

    bufs = parts + lands
    res = pl.pallas_call(
        body, name=name, out_shape=tuple(pltpu.HBM(b.shape, b.dtype) for b in bufs),
        in_specs=(HBM,) * (2 * n) + (SEM, SEM, ANY), out_specs=(HBM,) * (2 * n),
        input_output_aliases={i: i for i in range(2 * n)},
        compiler_params=pltpu.CompilerParams(has_side_effects=SIDE_EFFECT),
    )(*bufs, send_sems, recv_sems, after)
    return list(res[:n]), list(res[n:])


def _gather_copies(buf_ref, send_sems, recv_sems, over_d2d, arriving):
    x, y, c, chips = _place()
    half = buf_ref.shape[1] // 2
    out = []
    for kk, (px, py) in enumerate(chips):
        if over_d2d:
            slot, core, peer = 2 * px + py, (1 - c) if arriving else c, (x, y, 1 - c)
        else:
            slot, core, peer = (2 * px + py) if arriving else (2 * x + y), c, (px, py, c)
        blk = buf_ref.at[slot, pl.ds(core * half, half), :]
        out.append(pltpu.make_async_remote_copy(src_ref=blk, dst_ref=blk, send_sem=send_sems.at[kk],
                                                recv_sem=recv_sems.at[kk], device_id=peer, device_id_type=MESH))
    return out


def _gather_step(buf, after, name, sems_in=None, start_d2d=None):
    n_sem = 0 if sems_in is None else 2

    def body(*refs):
        buf_ref = refs[0]
        ins = refs[1:1 + n_sem]
        outs = refs[2 + n_sem:]
        if sems_in is not None:
            waited_d2d = start_d2d is None
            for mine, theirs in zip(_gather_copies(buf_ref, ins[0], ins[1], waited_d2d, False),
                                    _gather_copies(buf_ref, ins[0], ins[1], waited_d2d, True)):
                theirs.wait_recv()
                mine.wait_send()
        if start_d2d is not None:
            for cp in _gather_copies(buf_ref, outs[0], outs[1], start_d2d, False):
                cp.start()
            outs[3][...] = jnp.zeros_like(outs[3])

    sem_out = () if start_d2d is None else (pltpu.SemaphoreType.DMA((3,)), pltpu.SemaphoreType.DMA((3,)))
    tok_out = () if start_d2d is None else (jax.ShapeDtypeStruct((8, LANES), F32),)
    res = pl.pallas_call(
        body, name=name,
        out_shape=sem_out + (pltpu.HBM(buf.shape, buf.dtype),) + tok_out,
        in_specs=(HBM,) + (SEM,) * n_sem + (ANY,),
        out_specs=(SEM,) * len(sem_out) + (HBM,) + (pl.BlockSpec(memory_space=pltpu.VMEM),) * len(tok_out),
        input_output_aliases={0: len(sem_out)},
        compiler_params=pltpu.CompilerParams(has_side_effects=SIDE_EFFECT),
    )(pltpu.with_memory_space_constraint(buf, pltpu.HBM), *(sems_in or ()), after)
    if start_d2d is None:
        return res[0]
    return (res[0], res[1]), res[2], res[3]


def _share_rider(halves, eighths=None):
    n = len(halves)
    bufs = list(halves) + ([eighths] if eighths is not None else [])

    def half_copy(out, sems, i, core):
        x, y, c, _ = _place()
        blk = out[i].at[core]
        return pltpu.make_async_remote_copy(src_ref=blk, dst_ref=blk, send_sem=sems[0].at[i], recv_sem=sems[1].at[i],
                                            device_id=(x, y, 1 - c), device_id_type=MESH)

    def eighth_copy(out, sems, r, mine):
        x, y, c, _ = _place()
        px, py, pc = x ^ ((r >> 2) & 1), y ^ ((r >> 1) & 1), c ^ (r & 1)
        blk = out[n].at[2 * x + y, c] if mine else out[n].at[2 * px + py, pc]
        return pltpu.make_async_remote_copy(src_ref=blk, dst_ref=blk, send_sem=sems[2].at[r - 1],
                                            recv_sem=sems[3].at[r - 1], device_id=(px, py, pc), device_id_type=MESH)

    def start(ins, out, sems):
        c = lax.axis_index("c")
        for i in range(n):
            half_copy(out, sems, i, c).start()
        if eighths is not None:
            for r in range(1, N_DEV):
                eighth_copy(out, sems, r, True).start()

    def finish(ins, out, sems):
        c = lax.axis_index("c")
        for i in range(n):
            half_copy(out, sems, i, 1 - c).wait_recv()
        if eighths is not None:
            for r in range(1, N_DEV):
                eighth_copy(out, sems, r, False).wait_recv()
        for i in range(n):
            half_copy(out, sems, i, c).wait_send()
        if eighths is not None:
            for r in range(1, N_DEV):
                eighth_copy(out, sems, r, True).wait_send()

    return _Rider(bufs, [jax.ShapeDtypeStruct(s.shape, s.dtype) for s in bufs], {i: i for i in range(len(bufs))},
                  [pltpu.SemaphoreType.DMA((max(n, 1),)), pltpu.SemaphoreType.DMA((max(n, 1),)),
                   pltpu.SemaphoreType.DMA((N_DEV - 1,)), pltpu.SemaphoreType.DMA((N_DEV - 1,))], start, finish)


ATTN_ROWS = GROUP * ATTN_BLOCK
ATTN_KEYS = 2 * ATTN_BLOCK


def _attn_geometry(n):
    row = lax.broadcasted_iota(jnp.int32, (ATTN_ROWS, ATTN_KEYS), 0)
    col = lax.broadcasted_iota(jnp.int32, (ATTN_ROWS, ATTN_KEYS), 1)
    dist = ATTN_BLOCK + jnp.bitwise_and(row, ATTN_BLOCK - 1) - col
    valid = jnp.logical_and(jnp.logical_and(dist >= 0, dist < ATTN_BLOCK),
                            jnp.logical_or(col >= ATTN_BLOCK, n > 0))
    return dist.astype(F32), valid


def _per_head_column(values):
    head = lax.broadcasted_iota(jnp.int32, (ATTN_ROWS, 1), 0) // ATTN_BLOCK
    col = jnp.zeros((ATTN_ROWS, 1), F32)
    for hh, v in enumerate(values):
        col = jnp.where(head == hh, v, col)
    return col


def _stack_heads(ref, g):
    return jnp.concatenate(
        [ref[:, (g * GROUP + hh) * HEAD_DIM:(g * GROUP + hh + 1) * HEAD_DIM].astype(BF16) for hh in range(GROUP)],
        axis=0)


def _attn_probs(q_s, k2, slope_col, sink_col, dist, valid):
    s = lax.dot_general(q_s, k2, (((1,), (1,)), ((), ())), preferred_element_type=F32) * (HEAD_DIM ** -0.5)
    s = jnp.where(valid, s - slope_col * dist, NEG)
    m = jnp.maximum(jnp.max(s, axis=1, keepdims=True), sink_col)
    e = jnp.exp(s - m)
    es = jnp.exp(sink_col - m)
    inv = 1.0 / (jnp.sum(e, axis=1, keepdims=True) + es)
    return e * inv, es * inv


def _attn_specs(T, d_attn, d_kv, q_blk, k_blk, v_blk):
    bq = pl.BlockSpec((ATTN_BLOCK, d_attn), lambda n: (n, q_blk))
    kp = pl.BlockSpec((ATTN_BLOCK, d_kv), lambda n: (jnp.maximum(n - 1, 0), k_blk))
    kc = pl.BlockSpec((ATTN_BLOCK, d_kv), lambda n: (n, k_blk))
    vp = pl.BlockSpec((ATTN_BLOCK, d_kv), lambda n: (jnp.maximum(n - 1, 0), v_blk))
    vc = pl.BlockSpec((ATTN_BLOCK, d_kv), lambda n: (n, v_blk))
    return bq, kp, kc, vp, vc


def _attn_fwd(proj, sinks, nq, cols, after=None):
    T = proj.shape[0]
    nkv = nq // GROUP
    d_attn, d_kv = nq * HEAD_DIM, nkv * HEAD_DIM
    q_off, k_off, v_off = cols
    bq, kp, kc, vp, vc = _attn_specs(T, d_attn, d_kv, q_off // d_attn, k_off // d_kv, v_off // d_kv)

    def body(sink_ref, q_ref, kp_ref, kc_ref, vp_ref, vc_ref, o_ref):
        n = pl.program_id(0)
        dist, valid = _attn_geometry(n)
        for g in range(nkv):
            ks = slice(g * HEAD_DIM, (g + 1) * HEAD_DIM)
            k2 = jnp.concatenate([kp_ref[:, ks], kc_ref[:, ks]], axis=0).astype(BF16)
            v2 = jnp.concatenate([vp_ref[:, ks], vc_ref[:, ks]], axis=0).astype(BF16)
            slope_col = _per_head_column([2.0 ** (-8.0 * (g * GROUP + hh + 1) / nq) for hh in range(GROUP)])
            sink_col = _per_head_column([sink_ref[0, g * GROUP + hh] for hh in range(GROUP)])
            p, _ = _attn_probs(_stack_heads(q_ref, g), k2, slope_col, sink_col, dist, valid)
            o = jnp.dot(p.astype(BF16), v2, preferred_element_type=F32).astype(BF16)
            for hh in range(GROUP):
                h = g * GROUP + hh
                o_ref[:, h * HEAD_DIM:(h + 1) * HEAD_DIM] = o[hh * ATTN_BLOCK:(hh + 1) * ATTN_BLOCK, :]

    (out,), carried = _call(
        body, name="attn_fwd", out_shape=[jax.ShapeDtypeStruct((T, d_attn), BF16)], grid=(T // ATTN_BLOCK,),
        in_specs=[pl.BlockSpec(memory_space=pltpu.SMEM), bq, kp, kc, vp, vc],
        out_specs=[pl.BlockSpec((ATTN_BLOCK, d_attn), lambda n: (n, 0))], scratch_shapes=[],
        args=(sinks, proj, proj, proj, proj, proj), sem=("parallel",), after=after)
    return out


def _attn_bwd(proj, d_attn_out, sinks, nq, cols, after=None):
    T = proj.shape[0]
    nkv = nq // GROUP
    d_attn, d_kv = nq * HEAD_DIM, nkv * HEAD_DIM
    q_off, k_off, v_off = cols
    bq, kp, kc, vp, vc = _attn_specs(T, d_attn, d_kv, q_off // d_attn, k_off // d_kv, v_off // d_kv)
    scale = HEAD_DIM ** -0.5
    dn_t = (((1,), (1,)), ((), ()))
    dn_r = (((0,), (0,)), ((), ()))

    def body(sink_ref, q_ref, kp_ref, kc_ref, vp_ref, vc_ref, do_ref, dq_ref, dk_ref, dv_ref, ds_ref):
        n = pl.program_id(0)

        @pl.when(n == 0)
        def _():
            dk_ref[...] = jnp.zeros_like(dk_ref)
            dv_ref[...] = jnp.zeros_like(dv_ref)
            ds_ref[...] = jnp.zeros_like(ds_ref)

        dist, valid = _attn_geometry(n)
        rows_c = pl.ds(pl.multiple_of(n * ATTN_BLOCK, ATTN_BLOCK), ATTN_BLOCK)
        rows_p = pl.ds(pl.multiple_of(jnp.maximum(n - 1, 0) * ATTN_BLOCK, ATTN_BLOCK), ATTN_BLOCK)
        lane = lax.broadcasted_iota(jnp.int32, ds_ref.shape, 1)
        srow = lax.broadcasted_iota(jnp.int32, ds_ref.shape, 0)
        ds_acc = jnp.zeros(ds_ref.shape, F32)
        for g in range(nkv):
            ks = slice(g * HEAD_DIM, (g + 1) * HEAD_DIM)
            k2 = jnp.concatenate([kp_ref[:, ks], kc_ref[:, ks]], axis=0).astype(BF16)
            v2 = jnp.concatenate([vp_ref[:, ks], vc_ref[:, ks]], axis=0).astype(BF16)
            slope_col = _per_head_column([2.0 ** (-8.0 * (g * GROUP + hh + 1) / nq) for hh in range(GROUP)])
            sink_col = _per_head_column([sink_ref[0, g * GROUP + hh] for hh in range(GROUP)])
            q_s = _stack_heads(q_ref, g)
            do_s = _stack_heads(do_ref, g)
            p, p_sink = _attn_probs(q_s, k2, slope_col, sink_col, dist, valid)
            dp = lax.dot_general(do_s, v2, dn_t, preferred_element_type=F32)
            delta = jnp.sum(p * dp, axis=1, keepdims=True)
            ds = (p * (dp - delta)).astype(BF16)
            sink_part = p_sink * delta
            dq = (jnp.dot(ds, k2, preferred_element_type=F32) * scale).astype(BF16)
            for hh in range(GROUP):
                h = g * GROUP + hh
                blk = slice(hh * ATTN_BLOCK, (hh + 1) * ATTN_BLOCK)
                dq_ref[:, h * HEAD_DIM:(h + 1) * HEAD_DIM] = dq[blk, :]
                ds_acc = ds_acc + jnp.where(jnp.logical_and(lane == h, srow == 0), -jnp.sum(sink_part[blk, :]), 0.0)
            dk2 = lax.dot_general(ds, q_s, dn_r, preferred_element_type=F32) * scale
            dv2 = lax.dot_general(p.astype(BF16), do_s, dn_r, preferred_element_type=F32)
            dk_ref[rows_p, ks] += dk2[:ATTN_BLOCK, :]
            dv_ref[rows_p, ks] += dv2[:ATTN_BLOCK, :]
            dk_ref[rows_c, ks] += dk2[ATTN_BLOCK:, :]
            dv_ref[rows_c, ks] += dv2[ATTN_BLOCK:, :]
        ds_ref[...] += ds_acc

    out_shape = (jax.ShapeDtypeStruct((T, d_attn), BF16), jax.ShapeDtypeStruct((T, d_kv), F32),
                 jax.ShapeDtypeStruct((T, d_kv), F32), jax.ShapeDtypeStruct((8, LANES), F32))
    return _call(
        body, name="attn_bwd", out_shape=out_shape, grid=(T // ATTN_BLOCK,),
        in_specs=[pl.BlockSpec(memory_space=pltpu.SMEM), bq, kp, kc, vp, vc,
                  pl.BlockSpec((ATTN_BLOCK, d_attn), lambda n: (n, 0))],
        out_specs=(pl.BlockSpec((ATTN_BLOCK, d_attn), lambda n: (n, 0)),
                   pl.BlockSpec((T, d_kv), lambda n: (0, 0)), pl.BlockSpec((T, d_kv), lambda n: (0, 0)),
                   pl.BlockSpec((8, LANES), lambda n: (0, 0))),
        scratch_shapes=[], args=(sinks, proj, proj, proj, proj, proj, d_attn_out), sem=("arbitrary",), after=after)[0]


def _rnn_tile(T):
    return _pick(T, (256, 128))


def _rnn_gates(x_ext, cw_ref, cb_ref, wa_ref, wi_ref, ba_ref, bi_ref, lam_ref, tt):
    xs = [pltpu.roll(x_ext, 3 - k, 0)[8:, :] if k < 3 else x_ext[8:, :] for k in range(4)]
    cx = cb_ref[...] + xs[0] * cw_ref[0:1, :]
    for k in range(1, 4):
        cx = cx + xs[k] * cw_ref[k:k + 1, :]
    cxb = cx.astype(BF16)
    r = jax.nn.sigmoid(jnp.dot(cxb, wa_ref[...], preferred_element_type=F32) + ba_ref[...])
    i = jax.nn.sigmoid(jnp.dot(cxb, wi_ref[...], preferred_element_type=F32) + bi_ref[...])
    lam = lam_ref[...]
    sp = jnp.maximum(-lam, 0.0) + jnp.log1p(jnp.exp(-jnp.abs(lam)))
    log_a = -LRU_C * r * sp
    a = jnp.exp(log_a)
    z = 2.0 * log_a
    em1 = jnp.where(z > -1e-2, z * (1.0 + z * (0.5 + z * (1.0 / 6.0 + z * (1.0 / 24.0)))), jnp.exp(z) - 1.0)
    s = jnp.sqrt(-em1)
    return xs, cx, r, i, sp, a, s


def _rnn_specs(T, gw, tt, rx_blk, ry_blk, rev):
    nT = T // tt
    hb = tt // 8

    def tile(t):
        return (nT - 1 - t) if rev else t

    rx = pl.BlockSpec((tt, gw), lambda g, t: (tile(t), rx_blk + g))
    rx_halo = pl.BlockSpec((8, gw), lambda g, t: (jnp.maximum(tile(t) * hb - 1, 0), rx_blk + g))
    ry = pl.BlockSpec((tt, gw), lambda g, t: (tile(t), ry_blk + g))
    cw = pl.BlockSpec((4, gw), lambda g, t: (0, g))
    vec = pl.BlockSpec((1, gw), lambda g, t: (0, g))
    wg = pl.BlockSpec((None, gw, gw), lambda g, t: (g, 0, 0))
    act = pl.BlockSpec((tt, gw), lambda g, t: (tile(t), g))
    act_halo = pl.BlockSpec((8, gw), lambda g, t: (jnp.maximum(tile(t) * hb - 1, 0), g))
    return rx, rx_halo, ry, cw, vec, wg, act, act_halo, tile


def _rnn_fwd(proj, cols, conv_w, conv_b, wa_g, wi_g, ba, bi, lam, rider=None):
    T = proj.shape[0]
    G, gw, _ = wa_g.shape
    d_rnn = G * gw
    tt = _rnn_tile(T)
    rx_off, ry_off = cols
    rx, rx_halo, ry, cw, vec, wg, act, _, _ = _rnn_specs(T, gw, tt, rx_off // gw, ry_off // gw, False)

    def body(rx_ref, rxh_ref, ry_ref, cw_ref, cb_ref, wa_ref, wi_ref, ba_ref, bi_ref, lam_ref,
             b_ref, h_ref, carry):
        t = pl.program_id(1)

        @pl.when(t == 0)
        def _():
            carry[...] = jnp.zeros_like(carry)

        halo = jnp.where(t > 0, rxh_ref[...], 0.0)
        x_ext = jnp.concatenate([halo, rx_ref[...]], axis=0)
        _, cx, _, i, _, a, s = _rnn_gates(x_ext, cw_ref, cb_ref, wa_ref, wi_ref, ba_ref, bi_ref, lam_ref, tt)
        acc_a, acc_b = a, s * (i * cx)
        d = 1
        while d < tt:
            acc_b = acc_a * _shift_down(acc_b, d, 0.0) + acc_b
            acc_a = acc_a * _shift_down(acc_a, d, 1.0)
            d *= 2
        h = acc_b + acc_a * carry[7:8, :]
        carry[...] = h[tt - 8:, :]
        h_ref[...] = h
        b_ref[...] = (h * _gelu(ry_ref[...])).astype(BF16)

    return _call(
        body, name="rnn_fwd",
        out_shape=(jax.ShapeDtypeStruct((T, d_rnn), BF16), jax.ShapeDtypeStruct((T, d_rnn), F32)),
        grid=(G, T // tt),
        in_specs=[rx, rx_halo, ry, cw, vec, wg, wg, vec, vec, vec], out_specs=(act, act),
        scratch_shapes=[pltpu.VMEM((8, gw), F32)],
        args=(proj, proj, proj, conv_w, conv_b, wa_g, wi_g, ba, bi, lam), sem=("parallel", "arbitrary"), rider=rider)


def _rnn_bwd(proj, cols, h_all, d_b, conv_w, conv_b, wa_g, wi_g, ba, bi, lam, rider=None):
    T = proj.shape[0]
    G, gw, _ = wa_g.shape
    d_rnn = G * gw
    tt = _rnn_tile(T)
    nT = T // tt
    rx_off, ry_off = cols
    rx, rx_halo, ry, cw, vec, wg, act, act_halo, _ = _rnn_specs(T, gw, tt, rx_off // gw, ry_off // gw, True)
    dn_t = (((1,), (1,)), ((), ()))
    dn_r = (((0,), (0,)), ((), ()))

    def body(rx_ref, rxh_ref, ry_ref, h_ref, hh_ref, db_ref, cw_ref, cb_ref, wa_ref, wi_ref, ba_ref, bi_ref, lam_ref,
             drx_ref, dry_ref, dcw_ref, dcb_ref, dba_ref, dbi_ref, dlam_ref, dwa_ref, dwi_ref,
             lam_carry, dcx_carry):
        t = pl.program_id(1)
        first_tile = t == nT - 1

        @pl.when(t == 0)
        def _():
            lam_carry[...] = jnp.zeros_like(lam_carry)
            dcx_carry[...] = jnp.zeros_like(dcx_carry)
            dcw_ref[...] = jnp.zeros_like(dcw_ref)
            dcb_ref[...] = jnp.zeros_like(dcb_ref)
            dba_ref[...] = jnp.zeros_like(dba_ref)
            dbi_ref[...] = jnp.zeros_like(dbi_ref)
            dlam_ref[...] = jnp.zeros_like(dlam_ref)
            dwa_ref[...] = jnp.zeros_like(dwa_ref)
            dwi_ref[...] = jnp.zeros_like(dwi_ref)

        halo = jnp.where(first_tile, 0.0, rxh_ref[...])
        x_ext = jnp.concatenate([halo, rx_ref[...]], axis=0)
        xs, cx, r, i, sp, a, s = _rnn_gates(x_ext, cw_ref, cb_ref, wa_ref, wi_ref, ba_ref, bi_ref, lam_ref, tt)
        h = h_ref[...]
        h_halo = jnp.where(first_tile, 0.0, hh_ref[...])
        h_prev = pltpu.roll(jnp.concatenate([h_halo, h], axis=0), 1, 0)[8:, :]
        gel, dgel = _gelu_and_grad(ry_ref[...])
        d_b_t = db_ref[...]
        dry_ref[...] = (d_b_t * h * dgel).astype(BF16)
        dh = d_b_t * gel

        acc_c = _shift_up(a, 1, 1.0)
        acc_l = dh
        d = 1
        while d < tt:
            acc_l = acc_c * _shift_up(acc_l, d, 0.0) + acc_l
            acc_c = acc_c * _shift_up(acc_c, d, 1.0)
            d *= 2
        lam_t = acc_l + acc_c * lam_carry[0:1, :]
        lam_carry[...] = (a * lam_t)[0:8, :]

        icx = i * cx
        d_s = lam_t * icx
        d_i = lam_t * s * cx
        dcx = lam_t * s * i
        d_a = lam_t * h_prev - d_s * (a / s)
        dlog_a = d_a * a
        d_r = dlog_a * (-LRU_C * sp)
        lam = lam_ref[...]
        dlam_ref[...] += jnp.sum(dlog_a * r, axis=0, keepdims=True) * (LRU_C * jax.nn.sigmoid(-lam))
        dpr = d_r * r * (1.0 - r)
        dpi = d_i * i * (1.0 - i)
        dba_ref[...] += jnp.sum(dpr, axis=0, keepdims=True)
        dbi_ref[...] += jnp.sum(dpi, axis=0, keepdims=True)
        cxb = cx.astype(BF16)
        dprb, dpib = dpr.astype(BF16), dpi.astype(BF16)
        dwa_ref[...] += lax.dot_general(cxb, dprb, dn_r, preferred_element_type=F32)
        dwi_ref[...] += lax.dot_general(cxb, dpib, dn_r, preferred_element_type=F32)
        dcx = (dcx + lax.dot_general(dprb, wa_ref[...], dn_t, preferred_element_type=F32)
               + lax.dot_general(dpib, wi_ref[...], dn_t, preferred_element_type=F32))

        dcb_ref[...] += jnp.sum(dcx, axis=0, keepdims=True)
        for k in range(4):
            dcw_ref[k:k + 1, :] += jnp.sum(dcx * xs[k], axis=0, keepdims=True)
        d_ext = jnp.concatenate([dcx, dcx_carry[...]], axis=0)
        drx = dcx * cw_ref[3:4, :]
        for k in range(3):
            drx = drx + pltpu.roll(d_ext, tt + 8 - (3 - k), 0)[:tt, :] * cw_ref[k:k + 1, :]
        drx_ref[...] = drx.astype(BF16)
        dcx_carry[...] = dcx[0:8, :]

    out_shape = (jax.ShapeDtypeStruct((T, d_rnn), BF16), jax.ShapeDtypeStruct((T, d_rnn), BF16),
                 jax.ShapeDtypeStruct((4, d_rnn), F32), jax.ShapeDtypeStruct((1, d_rnn), F32),
                 jax.ShapeDtypeStruct((1, d_rnn), F32), jax.ShapeDtypeStruct((1, d_rnn), F32),
                 jax.ShapeDtypeStruct((1, d_rnn), F32), jax.ShapeDtypeStruct((G, gw, gw), F32),
                 jax.ShapeDtypeStruct((G, gw, gw), F32))
    return _call(
        body, name="rnn_bwd", out_shape=out_shape, grid=(G, nT),
        in_specs=[rx, rx_halo, ry, act, act_halo, act, cw, vec, wg, wg, vec, vec, vec],
        out_specs=(act, act, cw, vec, vec, vec, vec, wg, wg),
        scratch_shapes=[pltpu.VMEM((8, gw), F32), pltpu.VMEM((8, gw), F32)],
        args=(proj, proj, proj, h_all, h_all, d_b, conv_w, conv_b, wa_g, wi_g, ba, bi, lam),
        sem=("parallel", "arbitrary"), rider=rider)


def _merge_fwd(proj, gl_off, b_gate, y_attn, y_rnn, rider=None):
    T, D = y_attn.shape
    tm = _pick(T, (256, 128))
    ct = _pick(math.gcd(gl_off, D), (512, 256, 128))
    oa, orr, nd = gl_off // ct, (gl_off + D) // ct, D // ct

    def body(ga_ref, gr_ref, ba_ref, br_ref, ya_ref, yr_ref, m_ref):
        ga = jax.nn.sigmoid(ga_ref[...] + ba_ref[...])
        gr = jax.nn.sigmoid(gr_ref[...] + br_ref[...])
        m_ref[...] = (ga * ya_ref[...] + gr * yr_ref[...]).astype(BF16)

    blk = pl.BlockSpec((tm, ct), lambda i, j: (i, j))
    (merged,), carried = _call(
        body, name="merge_fwd", out_shape=[jax.ShapeDtypeStruct((T, D), BF16)], grid=(T // tm, nd),
        in_specs=[pl.BlockSpec((tm, ct), lambda i, j: (i, oa + j)), pl.BlockSpec((tm, ct), lambda i, j: (i, orr + j)),
                  pl.BlockSpec((1, ct), lambda i, j: (0, j)), pl.BlockSpec((1, ct), lambda i, j: (0, nd + j)),
                  blk, blk],
        out_specs=[blk], scratch_shapes=[], args=(proj, proj, b_gate, b_gate, y_attn, y_rnn),
        sem=("parallel", "parallel"), rider=rider)
    return merged, carried


def _merge_bwd(proj, gl_off, b_gate, y_attn, y_rnn, d_m):
    T, D = y_attn.shape
    tm = _pick(T, (256, 128))
    ct = _pick(math.gcd(gl_off, D), (512, 256, 128))
    oa, orr, nd = gl_off // ct, (gl_off + D) // ct, D // ct

    def body(ga_ref, gr_ref, ba_ref, br_ref, ya_ref, yr_ref, dm_ref,
             dya_ref, dyr_ref, dga_ref, dgr_ref, dba_ref, dbr_ref):
        i = pl.program_id(1)

        @pl.when(i == 0)
        def _():
            dba_ref[...] = jnp.zeros_like(dba_ref)
            dbr_ref[...] = jnp.zeros_like(dbr_ref)

        ga = jax.nn.sigmoid(ga_ref[...] + ba_ref[...])
        gr = jax.nn.sigmoid(gr_ref[...] + br_ref[...])
        dm = dm_ref[...]
        dya_ref[...] = (dm * ga).astype(BF16)
        dyr_ref[...] = (dm * gr).astype(BF16)
        dga = dm * ya_ref[...] * ga * (1.0 - ga)
        dgr = dm * yr_ref[...] * gr * (1.0 - gr)
        dga_ref[...] = dga.astype(BF16)
        dgr_ref[...] = dgr.astype(BF16)
        dba_ref[...] += jnp.sum(dga, axis=0, keepdims=True)
        dbr_ref[...] += jnp.sum(dgr, axis=0, keepdims=True)

    blk = pl.BlockSpec((tm, ct), lambda j, i: (i, j))
    vec = pl.BlockSpec((1, ct), lambda j, i: (0, j))
    act = jax.ShapeDtypeStruct((T, D), BF16)
    v1 = jax.ShapeDtypeStruct((1, D), F32)
    return pl.pallas_call(
        body, name="merge_bwd", out_shape=(act, act, act, act, v1, v1), grid=(nd, T // tm),
        in_specs=[pl.BlockSpec((tm, ct), lambda j, i: (i, oa + j)), pl.BlockSpec((tm, ct), lambda j, i: (i, orr + j)),
                  vec, pl.BlockSpec((1, ct), lambda j, i: (0, nd + j)), blk, blk, blk],
        out_specs=(blk, blk, blk, blk, vec, vec),
        compiler_params=_cparams(("parallel", "arbitrary")),
    )(proj, proj, b_gate, b_gate, y_attn, y_rnn, d_m)


def _ln_fwd(x_res, delta, g, b, name, rider=None):
    T, D = x_res.shape
    tm = _pick(T, (256, 128))

    def body(x_ref, d_ref, g_ref, b_ref, y_ref, yb_ref, xh_ref, rs_ref):
        z = ALPHA * x_ref[...] + d_ref[...]
        mu = jnp.mean(z, axis=1, keepdims=True)
        zc = z - mu
        var = jnp.mean(zc * zc, axis=1, keepdims=True)
        rstd = lax.rsqrt(var + LN_EPS)
        xh = zc * rstd
        xh_ref[...] = xh
        rs_ref[...] = rstd
        y = xh * g_ref[...] + b_ref[...]
        y_ref[...] = y
        yb_ref[...] = y.astype(BF16)

    row = pl.BlockSpec((tm, D), lambda i: (i, 0))
    vec = pl.BlockSpec((1, D), lambda i: (0, 0))
    return _call(
        body, name=name,
        out_shape=(jax.ShapeDtypeStruct((T, D), F32), jax.ShapeDtypeStruct((T, D), BF16),
                   jax.ShapeDtypeStruct((T, D), F32), jax.ShapeDtypeStruct((T, 1), F32)),
        grid=(T // tm,), in_specs=[row, row, vec, vec],
        out_specs=(row, row, row, pl.BlockSpec((tm, 1), lambda i: (i, 0))),
        scratch_shapes=[], args=(x_res, delta, g, b), sem=("parallel",), rider=rider)


def _ln_bwd_rows(dy, xh, rstd, g):
    dxh = dy * g
    m1 = jnp.mean(dxh, axis=1, keepdims=True)
    m2 = jnp.mean(dxh * xh, axis=1, keepdims=True)
    return rstd * (dxh - m1 - xh * m2)


def _ln_loss_bwd(x_res, delta, g, b, target):
    T, D = x_res.shape
    tm = _pick(T, (256, 128))

    def body(x_ref, d_ref, g_ref, b_ref, t_ref, dz_ref, dzb_ref, loss_ref, dg_ref, db_ref):
        i = pl.program_id(0)

        @pl.when(i == 0)
        def _():
            loss_ref[...] = jnp.zeros_like(loss_ref)
            dg_ref[...] = jnp.zeros_like(dg_ref)
            db_ref[...] = jnp.zeros_like(db_ref)

        z = ALPHA * x_ref[...] + d_ref[...]
        mu = jnp.mean(z, axis=1, keepdims=True)
        zc = z - mu
        var = jnp.mean(zc * zc, axis=1, keepdims=True)
        rstd = lax.rsqrt(var + LN_EPS)
        xh = zc * rstd
        gv = g_ref[...]
        err = xh * gv + b_ref[...] - t_ref[...]
        loss_ref[...] += 0.5 * jnp.sum(jnp.mean(err * err, axis=1, keepdims=True))
        dy = err * (1.0 / D)
        dg_ref[...] += jnp.sum(dy * xh, axis=0, keepdims=True)
        db_ref[...] += jnp.sum(dy, axis=0, keepdims=True)
        dz = _ln_bwd_rows(dy, xh, rstd, gv)
        dz_ref[...] = dz
        dzb_ref[...] = dz.astype(BF16)

    row = pl.BlockSpec((tm, D), lambda i: (i, 0))
    vec = pl.BlockSpec((1, D), lambda i: (0, 0))
    return pl.pallas_call(
        body, name="ln2_loss_bwd",
        out_shape=(jax.ShapeDtypeStruct((T, D), F32), jax.ShapeDtypeStruct((T, D), BF16),
                   jax.ShapeDtypeStruct((8, LANES), F32),
                   jax.ShapeDtypeStruct((1, D), F32), jax.ShapeDtypeStruct((1, D), F32)),
        grid=(T // tm,), in_specs=[row, row, vec, vec, row],
        out_specs=(row, row, pl.BlockSpec((8, LANES), lambda i: (0, 0)), vec, vec),
        compiler_params=_cparams(("arbitrary",)),
    )(x_res, delta, g, b, target)


def _ln_bwd(dy, xh, rstd, g):
    T, D = dy.shape
    tm = _pick(T, (256, 128))

    def body(dy_ref, xh_ref, rs_ref, g_ref, dz_ref, dzb_ref, dg_ref, db_ref):
        i = pl.program_id(0)

        @pl.when(i == 0)
        def _():
            dg_ref[...] = jnp.zeros_like(dg_ref)
            db_ref[...] = jnp.zeros_like(db_ref)

        dyv, xhv = dy_ref[...], xh_ref[...]
        dg_ref[...] += jnp.sum(dyv * xhv, axis=0, keepdims=True)
        db_ref[...] += jnp.sum(dyv, axis=0, keepdims=True)
        dz = _ln_bwd_rows(dyv, xhv, rs_ref[...], g_ref[...])
        dz_ref[...] = dz
        dzb_ref[...] = dz.astype(BF16)

    row = pl.BlockSpec((tm, D), lambda i: (i, 0))
    vec = pl.BlockSpec((1, D), lambda i: (0, 0))
    return pl.pallas_call(
        body, name="ln1_bwd",
        out_shape=(jax.ShapeDtypeStruct((T, D), F32), jax.ShapeDtypeStruct((T, D), BF16),
                   jax.ShapeDtypeStruct((1, D), F32), jax.ShapeDtypeStruct((1, D), F32)),
        grid=(T // tm,), in_specs=[row, row, pl.BlockSpec((tm, 1), lambda i: (i, 0)), vec],
        out_specs=(row, row, vec, vec), compiler_params=_cparams(("arbitrary",)),
    )(dy, xh, rstd, g)


def _ffn_col_tile(T, d_ff):
    return _pick(d_ff, (256, 128)) if T >= 1024 else _pick(d_ff, (512, 256, 128))


def _ffn_gate(gp, cw_ref, cb_ref):
    return (cb_ref[...] + gp * cw_ref[2:3, :] + _shift_down(gp, 1) * cw_ref[1:2, :]
            + _shift_down(gp, 2) * cw_ref[0:1, :])


def _ffn_fwd(up, gpre, conv_w, conv_b, rider=None):
    T, d_ff = up.shape
    ct = _ffn_col_tile(T, d_ff)

    def body(up_ref, gp_ref, cw_ref, cb_ref, f_ref):
        gate = _ffn_gate(gp_ref[...], cw_ref, cb_ref)
        f_ref[...] = (_gelu(gate) * up_ref[...]).astype(BF16)

    col = pl.BlockSpec((T, ct), lambda j: (0, j))
    (f,), carried = _call(
        body, name="ffn_act_fwd", out_shape=[jax.ShapeDtypeStruct((T, d_ff), BF16)], grid=(d_ff // ct,),
        in_specs=[col, col, pl.BlockSpec((3, ct), lambda j: (0, j)), pl.BlockSpec((1, ct), lambda j: (0, j))],
        out_specs=[col], scratch_shapes=[], args=(up, gpre, conv_w, conv_b), sem=("parallel",), rider=rider)
    return f, carried


def _ffn_bwd(up, gpre, conv_w, conv_b, d_f, after=None):
    T, d_ff = up.shape
    ct = _ffn_col_tile(T, d_ff)

    def body(up_ref, gp_ref, cw_ref, cb_ref, df_ref, dup_ref, dgp_ref, dcw_ref, dcb_ref):
        gp = gp_ref[...]
        gate = _ffn_gate(gp, cw_ref, cb_ref)
        gel, dgel = _gelu_and_grad(gate)
        df = df_ref[...]
        dup_ref[...] = (df * gel).astype(BF16)
        dgate = df * up_ref[...] * dgel
        dcb_ref[...] = jnp.sum(dgate, axis=0, keepdims=True)
        dcw_ref[2:3, :] = jnp.sum(dgate * gp, axis=0, keepdims=True)
        dcw_ref[1:2, :] = jnp.sum(dgate * _shift_down(gp, 1), axis=0, keepdims=True)
        dcw_ref[0:1, :] = jnp.sum(dgate * _shift_down(gp, 2), axis=0, keepdims=True)
        dgp = (dgate * cw_ref[2:3, :] + _shift_up(dgate, 1) * cw_ref[1:2, :]
               + _shift_up(dgate, 2) * cw_ref[0:1, :])
        dgp_ref[...] = dgp.astype(BF16)

    col = pl.BlockSpec((T, ct), lambda j: (0, j))
    w3 = pl.BlockSpec((3, ct), lambda j: (0, j))
    v1 = pl.BlockSpec((1, ct), lambda j: (0, j))
    return _call(
        body, name="ffn_act_bwd",
        out_shape=(jax.ShapeDtypeStruct((T, d_ff), BF16), jax.ShapeDtypeStruct((T, d_ff), BF16),
                   jax.ShapeDtypeStruct((3, d_ff), F32), jax.ShapeDtypeStruct((1, d_ff), F32)),
        grid=(d_ff // ct,), in_specs=[col, col, w3, v1, col], out_specs=(col, col, w3, v1),
        scratch_shapes=[], args=(up, gpre, conv_w, conv_b, d_f), sem=("parallel",), after=after)[0]


def _adamw(w, g, m, v, name, after=None):
    R, C = w.shape
    tr = _row_tile(R, C * 4, 8, budget=1280 * 1024)
    c1 = 1.0 / (1.0 - ADAM_B1 ** ADAM_STEP)
    c2 = 1.0 / (1.0 - ADAM_B2 ** ADAM_STEP)

    def body(w_ref, g_ref, m_ref, v_ref, go_ref, d_ref, nm_ref, nv_ref):
        gv = g_ref[...]
        go_ref[...] = gv
        nm = ADAM_B1 * m_ref[...] + (1.0 - ADAM_B1) * gv
        nv = ADAM_B2 * v_ref[...] + (1.0 - ADAM_B2) * (gv * gv)
        nm_ref[...] = nm
        nv_ref[...] = nv
        d_ref[...] = -ADAM_LR * ((nm * c1) / (jnp.sqrt(nv * c2) + ADAM_EPS) + ADAM_WD * w_ref[...])

    blk = pl.BlockSpec((tr, C), lambda r: (r, 0))
    sh = jax.ShapeDtypeStruct((R, C), F32)
    return _call(body, name=name, out_shape=(sh,) * 4, grid=(R // tr,), in_specs=[blk] * 4, out_specs=(blk,) * 4,
                 scratch_shapes=[], args=(w, g, m, v), sem=("parallel",), after=after)[0]


def _group_blocks(w_blocks, per):
    nb, bw, _ = w_blocks.shape
    G = nb // per
    w4 = w_blocks.reshape(G, per, bw, bw)
    rows = []
    for p in range(per):
        parts = [w4[:, p] if q == p else jnp.zeros((G, bw, bw), w_blocks.dtype) for q in range(per)]
        rows.append(jnp.concatenate(parts, axis=2))
    return jnp.concatenate(rows, axis=1)


def _ungroup_blocks(w_groups, per):
    G, gw, _ = w_groups.shape
    bw = gw // per
    blocks = [w_groups[:, p * bw:(p + 1) * bw, p * bw:(p + 1) * bw] for p in range(per)]
    return jnp.stack(blocks, axis=1).reshape(G * per, bw, bw)


def _pack(parts):
    flat = jnp.concatenate([p.reshape(-1).astype(F32) for p in parts])
    n = flat.shape[0]
    rows = -(-n // LANES)
    rows = -(-rows // PACK_ROW_MULT) * PACK_ROW_MULT
    flat = jnp.pad(flat, (0, rows * LANES - n))
    return flat.reshape(rows, LANES)


def _unpack(packed, shapes):
    flat = packed.reshape(-1)
    out, off = [], 0
    for s in shapes:
        n = math.prod(s)
        out.append(flat[off:off + n].reshape(s))
        off += n
    return out


def kernel(x, w_in, b_gate, rnn_conv_w, rnn_conv_b, lru_wa, lru_ba, lru_wi, lru_bi, lru_lambda, attn_sinks, w_attn_proj, w_rnn_proj, w_out, ln1_g, ln1_b, ffn_w_up, ffn_w_gate, ffn_conv_w, ffn_conv_b, ffn_w_down, ln2_g, ln2_b, loss_target, m_w_in, m_b_gate, m_rnn_conv_w, m_rnn_conv_b, m_lru_wa, m_lru_ba, m_lru_wi, m_lru_bi, m_lru_lambda, m_attn_sinks, m_w_attn_proj, m_w_rnn_proj, m_w_out, m_ln1_g, m_ln1_b, m_ffn_w_up, m_ffn_w_gate, m_ffn_conv_w, m_ffn_conv_b, m_ffn_w_down, m_ln2_g, m_ln2_b, v_w_in, v_b_gate, v_rnn_conv_w, v_rnn_conv_b, v_lru_wa, v_lru_ba, v_lru_wi, v_lru_bi, v_lru_lambda, v_attn_sinks, v_w_attn_proj, v_w_rnn_proj, v_w_out, v_ln1_g, v_ln1_b, v_ffn_w_up, v_ffn_w_gate, v_ffn_conv_w, v_ffn_conv_b, v_ffn_w_down, v_ln2_g, v_ln2_b):
    weights = dict(w_in=w_in, b_gate=b_gate, rnn_conv_w=rnn_conv_w, rnn_conv_b=rnn_conv_b, lru_wa=lru_wa,
                   lru_ba=lru_ba, lru_wi=lru_wi, lru_bi=lru_bi, lru_lambda=lru_lambda, attn_sinks=attn_sinks,
                   w_attn_proj=w_attn_proj, w_rnn_proj=w_rnn_proj, w_out=w_out, ln1_g=ln1_g, ln1_b=ln1_b,
                   ffn_w_up=ffn_w_up, ffn_w_gate=ffn_w_gate, ffn_conv_w=ffn_conv_w, ffn_conv_b=ffn_conv_b,
                   ffn_w_down=ffn_w_down, ln2_g=ln2_g, ln2_b=ln2_b)
    m_in = dict(w_in=m_w_in, b_gate=m_b_gate, rnn_conv_w=m_rnn_conv_w, rnn_conv_b=m_rnn_conv_b, lru_wa=m_lru_wa,
                lru_ba=m_lru_ba, lru_wi=m_lru_wi, lru_bi=m_lru_bi, lru_lambda=m_lru_lambda, attn_sinks=m_attn_sinks,
                w_attn_proj=m_w_attn_proj, w_rnn_proj=m_w_rnn_proj, w_out=m_w_out, ln1_g=m_ln1_g, ln1_b=m_ln1_b,
                ffn_w_up=m_ffn_w_up, ffn_w_gate=m_ffn_w_gate, ffn_conv_w=m_ffn_conv_w, ffn_conv_b=m_ffn_conv_b,
                ffn_w_down=m_ffn_w_down, ln2_g=m_ln2_g, ln2_b=m_ln2_b)
    v_in = dict(w_in=v_w_in, b_gate=v_b_gate, rnn_conv_w=v_rnn_conv_w, rnn_conv_b=v_rnn_conv_b, lru_wa=v_lru_wa,
                lru_ba=v_lru_ba, lru_wi=v_lru_wi, lru_bi=v_lru_bi, lru_lambda=v_lru_lambda, attn_sinks=v_attn_sinks,
                w_attn_proj=v_w_attn_proj, w_rnn_proj=v_w_rnn_proj, w_out=v_w_out, ln1_g=v_ln1_g, ln1_b=v_ln1_b,
                ffn_w_up=v_ffn_w_up, ffn_w_gate=v_ffn_w_gate, ffn_conv_w=v_ffn_conv_w, ffn_conv_b=v_ffn_conv_b,
                ffn_w_down=v_ffn_w_down, ln2_g=v_ln2_g, ln2_b=v_ln2_b)
    order = list(weights)

    assert x.shape[0] == 1 and w_in.shape[0] == 1, "one sequence per device, depth 1"
    T, D = x.shape[1], x.shape[2]
    nq = attn_sinks.shape[-1]
    nkv = nq // GROUP
    d_attn, d_kv = nq * HEAD_DIM, nkv * HEAD_DIM
    d_rnn = rnn_conv_b.shape[-1]
    d_ff = ffn_conv_b.shape[-1]
    n_blocks, bw = lru_wa.shape[1], lru_wa.shape[2]
    per = (bw * LANES // math.gcd(bw, LANES)) // bw
    gw = per * bw
    assert n_blocks % per == 0 and d_rnn == n_blocks * bw
    q_off, k_off, v_off = 0, d_attn, d_attn + d_kv
    rx_off = d_attn + 2 * d_kv
    ry_off = rx_off + d_rnn
    gl_off = ry_off + d_rnn
    d_in = gl_off + 2 * D
    assert w_in.shape[-1] * N_SHARDS == d_in
    assert k_off % d_kv == 0 and rx_off % gw == 0 and T % ATTN_BLOCK == 0

    xi, yi, ci = lax.axis_index("x"), lax.axis_index("y"), lax.axis_index("c")
    j_me = 2 * xi + yi
    jc_arr = jnp.stack([j_me, ci]).astype(jnp.int32)

    x0 = x[0]
    x0b = _cast_bf16(x0, "cast_x")
    tgt = loss_target[0]
    big = ["w_in", "w_attn_proj", "w_rnn_proj", "w_out", "ffn_w_up", "ffn_w_gate", "ffn_w_down"]
    own = {n: _cast_bf16_into_slot(weights[n][0], jc_arr, "cast_" + n) for n in big}
    order_arr = jnp.stack([j_me, j_me ^ 2, j_me ^ 1, j_me ^ 3]).astype(jnp.int32)

    rcw_s, fcw_s = _all_gather_small([rnn_conv_w[0], ffn_conv_w[0]])
    rcw = jnp.concatenate([rcw_s[j] for j in range(N_SHARDS)], axis=1)
    fcw = jnp.concatenate([fcw_s[j] for j in range(N_SHARDS)], axis=1)

    wa_g = _group_blocks(lru_wa[0], per).astype(BF16)
    wi_g = _group_blocks(lru_wi[0], per).astype(BF16)

    near, diag = (0, 1), (2,)
    proj, w_in_s = _mm_gathering(x0b, own["w_in"], order_arr, "mm_proj", after=fcw_s)
    ici, last = {}, proj
    for n in big[1:]:
        ici[n] = _gather_step(own[n], last, "gather_start_" + n, start_d2d=False)
        last = ici[n][2]

    def forward_halves(n, after):
        sems, buf, _ = ici[n]
        return _gather_step(buf, after, "gather_forward_" + n, sems_in=sems, start_d2d=True)

    def gathered(d2d, after, n):
        sems, buf, _ = d2d
        return _gather_step(buf, after, "gather_finish_" + n, sems_in=sems)

    a_out = _attn_fwd(proj, attn_sinks, nq, (q_off, k_off, v_off), after=last)
    fw_ap = forward_halves("w_attn_proj", a_out)
    (b_out, h_all), _ = _rnn_fwd(proj, (rx_off, ry_off), rcw, rnn_conv_b, wa_g, wi_g, lru_ba, lru_bi, lru_lambda)
    fw_rp = forward_halves("w_rnn_proj", b_out)
    w_ap = gathered(fw_ap, b_out, "w_attn_proj").reshape(d_attn, D)
    y_attn = _mm(a_out, w_ap, name="mm_attn_proj")
    fw_o = forward_halves("w_out", y_attn)
    w_rp = gathered(fw_rp, y_attn, "w_rnn_proj").reshape(d_rnn, D)
    y_rnn = _mm(b_out, w_rp, name="mm_rnn_proj")
    merged, _ = _merge_fwd(proj, gl_off, b_gate, y_attn, y_rnn)
    w_o = gathered(fw_o, merged, "w_out").reshape(D, D)
    mix = _mm(merged, w_o, name="mm_out")
    fw_up = forward_halves("ffn_w_up", mix)
    (x1, x1b, xh1, rstd1), _ = _ln_fwd(x0, mix, ln1_g, ln1_b, "ln1_fwd")
    w_up_s = gathered(fw_up, x1b, "ffn_w_up")
    up = _mm(x1b, w_up_s, name="mm_up", b_shards=N_SHARDS)
    fw_gate = forward_halves("ffn_w_gate", up)
    w_gate_s = gathered(fw_gate, fw_gate[2], "ffn_w_gate")
    gpre = _mm(x1b, w_gate_s, name="mm_gate", b_shards=N_SHARDS)
    f_act, _ = _ffn_fwd(up, gpre, fcw, ffn_conv_b)
    fw_dn = forward_halves("ffn_w_down", f_act)
    w_dn = gathered(fw_dn, fw_dn[2], "ffn_w_down").reshape(d_ff, D)
    f_out = _mm(f_act, w_dn, name="mm_down")
    dz2, dz2b, loss_acc, dg2, db2 = _ln_loss_bwd(x1, f_out, ln2_g, ln2_b, tgt)

    def pair_sums(arrs, from_sibling, names):
        return [_pair_sum(g, la, jc_arr, "pair_sum_" + n) for g, la, n in zip(arrs, from_sibling, names)]

    def shard_sums(parts, landed, names):
        return [_shard_sum(cp, lb, jc_arr, "shard_sum_" + n) for cp, lb, n in zip(parts, landed, names)]

    halves = {}
    g_down = _mm(f_act, dz2b, name="mm_d_w_down", ta=True, out_dtype=BF16)
    g1 = [g_down.reshape(N_SHARDS, d_ff // N_SHARDS, D)]
    d_f, sib1 = _mm(dz2b, w_dn, name="mm_d_f", tb=True, rider=_pair_rider(g1))
    sent1 = _shard_exchange_start(pair_sums(g1, sib1, ["ffn_w_down"]), "shard_exchange_start_down")
    dup, dgp, d_fcw, d_fcb = _ffn_bwd(up, gpre, fcw, ffn_conv_b, d_f, after=sent1[4])
    g_up = _mm(x1b, dup, name="mm_d_w_up", ta=True, out_dtype=BF16, out_shards=N_SHARDS)
    g_gate = _mm(x1b, dgp, name="mm_d_w_gate", ta=True, out_dtype=BF16, out_shards=N_SHARDS)
    g2 = [g_up, g_gate]
    dx1_a, sib2 = _mm(dup, w_up_s, name="mm_dx1_up", tb=True, b_shards=N_SHARDS, adds=((ALPHA, dz2),),
                      rider=_pair_rider(g2))
    halves["ffn_w_down"], = shard_sums(*_shard_exchange_wait(sent1, dx1_a, "shard_exchange_wait_down"),
                                       ["ffn_w_down"])
    sent2 = _shard_exchange_start(pair_sums(g2, sib2, ["ffn_w_up", "ffn_w_gate"]), "shard_exchange_start_up_gate")
    dx1 = _mm(dgp, w_gate_s, name="mm_dx1_gate", tb=True, b_shards=N_SHARDS, adds=((1.0, dx1_a),), after=sent2[4])
    dz1, dz1b, dg1, db1 = _ln_bwd(dx1, xh1, rstd1, ln1_g)
    g_out = _mm(merged, dz1b, name="mm_d_w_out", ta=True, out_dtype=BF16)
    d_m = _mm(dz1b, w_o, name="mm_d_merged", tb=True)
    dya, dyr, dgl_a, dgl_r, dbg_a, dbg_r = _merge_bwd(proj, gl_off, b_gate, y_attn, y_rnn, d_m)
    g_ap = _mm(a_out, dya, name="mm_d_w_attn_proj", ta=True, out_dtype=BF16)
    g_rp = _mm(b_out, dyr, name="mm_d_w_rnn_proj", ta=True, out_dtype=BF16)
    names3 = ["w_out", "w_attn_proj", "w_rnn_proj"]
    g3 = [g_out.reshape(N_SHARDS, D // N_SHARDS, D), g_ap.reshape(N_SHARDS, d_attn // N_SHARDS, D),
          g_rp.reshape(N_SHARDS, d_rnn // N_SHARDS, D)]
    d_a = _mm(dya, w_ap, name="mm_d_attn", tb=True)
    d_b, sib3 = _mm(dyr, w_rp, name="mm_d_rnn", tb=True, rider=_pair_rider(g3))
    sent3 = _shard_exchange_start(pair_sums(g3, sib3, names3), "shard_exchange_start_mixers")
    dq, dk, dv, dsink = _attn_bwd(proj, d_a, attn_sinks, nq, (q_off, k_off, v_off), after=sent3[4])
    (drx, dry, d_rcw, d_rcb, d_ba, d_bi, d_lam, d_wa_g, d_wi_g), _ = _rnn_bwd(
        proj, (rx_off, ry_off), h_all, d_b, rcw, rnn_conv_b, wa_g, wi_g, lru_ba, lru_bi, lru_lambda)
    halves["ffn_w_up"], halves["ffn_w_gate"] = shard_sums(
        *_shard_exchange_wait(sent2, drx, "shard_exchange_wait_up_gate"), ["ffn_w_up", "ffn_w_gate"])
    d_proj = jnp.concatenate([dq, dk.astype(BF16), dv.astype(BF16), drx, dry, dgl_a, dgl_r], axis=1)
    ffn_names = ["ffn_w_down", "ffn_w_up", "ffn_w_gate"]
    g_in, shared_ffn = _mm(x0b, d_proj, name="mm_d_w_in", ta=True, out_dtype=BF16, out_shards=N_SHARDS,
                           rider=_share_rider([halves[n] for n in ffn_names]))
    halves["w_out"], halves["w_attn_proj"], halves["w_rnn_proj"] = shard_sums(
        *_shard_exchange_wait(sent3, g_in, "shard_exchange_wait_mixers"), names3)

    small_parts = [
        ("loss", loss_acc[0:1, 0:1]),
        ("b_gate", jnp.concatenate([dbg_a, dbg_r], axis=1)),
        ("rnn_conv_w", d_rcw), ("rnn_conv_b", d_rcb),
        ("lru_wa", _ungroup_blocks(d_wa_g, per)), ("lru_ba", d_ba),
        ("lru_wi", _ungroup_blocks(d_wi_g, per)), ("lru_bi", d_bi), ("lru_lambda", d_lam),
        ("attn_sinks", dsink[0:1, 0:nq]),
        ("ln1_g", dg1), ("ln1_b", db1),
        ("ffn_conv_w", d_fcw), ("ffn_conv_b", d_fcb),
        ("ln2_g", dg2), ("ln2_b", db2),
    ]
    packed = _pack([p for _, p in small_parts])
    rs = packed.shape[0]

    def whole(g):
        return g.reshape(2 * g.shape[1], g.shape[2])

    grads = {n: whole(g) for n, g in zip(ffn_names, shared_ffn)}
    out_g, out_d, out_m, out_v = {}, {}, {}, {}

    def adamw(n, after=None):
        shape = weights[n].shape
        two_d = (math.prod(shape[:-1]), shape[-1])
        g2, d2, m2, v2 = _adamw(weights[n].reshape(two_d), grads[n].reshape(two_d), m_in[n].reshape(two_d),
                                v_in[n].reshape(two_d), "adamw_" + n, after=after)
        out_g[n], out_d[n] = g2.reshape(shape), d2.reshape(shape)
        out_m[n], out_v[n] = m2.reshape(shape), v2.reshape(shape)

    g4 = [g_in, packed.reshape(N_SHARDS, rs // N_SHARDS, LANES)]
    sib4 = _run_rider(_pair_rider(g4), "pair_exchange_in_small")
    part4 = pair_sums(g4, sib4, ["w_in", "small"])
    grad_x, (lb_in, lb_small, *shared_mix) = _mm(
        d_proj, w_in_s, name="mm_d_x", tb=True, b_shards=N_SHARDS, adds=((ALPHA, dz1),),
        rider=_join_riders(_shard_exchange_rider(part4, _atoms([0], near) + _atoms([1])),
                           _share_rider([halves[n] for n in names3])))
    grads.update({n: whole(g) for n, g in zip(names3, shared_mix)})
    sent5 = _shard_exchange_start(part4[:1], "shard_exchange_start_in_diag", relations=diag, lands=[lb_in])
    for n in ffn_names + names3:
        adamw(n, after=sent5[4])
    (part_in,), (lb_in,) = _shard_exchange_wait(sent5, out_d[names3[-1]], "shard_exchange_wait_in_diag")
    part_small = part4[1]
    halves["w_in"], = shard_sums([part_in], [lb_in], ["w_in"])
    eighths = _shard_sum(part_small, lb_small, jc_arr, "shard_sum_small", all_slots=True)
    shared_in, reduced = _run_rider(_share_rider([halves["w_in"]], eighths), "share_in_small")
    grads["w_in"] = whole(shared_in)
    reduced = reduced.reshape(rs, LANES)
    small = dict(zip([n for n, _ in small_parts], _unpack(reduced, [p.shape for _, p in small_parts])))
    loss = small.pop("loss").reshape(())
    rcw_n = d_rnn // N_SHARDS
    fcw_n = d_ff // N_SHARDS
    small["rnn_conv_w"] = lax.dynamic_slice(small["rnn_conv_w"], (0, j_me * rcw_n), (4, rcw_n))
    small["ffn_conv_w"] = lax.dynamic_slice(small["ffn_conv_w"], (0, j_me * fcw_n), (3, fcw_n))
    for n, g in small.items():
        grads[n] = g

    for n in order:
        if n not in out_g:
            adamw(n)

    return (loss, grad_x.reshape(x.shape), *[out_g[n] for n in order], *[out_d[n] for n in order],
            *[out_m[n] for n in order], *[out_v[n] for n in order])
```

```python
import functools
import math

import jax
import jax.numpy as jnp
from jax import lax
from jax.experimental import pallas as pl
from jax.experimental.pallas import tpu as pltpu

F32 = jnp.float32
BF16 = jnp.bfloat16
MESH = pl.DeviceIdType.MESH

HEAD_DIM = 64
GROUP = 8
ATTN_BLOCK = 128
LRU_C = 8.0
LN_EPS = 1e-5
ALPHA = 2.0 ** 0.25
LANES = 128
N_SHARDS = 4
N_DEV = 8
VMEM_LIMIT = 56 * 1024 * 1024
MM_VMEM_BUDGET = 40 * 1024 * 1024
MM_MAX_TILE = 3072
PACK_ROW_MULT = 8 * 64
NEG = -1e30

ADAM_LR, ADAM_B1, ADAM_B2, ADAM_EPS, ADAM_WD, ADAM_STEP = 0.001, 0.9, 0.999, 1e-08, 0.01, 10

GELU_C = math.sqrt(2.0 / math.pi)
GELU_A = 0.044715


def _cparams(sem=None):
    kw = dict(vmem_limit_bytes=VMEM_LIMIT)
    if sem is not None:
        kw["dimension_semantics"] = sem
    return pltpu.CompilerParams(**kw)


def _pick(n, prefs):
    for p in prefs:
        if n % p == 0:
            return p
    return n


def _row_tile(rows, row_bytes, mult, budget=2 * 1024 * 1024):
    best = None
    for d in range(mult, rows + 1, mult):
        if rows % d == 0 and d * row_bytes <= budget:
            best = d
    return best if best is not None else rows


def _gelu(x):
    return 0.5 * x * (1.0 + jnp.tanh(GELU_C * (x + GELU_A * x * x * x)))


def _gelu_and_grad(x):
    t = jnp.tanh(GELU_C * (x + GELU_A * x * x * x))
    g = 0.5 * x * (1.0 + t)
    dg = 0.5 * (1.0 + t) + 0.5 * x * (1.0 - t * t) * GELU_C * (1.0 + 3.0 * GELU_A * x * x)
    return g, dg


def _shift_down(x, s, fill=0.0):
    row = lax.broadcasted_iota(jnp.int32, x.shape, 0)
    return jnp.where(row >= s, pltpu.roll(x, s, 0), fill)


def _shift_up(x, s, fill=0.0):
    n = x.shape[0]
    row = lax.broadcasted_iota(jnp.int32, x.shape, 0)
    return jnp.where(row < n - s, pltpu.roll(x, n - s, 0), fill)


def _mm(a, b, *, name, ta=False, tb=False, out_dtype=F32, adds=(), b_shards=1, out_shards=1,
        tm=None, tn=None, tk=None, rider=None, after=None):
    if ta:
        K, M = a.shape
    else:
        M, K = a.shape
    if b_shards > 1:
        n_sh = b.shape[-1]
        if tb:
            N = b.shape[1]
            assert b_shards * n_sh == K
        else:
            N = b_shards * n_sh
            assert b.shape[1] == K
    else:
        n_sh = None
        if tb:
            N = b.shape[0]
            assert b.shape[1] == K
        else:
            N = b.shape[1]
            assert b.shape[0] == K
    wide = (1024, 1536, 1280, 768, 640, 512, 256, 128)
    if tn is None:
        if b_shards > 1 and not tb:
            tn = n_sh if n_sh <= MM_MAX_TILE else _pick(n_sh, wide)
        elif out_shards > 1:
            tn = N // out_shards if N // out_shards <= MM_MAX_TILE else _pick(N // out_shards, wide)
        else:
            tn = _pick(N, wide)
    if tk is None:
        if b_shards > 1 and tb:
            tk = n_sh if n_sh <= MM_MAX_TILE else _pick(n_sh, wide)
        else:
            tk = K if K <= MM_MAX_TILE else _pick(K, (2048,) + wide)
    assert N % tn == 0 and K % tk == 0, (name, M, N, K, tn, tk)
    nk = K // tk
    n_add = len(adds)
    sa, sb, so = a.dtype.itemsize, b.dtype.itemsize, jnp.dtype(out_dtype).itemsize

    def vmem_bytes(tm_):
        return (2 * (tm_ * tk * sa + tk * tn * sb + tm_ * tn * so + n_add * tm_ * tn * 4)
                + (tm_ * tn * 4 if nk > 1 else 0))

    if tm is None:
        tm = _pick(M, (1024, 512, 256, 128)) if nk > 1 else _pick(M, (512, 256, 128))
        while vmem_bytes(tm) > MM_VMEM_BUDGET and tm % 256 == 0:
            tm //= 2
    assert M % tm == 0, (name, M, tm)
    b_outer = b.size * sb >= a.size * sa

    def ij(g0, g1):
        return (g1, g0) if b_outer else (g0, g1)

    def amap(g0, g1, k):
        i, _ = ij(g0, g1)
        return (k, i) if ta else (i, k)

    def bmap(g0, g1, k):
        _, j = ij(g0, g1)
        if b_shards > 1 and not tb:
            per = n_sh // tn
            return (j // per, k, j % per)
        if b_shards > 1 and tb:
            per = n_sh // tk
            return (k // per, j, k % per)
        return (j, k) if tb else (k, j)

    def omap(g0, g1, k):
        i, j = ij(g0, g1)
        if out_shards > 1:
            per_o = (N // out_shards) // tn
            return (j // per_o, i, j % per_o)
        return (i, j)

    a_spec = pl.BlockSpec((tk, tm) if ta else (tm, tk), amap)
    if b_shards > 1:
        b_spec = pl.BlockSpec((None, tn, tk) if tb else (None, tk, tn), bmap)
    else:
        b_spec = pl.BlockSpec((tn, tk) if tb else (tk, tn), bmap)
    add_specs = [pl.BlockSpec((tm, tn), lambda g0, g1, k: ij(g0, g1)) for _ in adds]
    if out_shards > 1:
        out_spec = pl.BlockSpec((None, tm, tn), omap)
        out_shape = jax.ShapeDtypeStruct((out_shards, M, N // out_shards), out_dtype)
    else:
        out_spec = pl.BlockSpec((tm, tn), omap)
        out_shape = jax.ShapeDtypeStruct((M, N), out_dtype)

    if ta:
        dims = (((0,), (0,)), ((), ()))
    elif tb:
        dims = (((1,), (1,)), ((), ()))
    else:
        dims = (((1,), (0,)), ((), ()))
    scales = tuple(s for s, _ in adds)

    def finish(r, add_refs, o_ref):
        for s, ref in zip(scales, add_refs):
            r = r + s * ref[...].astype(F32)
        o_ref[...] = r.astype(out_dtype)

    def body(a_ref, b_ref, *rest):
        add_refs = rest[:n_add]
        o_ref = rest[n_add]
        part = lax.dot_general(a_ref[...].astype(BF16), b_ref[...].astype(BF16), dims, preferred_element_type=F32)
        if nk == 1:
            finish(part, add_refs, o_ref)
            return
        acc = rest[n_add + 1]
        k = pl.program_id(2)

        @pl.when(k == 0)
        def _():
            acc[...] = part

        @pl.when(k > 0)
        def _():
            acc[...] += part

        @pl.when(k == nk - 1)
        def _():
            finish(acc[...], add_refs, o_ref)

    grid = (N // tn, M // tm, nk) if b_outer else (M // tm, N // tn, nk)
    (res,), carried = _call(
        body, name=name, grid=grid, in_specs=[a_spec, b_spec] + add_specs, out_specs=[out_spec],
        out_shape=[out_shape], scratch_shapes=[pltpu.VMEM((tm, tn), F32)] if nk > 1 else [],
        args=(a, b, *[x for _, x in adds]), sem=("parallel", "parallel", "arbitrary"), rider=rider, after=after)
    return (res, carried) if rider is not None else res


def _cast_bf16(w, name, after):
    R, C = w.shape
    tr = _row_tile(R, C * 4, 16)

    def body(w_ref, after_ref, o_ref):
        o_ref[...] = w_ref[...].astype(BF16)

    return pl.pallas_call(
        body, name=name, out_shape=jax.ShapeDtypeStruct((R, C), BF16), grid=(R // tr,),
        in_specs=[pl.BlockSpec((tr, C), lambda r: (r, 0)), pl.BlockSpec(memory_space=pl.ANY)],
        out_specs=pl.BlockSpec((tr, C), lambda r: (r, 0)), compiler_params=_cparams(("parallel",)),
    )(w, after)


def _cast_bf16_into_slot(w, jc_arr, name, after):
    R, C = w.shape
    tr = _row_tile(R, C * 4, 16)

    def body(jc_ref, w_ref, after_ref, o_ref):
        o_ref[...] = w_ref[...].astype(BF16)

    gs = pltpu.PrefetchScalarGridSpec(
        num_scalar_prefetch=1, grid=(R // tr,),
        in_specs=[pl.BlockSpec((tr, C), lambda r, jc: (r, 0)), pl.BlockSpec(memory_space=pl.ANY)],
        out_specs=pl.BlockSpec((None, tr, C), lambda r, jc: (jc[0], r, 0)))
    return pl.pallas_call(body, name=name, out_shape=jax.ShapeDtypeStruct((N_SHARDS, R, C), BF16), grid_spec=gs,
                          compiler_params=_cparams(("parallel",)))(jc_arr, w, after)


def _pair_sum(g, la, jc_arr, name):
    S, R, C = g.shape
    half = R // 2
    tr = _row_tile(half, C * 4, 16)
    nrt = half // tr
    dt = g.dtype

    def body(jc_ref, g_ref, la_ref, o_ref):
        o_ref[...] = (g_ref[...].astype(F32) + la_ref[...].astype(F32)).astype(dt)

    gs = pltpu.PrefetchScalarGridSpec(
        num_scalar_prefetch=1, grid=(S, nrt),
        in_specs=[pl.BlockSpec((None, tr, C), lambda s, r, jc: (s, jc[1] * nrt + r, 0)),
                  pl.BlockSpec((None, tr, C), lambda s, r, jc: (s, r, 0))],
        out_specs=pl.BlockSpec((None, tr, C), lambda s, r, jc: (s, r, 0)))
    return pl.pallas_call(body, name=name, out_shape=jax.ShapeDtypeStruct((S, half, C), dt), grid_spec=gs,
                          compiler_params=_cparams(("parallel", "parallel")))(jc_arr, g, la)


def _shard_sum(cp, lb, jc_arr, name, all_slots=False):
    S, h, C = cp.shape
    tr = _row_tile(h, C * 4, 16)

    def body(jc_ref, cp_ref, l0, l1, l2, o_ref):
        o_ref[...] = ((cp_ref[...].astype(F32) + l0[...].astype(F32)) + l1[...].astype(F32)) + l2[...].astype(F32)

    def lspec(kk):
        return pl.BlockSpec((None, tr, C), lambda r, jc: (kk, r, 0))

    if all_slots:
        out_spec = pl.BlockSpec((None, None, tr, C), lambda r, jc: (jc[0], jc[1], r, 0))
        out_shape = jax.ShapeDtypeStruct((S, 2, h, C), F32)
    else:
        out_spec = pl.BlockSpec((None, tr, C), lambda r, jc: (jc[1], r, 0))
        out_shape = jax.ShapeDtypeStruct((2, h, C), F32)
    gs = pltpu.PrefetchScalarGridSpec(
        num_scalar_prefetch=1, grid=(h // tr,),
        in_specs=[pl.BlockSpec((None, tr, C), lambda r, jc: (jc[0], r, 0)), lspec(0), lspec(1), lspec(2)],
        out_specs=out_spec)
    return pl.pallas_call(body, name=name, out_shape=out_shape, grid_spec=gs,
                          compiler_params=_cparams(("parallel",)))(jc_arr, cp, lb, lb, lb)


ANY = pl.BlockSpec(memory_space=pl.ANY)


def _place():
    x, y, c = lax.axis_index("x"), lax.axis_index("y"), lax.axis_index("c")
    chips = [(1 - x, y), (x, 1 - y), (1 - x, 1 - y)]
    return x, y, c, chips


class _Rider:
    def __init__(self, inputs, out_shape, aliases, sems, start, finish):
        self.inputs, self.out_shape, self.aliases, self.sems = list(inputs), list(out_shape), dict(aliases), list(sems)
        self.start, self.finish = start, finish


def _join_riders(r1, r2):
    i1, o1, s1 = len(r1.inputs), len(r1.out_shape), len(r1.sems)
    aliases = dict(r1.aliases)
    aliases.update({i1 + i: o1 + o for i, o in r2.aliases.items()})

    def start(ins, outs, sems):
        r1.start(ins[:i1], outs[:o1], sems[:s1])
        r2.start(ins[i1:], outs[o1:], sems[s1:])

    def finish(ins, outs, sems):
        r1.finish(ins[:i1], outs[:o1], sems[:s1])
        r2.finish(ins[i1:], outs[o1:], sems[s1:])

    return _Rider(r1.inputs + r2.inputs, r1.out_shape + r2.out_shape, aliases, r1.sems + r2.sems, start, finish)


def _after_rider(x):
    return _Rider([x], [], {}, [], lambda *a: None, lambda *a: None)


def _call(body, *, name, grid, in_specs, out_specs, out_shape, scratch_shapes, args, sem, rider=None, after=None):
    out_specs, out_shape = tuple(out_specs), tuple(out_shape)
    if after is not None:
        rider = _after_rider(after) if rider is None else _join_riders(_after_rider(after), rider)
    if rider is None:
        res = pl.pallas_call(body, name=name, out_shape=out_shape, grid=grid, in_specs=list(in_specs),
                             out_specs=out_specs, scratch_shapes=list(scratch_shapes),
                             compiler_params=_cparams(sem))(*args)
        return tuple(res), []
    n_in, n_out, n_sc = len(in_specs), len(out_specs), len(scratch_shapes)
    r_in, r_out = len(rider.inputs), len(rider.out_shape)

    def wrapped(*refs):
        p = 0
        host_in = refs[p:p + n_in]; p += n_in
        rid_in = refs[p:p + r_in]; p += r_in
        host_out = refs[p:p + n_out]; p += n_out
        rid_out = refs[p:p + r_out]; p += r_out
        host_sc = refs[p:p + n_sc]; p += n_sc
        rid_sem = refs[p:]
        first = functools.reduce(jnp.logical_and, [pl.program_id(a) == 0 for a in range(len(grid))])
        last = functools.reduce(jnp.logical_and, [pl.program_id(a) == grid[a] - 1 for a in range(len(grid))])

        @pl.when(first)
        def _():
            rider.start(rid_in, rid_out, rid_sem)

        body(*host_in, *host_out, *host_sc)

        @pl.when(last)
        def _():
            rider.finish(rid_in, rid_out, rid_sem)

    res = pl.pallas_call(
        wrapped, name=name, out_shape=out_shape + tuple(rider.out_shape), grid=grid,
        in_specs=list(in_specs) + [ANY] * r_in, out_specs=out_specs + (ANY,) * r_out,
        input_output_aliases={n_in + i: n_out + o for i, o in rider.aliases.items()},
        scratch_shapes=list(scratch_shapes) + rider.sems,
        compiler_params=_cparams(("arbitrary",) * len(grid)),
    )(*args, *rider.inputs)
    return tuple(res[:n_out]), list(res[n_out:])


def _run_rider(rider, name):
    def body(*refs):
        r_in, r_out = len(rider.inputs), len(rider.out_shape)
        ins, outs, sems = refs[:r_in], refs[r_in:r_in + r_out], refs[r_in + r_out:]
        rider.start(ins, outs, sems)
        rider.finish(ins, outs, sems)

    return pl.pallas_call(
        body, name=name, out_shape=rider.out_shape, in_specs=[ANY] * len(rider.inputs),
        out_specs=[ANY] * len(rider.out_shape), input_output_aliases=rider.aliases, scratch_shapes=rider.sems,
    )(*rider.inputs)


def _atoms(indices, kks=(0, 1, 2), q=0, nq=1):
    return [(i, kk, q, nq) for i in indices for kk in kks]


def _gather_rider(bufs, atoms=None):
    n = len(bufs)
    if atoms is None:
        atoms = _atoms(range(n))
    na = len(atoms)

    def rows(out, atom, core):
        i, _, q, nq = atom
        half = out[i].shape[1] // 2
        assert half % (16 * nq) == 0, (half, nq)
        return pl.ds(core * half + q * (half // nq), half // nq)

    def ici_copy(out, sems, a, slot, peer):
        c = lax.axis_index("c")
        blk = out[atoms[a][0]].at[slot, rows(out, atoms[a], c), :]
        return pltpu.make_async_remote_copy(
            src_ref=blk, dst_ref=blk, send_sem=sems[0].at[a], recv_sem=sems[1].at[a],
            device_id=(peer[0], peer[1], c), device_id_type=MESH)

    def d2d_copy(out, sems, a, slot, from_core):
        x, y, c, _ = _place()
        blk = out[atoms[a][0]].at[slot, rows(out, atoms[a], from_core), :]
        return pltpu.make_async_remote_copy(
            src_ref=blk, dst_ref=blk, send_sem=sems[2].at[a], recv_sem=sems[3].at[a],
            device_id=(x, y, 1 - c), device_id_type=MESH)

    def start(ins, out, sems):
        x, y, c, chips = _place()
        for a in range(na):
            ici_copy(out, sems, a, 2 * x + y, chips[atoms[a][1]]).start()

    def finish(ins, out, sems):
        x, y, c, chips = _place()
        src = [2 * chips[atoms[a][1]][0] + chips[atoms[a][1]][1] for a in range(na)]
        for a in range(na):
            ici_copy(out, sems, a, src[a], chips[atoms[a][1]]).wait_recv()
            d2d_copy(out, sems, a, src[a], c).start()
        for a in range(na):
            d2d_copy(out, sems, a, src[a], 1 - c).wait_recv()
        for a in range(na):
            ici_copy(out, sems, a, 2 * x + y, chips[atoms[a][1]]).wait_send()
            d2d_copy(out, sems, a, src[a], c).wait_send()

    return _Rider(bufs, [jax.ShapeDtypeStruct(s.shape, s.dtype) for s in bufs], {i: i for i in range(n)},
                  [pltpu.SemaphoreType.DMA((na,))] * 4, start, finish)


def _mm_shards(a, buf, order_arr, which, name, after, out=None):
    M, K = a.shape
    S, _, n = buf.shape
    tm = _pick(M, (512, 256, 128))
    s0 = which[0]

    def body(order_ref, a_ref, b_ref, *rest):
        rest[-1][...] = jnp.dot(a_ref[...], b_ref[...], preferred_element_type=F32)

    gs = pltpu.PrefetchScalarGridSpec(
        num_scalar_prefetch=1, grid=(len(which), M // tm),
        in_specs=[pl.BlockSpec((tm, K), lambda g, i, order: (i, 0)),
                  pl.BlockSpec((None, K, n), lambda g, i, order: (order[s0 + g], 0, 0)), ANY]
        + ([ANY] if out is not None else []),
        out_specs=pl.BlockSpec((tm, n), lambda g, i, order: (i, order[s0 + g])))
    return pl.pallas_call(
        body, name=name, grid_spec=gs, out_shape=jax.ShapeDtypeStruct((M, S * n), F32),
        input_output_aliases={4: 0} if out is not None else {},
        compiler_params=_cparams(("arbitrary", "arbitrary")),
    )(order_arr, a, buf, after, *([out] if out is not None else []))


def _mm_gathering(a, buf, order_arr, name, after):
    M, K = a.shape
    S, _, n = buf.shape
    tm = _pick(M, (512, 256, 128))
    n_i = M // tm
    half = K // 2

    def body(order_ref, a_ref, w_in_ref, after_ref, o_ref, w_ref, b_vmem, load_sem, s_ici, r_ici, s_d2d, r_d2d):
        s, i = pl.program_id(0), pl.program_id(1)
        x, y, c, chips = _place()
        j_me = 2 * x + y
        slots = [2 * px + py for px, py in chips]

        def ici(kk, slot):
            blk = w_ref.at[slot, pl.ds(c * half, half), :]
            return pltpu.make_async_remote_copy(
                src_ref=blk, dst_ref=blk, send_sem=s_ici.at[kk], recv_sem=r_ici.at[kk],
                device_id=(chips[kk][0], chips[kk][1], c), device_id_type=MESH)

        def d2d(kk, from_core):
            blk = w_ref.at[slots[kk], pl.ds(from_core * half, half), :]
            return pltpu.make_async_remote_copy(
                src_ref=blk, dst_ref=blk, send_sem=s_d2d.at[kk], recv_sem=r_d2d.at[kk],
                device_id=(x, y, 1 - c), device_id_type=MESH)

        def load(slot, b):
            return pltpu.make_async_copy(w_ref.at[slot], b_vmem.at[b], load_sem.at[b])

        @pl.when(jnp.logical_and(s == 0, i == 0))
        def _():
            for kk in range(3):
                ici(kk, j_me).start()
            load(j_me, 0).start()

        @pl.when(i == 0)
        def _():
            load(order_ref[s], s % 2).wait()

        o_ref[...] = jnp.dot(a_ref[...], b_vmem[s % 2], preferred_element_type=F32)

        last = i == n_i - 1

        @pl.when(jnp.logical_and(last, s == 0))
        def _():
            ici(0, slots[0]).wait_recv()
            d2d(0, c).start()
            ici(1, slots[1]).wait_recv()
            d2d(1, c).start()
            d2d(0, 1 - c).wait_recv()
            load(slots[0], 1).start()

        @pl.when(jnp.logical_and(last, s == 1))
        def _():
            d2d(1, 1 - c).wait_recv()
            load(slots[1], 0).start()

        @pl.when(jnp.logical_and(last, s == 2))
        def _():
            ici(2, slots[2]).wait_recv()
            d2d(2, c).start()
            d2d(2, 1 - c).wait_recv()
            load(slots[2], 1).start()

        @pl.when(jnp.logical_and(last, s == 3))
        def _():
            for kk in range(3):
                ici(kk, j_me).wait_send()
                d2d(kk, c).wait_send()

    gs = pltpu.PrefetchScalarGridSpec(
        num_scalar_prefetch=1, grid=(S, n_i),
        in_specs=[pl.BlockSpec((tm, K), lambda s, i, order: (i, 0)), ANY, ANY],
        out_specs=[pl.BlockSpec((tm, n), lambda s, i, order: (i, order[s])), ANY],
        scratch_shapes=[pltpu.VMEM((2, K, n), BF16), pltpu.SemaphoreType.DMA((2,))]
        + [pltpu.SemaphoreType.DMA((3,))] * 4)
    return pl.pallas_call(
        body, name=name, grid_spec=gs,
        out_shape=[jax.ShapeDtypeStruct((M, S * n), F32), jax.ShapeDtypeStruct(buf.shape, buf.dtype)],
        input_output_aliases={2: 1}, compiler_params=_cparams(("arbitrary", "arbitrary")),
    )(order_arr, a, buf, after)


def _all_gather_small(shards):
    n = len(shards)

    def body(*refs):
        w = refs[:n]
        out = refs[n:2 * n]
        local_sem, s_sem, r_sem = refs[2 * n:]
        x, y, c, chips = _place()
        j_me = 2 * x + y
        cps = []
        for i in range(n):
            lc = pltpu.make_async_copy(w[i], out[i].at[j_me], local_sem.at[i])
            lc.start()
            cps.append(lc)
        sends = []
        for i in range(n):
            for kk, (px, py) in enumerate(chips):
                cp = pltpu.make_async_remote_copy(
                    src_ref=w[i], dst_ref=out[i].at[j_me], send_sem=s_sem.at[3 * i + kk],
                    recv_sem=r_sem.at[3 * i + kk], device_id=(px, py, c), device_id_type=MESH)
                cp.start()
                sends.append(cp)
        for i in range(n):
            for kk, (px, py) in enumerate(chips):
                sends[3 * i + kk].wait_send()
                pltpu.make_async_remote_copy(
                    src_ref=w[i], dst_ref=out[i].at[2 * px + py], send_sem=s_sem.at[3 * i + kk],
                    recv_sem=r_sem.at[3 * i + kk], device_id=(px, py, c), device_id_type=MESH).wait_recv()
        for lc in cps:
            lc.wait()

    out_shape = [jax.ShapeDtypeStruct((N_SHARDS,) + s.shape, s.dtype) for s in shards]
    return pl.pallas_call(
        body, name="all_gather_conv_weights", out_shape=out_shape, in_specs=[ANY] * n, out_specs=[ANY] * n,
        scratch_shapes=[pltpu.SemaphoreType.DMA((n,)), pltpu.SemaphoreType.DMA((3 * n,)),
                        pltpu.SemaphoreType.DMA((3 * n,))],
    )(*shards)


def _pair_rider(grads):
    n = len(grads)

    def copies(g, la, sems):
        x, y, c, _ = _place()
        return [pltpu.make_async_remote_copy(
            src_ref=g[i].at[:, pl.ds((1 - c) * (g[i].shape[1] // 2), g[i].shape[1] // 2), :], dst_ref=la[i],
            send_sem=sems[0].at[i], recv_sem=sems[1].at[i], device_id=(x, y, 1 - c), device_id_type=MESH)
            for i in range(n)]

    def start(g, la, sems):
        for cp in copies(g, la, sems):
            cp.start()

    def finish(g, la, sems):
        for cp in copies(g, la, sems):
            cp.wait()

    return _Rider(grads, [jax.ShapeDtypeStruct((s.shape[0], s.shape[1] // 2, s.shape[2]), s.dtype) for s in grads],
                  {}, [pltpu.SemaphoreType.DMA((n,)), pltpu.SemaphoreType.DMA((n,))], start, finish)


def _shard_exchange_rider(cps_in, atoms=None):
    n = len(cps_in)
    if atoms is None:
        atoms = _atoms(range(n))

    def copies(ins, lb, sems):
        x, y, c, chips = _place()
        out = []
        for a, (i, kk, q, nq) in enumerate(atoms):
            h = ins[i].shape[1]
            assert h % (16 * nq) == 0, (h, nq)
            rows = pl.ds(q * (h // nq), h // nq)
            px, py = chips[kk]
            out.append(pltpu.make_async_remote_copy(
                src_ref=ins[i].at[2 * px + py, rows, :], dst_ref=lb[i].at[kk, rows, :],
                send_sem=sems[0].at[a], recv_sem=sems[1].at[a], device_id=(px, py, c), device_id_type=MESH))
        return out

    def start(ins, lb, sems):
        for cp in copies(ins, lb, sems):
            cp.start()

    def finish(ins, lb, sems):
        for cp in copies(ins, lb, sems):
            cp.wait()

    return _Rider(cps_in, [jax.ShapeDtypeStruct((3,) + s.shape[1:], s.dtype) for s in cps_in], {},
                  [pltpu.SemaphoreType.DMA((len(atoms),)), pltpu.SemaphoreType.DMA((len(atoms),))], start, finish)


HBM = pl.BlockSpec(memory_space=pltpu.HBM)
SEM = pl.BlockSpec(memory_space=pltpu.SEMAPHORE)


def _shard_copies(part_refs, land_refs, send_sems, recv_sems, relations):
    x, y, c, chips = _place()
    nr = len(relations)
    return [pltpu.make_async_remote_copy(
        src_ref=part_refs[i].at[2 * chips[kk][0] + chips[kk][1]], dst_ref=land_refs[i].at[kk],
        send_sem=send_sems.at[nr * i + r], recv_sem=recv_sems.at[nr * i + r],
        device_id=(chips[kk][0], chips[kk][1], c), device_id_type=MESH)
        for i in range(len(part_refs)) for r, kk in enumerate(relations)]


SIDE_EFFECT = pltpu.SideEffectType.DATAFLOW_SIDE_EFFECTING


def _shard_exchange_start(parts, name, relations=(0, 1, 2), lands=None):
    n = len(parts)
    ns = n * len(relations)

    def body(*refs):
        part_refs, land_refs = refs[:n], refs[n:2 * n]
        send_sems, recv_sems = refs[2 * n], refs[2 * n + 1]
        token = refs[4 * n + 2]
        for cp in _shard_copies(part_refs, land_refs, send_sems, recv_sems, relations):
            cp.start()
        token[...] = jnp.zeros_like(token)

    if lands is None:
        lands = [lax.empty((3,) + p.shape[1:], p.dtype) for p in parts]
    bufs = list(parts) + list(lands)
    res = pl.pallas_call(
        body, name=name,
        out_shape=(pltpu.SemaphoreType.DMA((ns,)), pltpu.SemaphoreType.DMA((ns,)),
                   *[pltpu.HBM(b.shape, b.dtype) for b in bufs], jax.ShapeDtypeStruct((8, LANES), F32)),
        in_specs=(HBM,) * (2 * n), out_specs=(SEM, SEM) + (HBM,) * (2 * n) + (pl.BlockSpec(memory_space=pltpu.VMEM),),
        input_output_aliases={i: 2 + i for i in range(2 * n)},
        compiler_params=pltpu.CompilerParams(has_side_effects=SIDE_EFFECT),
    )(*[pltpu.with_memory_space_constraint(b, pltpu.HBM) for b in bufs])
    return res[0], res[1], list(res[2:2 + n]), list(res[2 + n:2 + 2 * n]), res[2 + 2 * n], relations


def _shard_exchange_wait(started, after, name):
    send_sems, recv_sems, parts, lands, _, relations = started
    n = len(parts)

    def body(*refs):
        part_refs, land_refs = refs[:n], refs[n:2 * n]
        send_sems_ref, recv_sems_ref = refs[2 * n], refs[2 * n + 1]
        for cp in _shard_copies(part_refs, land_refs, send_sems_ref, recv_sems_ref, relations):
            cp.wait_send()
            cp.wait_recv()

    bufs = parts + lands
    res = pl.pallas_call(
        body, name=name, out_shape=tuple(pltpu.HBM(b.shape, b.dtype) for b in bufs),
        in_specs=(HBM,) * (2 * n) + (SEM, SEM, ANY), out_specs=(HBM,) * (2 * n),
        input_output_aliases={i: i for i in range(2 * n)},
        compiler_params=pltpu.CompilerParams(has_side_effects=SIDE_EFFECT),
    )(*bufs, send_sems, recv_sems, after)
    return list(res[:n]), list(res[n:])


def _gather_copies(buf_ref, send_sems, recv_sems, over_d2d, arriving, relations):
    x, y, c, chips = _place()
    half = buf_ref.shape[1] // 2
    out = []
    for r, kk in enumerate(relations):
        px, py = chips[kk]
        if over_d2d:
            slot, core, peer = 2 * px + py, (1 - c) if arriving else c, (x, y, 1 - c)
        else:
            slot, core, peer = (2 * px + py) if arriving else (2 * x + y), c, (px, py, c)
        blk = buf_ref.at[slot, pl.ds(core * half, half), :]
        out.append(pltpu.make_async_remote_copy(src_ref=blk, dst_ref=blk, send_sem=send_sems.at[r],
                                                recv_sem=recv_sems.at[r], device_id=peer, device_id_type=MESH))
    return out


def _gather_step(buf, after, name, sems_in=None, start_d2d=None, relations=(0, 1, 2)):
    n_sem = 0 if sems_in is None else 2

    def body(*refs):
        buf_ref = refs[0]
        ins = refs[1:1 + n_sem]
        outs = refs[2 + n_sem:]
        if sems_in is not None:
            waited_d2d = start_d2d is None
            for mine, theirs in zip(_gather_copies(buf_ref, ins[0], ins[1], waited_d2d, False, relations),
                                    _gather_copies(buf_ref, ins[0], ins[1], waited_d2d, True, relations)):
                theirs.wait_recv()
                mine.wait_send()
        if start_d2d is not None:
            for cp in _gather_copies(buf_ref, outs[0], outs[1], start_d2d, False, relations):
                cp.start()
            outs[3][...] = jnp.zeros_like(outs[3])

    nr = len(relations)
    sem_out = () if start_d2d is None else (pltpu.SemaphoreType.DMA((nr,)), pltpu.SemaphoreType.DMA((nr,)))
    tok_out = () if start_d2d is None else (jax.ShapeDtypeStruct((8, LANES), F32),)
    res = pl.pallas_call(
        body, name=name,
        out_shape=sem_out + (pltpu.HBM(buf.shape, buf.dtype),) + tok_out,
        in_specs=(HBM,) + (SEM,) * n_sem + (ANY,),
        out_specs=(SEM,) * len(sem_out) + (HBM,) + (pl.BlockSpec(memory_space=pltpu.VMEM),) * len(tok_out),
        input_output_aliases={0: len(sem_out)},
        compiler_params=pltpu.CompilerParams(has_side_effects=SIDE_EFFECT),
    )(pltpu.with_memory_space_constraint(buf, pltpu.HBM), *(sems_in or ()), after)
    if start_d2d is None:
        return res[0]
    return (res[0], res[1]), res[2], res[3]


def _share_rider(halves, eighths=None):
    n = len(halves)
    bufs = list(halves) + ([eighths] if eighths is not None else [])

    def half_copy(out, sems, i, core):
        x, y, c, _ = _place()
        blk = out[i].at[core]
        return pltpu.make_async_remote_copy(src_ref=blk, dst_ref=blk, send_sem=sems[0].at[i], recv_sem=sems[1].at[i],
                                            device_id=(x, y, 1 - c), device_id_type=MESH)

    def eighth_copy(out, sems, r, mine):
        x, y, c, _ = _place()
        px, py, pc = x ^ ((r >> 2) & 1), y ^ ((r >> 1) & 1), c ^ (r & 1)
        blk = out[n].at[2 * x + y, c] if mine else out[n].at[2 * px + py, pc]
        return pltpu.make_async_remote_copy(src_ref=blk, dst_ref=blk, send_sem=sems[2].at[r - 1],
                                            recv_sem=sems[3].at[r - 1], device_id=(px, py, pc), device_id_type=MESH)

    def start(ins, out, sems):
        c = lax.axis_index("c")
        for i in range(n):
            half_copy(out, sems, i, c).start()
        if eighths is not None:
            for r in range(1, N_DEV):
                eighth_copy(out, sems, r, True).start()

    def finish(ins, out, sems):
        c = lax.axis_index("c")
        for i in range(n):
            half_copy(out, sems, i, 1 - c).wait_recv()
        if eighths is not None:
            for r in range(1, N_DEV):
                eighth_copy(out, sems, r, False).wait_recv()
        for i in range(n):
            half_copy(out, sems, i, c).wait_send()
        if eighths is not None:
            for r in range(1, N_DEV):
                eighth_copy(out, sems, r, True).wait_send()

    return _Rider(bufs, [jax.ShapeDtypeStruct(s.shape, s.dtype) for s in bufs], {i: i for i in range(len(bufs))},
                  [pltpu.SemaphoreType.DMA((max(n, 1),)), pltpu.SemaphoreType.DMA((max(n, 1),)),
                   pltpu.SemaphoreType.DMA((N_DEV - 1,)), pltpu.SemaphoreType.DMA((N_DEV - 1,))], start, finish)


ATTN_ROWS = GROUP * ATTN_BLOCK
ATTN_KEYS = 2 * ATTN_BLOCK


def _attn_geometry(n):
    row = lax.broadcasted_iota(jnp.int32, (ATTN_ROWS, ATTN_KEYS), 0)
    col = lax.broadcasted_iota(jnp.int32, (ATTN_ROWS, ATTN_KEYS), 1)
    dist = ATTN_BLOCK + jnp.bitwise_and(row, ATTN_BLOCK - 1) - col
    valid = jnp.logical_and(jnp.logical_and(dist >= 0, dist < ATTN_BLOCK),
                            jnp.logical_or(col >= ATTN_BLOCK, n > 0))
    return dist.astype(F32), valid


def _per_head_column(values):
    head = lax.broadcasted_iota(jnp.int32, (ATTN_ROWS, 1), 0) // ATTN_BLOCK
    col = jnp.zeros((ATTN_ROWS, 1), F32)
    for hh, v in enumerate(values):
        col = jnp.where(head == hh, v, col)
    return col


def _stack_heads(ref, g):
    return jnp.concatenate(
        [ref[:, (g * GROUP + hh) * HEAD_DIM:(g * GROUP + hh + 1) * HEAD_DIM].astype(BF16) for hh in range(GROUP)],
        axis=0)


def _attn_probs(q_s, k2, slope_col, sink_col, dist, valid):
    s = lax.dot_general(q_s, k2, (((1,), (1,)), ((), ())), preferred_element_type=F32) * (HEAD_DIM ** -0.5)
    s = jnp.where(valid, s - slope_col * dist, NEG)
    m = jnp.maximum(jnp.max(s, axis=1, keepdims=True), sink_col)
    e = jnp.exp(s - m)
    es = jnp.exp(sink_col - m)
    inv = 1.0 / (jnp.sum(e, axis=1, keepdims=True) + es)
    return e * inv, es * inv


def _attn_specs(T, d_attn, d_kv, q_blk, k_blk, v_blk):
    bq = pl.BlockSpec((ATTN_BLOCK, d_attn), lambda n: (n, q_blk))
    kp = pl.BlockSpec((ATTN_BLOCK, d_kv), lambda n: (jnp.maximum(n - 1, 0), k_blk))
    kc = pl.BlockSpec((ATTN_BLOCK, d_kv), lambda n: (n, k_blk))
    vp = pl.BlockSpec((ATTN_BLOCK, d_kv), lambda n: (jnp.maximum(n - 1, 0), v_blk))
    vc = pl.BlockSpec((ATTN_BLOCK, d_kv), lambda n: (n, v_blk))
    return bq, kp, kc, vp, vc


def _attn_fwd(proj, sinks, nq, cols, after=None):
    T = proj.shape[0]
    nkv = nq // GROUP
    d_attn, d_kv = nq * HEAD_DIM, nkv * HEAD_DIM
    q_off, k_off, v_off = cols
    bq, kp, kc, vp, vc = _attn_specs(T, d_attn, d_kv, q_off // d_attn, k_off // d_kv, v_off // d_kv)

    def body(sink_ref, q_ref, kp_ref, kc_ref, vp_ref, vc_ref, o_ref):
        n = pl.program_id(0)
        dist, valid = _attn_geometry(n)
        for g in range(nkv):
            ks = slice(g * HEAD_DIM, (g + 1) * HEAD_DIM)
            k2 = jnp.concatenate([kp_ref[:, ks], kc_ref[:, ks]], axis=0).astype(BF16)
            v2 = jnp.concatenate([vp_ref[:, ks], vc_ref[:, ks]], axis=0).astype(BF16)
            slope_col = _per_head_column([2.0 ** (-8.0 * (g * GROUP + hh + 1) / nq) for hh in range(GROUP)])
            sink_col = _per_head_column([sink_ref[0, g * GROUP + hh] for hh in range(GROUP)])
            p, _ = _attn_probs(_stack_heads(q_ref, g), k2, slope_col, sink_col, dist, valid)
            o = jnp.dot(p.astype(BF16), v2, preferred_element_type=F32).astype(BF16)
            for hh in range(GROUP):
                h = g * GROUP + hh
                o_ref[:, h * HEAD_DIM:(h + 1) * HEAD_DIM] = o[hh * ATTN_BLOCK:(hh + 1) * ATTN_BLOCK, :]

    (out,), carried = _call(
        body, name="attn_fwd", out_shape=[jax.ShapeDtypeStruct((T, d_attn), BF16)], grid=(T // ATTN_BLOCK,),
        in_specs=[pl.BlockSpec(memory_space=pltpu.SMEM), bq, kp, kc, vp, vc],
        out_specs=[pl.BlockSpec((ATTN_BLOCK, d_attn), lambda n: (n, 0))], scratch_shapes=[],
        args=(sinks, proj, proj, proj, proj, proj), sem=("parallel",), after=after)
    return out


def _attn_bwd(proj, d_attn_out, sinks, nq, cols, after=None):
    T = proj.shape[0]
    nkv = nq // GROUP
    d_attn, d_kv = nq * HEAD_DIM, nkv * HEAD_DIM
    q_off, k_off, v_off = cols
    bq, kp, kc, vp, vc = _attn_specs(T, d_attn, d_kv, q_off // d_attn, k_off // d_kv, v_off // d_kv)
    scale = HEAD_DIM ** -0.5
    dn_t = (((1,), (1,)), ((), ()))
    dn_r = (((0,), (0,)), ((), ()))

    def body(sink_ref, q_ref, kp_ref, kc_ref, vp_ref, vc_ref, do_ref, dq_ref, dk_ref, dv_ref, ds_ref):
        n = pl.program_id(0)

        @pl.when(n == 0)
        def _():
            dk_ref[...] = jnp.zeros_like(dk_ref)
            dv_ref[...] = jnp.zeros_like(dv_ref)
            ds_ref[...] = jnp.zeros_like(ds_ref)

        dist, valid = _attn_geometry(n)
        rows_c = pl.ds(pl.multiple_of(n * ATTN_BLOCK, ATTN_BLOCK), ATTN_BLOCK)
        rows_p = pl.ds(pl.multiple_of(jnp.maximum(n - 1, 0) * ATTN_BLOCK, ATTN_BLOCK), ATTN_BLOCK)
        lane = lax.broadcasted_iota(jnp.int32, ds_ref.shape, 1)
        srow = lax.broadcasted_iota(jnp.int32, ds_ref.shape, 0)
        ds_acc = jnp.zeros(ds_ref.shape, F32)
        for g in range(nkv):
            ks = slice(g * HEAD_DIM, (g + 1) * HEAD_DIM)
            k2 = jnp.concatenate([kp_ref[:, ks], kc_ref[:, ks]], axis=0).astype(BF16)
            v2 = jnp.concatenate([vp_ref[:, ks], vc_ref[:, ks]], axis=0).astype(BF16)
            slope_col = _per_head_column([2.0 ** (-8.0 * (g * GROUP + hh + 1) / nq) for hh in range(GROUP)])
            sink_col = _per_head_column([sink_ref[0, g * GROUP + hh] for hh in range(GROUP)])
            q_s = _stack_heads(q_ref, g)
            do_s = _stack_heads(do_ref, g)
            p, p_sink = _attn_probs(q_s, k2, slope_col, sink_col, dist, valid)
            dp = lax.dot_general(do_s, v2, dn_t, preferred_element_type=F32)
            delta = jnp.sum(p * dp, axis=1, keepdims=True)
            ds = (p * (dp - delta)).astype(BF16)
            sink_part = p_sink * delta
            dq = (jnp.dot(ds, k2, preferred_element_type=F32) * scale).astype(BF16)
            for hh in range(GROUP):
                h = g * GROUP + hh
                blk = slice(hh * ATTN_BLOCK, (hh + 1) * ATTN_BLOCK)
                dq_ref[:, h * HEAD_DIM:(h + 1) * HEAD_DIM] = dq[blk, :]
                ds_acc = ds_acc + jnp.where(jnp.logical_and(lane == h, srow == 0), -jnp.sum(sink_part[blk, :]), 0.0)
            dk2 = lax.dot_general(ds, q_s, dn_r, preferred_element_type=F32) * scale
            dv2 = lax.dot_general(p.astype(BF16), do_s, dn_r, preferred_element_type=F32)
            dk_ref[rows_p, ks] += dk2[:ATTN_BLOCK, :]
            dv_ref[rows_p, ks] += dv2[:ATTN_BLOCK, :]
            dk_ref[rows_c, ks] += dk2[ATTN_BLOCK:, :]
            dv_ref[rows_c, ks] += dv2[ATTN_BLOCK:, :]
        ds_ref[...] += ds_acc

    out_shape = (jax.ShapeDtypeStruct((T, d_attn), BF16), jax.ShapeDtypeStruct((T, d_kv), F32),
                 jax.ShapeDtypeStruct((T, d_kv), F32), jax.ShapeDtypeStruct((8, LANES), F32))
    return _call(
        body, name="attn_bwd", out_shape=out_shape, grid=(T // ATTN_BLOCK,),
        in_specs=[pl.BlockSpec(memory_space=pltpu.SMEM), bq, kp, kc, vp, vc,
                  pl.BlockSpec((ATTN_BLOCK, d_attn), lambda n: (n, 0))],
        out_specs=(pl.BlockSpec((ATTN_BLOCK, d_attn), lambda n: (n, 0)),
                   pl.BlockSpec((T, d_kv), lambda n: (0, 0)), pl.BlockSpec((T, d_kv), lambda n: (0, 0)),
                   pl.BlockSpec((8, LANES), lambda n: (0, 0))),
        scratch_shapes=[], args=(sinks, proj, proj, proj, proj, proj, d_attn_out), sem=("arbitrary",), after=after)[0]


def _rnn_tile(T):
    return _pick(T, (256, 128))


def _rnn_gates(x_ext, cw_ref, cb_ref, wa_ref, wi_ref, ba_ref, bi_ref, lam_ref, tt):
    xs = [pltpu.roll(x_ext, 3 - k, 0)[8:, :] if k < 3 else x_ext[8:, :] for k in range(4)]
    cx = cb_ref[...] + xs[0] * cw_ref[0:1, :]
    for k in range(1, 4):
        cx = cx + xs[k] * cw_ref[k:k + 1, :]
    cxb = cx.astype(BF16)
    r = jax.nn.sigmoid(jnp.dot(cxb, wa_ref[...], preferred_element_type=F32) + ba_ref[...])
    i = jax.nn.sigmoid(jnp.dot(cxb, wi_ref[...], preferred_element_type=F32) + bi_ref[...])
    lam = lam_ref[...]
    sp = jnp.maximum(-lam, 0.0) + jnp.log1p(jnp.exp(-jnp.abs(lam)))
    log_a = -LRU_C * r * sp
    a = jnp.exp(log_a)
    z = 2.0 * log_a
    em1 = jnp.where(z > -1e-2, z * (1.0 + z * (0.5 + z * (1.0 / 6.0 + z * (1.0 / 24.0)))), jnp.exp(z) - 1.0)
    s = jnp.sqrt(-em1)
    return xs, cx, r, i, sp, a, s


def _rnn_specs(T, gw, tt, rx_blk, ry_blk, rev):
    nT = T // tt
    hb = tt // 8

    def tile(t):
        return (nT - 1 - t) if rev else t

    rx = pl.BlockSpec((tt, gw), lambda g, t: (tile(t), rx_blk + g))
    rx_halo = pl.BlockSpec((8, gw), lambda g, t: (jnp.maximum(tile(t) * hb - 1, 0), rx_blk + g))
    ry = pl.BlockSpec((tt, gw), lambda g, t: (tile(t), ry_blk + g))
    cw = pl.BlockSpec((4, gw), lambda g, t: (0, g))
    vec = pl.BlockSpec((1, gw), lambda g, t: (0, g))
    wg = pl.BlockSpec((None, gw, gw), lambda g, t: (g, 0, 0))
    act = pl.BlockSpec((tt, gw), lambda g, t: (tile(t), g))
    act_halo = pl.BlockSpec((8, gw), lambda g, t: (jnp.maximum(tile(t) * hb - 1, 0), g))
    return rx, rx_halo, ry, cw, vec, wg, act, act_halo, tile


def _rnn_fwd(proj, cols, conv_w, conv_b, wa_g, wi_g, ba, bi, lam, rider=None):
    T = proj.shape[0]
    G, gw, _ = wa_g.shape
    d_rnn = G * gw
    tt = _rnn_tile(T)
    rx_off, ry_off = cols
    rx, rx_halo, ry, cw, vec, wg, act, _, _ = _rnn_specs(T, gw, tt, rx_off // gw, ry_off // gw, False)

    def body(rx_ref, rxh_ref, ry_ref, cw_ref, cb_ref, wa_ref, wi_ref, ba_ref, bi_ref, lam_ref,
             b_ref, h_ref, carry):
        t = pl.program_id(1)

        @pl.when(t == 0)
        def _():
            carry[...] = jnp.zeros_like(carry)

        halo = jnp.where(t > 0, rxh_ref[...], 0.0)
        x_ext = jnp.concatenate([halo, rx_ref[...]], axis=0)
        _, cx, _, i, _, a, s = _rnn_gates(x_ext, cw_ref, cb_ref, wa_ref, wi_ref, ba_ref, bi_ref, lam_ref, tt)
        acc_a, acc_b = a, s * (i * cx)
        d = 1
        while d < tt:
            acc_b = acc_a * _shift_down(acc_b, d, 0.0) + acc_b
            acc_a = acc_a * _shift_down(acc_a, d, 1.0)
            d *= 2
        h = acc_b + acc_a * carry[7:8, :]
        carry[...] = h[tt - 8:, :]
        h_ref[...] = h
        b_ref[...] = (h * _gelu(ry_ref[...])).astype(BF16)

    return _call(
        body, name="rnn_fwd",
        out_shape=(jax.ShapeDtypeStruct((T, d_rnn), BF16), jax.ShapeDtypeStruct((T, d_rnn), F32)),
        grid=(G, T // tt),
        in_specs=[rx, rx_halo, ry, cw, vec, wg, wg, vec, vec, vec], out_specs=(act, act),
        scratch_shapes=[pltpu.VMEM((8, gw), F32)],
        args=(proj, proj, proj, conv_w, conv_b, wa_g, wi_g, ba, bi, lam), sem=("parallel", "arbitrary"), rider=rider)


def _rnn_bwd(proj, cols, h_all, d_b, conv_w, conv_b, wa_g, wi_g, ba, bi, lam, rider=None):
    T = proj.shape[0]
    G, gw, _ = wa_g.shape
    d_rnn = G * gw
    tt = _rnn_tile(T)
    nT = T // tt
    rx_off, ry_off = cols
    rx, rx_halo, ry, cw, vec, wg, act, act_halo, _ = _rnn_specs(T, gw, tt, rx_off // gw, ry_off // gw, True)
    dn_t = (((1,), (1,)), ((), ()))
    dn_r = (((0,), (0,)), ((), ()))

    def body(rx_ref, rxh_ref, ry_ref, h_ref, hh_ref, db_ref, cw_ref, cb_ref, wa_ref, wi_ref, ba_ref, bi_ref, lam_ref,
             drx_ref, dry_ref, dcw_ref, dcb_ref, dba_ref, dbi_ref, dlam_ref, dwa_ref, dwi_ref,
             lam_carry, dcx_carry):
        t = pl.program_id(1)
        first_tile = t == nT - 1

        @pl.when(t == 0)
        def _():
            lam_carry[...] = jnp.zeros_like(lam_carry)
            dcx_carry[...] = jnp.zeros_like(dcx_carry)
            dcw_ref[...] = jnp.zeros_like(dcw_ref)
            dcb_ref[...] = jnp.zeros_like(dcb_ref)
            dba_ref[...] = jnp.zeros_like(dba_ref)
            dbi_ref[...] = jnp.zeros_like(dbi_ref)
            dlam_ref[...] = jnp.zeros_like(dlam_ref)
            dwa_ref[...] = jnp.zeros_like(dwa_ref)
            dwi_ref[...] = jnp.zeros_like(dwi_ref)

        halo = jnp.where(first_tile, 0.0, rxh_ref[...])
        x_ext = jnp.concatenate([halo, rx_ref[...]], axis=0)
        xs, cx, r, i, sp, a, s = _rnn_gates(x_ext, cw_ref, cb_ref, wa_ref, wi_ref, ba_ref, bi_ref, lam_ref, tt)
        h = h_ref[...]
        h_halo = jnp.where(first_tile, 0.0, hh_ref[...])
        h_prev = pltpu.roll(jnp.concatenate([h_halo, h], axis=0), 1, 0)[8:, :]
        gel, dgel = _gelu_and_grad(ry_ref[...])
        d_b_t = db_ref[...]
        dry_ref[...] = (d_b_t * h * dgel).astype(BF16)
        dh = d_b_t * gel

        acc_c = _shift_up(a, 1, 1.0)
        acc_l = dh
        d = 1
        while d < tt:
            acc_l = acc_c * _shift_up(acc_l, d, 0.0) + acc_l
            acc_c = acc_c * _shift_up(acc_c, d, 1.0)
            d *= 2
        lam_t = acc_l + acc_c * lam_carry[0:1, :]
        lam_carry[...] = (a * lam_t)[0:8, :]

        icx = i * cx
        d_s = lam_t * icx
        d_i = lam_t * s * cx
        dcx = lam_t * s * i
        d_a = lam_t * h_prev - d_s * (a / s)
        dlog_a = d_a * a
        d_r = dlog_a * (-LRU_C * sp)
        lam = lam_ref[...]
        dlam_ref[...] += jnp.sum(dlog_a * r, axis=0, keepdims=True) * (LRU_C * jax.nn.sigmoid(-lam))
        dpr = d_r * r * (1.0 - r)
        dpi = d_i * i * (1.0 - i)
        dba_ref[...] += jnp.sum(dpr, axis=0, keepdims=True)
        dbi_ref[...] += jnp.sum(dpi, axis=0, keepdims=True)
        cxb = cx.astype(BF16)
        dprb, dpib = dpr.astype(BF16), dpi.astype(BF16)
        dwa_ref[...] += lax.dot_general(cxb, dprb, dn_r, preferred_element_type=F32)
        dwi_ref[...] += lax.dot_general(cxb, dpib, dn_r, preferred_element_type=F32)
        dcx = (dcx + lax.dot_general(dprb, wa_ref[...], dn_t, preferred_element_type=F32)
               + lax.dot_general(dpib, wi_ref[...], dn_t, preferred_element_type=F32))

        dcb_ref[...] += jnp.sum(dcx, axis=0, keepdims=True)
        for k in range(4):
            dcw_ref[k:k + 1, :] += jnp.sum(dcx * xs[k], axis=0, keepdims=True)
        d_ext = jnp.concatenate([dcx, dcx_carry[...]], axis=0)
        drx = dcx * cw_ref[3:4, :]
        for k in range(3):
            drx = drx + pltpu.roll(d_ext, tt + 8 - (3 - k), 0)[:tt, :] * cw_ref[k:k + 1, :]
        drx_ref[...] = drx.astype(BF16)
        dcx_carry[...] = dcx[0:8, :]

    out_shape = (jax.ShapeDtypeStruct((T, d_rnn), BF16), jax.ShapeDtypeStruct((T, d_rnn), BF16),
                 jax.ShapeDtypeStruct((4, d_rnn), F32), jax.ShapeDtypeStruct((1, d_rnn), F32),
                 jax.ShapeDtypeStruct((1, d_rnn), F32), jax.ShapeDtypeStruct((1, d_rnn), F32),
                 jax.ShapeDtypeStruct((1, d_rnn), F32), jax.ShapeDtypeStruct((G, gw, gw), F32),
                 jax.ShapeDtypeStruct((G, gw, gw), F32))
    return _call(
        body, name="rnn_bwd", out_shape=out_shape, grid=(G, nT),
        in_specs=[rx, rx_halo, ry, act, act_halo, act, cw, vec, wg, wg, vec, vec, vec],
        out_specs=(act, act, cw, vec, vec, vec, vec, wg, wg),
        scratch_shapes=[pltpu.VMEM((8, gw), F32), pltpu.VMEM((8, gw), F32)],
        args=(proj, proj, proj, h_all, h_all, d_b, conv_w, conv_b, wa_g, wi_g, ba, bi, lam),
        sem=("parallel", "arbitrary"), rider=rider)


def _merge_fwd(proj, gl_off, b_gate, y_attn, y_rnn, rider=None):
    T, D = y_attn.shape
    tm = _pick(T, (256, 128))
    ct = _pick(math.gcd(gl_off, D), (512, 256, 128))
    oa, orr, nd = gl_off // ct, (gl_off + D) // ct, D // ct

    def body(ga_ref, gr_ref, ba_ref, br_ref, ya_ref, yr_ref, m_ref):
        ga = jax.nn.sigmoid(ga_ref[...] + ba_ref[...])
        gr = jax.nn.sigmoid(gr_ref[...] + br_ref[...])
        m_ref[...] = (ga * ya_ref[...] + gr * yr_ref[...]).astype(BF16)

    blk = pl.BlockSpec((tm, ct), lambda i, j: (i, j))
    (merged,), carried = _call(
        body, name="merge_fwd", out_shape=[jax.ShapeDtypeStruct((T, D), BF16)], grid=(T // tm, nd),
        in_specs=[pl.BlockSpec((tm, ct), lambda i, j: (i, oa + j)), pl.BlockSpec((tm, ct), lambda i, j: (i, orr + j)),
                  pl.BlockSpec((1, ct), lambda i, j: (0, j)), pl.BlockSpec((1, ct), lambda i, j: (0, nd + j)),
                  blk, blk],
        out_specs=[blk], scratch_shapes=[], args=(proj, proj, b_gate, b_gate, y_attn, y_rnn),
        sem=("parallel", "parallel"), rider=rider)
    return merged, carried


def _merge_bwd(proj, gl_off, b_gate, y_attn, y_rnn, d_m):
    T, D = y_attn.shape
    tm = _pick(T, (256, 128))
    ct = _pick(math.gcd(gl_off, D), (512, 256, 128))
    oa, orr, nd = gl_off // ct, (gl_off + D) // ct, D // ct

    def body(ga_ref, gr_ref, ba_ref, br_ref, ya_ref, yr_ref, dm_ref,
             dya_ref, dyr_ref, dga_ref, dgr_ref, dba_ref, dbr_ref):
        i = pl.program_id(1)

        @pl.when(i == 0)
        def _():
            dba_ref[...] = jnp.zeros_like(dba_ref)
            dbr_ref[...] = jnp.zeros_like(dbr_ref)

        ga = jax.nn.sigmoid(ga_ref[...] + ba_ref[...])
        gr = jax.nn.sigmoid(gr_ref[...] + br_ref[...])
        dm = dm_ref[...]
        dya_ref[...] = (dm * ga).astype(BF16)
        dyr_ref[...] = (dm * gr).astype(BF16)
        dga = dm * ya_ref[...] * ga * (1.0 - ga)
        dgr = dm * yr_ref[...] * gr * (1.0 - gr)
        dga_ref[...] = dga.astype(BF16)
        dgr_ref[...] = dgr.astype(BF16)
        dba_ref[...] += jnp.sum(dga, axis=0, keepdims=True)
        dbr_ref[...] += jnp.sum(dgr, axis=0, keepdims=True)

    blk = pl.BlockSpec((tm, ct), lambda j, i: (i, j))
    vec = pl.BlockSpec((1, ct), lambda j, i: (0, j))
    act = jax.ShapeDtypeStruct((T, D), BF16)
    v1 = jax.ShapeDtypeStruct((1, D), F32)
    return pl.pallas_call(
        body, name="merge_bwd", out_shape=(act, act, act, act, v1, v1), grid=(nd, T // tm),
        in_specs=[pl.BlockSpec((tm, ct), lambda j, i: (i, oa + j)), pl.BlockSpec((tm, ct), lambda j, i: (i, orr + j)),
                  vec, pl.BlockSpec((1, ct), lambda j, i: (0, nd + j)), blk, blk, blk],
        out_specs=(blk, blk, blk, blk, vec, vec),
        compiler_params=_cparams(("parallel", "arbitrary")),
    )(proj, proj, b_gate, b_gate, y_attn, y_rnn, d_m)


def _ln_fwd(x_res, delta, g, b, name, rider=None):
    T, D = x_res.shape
    tm = _pick(T, (256, 128))

    def body(x_ref, d_ref, g_ref, b_ref, y_ref, yb_ref, xh_ref, rs_ref):
        z = ALPHA * x_ref[...] + d_ref[...]
        mu = jnp.mean(z, axis=1, keepdims=True)
        zc = z - mu
        var = jnp.mean(zc * zc, axis=1, keepdims=True)
        rstd = lax.rsqrt(var + LN_EPS)
        xh = zc * rstd
        xh_ref[...] = xh
        rs_ref[...] = rstd
        y = xh * g_ref[...] + b_ref[...]
        y_ref[...] = y
        yb_ref[...] = y.astype(BF16)

    row = pl.BlockSpec((tm, D), lambda i: (i, 0))
    vec = pl.BlockSpec((1, D), lambda i: (0, 0))
    return _call(
        body, name=name,
        out_shape=(jax.ShapeDtypeStruct((T, D), F32), jax.ShapeDtypeStruct((T, D), BF16),
                   jax.ShapeDtypeStruct((T, D), F32), jax.ShapeDtypeStruct((T, 1), F32)),
        grid=(T // tm,), in_specs=[row, row, vec, vec],
        out_specs=(row, row, row, pl.BlockSpec((tm, 1), lambda i: (i, 0))),
        scratch_shapes=[], args=(x_res, delta, g, b), sem=("parallel",), rider=rider)


def _ln_bwd_rows(dy, xh, rstd, g):
    dxh = dy * g
    m1 = jnp.mean(dxh, axis=1, keepdims=True)
    m2 = jnp.mean(dxh * xh, axis=1, keepdims=True)
    return rstd * (dxh - m1 - xh * m2)


def _ln_loss_bwd(x_res, delta, g, b, target):
    T, D = x_res.shape
    tm = _pick(T, (256, 128))

    def body(x_ref, d_ref, g_ref, b_ref, t_ref, dz_ref, dzb_ref, loss_ref, dg_ref, db_ref):
        i = pl.program_id(0)

        @pl.when(i == 0)
        def _():
            loss_ref[...] = jnp.zeros_like(loss_ref)
            dg_ref[...] = jnp.zeros_like(dg_ref)
            db_ref[...] = jnp.zeros_like(db_ref)

        z = ALPHA * x_ref[...] + d_ref[...]
        mu = jnp.mean(z, axis=1, keepdims=True)
        zc = z - mu
        var = jnp.mean(zc * zc, axis=1, keepdims=True)
        rstd = lax.rsqrt(var + LN_EPS)
        xh = zc * rstd
        gv = g_ref[...]
        err = xh * gv + b_ref[...] - t_ref[...]
        loss_ref[...] += 0.5 * jnp.sum(jnp.mean(err * err, axis=1, keepdims=True))
        dy = err * (1.0 / D)
        dg_ref[...] += jnp.sum(dy * xh, axis=0, keepdims=True)
        db_ref[...] += jnp.sum(dy, axis=0, keepdims=True)
        dz = _ln_bwd_rows(dy, xh, rstd, gv)
        dz_ref[...] = dz
        dzb_ref[...] = dz.astype(BF16)

    row = pl.BlockSpec((tm, D), lambda i: (i, 0))
    vec = pl.BlockSpec((1, D), lambda i: (0, 0))
    return pl.pallas_call(
        body, name="ln2_loss_bwd",
        out_shape=(jax.ShapeDtypeStruct((T, D), F32), jax.ShapeDtypeStruct((T, D), BF16),
                   jax.ShapeDtypeStruct((8, LANES), F32),
                   jax.ShapeDtypeStruct((1, D), F32), jax.ShapeDtypeStruct((1, D), F32)),
        grid=(T // tm,), in_specs=[row, row, vec, vec, row],
        out_specs=(row, row, pl.BlockSpec((8, LANES), lambda i: (0, 0)), vec, vec),
        compiler_params=_cparams(("arbitrary",)),
    )(x_res, delta, g, b, target)


def _ln_bwd(dy, xh, rstd, g):
    T, D = dy.shape
    tm = _pick(T, (256, 128))

    def body(dy_ref, xh_ref, rs_ref, g_ref, dz_ref, dzb_ref, dg_ref, db_ref):
        i = pl.program_id(0)

        @pl.when(i == 0)
        def _():
            dg_ref[...] = jnp.zeros_like(dg_ref)
            db_ref[...] = jnp.zeros_like(db_ref)

        dyv, xhv = dy_ref[...], xh_ref[...]
        dg_ref[...] += jnp.sum(dyv * xhv, axis=0, keepdims=True)
        db_ref[...] += jnp.sum(dyv, axis=0, keepdims=True)
        dz = _ln_bwd_rows(dyv, xhv, rs_ref[...], g_ref[...])
        dz_ref[...] = dz
        dzb_ref[...] = dz.astype(BF16)

    row = pl.BlockSpec((tm, D), lambda i: (i, 0))
    vec = pl.BlockSpec((1, D), lambda i: (0, 0))
    return pl.pallas_call(
        body, name="ln1_bwd",
        out_shape=(jax.ShapeDtypeStruct((T, D), F32), jax.ShapeDtypeStruct((T, D), BF16),
                   jax.ShapeDtypeStruct((1, D), F32), jax.ShapeDtypeStruct((1, D), F32)),
        grid=(T // tm,), in_specs=[row, row, pl.BlockSpec((tm, 1), lambda i: (i, 0)), vec],
        out_specs=(row, row, vec, vec), compiler_params=_cparams(("arbitrary",)),
    )(dy, xh, rstd, g)


def _ffn_col_tile(T, d_ff):
    return _pick(d_ff, (256, 128)) if T >= 1024 else _pick(d_ff, (512, 256, 128))


def _ffn_gate(gp, cw_ref, cb_ref):
    return (cb_ref[...] + gp * cw_ref[2:3, :] + _shift_down(gp, 1) * cw_ref[1:2, :]
            + _shift_down(gp, 2) * cw_ref[0:1, :])


def _ffn_fwd(up, gpre, conv_w, conv_b, rider=None):
    T, d_ff = up.shape
    ct = _ffn_col_tile(T, d_ff)

    def body(up_ref, gp_ref, cw_ref, cb_ref, f_ref):
        gate = _ffn_gate(gp_ref[...], cw_ref, cb_ref)
        f_ref[...] = (_gelu(gate) * up_ref[...]).astype(BF16)

    col = pl.BlockSpec((T, ct), lambda j: (0, j))
    (f,), carried = _call(
        body, name="ffn_act_fwd", out_shape=[jax.ShapeDtypeStruct((T, d_ff), BF16)], grid=(d_ff // ct,),
        in_specs=[col, col, pl.BlockSpec((3, ct), lambda j: (0, j)), pl.BlockSpec((1, ct), lambda j: (0, j))],
        out_specs=[col], scratch_shapes=[], args=(up, gpre, conv_w, conv_b), sem=("parallel",), rider=rider)
    return f, carried


def _ffn_bwd(up, gpre, conv_w, conv_b, d_f, after=None):
    T, d_ff = up.shape
    ct = _ffn_col_tile(T, d_ff)

    def body(up_ref, gp_ref, cw_ref, cb_ref, df_ref, dup_ref, dgp_ref, dcw_ref, dcb_ref):
        gp = gp_ref[...]
        gate = _ffn_gate(gp, cw_ref, cb_ref)
        gel, dgel = _gelu_and_grad(gate)
        df = df_ref[...]
        dup_ref[...] = (df * gel).astype(BF16)
        dgate = df * up_ref[...] * dgel
        dcb_ref[...] = jnp.sum(dgate, axis=0, keepdims=True)
        dcw_ref[2:3, :] = jnp.sum(dgate * gp, axis=0, keepdims=True)
        dcw_ref[1:2, :] = jnp.sum(dgate * _shift_down(gp, 1), axis=0, keepdims=True)
        dcw_ref[0:1, :] = jnp.sum(dgate * _shift_down(gp, 2), axis=0, keepdims=True)
        dgp = (dgate * cw_ref[2:3, :] + _shift_up(dgate, 1) * cw_ref[1:2, :]
               + _shift_up(dgate, 2) * cw_ref[0:1, :])
        dgp_ref[...] = dgp.astype(BF16)

    col = pl.BlockSpec((T, ct), lambda j: (0, j))
    w3 = pl.BlockSpec((3, ct), lambda j: (0, j))
    v1 = pl.BlockSpec((1, ct), lambda j: (0, j))
    return _call(
        body, name="ffn_act_bwd",
        out_shape=(jax.ShapeDtypeStruct((T, d_ff), BF16), jax.ShapeDtypeStruct((T, d_ff), BF16),
                   jax.ShapeDtypeStruct((3, d_ff), F32), jax.ShapeDtypeStruct((1, d_ff), F32)),
        grid=(d_ff // ct,), in_specs=[col, col, w3, v1, col], out_specs=(col, col, w3, v1),
        scratch_shapes=[], args=(up, gpre, conv_w, conv_b, d_f), sem=("parallel",), after=after)[0]


def _adamw(w, g, m, v, name, after=None):
    R, C = w.shape
    tr = _row_tile(R, C * 4, 8, budget=1280 * 1024)
    c1 = 1.0 / (1.0 - ADAM_B1 ** ADAM_STEP)
    c2 = 1.0 / (1.0 - ADAM_B2 ** ADAM_STEP)

    def body(w_ref, g_ref, m_ref, v_ref, go_ref, d_ref, nm_ref, nv_ref):
        gv = g_ref[...]
        go_ref[...] = gv
        nm = ADAM_B1 * m_ref[...] + (1.0 - ADAM_B1) * gv
        nv = ADAM_B2 * v_ref[...] + (1.0 - ADAM_B2) * (gv * gv)
        nm_ref[...] = nm
        nv_ref[...] = nv
        d_ref[...] = -ADAM_LR * ((nm * c1) / (jnp.sqrt(nv * c2) + ADAM_EPS) + ADAM_WD * w_ref[...])

    blk = pl.BlockSpec((tr, C), lambda r: (r, 0))
    sh = jax.ShapeDtypeStruct((R, C), F32)
    return _call(body, name=name, out_shape=(sh,) * 4, grid=(R // tr,), in_specs=[blk] * 4, out_specs=(blk,) * 4,
                 scratch_shapes=[], args=(w, g, m, v), sem=("parallel",), after=after)[0]


def _group_blocks(w_blocks, per):
    nb, bw, _ = w_blocks.shape
    G = nb // per
    w4 = w_blocks.reshape(G, per, bw, bw)
    rows = []
    for p in range(per):
        parts = [w4[:, p] if q == p else jnp.zeros((G, bw, bw), w_blocks.dtype) for q in range(per)]
        rows.append(jnp.concatenate(parts, axis=2))
    return jnp.concatenate(rows, axis=1)


def _ungroup_blocks(w_groups, per):
    G, gw, _ = w_groups.shape
    bw = gw // per
    blocks = [w_groups[:, p * bw:(p + 1) * bw, p * bw:(p + 1) * bw] for p in range(per)]
    return jnp.stack(blocks, axis=1).reshape(G * per, bw, bw)


def _pack(parts):
    flat = jnp.concatenate([p.reshape(-1).astype(F32) for p in parts])
    n = flat.shape[0]
    rows = -(-n // LANES)
    rows = -(-rows // PACK_ROW_MULT) * PACK_ROW_MULT
    flat = jnp.pad(flat, (0, rows * LANES - n))
    return flat.reshape(rows, LANES)


def _unpack(packed, shapes):
    flat = packed.reshape(-1)
    out, off = [], 0
    for s in shapes:
        n = math.prod(s)
        out.append(flat[off:off + n].reshape(s))
        off += n
    return out


def kernel(x, w_in, b_gate, rnn_conv_w, rnn_conv_b, lru_wa, lru_ba, lru_wi, lru_bi, lru_lambda, attn_sinks, w_attn_proj, w_rnn_proj, w_out, ln1_g, ln1_b, ffn_w_up, ffn_w_gate, ffn_conv_w, ffn_conv_b, ffn_w_down, ln2_g, ln2_b, loss_target, m_w_in, m_b_gate, m_rnn_conv_w, m_rnn_conv_b, m_lru_wa, m_lru_ba, m_lru_wi, m_lru_bi, m_lru_lambda, m_attn_sinks, m_w_attn_proj, m_w_rnn_proj, m_w_out, m_ln1_g, m_ln1_b, m_ffn_w_up, m_ffn_w_gate, m_ffn_conv_w, m_ffn_conv_b, m_ffn_w_down, m_ln2_g, m_ln2_b, v_w_in, v_b_gate, v_rnn_conv_w, v_rnn_conv_b, v_lru_wa, v_lru_ba, v_lru_wi, v_lru_bi, v_lru_lambda, v_attn_sinks, v_w_attn_proj, v_w_rnn_proj, v_w_out, v_ln1_g, v_ln1_b, v_ffn_w_up, v_ffn_w_gate, v_ffn_conv_w, v_ffn_conv_b, v_ffn_w_down, v_ln2_g, v_ln2_b):
    weights = dict(w_in=w_in, b_gate=b_gate, rnn_conv_w=rnn_conv_w, rnn_conv_b=rnn_conv_b, lru_wa=lru_wa,
                   lru_ba=lru_ba, lru_wi=lru_wi, lru_bi=lru_bi, lru_lambda=lru_lambda, attn_sinks=attn_sinks,
                   w_attn_proj=w_attn_proj, w_rnn_proj=w_rnn_proj, w_out=w_out, ln1_g=ln1_g, ln1_b=ln1_b,
                   ffn_w_up=ffn_w_up, ffn_w_gate=ffn_w_gate, ffn_conv_w=ffn_conv_w, ffn_conv_b=ffn_conv_b,
                   ffn_w_down=ffn_w_down, ln2_g=ln2_g, ln2_b=ln2_b)
    m_in = dict(w_in=m_w_in, b_gate=m_b_gate, rnn_conv_w=m_rnn_conv_w, rnn_conv_b=m_rnn_conv_b, lru_wa=m_lru_wa,
                lru_ba=m_lru_ba, lru_wi=m_lru_wi, lru_bi=m_lru_bi, lru_lambda=m_lru_lambda, attn_sinks=m_attn_sinks,
                w_attn_proj=m_w_attn_proj, w_rnn_proj=m_w_rnn_proj, w_out=m_w_out, ln1_g=m_ln1_g, ln1_b=m_ln1_b,
                ffn_w_up=m_ffn_w_up, ffn_w_gate=m_ffn_w_gate, ffn_conv_w=m_ffn_conv_w, ffn_conv_b=m_ffn_conv_b,
                ffn_w_down=m_ffn_w_down, ln2_g=m_ln2_g, ln2_b=m_ln2_b)
    v_in = dict(w_in=v_w_in, b_gate=v_b_gate, rnn_conv_w=v_rnn_conv_w, rnn_conv_b=v_rnn_conv_b, lru_wa=v_lru_wa,
                lru_ba=v_lru_ba, lru_wi=v_lru_wi, lru_bi=v_lru_bi, lru_lambda=v_lru_lambda, attn_sinks=v_attn_sinks,
                w_attn_proj=v_w_attn_proj, w_rnn_proj=v_w_rnn_proj, w_out=v_w_out, ln1_g=v_ln1_g, ln1_b=v_ln1_b,
                ffn_w_up=v_ffn_w_up, ffn_w_gate=v_ffn_w_gate, ffn_conv_w=v_ffn_conv_w, ffn_conv_b=v_ffn_conv_b,
                ffn_w_down=v_ffn_w_down, ln2_g=v_ln2_g, ln2_b=v_ln2_b)
    order = list(weights)

    assert x.shape[0] == 1 and w_in.shape[0] == 1, "one sequence per device, depth 1"
    T, D = x.shape[1], x.shape[2]
    nq = attn_sinks.shape[-1]
    nkv = nq // GROUP
    d_attn, d_kv = nq * HEAD_DIM, nkv * HEAD_DIM
    d_rnn = rnn_conv_b.shape[-1]
    d_ff = ffn_conv_b.shape[-1]
    n_blocks, bw = lru_wa.shape[1], lru_wa.shape[2]
    per = (bw * LANES // math.gcd(bw, LANES)) // bw
    gw = per * bw
    assert n_blocks % per == 0 and d_rnn == n_blocks * bw
    q_off, k_off, v_off = 0, d_attn, d_attn + d_kv
    rx_off = d_attn + 2 * d_kv
    ry_off = rx_off + d_rnn
    gl_off = ry_off + d_rnn
    d_in = gl_off + 2 * D
    assert w_in.shape[-1] * N_SHARDS == d_in
    assert k_off % d_kv == 0 and rx_off % gw == 0 and T % ATTN_BLOCK == 0

    xi, yi, ci = lax.axis_index("x"), lax.axis_index("y"), lax.axis_index("c")
    j_me = 2 * xi + yi
    jc_arr = jnp.stack([j_me, ci]).astype(jnp.int32)

    x0 = x[0]
    tgt = loss_target[0]
    big = ["w_in", "w_attn_proj", "w_rnn_proj", "w_out", "ffn_w_up", "ffn_w_gate", "ffn_w_down"]
    near, diag = (0, 1), (2,)
    order_arr = jnp.stack([j_me, j_me ^ 2, j_me ^ 1, j_me ^ 3]).astype(jnp.int32)

    rcw_s, fcw_s = _all_gather_small([rnn_conv_w[0], ffn_conv_w[0]])
    rcw = jnp.concatenate([rcw_s[j] for j in range(N_SHARDS)], axis=1)
    fcw = jnp.concatenate([fcw_s[j] for j in range(N_SHARDS)], axis=1)

    own = {"w_in": _cast_bf16_into_slot(w_in[0], jc_arr, "cast_w_in", fcw_s)}
    in_near = _gather_step(own["w_in"], fcw_s, "gather_start_w_in_near", start_d2d=False, relations=near)
    in_diag = _gather_step(in_near[1], in_near[2], "gather_start_w_in_diag", start_d2d=False, relations=diag)
    last = in_diag[2]
    for n in big[1:]:
        own[n] = last = _cast_bf16_into_slot(weights[n][0], jc_arr, "cast_" + n, last)
    x0b = _cast_bf16(x0, "cast_x", last)

    wa_g = _group_blocks(lru_wa[0], per).astype(BF16)
    wi_g = _group_blocks(lru_wi[0], per).astype(BF16)

    proj = _mm_shards(x0b, in_diag[1], order_arr, [0], "mm_proj_own", x0b)
    fw = _gather_step(in_diag[1], proj, "gather_forward_w_in_near", sems_in=in_near[0], start_d2d=True,
                      relations=near)
    w_in_s = _gather_step(fw[1], fw[2], "gather_finish_w_in_near", sems_in=fw[0], relations=near)
    proj = _mm_shards(x0b, w_in_s, order_arr, [1, 2], "mm_proj_near", w_in_s, out=proj)
    fw = _gather_step(w_in_s, proj, "gather_forward_w_in_diag", sems_in=in_diag[0], start_d2d=True, relations=diag)
    w_in_s = _gather_step(fw[1], fw[2], "gather_finish_w_in_diag", sems_in=fw[0], relations=diag)
    proj = _mm_shards(x0b, w_in_s, order_arr, [3], "mm_proj_diag", w_in_s, out=proj)
    ici, last = {}, proj
    for n in big[1:]:
        ici[n] = _gather_step(own[n], last, "gather_start_" + n, start_d2d=False)
        last = ici[n][2]

    def forward_halves(n, after):
        sems, buf, _ = ici[n]
        return _gather_step(buf, after, "gather_forward_" + n, sems_in=sems, start_d2d=True)

    def gathered(d2d, after, n):
        sems, buf, _ = d2d
        return _gather_step(buf, after, "gather_finish_" + n, sems_in=sems)

    a_out = _attn_fwd(proj, attn_sinks, nq, (q_off, k_off, v_off), after=last)
    fw_ap = forward_halves("w_attn_proj", a_out)
    (b_out, h_all), _ = _rnn_fwd(proj, (rx_off, ry_off), rcw, rnn_conv_b, wa_g, wi_g, lru_ba, lru_bi, lru_lambda)
    fw_rp = forward_halves("w_rnn_proj", b_out)
    w_ap = gathered(fw_ap, b_out, "w_attn_proj").reshape(d_attn, D)
    y_attn = _mm(a_out, w_ap, name="mm_attn_proj")
    fw_o = forward_halves("w_out", y_attn)
    w_rp = gathered(fw_rp, y_attn, "w_rnn_proj").reshape(d_rnn, D)
    y_rnn = _mm(b_out, w_rp, name="mm_rnn_proj")
    merged, _ = _merge_fwd(proj, gl_off, b_gate, y_attn, y_rnn)
    w_o = gathered(fw_o, merged, "w_out").reshape(D, D)
    mix = _mm(merged, w_o, name="mm_out")
    fw_up = forward_halves("ffn_w_up", mix)
    (x1, x1b, xh1, rstd1), _ = _ln_fwd(x0, mix, ln1_g, ln1_b, "ln1_fwd")
    w_up_s = gathered(fw_up, x1b, "ffn_w_up")
    up = _mm(x1b, w_up_s, name="mm_up", b_shards=N_SHARDS)
    fw_gate = forward_halves("ffn_w_gate", up)
    w_gate_s = gathered(fw_gate, fw_gate[2], "ffn_w_gate")
    gpre = _mm(x1b, w_gate_s, name="mm_gate", b_shards=N_SHARDS)
    f_act, _ = _ffn_fwd(up, gpre, fcw, ffn_conv_b)
    fw_dn = forward_halves("ffn_w_down", f_act)
    w_dn = gathered(fw_dn, fw_dn[2], "ffn_w_down").reshape(d_ff, D)
    f_out = _mm(f_act, w_dn, name="mm_down")
    dz2, dz2b, loss_acc, dg2, db2 = _ln_loss_bwd(x1, f_out, ln2_g, ln2_b, tgt)

    def pair_sums(arrs, from_sibling, names):
        return [_pair_sum(g, la, jc_arr, "pair_sum_" + n) for g, la, n in zip(arrs, from_sibling, names)]

    def shard_sums(parts, landed, names):
        return [_shard_sum(cp, lb, jc_arr, "shard_sum_" + n) for cp, lb, n in zip(parts, landed, names)]

    halves = {}
    g_down = _mm(f_act, dz2b, name="mm_d_w_down", ta=True, out_dtype=BF16)
    g1 = [g_down.reshape(N_SHARDS, d_ff // N_SHARDS, D)]
    d_f, sib1 = _mm(dz2b, w_dn, name="mm_d_f", tb=True, rider=_pair_rider(g1))
    sent1 = _shard_exchange_start(pair_sums(g1, sib1, ["ffn_w_down"]), "shard_exchange_start_down")
    dup, dgp, d_fcw, d_fcb = _ffn_bwd(up, gpre, fcw, ffn_conv_b, d_f, after=sent1[4])
    g_up = _mm(x1b, dup, name="mm_d_w_up", ta=True, out_dtype=BF16, out_shards=N_SHARDS)
    g_gate = _mm(x1b, dgp, name="mm_d_w_gate", ta=True, out_dtype=BF16, out_shards=N_SHARDS)
    g2 = [g_up, g_gate]
    dx1_a, sib2 = _mm(dup, w_up_s, name="mm_dx1_up", tb=True, b_shards=N_SHARDS, adds=((ALPHA, dz2),),
                      rider=_pair_rider(g2))
    halves["ffn_w_down"], = shard_sums(*_shard_exchange_wait(sent1, dx1_a, "shard_exchange_wait_down"),
                                       ["ffn_w_down"])
    sent2 = _shard_exchange_start(pair_sums(g2, sib2, ["ffn_w_up", "ffn_w_gate"]), "shard_exchange_start_up_gate")
    dx1 = _mm(dgp, w_gate_s, name="mm_dx1_gate", tb=True, b_shards=N_SHARDS, adds=((1.0, dx1_a),), after=sent2[4])
    dz1, dz1b, dg1, db1 = _ln_bwd(dx1, xh1, rstd1, ln1_g)
    g_out = _mm(merged, dz1b, name="mm_d_w_out", ta=True, out_dtype=BF16)
    d_m = _mm(dz1b, w_o, name="mm_d_merged", tb=True)
    dya, dyr, dgl_a, dgl_r, dbg_a, dbg_r = _merge_bwd(proj, gl_off, b_gate, y_attn, y_rnn, d_m)
    g_ap = _mm(a_out, dya, name="mm_d_w_attn_proj", ta=True, out_dtype=BF16)
    g_rp = _mm(b_out, dyr, name="mm_d_w_rnn_proj", ta=True, out_dtype=BF16)
    names3 = ["w_out", "w_attn_proj", "w_rnn_proj"]
    g3 = [g_out.reshape(N_SHARDS, D // N_SHARDS, D), g_ap.reshape(N_SHARDS, d_attn // N_SHARDS, D),
          g_rp.reshape(N_SHARDS, d_rnn // N_SHARDS, D)]
    d_a = _mm(dya, w_ap, name="mm_d_attn", tb=True)
    d_b, sib3 = _mm(dyr, w_rp, name="mm_d_rnn", tb=True, rider=_pair_rider(g3))
    sent3 = _shard_exchange_start(pair_sums(g3, sib3, names3), "shard_exchange_start_mixers")
    dq, dk, dv, dsink = _attn_bwd(proj, d_a, attn_sinks, nq, (q_off, k_off, v_off), after=sent3[4])
    (drx, dry, d_rcw, d_rcb, d_ba, d_bi, d_lam, d_wa_g, d_wi_g), _ = _rnn_bwd(
        proj, (rx_off, ry_off), h_all, d_b, rcw, rnn_conv_b, wa_g, wi_g, lru_ba, lru_bi, lru_lambda)
    halves["ffn_w_up"], halves["ffn_w_gate"] = shard_sums(
        *_shard_exchange_wait(sent2, drx, "shard_exchange_wait_up_gate"), ["ffn_w_up", "ffn_w_gate"])
    d_proj = jnp.concatenate([dq, dk.astype(BF16), dv.astype(BF16), drx, dry, dgl_a, dgl_r], axis=1)
    ffn_names = ["ffn_w_down", "ffn_w_up", "ffn_w_gate"]
    g_in, shared_ffn = _mm(x0b, d_proj, name="mm_d_w_in", ta=True, out_dtype=BF16, out_shards=N_SHARDS,
                           rider=_share_rider([halves[n] for n in ffn_names]))
    halves["w_out"], halves["w_attn_proj"], halves["w_rnn_proj"] = shard_sums(
        *_shard_exchange_wait(sent3, g_in, "shard_exchange_wait_mixers"), names3)

    small_parts = [
        ("loss", loss_acc[0:1, 0:1]),
        ("b_gate", jnp.concatenate([dbg_a, dbg_r], axis=1)),
        ("rnn_conv_w", d_rcw), ("rnn_conv_b", d_rcb),
        ("lru_wa", _ungroup_blocks(d_wa_g, per)), ("lru_ba", d_ba),
        ("lru_wi", _ungroup_blocks(d_wi_g, per)), ("lru_bi", d_bi), ("lru_lambda", d_lam),
        ("attn_sinks", dsink[0:1, 0:nq]),
        ("ln1_g", dg1), ("ln1_b", db1),
        ("ffn_conv_w", d_fcw), ("ffn_conv_b", d_fcb),
        ("ln2_g", dg2), ("ln2_b", db2),
    ]
    packed = _pack([p for _, p in small_parts])
    rs = packed.shape[0]

    def whole(g):
        return g.reshape(2 * g.shape[1], g.shape[2])

    grads = {n: whole(g) for n, g in zip(ffn_names, shared_ffn)}
    out_g, out_d, out_m, out_v = {}, {}, {}, {}

    def adamw(n, after=None):
        shape = weights[n].shape
        two_d = (math.prod(shape[:-1]), shape[-1])
        g2, d2, m2, v2 = _adamw(weights[n].reshape(two_d), grads[n].reshape(two_d), m_in[n].reshape(two_d),
                                v_in[n].reshape(two_d), "adamw_" + n, after=after)
        out_g[n], out_d[n] = g2.reshape(shape), d2.reshape(shape)
        out_m[n], out_v[n] = m2.reshape(shape), v2.reshape(shape)

    g4 = [g_in, packed.reshape(N_SHARDS, rs // N_SHARDS, LANES)]
    sib4 = _run_rider(_pair_rider(g4), "pair_exchange_in_small")
    part4 = pair_sums(g4, sib4, ["w_in", "small"])
    grad_x, (lb_in, lb_small, *shared_mix) = _mm(
        d_proj, w_in_s, name="mm_d_x", tb=True, b_shards=N_SHARDS, adds=((ALPHA, dz1),),
        rider=_join_riders(_shard_exchange_rider(part4, _atoms([0], near) + _atoms([1])),
                           _share_rider([halves[n] for n in names3])))
    grads.update({n: whole(g) for n, g in zip(names3, shared_mix)})
    sent5 = _shard_exchange_start(part4[:1], "shard_exchange_start_in_diag", relations=diag, lands=[lb_in])
    for n in ffn_names + names3:
        adamw(n, after=sent5[4])
    (part_in,), (lb_in,) = _shard_exchange_wait(sent5, out_d[names3[-1]], "shard_exchange_wait_in_diag")
    part_small = part4[1]
    halves["w_in"], = shard_sums([part_in], [lb_in], ["w_in"])
    eighths = _shard_sum(part_small, lb_small, jc_arr, "shard_sum_small", all_slots=True)
    shared_in, reduced = _run_rider(_share_rider([halves["w_in"]], eighths), "share_in_small")
    grads["w_in"] = whole(shared_in)
    reduced = reduced.reshape(rs, LANES)
    small = dict(zip([n for n, _ in small_parts], _unpack(reduced, [p.shape for _, p in small_parts])))
    loss = small.pop("loss").reshape(())
    rcw_n = d_rnn // N_SHARDS
    fcw_n = d_ff // N_SHARDS
    small["rnn_conv_w"] = lax.dynamic_slice(small["rnn_conv_w"], (0, j_me * rcw_n), (4, rcw_n))
    small["ffn_conv_w"] = lax.dynamic_slice(small["ffn_conv_w"], (0, j_me * fcw_n), (3, fcw_n))
    for n, g in small.items():
        grads[n] = g

    for n in order:
        if n not in out_g:
            adamw(n)

    return (loss, grad_x.reshape(x.shape), *[out_g[n] for n in order], *[out_d[n] for n in order],
            *[out_m[n] for n in order], *[out_v[n] for n in order])
```

```python
import functools
import math

import jax
import jax.numpy as jnp
from jax import lax
from jax.experimental import pallas as pl
from jax.experimental.pallas import tpu as pltpu

F32 = jnp.float32
BF16 = jnp.bfloat16
MESH = pl.DeviceIdType.MESH

HEAD_DIM = 64
GROUP = 8
ATTN_BLOCK = 128
LRU_C = 8.0
LN_EPS = 1e-5
ALPHA = 2.0 ** 0.25
LANES = 128
N_SHARDS = 4
N_DEV = 8
VMEM_LIMIT = 56 * 1024 * 1024
MM_VMEM_BUDGET = 40 * 1024 * 1024
MM_MAX_TILE = 3072
PACK_ROW_MULT = 8 * 64
NEG = -1e30

ADAM_LR, ADAM_B1, ADAM_B2, ADAM_EPS, ADAM_WD, ADAM_STEP = 0.001, 0.9, 0.999, 1e-08, 0.01, 10

GELU_C = math.sqrt(2.0 / math.pi)
GELU_A = 0.044715


def _cparams(sem=None):
    kw = dict(vmem_limit_bytes=VMEM_LIMIT)
    if sem is not None:
        kw["dimension_semantics"] = sem
    return pltpu.CompilerParams(**kw)


def _pick(n, prefs):
    for p in prefs:
        if n % p == 0:
            return p
    return n


def _row_tile(rows, row_bytes, mult, budget=2 * 1024 * 1024):
    best = None
    for d in range(mult, rows + 1, mult):
        if rows % d == 0 and d * row_bytes <= budget:
            best = d
    return best if best is not None else rows


def _gelu(x):
    return 0.5 * x * (1.0 + jnp.tanh(GELU_C * (x + GELU_A * x * x * x)))


def _gelu_and_grad(x):
    t = jnp.tanh(GELU_C * (x + GELU_A * x * x * x))
    g = 0.5 * x * (1.0 + t)
    dg = 0.5 * (1.0 + t) + 0.5 * x * (1.0 - t * t) * GELU_C * (1.0 + 3.0 * GELU_A * x * x)
    return g, dg


def _shift_down(x, s, fill=0.0):
    row = lax.broadcasted_iota(jnp.int32, x.shape, 0)
    return jnp.where(row >= s, pltpu.roll(x, s, 0), fill)


def _shift_up(x, s, fill=0.0):
    n = x.shape[0]
    row = lax.broadcasted_iota(jnp.int32, x.shape, 0)
    return jnp.where(row < n - s, pltpu.roll(x, n - s, 0), fill)


def _mm(a, b, *, name, ta=False, tb=False, out_dtype=F32, adds=(), b_shards=1, out_shards=1,
        tm=None, tn=None, tk=None, rider=None, after=None):
    if ta:
        K, M = a.shape
    else:
        M, K = a.shape
    if b_shards > 1:
        n_sh = b.shape[-1]
        if tb:
            N = b.shape[1]
            assert b_shards * n_sh == K
        else:
            N = b_shards * n_sh
            assert b.shape[1] == K
    else:
        n_sh = None
        if tb:
            N = b.shape[0]
            assert b.shape[1] == K
        else:
            N = b.shape[1]
            assert b.shape[0] == K
    wide = (1024, 1536, 1280, 768, 640, 512, 256, 128)
    if tn is None:
        if b_shards > 1 and not tb:
            tn = n_sh if n_sh <= MM_MAX_TILE else _pick(n_sh, wide)
        elif out_shards > 1:
            tn = N // out_shards if N // out_shards <= MM_MAX_TILE else _pick(N // out_shards, wide)
        else:
            tn = _pick(N, wide)
    if tk is None:
        if b_shards > 1 and tb:
            tk = n_sh if n_sh <= MM_MAX_TILE else _pick(n_sh, wide)
        else:
            tk = K if K <= MM_MAX_TILE else _pick(K, (2048,) + wide)
    assert N % tn == 0 and K % tk == 0, (name, M, N, K, tn, tk)
    nk = K // tk
    n_add = len(adds)
    sa, sb, so = a.dtype.itemsize, b.dtype.itemsize, jnp.dtype(out_dtype).itemsize

    def vmem_bytes(tm_):
        return (2 * (tm_ * tk * sa + tk * tn * sb + tm_ * tn * so + n_add * tm_ * tn * 4)
                + (tm_ * tn * 4 if nk > 1 else 0))

    if tm is None:
        tm = _pick(M, (1024, 512, 256, 128)) if nk > 1 else _pick(M, (512, 256, 128))
        while vmem_bytes(tm) > MM_VMEM_BUDGET and tm % 256 == 0:
            tm //= 2
    assert M % tm == 0, (name, M, tm)
    b_outer = b.size * sb >= a.size * sa

    def ij(g0, g1):
        return (g1, g0) if b_outer else (g0, g1)

    def amap(g0, g1, k):
        i, _ = ij(g0, g1)
        return (k, i) if ta else (i, k)

    def bmap(g0, g1, k):
        _, j = ij(g0, g1)
        if b_shards > 1 and not tb:
            per = n_sh // tn
            return (j // per, k, j % per)
        if b_shards > 1 and tb:
            per = n_sh // tk
            return (k // per, j, k % per)
        return (j, k) if tb else (k, j)

    def omap(g0, g1, k):
        i, j = ij(g0, g1)
        if out_shards > 1:
            per_o = (N // out_shards) // tn
            return (j // per_o, i, j % per_o)
        return (i, j)

    a_spec = pl.BlockSpec((tk, tm) if ta else (tm, tk), amap)
    if b_shards > 1:
        b_spec = pl.BlockSpec((None, tn, tk) if tb else (None, tk, tn), bmap)
    else:
        b_spec = pl.BlockSpec((tn, tk) if tb else (tk, tn), bmap)
    add_specs = [pl.BlockSpec((tm, tn), lambda g0, g1, k: ij(g0, g1)) for _ in adds]
    if out_shards > 1:
        out_spec = pl.BlockSpec((None, tm, tn), omap)
        out_shape = jax.ShapeDtypeStruct((out_shards, M, N // out_shards), out_dtype)
    else:
        out_spec = pl.BlockSpec((tm, tn), omap)
        out_shape = jax.ShapeDtypeStruct((M, N), out_dtype)

    if ta:
        dims = (((0,), (0,)), ((), ()))
    elif tb:
        dims = (((1,), (1,)), ((), ()))
    else:
        dims = (((1,), (0,)), ((), ()))
    scales = tuple(s for s, _ in adds)

    def finish(r, add_refs, o_ref):
        for s, ref in zip(scales, add_refs):
            r = r + s * ref[...].astype(F32)
        o_ref[...] = r.astype(out_dtype)

    def body(a_ref, b_ref, *rest):
        add_refs = rest[:n_add]
        o_ref = rest[n_add]
        part = lax.dot_general(a_ref[...].astype(BF16), b_ref[...].astype(BF16), dims, preferred_element_type=F32)
        if nk == 1:
            finish(part, add_refs, o_ref)
            return
        acc = rest[n_add + 1]
        k = pl.program_id(2)

        @pl.when(k == 0)
        def _():
            acc[...] = part

        @pl.when(k > 0)
        def _():
            acc[...] += part

        @pl.when(k == nk - 1)
        def _():
            finish(acc[...], add_refs, o_ref)

    grid = (N // tn, M // tm, nk) if b_outer else (M // tm, N // tn, nk)
    (res,), carried = _call(
        body, name=name, grid=grid, in_specs=[a_spec, b_spec] + add_specs, out_specs=[out_spec],
        out_shape=[out_shape], scratch_shapes=[pltpu.VMEM((tm, tn), F32)] if nk > 1 else [],
        args=(a, b, *[x for _, x in adds]), sem=("parallel", "parallel", "arbitrary"), rider=rider, after=after)
    return (res, carried) if rider is not None else res


def _cast_bf16(w, name, after):
    R, C = w.shape
    tr = _row_tile(R, C * 4, 16)

    def body(w_ref, after_ref, o_ref):
        o_ref[...] = w_ref[...].astype(BF16)

    return pl.pallas_call(
        body, name=name, out_shape=jax.ShapeDtypeStruct((R, C), BF16), grid=(R // tr,),
        in_specs=[pl.BlockSpec((tr, C), lambda r: (r, 0)), pl.BlockSpec(memory_space=pl.ANY)],
        out_specs=pl.BlockSpec((tr, C), lambda r: (r, 0)), compiler_params=_cparams(("parallel",)),
    )(w, after)


def _cast_bf16_into_slot(w, jc_arr, name, after):
    R, C = w.shape
    tr = _row_tile(R, C * 4, 16)

    def body(jc_ref, w_ref, after_ref, o_ref):
        o_ref[...] = w_ref[...].astype(BF16)

    gs = pltpu.PrefetchScalarGridSpec(
        num_scalar_prefetch=1, grid=(R // tr,),
        in_specs=[pl.BlockSpec((tr, C), lambda r, jc: (r, 0)), pl.BlockSpec(memory_space=pl.ANY)],
        out_specs=pl.BlockSpec((None, tr, C), lambda r, jc: (jc[0], r, 0)))
    return pl.pallas_call(body, name=name, out_shape=jax.ShapeDtypeStruct((N_SHARDS, R, C), BF16), grid_spec=gs,
                          compiler_params=_cparams(("parallel",)))(jc_arr, w, after)


def _pair_sum(g, la, jc_arr, name):
    S, R, C = g.shape
    half = R // 2
    tr = _row_tile(half, C * 4, 16)
    nrt = half // tr
    dt = g.dtype

    def body(jc_ref, g_ref, la_ref, o_ref):
        o_ref[...] = (g_ref[...].astype(F32) + la_ref[...].astype(F32)).astype(dt)

    gs = pltpu.PrefetchScalarGridSpec(
        num_scalar_prefetch=1, grid=(S, nrt),
        in_specs=[pl.BlockSpec((None, tr, C), lambda s, r, jc: (s, jc[1] * nrt + r, 0)),
                  pl.BlockSpec((None, tr, C), lambda s, r, jc: (s, r, 0))],
        out_specs=pl.BlockSpec((None, tr, C), lambda s, r, jc: (s, r, 0)))
    return pl.pallas_call(body, name=name, out_shape=jax.ShapeDtypeStruct((S, half, C), dt), grid_spec=gs,
                          compiler_params=_cparams(("parallel", "parallel")))(jc_arr, g, la)


def _shard_sum(cp, lb, jc_arr, name, all_slots=False):
    S, h, C = cp.shape
    tr = _row_tile(h, C * 4, 16)

    def body(jc_ref, cp_ref, l0, l1, l2, o_ref):
        o_ref[...] = ((cp_ref[...].astype(F32) + l0[...].astype(F32)) + l1[...].astype(F32)) + l2[...].astype(F32)

    def lspec(kk):
        return pl.BlockSpec((None, tr, C), lambda r, jc: (kk, r, 0))

    if all_slots:
        out_spec = pl.BlockSpec((None, None, tr, C), lambda r, jc: (jc[0], jc[1], r, 0))
        out_shape = jax.ShapeDtypeStruct((S, 2, h, C), F32)
    else:
        out_spec = pl.BlockSpec((None, tr, C), lambda r, jc: (jc[1], r, 0))
        out_shape = jax.ShapeDtypeStruct((2, h, C), F32)
    gs = pltpu.PrefetchScalarGridSpec(
        num_scalar_prefetch=1, grid=(h // tr,),
        in_specs=[pl.BlockSpec((None, tr, C), lambda r, jc: (jc[0], r, 0)), lspec(0), lspec(1), lspec(2)],
        out_specs=out_spec)
    return pl.pallas_call(body, name=name, out_shape=out_shape, grid_spec=gs,
                          compiler_params=_cparams(("parallel",)))(jc_arr, cp, lb, lb, lb)


ANY = pl.BlockSpec(memory_space=pl.ANY)


def _place():
    x, y, c = lax.axis_index("x"), lax.axis_index("y"), lax.axis_index("c")
    chips = [(1 - x, y), (x, 1 - y), (1 - x, 1 - y)]
    return x, y, c, chips


class _Rider:
    def __init__(self, inputs, out_shape, aliases, sems, start, finish):
        self.inputs, self.out_shape, self.aliases, self.sems = list(inputs), list(out_shape), dict(aliases), list(sems)
        self.start, self.finish = start, finish


def _join_riders(r1, r2):
    i1, o1, s1 = len(r1.inputs), len(r1.out_shape), len(r1.sems)
    aliases = dict(r1.aliases)
    aliases.update({i1 + i: o1 + o for i, o in r2.aliases.items()})

    def start(ins, outs, sems):
        r1.start(ins[:i1], outs[:o1], sems[:s1])
        r2.start(ins[i1:], outs[o1:], sems[s1:])

    def finish(ins, outs, sems):
        r1.finish(ins[:i1], outs[:o1], sems[:s1])
        r2.finish(ins[i1:], outs[o1:], sems[s1:])

    return _Rider(r1.inputs + r2.inputs, r1.out_shape + r2.out_shape, aliases, r1.sems + r2.sems, start, finish)


def _after_rider(x):
    return _Rider([x], [], {}, [], lambda *a: None, lambda *a: None)


def _call(body, *, name, grid, in_specs, out_specs, out_shape, scratch_shapes, args, sem, rider=None, after=None):
    out_specs, out_shape = tuple(out_specs), tuple(out_shape)
    if after is not None:
        rider = _after_rider(after) if rider is None else _join_riders(_after_rider(after), rider)
    if rider is None:
        res = pl.pallas_call(body, name=name, out_shape=out_shape, grid=grid, in_specs=list(in_specs),
                             out_specs=out_specs, scratch_shapes=list(scratch_shapes),
                             compiler_params=_cparams(sem))(*args)
        return tuple(res), []
    n_in, n_out, n_sc = len(in_specs), len(out_specs), len(scratch_shapes)
    r_in, r_out = len(rider.inputs), len(rider.out_shape)

    def wrapped(*refs):
        p = 0
        host_in = refs[p:p + n_in]; p += n_in
        rid_in = refs[p:p + r_in]; p += r_in
        host_out = refs[p:p + n_out]; p += n_out
        rid_out = refs[p:p + r_out]; p += r_out
        host_sc = refs[p:p + n_sc]; p += n_sc
        rid_sem = refs[p:]
        first = functools.reduce(jnp.logical_and, [pl.program_id(a) == 0 for a in range(len(grid))])
        last = functools.reduce(jnp.logical_and, [pl.program_id(a) == grid[a] - 1 for a in range(len(grid))])

        @pl.when(first)
        def _():
            rider.start(rid_in, rid_out, rid_sem)

        body(*host_in, *host_out, *host_sc)

        @pl.when(last)
        def _():
            rider.finish(rid_in, rid_out, rid_sem)

    res = pl.pallas_call(
        wrapped, name=name, out_shape=out_shape + tuple(rider.out_shape), grid=grid,
        in_specs=list(in_specs) + [ANY] * r_in, out_specs=out_specs + (ANY,) * r_out,
        input_output_aliases={n_in + i: n_out + o for i, o in rider.aliases.items()},
        scratch_shapes=list(scratch_shapes) + rider.sems,
        compiler_params=_cparams(("arbitrary",) * len(grid)),
    )(*args, *rider.inputs)
    return tuple(res[:n_out]), list(res[n_out:])


def _run_rider(rider, name):
    def body(*refs):
        r_in, r_out = len(rider.inputs), len(rider.out_shape)
        ins, outs, sems = refs[:r_in], refs[r_in:r_in + r_out], refs[r_in + r_out:]
        rider.start(ins, outs, sems)
        rider.finish(ins, outs, sems)

    return pl.pallas_call(
        body, name=name, out_shape=rider.out_shape, in_specs=[ANY] * len(rider.inputs),
        out_specs=[ANY] * len(rider.out_shape), input_output_aliases=rider.aliases, scratch_shapes=rider.sems,
    )(*rider.inputs)


def _atoms(indices, kks=(0, 1, 2), q=0, nq=1):
    return [(i, kk, q, nq) for i in indices for kk in kks]


def _gather_rider(bufs, atoms=None):
    n = len(bufs)
    if atoms is None:
        atoms = _atoms(range(n))
    na = len(atoms)

    def rows(out, atom, core):
        i, _, q, nq = atom
        half = out[i].shape[1] // 2
        assert half % (16 * nq) == 0, (half, nq)
        return pl.ds(core * half + q * (half // nq), half // nq)

    def ici_copy(out, sems, a, slot, peer):
        c = lax.axis_index("c")
        blk = out[atoms[a][0]].at[slot, rows(out, atoms[a], c), :]
        return pltpu.make_async_remote_copy(
            src_ref=blk, dst_ref=blk, send_sem=sems[0].at[a], recv_sem=sems[1].at[a],
            device_id=(peer[0], peer[1], c), device_id_type=MESH)

    def d2d_copy(out, sems, a, slot, from_core):
        x, y, c, _ = _place()
        blk = out[atoms[a][0]].at[slot, rows(out, atoms[a], from_core), :]
        return pltpu.make_async_remote_copy(
            src_ref=blk, dst_ref=blk, send_sem=sems[2].at[a], recv_sem=sems[3].at[a],
            device_id=(x, y, 1 - c), device_id_type=MESH)

    def start(ins, out, sems):
        x, y, c, chips = _place()
        for a in range(na):
            ici_copy(out, sems, a, 2 * x + y, chips[atoms[a][1]]).start()

    def finish(ins, out, sems):
        x, y, c, chips = _place()
        src = [2 * chips[atoms[a][1]][0] + chips[atoms[a][1]][1] for a in range(na)]
        for a in range(na):
            ici_copy(out, sems, a, src[a], chips[atoms[a][1]]).wait_recv()
            d2d_copy(out, sems, a, src[a], c).start()
        for a in range(na):
            d2d_copy(out, sems, a, src[a], 1 - c).wait_recv()
        for a in range(na):
            ici_copy(out, sems, a, 2 * x + y, chips[atoms[a][1]]).wait_send()
            d2d_copy(out, sems, a, src[a], c).wait_send()

    return _Rider(bufs, [jax.ShapeDtypeStruct(s.shape, s.dtype) for s in bufs], {i: i for i in range(n)},
                  [pltpu.SemaphoreType.DMA((na,))] * 4, start, finish)


def _mm_shards(a, buf, order_arr, which, name, after, out=None):
    M, K = a.shape
    S, _, n = buf.shape
    tm = _pick(M, (512, 256, 128))
    s0 = which[0]

    def body(order_ref, a_ref, b_ref, *rest):
        rest[-1][...] = jnp.dot(a_ref[...], b_ref[...], preferred_element_type=F32)

    gs = pltpu.PrefetchScalarGridSpec(
        num_scalar_prefetch=1, grid=(len(which), M // tm),
        in_specs=[pl.BlockSpec((tm, K), lambda g, i, order: (i, 0)),
                  pl.BlockSpec((None, K, n), lambda g, i, order: (order[s0 + g], 0, 0)), ANY]
        + ([ANY] if out is not None else []),
        out_specs=pl.BlockSpec((tm, n), lambda g, i, order: (i, order[s0 + g])))
    return pl.pallas_call(
        body, name=name, grid_spec=gs, out_shape=jax.ShapeDtypeStruct((M, S * n), F32),
        input_output_aliases={4: 0} if out is not None else {},
        compiler_params=_cparams(("arbitrary", "arbitrary")),
    )(order_arr, a, buf, after, *([out] if out is not None else []))


def _mm_gathering(a, buf, order_arr, name, after):
    M, K = a.shape
    S, _, n = buf.shape
    tm = _pick(M, (512, 256, 128))
    n_i = M // tm
    half = K // 2

    def body(order_ref, a_ref, w_in_ref, after_ref, o_ref, w_ref, b_vmem, load_sem, s_ici, r_ici, s_d2d, r_d2d):
        s, i = pl.program_id(0), pl.program_id(1)
        x, y, c, chips = _place()
        j_me = 2 * x + y
        slots = [2 * px + py for px, py in chips]

        def ici(kk, slot):
            blk = w_ref.at[slot, pl.ds(c * half, half), :]
            return pltpu.make_async_remote_copy(
                src_ref=blk, dst_ref=blk, send_sem=s_ici.at[kk], recv_sem=r_ici.at[kk],
                device_id=(chips[kk][0], chips[kk][1], c), device_id_type=MESH)

        def d2d(kk, from_core):
            blk = w_ref.at[slots[kk], pl.ds(from_core * half, half), :]
            return pltpu.make_async_remote_copy(
                src_ref=blk, dst_ref=blk, send_sem=s_d2d.at[kk], recv_sem=r_d2d.at[kk],
                device_id=(x, y, 1 - c), device_id_type=MESH)

        def load(slot, b):
            return pltpu.make_async_copy(w_ref.at[slot], b_vmem.at[b], load_sem.at[b])

        @pl.when(jnp.logical_and(s == 0, i == 0))
        def _():
            for kk in range(3):
                ici(kk, j_me).start()
            load(j_me, 0).start()

        @pl.when(i == 0)
        def _():
            load(order_ref[s], s % 2).wait()

        o_ref[...] = jnp.dot(a_ref[...], b_vmem[s % 2], preferred_element_type=F32)

        last = i == n_i - 1

        @pl.when(jnp.logical_and(last, s == 0))
        def _():
            ici(0, slots[0]).wait_recv()
            d2d(0, c).start()
            ici(1, slots[1]).wait_recv()
            d2d(1, c).start()
            d2d(0, 1 - c).wait_recv()
            load(slots[0], 1).start()

        @pl.when(jnp.logical_and(last, s == 1))
        def _():
            d2d(1, 1 - c).wait_recv()
            load(slots[1], 0).start()

        @pl.when(jnp.logical_and(last, s == 2))
        def _():
            ici(2, slots[2]).wait_recv()
            d2d(2, c).start()
            d2d(2, 1 - c).wait_recv()
            load(slots[2], 1).start()

        @pl.when(jnp.logical_and(last, s == 3))
        def _():
            for kk in range(3):
                ici(kk, j_me).wait_send()
                d2d(kk, c).wait_send()

    gs = pltpu.PrefetchScalarGridSpec(
        num_scalar_prefetch=1, grid=(S, n_i),
        in_specs=[pl.BlockSpec((tm, K), lambda s, i, order: (i, 0)), ANY, ANY],
        out_specs=[pl.BlockSpec((tm, n), lambda s, i, order: (i, order[s])), ANY],
        scratch_shapes=[pltpu.VMEM((2, K, n), BF16), pltpu.SemaphoreType.DMA((2,))]
        + [pltpu.SemaphoreType.DMA((3,))] * 4)
    return pl.pallas_call(
        body, name=name, grid_spec=gs,
        out_shape=[jax.ShapeDtypeStruct((M, S * n), F32), jax.ShapeDtypeStruct(buf.shape, buf.dtype)],
        input_output_aliases={2: 1}, compiler_params=_cparams(("arbitrary", "arbitrary")),
    )(order_arr, a, buf, after)


def _all_gather_small(shards):
    n = len(shards)

    def body(*refs):
        w = refs[:n]
        out = refs[n:2 * n]
        local_sem, s_sem, r_sem = refs[2 * n:]
        x, y, c, chips = _place()
        j_me = 2 * x + y
        cps = []
        for i in range(n):
            lc = pltpu.make_async_copy(w[i], out[i].at[j_me], local_sem.at[i])
            lc.start()
            cps.append(lc)
        sends = []
        for i in range(n):
            for kk, (px, py) in enumerate(chips):
                cp = pltpu.make_async_remote_copy(
                    src_ref=w[i], dst_ref=out[i].at[j_me], send_sem=s_sem.at[3 * i + kk],
                    recv_sem=r_sem.at[3 * i + kk], device_id=(px, py, c), device_id_type=MESH)
                cp.start()
                sends.append(cp)
        for i in range(n):
            for kk, (px, py) in enumerate(chips):
                sends[3 * i + kk].wait_send()
                pltpu.make_async_remote_copy(
                    src_ref=w[i], dst_ref=out[i].at[2 * px + py], send_sem=s_sem.at[3 * i + kk],
                    recv_sem=r_sem.at[3 * i + kk], device_id=(px, py, c), device_id_type=MESH).wait_recv()
        for lc in cps:
            lc.wait()

    out_shape = [jax.ShapeDtypeStruct((N_SHARDS,) + s.shape, s.dtype) for s in shards]
    return pl.pallas_call(
        body, name="all_gather_conv_weights", out_shape=out_shape, in_specs=[ANY] * n, out_specs=[ANY] * n,
        scratch_shapes=[pltpu.SemaphoreType.DMA((n,)), pltpu.SemaphoreType.DMA((3 * n,)),
                        pltpu.SemaphoreType.DMA((3 * n,))],
    )(*shards)


def _pair_rider(grads):
    n = len(grads)

    def copies(g, la, sems):
        x, y, c, _ = _place()
        return [pltpu.make_async_remote_copy(
            src_ref=g[i].at[:, pl.ds((1 - c) * (g[i].shape[1] // 2), g[i].shape[1] // 2), :], dst_ref=la[i],
            send_sem=sems[0].at[i], recv_sem=sems[1].at[i], device_id=(x, y, 1 - c), device_id_type=MESH)
            for i in range(n)]

    def start(g, la, sems):
        for cp in copies(g, la, sems):
            cp.start()

    def finish(g, la, sems):
        for cp in copies(g, la, sems):
            cp.wait()

    return _Rider(grads, [jax.ShapeDtypeStruct((s.shape[0], s.shape[1] // 2, s.shape[2]), s.dtype) for s in grads],
                  {}, [pltpu.SemaphoreType.DMA((n,)), pltpu.SemaphoreType.DMA((n,))], start, finish)


def _shard_exchange_rider(cps_in, atoms=None):
    n = len(cps_in)
    if atoms is None:
        atoms = _atoms(range(n))

    def copies(ins, lb, sems):
        x, y, c, chips = _place()
        out = []
        for a, (i, kk, q, nq) in enumerate(atoms):
            h = ins[i].shape[1]
            assert h % (16 * nq) == 0, (h, nq)
            rows = pl.ds(q * (h // nq), h // nq)
            px, py = chips[kk]
            out.append(pltpu.make_async_remote_copy(
                src_ref=ins[i].at[2 * px + py, rows, :], dst_ref=lb[i].at[kk, rows, :],
                send_sem=sems[0].at[a], recv_sem=sems[1].at[a], device_id=(px, py, c), device_id_type=MESH))
        return out

    def start(ins, lb, sems):
        for cp in copies(ins, lb, sems):
            cp.start()

    def finish(ins, lb, sems):
        for cp in copies(ins, lb, sems):
            cp.wait()

    return _Rider(cps_in, [jax.ShapeDtypeStruct((3,) + s.shape[1:], s.dtype) for s in cps_in], {},
                  [pltpu.SemaphoreType.DMA((len(atoms),)), pltpu.SemaphoreType.DMA((len(atoms),))], start, finish)


HBM = pl.BlockSpec(memory_space=pltpu.HBM)
SEM = pl.BlockSpec(memory_space=pltpu.SEMAPHORE)


def _shard_copies(part_refs, land_refs, send_sems, recv_sems, relations):
    x, y, c, chips = _place()
    nr = len(relations)
    return [pltpu.make_async_remote_copy(
        src_ref=part_refs[i].at[2 * chips[kk][0] + chips[kk][1]], dst_ref=land_refs[i].at[kk],
        send_sem=send_sems.at[nr * i + r], recv_sem=recv_sems.at[nr * i + r],
        device_id=(chips[kk][0], chips[kk][1], c), device_id_type=MESH)
        for i in range(len(part_refs)) for r, kk in enumerate(relations)]


SIDE_EFFECT = pltpu.SideEffectType.DATAFLOW_SIDE_EFFECTING


def _shard_exchange_start(parts, name, relations=(0, 1, 2), lands=None):
    n = len(parts)
    ns = n * len(relations)

    def body(*refs):
        part_refs, land_refs = refs[:n], refs[n:2 * n]
        send_sems, recv_sems = refs[2 * n], refs[2 * n + 1]
        token = refs[4 * n + 2]
        for cp in _shard_copies(part_refs, land_refs, send_sems, recv_sems, relations):
            cp.start()
        token[...] = jnp.zeros_like(token)

    if lands is None:
        lands = [lax.empty((3,) + p.shape[1:], p.dtype) for p in parts]
    bufs = list(parts) + list(lands)
    res = pl.pallas_call(
        body, name=name,
        out_shape=(pltpu.SemaphoreType.DMA((ns,)), pltpu.SemaphoreType.DMA((ns,)),
                   *[pltpu.HBM(b.shape, b.dtype) for b in bufs], jax.ShapeDtypeStruct((8, LANES), F32)),
        in_specs=(HBM,) * (2 * n), out_specs=(SEM, SEM) + (HBM,) * (2 * n) + (pl.BlockSpec(memory_space=pltpu.VMEM),),
        input_output_aliases={i: 2 + i for i in range(2 * n)},
        compiler_params=pltpu.CompilerParams(has_side_effects=SIDE_EFFECT),
    )(*[pltpu.with_memory_space_constraint(b, pltpu.HBM) for b in bufs])
    return res[0], res[1], list(res[2:2 + n]), list(res[2 + n:2 + 2 * n]), res[2 + 2 * n], relations


def _shard_exchange_wait(started, after, name):
    send_sems, recv_sems, parts, lands, _, relations = started
    n = len(parts)

    def body(*refs):
        part_refs, land_refs = refs[:n], refs[n:2 * n]
        send_sems_ref, recv_sems_ref = refs[2 * n], refs[2 * n + 1]
        for cp in _shard_copies(part_refs, land_refs, send_sems_ref, recv_sems_ref, relations):
            cp.wait_send()
            cp.wait_recv()

    bufs = parts + lands
    res = pl.pallas_call(
        body, name=name, out_shape=tuple(pltpu.HBM(b.shape, b.dtype) for b in bufs),
        in_specs=(HBM,) * (2 * n) + (SEM, SEM, ANY), out_specs=(HBM,) * (2 * n),
        input_output_aliases={i: i for i in range(2 * n)},
        compiler_params=pltpu.CompilerParams(has_side_effects=SIDE_EFFECT),
    )(*bufs, send_sems, recv_sems, after)
    return list(res[:n]), list(res[n:])


def _gather_copies(buf_ref, send_sems, recv_sems, over_d2d, arriving, relations):
    x, y, c, chips = _place()
    half = buf_ref.shape[1] // 2
    out = []
    for r, kk in enumerate(relations):
        px, py = chips[kk]
        if over_d2d:
            slot, core, peer = 2 * px + py, (1 - c) if arriving else c, (x, y, 1 - c)
        else:
            slot, core, peer = (2 * px + py) if arriving else (2 * x + y), c, (px, py, c)
        blk = buf_ref.at[slot, pl.ds(core * half, half), :]
        out.append(pltpu.make_async_remote_copy(src_ref=blk, dst_ref=blk, send_sem=send_sems.at[r],
                                                recv_sem=recv_sems.at[r], device_id=peer, device_id_type=MESH))
    return out


def _gather_step(buf, after, name, sems_in=None, start_d2d=None, relations=(0, 1, 2)):
    n_sem = 0 if sems_in is None else 2

    def body(*refs):
        buf_ref = refs[0]
        ins = refs[1:1 + n_sem]
        outs = refs[2 + n_sem:]
        if sems_in is not None:
            waited_d2d = start_d2d is None
            for mine, theirs in zip(_gather_copies(buf_ref, ins[0], ins[1], waited_d2d, False, relations),
                                    _gather_copies(buf_ref, ins[0], ins[1], waited_d2d, True, relations)):
                theirs.wait_recv()
                mine.wait_send()
        if start_d2d is not None:
            for cp in _gather_copies(buf_ref, outs[0], outs[1], start_d2d, False, relations):
                cp.start()
            outs[3][...] = jnp.zeros_like(outs[3])

    nr = len(relations)
    sem_out = () if start_d2d is None else (pltpu.SemaphoreType.DMA((nr,)), pltpu.SemaphoreType.DMA((nr,)))
    tok_out = () if start_d2d is None else (jax.ShapeDtypeStruct((8, LANES), F32),)
    res = pl.pallas_call(
        body, name=name,
        out_shape=sem_out + (pltpu.HBM(buf.shape, buf.dtype),) + tok_out,
        in_specs=(HBM,) + (SEM,) * n_sem + (ANY,),
        out_specs=(SEM,) * len(sem_out) + (HBM,) + (pl.BlockSpec(memory_space=pltpu.VMEM),) * len(tok_out),
        input_output_aliases={0: len(sem_out)},
        compiler_params=pltpu.CompilerParams(has_side_effects=SIDE_EFFECT),
    )(pltpu.with_memory_space_constraint(buf, pltpu.HBM), *(sems_in or ()), after)
    if start_d2d is None:
        return res[0]
    return (res[0], res[1]), res[2], res[3]


def _relay_copies(buf_ref, send_sems, recv_sems, arriving):
    x, y, c, chips = _place()
    quarter = buf_ref.shape[1] // 4
    out = []
    for r, (src_kk, dst_kk) in enumerate(((0, 1), (1, 0))):
        slot = (2 * chips[2][0] + chips[2][1]) if arriving else (2 * chips[src_kk][0] + chips[src_kk][1])
        blk = buf_ref.at[slot, pl.ds((2 * c + r) * quarter, quarter), :]
        out.append(pltpu.make_async_remote_copy(
            src_ref=blk, dst_ref=blk, send_sem=send_sems.at[r], recv_sem=recv_sems.at[r],
            device_id=(chips[dst_kk][0], chips[dst_kk][1], c), device_id_type=MESH))
    return out


def _gather_relay_step(buf, after, name, sems_in, relay_in=None):
    first = relay_in is None
    ins_sems = sems_in if first else relay_in
    near, diag = (0, 1), (2,)

    def body(*refs):
        buf_ref, in_s, in_r = refs[0], refs[1], refs[2]
        outs = refs[4:]
        if first:
            for mine, theirs in zip(_gather_copies(buf_ref, in_s, in_r, False, False, near),
                                    _gather_copies(buf_ref, in_s, in_r, False, True, near)):
                theirs.wait_recv()
                mine.wait_send()
            for cp in _gather_copies(buf_ref, outs[0], outs[1], True, False, near):
                cp.start()
            for cp in _relay_copies(buf_ref, outs[2], outs[3], False):
                cp.start()
        else:
            for mine, theirs in zip(_relay_copies(buf_ref, in_s, in_r, False), _relay_copies(buf_ref, in_s, in_r, True)):
                theirs.wait_recv()
                mine.wait_send()
            for cp in _gather_copies(buf_ref, outs[0], outs[1], True, False, diag):
                cp.start()
        outs[-1][...] = jnp.zeros_like(outs[-1])

    def sem(n):
        return pltpu.SemaphoreType.DMA((n,))

    sem_out = (sem(2), sem(2), sem(2), sem(2)) if first else (sem(1), sem(1))
    res = pl.pallas_call(
        body, name=name,
        out_shape=sem_out + (pltpu.HBM(buf.shape, buf.dtype), jax.ShapeDtypeStruct((8, LANES), F32)),
        in_specs=(HBM, SEM, SEM, ANY),
        out_specs=(SEM,) * len(sem_out) + (HBM, pl.BlockSpec(memory_space=pltpu.VMEM)),
        input_output_aliases={0: len(sem_out)},
        compiler_params=pltpu.CompilerParams(has_side_effects=SIDE_EFFECT),
    )(pltpu.with_memory_space_constraint(buf, pltpu.HBM), *ins_sems, after)
    if first:
        return (res[0], res[1]), (res[2], res[3]), res[4], res[5]
    return (res[0], res[1]), res[2], res[3]


def _share_rider(halves, eighths=None):
    n = len(halves)
    bufs = list(halves) + ([eighths] if eighths is not None else [])

    def half_copy(out, sems, i, core):
        x, y, c, _ = _place()
        blk = out[i].at[core]
        return pltpu.make_async_remote_copy(src_ref=blk, dst_ref=blk, send_sem=sems[0].at[i], recv_sem=sems[1].at[i],
                                            device_id=(x, y, 1 - c), device_id_type=MESH)

    def eighth_copy(out, sems, r, mine):
        x, y, c, _ = _place()
        px, py, pc = x ^ ((r >> 2) & 1), y ^ ((r >> 1) & 1), c ^ (r & 1)
        blk = out[n].at[2 * x + y, c] if mine else out[n].at[2 * px + py, pc]
        return pltpu.make_async_remote_copy(src_ref=blk, dst_ref=blk, send_sem=sems[2].at[r - 1],
                                            recv_sem=sems[3].at[r - 1], device_id=(px, py, pc), device_id_type=MESH)

    def start(ins, out, sems):
        c = lax.axis_index("c")
        for i in range(n):
            half_copy(out, sems, i, c).start()
        if eighths is not None:
            for r in range(1, N_DEV):
                eighth_copy(out, sems, r, True).start()

    def finish(ins, out, sems):
        c = lax.axis_index("c")
        for i in range(n):
            half_copy(out, sems, i, 1 - c).wait_recv()
        if eighths is not None:
            for r in range(1, N_DEV):
                eighth_copy(out, sems, r, False).wait_recv()
        for i in range(n):
            half_copy(out, sems, i, c).wait_send()
        if eighths is not None:
            for r in range(1, N_DEV):
                eighth_copy(out, sems, r, True).wait_send()

    return _Rider(bufs, [jax.ShapeDtypeStruct(s.shape, s.dtype) for s in bufs], {i: i for i in range(len(bufs))},
                  [pltpu.SemaphoreType.DMA((max(n, 1),)), pltpu.SemaphoreType.DMA((max(n, 1),)),
                   pltpu.SemaphoreType.DMA((N_DEV - 1,)), pltpu.SemaphoreType.DMA((N_DEV - 1,))], start, finish)


ATTN_ROWS = GROUP * ATTN_BLOCK
ATTN_KEYS = 2 * ATTN_BLOCK


def _attn_geometry(n):
    row = lax.broadcasted_iota(jnp.int32, (ATTN_ROWS, ATTN_KEYS), 0)
    col = lax.broadcasted_iota(jnp.int32, (ATTN_ROWS, ATTN_KEYS), 1)
    dist = ATTN_BLOCK + jnp.bitwise_and(row, ATTN_BLOCK - 1) - col
    valid = jnp.logical_and(jnp.logical_and(dist >= 0, dist < ATTN_BLOCK),
                            jnp.logical_or(col >= ATTN_BLOCK, n > 0))
    return dist.astype(F32), valid


def _per_head_column(values):
    head = lax.broadcasted_iota(jnp.int32, (ATTN_ROWS, 1), 0) // ATTN_BLOCK
    col = jnp.zeros((ATTN_ROWS, 1), F32)
    for hh, v in enumerate(values):
        col = jnp.where(head == hh, v, col)
    return col


def _stack_heads(ref, g):
    return jnp.concatenate(
        [ref[:, (g * GROUP + hh) * HEAD_DIM:(g * GROUP + hh + 1) * HEAD_DIM].astype(BF16) for hh in range(GROUP)],
        axis=0)


def _attn_probs(q_s, k2, slope_col, sink_col, dist, valid):
    s = lax.dot_general(q_s, k2, (((1,), (1,)), ((), ())), preferred_element_type=F32) * (HEAD_DIM ** -0.5)
    s = jnp.where(valid, s - slope_col * dist, NEG)
    m = jnp.maximum(jnp.max(s, axis=1, keepdims=True), sink_col)
    e = jnp.exp(s - m)
    es = jnp.exp(sink_col - m)
    inv = 1.0 / (jnp.sum(e, axis=1, keepdims=True) + es)
    return e * inv, es * inv


def _attn_specs(T, d_attn, d_kv, q_blk, k_blk, v_blk):
    bq = pl.BlockSpec((ATTN_BLOCK, d_attn), lambda n: (n, q_blk))
    kp = pl.BlockSpec((ATTN_BLOCK, d_kv), lambda n: (jnp.maximum(n - 1, 0), k_blk))
    kc = pl.BlockSpec((ATTN_BLOCK, d_kv), lambda n: (n, k_blk))
    vp = pl.BlockSpec((ATTN_BLOCK, d_kv), lambda n: (jnp.maximum(n - 1, 0), v_blk))
    vc = pl.BlockSpec((ATTN_BLOCK, d_kv), lambda n: (n, v_blk))
    return bq, kp, kc, vp, vc


def _attn_fwd(proj, sinks, nq, cols, after=None):
    T = proj.shape[0]
    nkv = nq // GROUP
    d_attn, d_kv = nq * HEAD_DIM, nkv * HEAD_DIM
    q_off, k_off, v_off = cols
    bq, kp, kc, vp, vc = _attn_specs(T, d_attn, d_kv, q_off // d_attn, k_off // d_kv, v_off // d_kv)

    def body(sink_ref, q_ref, kp_ref, kc_ref, vp_ref, vc_ref, o_ref):
        n = pl.program_id(0)
        dist, valid = _attn_geometry(n)
        for g in range(nkv):
            ks = slice(g * HEAD_DIM, (g + 1) * HEAD_DIM)
            k2 = jnp.concatenate([kp_ref[:, ks], kc_ref[:, ks]], axis=0).astype(BF16)
            v2 = jnp.concatenate([vp_ref[:, ks], vc_ref[:, ks]], axis=0).astype(BF16)
            slope_col = _per_head_column([2.0 ** (-8.0 * (g * GROUP + hh + 1) / nq) for hh in range(GROUP)])
            sink_col = _per_head_column([sink_ref[0, g * GROUP + hh] for hh in range(GROUP)])
            p, _ = _attn_probs(_stack_heads(q_ref, g), k2, slope_col, sink_col, dist, valid)
            o = jnp.dot(p.astype(BF16), v2, preferred_element_type=F32).astype(BF16)
            for hh in range(GROUP):
                h = g * GROUP + hh
                o_ref[:, h * HEAD_DIM:(h + 1) * HEAD_DIM] = o[hh * ATTN_BLOCK:(hh + 1) * ATTN_BLOCK, :]

    (out,), carried = _call(
        body, name="attn_fwd", out_shape=[jax.ShapeDtypeStruct((T, d_attn), BF16)], grid=(T // ATTN_BLOCK,),
        in_specs=[pl.BlockSpec(memory_space=pltpu.SMEM), bq, kp, kc, vp, vc],
        out_specs=[pl.BlockSpec((ATTN_BLOCK, d_attn), lambda n: (n, 0))], scratch_shapes=[],
        args=(sinks, proj, proj, proj, proj, proj), sem=("parallel",), after=after)
    return out


def _attn_bwd(proj, d_attn_out, sinks, nq, cols, after=None):
    T = proj.shape[0]
    nkv = nq // GROUP
    d_attn, d_kv = nq * HEAD_DIM, nkv * HEAD_DIM
    q_off, k_off, v_off = cols
    bq, kp, kc, vp, vc = _attn_specs(T, d_attn, d_kv, q_off // d_attn, k_off // d_kv, v_off // d_kv)
    scale = HEAD_DIM ** -0.5
    dn_t = (((1,), (1,)), ((), ()))
    dn_r = (((0,), (0,)), ((), ()))

    def body(sink_ref, q_ref, kp_ref, kc_ref, vp_ref, vc_ref, do_ref, dq_ref, dk_ref, dv_ref, ds_ref):
        n = pl.program_id(0)

        @pl.when(n == 0)
        def _():
            dk_ref[...] = jnp.zeros_like(dk_ref)
            dv_ref[...] = jnp.zeros_like(dv_ref)
            ds_ref[...] = jnp.zeros_like(ds_ref)

        dist, valid = _attn_geometry(n)
        rows_c = pl.ds(pl.multiple_of(n * ATTN_BLOCK, ATTN_BLOCK), ATTN_BLOCK)
        rows_p = pl.ds(pl.multiple_of(jnp.maximum(n - 1, 0) * ATTN_BLOCK, ATTN_BLOCK), ATTN_BLOCK)
        lane = lax.broadcasted_iota(jnp.int32, ds_ref.shape, 1)
        srow = lax.broadcasted_iota(jnp.int32, ds_ref.shape, 0)
        ds_acc = jnp.zeros(ds_ref.shape, F32)
        for g in range(nkv):
            ks = slice(g * HEAD_DIM, (g + 1) * HEAD_DIM)
            k2 = jnp.concatenate([kp_ref[:, ks], kc_ref[:, ks]], axis=0).astype(BF16)
            v2 = jnp.concatenate([vp_ref[:, ks], vc_ref[:, ks]], axis=0).astype(BF16)
            slope_col = _per_head_column([2.0 ** (-8.0 * (g * GROUP + hh + 1) / nq) for hh in range(GROUP)])
            sink_col = _per_head_column([sink_ref[0, g * GROUP + hh] for hh in range(GROUP)])
            q_s = _stack_heads(q_ref, g)
            do_s = _stack_heads(do_ref, g)
            p, p_sink = _attn_probs(q_s, k2, slope_col, sink_col, dist, valid)
            dp = lax.dot_general(do_s, v2, dn_t, preferred_element_type=F32)
            delta = jnp.sum(p * dp, axis=1, keepdims=True)
            ds = (p * (dp - delta)).astype(BF16)
            sink_part = p_sink * delta
            dq = (jnp.dot(ds, k2, preferred_element_type=F32) * scale).astype(BF16)
            for hh in range(GROUP):
                h = g * GROUP + hh
                blk = slice(hh * ATTN_BLOCK, (hh + 1) * ATTN_BLOCK)
                dq_ref[:, h * HEAD_DIM:(h + 1) * HEAD_DIM] = dq[blk, :]
                ds_acc = ds_acc + jnp.where(jnp.logical_and(lane == h, srow == 0), -jnp.sum(sink_part[blk, :]), 0.0)
            dk2 = lax.dot_general(ds, q_s, dn_r, preferred_element_type=F32) * scale
            dv2 = lax.dot_general(p.astype(BF16), do_s, dn_r, preferred_element_type=F32)
            dk_ref[rows_p, ks] += dk2[:ATTN_BLOCK, :]
            dv_ref[rows_p, ks] += dv2[:ATTN_BLOCK, :]
            dk_ref[rows_c, ks] += dk2[ATTN_BLOCK:, :]
            dv_ref[rows_c, ks] += dv2[ATTN_BLOCK:, :]
        ds_ref[...] += ds_acc

    out_shape = (jax.ShapeDtypeStruct((T, d_attn), BF16), jax.ShapeDtypeStruct((T, d_kv), F32),
                 jax.ShapeDtypeStruct((T, d_kv), F32), jax.ShapeDtypeStruct((8, LANES), F32))
    return _call(
        body, name="attn_bwd", out_shape=out_shape, grid=(T // ATTN_BLOCK,),
        in_specs=[pl.BlockSpec(memory_space=pltpu.SMEM), bq, kp, kc, vp, vc,
                  pl.BlockSpec((ATTN_BLOCK, d_attn), lambda n: (n, 0))],
        out_specs=(pl.BlockSpec((ATTN_BLOCK, d_attn), lambda n: (n, 0)),
                   pl.BlockSpec((T, d_kv), lambda n: (0, 0)), pl.BlockSpec((T, d_kv), lambda n: (0, 0)),
                   pl.BlockSpec((8, LANES), lambda n: (0, 0))),
        scratch_shapes=[], args=(sinks, proj, proj, proj, proj, proj, d_attn_out), sem=("arbitrary",), after=after)[0]


def _rnn_tile(T):
    return _pick(T, (256, 128))


def _rnn_gates(x_ext, cw_ref, cb_ref, wa_ref, wi_ref, ba_ref, bi_ref, lam_ref, tt):
    xs = [pltpu.roll(x_ext, 3 - k, 0)[8:, :] if k < 3 else x_ext[8:, :] for k in range(4)]
    cx = cb_ref[...] + xs[0] * cw_ref[0:1, :]
    for k in range(1, 4):
        cx = cx + xs[k] * cw_ref[k:k + 1, :]
    cxb = cx.astype(BF16)
    r = jax.nn.sigmoid(jnp.dot(cxb, wa_ref[...], preferred_element_type=F32) + ba_ref[...])
    i = jax.nn.sigmoid(jnp.dot(cxb, wi_ref[...], preferred_element_type=F32) + bi_ref[...])
    lam = lam_ref[...]
    sp = jnp.maximum(-lam, 0.0) + jnp.log1p(jnp.exp(-jnp.abs(lam)))
    log_a = -LRU_C * r * sp
    a = jnp.exp(log_a)
    z = 2.0 * log_a
    em1 = jnp.where(z > -1e-2, z * (1.0 + z * (0.5 + z * (1.0 / 6.0 + z * (1.0 / 24.0)))), jnp.exp(z) - 1.0)
    s = jnp.sqrt(-em1)
    return xs, cx, r, i, sp, a, s


def _rnn_specs(T, gw, tt, rx_blk, ry_blk, rev):
    nT = T // tt
    hb = tt // 8

    def tile(t):
        return (nT - 1 - t) if rev else t

    rx = pl.BlockSpec((tt, gw), lambda g, t: (tile(t), rx_blk + g))
    rx_halo = pl.BlockSpec((8, gw), lambda g, t: (jnp.maximum(tile(t) * hb - 1, 0), rx_blk + g))
    ry = pl.BlockSpec((tt, gw), lambda g, t: (tile(t), ry_blk + g))
    cw = pl.BlockSpec((4, gw), lambda g, t: (0, g))
    vec = pl.BlockSpec((1, gw), lambda g, t: (0, g))
    wg = pl.BlockSpec((None, gw, gw), lambda g, t: (g, 0, 0))
    act = pl.BlockSpec((tt, gw), lambda g, t: (tile(t), g))
    act_halo = pl.BlockSpec((8, gw), lambda g, t: (jnp.maximum(tile(t) * hb - 1, 0), g))
    return rx, rx_halo, ry, cw, vec, wg, act, act_halo, tile


def _rnn_fwd(proj, cols, conv_w, conv_b, wa_g, wi_g, ba, bi, lam, rider=None):
    T = proj.shape[0]
    G, gw, _ = wa_g.shape
    d_rnn = G * gw
    tt = _rnn_tile(T)
    rx_off, ry_off = cols
    rx, rx_halo, ry, cw, vec, wg, act, _, _ = _rnn_specs(T, gw, tt, rx_off // gw, ry_off // gw, False)

    def body(rx_ref, rxh_ref, ry_ref, cw_ref, cb_ref, wa_ref, wi_ref, ba_ref, bi_ref, lam_ref,
             b_ref, h_ref, carry):
        t = pl.program_id(1)

        @pl.when(t == 0)
        def _():
            carry[...] = jnp.zeros_like(carry)

        halo = jnp.where(t > 0, rxh_ref[...], 0.0)
        x_ext = jnp.concatenate([halo, rx_ref[...]], axis=0)
        _, cx, _, i, _, a, s = _rnn_gates(x_ext, cw_ref, cb_ref, wa_ref, wi_ref, ba_ref, bi_ref, lam_ref, tt)
        acc_a, acc_b = a, s * (i * cx)
        d = 1
        while d < tt:
            acc_b = acc_a * _shift_down(acc_b, d, 0.0) + acc_b
            acc_a = acc_a * _shift_down(acc_a, d, 1.0)
            d *= 2
        h = acc_b + acc_a * carry[7:8, :]
        carry[...] = h[tt - 8:, :]
        h_ref[...] = h
        b_ref[...] = (h * _gelu(ry_ref[...])).astype(BF16)

    return _call(
        body, name="rnn_fwd",
        out_shape=(jax.ShapeDtypeStruct((T, d_rnn), BF16), jax.ShapeDtypeStruct((T, d_rnn), F32)),
        grid=(G, T // tt),
        in_specs=[rx, rx_halo, ry, cw, vec, wg, wg, vec, vec, vec], out_specs=(act, act),
        scratch_shapes=[pltpu.VMEM((8, gw), F32)],
        args=(proj, proj, proj, conv_w, conv_b, wa_g, wi_g, ba, bi, lam), sem=("parallel", "arbitrary"), rider=rider)


def _rnn_bwd(proj, cols, h_all, d_b, conv_w, conv_b, wa_g, wi_g, ba, bi, lam, rider=None):
    T = proj.shape[0]
    G, gw, _ = wa_g.shape
    d_rnn = G * gw
    tt = _rnn_tile(T)
    nT = T // tt
    rx_off, ry_off = cols
    rx, rx_halo, ry, cw, vec, wg, act, act_halo, _ = _rnn_specs(T, gw, tt, rx_off // gw, ry_off // gw, True)
    dn_t = (((1,), (1,)), ((), ()))
    dn_r = (((0,), (0,)), ((), ()))

    def body(rx_ref, rxh_ref, ry_ref, h_ref, hh_ref, db_ref, cw_ref, cb_ref, wa_ref, wi_ref, ba_ref, bi_ref, lam_ref,
             drx_ref, dry_ref, dcw_ref, dcb_ref, dba_ref, dbi_ref, dlam_ref, dwa_ref, dwi_ref,
             lam_carry, dcx_carry):
        t = pl.program_id(1)
        first_tile = t == nT - 1

        @pl.when(t == 0)
        def _():
            lam_carry[...] = jnp.zeros_like(lam_carry)
            dcx_carry[...] = jnp.zeros_like(dcx_carry)
            dcw_ref[...] = jnp.zeros_like(dcw_ref)
            dcb_ref[...] = jnp.zeros_like(dcb_ref)
            dba_ref[...] = jnp.zeros_like(dba_ref)
            dbi_ref[...] = jnp.zeros_like(dbi_ref)
            dlam_ref[...] = jnp.zeros_like(dlam_ref)
            dwa_ref[...] = jnp.zeros_like(dwa_ref)
            dwi_ref[...] = jnp.zeros_like(dwi_ref)

        halo = jnp.where(first_tile, 0.0, rxh_ref[...])
        x_ext = jnp.concatenate([halo, rx_ref[...]], axis=0)
        xs, cx, r, i, sp, a, s = _rnn_gates(x_ext, cw_ref, cb_ref, wa_ref, wi_ref, ba_ref, bi_ref, lam_ref, tt)
        h = h_ref[...]
        h_halo = jnp.where(first_tile, 0.0, hh_ref[...])
        h_prev = pltpu.roll(jnp.concatenate([h_halo, h], axis=0), 1, 0)[8:, :]
        gel, dgel = _gelu_and_grad(ry_ref[...])
        d_b_t = db_ref[...]
        dry_ref[...] = (d_b_t * h * dgel).astype(BF16)
        dh = d_b_t * gel

        acc_c = _shift_up(a, 1, 1.0)
        acc_l = dh
        d = 1
        while d < tt:
            acc_l = acc_c * _shift_up(acc_l, d, 0.0) + acc_l
            acc_c = acc_c * _shift_up(acc_c, d, 1.0)
            d *= 2
        lam_t = acc_l + acc_c * lam_carry[0:1, :]
        lam_carry[...] = (a * lam_t)[0:8, :]

        icx = i * cx
        d_s = lam_t * icx
        d_i = lam_t * s * cx
        dcx = lam_t * s * i
        d_a = lam_t * h_prev - d_s * (a / s)
        dlog_a = d_a * a
        d_r = dlog_a * (-LRU_C * sp)
        lam = lam_ref[...]
        dlam_ref[...] += jnp.sum(dlog_a * r, axis=0, keepdims=True) * (LRU_C * jax.nn.sigmoid(-lam))
        dpr = d_r * r * (1.0 - r)
        dpi = d_i * i * (1.0 - i)
        dba_ref[...] += jnp.sum(dpr, axis=0, keepdims=True)
        dbi_ref[...] += jnp.sum(dpi, axis=0, keepdims=True)
        cxb = cx.astype(BF16)
        dprb, dpib = dpr.astype(BF16), dpi.astype(BF16)
        dwa_ref[...] += lax.dot_general(cxb, dprb, dn_r, preferred_element_type=F32)
        dwi_ref[...] += lax.dot_general(cxb, dpib, dn_r, preferred_element_type=F32)
        dcx = (dcx + lax.dot_general(dprb, wa_ref[...], dn_t, preferred_element_type=F32)
               + lax.dot_general(dpib, wi_ref[...], dn_t, preferred_element_type=F32))

        dcb_ref[...] += jnp.sum(dcx, axis=0, keepdims=True)
        for k in range(4):
            dcw_ref[k:k + 1, :] += jnp.sum(dcx * xs[k], axis=0, keepdims=True)
        d_ext = jnp.concatenate([dcx, dcx_carry[...]], axis=0)
        drx = dcx * cw_ref[3:4, :]
        for k in range(3):
            drx = drx + pltpu.roll(d_ext, tt + 8 - (3 - k), 0)[:tt, :] * cw_ref[k:k + 1, :]
        drx_ref[...] = drx.astype(BF16)
        dcx_carry[...] = dcx[0:8, :]

    out_shape = (jax.ShapeDtypeStruct((T, d_rnn), BF16), jax.ShapeDtypeStruct((T, d_rnn), BF16),
                 jax.ShapeDtypeStruct((4, d_rnn), F32), jax.ShapeDtypeStruct((1, d_rnn), F32),
                 jax.ShapeDtypeStruct((1, d_rnn), F32), jax.ShapeDtypeStruct((1, d_rnn), F32),
                 jax.ShapeDtypeStruct((1, d_rnn), F32), jax.ShapeDtypeStruct((G, gw, gw), F32),
                 jax.ShapeDtypeStruct((G, gw, gw), F32))
    return _call(
        body, name="rnn_bwd", out_shape=out_shape, grid=(G, nT),
        in_specs=[rx, rx_halo, ry, act, act_halo, act, cw, vec, wg, wg, vec, vec, vec],
        out_specs=(act, act, cw, vec, vec, vec, vec, wg, wg),
        scratch_shapes=[pltpu.VMEM((8, gw), F32), pltpu.VMEM((8, gw), F32)],
        args=(proj, proj, proj, h_all, h_all, d_b, conv_w, conv_b, wa_g, wi_g, ba, bi, lam),
        sem=("parallel", "arbitrary"), rider=rider)


def _merge_fwd(proj, gl_off, b_gate, y_attn, y_rnn, rider=None):
    T, D = y_attn.shape
    tm = _pick(T, (256, 128))
    ct = _pick(math.gcd(gl_off, D), (512, 256, 128))
    oa, orr, nd = gl_off // ct, (gl_off + D) // ct, D // ct

    def body(ga_ref, gr_ref, ba_ref, br_ref, ya_ref, yr_ref, m_ref):
        ga = jax.nn.sigmoid(ga_ref[...] + ba_ref[...])
        gr = jax.nn.sigmoid(gr_ref[...] + br_ref[...])
        m_ref[...] = (ga * ya_ref[...] + gr * yr_ref[...]).astype(BF16)

    blk = pl.BlockSpec((tm, ct), lambda i, j: (i, j))
    (merged,), carried = _call(
        body, name="merge_fwd", out_shape=[jax.ShapeDtypeStruct((T, D), BF16)], grid=(T // tm, nd),
        in_specs=[pl.BlockSpec((tm, ct), lambda i, j: (i, oa + j)), pl.BlockSpec((tm, ct), lambda i, j: (i, orr + j)),
                  pl.BlockSpec((1, ct), lambda i, j: (0, j)), pl.BlockSpec((1, ct), lambda i, j: (0, nd + j)),
                  blk, blk],
        out_specs=[blk], scratch_shapes=[], args=(proj, proj, b_gate, b_gate, y_attn, y_rnn),
        sem=("parallel", "parallel"), rider=rider)
    return merged, carried


def _merge_bwd(proj, gl_off, b_gate, y_attn, y_rnn, d_m):
    T, D = y_attn.shape
    tm = _pick(T, (256, 128))
    ct = _pick(math.gcd(gl_off, D), (512, 256, 128))
    oa, orr, nd = gl_off // ct, (gl_off + D) // ct, D // ct

    def body(ga_ref, gr_ref, ba_ref, br_ref, ya_ref, yr_ref, dm_ref,
             dya_ref, dyr_ref, dga_ref, dgr_ref, dba_ref, dbr_ref):
        i = pl.program_id(1)

        @pl.when(i == 0)
        def _():
            dba_ref[...] = jnp.zeros_like(dba_ref)
            dbr_ref[...] = jnp.zeros_like(dbr_ref)

        ga = jax.nn.sigmoid(ga_ref[...] + ba_ref[...])
        gr = jax.nn.sigmoid(gr_ref[...] + br_ref[...])
        dm = dm_ref[...]
        dya_ref[...] = (dm * ga).astype(BF16)
        dyr_ref[...] = (dm * gr).astype(BF16)
        dga = dm * ya_ref[...] * ga * (1.0 - ga)
        dgr = dm * yr_ref[...] * gr * (1.0 - gr)
        dga_ref[...] = dga.astype(BF16)
        dgr_ref[...] = dgr.astype(BF16)
        dba_ref[...] += jnp.sum(dga, axis=0, keepdims=True)
        dbr_ref[...] += jnp.sum(dgr, axis=0, keepdims=True)

    blk = pl.BlockSpec((tm, ct), lambda j, i: (i, j))
    vec = pl.BlockSpec((1, ct), lambda j, i: (0, j))
    act = jax.ShapeDtypeStruct((T, D), BF16)
    v1 = jax.ShapeDtypeStruct((1, D), F32)
    return pl.pallas_call(
        body, name="merge_bwd", out_shape=(act, act, act, act, v1, v1), grid=(nd, T // tm),
        in_specs=[pl.BlockSpec((tm, ct), lambda j, i: (i, oa + j)), pl.BlockSpec((tm, ct), lambda j, i: (i, orr + j)),
                  vec, pl.BlockSpec((1, ct), lambda j, i: (0, nd + j)), blk, blk, blk],
        out_specs=(blk, blk, blk, blk, vec, vec),
        compiler_params=_cparams(("parallel", "arbitrary")),
    )(proj, proj, b_gate, b_gate, y_attn, y_rnn, d_m)


def _ln_fwd(x_res, delta, g, b, name, rider=None):
    T, D = x_res.shape
    tm = _pick(T, (256, 128))

    def body(x_ref, d_ref, g_ref, b_ref, y_ref, yb_ref, xh_ref, rs_ref):
        z = ALPHA * x_ref[...] + d_ref[...]
        mu = jnp.mean(z, axis=1, keepdims=True)
        zc = z - mu
        var = jnp.mean(zc * zc, axis=1, keepdims=True)
        rstd = lax.rsqrt(var + LN_EPS)
        xh = zc * rstd
        xh_ref[...] = xh
        rs_ref[...] = rstd
        y = xh * g_ref[...] + b_ref[...]
        y_ref[...] = y
        yb_ref[...] = y.astype(BF16)

    row = pl.BlockSpec((tm, D), lambda i: (i, 0))
    vec = pl.BlockSpec((1, D), lambda i: (0, 0))
    return _call(
        body, name=name,
        out_shape=(jax.ShapeDtypeStruct((T, D), F32), jax.ShapeDtypeStruct((T, D), BF16),
                   jax.ShapeDtypeStruct((T, D), F32), jax.ShapeDtypeStruct((T, 1), F32)),
        grid=(T // tm,), in_specs=[row, row, vec, vec],
        out_specs=(row, row, row, pl.BlockSpec((tm, 1), lambda i: (i, 0))),
        scratch_shapes=[], args=(x_res, delta, g, b), sem=("parallel",), rider=rider)


def _ln_bwd_rows(dy, xh, rstd, g):
    dxh = dy * g
    m1 = jnp.mean(dxh, axis=1, keepdims=True)
    m2 = jnp.mean(dxh * xh, axis=1, keepdims=True)
    return rstd * (dxh - m1 - xh * m2)


def _ln_loss_bwd(x_res, delta, g, b, target):
    T, D = x_res.shape
    tm = _pick(T, (256, 128))

    def body(x_ref, d_ref, g_ref, b_ref, t_ref, dz_ref, dzb_ref, loss_ref, dg_ref, db_ref):
        i = pl.program_id(0)

        @pl.when(i == 0)
        def _():
            loss_ref[...] = jnp.zeros_like(loss_ref)
            dg_ref[...] = jnp.zeros_like(dg_ref)
            db_ref[...] = jnp.zeros_like(db_ref)

        z = ALPHA * x_ref[...] + d_ref[...]
        mu = jnp.mean(z, axis=1, keepdims=True)
        zc = z - mu
        var = jnp.mean(zc * zc, axis=1, keepdims=True)
        rstd = lax.rsqrt(var + LN_EPS)
        xh = zc * rstd
        gv = g_ref[...]
        err = xh * gv + b_ref[...] - t_ref[...]
        loss_ref[...] += 0.5 * jnp.sum(jnp.mean(err * err, axis=1, keepdims=True))
        dy = err * (1.0 / D)
        dg_ref[...] += jnp.sum(dy * xh, axis=0, keepdims=True)
        db_ref[...] += jnp.sum(dy, axis=0, keepdims=True)
        dz = _ln_bwd_rows(dy, xh, rstd, gv)
        dz_ref[...] = dz
        dzb_ref[...] = dz.astype(BF16)

    row = pl.BlockSpec((tm, D), lambda i: (i, 0))
    vec = pl.BlockSpec((1, D), lambda i: (0, 0))
    return pl.pallas_call(
        body, name="ln2_loss_bwd",
        out_shape=(jax.ShapeDtypeStruct((T, D), F32), jax.ShapeDtypeStruct((T, D), BF16),
                   jax.ShapeDtypeStruct((8, LANES), F32),
                   jax.ShapeDtypeStruct((1, D), F32), jax.ShapeDtypeStruct((1, D), F32)),
        grid=(T // tm,), in_specs=[row, row, vec, vec, row],
        out_specs=(row, row, pl.BlockSpec((8, LANES), lambda i: (0, 0)), vec, vec),
        compiler_params=_cparams(("arbitrary",)),
    )(x_res, delta, g, b, target)


def _ln_bwd(dy, xh, rstd, g):
    T, D = dy.shape
    tm = _pick(T, (256, 128))

    def body(dy_ref, xh_ref, rs_ref, g_ref, dz_ref, dzb_ref, dg_ref, db_ref):
        i = pl.program_id(0)

        @pl.when(i == 0)
        def _():
            dg_ref[...] = jnp.zeros_like(dg_ref)
            db_ref[...] = jnp.zeros_like(db_ref)

        dyv, xhv = dy_ref[...], xh_ref[...]
        dg_ref[...] += jnp.sum(dyv * xhv, axis=0, keepdims=True)
        db_ref[...] += jnp.sum(dyv, axis=0, keepdims=True)
        dz = _ln_bwd_rows(dyv, xhv, rs_ref[...], g_ref[...])
        dz_ref[...] = dz
        dzb_ref[...] = dz.astype(BF16)

    row = pl.BlockSpec((tm, D), lambda i: (i, 0))
    vec = pl.BlockSpec((1, D), lambda i: (0, 0))
    return pl.pallas_call(
        body, name="ln1_bwd",
        out_shape=(jax.ShapeDtypeStruct((T, D), F32), jax.ShapeDtypeStruct((T, D), BF16),
                   jax.ShapeDtypeStruct((1, D), F32), jax.ShapeDtypeStruct((1, D), F32)),
        grid=(T // tm,), in_specs=[row, row, pl.BlockSpec((tm, 1), lambda i: (i, 0)), vec],
        out_specs=(row, row, vec, vec), compiler_params=_cparams(("arbitrary",)),
    )(dy, xh, rstd, g)


def _ffn_col_tile(T, d_ff):
    return _pick(d_ff, (256, 128)) if T >= 1024 else _pick(d_ff, (512, 256, 128))


def _ffn_gate(gp, cw_ref, cb_ref):
    return (cb_ref[...] + gp * cw_ref[2:3, :] + _shift_down(gp, 1) * cw_ref[1:2, :]
            + _shift_down(gp, 2) * cw_ref[0:1, :])


def _ffn_fwd(up, gpre, conv_w, conv_b, rider=None):
    T, d_ff = up.shape
    ct = _ffn_col_tile(T, d_ff)

    def body(up_ref, gp_ref, cw_ref, cb_ref, f_ref):
        gate = _ffn_gate(gp_ref[...], cw_ref, cb_ref)
        f_ref[...] = (_gelu(gate) * up_ref[...]).astype(BF16)

    col = pl.BlockSpec((T, ct), lambda j: (0, j))
    (f,), carried = _call(
        body, name="ffn_act_fwd", out_shape=[jax.ShapeDtypeStruct((T, d_ff), BF16)], grid=(d_ff // ct,),
        in_specs=[col, col, pl.BlockSpec((3, ct), lambda j: (0, j)), pl.BlockSpec((1, ct), lambda j: (0, j))],
        out_specs=[col], scratch_shapes=[], args=(up, gpre, conv_w, conv_b), sem=("parallel",), rider=rider)
    return f, carried


def _ffn_bwd(up, gpre, conv_w, conv_b, d_f, after=None):
    T, d_ff = up.shape
    ct = _ffn_col_tile(T, d_ff)

    def body(up_ref, gp_ref, cw_ref, cb_ref, df_ref, dup_ref, dgp_ref, dcw_ref, dcb_ref):
        gp = gp_ref[...]
        gate = _ffn_gate(gp, cw_ref, cb_ref)
        gel, dgel = _gelu_and_grad(gate)
        df = df_ref[...]
        dup_ref[...] = (df * gel).astype(BF16)
        dgate = df * up_ref[...] * dgel
        dcb_ref[...] = jnp.sum(dgate, axis=0, keepdims=True)
        dcw_ref[2:3, :] = jnp.sum(dgate * gp, axis=0, keepdims=True)
        dcw_ref[1:2, :] = jnp.sum(dgate * _shift_down(gp, 1), axis=0, keepdims=True)
        dcw_ref[0:1, :] = jnp.sum(dgate * _shift_down(gp, 2), axis=0, keepdims=True)
        dgp = (dgate * cw_ref[2:3, :] + _shift_up(dgate, 1) * cw_ref[1:2, :]
               + _shift_up(dgate, 2) * cw_ref[0:1, :])
        dgp_ref[...] = dgp.astype(BF16)

    col = pl.BlockSpec((T, ct), lambda j: (0, j))
    w3 = pl.BlockSpec((3, ct), lambda j: (0, j))
    v1 = pl.BlockSpec((1, ct), lambda j: (0, j))
    return _call(
        body, name="ffn_act_bwd",
        out_shape=(jax.ShapeDtypeStruct((T, d_ff), BF16), jax.ShapeDtypeStruct((T, d_ff), BF16),
                   jax.ShapeDtypeStruct((3, d_ff), F32), jax.ShapeDtypeStruct((1, d_ff), F32)),
        grid=(d_ff // ct,), in_specs=[col, col, w3, v1, col], out_specs=(col, col, w3, v1),
        scratch_shapes=[], args=(up, gpre, conv_w, conv_b, d_f), sem=("parallel",), after=after)[0]


def _adamw(w, g, m, v, name, after=None):
    R, C = w.shape
    tr = _row_tile(R, C * 4, 8, budget=1280 * 1024)
    c1 = 1.0 / (1.0 - ADAM_B1 ** ADAM_STEP)
    c2 = 1.0 / (1.0 - ADAM_B2 ** ADAM_STEP)

    def body(w_ref, g_ref, m_ref, v_ref, go_ref, d_ref, nm_ref, nv_ref):
        gv = g_ref[...]
        go_ref[...] = gv
        nm = ADAM_B1 * m_ref[...] + (1.0 - ADAM_B1) * gv
        nv = ADAM_B2 * v_ref[...] + (1.0 - ADAM_B2) * (gv * gv)
        nm_ref[...] = nm
        nv_ref[...] = nv
        d_ref[...] = -ADAM_LR * ((nm * c1) / (jnp.sqrt(nv * c2) + ADAM_EPS) + ADAM_WD * w_ref[...])

    blk = pl.BlockSpec((tr, C), lambda r: (r, 0))
    sh = jax.ShapeDtypeStruct((R, C), F32)
    return _call(body, name=name, out_shape=(sh,) * 4, grid=(R // tr,), in_specs=[blk] * 4, out_specs=(blk,) * 4,
                 scratch_shapes=[], args=(w, g, m, v), sem=("parallel",), after=after)[0]


def _group_blocks(w_blocks, per):
    nb, bw, _ = w_blocks.shape
    G = nb // per
    w4 = w_blocks.reshape(G, per, bw, bw)
    rows = []
    for p in range(per):
        parts = [w4[:, p] if q == p else jnp.zeros((G, bw, bw), w_blocks.dtype) for q in range(per)]
        rows.append(jnp.concatenate(parts, axis=2))
    return jnp.concatenate(rows, axis=1)


def _ungroup_blocks(w_groups, per):
    G, gw, _ = w_groups.shape
    bw = gw // per
    blocks = [w_groups[:, p * bw:(p + 1) * bw, p * bw:(p + 1) * bw] for p in range(per)]
    return jnp.stack(blocks, axis=1).reshape(G * per, bw, bw)


def _pack(parts):
    flat = jnp.concatenate([p.reshape(-1).astype(F32) for p in parts])
    n = flat.shape[0]
    rows = -(-n // LANES)
    rows = -(-rows // PACK_ROW_MULT) * PACK_ROW_MULT
    flat = jnp.pad(flat, (0, rows * LANES - n))
    return flat.reshape(rows, LANES)


def _unpack(packed, shapes):
    flat = packed.reshape(-1)
    out, off = [], 0
    for s in shapes:
        n = math.prod(s)
        out.append(flat[off:off + n].reshape(s))
        off += n
    return out


def kernel(x, w_in, b_gate, rnn_conv_w, rnn_conv_b, lru_wa, lru_ba, lru_wi, lru_bi, lru_lambda, attn_sinks, w_attn_proj, w_rnn_proj, w_out, ln1_g, ln1_b, ffn_w_up, ffn_w_gate, ffn_conv_w, ffn_conv_b, ffn_w_down, ln2_g, ln2_b, loss_target, m_w_in, m_b_gate, m_rnn_conv_w, m_rnn_conv_b, m_lru_wa, m_lru_ba, m_lru_wi, m_lru_bi, m_lru_lambda, m_attn_sinks, m_w_attn_proj, m_w_rnn_proj, m_w_out, m_ln1_g, m_ln1_b, m_ffn_w_up, m_ffn_w_gate, m_ffn_conv_w, m_ffn_conv_b, m_ffn_w_down, m_ln2_g, m_ln2_b, v_w_in, v_b_gate, v_rnn_conv_w, v_rnn_conv_b, v_lru_wa, v_lru_ba, v_lru_wi, v_lru_bi, v_lru_lambda, v_attn_sinks, v_w_attn_proj, v_w_rnn_proj, v_w_out, v_ln1_g, v_ln1_b, v_ffn_w_up, v_ffn_w_gate, v_ffn_conv_w, v_ffn_conv_b, v_ffn_w_down, v_ln2_g, v_ln2_b):
    weights = dict(w_in=w_in, b_gate=b_gate, rnn_conv_w=rnn_conv_w, rnn_conv_b=rnn_conv_b, lru_wa=lru_wa,
                   lru_ba=lru_ba, lru_wi=lru_wi, lru_bi=lru_bi, lru_lambda=lru_lambda, attn_sinks=attn_sinks,
                   w_attn_proj=w_attn_proj, w_rnn_proj=w_rnn_proj, w_out=w_out, ln1_g=ln1_g, ln1_b=ln1_b,
                   ffn_w_up=ffn_w_up, ffn_w_gate=ffn_w_gate, ffn_conv_w=ffn_conv_w, ffn_conv_b=ffn_conv_b,
                   ffn_w_down=ffn_w_down, ln2_g=ln2_g, ln2_b=ln2_b)
    m_in = dict(w_in=m_w_in, b_gate=m_b_gate, rnn_conv_w=m_rnn_conv_w, rnn_conv_b=m_rnn_conv_b, lru_wa=m_lru_wa,
                lru_ba=m_lru_ba, lru_wi=m_lru_wi, lru_bi=m_lru_bi, lru_lambda=m_lru_lambda, attn_sinks=m_attn_sinks,
                w_attn_proj=m_w_attn_proj, w_rnn_proj=m_w_rnn_proj, w_out=m_w_out, ln1_g=m_ln1_g, ln1_b=m_ln1_b,
                ffn_w_up=m_ffn_w_up, ffn_w_gate=m_ffn_w_gate, ffn_conv_w=m_ffn_conv_w, ffn_conv_b=m_ffn_conv_b,
                ffn_w_down=m_ffn_w_down, ln2_g=m_ln2_g, ln2_b=m_ln2_b)
    v_in = dict(w_in=v_w_in, b_gate=v_b_gate, rnn_conv_w=v_rnn_conv_w, rnn_conv_b=v_rnn_conv_b, lru_wa=v_lru_wa,
                lru_ba=v_lru_ba, lru_wi=v_lru_wi, lru_bi=v_lru_bi, lru_lambda=v_lru_lambda, attn_sinks=v_attn_sinks,
                w_attn_proj=v_w_attn_proj, w_rnn_proj=v_w_rnn_proj, w_out=v_w_out, ln1_g=v_ln1_g, ln1_b=v_ln1_b,
                ffn_w_up=v_ffn_w_up, ffn_w_gate=v_ffn_w_gate, ffn_conv_w=v_ffn_conv_w, ffn_conv_b=v_ffn_conv_b,
                ffn_w_down=v_ffn_w_down, ln2_g=v_ln2_g, ln2_b=v_ln2_b)
    order = list(weights)

    assert x.shape[0] == 1 and w_in.shape[0] == 1, "one sequence per device, depth 1"
    T, D = x.shape[1], x.shape[2]
    nq = attn_sinks.shape[-1]
    nkv = nq // GROUP
    d_attn, d_kv = nq * HEAD_DIM, nkv * HEAD_DIM
    d_rnn = rnn_conv_b.shape[-1]
    d_ff = ffn_conv_b.shape[-1]
    n_blocks, bw = lru_wa.shape[1], lru_wa.shape[2]
    per = (bw * LANES // math.gcd(bw, LANES)) // bw
    gw = per * bw
    assert n_blocks % per == 0 and d_rnn == n_blocks * bw
    q_off, k_off, v_off = 0, d_attn, d_attn + d_kv
    rx_off = d_attn + 2 * d_kv
    ry_off = rx_off + d_rnn
    gl_off = ry_off + d_rnn
    d_in = gl_off + 2 * D
    assert w_in.shape[-1] * N_SHARDS == d_in
    assert k_off % d_kv == 0 and rx_off % gw == 0 and T % ATTN_BLOCK == 0

    xi, yi, ci = lax.axis_index("x"), lax.axis_index("y"), lax.axis_index("c")
    j_me = 2 * xi + yi
    jc_arr = jnp.stack([j_me, ci]).astype(jnp.int32)

    x0 = x[0]
    tgt = loss_target[0]
    big = ["w_in", "w_attn_proj", "w_rnn_proj", "w_out", "ffn_w_up", "ffn_w_gate", "ffn_w_down"]
    near, diag = (0, 1), (2,)
    order_arr = jnp.stack([j_me, j_me ^ 2, j_me ^ 1, j_me ^ 3]).astype(jnp.int32)

    rcw_s, fcw_s = _all_gather_small([rnn_conv_w[0], ffn_conv_w[0]])
    rcw = jnp.concatenate([rcw_s[j] for j in range(N_SHARDS)], axis=1)
    fcw = jnp.concatenate([fcw_s[j] for j in range(N_SHARDS)], axis=1)

    own = {"w_in": _cast_bf16_into_slot(w_in[0], jc_arr, "cast_w_in", fcw_s)}
    in_near = _gather_step(own["w_in"], fcw_s, "gather_start_w_in", start_d2d=False, relations=near)
    last = in_near[2]
    for n in big[1:]:
        own[n] = last = _cast_bf16_into_slot(weights[n][0], jc_arr, "cast_" + n, last)
    x0b = _cast_bf16(x0, "cast_x", last)

    wa_g = _group_blocks(lru_wa[0], per).astype(BF16)
    wi_g = _group_blocks(lru_wi[0], per).astype(BF16)

    proj = _mm_shards(x0b, in_near[1], order_arr, [0], "mm_proj_own", x0b)
    d2d_sems, relay_sems, buf, tok = _gather_relay_step(in_near[1], proj, "gather_forward_w_in_near", in_near[0])
    w_in_s = _gather_step(buf, tok, "gather_finish_w_in_near", sems_in=d2d_sems, relations=near)
    proj = _mm_shards(x0b, w_in_s, order_arr, [1, 2], "mm_proj_near", w_in_s, out=proj)
    d2d_sems, buf, tok = _gather_relay_step(w_in_s, proj, "gather_forward_w_in_diag", None, relay_in=relay_sems)
    w_in_s = _gather_step(buf, tok, "gather_finish_w_in_diag", sems_in=d2d_sems, relations=diag)
    proj = _mm_shards(x0b, w_in_s, order_arr, [3], "mm_proj_diag", w_in_s, out=proj)
    ici, last = {}, proj
    for n in big[1:]:
        ici[n] = _gather_step(own[n], last, "gather_start_" + n, start_d2d=False)
        last = ici[n][2]

    def forward_halves(n, after):
        sems, buf, _ = ici[n]
        return _gather_step(buf, after, "gather_forward_" + n, sems_in=sems, start_d2d=True)

    def gathered(d2d, after, n):
        sems, buf, _ = d2d
        return _gather_step(buf, after, "gather_finish_" + n, sems_in=sems)

    a_out = _attn_fwd(proj, attn_sinks, nq, (q_off, k_off, v_off), after=last)
    fw_ap = forward_halves("w_attn_proj", a_out)
    (b_out, h_all), _ = _rnn_fwd(proj, (rx_off, ry_off), rcw, rnn_conv_b, wa_g, wi_g, lru_ba, lru_bi, lru_lambda)
    fw_rp = forward_halves("w_rnn_proj", b_out)
    w_ap = gathered(fw_ap, b_out, "w_attn_proj").reshape(d_attn, D)
    y_attn = _mm(a_out, w_ap, name="mm_attn_proj")
    fw_o = forward_halves("w_out", y_attn)
    w_rp = gathered(fw_rp, y_attn, "w_rnn_proj").reshape(d_rnn, D)
    y_rnn = _mm(b_out, w_rp, name="mm_rnn_proj")
    merged, _ = _merge_fwd(proj, gl_off, b_gate, y_attn, y_rnn)
    w_o = gathered(fw_o, merged, "w_out").reshape(D, D)
    mix = _mm(merged, w_o, name="mm_out")
    fw_up = forward_halves("ffn_w_up", mix)
    (x1, x1b, xh1, rstd1), _ = _ln_fwd(x0, mix, ln1_g, ln1_b, "ln1_fwd")
    w_up_s = gathered(fw_up, x1b, "ffn_w_up")
    up = _mm(x1b, w_up_s, name="mm_up", b_shards=N_SHARDS)
    fw_gate = forward_halves("ffn_w_gate", up)
    w_gate_s = gathered(fw_gate, fw_gate[2], "ffn_w_gate")
    gpre = _mm(x1b, w_gate_s, name="mm_gate", b_shards=N_SHARDS)
    f_act, _ = _ffn_fwd(up, gpre, fcw, ffn_conv_b)
    fw_dn = forward_halves("ffn_w_down", f_act)
    w_dn = gathered(fw_dn, fw_dn[2], "ffn_w_down").reshape(d_ff, D)
    f_out = _mm(f_act, w_dn, name="mm_down")
    dz2, dz2b, loss_acc, dg2, db2 = _ln_loss_bwd(x1, f_out, ln2_g, ln2_b, tgt)

    def pair_sums(arrs, from_sibling, names):
        return [_pair_sum(g, la, jc_arr, "pair_sum_" + n) for g, la, n in zip(arrs, from_sibling, names)]

    def shard_sums(parts, landed, names):
        return [_shard_sum(cp, lb, jc_arr, "shard_sum_" + n) for cp, lb, n in zip(parts, landed, names)]

    halves = {}
    g_down = _mm(f_act, dz2b, name="mm_d_w_down", ta=True, out_dtype=BF16)
    g1 = [g_down.reshape(N_SHARDS, d_ff // N_SHARDS, D)]
    d_f, sib1 = _mm(dz2b, w_dn, name="mm_d_f", tb=True, rider=_pair_rider(g1))
    sent1 = _shard_exchange_start(pair_sums(g1, sib1, ["ffn_w_down"]), "shard_exchange_start_down")
    dup, dgp, d_fcw, d_fcb = _ffn_bwd(up, gpre, fcw, ffn_conv_b, d_f, after=sent1[4])
    g_up = _mm(x1b, dup, name="mm_d_w_up", ta=True, out_dtype=BF16, out_shards=N_SHARDS)
    g_gate = _mm(x1b, dgp, name="mm_d_w_gate", ta=True, out_dtype=BF16, out_shards=N_SHARDS)
    g2 = [g_up, g_gate]
    dx1_a, sib2 = _mm(dup, w_up_s, name="mm_dx1_up", tb=True, b_shards=N_SHARDS, adds=((ALPHA, dz2),),
                      rider=_pair_rider(g2))
    halves["ffn_w_down"], = shard_sums(*_shard_exchange_wait(sent1, dx1_a, "shard_exchange_wait_down"),
                                       ["ffn_w_down"])
    sent2 = _shard_exchange_start(pair_sums(g2, sib2, ["ffn_w_up", "ffn_w_gate"]), "shard_exchange_start_up_gate")
    dx1 = _mm(dgp, w_gate_s, name="mm_dx1_gate", tb=True, b_shards=N_SHARDS, adds=((1.0, dx1_a),), after=sent2[4])
    dz1, dz1b, dg1, db1 = _ln_bwd(dx1, xh1, rstd1, ln1_g)
    g_out = _mm(merged, dz1b, name="mm_d_w_out", ta=True, out_dtype=BF16)
    d_m = _mm(dz1b, w_o, name="mm_d_merged", tb=True)
    dya, dyr, dgl_a, dgl_r, dbg_a, dbg_r = _merge_bwd(proj, gl_off, b_gate, y_attn, y_rnn, d_m)
    g_ap = _mm(a_out, dya, name="mm_d_w_attn_proj", ta=True, out_dtype=BF16)
    g_rp = _mm(b_out, dyr, name="mm_d_w_rnn_proj", ta=True, out_dtype=BF16)
    names3 = ["w_out", "w_attn_proj", "w_rnn_proj"]
    g3 = [g_out.reshape(N_SHARDS, D // N_SHARDS, D), g_ap.reshape(N_SHARDS, d_attn // N_SHARDS, D),
          g_rp.reshape(N_SHARDS, d_rnn // N_SHARDS, D)]
    d_a = _mm(dya, w_ap, name="mm_d_attn", tb=True)
    d_b, sib3 = _mm(dyr, w_rp, name="mm_d_rnn", tb=True, rider=_pair_rider(g3))
    sent3 = _shard_exchange_start(pair_sums(g3, sib3, names3), "shard_exchange_start_mixers")
    dq, dk, dv, dsink = _attn_bwd(proj, d_a, attn_sinks, nq, (q_off, k_off, v_off), after=sent3[4])
    (drx, dry, d_rcw, d_rcb, d_ba, d_bi, d_lam, d_wa_g, d_wi_g), _ = _rnn_bwd(
        proj, (rx_off, ry_off), h_all, d_b, rcw, rnn_conv_b, wa_g, wi_g, lru_ba, lru_bi, lru_lambda)
    halves["ffn_w_up"], halves["ffn_w_gate"] = shard_sums(
        *_shard_exchange_wait(sent2, drx, "shard_exchange_wait_up_gate"), ["ffn_w_up", "ffn_w_gate"])
    d_proj = jnp.concatenate([dq, dk.astype(BF16), dv.astype(BF16), drx, dry, dgl_a, dgl_r], axis=1)
    ffn_names = ["ffn_w_down", "ffn_w_up", "ffn_w_gate"]
    g_in, shared_ffn = _mm(x0b, d_proj, name="mm_d_w_in", ta=True, out_dtype=BF16, out_shards=N_SHARDS,
                           rider=_share_rider([halves[n] for n in ffn_names]))
    halves["w_out"], halves["w_attn_proj"], halves["w_rnn_proj"] = shard_sums(
        *_shard_exchange_wait(sent3, g_in, "shard_exchange_wait_mixers"), names3)

    small_parts = [
        ("loss", loss_acc[0:1, 0:1]),
        ("b_gate", jnp.concatenate([dbg_a, dbg_r], axis=1)),
        ("rnn_conv_w", d_rcw), ("rnn_conv_b", d_rcb),
        ("lru_wa", _ungroup_blocks(d_wa_g, per)), ("lru_ba", d_ba),
        ("lru_wi", _ungroup_blocks(d_wi_g, per)), ("lru_bi", d_bi), ("lru_lambda", d_lam),
        ("attn_sinks", dsink[0:1, 0:nq]),
        ("ln1_g", dg1), ("ln1_b", db1),
        ("ffn_conv_w", d_fcw), ("ffn_conv_b", d_fcb),
        ("ln2_g", dg2), ("ln2_b", db2),
    ]
    packed = _pack([p for _, p in small_parts])
    rs = packed.shape[0]

    def whole(g):
        return g.reshape(2 * g.shape[1], g.shape[2])

    grads = {n: whole(g) for n, g in zip(ffn_names, shared_ffn)}
    out_g, out_d, out_m, out_v = {}, {}, {}, {}

    def adamw(n, after=None):
        shape = weights[n].shape
        two_d = (math.prod(shape[:-1]), shape[-1])
        g2, d2, m2, v2 = _adamw(weights[n].reshape(two_d), grads[n].reshape(two_d), m_in[n].reshape(two_d),
                                v_in[n].reshape(two_d), "adamw_" + n, after=after)
        out_g[n], out_d[n] = g2.reshape(shape), d2.reshape(shape)
        out_m[n], out_v[n] = m2.reshape(shape), v2.reshape(shape)

    g4 = [g_in, packed.reshape(N_SHARDS, rs // N_SHARDS, LANES)]
    sib4 = _run_rider(_pair_rider(g4), "pair_exchange_in_small")
    part4 = pair_sums(g4, sib4, ["w_in", "small"])
    grad_x, (lb_in, lb_small, *shared_mix) = _mm(
        d_proj, w_in_s, name="mm_d_x", tb=True, b_shards=N_SHARDS, adds=((ALPHA, dz1),),
        rider=_join_riders(_shard_exchange_rider(part4, _atoms([0], near) + _atoms([1])),
                           _share_rider([halves[n] for n in names3])))
    grads.update({n: whole(g) for n, g in zip(names3, shared_mix)})
    sent5 = _shard_exchange_start(part4[:1], "shard_exchange_start_in_diag", relations=diag, lands=[lb_in])
    for n in ffn_names + names3:
        adamw(n, after=sent5[4])
    (part_in,), (lb_in,) = _shard_exchange_wait(sent5, out_d[names3[-1]], "shard_exchange_wait_in_diag")
    part_small = part4[1]
    halves["w_in"], = shard_sums([part_in], [lb_in], ["w_in"])
    eighths = _shard_sum(part_small, lb_small, jc_arr, "shard_sum_small", all_slots=True)
    shared_in, reduced = _run_rider(_share_rider([halves["w_in"]], eighths), "share_in_small")
    grads["w_in"] = whole(shared_in)
    reduced = reduced.reshape(rs, LANES)
    small = dict(zip([n for n, _ in small_parts], _unpack(reduced, [p.shape for _, p in small_parts])))
    loss = small.pop("loss").reshape(())
    rcw_n = d_rnn // N_SHARDS
    fcw_n = d_ff // N_SHARDS
    small["rnn_conv_w"] = lax.dynamic_slice(small["rnn_conv_w"], (0, j_me * rcw_n), (4, rcw_n))
    small["ffn_conv_w"] = lax.dynamic_slice(small["ffn_conv_w"], (0, j_me * fcw_n), (3, fcw_n))
    for n, g in small.items():
        grads[n] = g

    for n in order:
        if n not in out_g:
            adamw(n)

    return (loss, grad_x.reshape(x.shape), *[out_g[n] for n in order], *[out_d[n] for n in order],
            *[out_m[n] for n in order], *[out_v[n] for n in order])
```

```python
import functools
import math

import jax
import jax.numpy as jnp
from jax import lax
from jax.experimental import pallas as pl
from jax.experimental.pallas import tpu as pltpu

F32 = jnp.float32
BF16 = jnp.bfloat16
MESH = pl.DeviceIdType.MESH

HEAD_DIM = 64
GROUP = 8
ATTN_BLOCK = 128
LRU_C = 8.0
LN_EPS = 1e-5
ALPHA = 2.0 ** 0.25
LANES = 128
N_SHARDS = 4
N_DEV = 8
VMEM_LIMIT = 56 * 1024 * 1024
MM_VMEM_BUDGET = 40 * 1024 * 1024
MM_MAX_TILE = 3072
PACK_ROW_MULT = 8 * 64
NEG = -1e30

ADAM_LR, ADAM_B1, ADAM_B2, ADAM_EPS, ADAM_WD, ADAM_STEP = 0.001, 0.9, 0.999, 1e-08, 0.01, 10

GELU_C = math.sqrt(2.0 / math.pi)
GELU_A = 0.044715


def _cparams(sem=None):
    kw = dict(vmem_limit_bytes=VMEM_LIMIT)
    if sem is not None:
        kw["dimension_semantics"] = sem
    return pltpu.CompilerParams(**kw)


def _pick(n, prefs):
    for p in prefs:
        if n % p == 0:
            return p
    return n


def _row_tile(rows, row_bytes, mult, budget=2 * 1024 * 1024):
    best = None
    for d in range(mult, rows + 1, mult):
        if rows % d == 0 and d * row_bytes <= budget:
            best = d
    return best if best is not None else rows


def _gelu(x):
    return 0.5 * x * (1.0 + jnp.tanh(GELU_C * (x + GELU_A * x * x * x)))


def _gelu_and_grad(x):
    t = jnp.tanh(GELU_C * (x + GELU_A * x * x * x))
    g = 0.5 * x * (1.0 + t)
    dg = 0.5 * (1.0 + t) + 0.5 * x * (1.0 - t * t) * GELU_C * (1.0 + 3.0 * GELU_A * x * x)
    return g, dg


def _shift_down(x, s, fill=0.0):
    row = lax.broadcasted_iota(jnp.int32, x.shape, 0)
    return jnp.where(row >= s, pltpu.roll(x, s, 0), fill)


def _shift_up(x, s, fill=0.0):
    n = x.shape[0]
    row = lax.broadcasted_iota(jnp.int32, x.shape, 0)
    return jnp.where(row < n - s, pltpu.roll(x, n - s, 0), fill)


def _mm(a, b, *, name, ta=False, tb=False, out_dtype=F32, adds=(), b_shards=1, out_shards=1,
        tm=None, tn=None, tk=None, rider=None, after=None):
    if ta:
        K, M = a.shape
    else:
        M, K = a.shape
    if b_shards > 1:
        n_sh = b.shape[-1]
        if tb:
            N = b.shape[1]
            assert b_shards * n_sh == K
        else:
            N = b_shards * n_sh
            assert b.shape[1] == K
    else:
        n_sh = None
        if tb:
            N = b.shape[0]
            assert b.shape[1] == K
        else:
            N = b.shape[1]
            assert b.shape[0] == K
    wide = (1024, 1536, 1280, 768, 640, 512, 256, 128)
    if tn is None:
        if b_shards > 1 and not tb:
            tn = n_sh if n_sh <= MM_MAX_TILE else _pick(n_sh, wide)
        elif out_shards > 1:
            tn = N // out_shards if N // out_shards <= MM_MAX_TILE else _pick(N // out_shards, wide)
        else:
            tn = _pick(N, wide)
    if tk is None:
        if b_shards > 1 and tb:
            tk = n_sh if n_sh <= MM_MAX_TILE else _pick(n_sh, wide)
        else:
            tk = K if K <= MM_MAX_TILE else _pick(K, (2048,) + wide)
    assert N % tn == 0 and K % tk == 0, (name, M, N, K, tn, tk)
    nk = K // tk
    n_add = len(adds)
    sa, sb, so = a.dtype.itemsize, b.dtype.itemsize, jnp.dtype(out_dtype).itemsize

    def vmem_bytes(tm_):
        return (2 * (tm_ * tk * sa + tk * tn * sb + tm_ * tn * so + n_add * tm_ * tn * 4)
                + (tm_ * tn * 4 if nk > 1 else 0))

    if tm is None:
        tm = _pick(M, (1024, 512, 256, 128)) if nk > 1 else _pick(M, (512, 256, 128))
        while vmem_bytes(tm) > MM_VMEM_BUDGET and tm % 256 == 0:
            tm //= 2
    assert M % tm == 0, (name, M, tm)
    b_outer = b.size * sb >= a.size * sa

    def ij(g0, g1):
        return (g1, g0) if b_outer else (g0, g1)

    def amap(g0, g1, k):
        i, _ = ij(g0, g1)
        return (k, i) if ta else (i, k)

    def bmap(g0, g1, k):
        _, j = ij(g0, g1)
        if b_shards > 1 and not tb:
            per = n_sh // tn
            return (j // per, k, j % per)
        if b_shards > 1 and tb:
            per = n_sh // tk
            return (k // per, j, k % per)
        return (j, k) if tb else (k, j)

    def omap(g0, g1, k):
        i, j = ij(g0, g1)
        if out_shards > 1:
            per_o = (N // out_shards) // tn
            return (j // per_o, i, j % per_o)
        return (i, j)

    a_spec = pl.BlockSpec((tk, tm) if ta else (tm, tk), amap)
    if b_shards > 1:
        b_spec = pl.BlockSpec((None, tn, tk) if tb else (None, tk, tn), bmap)
    else:
        b_spec = pl.BlockSpec((tn, tk) if tb else (tk, tn), bmap)
    add_specs = [pl.BlockSpec((tm, tn), lambda g0, g1, k: ij(g0, g1)) for _ in adds]
    if out_shards > 1:
        out_spec = pl.BlockSpec((None, tm, tn), omap)
        out_shape = jax.ShapeDtypeStruct((out_shards, M, N // out_shards), out_dtype)
    else:
        out_spec = pl.BlockSpec((tm, tn), omap)
        out_shape = jax.ShapeDtypeStruct((M, N), out_dtype)

    if ta:
        dims = (((0,), (0,)), ((), ()))
    elif tb:
        dims = (((1,), (1,)), ((), ()))
    else:
        dims = (((1,), (0,)), ((), ()))
    scales = tuple(s for s, _ in adds)

    def finish(r, add_refs, o_ref):
        for s, ref in zip(scales, add_refs):
            r = r + s * ref[...].astype(F32)
        o_ref[...] = r.astype(out_dtype)

    def body(a_ref, b_ref, *rest):
        add_refs = rest[:n_add]
        o_ref = rest[n_add]
        part = lax.dot_general(a_ref[...].astype(BF16), b_ref[...].astype(BF16), dims, preferred_element_type=F32)
        if nk == 1:
            finish(part, add_refs, o_ref)
            return
        acc = rest[n_add + 1]
        k = pl.program_id(2)

        @pl.when(k == 0)
        def _():
            acc[...] = part

        @pl.when(k > 0)
        def _():
            acc[...] += part

        @pl.when(k == nk - 1)
        def _():
            finish(acc[...], add_refs, o_ref)

    grid = (N // tn, M // tm, nk) if b_outer else (M // tm, N // tn, nk)
    (res,), carried = _call(
        body, name=name, grid=grid, in_specs=[a_spec, b_spec] + add_specs, out_specs=[out_spec],
        out_shape=[out_shape], scratch_shapes=[pltpu.VMEM((tm, tn), F32)] if nk > 1 else [],
        args=(a, b, *[x for _, x in adds]), sem=("parallel", "parallel", "arbitrary"), rider=rider, after=after)
    return (res, carried) if rider is not None else res


def _cast_bf16(w, name, after):
    R, C = w.shape
    tr = _row_tile(R, C * 4, 16)

    def body(w_ref, after_ref, o_ref):
        o_ref[...] = w_ref[...].astype(BF16)

    return pl.pallas_call(
        body, name=name, out_shape=jax.ShapeDtypeStruct((R, C), BF16), grid=(R // tr,),
        in_specs=[pl.BlockSpec((tr, C), lambda r: (r, 0)), pl.BlockSpec(memory_space=pl.ANY)],
        out_specs=pl.BlockSpec((tr, C), lambda r: (r, 0)), compiler_params=_cparams(("parallel",)),
    )(w, after)


def _cast_bf16_into_slot(w, jc_arr, name, after):
    R, C = w.shape
    tr = _row_tile(R, C * 4, 16)

    def body(jc_ref, w_ref, after_ref, o_ref):
        o_ref[...] = w_ref[...].astype(BF16)

    gs = pltpu.PrefetchScalarGridSpec(
        num_scalar_prefetch=1, grid=(R // tr,),
        in_specs=[pl.BlockSpec((tr, C), lambda r, jc: (r, 0)), pl.BlockSpec(memory_space=pl.ANY)],
        out_specs=pl.BlockSpec((None, tr, C), lambda r, jc: (jc[0], r, 0)))
    return pl.pallas_call(body, name=name, out_shape=jax.ShapeDtypeStruct((N_SHARDS, R, C), BF16), grid_spec=gs,
                          compiler_params=_cparams(("parallel",)))(jc_arr, w, after)


def _pair_sum(g, la, jc_arr, name):
    S, R, C = g.shape
    half = R // 2
    tr = _row_tile(half, C * 4, 16)
    nrt = half // tr
    dt = g.dtype

    def body(jc_ref, g_ref, la_ref, o_ref):
        o_ref[...] = (g_ref[...].astype(F32) + la_ref[...].astype(F32)).astype(dt)

    gs = pltpu.PrefetchScalarGridSpec(
        num_scalar_prefetch=1, grid=(S, nrt),
        in_specs=[pl.BlockSpec((None, tr, C), lambda s, r, jc: (s, jc[1] * nrt + r, 0)),
                  pl.BlockSpec((None, tr, C), lambda s, r, jc: (s, r, 0))],
        out_specs=pl.BlockSpec((None, tr, C), lambda s, r, jc: (s, r, 0)))
    return pl.pallas_call(body, name=name, out_shape=jax.ShapeDtypeStruct((S, half, C), dt), grid_spec=gs,
                          compiler_params=_cparams(("parallel", "parallel")))(jc_arr, g, la)


def _shard_sum(cp, lb, jc_arr, name, all_slots=False):
    S, h, C = cp.shape
    tr = _row_tile(h, C * 4, 16)

    def body(jc_ref, cp_ref, l0, l1, l2, o_ref):
        o_ref[...] = ((cp_ref[...].astype(F32) + l0[...].astype(F32)) + l1[...].astype(F32)) + l2[...].astype(F32)

    def lspec(kk):
        return pl.BlockSpec((None, tr, C), lambda r, jc: (kk, r, 0))

    if all_slots:
        out_spec = pl.BlockSpec((None, None, tr, C), lambda r, jc: (jc[0], jc[1], r, 0))
        out_shape = jax.ShapeDtypeStruct((S, 2, h, C), F32)
    else:
        out_spec = pl.BlockSpec((None, tr, C), lambda r, jc: (jc[1], r, 0))
        out_shape = jax.ShapeDtypeStruct((2, h, C), F32)
    gs = pltpu.PrefetchScalarGridSpec(
        num_scalar_prefetch=1, grid=(h // tr,),
        in_specs=[pl.BlockSpec((None, tr, C), lambda r, jc: (jc[0], r, 0)), lspec(0), lspec(1), lspec(2)],
        out_specs=out_spec)
    return pl.pallas_call(body, name=name, out_shape=out_shape, grid_spec=gs,
                          compiler_params=_cparams(("parallel",)))(jc_arr, cp, lb, lb, lb)


ANY = pl.BlockSpec(memory_space=pl.ANY)


def _place():
    x, y, c = lax.axis_index("x"), lax.axis_index("y"), lax.axis_index("c")
    chips = [(1 - x, y), (x, 1 - y), (1 - x, 1 - y)]
    return x, y, c, chips


class _Rider:
    def __init__(self, inputs, out_shape, aliases, sems, start, finish):
        self.inputs, self.out_shape, self.aliases, self.sems = list(inputs), list(out_shape), dict(aliases), list(sems)
        self.start, self.finish = start, finish


def _join_riders(r1, r2):
    i1, o1, s1 = len(r1.inputs), len(r1.out_shape), len(r1.sems)
    aliases = dict(r1.aliases)
    aliases.update({i1 + i: o1 + o for i, o in r2.aliases.items()})

    def start(ins, outs, sems):
        r1.start(ins[:i1], outs[:o1], sems[:s1])
        r2.start(ins[i1:], outs[o1:], sems[s1:])

    def finish(ins, outs, sems):
        r1.finish(ins[:i1], outs[:o1], sems[:s1])
        r2.finish(ins[i1:], outs[o1:], sems[s1:])

    return _Rider(r1.inputs + r2.inputs, r1.out_shape + r2.out_shape, aliases, r1.sems + r2.sems, start, finish)


def _after_rider(x):
    return _Rider([x], [], {}, [], lambda *a: None, lambda *a: None)


def _call(body, *, name, grid, in_specs, out_specs, out_shape, scratch_shapes, args, sem, rider=None, after=None):
    out_specs, out_shape = tuple(out_specs), tuple(out_shape)
    if after is not None:
        rider = _after_rider(after) if rider is None else _join_riders(_after_rider(after), rider)
    if rider is None:
        res = pl.pallas_call(body, name=name, out_shape=out_shape, grid=grid, in_specs=list(in_specs),
                             out_specs=out_specs, scratch_shapes=list(scratch_shapes),
                             compiler_params=_cparams(sem))(*args)
        return tuple(res), []
    n_in, n_out, n_sc = len(in_specs), len(out_specs), len(scratch_shapes)
    r_in, r_out = len(rider.inputs), len(rider.out_shape)

    def wrapped(*refs):
        p = 0
        host_in = refs[p:p + n_in]; p += n_in
        rid_in = refs[p:p + r_in]; p += r_in
        host_out = refs[p:p + n_out]; p += n_out
        rid_out = refs[p:p + r_out]; p += r_out
        host_sc = refs[p:p + n_sc]; p += n_sc
        rid_sem = refs[p:]
        first = functools.reduce(jnp.logical_and, [pl.program_id(a) == 0 for a in range(len(grid))])
        last = functools.reduce(jnp.logical_and, [pl.program_id(a) == grid[a] - 1 for a in range(len(grid))])

        @pl.when(first)
        def _():
            rider.start(rid_in, rid_out, rid_sem)

        body(*host_in, *host_out, *host_sc)

        @pl.when(last)
        def _():
            rider.finish(rid_in, rid_out, rid_sem)

    res = pl.pallas_call(
        wrapped, name=name, out_shape=out_shape + tuple(rider.out_shape), grid=grid,
        in_specs=list(in_specs) + [ANY] * r_in, out_specs=out_specs + (ANY,) * r_out,
        input_output_aliases={n_in + i: n_out + o for i, o in rider.aliases.items()},
        scratch_shapes=list(scratch_shapes) + rider.sems,
        compiler_params=_cparams(("arbitrary",) * len(grid)),
    )(*args, *rider.inputs)
    return tuple(res[:n_out]), list(res[n_out:])


def _run_rider(rider, name):
    def body(*refs):
        r_in, r_out = len(rider.inputs), len(rider.out_shape)
        ins, outs, sems = refs[:r_in], refs[r_in:r_in + r_out], refs[r_in + r_out:]
        rider.start(ins, outs, sems)
        rider.finish(ins, outs, sems)

    return pl.pallas_call(
        body, name=name, out_shape=rider.out_shape, in_specs=[ANY] * len(rider.inputs),
        out_specs=[ANY] * len(rider.out_shape), input_output_aliases=rider.aliases, scratch_shapes=rider.sems,
    )(*rider.inputs)


def _atoms(indices, kks=(0, 1, 2), q=0, nq=1):
    return [(i, kk, q, nq) for i in indices for kk in kks]


def _mm_shards(a, buf, order_arr, which, name, after, out=None):
    M, K = a.shape
    S, _, n = buf.shape
    tm = _pick(M, (512, 256, 128))
    s0 = which[0]

    def body(order_ref, a_ref, b_ref, *rest):
        rest[-1][...] = jnp.dot(a_ref[...], b_ref[...], preferred_element_type=F32)

    gs = pltpu.PrefetchScalarGridSpec(
        num_scalar_prefetch=1, grid=(len(which), M // tm),
        in_specs=[pl.BlockSpec((tm, K), lambda g, i, order: (i, 0)),
                  pl.BlockSpec((None, K, n), lambda g, i, order: (order[s0 + g], 0, 0)), ANY]
        + ([ANY] if out is not None else []),
        out_specs=pl.BlockSpec((tm, n), lambda g, i, order: (i, order[s0 + g])))
    return pl.pallas_call(
        body, name=name, grid_spec=gs, out_shape=jax.ShapeDtypeStruct((M, S * n), F32),
        input_output_aliases={4: 0} if out is not None else {},
        compiler_params=_cparams(("arbitrary", "arbitrary")),
    )(order_arr, a, buf, after, *([out] if out is not None else []))


def _all_gather_small(shards):
    n = len(shards)

    def body(*refs):
        w = refs[:n]
        out = refs[n:2 * n]
        local_sem, s_sem, r_sem = refs[2 * n:]
        x, y, c, chips = _place()
        j_me = 2 * x + y
        cps = []
        for i in range(n):
            lc = pltpu.make_async_copy(w[i], out[i].at[j_me], local_sem.at[i])
            lc.start()
            cps.append(lc)
        sends = []
        for i in range(n):
            for kk, (px, py) in enumerate(chips):
                cp = pltpu.make_async_remote_copy(
                    src_ref=w[i], dst_ref=out[i].at[j_me], send_sem=s_sem.at[3 * i + kk],
                    recv_sem=r_sem.at[3 * i + kk], device_id=(px, py, c), device_id_type=MESH)
                cp.start()
                sends.append(cp)
        for i in range(n):
            for kk, (px, py) in enumerate(chips):
                sends[3 * i + kk].wait_send()
                pltpu.make_async_remote_copy(
                    src_ref=w[i], dst_ref=out[i].at[2 * px + py], send_sem=s_sem.at[3 * i + kk],
                    recv_sem=r_sem.at[3 * i + kk], device_id=(px, py, c), device_id_type=MESH).wait_recv()
        for lc in cps:
            lc.wait()

    out_shape = [jax.ShapeDtypeStruct((N_SHARDS,) + s.shape, s.dtype) for s in shards]
    return pl.pallas_call(
        body, name="all_gather_conv_weights", out_shape=out_shape, in_specs=[ANY] * n, out_specs=[ANY] * n,
        scratch_shapes=[pltpu.SemaphoreType.DMA((n,)), pltpu.SemaphoreType.DMA((3 * n,)),
                        pltpu.SemaphoreType.DMA((3 * n,))],
    )(*shards)


def _pair_rider(grads):
    n = len(grads)

    def copies(g, la, sems):
        x, y, c, _ = _place()
        return [pltpu.make_async_remote_copy(
            src_ref=g[i].at[:, pl.ds((1 - c) * (g[i].shape[1] // 2), g[i].shape[1] // 2), :], dst_ref=la[i],
            send_sem=sems[0].at[i], recv_sem=sems[1].at[i], device_id=(x, y, 1 - c), device_id_type=MESH)
            for i in range(n)]

    def start(g, la, sems):
        for cp in copies(g, la, sems):
            cp.start()

    def finish(g, la, sems):
        for cp in copies(g, la, sems):
            cp.wait()

    return _Rider(grads, [jax.ShapeDtypeStruct((s.shape[0], s.shape[1] // 2, s.shape[2]), s.dtype) for s in grads],
                  {}, [pltpu.SemaphoreType.DMA((n,)), pltpu.SemaphoreType.DMA((n,))], start, finish)


def _shard_exchange_rider(cps_in, atoms=None):
    n = len(cps_in)
    if atoms is None:
        atoms = _atoms(range(n))

    def copies(ins, lb, sems):
        x, y, c, chips = _place()
        out = []
        for a, (i, kk, q, nq) in enumerate(atoms):
            h = ins[i].shape[1]
            assert h % (16 * nq) == 0, (h, nq)
            rows = pl.ds(q * (h // nq), h // nq)
            px, py = chips[kk]
            out.append(pltpu.make_async_remote_copy(
                src_ref=ins[i].at[2 * px + py, rows, :], dst_ref=lb[i].at[kk, rows, :],
                send_sem=sems[0].at[a], recv_sem=sems[1].at[a], device_id=(px, py, c), device_id_type=MESH))
        return out

    def start(ins, lb, sems):
        for cp in copies(ins, lb, sems):
            cp.start()

    def finish(ins, lb, sems):
        for cp in copies(ins, lb, sems):
            cp.wait()

    return _Rider(cps_in, [jax.ShapeDtypeStruct((3,) + s.shape[1:], s.dtype) for s in cps_in], {},
                  [pltpu.SemaphoreType.DMA((len(atoms),)), pltpu.SemaphoreType.DMA((len(atoms),))], start, finish)


HBM = pl.BlockSpec(memory_space=pltpu.HBM)
SEM = pl.BlockSpec(memory_space=pltpu.SEMAPHORE)


def _shard_copies(part_refs, land_refs, send_sems, recv_sems, relations):
    x, y, c, chips = _place()
    nr = len(relations)
    return [pltpu.make_async_remote_copy(
        src_ref=part_refs[i].at[2 * chips[kk][0] + chips[kk][1]], dst_ref=land_refs[i].at[kk],
        send_sem=send_sems.at[nr * i + r], recv_sem=recv_sems.at[nr * i + r],
        device_id=(chips[kk][0], chips[kk][1], c), device_id_type=MESH)
        for i in range(len(part_refs)) for r, kk in enumerate(relations)]


SIDE_EFFECT = pltpu.SideEffectType.DATAFLOW_SIDE_EFFECTING


def _shard_exchange_start(parts, name, relations=(0, 1, 2), lands=None):
    n = len(parts)
    ns = n * len(relations)

    def body(*refs):
        part_refs, land_refs = refs[:n], refs[n:2 * n]
        send_sems, recv_sems = refs[2 * n], refs[2 * n + 1]
        token = refs[4 * n + 2]
        for cp in _shard_copies(part_refs, land_refs, send_sems, recv_sems, relations):
            cp.start()
        token[...] = jnp.zeros_like(token)

    if lands is None:
        lands = [lax.empty((3,) + p.shape[1:], p.dtype) for p in parts]
    bufs = list(parts) + list(lands)
    res = pl.pallas_call(
        body, name=name,
        out_shape=(pltpu.SemaphoreType.DMA((ns,)), pltpu.SemaphoreType.DMA((ns,)),
                   *[pltpu.HBM(b.shape, b.dtype) for b in bufs], jax.ShapeDtypeStruct((8, LANES), F32)),
        in_specs=(HBM,) * (2 * n), out_specs=(SEM, SEM) + (HBM,) * (2 * n) + (pl.BlockSpec(memory_space=pltpu.VMEM),),
        input_output_aliases={i: 2 + i for i in range(2 * n)},
        compiler_params=pltpu.CompilerParams(has_side_effects=SIDE_EFFECT),
    )(*[pltpu.with_memory_space_constraint(b, pltpu.HBM) for b in bufs])
    return res[0], res[1], list(res[2:2 + n]), list(res[2 + n:2 + 2 * n]), res[2 + 2 * n], relations


def _shard_exchange_wait(started, after, name):
    send_sems, recv_sems, parts, lands, _, relations = started
    n = len(parts)

    def body(*refs):
        part_refs, land_refs = refs[:n], refs[n:2 * n]
        send_sems_ref, recv_sems_ref = refs[2 * n], refs[2 * n + 1]
        for cp in _shard_copies(part_refs, land_refs, send_sems_ref, recv_sems_ref, relations):
            cp.wait_send()
            cp.wait_recv()

    bufs = parts + lands
    res = pl.pallas_call(
        body, name=name, out_shape=tuple(pltpu.HBM(b.shape, b.dtype) for b in bufs),
        in_specs=(HBM,) * (2 * n) + (SEM, SEM, ANY), out_specs=(HBM,) * (2 * n),
        input_output_aliases={i: i for i in range(2 * n)},
        compiler_params=pltpu.CompilerParams(has_side_effects=SIDE_EFFECT),
    )(*bufs, send_sems, recv_sems, after)
    return list(res[:n]), list(res[n:])


def _gather_copies(buf_ref, send_sems, recv_sems, over_d2d, arriving, relations):
    x, y, c, chips = _place()
    half = buf_ref.shape[1] // 2
    out = []
    for r, kk in enumerate(relations):
        px, py = chips[kk]
        if over_d2d:
            slot, core, peer = 2 * px + py, (1 - c) if arriving else c, (x, y, 1 - c)
        else:
            slot, core, peer = (2 * px + py) if arriving else (2 * x + y), c, (px, py, c)
        blk = buf_ref.at[slot, pl.ds(core * half, half), :]
        out.append(pltpu.make_async_remote_copy(src_ref=blk, dst_ref=blk, send_sem=send_sems.at[r],
                                                recv_sem=recv_sems.at[r], device_id=peer, device_id_type=MESH))
    return out


def _gather_step(buf, after, name, sems_in=None, start_d2d=None, relations=(0, 1, 2)):
    n_sem = 0 if sems_in is None else 2

    def body(*refs):
        buf_ref = refs[0]
        ins = refs[1:1 + n_sem]
        outs = refs[2 + n_sem:]
        if sems_in is not None:
            waited_d2d = start_d2d is None
            for mine, theirs in zip(_gather_copies(buf_ref, ins[0], ins[1], waited_d2d, False, relations),
                                    _gather_copies(buf_ref, ins[0], ins[1], waited_d2d, True, relations)):
                theirs.wait_recv()
                mine.wait_send()
        if start_d2d is not None:
            for cp in _gather_copies(buf_ref, outs[0], outs[1], start_d2d, False, relations):
                cp.start()
            outs[3][...] = jnp.zeros_like(outs[3])

    nr = len(relations)
    sem_out = () if start_d2d is None else (pltpu.SemaphoreType.DMA((nr,)), pltpu.SemaphoreType.DMA((nr,)))
    tok_out = () if start_d2d is None else (jax.ShapeDtypeStruct((8, LANES), F32),)
    res = pl.pallas_call(
        body, name=name,
        out_shape=sem_out + (pltpu.HBM(buf.shape, buf.dtype),) + tok_out,
        in_specs=(HBM,) + (SEM,) * n_sem + (ANY,),
        out_specs=(SEM,) * len(sem_out) + (HBM,) + (pl.BlockSpec(memory_space=pltpu.VMEM),) * len(tok_out),
        input_output_aliases={0: len(sem_out)},
        compiler_params=pltpu.CompilerParams(has_side_effects=SIDE_EFFECT),
    )(pltpu.with_memory_space_constraint(buf, pltpu.HBM), *(sems_in or ()), after)
    if start_d2d is None:
        return res[0]
    return (res[0], res[1]), res[2], res[3]


def _relay_copies(buf_ref, send_sems, recv_sems, arriving):
    x, y, c, chips = _place()
    quarter = buf_ref.shape[1] // 4
    out = []
    for r, (src_kk, dst_kk) in enumerate(((0, 1), (1, 0))):
        slot = (2 * chips[2][0] + chips[2][1]) if arriving else (2 * chips[src_kk][0] + chips[src_kk][1])
        blk = buf_ref.at[slot, pl.ds((2 * c + r) * quarter, quarter), :]
        out.append(pltpu.make_async_remote_copy(
            src_ref=blk, dst_ref=blk, send_sem=send_sems.at[r], recv_sem=recv_sems.at[r],
            device_id=(chips[dst_kk][0], chips[dst_kk][1], c), device_id_type=MESH))
    return out


def _gather_relay_step(buf, after, name, sems_in, relay_in=None):
    first = relay_in is None
    ins_sems = sems_in if first else relay_in
    near, diag = (0, 1), (2,)

    def body(*refs):
        buf_ref, in_s, in_r = refs[0], refs[1], refs[2]
        outs = refs[4:]
        if first:
            for mine, theirs in zip(_gather_copies(buf_ref, in_s, in_r, False, False, near),
                                    _gather_copies(buf_ref, in_s, in_r, False, True, near)):
                theirs.wait_recv()
                mine.wait_send()
            for cp in _gather_copies(buf_ref, outs[0], outs[1], True, False, near):
                cp.start()
            for cp in _relay_copies(buf_ref, outs[2], outs[3], False):
                cp.start()
        else:
            for mine, theirs in zip(_relay_copies(buf_ref, in_s, in_r, False), _relay_copies(buf_ref, in_s, in_r, True)):
                theirs.wait_recv()
                mine.wait_send()
            for cp in _gather_copies(buf_ref, outs[0], outs[1], True, False, diag):
                cp.start()
        outs[-1][...] = jnp.zeros_like(outs[-1])

    def sem(n):
        return pltpu.SemaphoreType.DMA((n,))

    sem_out = (sem(2), sem(2), sem(2), sem(2)) if first else (sem(1), sem(1))
    res = pl.pallas_call(
        body, name=name,
        out_shape=sem_out + (pltpu.HBM(buf.shape, buf.dtype), jax.ShapeDtypeStruct((8, LANES), F32)),
        in_specs=(HBM, SEM, SEM, ANY),
        out_specs=(SEM,) * len(sem_out) + (HBM, pl.BlockSpec(memory_space=pltpu.VMEM)),
        input_output_aliases={0: len(sem_out)},
        compiler_params=pltpu.CompilerParams(has_side_effects=SIDE_EFFECT),
    )(pltpu.with_memory_space_constraint(buf, pltpu.HBM), *ins_sems, after)
    if first:
        return (res[0], res[1]), (res[2], res[3]), res[4], res[5]
    return (res[0], res[1]), res[2], res[3]


def _share_rider(halves, eighths=None):
    n = len(halves)
    bufs = list(halves) + ([eighths] if eighths is not None else [])

    def half_copy(out, sems, i, core):
        x, y, c, _ = _place()
        blk = out[i].at[core]
        return pltpu.make_async_remote_copy(src_ref=blk, dst_ref=blk, send_sem=sems[0].at[i], recv_sem=sems[1].at[i],
                                            device_id=(x, y, 1 - c), device_id_type=MESH)

    def eighth_copy(out, sems, r, mine):
        x, y, c, _ = _place()
        px, py, pc = x ^ ((r >> 2) & 1), y ^ ((r >> 1) & 1), c ^ (r & 1)
        blk = out[n].at[2 * x + y, c] if mine else out[n].at[2 * px + py, pc]
        return pltpu.make_async_remote_copy(src_ref=blk, dst_ref=blk, send_sem=sems[2].at[r - 1],
                                            recv_sem=sems[3].at[r - 1], device_id=(px, py, pc), device_id_type=MESH)

    def start(ins, out, sems):
        c = lax.axis_index("c")
        for i in range(n):
            half_copy(out, sems, i, c).start()
        if eighths is not None:
            for r in range(1, N_DEV):
                eighth_copy(out, sems, r, True).start()

    def finish(ins, out, sems):
        c = lax.axis_index("c")
        for i in range(n):
            half_copy(out, sems, i, 1 - c).wait_recv()
        if eighths is not None:
            for r in range(1, N_DEV):
                eighth_copy(out, sems, r, False).wait_recv()
        for i in range(n):
            half_copy(out, sems, i, c).wait_send()
        if eighths is not None:
            for r in range(1, N_DEV):
                eighth_copy(out, sems, r, True).wait_send()

    return _Rider(bufs, [jax.ShapeDtypeStruct(s.shape, s.dtype) for s in bufs], {i: i for i in range(len(bufs))},
                  [pltpu.SemaphoreType.DMA((max(n, 1),)), pltpu.SemaphoreType.DMA((max(n, 1),)),
                   pltpu.SemaphoreType.DMA((N_DEV - 1,)), pltpu.SemaphoreType.DMA((N_DEV - 1,))], start, finish)


ATTN_ROWS = GROUP * ATTN_BLOCK
ATTN_KEYS = 2 * ATTN_BLOCK


def _attn_geometry(n):
    row = lax.broadcasted_iota(jnp.int32, (ATTN_ROWS, ATTN_KEYS), 0)
    col = lax.broadcasted_iota(jnp.int32, (ATTN_ROWS, ATTN_KEYS), 1)
    dist = ATTN_BLOCK + jnp.bitwise_and(row, ATTN_BLOCK - 1) - col
    valid = jnp.logical_and(jnp.logical_and(dist >= 0, dist < ATTN_BLOCK),
                            jnp.logical_or(col >= ATTN_BLOCK, n > 0))
    return dist.astype(F32), valid


def _per_head_column(values):
    head = lax.broadcasted_iota(jnp.int32, (ATTN_ROWS, 1), 0) // ATTN_BLOCK
    col = jnp.zeros((ATTN_ROWS, 1), F32)
    for hh, v in enumerate(values):
        col = jnp.where(head == hh, v, col)
    return col


def _stack_heads(ref, g):
    return jnp.concatenate(
        [ref[:, (g * GROUP + hh) * HEAD_DIM:(g * GROUP + hh + 1) * HEAD_DIM].astype(BF16) for hh in range(GROUP)],
        axis=0)


def _attn_probs(q_s, k2, slope_col, sink_col, dist, valid):
    s = lax.dot_general(q_s, k2, (((1,), (1,)), ((), ())), preferred_element_type=F32) * (HEAD_DIM ** -0.5)
    s = jnp.where(valid, s - slope_col * dist, NEG)
    m = jnp.maximum(jnp.max(s, axis=1, keepdims=True), sink_col)
    e = jnp.exp(s - m)
    es = jnp.exp(sink_col - m)
    inv = 1.0 / (jnp.sum(e, axis=1, keepdims=True) + es)
    return e * inv, es * inv


def _attn_specs(T, d_attn, d_kv, q_blk, k_blk, v_blk):
    bq = pl.BlockSpec((ATTN_BLOCK, d_attn), lambda n: (n, q_blk))
    kp = pl.BlockSpec((ATTN_BLOCK, d_kv), lambda n: (jnp.maximum(n - 1, 0), k_blk))
    kc = pl.BlockSpec((ATTN_BLOCK, d_kv), lambda n: (n, k_blk))
    vp = pl.BlockSpec((ATTN_BLOCK, d_kv), lambda n: (jnp.maximum(n - 1, 0), v_blk))
    vc = pl.BlockSpec((ATTN_BLOCK, d_kv), lambda n: (n, v_blk))
    return bq, kp, kc, vp, vc


def _attn_fwd(proj, sinks, nq, cols, after=None):
    T = proj.shape[0]
    nkv = nq // GROUP
    d_attn, d_kv = nq * HEAD_DIM, nkv * HEAD_DIM
    q_off, k_off, v_off = cols
    bq, kp, kc, vp, vc = _attn_specs(T, d_attn, d_kv, q_off // d_attn, k_off // d_kv, v_off // d_kv)

    def body(sink_ref, q_ref, kp_ref, kc_ref, vp_ref, vc_ref, o_ref):
        n = pl.program_id(0)
        dist, valid = _attn_geometry(n)
        for g in range(nkv):
            ks = slice(g * HEAD_DIM, (g + 1) * HEAD_DIM)
            k2 = jnp.concatenate([kp_ref[:, ks], kc_ref[:, ks]], axis=0).astype(BF16)
            v2 = jnp.concatenate([vp_ref[:, ks], vc_ref[:, ks]], axis=0).astype(BF16)
            slope_col = _per_head_column([2.0 ** (-8.0 * (g * GROUP + hh + 1) / nq) for hh in range(GROUP)])
            sink_col = _per_head_column([sink_ref[0, g * GROUP + hh] for hh in range(GROUP)])
            p, _ = _attn_probs(_stack_heads(q_ref, g), k2, slope_col, sink_col, dist, valid)
            o = jnp.dot(p.astype(BF16), v2, preferred_element_type=F32).astype(BF16)
            for hh in range(GROUP):
                h = g * GROUP + hh
                o_ref[:, h * HEAD_DIM:(h + 1) * HEAD_DIM] = o[hh * ATTN_BLOCK:(hh + 1) * ATTN_BLOCK, :]

    (out,), carried = _call(
        body, name="attn_fwd", out_shape=[jax.ShapeDtypeStruct((T, d_attn), BF16)], grid=(T // ATTN_BLOCK,),
        in_specs=[pl.BlockSpec(memory_space=pltpu.SMEM), bq, kp, kc, vp, vc],
        out_specs=[pl.BlockSpec((ATTN_BLOCK, d_attn), lambda n: (n, 0))], scratch_shapes=[],
        args=(sinks, proj, proj, proj, proj, proj), sem=("parallel",), after=after)
    return out


def _attn_bwd(proj, d_attn_out, sinks, nq, cols, after=None):
    T = proj.shape[0]
    nkv = nq // GROUP
    d_attn, d_kv = nq * HEAD_DIM, nkv * HEAD_DIM
    q_off, k_off, v_off = cols
    bq, kp, kc, vp, vc = _attn_specs(T, d_attn, d_kv, q_off // d_attn, k_off // d_kv, v_off // d_kv)
    scale = HEAD_DIM ** -0.5
    dn_t = (((1,), (1,)), ((), ()))
    dn_r = (((0,), (0,)), ((), ()))

    def body(sink_ref, q_ref, kp_ref, kc_ref, vp_ref, vc_ref, do_ref, dq_ref, dk_ref, dv_ref, ds_ref):
        n = pl.program_id(0)

        @pl.when(n == 0)
        def _():
            dk_ref[...] = jnp.zeros_like(dk_ref)
            dv_ref[...] = jnp.zeros_like(dv_ref)
            ds_ref[...] = jnp.zeros_like(ds_ref)

        dist, valid = _attn_geometry(n)
        rows_c = pl.ds(pl.multiple_of(n * ATTN_BLOCK, ATTN_BLOCK), ATTN_BLOCK)
        rows_p = pl.ds(pl.multiple_of(jnp.maximum(n - 1, 0) * ATTN_BLOCK, ATTN_BLOCK), ATTN_BLOCK)
        lane = lax.broadcasted_iota(jnp.int32, ds_ref.shape, 1)
        srow = lax.broadcasted_iota(jnp.int32, ds_ref.shape, 0)
        ds_acc = jnp.zeros(ds_ref.shape, F32)
        for g in range(nkv):
            ks = slice(g * HEAD_DIM, (g + 1) * HEAD_DIM)
            k2 = jnp.concatenate([kp_ref[:, ks], kc_ref[:, ks]], axis=0).astype(BF16)
            v2 = jnp.concatenate([vp_ref[:, ks], vc_ref[:, ks]], axis=0).astype(BF16)
            slope_col = _per_head_column([2.0 ** (-8.0 * (g * GROUP + hh + 1) / nq) for hh in range(GROUP)])
            sink_col = _per_head_column([sink_ref[0, g * GROUP + hh] for hh in range(GROUP)])
            q_s = _stack_heads(q_ref, g)
            do_s = _stack_heads(do_ref, g)
            p, p_sink = _attn_probs(q_s, k2, slope_col, sink_col, dist, valid)
            dp = lax.dot_general(do_s, v2, dn_t, preferred_element_type=F32)
            delta = jnp.sum(p * dp, axis=1, keepdims=True)
            ds = (p * (dp - delta)).astype(BF16)
            sink_part = p_sink * delta
            dq = (jnp.dot(ds, k2, preferred_element_type=F32) * scale).astype(BF16)
            for hh in range(GROUP):
                h = g * GROUP + hh
                blk = slice(hh * ATTN_BLOCK, (hh + 1) * ATTN_BLOCK)
                dq_ref[:, h * HEAD_DIM:(h + 1) * HEAD_DIM] = dq[blk, :]
                ds_acc = ds_acc + jnp.where(jnp.logical_and(lane == h, srow == 0), -jnp.sum(sink_part[blk, :]), 0.0)
            dk2 = lax.dot_general(ds, q_s, dn_r, preferred_element_type=F32) * scale
            dv2 = lax.dot_general(p.astype(BF16), do_s, dn_r, preferred_element_type=F32)
            dk_ref[rows_p, ks] += dk2[:ATTN_BLOCK, :]
            dv_ref[rows_p, ks] += dv2[:ATTN_BLOCK, :]
            dk_ref[rows_c, ks] += dk2[ATTN_BLOCK:, :]
            dv_ref[rows_c, ks] += dv2[ATTN_BLOCK:, :]
        ds_ref[...] += ds_acc

    out_shape = (jax.ShapeDtypeStruct((T, d_attn), BF16), jax.ShapeDtypeStruct((T, d_kv), F32),
                 jax.ShapeDtypeStruct((T, d_kv), F32), jax.ShapeDtypeStruct((8, LANES), F32))
    return _call(
        body, name="attn_bwd", out_shape=out_shape, grid=(T // ATTN_BLOCK,),
        in_specs=[pl.BlockSpec(memory_space=pltpu.SMEM), bq, kp, kc, vp, vc,
                  pl.BlockSpec((ATTN_BLOCK, d_attn), lambda n: (n, 0))],
        out_specs=(pl.BlockSpec((ATTN_BLOCK, d_attn), lambda n: (n, 0)),
                   pl.BlockSpec((T, d_kv), lambda n: (0, 0)), pl.BlockSpec((T, d_kv), lambda n: (0, 0)),
                   pl.BlockSpec((8, LANES), lambda n: (0, 0))),
        scratch_shapes=[], args=(sinks, proj, proj, proj, proj, proj, d_attn_out), sem=("arbitrary",), after=after)[0]


def _rnn_tile(T):
    return _pick(T, (256, 128))


def _rnn_gates(x_ext, cw_ref, cb_ref, wa_ref, wi_ref, ba_ref, bi_ref, lam_ref, tt):
    xs = [pltpu.roll(x_ext, 3 - k, 0)[8:, :] if k < 3 else x_ext[8:, :] for k in range(4)]
    cx = cb_ref[...] + xs[0] * cw_ref[0:1, :]
    for k in range(1, 4):
        cx = cx + xs[k] * cw_ref[k:k + 1, :]
    cxb = cx.astype(BF16)
    r = jax.nn.sigmoid(jnp.dot(cxb, wa_ref[...], preferred_element_type=F32) + ba_ref[...])
    i = jax.nn.sigmoid(jnp.dot(cxb, wi_ref[...], preferred_element_type=F32) + bi_ref[...])
    lam = lam_ref[...]
    sp = jnp.maximum(-lam, 0.0) + jnp.log1p(jnp.exp(-jnp.abs(lam)))
    log_a = -LRU_C * r * sp
    a = jnp.exp(log_a)
    z = 2.0 * log_a
    em1 = jnp.where(z > -1e-2, z * (1.0 + z * (0.5 + z * (1.0 / 6.0 + z * (1.0 / 24.0)))), jnp.exp(z) - 1.0)
    s = jnp.sqrt(-em1)
    return xs, cx, r, i, sp, a, s


def _rnn_specs(T, gw, tt, rx_blk, ry_blk, rev):
    nT = T // tt
    hb = tt // 8

    def tile(t):
        return (nT - 1 - t) if rev else t

    rx = pl.BlockSpec((tt, gw), lambda g, t: (tile(t), rx_blk + g))
    rx_halo = pl.BlockSpec((8, gw), lambda g, t: (jnp.maximum(tile(t) * hb - 1, 0), rx_blk + g))
    ry = pl.BlockSpec((tt, gw), lambda g, t: (tile(t), ry_blk + g))
    cw = pl.BlockSpec((4, gw), lambda g, t: (0, g))
    vec = pl.BlockSpec((1, gw), lambda g, t: (0, g))
    wg = pl.BlockSpec((None, gw, gw), lambda g, t: (g, 0, 0))
    act = pl.BlockSpec((tt, gw), lambda g, t: (tile(t), g))
    act_halo = pl.BlockSpec((8, gw), lambda g, t: (jnp.maximum(tile(t) * hb - 1, 0), g))
    return rx, rx_halo, ry, cw, vec, wg, act, act_halo, tile


def _rnn_fwd(proj, cols, conv_w, conv_b, wa_g, wi_g, ba, bi, lam, rider=None):
    T = proj.shape[0]
    G, gw, _ = wa_g.shape
    d_rnn = G * gw
    tt = _rnn_tile(T)
    rx_off, ry_off = cols
    rx, rx_halo, ry, cw, vec, wg, act, _, _ = _rnn_specs(T, gw, tt, rx_off // gw, ry_off // gw, False)

    def body(rx_ref, rxh_ref, ry_ref, cw_ref, cb_ref, wa_ref, wi_ref, ba_ref, bi_ref, lam_ref,
             b_ref, h_ref, carry):
        t = pl.program_id(1)

        @pl.when(t == 0)
        def _():
            carry[...] = jnp.zeros_like(carry)

        halo = jnp.where(t > 0, rxh_ref[...], 0.0)
        x_ext = jnp.concatenate([halo, rx_ref[...]], axis=0)
        _, cx, _, i, _, a, s = _rnn_gates(x_ext, cw_ref, cb_ref, wa_ref, wi_ref, ba_ref, bi_ref, lam_ref, tt)
        acc_a, acc_b = a, s * (i * cx)
        d = 1
        while d < tt:
            acc_b = acc_a * _shift_down(acc_b, d, 0.0) + acc_b
            acc_a = acc_a * _shift_down(acc_a, d, 1.0)
            d *= 2
        h = acc_b + acc_a * carry[7:8, :]
        carry[...] = h[tt - 8:, :]
        h_ref[...] = h
        b_ref[...] = (h * _gelu(ry_ref[...])).astype(BF16)

    return _call(
        body, name="rnn_fwd",
        out_shape=(jax.ShapeDtypeStruct((T, d_rnn), BF16), jax.ShapeDtypeStruct((T, d_rnn), F32)),
        grid=(G, T // tt),
        in_specs=[rx, rx_halo, ry, cw, vec, wg, wg, vec, vec, vec], out_specs=(act, act),
        scratch_shapes=[pltpu.VMEM((8, gw), F32)],
        args=(proj, proj, proj, conv_w, conv_b, wa_g, wi_g, ba, bi, lam), sem=("parallel", "arbitrary"), rider=rider)


def _rnn_bwd(proj, cols, h_all, d_b, conv_w, conv_b, wa_g, wi_g, ba, bi, lam, rider=None):
    T = proj.shape[0]
    G, gw, _ = wa_g.shape
    d_rnn = G * gw
    tt = _rnn_tile(T)
    nT = T // tt
    rx_off, ry_off = cols
    rx, rx_halo, ry, cw, vec, wg, act, act_halo, _ = _rnn_specs(T, gw, tt, rx_off // gw, ry_off // gw, True)
    dn_t = (((1,), (1,)), ((), ()))
    dn_r = (((0,), (0,)), ((), ()))

    def body(rx_ref, rxh_ref, ry_ref, h_ref, hh_ref, db_ref, cw_ref, cb_ref, wa_ref, wi_ref, ba_ref, bi_ref, lam_ref,
             drx_ref, dry_ref, dcw_ref, dcb_ref, dba_ref, dbi_ref, dlam_ref, dwa_ref, dwi_ref,
             lam_carry, dcx_carry):
        t = pl.program_id(1)
        first_tile = t == nT - 1

        @pl.when(t == 0)
        def _():
            lam_carry[...] = jnp.zeros_like(lam_carry)
            dcx_carry[...] = jnp.zeros_like(dcx_carry)
            dcw_ref[...] = jnp.zeros_like(dcw_ref)
            dcb_ref[...] = jnp.zeros_like(dcb_ref)
            dba_ref[...] = jnp.zeros_like(dba_ref)
            dbi_ref[...] = jnp.zeros_like(dbi_ref)
            dlam_ref[...] = jnp.zeros_like(dlam_ref)
            dwa_ref[...] = jnp.zeros_like(dwa_ref)
            dwi_ref[...] = jnp.zeros_like(dwi_ref)

        halo = jnp.where(first_tile, 0.0, rxh_ref[...])
        x_ext = jnp.concatenate([halo, rx_ref[...]], axis=0)
        xs, cx, r, i, sp, a, s = _rnn_gates(x_ext, cw_ref, cb_ref, wa_ref, wi_ref, ba_ref, bi_ref, lam_ref, tt)
        h = h_ref[...]
        h_halo = jnp.where(first_tile, 0.0, hh_ref[...])
        h_prev = pltpu.roll(jnp.concatenate([h_halo, h], axis=0), 1, 0)[8:, :]
        gel, dgel = _gelu_and_grad(ry_ref[...])
        d_b_t = db_ref[...]
        dry_ref[...] = (d_b_t * h * dgel).astype(BF16)
        dh = d_b_t * gel

        acc_c = _shift_up(a, 1, 1.0)
        acc_l = dh
        d = 1
        while d < tt:
            acc_l = acc_c * _shift_up(acc_l, d, 0.0) + acc_l
            acc_c = acc_c * _shift_up(acc_c, d, 1.0)
            d *= 2
        lam_t = acc_l + acc_c * lam_carry[0:1, :]
        lam_carry[...] = (a * lam_t)[0:8, :]

        icx = i * cx
        d_s = lam_t * icx
        d_i = lam_t * s * cx
        dcx = lam_t * s * i
        d_a = lam_t * h_prev - d_s * (a / s)
        dlog_a = d_a * a
        d_r = dlog_a * (-LRU_C * sp)
        lam = lam_ref[...]
        dlam_ref[...] += jnp.sum(dlog_a * r, axis=0, keepdims=True) * (LRU_C * jax.nn.sigmoid(-lam))
        dpr = d_r * r * (1.0 - r)
        dpi = d_i * i * (1.0 - i)
        dba_ref[...] += jnp.sum(dpr, axis=0, keepdims=True)
        dbi_ref[...] += jnp.sum(dpi, axis=0, keepdims=True)
        cxb = cx.astype(BF16)
        dprb, dpib = dpr.astype(BF16), dpi.astype(BF16)
        dwa_ref[...] += lax.dot_general(cxb, dprb, dn_r, preferred_element_type=F32)
        dwi_ref[...] += lax.dot_general(cxb, dpib, dn_r, preferred_element_type=F32)
        dcx = (dcx + lax.dot_general(dprb, wa_ref[...], dn_t, preferred_element_type=F32)
               + lax.dot_general(dpib, wi_ref[...], dn_t, preferred_element_type=F32))

        dcb_ref[...] += jnp.sum(dcx, axis=0, keepdims=True)
        for k in range(4):
            dcw_ref[k:k + 1, :] += jnp.sum(dcx * xs[k], axis=0, keepdims=True)
        d_ext = jnp.concatenate([dcx, dcx_carry[...]], axis=0)
        drx = dcx * cw_ref[3:4, :]
        for k in range(3):
            drx = drx + pltpu.roll(d_ext, tt + 8 - (3 - k), 0)[:tt, :] * cw_ref[k:k + 1, :]
        drx_ref[...] = drx.astype(BF16)
        dcx_carry[...] = dcx[0:8, :]

    out_shape = (jax.ShapeDtypeStruct((T, d_rnn), BF16), jax.ShapeDtypeStruct((T, d_rnn), BF16),
                 jax.ShapeDtypeStruct((4, d_rnn), F32), jax.ShapeDtypeStruct((1, d_rnn), F32),
                 jax.ShapeDtypeStruct((1, d_rnn), F32), jax.ShapeDtypeStruct((1, d_rnn), F32),
                 jax.ShapeDtypeStruct((1, d_rnn), F32), jax.ShapeDtypeStruct((G, gw, gw), F32),
                 jax.ShapeDtypeStruct((G, gw, gw), F32))
    return _call(
        body, name="rnn_bwd", out_shape=out_shape, grid=(G, nT),
        in_specs=[rx, rx_halo, ry, act, act_halo, act, cw, vec, wg, wg, vec, vec, vec],
        out_specs=(act, act, cw, vec, vec, vec, vec, wg, wg),
        scratch_shapes=[pltpu.VMEM((8, gw), F32), pltpu.VMEM((8, gw), F32)],
        args=(proj, proj, proj, h_all, h_all, d_b, conv_w, conv_b, wa_g, wi_g, ba, bi, lam),
        sem=("parallel", "arbitrary"), rider=rider)


def _merge_fwd(proj, gl_off, b_gate, y_attn, y_rnn, rider=None):
    T, D = y_attn.shape
    tm = _pick(T, (256, 128))
    ct = _pick(math.gcd(gl_off, D), (512, 256, 128))
    oa, orr, nd = gl_off // ct, (gl_off + D) // ct, D // ct

    def body(ga_ref, gr_ref, ba_ref, br_ref, ya_ref, yr_ref, m_ref):
        ga = jax.nn.sigmoid(ga_ref[...] + ba_ref[...])
        gr = jax.nn.sigmoid(gr_ref[...] + br_ref[...])
        m_ref[...] = (ga * ya_ref[...] + gr * yr_ref[...]).astype(BF16)

    blk = pl.BlockSpec((tm, ct), lambda i, j: (i, j))
    (merged,), carried = _call(
        body, name="merge_fwd", out_shape=[jax.ShapeDtypeStruct((T, D), BF16)], grid=(T // tm, nd),
        in_specs=[pl.BlockSpec((tm, ct), lambda i, j: (i, oa + j)), pl.BlockSpec((tm, ct), lambda i, j: (i, orr + j)),
                  pl.BlockSpec((1, ct), lambda i, j: (0, j)), pl.BlockSpec((1, ct), lambda i, j: (0, nd + j)),
                  blk, blk],
        out_specs=[blk], scratch_shapes=[], args=(proj, proj, b_gate, b_gate, y_attn, y_rnn),
        sem=("parallel", "parallel"), rider=rider)
    return merged, carried


def _merge_bwd(proj, gl_off, b_gate, y_attn, y_rnn, d_m):
    T, D = y_attn.shape
    tm = _pick(T, (256, 128))
    ct = _pick(math.gcd(gl_off, D), (512, 256, 128))
    oa, orr, nd = gl_off // ct, (gl_off + D) // ct, D // ct

    def body(ga_ref, gr_ref, ba_ref, br_ref, ya_ref, yr_ref, dm_ref,
             dya_ref, dyr_ref, dga_ref, dgr_ref, dba_ref, dbr_ref):
        i = pl.program_id(1)

        @pl.when(i == 0)
        def _():
            dba_ref[...] = jnp.zeros_like(dba_ref)
            dbr_ref[...] = jnp.zeros_like(dbr_ref)

        ga = jax.nn.sigmoid(ga_ref[...] + ba_ref[...])
        gr = jax.nn.sigmoid(gr_ref[...] + br_ref[...])
        dm = dm_ref[...]
        dya_ref[...] = (dm * ga).astype(BF16)
        dyr_ref[...] = (dm * gr).astype(BF16)
        dga = dm * ya_ref[...] * ga * (1.0 - ga)
        dgr = dm * yr_ref[...] * gr * (1.0 - gr)
        dga_ref[...] = dga.astype(BF16)
        dgr_ref[...] = dgr.astype(BF16)
        dba_ref[...] += jnp.sum(dga, axis=0, keepdims=True)
        dbr_ref[...] += jnp.sum(dgr, axis=0, keepdims=True)

    blk = pl.BlockSpec((tm, ct), lambda j, i: (i, j))
    vec = pl.BlockSpec((1, ct), lambda j, i: (0, j))
    act = jax.ShapeDtypeStruct((T, D), BF16)
    v1 = jax.ShapeDtypeStruct((1, D), F32)
    return pl.pallas_call(
        body, name="merge_bwd", out_shape=(act, act, act, act, v1, v1), grid=(nd, T // tm),
        in_specs=[pl.BlockSpec((tm, ct), lambda j, i: (i, oa + j)), pl.BlockSpec((tm, ct), lambda j, i: (i, orr + j)),
                  vec, pl.BlockSpec((1, ct), lambda j, i: (0, nd + j)), blk, blk, blk],
        out_specs=(blk, blk, blk, blk, vec, vec),
        compiler_params=_cparams(("parallel", "arbitrary")),
    )(proj, proj, b_gate, b_gate, y_attn, y_rnn, d_m)


def _ln_fwd(x_res, delta, g, b, name, rider=None):
    T, D = x_res.shape
    tm = _pick(T, (256, 128))

    def body(x_ref, d_ref, g_ref, b_ref, y_ref, yb_ref, xh_ref, rs_ref):
        z = ALPHA * x_ref[...] + d_ref[...]
        mu = jnp.mean(z, axis=1, keepdims=True)
        zc = z - mu
        var = jnp.mean(zc * zc, axis=1, keepdims=True)
        rstd = lax.rsqrt(var + LN_EPS)
        xh = zc * rstd
        xh_ref[...] = xh
        rs_ref[...] = rstd
        y = xh * g_ref[...] + b_ref[...]
        y_ref[...] = y
        yb_ref[...] = y.astype(BF16)

    row = pl.BlockSpec((tm, D), lambda i: (i, 0))
    vec = pl.BlockSpec((1, D), lambda i: (0, 0))
    return _call(
        body, name=name,
        out_shape=(jax.ShapeDtypeStruct((T, D), F32), jax.ShapeDtypeStruct((T, D), BF16),
                   jax.ShapeDtypeStruct((T, D), F32), jax.ShapeDtypeStruct((T, 1), F32)),
        grid=(T // tm,), in_specs=[row, row, vec, vec],
        out_specs=(row, row, row, pl.BlockSpec((tm, 1), lambda i: (i, 0))),
        scratch_shapes=[], args=(x_res, delta, g, b), sem=("parallel",), rider=rider)


def _ln_bwd_rows(dy, xh, rstd, g):
    dxh = dy * g
    m1 = jnp.mean(dxh, axis=1, keepdims=True)
    m2 = jnp.mean(dxh * xh, axis=1, keepdims=True)
    return rstd * (dxh - m1 - xh * m2)


def _ln_loss_bwd(x_res, delta, g, b, target):
    T, D = x_res.shape
    tm = _pick(T, (256, 128))

    def body(x_ref, d_ref, g_ref, b_ref, t_ref, dz_ref, dzb_ref, loss_ref, dg_ref, db_ref):
        i = pl.program_id(0)

        @pl.when(i == 0)
        def _():
            loss_ref[...] = jnp.zeros_like(loss_ref)
            dg_ref[...] = jnp.zeros_like(dg_ref)
            db_ref[...] = jnp.zeros_like(db_ref)

        z = ALPHA * x_ref[...] + d_ref[...]
        mu = jnp.mean(z, axis=1, keepdims=True)
        zc = z - mu
        var = jnp.mean(zc * zc, axis=1, keepdims=True)
        rstd = lax.rsqrt(var + LN_EPS)
        xh = zc * rstd
        gv = g_ref[...]
        err = xh * gv + b_ref[...] - t_ref[...]
        loss_ref[...] += 0.5 * jnp.sum(jnp.mean(err * err, axis=1, keepdims=True))
        dy = err * (1.0 / D)
        dg_ref[...] += jnp.sum(dy * xh, axis=0, keepdims=True)
        db_ref[...] += jnp.sum(dy, axis=0, keepdims=True)
        dz = _ln_bwd_rows(dy, xh, rstd, gv)
        dz_ref[...] = dz
        dzb_ref[...] = dz.astype(BF16)

    row = pl.BlockSpec((tm, D), lambda i: (i, 0))
    vec = pl.BlockSpec((1, D), lambda i: (0, 0))
    return pl.pallas_call(
        body, name="ln2_loss_bwd",
        out_shape=(jax.ShapeDtypeStruct((T, D), F32), jax.ShapeDtypeStruct((T, D), BF16),
                   jax.ShapeDtypeStruct((8, LANES), F32),
                   jax.ShapeDtypeStruct((1, D), F32), jax.ShapeDtypeStruct((1, D), F32)),
        grid=(T // tm,), in_specs=[row, row, vec, vec, row],
        out_specs=(row, row, pl.BlockSpec((8, LANES), lambda i: (0, 0)), vec, vec),
        compiler_params=_cparams(("arbitrary",)),
    )(x_res, delta, g, b, target)


def _ln_bwd(dy, xh, rstd, g):
    T, D = dy.shape
    tm = _pick(T, (256, 128))

    def body(dy_ref, xh_ref, rs_ref, g_ref, dz_ref, dzb_ref, dg_ref, db_ref):
        i = pl.program_id(0)

        @pl.when(i == 0)
        def _():
            dg_ref[...] = jnp.zeros_like(dg_ref)
            db_ref[...] = jnp.zeros_like(db_ref)

        dyv, xhv = dy_ref[...], xh_ref[...]
        dg_ref[...] += jnp.sum(dyv * xhv, axis=0, keepdims=True)
        db_ref[...] += jnp.sum(dyv, axis=0, keepdims=True)
        dz = _ln_bwd_rows(dyv, xhv, rs_ref[...], g_ref[...])
        dz_ref[...] = dz
        dzb_ref[...] = dz.astype(BF16)

    row = pl.BlockSpec((tm, D), lambda i: (i, 0))
    vec = pl.BlockSpec((1, D), lambda i: (0, 0))
    return pl.pallas_call(
        body, name="ln1_bwd",
        out_shape=(jax.ShapeDtypeStruct((T, D), F32), jax.ShapeDtypeStruct((T, D), BF16),
                   jax.ShapeDtypeStruct((1, D), F32), jax.ShapeDtypeStruct((1, D), F32)),
        grid=(T // tm,), in_specs=[row, row, pl.BlockSpec((tm, 1), lambda i: (i, 0)), vec],
        out_specs=(row, row, vec, vec), compiler_params=_cparams(("arbitrary",)),
    )(dy, xh, rstd, g)


def _ffn_col_tile(T, d_ff):
    return _pick(d_ff, (256, 128)) if T >= 1024 else _pick(d_ff, (512, 256, 128))


def _ffn_gate(gp, cw_ref, cb_ref):
    return (cb_ref[...] + gp * cw_ref[2:3, :] + _shift_down(gp, 1) * cw_ref[1:2, :]
            + _shift_down(gp, 2) * cw_ref[0:1, :])


def _ffn_fwd(up, gpre, conv_w, conv_b, rider=None):
    T, d_ff = up.shape
    ct = _ffn_col_tile(T, d_ff)

    def body(up_ref, gp_ref, cw_ref, cb_ref, f_ref):
        gate = _ffn_gate(gp_ref[...], cw_ref, cb_ref)
        f_ref[...] = (_gelu(gate) * up_ref[...]).astype(BF16)

    col = pl.BlockSpec((T, ct), lambda j: (0, j))
    (f,), carried = _call(
        body, name="ffn_act_fwd", out_shape=[jax.ShapeDtypeStruct((T, d_ff), BF16)], grid=(d_ff // ct,),
        in_specs=[col, col, pl.BlockSpec((3, ct), lambda j: (0, j)), pl.BlockSpec((1, ct), lambda j: (0, j))],
        out_specs=[col], scratch_shapes=[], args=(up, gpre, conv_w, conv_b), sem=("parallel",), rider=rider)
    return f, carried


def _ffn_bwd(up, gpre, conv_w, conv_b, d_f, after=None):
    T, d_ff = up.shape
    ct = _ffn_col_tile(T, d_ff)

    def body(up_ref, gp_ref, cw_ref, cb_ref, df_ref, dup_ref, dgp_ref, dcw_ref, dcb_ref):
        gp = gp_ref[...]
        gate = _ffn_gate(gp, cw_ref, cb_ref)
        gel, dgel = _gelu_and_grad(gate)
        df = df_ref[...]
        dup_ref[...] = (df * gel).astype(BF16)
        dgate = df * up_ref[...] * dgel
        dcb_ref[...] = jnp.sum(dgate, axis=0, keepdims=True)
        dcw_ref[2:3, :] = jnp.sum(dgate * gp, axis=0, keepdims=True)
        dcw_ref[1:2, :] = jnp.sum(dgate * _shift_down(gp, 1), axis=0, keepdims=True)
        dcw_ref[0:1, :] = jnp.sum(dgate * _shift_down(gp, 2), axis=0, keepdims=True)
        dgp = (dgate * cw_ref[2:3, :] + _shift_up(dgate, 1) * cw_ref[1:2, :]
               + _shift_up(dgate, 2) * cw_ref[0:1, :])
        dgp_ref[...] = dgp.astype(BF16)

    col = pl.BlockSpec((T, ct), lambda j: (0, j))
    w3 = pl.BlockSpec((3, ct), lambda j: (0, j))
    v1 = pl.BlockSpec((1, ct), lambda j: (0, j))
    return _call(
        body, name="ffn_act_bwd",
        out_shape=(jax.ShapeDtypeStruct((T, d_ff), BF16), jax.ShapeDtypeStruct((T, d_ff), BF16),
                   jax.ShapeDtypeStruct((3, d_ff), F32), jax.ShapeDtypeStruct((1, d_ff), F32)),
        grid=(d_ff // ct,), in_specs=[col, col, w3, v1, col], out_specs=(col, col, w3, v1),
        scratch_shapes=[], args=(up, gpre, conv_w, conv_b, d_f), sem=("parallel",), after=after)[0]


def _adamw(w, g, m, v, name, after=None):
    R, C = w.shape
    tr = _row_tile(R, C * 4, 8, budget=1280 * 1024)
    c1 = 1.0 / (1.0 - ADAM_B1 ** ADAM_STEP)
    c2 = 1.0 / (1.0 - ADAM_B2 ** ADAM_STEP)

    def body(w_ref, g_ref, m_ref, v_ref, go_ref, d_ref, nm_ref, nv_ref):
        gv = g_ref[...]
        go_ref[...] = gv
        nm = ADAM_B1 * m_ref[...] + (1.0 - ADAM_B1) * gv
        nv = ADAM_B2 * v_ref[...] + (1.0 - ADAM_B2) * (gv * gv)
        nm_ref[...] = nm
        nv_ref[...] = nv
        d_ref[...] = -ADAM_LR * ((nm * c1) / (jnp.sqrt(nv * c2) + ADAM_EPS) + ADAM_WD * w_ref[...])

    blk = pl.BlockSpec((tr, C), lambda r: (r, 0))
    sh = jax.ShapeDtypeStruct((R, C), F32)
    return _call(body, name=name, out_shape=(sh,) * 4, grid=(R // tr,), in_specs=[blk] * 4, out_specs=(blk,) * 4,
                 scratch_shapes=[], args=(w, g, m, v), sem=("parallel",), after=after)[0]


def _group_blocks(w_blocks, per):
    nb, bw, _ = w_blocks.shape
    G = nb // per
    w4 = w_blocks.reshape(G, per, bw, bw)
    rows = []
    for p in range(per):
        parts = [w4[:, p] if q == p else jnp.zeros((G, bw, bw), w_blocks.dtype) for q in range(per)]
        rows.append(jnp.concatenate(parts, axis=2))
    return jnp.concatenate(rows, axis=1)


def _ungroup_blocks(w_groups, per):
    G, gw, _ = w_groups.shape
    bw = gw // per
    blocks = [w_groups[:, p * bw:(p + 1) * bw, p * bw:(p + 1) * bw] for p in range(per)]
    return jnp.stack(blocks, axis=1).reshape(G * per, bw, bw)


def _pack(parts):
    flat = jnp.concatenate([p.reshape(-1).astype(F32) for p in parts])
    n = flat.shape[0]
    rows = -(-n // LANES)
    rows = -(-rows // PACK_ROW_MULT) * PACK_ROW_MULT
    flat = jnp.pad(flat, (0, rows * LANES - n))
    return flat.reshape(rows, LANES)


def _unpack(packed, shapes):
    flat = packed.reshape(-1)
    out, off = [], 0
    for s in shapes:
        n = math.prod(s)
        out.append(flat[off:off + n].reshape(s))
        off += n
    return out


def kernel(x, w_in, b_gate, rnn_conv_w, rnn_conv_b, lru_wa, lru_ba, lru_wi, lru_bi, lru_lambda, attn_sinks, w_attn_proj, w_rnn_proj, w_out, ln1_g, ln1_b, ffn_w_up, ffn_w_gate, ffn_conv_w, ffn_conv_b, ffn_w_down, ln2_g, ln2_b, loss_target, m_w_in, m_b_gate, m_rnn_conv_w, m_rnn_conv_b, m_lru_wa, m_lru_ba, m_lru_wi, m_lru_bi, m_lru_lambda, m_attn_sinks, m_w_attn_proj, m_w_rnn_proj, m_w_out, m_ln1_g, m_ln1_b, m_ffn_w_up, m_ffn_w_gate, m_ffn_conv_w, m_ffn_conv_b, m_ffn_w_down, m_ln2_g, m_ln2_b, v_w_in, v_b_gate, v_rnn_conv_w, v_rnn_conv_b, v_lru_wa, v_lru_ba, v_lru_wi, v_lru_bi, v_lru_lambda, v_attn_sinks, v_w_attn_proj, v_w_rnn_proj, v_w_out, v_ln1_g, v_ln1_b, v_ffn_w_up, v_ffn_w_gate, v_ffn_conv_w, v_ffn_conv_b, v_ffn_w_down, v_ln2_g, v_ln2_b):
    weights = dict(w_in=w_in, b_gate=b_gate, rnn_conv_w=rnn_conv_w, rnn_conv_b=rnn_conv_b, lru_wa=lru_wa,
                   lru_ba=lru_ba, lru_wi=lru_wi, lru_bi=lru_bi, lru_lambda=lru_lambda, attn_sinks=attn_sinks,
                   w_attn_proj=w_attn_proj, w_rnn_proj=w_rnn_proj, w_out=w_out, ln1_g=ln1_g, ln1_b=ln1_b,
                   ffn_w_up=ffn_w_up, ffn_w_gate=ffn_w_gate, ffn_conv_w=ffn_conv_w, ffn_conv_b=ffn_conv_b,
                   ffn_w_down=ffn_w_down, ln2_g=ln2_g, ln2_b=ln2_b)
    m_in = dict(w_in=m_w_in, b_gate=m_b_gate, rnn_conv_w=m_rnn_conv_w, rnn_conv_b=m_rnn_conv_b, lru_wa=m_lru_wa,
                lru_ba=m_lru_ba, lru_wi=m_lru_wi, lru_bi=m_lru_bi, lru_lambda=m_lru_lambda, attn_sinks=m_attn_sinks,
                w_attn_proj=m_w_attn_proj, w_rnn_proj=m_w_rnn_proj, w_out=m_w_out, ln1_g=m_ln1_g, ln1_b=m_ln1_b,
                ffn_w_up=m_ffn_w_up, ffn_w_gate=m_ffn_w_gate, ffn_conv_w=m_ffn_conv_w, ffn_conv_b=m_ffn_conv_b,
                ffn_w_down=m_ffn_w_down, ln2_g=m_ln2_g, ln2_b=m_ln2_b)
    v_in = dict(w_in=v_w_in, b_gate=v_b_gate, rnn_conv_w=v_rnn_conv_w, rnn_conv_b=v_rnn_conv_b, lru_wa=v_lru_wa,
                lru_ba=v_lru_ba, lru_wi=v_lru_wi, lru_bi=v_lru_bi, lru_lambda=v_lru_lambda, attn_sinks=v_attn_sinks,
                w_attn_proj=v_w_attn_proj, w_rnn_proj=v_w_rnn_proj, w_out=v_w_out, ln1_g=v_ln1_g, ln1_b=v_ln1_b,
                ffn_w_up=v_ffn_w_up, ffn_w_gate=v_ffn_w_gate, ffn_conv_w=v_ffn_conv_w, ffn_conv_b=v_ffn_conv_b,
                ffn_w_down=v_ffn_w_down, ln2_g=v_ln2_g, ln2_b=v_ln2_b)
    order = list(weights)

    assert x.shape[0] == 1 and w_in.shape[0] == 1, "one sequence per device, depth 1"
    T, D = x.shape[1], x.shape[2]
    nq = attn_sinks.shape[-1]
    nkv = nq // GROUP
    d_attn, d_kv = nq * HEAD_DIM, nkv * HEAD_DIM
    d_rnn = rnn_conv_b.shape[-1]
    d_ff = ffn_conv_b.shape[-1]
    n_blocks, bw = lru_wa.shape[1], lru_wa.shape[2]
    per = (bw * LANES // math.gcd(bw, LANES)) // bw
    gw = per * bw
    assert n_blocks % per == 0 and d_rnn == n_blocks * bw
    q_off, k_off, v_off = 0, d_attn, d_attn + d_kv
    rx_off = d_attn + 2 * d_kv
    ry_off = rx_off + d_rnn
    gl_off = ry_off + d_rnn
    d_in = gl_off + 2 * D
    assert w_in.shape[-1] * N_SHARDS == d_in
    assert k_off % d_kv == 0 and rx_off % gw == 0 and T % ATTN_BLOCK == 0

    xi, yi, ci = lax.axis_index("x"), lax.axis_index("y"), lax.axis_index("c")
    j_me = 2 * xi + yi
    jc_arr = jnp.stack([j_me, ci]).astype(jnp.int32)

    x0 = x[0]
    tgt = loss_target[0]
    big = ["w_in", "w_attn_proj", "w_rnn_proj", "w_out", "ffn_w_up", "ffn_w_gate", "ffn_w_down"]
    near, diag = (0, 1), (2,)
    order_arr = jnp.stack([j_me, j_me ^ 2, j_me ^ 1, j_me ^ 3]).astype(jnp.int32)

    rcw_s, fcw_s = _all_gather_small([rnn_conv_w[0], ffn_conv_w[0]])
    rcw = jnp.concatenate([rcw_s[j] for j in range(N_SHARDS)], axis=1)
    fcw = jnp.concatenate([fcw_s[j] for j in range(N_SHARDS)], axis=1)

    own = {"w_in": _cast_bf16_into_slot(w_in[0], jc_arr, "cast_w_in", fcw_s)}
    in_near = _gather_step(own["w_in"], fcw_s, "gather_start_w_in", start_d2d=False, relations=near)
    last = in_near[2]
    for n in big[1:]:
        own[n] = last = _cast_bf16_into_slot(weights[n][0], jc_arr, "cast_" + n, last)
    x0b = _cast_bf16(x0, "cast_x", last)

    lru_wa0, lru_wi0, _ = lax.optimization_barrier((lru_wa[0], lru_wi[0], in_near[2]))
    wa_g, wi_g = lax.optimization_barrier((_group_blocks(lru_wa0, per).astype(BF16),
                                           _group_blocks(lru_wi0, per).astype(BF16)))

    proj = _mm_shards(x0b, in_near[1], order_arr, [0], "mm_proj_own", wi_g)
    d2d_sems, relay_sems, buf, tok = _gather_relay_step(in_near[1], proj, "gather_forward_w_in_near", in_near[0])
    w_in_s = _gather_step(buf, tok, "gather_finish_w_in_near", sems_in=d2d_sems, relations=near)
    proj = _mm_shards(x0b, w_in_s, order_arr, [1, 2], "mm_proj_near", w_in_s, out=proj)
    d2d_sems, buf, tok = _gather_relay_step(w_in_s, proj, "gather_forward_w_in_diag", None, relay_in=relay_sems)
    w_in_s = _gather_step(buf, tok, "gather_finish_w_in_diag", sems_in=d2d_sems, relations=diag)
    proj = _mm_shards(x0b, w_in_s, order_arr, [3], "mm_proj_diag", w_in_s, out=proj)
    ici, last = {}, proj
    for n in big[1:]:
        ici[n] = _gather_step(own[n], last, "gather_start_" + n, start_d2d=False)
        last = ici[n][2]

    def forward_halves(n, after):
        sems, buf, _ = ici[n]
        return _gather_step(buf, after, "gather_forward_" + n, sems_in=sems, start_d2d=True)

    def gathered(d2d, after, n):
        sems, buf, _ = d2d
        return _gather_step(buf, after, "gather_finish_" + n, sems_in=sems)

    a_out = _attn_fwd(proj, attn_sinks, nq, (q_off, k_off, v_off), after=last)
    fw_ap = forward_halves("w_attn_proj", a_out)
    (b_out, h_all), _ = _rnn_fwd(proj, (rx_off, ry_off), rcw, rnn_conv_b, wa_g, wi_g, lru_ba, lru_bi, lru_lambda)
    fw_rp = forward_halves("w_rnn_proj", b_out)
    w_ap = gathered(fw_ap, b_out, "w_attn_proj").reshape(d_attn, D)
    y_attn = _mm(a_out, w_ap, name="mm_attn_proj")
    fw_o = forward_halves("w_out", y_attn)
    w_rp = gathered(fw_rp, y_attn, "w_rnn_proj").reshape(d_rnn, D)
    y_rnn = _mm(b_out, w_rp, name="mm_rnn_proj")
    merged, _ = _merge_fwd(proj, gl_off, b_gate, y_attn, y_rnn)
    w_o = gathered(fw_o, merged, "w_out").reshape(D, D)
    mix = _mm(merged, w_o, name="mm_out")
    fw_up = forward_halves("ffn_w_up", mix)
    (x1, x1b, xh1, rstd1), _ = _ln_fwd(x0, mix, ln1_g, ln1_b, "ln1_fwd")
    w_up_s = gathered(fw_up, x1b, "ffn_w_up")
    up = _mm(x1b, w_up_s, name="mm_up", b_shards=N_SHARDS)
    fw_gate = forward_halves("ffn_w_gate", up)
    w_gate_s = gathered(fw_gate, fw_gate[2], "ffn_w_gate")
    gpre = _mm(x1b, w_gate_s, name="mm_gate", b_shards=N_SHARDS)
    f_act, _ = _ffn_fwd(up, gpre, fcw, ffn_conv_b)
    fw_dn = forward_halves("ffn_w_down", f_act)
    w_dn = gathered(fw_dn, fw_dn[2], "ffn_w_down").reshape(d_ff, D)
    f_out = _mm(f_act, w_dn, name="mm_down")
    dz2, dz2b, loss_acc, dg2, db2 = _ln_loss_bwd(x1, f_out, ln2_g, ln2_b, tgt)

    def pair_sums(arrs, from_sibling, names):
        return [_pair_sum(g, la, jc_arr, "pair_sum_" + n) for g, la, n in zip(arrs, from_sibling, names)]

    def shard_sums(parts, landed, names):
        return [_shard_sum(cp, lb, jc_arr, "shard_sum_" + n) for cp, lb, n in zip(parts, landed, names)]

    halves = {}
    g_down = _mm(f_act, dz2b, name="mm_d_w_down", ta=True, out_dtype=BF16)
    g1 = [g_down.reshape(N_SHARDS, d_ff // N_SHARDS, D)]
    d_f, sib1 = _mm(dz2b, w_dn, name="mm_d_f", tb=True, rider=_pair_rider(g1))
    sent1 = _shard_exchange_start(pair_sums(g1, sib1, ["ffn_w_down"]), "shard_exchange_start_down")
    dup, dgp, d_fcw, d_fcb = _ffn_bwd(up, gpre, fcw, ffn_conv_b, d_f, after=sent1[4])
    g_up = _mm(x1b, dup, name="mm_d_w_up", ta=True, out_dtype=BF16, out_shards=N_SHARDS)
    g_gate = _mm(x1b, dgp, name="mm_d_w_gate", ta=True, out_dtype=BF16, out_shards=N_SHARDS)
    g2 = [g_up, g_gate]
    dx1_a, sib2 = _mm(dup, w_up_s, name="mm_dx1_up", tb=True, b_shards=N_SHARDS, adds=((ALPHA, dz2),),
                      rider=_pair_rider(g2))
    halves["ffn_w_down"], = shard_sums(*_shard_exchange_wait(sent1, dx1_a, "shard_exchange_wait_down"),
                                       ["ffn_w_down"])
    sent2 = _shard_exchange_start(pair_sums(g2, sib2, ["ffn_w_up", "ffn_w_gate"]), "shard_exchange_start_up_gate")
    dx1 = _mm(dgp, w_gate_s, name="mm_dx1_gate", tb=True, b_shards=N_SHARDS, adds=((1.0, dx1_a),), after=sent2[4])
    dz1, dz1b, dg1, db1 = _ln_bwd(dx1, xh1, rstd1, ln1_g)
    g_out = _mm(merged, dz1b, name="mm_d_w_out", ta=True, out_dtype=BF16)
    d_m = _mm(dz1b, w_o, name="mm_d_merged", tb=True)
    dya, dyr, dgl_a, dgl_r, dbg_a, dbg_r = _merge_bwd(proj, gl_off, b_gate, y_attn, y_rnn, d_m)
    g_ap = _mm(a_out, dya, name="mm_d_w_attn_proj", ta=True, out_dtype=BF16)
    g_rp = _mm(b_out, dyr, name="mm_d_w_rnn_proj", ta=True, out_dtype=BF16)
    names3 = ["w_out", "w_attn_proj", "w_rnn_proj"]
    g3 = [g_out.reshape(N_SHARDS, D // N_SHARDS, D), g_ap.reshape(N_SHARDS, d_attn // N_SHARDS, D),
          g_rp.reshape(N_SHARDS, d_rnn // N_SHARDS, D)]
    d_a = _mm(dya, w_ap, name="mm_d_attn", tb=True)
    d_b, sib3 = _mm(dyr, w_rp, name="mm_d_rnn", tb=True, rider=_pair_rider(g3))
    sent3 = _shard_exchange_start(pair_sums(g3, sib3, names3), "shard_exchange_start_mixers")
    dq, dk, dv, dsink = _attn_bwd(proj, d_a, attn_sinks, nq, (q_off, k_off, v_off), after=sent3[4])
    (drx, dry, d_rcw, d_rcb, d_ba, d_bi, d_lam, d_wa_g, d_wi_g), _ = _rnn_bwd(
        proj, (rx_off, ry_off), h_all, d_b, rcw, rnn_conv_b, wa_g, wi_g, lru_ba, lru_bi, lru_lambda)
    halves["ffn_w_up"], halves["ffn_w_gate"] = shard_sums(
        *_shard_exchange_wait(sent2, drx, "shard_exchange_wait_up_gate"), ["ffn_w_up", "ffn_w_gate"])
    d_proj = jnp.concatenate([dq, dk.astype(BF16), dv.astype(BF16), drx, dry, dgl_a, dgl_r], axis=1)
    ffn_names = ["ffn_w_down", "ffn_w_up", "ffn_w_gate"]
    g_in, shared_ffn = _mm(x0b, d_proj, name="mm_d_w_in", ta=True, out_dtype=BF16, out_shards=N_SHARDS,
                           rider=_share_rider([halves[n] for n in ffn_names]))
    halves["w_out"], halves["w_attn_proj"], halves["w_rnn_proj"] = shard_sums(
        *_shard_exchange_wait(sent3, g_in, "shard_exchange_wait_mixers"), names3)

    small_parts = [
        ("loss", loss_acc[0:1, 0:1]),
        ("b_gate", jnp.concatenate([dbg_a, dbg_r], axis=1)),
        ("rnn_conv_w", d_rcw), ("rnn_conv_b", d_rcb),
        ("lru_wa", _ungroup_blocks(d_wa_g, per)), ("lru_ba", d_ba),
        ("lru_wi", _ungroup_blocks(d_wi_g, per)), ("lru_bi", d_bi), ("lru_lambda", d_lam),
        ("attn_sinks", dsink[0:1, 0:nq]),
        ("ln1_g", dg1), ("ln1_b", db1),
        ("ffn_conv_w", d_fcw), ("ffn_conv_b", d_fcb),
        ("ln2_g", dg2), ("ln2_b", db2),
    ]
    packed = _pack([p for _, p in small_parts])
    rs = packed.shape[0]

    def whole(g):
        return g.reshape(2 * g.shape[1], g.shape[2])

    grads = {n: whole(g) for n, g in zip(ffn_names, shared_ffn)}
    out_g, out_d, out_m, out_v = {}, {}, {}, {}

    def adamw(n, after=None):
        shape = weights[n].shape
        two_d = (math.prod(shape[:-1]), shape[-1])
        g2, d2, m2, v2 = _adamw(weights[n].reshape(two_d), grads[n].reshape(two_d), m_in[n].reshape(two_d),
                                v_in[n].reshape(two_d), "adamw_" + n, after=after)
        out_g[n], out_d[n] = g2.reshape(shape), d2.reshape(shape)
        out_m[n], out_v[n] = m2.reshape(shape), v2.reshape(shape)

    g4 = [g_in, packed.reshape(N_SHARDS, rs // N_SHARDS, LANES)]
    sib4 = _run_rider(_pair_rider(g4), "pair_exchange_in_small")
    part4 = pair_sums(g4, sib4, ["w_in", "small"])
    grad_x, (lb_in, lb_small, *shared_mix) = _mm(
        d_proj, w_in_s, name="mm_d_x", tb=True, b_shards=N_SHARDS, adds=((ALPHA, dz1),),
        rider=_join_riders(_shard_exchange_rider(part4, _atoms([0], near) + _atoms([1])),
                           _share_rider([halves[n] for n in names3])))
    grads.update({n: whole(g) for n, g in zip(names3, shared_mix)})
    sent5 = _shard_exchange_start(part4[:1], "shard_exchange_start_in_diag", relations=diag, lands=[lb_in])
    for n in ffn_names + names3:
        adamw(n, after=sent5[4])
    (part_in,), (lb_in,) = _shard_exchange_wait(sent5, out_d[names3[-1]], "shard_exchange_wait_in_diag")
    part_small = part4[1]
    halves["w_in"], = shard_sums([part_in], [lb_in], ["w_in"])
    eighths = _shard_sum(part_small, lb_small, jc_arr, "shard_sum_small", all_slots=True)
    shared_in, reduced = _run_rider(_share_rider([halves["w_in"]], eighths), "share_in_small")
    grads["w_in"] = whole(shared_in)
    reduced = reduced.reshape(rs, LANES)
    small = dict(zip([n for n, _ in small_parts], _unpack(reduced, [p.shape for _, p in small_parts])))
    loss = small.pop("loss").reshape(())
    rcw_n = d_rnn // N_SHARDS
    fcw_n = d_ff // N_SHARDS
    small["rnn_conv_w"] = lax.dynamic_slice(small["rnn_conv_w"], (0, j_me * rcw_n), (4, rcw_n))
    small["ffn_conv_w"] = lax.dynamic_slice(small["ffn_conv_w"], (0, j_me * fcw_n), (3, fcw_n))
    for n, g in small.items():
        grads[n] = g

    for n in order:
        if n not in out_g:
            adamw(n)

    return (loss, grad_x.reshape(x.shape), *[out_g[n] for n in order], *[out_d[n] for n in order],
            *[out_m[n] for n in order], *[out_v[n] for n in order])
```

```python
import functools
import math

import jax
import jax.numpy as jnp
from jax import lax
from jax.experimental import pallas as pl
from jax.experimental.pallas import tpu as pltpu

F32 = jnp.float32
BF16 = jnp.bfloat16
MESH = pl.DeviceIdType.MESH

HEAD_DIM = 64
GROUP = 8
ATTN_BLOCK = 128
LRU_C = 8.0
LN_EPS = 1e-5
ALPHA = 2.0 ** 0.25
LANES = 128
N_SHARDS = 4
N_DEV = 8
VMEM_LIMIT = 56 * 1024 * 1024
MM_VMEM_BUDGET = 40 * 1024 * 1024
MM_MAX_TILE = 3072
PACK_ROW_MULT = 8 * 64
NEG = -1e30

ADAM_LR, ADAM_B1, ADAM_B2, ADAM_EPS, ADAM_WD, ADAM_STEP = 0.001, 0.9, 0.999, 1e-08, 0.01, 10

GELU_C = math.sqrt(2.0 / math.pi)
GELU_A = 0.044715


def _cparams(sem=None):
    kw = dict(vmem_limit_bytes=VMEM_LIMIT)
    if sem is not None:
        kw["dimension_semantics"] = sem
    return pltpu.CompilerParams(**kw)


def _pick(n, prefs):
    for p in prefs:
        if n % p == 0:
            return p
    return n


def _row_tile(rows, row_bytes, mult, budget=2 * 1024 * 1024):
    best = None
    for d in range(mult, rows + 1, mult):
        if rows % d == 0 and d * row_bytes <= budget:
            best = d
    return best if best is not None else rows


def _gelu(x):
    return 0.5 * x * (1.0 + jnp.tanh(GELU_C * (x + GELU_A * x * x * x)))


def _gelu_and_grad(x):
    t = jnp.tanh(GELU_C * (x + GELU_A * x * x * x))
    g = 0.5 * x * (1.0 + t)
    dg = 0.5 * (1.0 + t) + 0.5 * x * (1.0 - t * t) * GELU_C * (1.0 + 3.0 * GELU_A * x * x)
    return g, dg


def _shift_down(x, s, fill=0.0):
    row = lax.broadcasted_iota(jnp.int32, x.shape, 0)
    return jnp.where(row >= s, pltpu.roll(x, s, 0), fill)


def _shift_up(x, s, fill=0.0):
    n = x.shape[0]
    row = lax.broadcasted_iota(jnp.int32, x.shape, 0)
    return jnp.where(row < n - s, pltpu.roll(x, n - s, 0), fill)


def _mm(a, b, *, name, ta=False, tb=False, out_dtype=F32, adds=(), b_shards=1, out_shards=1,
        tm=None, tn=None, tk=None, rider=None, after=None):
    if ta:
        K, M = a.shape
    else:
        M, K = a.shape
    if b_shards > 1:
        n_sh = b.shape[-1]
        if tb:
            N = b.shape[1]
            assert b_shards * n_sh == K
        else:
            N = b_shards * n_sh
            assert b.shape[1] == K
    else:
        n_sh = None
        if tb:
            N = b.shape[0]
            assert b.shape[1] == K
        else:
            N = b.shape[1]
            assert b.shape[0] == K
    wide = (1024, 1536, 1280, 768, 640, 512, 256, 128)
    if tn is None:
        if b_shards > 1 and not tb:
            tn = n_sh if n_sh <= MM_MAX_TILE else _pick(n_sh, wide)
        elif out_shards > 1:
            tn = N // out_shards if N // out_shards <= MM_MAX_TILE else _pick(N // out_shards, wide)
        else:
            tn = _pick(N, wide)
    if tk is None:
        if b_shards > 1 and tb:
            tk = n_sh if n_sh <= MM_MAX_TILE else _pick(n_sh, wide)
        else:
            tk = K if K <= MM_MAX_TILE else _pick(K, (2048,) + wide)
    assert N % tn == 0 and K % tk == 0, (name, M, N, K, tn, tk)
    nk = K // tk
    n_add = len(adds)
    sa, sb, so = a.dtype.itemsize, b.dtype.itemsize, jnp.dtype(out_dtype).itemsize

    def vmem_bytes(tm_):
        return (2 * (tm_ * tk * sa + tk * tn * sb + tm_ * tn * so + n_add * tm_ * tn * 4)
                + (tm_ * tn * 4 if nk > 1 else 0))

    if tm is None:
        tm = _pick(M, (1024, 512, 256, 128)) if nk > 1 else _pick(M, (512, 256, 128))
        while vmem_bytes(tm) > MM_VMEM_BUDGET and tm % 256 == 0:
            tm //= 2
    assert M % tm == 0, (name, M, tm)
    b_outer = b.size * sb >= a.size * sa

    def ij(g0, g1):
        return (g1, g0) if b_outer else (g0, g1)

    def amap(g0, g1, k):
        i, _ = ij(g0, g1)
        return (k, i) if ta else (i, k)

    def bmap(g0, g1, k):
        _, j = ij(g0, g1)
        if b_shards > 1 and not tb:
            per = n_sh // tn
            return (j // per, k, j % per)
        if b_shards > 1 and tb:
            per = n_sh // tk
            return (k // per, j, k % per)
        return (j, k) if tb else (k, j)

    def omap(g0, g1, k):
        i, j = ij(g0, g1)
        if out_shards > 1:
            per_o = (N // out_shards) // tn
            return (j // per_o, i, j % per_o)
        return (i, j)

    a_spec = pl.BlockSpec((tk, tm) if ta else (tm, tk), amap)
    if b_shards > 1:
        b_spec = pl.BlockSpec((None, tn, tk) if tb else (None, tk, tn), bmap)
    else:
        b_spec = pl.BlockSpec((tn, tk) if tb else (tk, tn), bmap)
    add_specs = [pl.BlockSpec((tm, tn), lambda g0, g1, k: ij(g0, g1)) for _ in adds]
    if out_shards > 1:
        out_spec = pl.BlockSpec((None, tm, tn), omap)
        out_shape = jax.ShapeDtypeStruct((out_shards, M, N // out_shards), out_dtype)
    else:
        out_spec = pl.BlockSpec((tm, tn), omap)
        out_shape = jax.ShapeDtypeStruct((M, N), out_dtype)

    if ta:
        dims = (((0,), (0,)), ((), ()))
    elif tb:
        dims = (((1,), (1,)), ((), ()))
    else:
        dims = (((1,), (0,)), ((), ()))
    scales = tuple(s for s, _ in adds)

    def finish(r, add_refs, o_ref):
        for s, ref in zip(scales, add_refs):
            r = r + s * ref[...].astype(F32)
        o_ref[...] = r.astype(out_dtype)

    def body(a_ref, b_ref, *rest):
        add_refs = rest[:n_add]
        o_ref = rest[n_add]
        part = lax.dot_general(a_ref[...].astype(BF16), b_ref[...].astype(BF16), dims, preferred_element_type=F32)
        if nk == 1:
            finish(part, add_refs, o_ref)
            return
        acc = rest[n_add + 1]
        k = pl.program_id(2)

        @pl.when(k == 0)
        def _():
            acc[...] = part

        @pl.when(k > 0)
        def _():
            acc[...] += part

        @pl.when(k == nk - 1)
        def _():
            finish(acc[...], add_refs, o_ref)

    grid = (N // tn, M // tm, nk) if b_outer else (M // tm, N // tn, nk)
    (res,), carried = _call(
        body, name=name, grid=grid, in_specs=[a_spec, b_spec] + add_specs, out_specs=[out_spec],
        out_shape=[out_shape], scratch_shapes=[pltpu.VMEM((tm, tn), F32)] if nk > 1 else [],
        args=(a, b, *[x for _, x in adds]), sem=("parallel", "parallel", "arbitrary"), rider=rider, after=after)
    return (res, carried) if rider is not None else res


def _cast_bf16(w, name, after):
    R, C = w.shape
    tr = _row_tile(R, C * 4, 16)

    def body(w_ref, after_ref, o_ref):
        o_ref[...] = w_ref[...].astype(BF16)

    return pl.pallas_call(
        body, name=name, out_shape=jax.ShapeDtypeStruct((R, C), BF16), grid=(R // tr,),
        in_specs=[pl.BlockSpec((tr, C), lambda r: (r, 0)), pl.BlockSpec(memory_space=pl.ANY)],
        out_specs=pl.BlockSpec((tr, C), lambda r: (r, 0)), compiler_params=_cparams(("parallel",)),
    )(w, after)


def _cast_bf16_into_slot(w, jc_arr, name, after):
    R, C = w.shape
    tr = _row_tile(R, C * 4, 16)

    def body(jc_ref, w_ref, after_ref, o_ref):
        o_ref[...] = w_ref[...].astype(BF16)

    gs = pltpu.PrefetchScalarGridSpec(
        num_scalar_prefetch=1, grid=(R // tr,),
        in_specs=[pl.BlockSpec((tr, C), lambda r, jc: (r, 0)), pl.BlockSpec(memory_space=pl.ANY)],
        out_specs=pl.BlockSpec((None, tr, C), lambda r, jc: (jc[0], r, 0)))
    return pl.pallas_call(body, name=name, out_shape=jax.ShapeDtypeStruct((N_SHARDS, R, C), BF16), grid_spec=gs,
                          compiler_params=_cparams(("parallel",)))(jc_arr, w, after)


def _pair_sum(g, la, jc_arr, name):
    S, R, C = g.shape
    half = R // 2
    tr = _row_tile(half, C * 4, 16)
    nrt = half // tr
    dt = g.dtype

    def body(jc_ref, g_ref, la_ref, o_ref):
        o_ref[...] = (g_ref[...].astype(F32) + la_ref[...].astype(F32)).astype(dt)

    gs = pltpu.PrefetchScalarGridSpec(
        num_scalar_prefetch=1, grid=(S, nrt),
        in_specs=[pl.BlockSpec((None, tr, C), lambda s, r, jc: (s, jc[1] * nrt + r, 0)),
                  pl.BlockSpec((None, tr, C), lambda s, r, jc: (s, r, 0))],
        out_specs=pl.BlockSpec((None, tr, C), lambda s, r, jc: (s, r, 0)))
    return pl.pallas_call(body, name=name, out_shape=jax.ShapeDtypeStruct((S, half, C), dt), grid_spec=gs,
                          compiler_params=_cparams(("parallel", "parallel")))(jc_arr, g, la)


def _shard_sum(cp, lb, jc_arr, name, all_slots=False):
    S, h, C = cp.shape
    tr = _row_tile(h, C * 4, 16)

    def body(jc_ref, cp_ref, l0, l1, l2, o_ref):
        o_ref[...] = ((cp_ref[...].astype(F32) + l0[...].astype(F32)) + l1[...].astype(F32)) + l2[...].astype(F32)

    def lspec(kk):
        return pl.BlockSpec((None, tr, C), lambda r, jc: (kk, r, 0))

    if all_slots:
        out_spec = pl.BlockSpec((None, None, tr, C), lambda r, jc: (jc[0], jc[1], r, 0))
        out_shape = jax.ShapeDtypeStruct((S, 2, h, C), F32)
    else:
        out_spec = pl.BlockSpec((None, tr, C), lambda r, jc: (jc[1], r, 0))
        out_shape = jax.ShapeDtypeStruct((2, h, C), F32)
    gs = pltpu.PrefetchScalarGridSpec(
        num_scalar_prefetch=1, grid=(h // tr,),
        in_specs=[pl.BlockSpec((None, tr, C), lambda r, jc: (jc[0], r, 0)), lspec(0), lspec(1), lspec(2)],
        out_specs=out_spec)
    return pl.pallas_call(body, name=name, out_shape=out_shape, grid_spec=gs,
                          compiler_params=_cparams(("parallel",)))(jc_arr, cp, lb, lb, lb)


ANY = pl.BlockSpec(memory_space=pl.ANY)


def _place():
    x, y, c = lax.axis_index("x"), lax.axis_index("y"), lax.axis_index("c")
    chips = [(1 - x, y), (x, 1 - y), (1 - x, 1 - y)]
    return x, y, c, chips


class _Rider:
    def __init__(self, inputs, out_shape, aliases, sems, start, finish):
        self.inputs, self.out_shape, self.aliases, self.sems = list(inputs), list(out_shape), dict(aliases), list(sems)
        self.start, self.finish = start, finish


def _join_riders(r1, r2):
    i1, o1, s1 = len(r1.inputs), len(r1.out_shape), len(r1.sems)
    aliases = dict(r1.aliases)
    aliases.update({i1 + i: o1 + o for i, o in r2.aliases.items()})

    def start(ins, outs, sems):
        r1.start(ins[:i1], outs[:o1], sems[:s1])
        r2.start(ins[i1:], outs[o1:], sems[s1:])

    def finish(ins, outs, sems):
        r1.finish(ins[:i1], outs[:o1], sems[:s1])
        r2.finish(ins[i1:], outs[o1:], sems[s1:])

    return _Rider(r1.inputs + r2.inputs, r1.out_shape + r2.out_shape, aliases, r1.sems + r2.sems, start, finish)


def _after_rider(x):
    return _Rider([x], [], {}, [], lambda *a: None, lambda *a: None)


def _call(body, *, name, grid, in_specs, out_specs, out_shape, scratch_shapes, args, sem, rider=None, after=None):
    out_specs, out_shape = tuple(out_specs), tuple(out_shape)
    if after is not None:
        rider = _after_rider(after) if rider is None else _join_riders(_after_rider(after), rider)
    if rider is None:
        res = pl.pallas_call(body, name=name, out_shape=out_shape, grid=grid, in_specs=list(in_specs),
                             out_specs=out_specs, scratch_shapes=list(scratch_shapes),
                             compiler_params=_cparams(sem))(*args)
        return tuple(res), []
    n_in, n_out, n_sc = len(in_specs), len(out_specs), len(scratch_shapes)
    r_in, r_out = len(rider.inputs), len(rider.out_shape)

    def wrapped(*refs):
        p = 0
        host_in = refs[p:p + n_in]; p += n_in
        rid_in = refs[p:p + r_in]; p += r_in
        host_out = refs[p:p + n_out]; p += n_out
        rid_out = refs[p:p + r_out]; p += r_out
        host_sc = refs[p:p + n_sc]; p += n_sc
        rid_sem = refs[p:]
        first = functools.reduce(jnp.logical_and, [pl.program_id(a) == 0 for a in range(len(grid))])
        last = functools.reduce(jnp.logical_and, [pl.program_id(a) == grid[a] - 1 for a in range(len(grid))])

        @pl.when(first)
        def _():
            rider.start(rid_in, rid_out, rid_sem)

        body(*host_in, *host_out, *host_sc)

        @pl.when(last)
        def _():
            rider.finish(rid_in, rid_out, rid_sem)

    res = pl.pallas_call(
        wrapped, name=name, out_shape=out_shape + tuple(rider.out_shape), grid=grid,
        in_specs=list(in_specs) + [ANY] * r_in, out_specs=out_specs + (ANY,) * r_out,
        input_output_aliases={n_in + i: n_out + o for i, o in rider.aliases.items()},
        scratch_shapes=list(scratch_shapes) + rider.sems,
        compiler_params=_cparams(("arbitrary",) * len(grid)),
    )(*args, *rider.inputs)
    return tuple(res[:n_out]), list(res[n_out:])


def _run_rider(rider, name):
    def body(*refs):
        r_in, r_out = len(rider.inputs), len(rider.out_shape)
        ins, outs, sems = refs[:r_in], refs[r_in:r_in + r_out], refs[r_in + r_out:]
        rider.start(ins, outs, sems)
        rider.finish(ins, outs, sems)

    return pl.pallas_call(
        body, name=name, out_shape=rider.out_shape, in_specs=[ANY] * len(rider.inputs),
        out_specs=[ANY] * len(rider.out_shape), input_output_aliases=rider.aliases, scratch_shapes=rider.sems,
    )(*rider.inputs)


def _atoms(indices, kks=(0, 1, 2), q=0, nq=1):
    return [(i, kk, q, nq) for i in indices for kk in kks]


def _mm_shards(a, buf, order_arr, which, name, after, out=None):
    M, K = a.shape
    S, _, n = buf.shape
    tm = _pick(M, (512, 256, 128))
    s0 = which[0]

    def body(order_ref, a_ref, b_ref, *rest):
        rest[-1][...] = jnp.dot(a_ref[...], b_ref[...], preferred_element_type=F32)

    gs = pltpu.PrefetchScalarGridSpec(
        num_scalar_prefetch=1, grid=(len(which), M // tm),
        in_specs=[pl.BlockSpec((tm, K), lambda g, i, order: (i, 0)),
                  pl.BlockSpec((None, K, n), lambda g, i, order: (order[s0 + g], 0, 0)), ANY]
        + ([ANY] if out is not None else []),
        out_specs=pl.BlockSpec((tm, n), lambda g, i, order: (i, order[s0 + g])))
    return pl.pallas_call(
        body, name=name, grid_spec=gs, out_shape=jax.ShapeDtypeStruct((M, S * n), F32),
        input_output_aliases={4: 0} if out is not None else {},
        compiler_params=_cparams(("arbitrary", "arbitrary")),
    )(order_arr, a, buf, after, *([out] if out is not None else []))


def _all_gather_small(shards):
    n = len(shards)

    def body(*refs):
        w = refs[:n]
        out = refs[n:2 * n]
        local_sem, s_sem, r_sem = refs[2 * n:]
        x, y, c, chips = _place()
        j_me = 2 * x + y
        cps = []
        for i in range(n):
            lc = pltpu.make_async_copy(w[i], out[i].at[j_me], local_sem.at[i])
            lc.start()
            cps.append(lc)
        sends = []
        for i in range(n):
            for kk, (px, py) in enumerate(chips):
                cp = pltpu.make_async_remote_copy(
                    src_ref=w[i], dst_ref=out[i].at[j_me], send_sem=s_sem.at[3 * i + kk],
                    recv_sem=r_sem.at[3 * i + kk], device_id=(px, py, c), device_id_type=MESH)
                cp.start()
                sends.append(cp)
        for i in range(n):
            for kk, (px, py) in enumerate(chips):
                sends[3 * i + kk].wait_send()
                pltpu.make_async_remote_copy(
                    src_ref=w[i], dst_ref=out[i].at[2 * px + py], send_sem=s_sem.at[3 * i + kk],
                    recv_sem=r_sem.at[3 * i + kk], device_id=(px, py, c), device_id_type=MESH).wait_recv()
        for lc in cps:
            lc.wait()

    out_shape = [jax.ShapeDtypeStruct((N_SHARDS,) + s.shape, s.dtype) for s in shards]
    return pl.pallas_call(
        body, name="all_gather_conv_weights", out_shape=out_shape, in_specs=[ANY] * n, out_specs=[ANY] * n,
        scratch_shapes=[pltpu.SemaphoreType.DMA((n,)), pltpu.SemaphoreType.DMA((3 * n,)),
                        pltpu.SemaphoreType.DMA((3 * n,))],
    )(*shards)


def _pair_rider(grads):
    n = len(grads)

    def copies(g, la, sems):
        x, y, c, _ = _place()
        return [pltpu.make_async_remote_copy(
            src_ref=g[i].at[:, pl.ds((1 - c) * (g[i].shape[1] // 2), g[i].shape[1] // 2), :], dst_ref=la[i],
            send_sem=sems[0].at[i], recv_sem=sems[1].at[i], device_id=(x, y, 1 - c), device_id_type=MESH)
            for i in range(n)]

    def start(g, la, sems):
        for cp in copies(g, la, sems):
            cp.start()

    def finish(g, la, sems):
        for cp in copies(g, la, sems):
            cp.wait()

    return _Rider(grads, [jax.ShapeDtypeStruct((s.shape[0], s.shape[1] // 2, s.shape[2]), s.dtype) for s in grads],
                  {}, [pltpu.SemaphoreType.DMA((n,)), pltpu.SemaphoreType.DMA((n,))], start, finish)


def _shard_exchange_rider(cps_in, atoms=None):
    n = len(cps_in)
    if atoms is None:
        atoms = _atoms(range(n))

    def copies(ins, lb, sems):
        x, y, c, chips = _place()
        out = []
        for a, (i, kk, q, nq) in enumerate(atoms):
            h = ins[i].shape[1]
            assert h % (16 * nq) == 0, (h, nq)
            rows = pl.ds(q * (h // nq), h // nq)
            px, py = chips[kk]
            out.append(pltpu.make_async_remote_copy(
                src_ref=ins[i].at[2 * px + py, rows, :], dst_ref=lb[i].at[kk, rows, :],
                send_sem=sems[0].at[a], recv_sem=sems[1].at[a], device_id=(px, py, c), device_id_type=MESH))
        return out

    def start(ins, lb, sems):
        for cp in copies(ins, lb, sems):
            cp.start()

    def finish(ins, lb, sems):
        for cp in copies(ins, lb, sems):
            cp.wait()

    return _Rider(cps_in, [jax.ShapeDtypeStruct((3,) + s.shape[1:], s.dtype) for s in cps_in], {},
                  [pltpu.SemaphoreType.DMA((len(atoms),)), pltpu.SemaphoreType.DMA((len(atoms),))], start, finish)


HBM = pl.BlockSpec(memory_space=pltpu.HBM)
SEM = pl.BlockSpec(memory_space=pltpu.SEMAPHORE)


def _shard_copies(part_refs, land_refs, send_sems, recv_sems, relations):
    x, y, c, chips = _place()
    nr = len(relations)
    return [pltpu.make_async_remote_copy(
        src_ref=part_refs[i].at[2 * chips[kk][0] + chips[kk][1]], dst_ref=land_refs[i].at[kk],
        send_sem=send_sems.at[nr * i + r], recv_sem=recv_sems.at[nr * i + r],
        device_id=(chips[kk][0], chips[kk][1], c), device_id_type=MESH)
        for i in range(len(part_refs)) for r, kk in enumerate(relations)]


SIDE_EFFECT = pltpu.SideEffectType.DATAFLOW_SIDE_EFFECTING


def _shard_exchange_start(parts, name, relations=(0, 1, 2), lands=None):
    n = len(parts)
    ns = n * len(relations)

    def body(*refs):
        part_refs, land_refs = refs[:n], refs[n:2 * n]
        send_sems, recv_sems = refs[2 * n], refs[2 * n + 1]
        token = refs[4 * n + 2]
        for cp in _shard_copies(part_refs, land_refs, send_sems, recv_sems, relations):
            cp.start()
        token[...] = jnp.zeros_like(token)

    if lands is None:
        lands = [lax.empty((3,) + p.shape[1:], p.dtype) for p in parts]
    bufs = list(parts) + list(lands)
    res = pl.pallas_call(
        body, name=name,
        out_shape=(pltpu.SemaphoreType.DMA((ns,)), pltpu.SemaphoreType.DMA((ns,)),
                   *[pltpu.HBM(b.shape, b.dtype) for b in bufs], jax.ShapeDtypeStruct((8, LANES), F32)),
        in_specs=(HBM,) * (2 * n), out_specs=(SEM, SEM) + (HBM,) * (2 * n) + (pl.BlockSpec(memory_space=pltpu.VMEM),),
        input_output_aliases={i: 2 + i for i in range(2 * n)},
        compiler_params=pltpu.CompilerParams(has_side_effects=SIDE_EFFECT),
    )(*[pltpu.with_memory_space_constraint(b, pltpu.HBM) for b in bufs])
    return res[0], res[1], list(res[2:2 + n]), list(res[2 + n:2 + 2 * n]), res[2 + 2 * n], relations


def _shard_exchange_wait(started, after, name):
    send_sems, recv_sems, parts, lands, _, relations = started
    n = len(parts)

    def body(*refs):
        part_refs, land_refs = refs[:n], refs[n:2 * n]
        send_sems_ref, recv_sems_ref = refs[2 * n], refs[2 * n + 1]
        for cp in _shard_copies(part_refs, land_refs, send_sems_ref, recv_sems_ref, relations):
            cp.wait_send()
            cp.wait_recv()

    bufs = parts + lands
    res = pl.pallas_call(
        body, name=name, out_shape=tuple(pltpu.HBM(b.shape, b.dtype) for b in bufs),
        in_specs=(HBM,) * (2 * n) + (SEM, SEM, ANY), out_specs=(HBM,) * (2 * n),
        input_output_aliases={i: i for i in range(2 * n)},
        compiler_params=pltpu.CompilerParams(has_side_effects=SIDE_EFFECT),
    )(*bufs, send_sems, recv_sems, after)
    return list(res[:n]), list(res[n:])


def _gather_copies(buf_ref, send_sems, recv_sems, over_d2d, arriving, relations):
    x, y, c, chips = _place()
    half = buf_ref.shape[1] // 2
    out = []
    for r, kk in enumerate(relations):
        px, py = chips[kk]
        if over_d2d:
            slot, core, peer = 2 * px + py, (1 - c) if arriving else c, (x, y, 1 - c)
        else:
            slot, core, peer = (2 * px + py) if arriving else (2 * x + y), c, (px, py, c)
        blk = buf_ref.at[slot, pl.ds(core * half, half), :]
        out.append(pltpu.make_async_remote_copy(src_ref=blk, dst_ref=blk, send_sem=send_sems.at[r],
                                                recv_sem=recv_sems.at[r], device_id=peer, device_id_type=MESH))
    return out


def _gather_step(buf, after, name, sems_in=None, start_d2d=None, relations=(0, 1, 2)):
    n_sem = 0 if sems_in is None else 2

    def body(*refs):
        buf_ref = refs[0]
        ins = refs[1:1 + n_sem]
        outs = refs[2 + n_sem:]
        if sems_in is not None:
            waited_d2d = start_d2d is None
            for mine, theirs in zip(_gather_copies(buf_ref, ins[0], ins[1], waited_d2d, False, relations),
                                    _gather_copies(buf_ref, ins[0], ins[1], waited_d2d, True, relations)):
                theirs.wait_recv()
                mine.wait_send()
        if start_d2d is not None:
            for cp in _gather_copies(buf_ref, outs[0], outs[1], start_d2d, False, relations):
                cp.start()
            outs[3][...] = jnp.zeros_like(outs[3])

    nr = len(relations)
    sem_out = () if start_d2d is None else (pltpu.SemaphoreType.DMA((nr,)), pltpu.SemaphoreType.DMA((nr,)))
    tok_out = () if start_d2d is None else (jax.ShapeDtypeStruct((8, LANES), F32),)
    res = pl.pallas_call(
        body, name=name,
        out_shape=sem_out + (pltpu.HBM(buf.shape, buf.dtype),) + tok_out,
        in_specs=(HBM,) + (SEM,) * n_sem + (ANY,),
        out_specs=(SEM,) * len(sem_out) + (HBM,) + (pl.BlockSpec(memory_space=pltpu.VMEM),) * len(tok_out),
        input_output_aliases={0: len(sem_out)},
        compiler_params=pltpu.CompilerParams(has_side_effects=SIDE_EFFECT),
    )(pltpu.with_memory_space_constraint(buf, pltpu.HBM), *(sems_in or ()), after)
    if start_d2d is None:
        return res[0]
    return (res[0], res[1]), res[2], res[3]


def _relay_copies(buf_ref, send_sems, recv_sems, arriving):
    x, y, c, chips = _place()
    quarter = buf_ref.shape[1] // 4
    out = []
    for r, (src_kk, dst_kk) in enumerate(((0, 1), (1, 0))):
        slot = (2 * chips[2][0] + chips[2][1]) if arriving else (2 * chips[src_kk][0] + chips[src_kk][1])
        blk = buf_ref.at[slot, pl.ds((2 * c + r) * quarter, quarter), :]
        out.append(pltpu.make_async_remote_copy(
            src_ref=blk, dst_ref=blk, send_sem=send_sems.at[r], recv_sem=recv_sems.at[r],
            device_id=(chips[dst_kk][0], chips[dst_kk][1], c), device_id_type=MESH))
    return out


def _gather_relay_step(buf, after, name, sems_in, relay_in=None):
    first = relay_in is None
    ins_sems = sems_in if first else relay_in
    near, diag = (0, 1), (2,)

    def body(*refs):
        buf_ref, in_s, in_r = refs[0], refs[1], refs[2]
        outs = refs[4:]
        if first:
            for mine, theirs in zip(_gather_copies(buf_ref, in_s, in_r, False, False, near),
                                    _gather_copies(buf_ref, in_s, in_r, False, True, near)):
                theirs.wait_recv()
                mine.wait_send()
            for cp in _gather_copies(buf_ref, outs[0], outs[1], True, False, near):
                cp.start()
            for cp in _relay_copies(buf_ref, outs[2], outs[3], False):
                cp.start()
        else:
            for mine, theirs in zip(_relay_copies(buf_ref, in_s, in_r, False), _relay_copies(buf_ref, in_s, in_r, True)):
                theirs.wait_recv()
                mine.wait_send()
            for cp in _gather_copies(buf_ref, outs[0], outs[1], True, False, diag):
                cp.start()
        outs[-1][...] = jnp.zeros_like(outs[-1])

    def sem(n):
        return pltpu.SemaphoreType.DMA((n,))

    sem_out = (sem(2), sem(2), sem(2), sem(2)) if first else (sem(1), sem(1))
    res = pl.pallas_call(
        body, name=name,
        out_shape=sem_out + (pltpu.HBM(buf.shape, buf.dtype), jax.ShapeDtypeStruct((8, LANES), F32)),
        in_specs=(HBM, SEM, SEM, ANY),
        out_specs=(SEM,) * len(sem_out) + (HBM, pl.BlockSpec(memory_space=pltpu.VMEM)),
        input_output_aliases={0: len(sem_out)},
        compiler_params=pltpu.CompilerParams(has_side_effects=SIDE_EFFECT),
    )(pltpu.with_memory_space_constraint(buf, pltpu.HBM), *ins_sems, after)
    if first:
        return (res[0], res[1]), (res[2], res[3]), res[4], res[5]
    return (res[0], res[1]), res[2], res[3]


def _half_share_copy(buf_ref, send_sem, recv_sem, arriving):
    x, y, c = lax.axis_index("x"), lax.axis_index("y"), lax.axis_index("c")
    blk = buf_ref.at[(1 - c) if arriving else c]
    return pltpu.make_async_remote_copy(src_ref=blk, dst_ref=blk, send_sem=send_sem, recv_sem=recv_sem,
                                        device_id=(x, y, 1 - c), device_id_type=MESH)


def _half_share_start(buf, name):
    def body(buf_ref, send_sem, recv_sem, buf_thru, token):
        _half_share_copy(buf_ref, send_sem, recv_sem, False).start()
        token[...] = jnp.zeros_like(token)

    return pl.pallas_call(
        body, name=name,
        out_shape=(pltpu.SemaphoreType.DMA(()), pltpu.SemaphoreType.DMA(()), pltpu.HBM(buf.shape, buf.dtype),
                   jax.ShapeDtypeStruct((8, LANES), F32)),
        in_specs=(HBM,), out_specs=(SEM, SEM, HBM, pl.BlockSpec(memory_space=pltpu.VMEM)),
        input_output_aliases={0: 2}, compiler_params=pltpu.CompilerParams(has_side_effects=SIDE_EFFECT),
    )(pltpu.with_memory_space_constraint(buf, pltpu.HBM))


def _half_share_wait(started, after, name):
    send_sem, recv_sem, buf, _ = started

    def body(buf_ref, send_sem_ref, recv_sem_ref, after_ref, buf_out):
        _half_share_copy(buf_ref, send_sem_ref, recv_sem_ref, False).wait_send()
        _half_share_copy(buf_ref, send_sem_ref, recv_sem_ref, True).wait_recv()

    return pl.pallas_call(
        body, name=name, out_shape=pltpu.HBM(buf.shape, buf.dtype),
        in_specs=(HBM, SEM, SEM, ANY), out_specs=HBM, input_output_aliases={0: 0},
        compiler_params=pltpu.CompilerParams(has_side_effects=SIDE_EFFECT),
    )(buf, send_sem, recv_sem, after)


def _share_rider(halves, eighths=None):
    n = len(halves)
    bufs = list(halves) + ([eighths] if eighths is not None else [])

    def half_copy(out, sems, i, core):
        x, y, c, _ = _place()
        blk = out[i].at[core]
        return pltpu.make_async_remote_copy(src_ref=blk, dst_ref=blk, send_sem=sems[0].at[i], recv_sem=sems[1].at[i],
                                            device_id=(x, y, 1 - c), device_id_type=MESH)

    def eighth_copy(out, sems, r, mine):
        x, y, c, _ = _place()
        px, py, pc = x ^ ((r >> 2) & 1), y ^ ((r >> 1) & 1), c ^ (r & 1)
        blk = out[n].at[2 * x + y, c] if mine else out[n].at[2 * px + py, pc]
        return pltpu.make_async_remote_copy(src_ref=blk, dst_ref=blk, send_sem=sems[2].at[r - 1],
                                            recv_sem=sems[3].at[r - 1], device_id=(px, py, pc), device_id_type=MESH)

    def start(ins, out, sems):
        c = lax.axis_index("c")
        for i in range(n):
            half_copy(out, sems, i, c).start()
        if eighths is not None:
            for r in range(1, N_DEV):
                eighth_copy(out, sems, r, True).start()

    def finish(ins, out, sems):
        c = lax.axis_index("c")
        for i in range(n):
            half_copy(out, sems, i, 1 - c).wait_recv()
        if eighths is not None:
            for r in range(1, N_DEV):
                eighth_copy(out, sems, r, False).wait_recv()
        for i in range(n):
            half_copy(out, sems, i, c).wait_send()
        if eighths is not None:
            for r in range(1, N_DEV):
                eighth_copy(out, sems, r, True).wait_send()

    return _Rider(bufs, [jax.ShapeDtypeStruct(s.shape, s.dtype) for s in bufs], {i: i for i in range(len(bufs))},
                  [pltpu.SemaphoreType.DMA((max(n, 1),)), pltpu.SemaphoreType.DMA((max(n, 1),)),
                   pltpu.SemaphoreType.DMA((N_DEV - 1,)), pltpu.SemaphoreType.DMA((N_DEV - 1,))], start, finish)


ATTN_ROWS = GROUP * ATTN_BLOCK
ATTN_KEYS = 2 * ATTN_BLOCK


def _attn_geometry(n):
    row = lax.broadcasted_iota(jnp.int32, (ATTN_ROWS, ATTN_KEYS), 0)
    col = lax.broadcasted_iota(jnp.int32, (ATTN_ROWS, ATTN_KEYS), 1)
    dist = ATTN_BLOCK + jnp.bitwise_and(row, ATTN_BLOCK - 1) - col
    valid = jnp.logical_and(jnp.logical_and(dist >= 0, dist < ATTN_BLOCK),
                            jnp.logical_or(col >= ATTN_BLOCK, n > 0))
    return dist.astype(F32), valid


def _per_head_column(values):
    head = lax.broadcasted_iota(jnp.int32, (ATTN_ROWS, 1), 0) // ATTN_BLOCK
    col = jnp.zeros((ATTN_ROWS, 1), F32)
    for hh, v in enumerate(values):
        col = jnp.where(head == hh, v, col)
    return col


def _stack_heads(ref, g):
    return jnp.concatenate(
        [ref[:, (g * GROUP + hh) * HEAD_DIM:(g * GROUP + hh + 1) * HEAD_DIM].astype(BF16) for hh in range(GROUP)],
        axis=0)


def _attn_probs(q_s, k2, slope_col, sink_col, dist, valid):
    s = lax.dot_general(q_s, k2, (((1,), (1,)), ((), ())), preferred_element_type=F32) * (HEAD_DIM ** -0.5)
    s = jnp.where(valid, s - slope_col * dist, NEG)
    m = jnp.maximum(jnp.max(s, axis=1, keepdims=True), sink_col)
    e = jnp.exp(s - m)
    es = jnp.exp(sink_col - m)
    inv = 1.0 / (jnp.sum(e, axis=1, keepdims=True) + es)
    return e * inv, es * inv


def _attn_specs(T, d_attn, d_kv, q_blk, k_blk, v_blk):
    bq = pl.BlockSpec((ATTN_BLOCK, d_attn), lambda n: (n, q_blk))
    kp = pl.BlockSpec((ATTN_BLOCK, d_kv), lambda n: (jnp.maximum(n - 1, 0), k_blk))
    kc = pl.BlockSpec((ATTN_BLOCK, d_kv), lambda n: (n, k_blk))
    vp = pl.BlockSpec((ATTN_BLOCK, d_kv), lambda n: (jnp.maximum(n - 1, 0), v_blk))
    vc = pl.BlockSpec((ATTN_BLOCK, d_kv), lambda n: (n, v_blk))
    return bq, kp, kc, vp, vc


def _attn_fwd(proj, sinks, nq, cols, after=None):
    T = proj.shape[0]
    nkv = nq // GROUP
    d_attn, d_kv = nq * HEAD_DIM, nkv * HEAD_DIM
    q_off, k_off, v_off = cols
    bq, kp, kc, vp, vc = _attn_specs(T, d_attn, d_kv, q_off // d_attn, k_off // d_kv, v_off // d_kv)

    def body(sink_ref, q_ref, kp_ref, kc_ref, vp_ref, vc_ref, o_ref):
        n = pl.program_id(0)
        dist, valid = _attn_geometry(n)
        for g in range(nkv):
            ks = slice(g * HEAD_DIM, (g + 1) * HEAD_DIM)
            k2 = jnp.concatenate([kp_ref[:, ks], kc_ref[:, ks]], axis=0).astype(BF16)
            v2 = jnp.concatenate([vp_ref[:, ks], vc_ref[:, ks]], axis=0).astype(BF16)
            slope_col = _per_head_column([2.0 ** (-8.0 * (g * GROUP + hh + 1) / nq) for hh in range(GROUP)])
            sink_col = _per_head_column([sink_ref[0, g * GROUP + hh] for hh in range(GROUP)])
            p, _ = _attn_probs(_stack_heads(q_ref, g), k2, slope_col, sink_col, dist, valid)
            o = jnp.dot(p.astype(BF16), v2, preferred_element_type=F32).astype(BF16)
            for hh in range(GROUP):
                h = g * GROUP + hh
                o_ref[:, h * HEAD_DIM:(h + 1) * HEAD_DIM] = o[hh * ATTN_BLOCK:(hh + 1) * ATTN_BLOCK, :]

    (out,), carried = _call(
        body, name="attn_fwd", out_shape=[jax.ShapeDtypeStruct((T, d_attn), BF16)], grid=(T // ATTN_BLOCK,),
        in_specs=[pl.BlockSpec(memory_space=pltpu.SMEM), bq, kp, kc, vp, vc],
        out_specs=[pl.BlockSpec((ATTN_BLOCK, d_attn), lambda n: (n, 0))], scratch_shapes=[],
        args=(sinks, proj, proj, proj, proj, proj), sem=("parallel",), after=after)
    return out


def _attn_bwd(proj, d_attn_out, sinks, nq, cols, after=None):
    T = proj.shape[0]
    nkv = nq // GROUP
    d_attn, d_kv = nq * HEAD_DIM, nkv * HEAD_DIM
    q_off, k_off, v_off = cols
    bq, kp, kc, vp, vc = _attn_specs(T, d_attn, d_kv, q_off // d_attn, k_off // d_kv, v_off // d_kv)
    scale = HEAD_DIM ** -0.5
    dn_t = (((1,), (1,)), ((), ()))
    dn_r = (((0,), (0,)), ((), ()))

    def body(sink_ref, q_ref, kp_ref, kc_ref, vp_ref, vc_ref, do_ref, dq_ref, dk_ref, dv_ref, ds_ref):
        n = pl.program_id(0)

        @pl.when(n == 0)
        def _():
            dk_ref[...] = jnp.zeros_like(dk_ref)
            dv_ref[...] = jnp.zeros_like(dv_ref)
            ds_ref[...] = jnp.zeros_like(ds_ref)

        dist, valid = _attn_geometry(n)
        rows_c = pl.ds(pl.multiple_of(n * ATTN_BLOCK, ATTN_BLOCK), ATTN_BLOCK)
        rows_p = pl.ds(pl.multiple_of(jnp.maximum(n - 1, 0) * ATTN_BLOCK, ATTN_BLOCK), ATTN_BLOCK)
        lane = lax.broadcasted_iota(jnp.int32, ds_ref.shape, 1)
        srow = lax.broadcasted_iota(jnp.int32, ds_ref.shape, 0)
        ds_acc = jnp.zeros(ds_ref.shape, F32)
        for g in range(nkv):
            ks = slice(g * HEAD_DIM, (g + 1) * HEAD_DIM)
            k2 = jnp.concatenate([kp_ref[:, ks], kc_ref[:, ks]], axis=0).astype(BF16)
            v2 = jnp.concatenate([vp_ref[:, ks], vc_ref[:, ks]], axis=0).astype(BF16)
            slope_col = _per_head_column([2.0 ** (-8.0 * (g * GROUP + hh + 1) / nq) for hh in range(GROUP)])
            sink_col = _per_head_column([sink_ref[0, g * GROUP + hh] for hh in range(GROUP)])
            q_s = _stack_heads(q_ref, g)
            do_s = _stack_heads(do_ref, g)
            p, p_sink = _attn_probs(q_s, k2, slope_col, sink_col, dist, valid)
            dp = lax.dot_general(do_s, v2, dn_t, preferred_element_type=F32)
            delta = jnp.sum(p * dp, axis=1, keepdims=True)
            ds = (p * (dp - delta)).astype(BF16)
            sink_part = p_sink * delta
            dq = (jnp.dot(ds, k2, preferred_element_type=F32) * scale).astype(BF16)
            for hh in range(GROUP):
                h = g * GROUP + hh
                blk = slice(hh * ATTN_BLOCK, (hh + 1) * ATTN_BLOCK)
                dq_ref[:, h * HEAD_DIM:(h + 1) * HEAD_DIM] = dq[blk, :]
                ds_acc = ds_acc + jnp.where(jnp.logical_and(lane == h, srow == 0), -jnp.sum(sink_part[blk, :]), 0.0)
            dk2 = lax.dot_general(ds, q_s, dn_r, preferred_element_type=F32) * scale
            dv2 = lax.dot_general(p.astype(BF16), do_s, dn_r, preferred_element_type=F32)
            dk_ref[rows_p, ks] += dk2[:ATTN_BLOCK, :]
            dv_ref[rows_p, ks] += dv2[:ATTN_BLOCK, :]
            dk_ref[rows_c, ks] += dk2[ATTN_BLOCK:, :]
            dv_ref[rows_c, ks] += dv2[ATTN_BLOCK:, :]
        ds_ref[...] += ds_acc

    out_shape = (jax.ShapeDtypeStruct((T, d_attn), BF16), jax.ShapeDtypeStruct((T, d_kv), F32),
                 jax.ShapeDtypeStruct((T, d_kv), F32), jax.ShapeDtypeStruct((8, LANES), F32))
    return _call(
        body, name="attn_bwd", out_shape=out_shape, grid=(T // ATTN_BLOCK,),
        in_specs=[pl.BlockSpec(memory_space=pltpu.SMEM), bq, kp, kc, vp, vc,
                  pl.BlockSpec((ATTN_BLOCK, d_attn), lambda n: (n, 0))],
        out_specs=(pl.BlockSpec((ATTN_BLOCK, d_attn), lambda n: (n, 0)),
                   pl.BlockSpec((T, d_kv), lambda n: (0, 0)), pl.BlockSpec((T, d_kv), lambda n: (0, 0)),
                   pl.BlockSpec((8, LANES), lambda n: (0, 0))),
        scratch_shapes=[], args=(sinks, proj, proj, proj, proj, proj, d_attn_out), sem=("arbitrary",), after=after)[0]


def _rnn_tile(T):
    return _pick(T, (256, 128))


def _rnn_gates(x_ext, cw_ref, cb_ref, wa_ref, wi_ref, ba_ref, bi_ref, lam_ref, tt):
    xs = [pltpu.roll(x_ext, 3 - k, 0)[8:, :] if k < 3 else x_ext[8:, :] for k in range(4)]
    cx = cb_ref[...] + xs[0] * cw_ref[0:1, :]
    for k in range(1, 4):
        cx = cx + xs[k] * cw_ref[k:k + 1, :]
    cxb = cx.astype(BF16)
    r = jax.nn.sigmoid(jnp.dot(cxb, wa_ref[...], preferred_element_type=F32) + ba_ref[...])
    i = jax.nn.sigmoid(jnp.dot(cxb, wi_ref[...], preferred_element_type=F32) + bi_ref[...])
    lam = lam_ref[...]
    sp = jnp.maximum(-lam, 0.0) + jnp.log1p(jnp.exp(-jnp.abs(lam)))
    log_a = -LRU_C * r * sp
    a = jnp.exp(log_a)
    z = 2.0 * log_a
    em1 = jnp.where(z > -1e-2, z * (1.0 + z * (0.5 + z * (1.0 / 6.0 + z * (1.0 / 24.0)))), jnp.exp(z) - 1.0)
    s = jnp.sqrt(-em1)
    return xs, cx, r, i, sp, a, s


def _rnn_specs(T, gw, tt, rx_blk, ry_blk, rev):
    nT = T // tt
    hb = tt // 8

    def tile(t):
        return (nT - 1 - t) if rev else t

    rx = pl.BlockSpec((tt, gw), lambda g, t: (tile(t), rx_blk + g))
    rx_halo = pl.BlockSpec((8, gw), lambda g, t: (jnp.maximum(tile(t) * hb - 1, 0), rx_blk + g))
    ry = pl.BlockSpec((tt, gw), lambda g, t: (tile(t), ry_blk + g))
    cw = pl.BlockSpec((4, gw), lambda g, t: (0, g))
    vec = pl.BlockSpec((1, gw), lambda g, t: (0, g))
    wg = pl.BlockSpec((None, gw, gw), lambda g, t: (g, 0, 0))
    act = pl.BlockSpec((tt, gw), lambda g, t: (tile(t), g))
    act_halo = pl.BlockSpec((8, gw), lambda g, t: (jnp.maximum(tile(t) * hb - 1, 0), g))
    return rx, rx_halo, ry, cw, vec, wg, act, act_halo, tile


def _rnn_fwd(proj, cols, conv_w, conv_b, wa_g, wi_g, ba, bi, lam, rider=None):
    T = proj.shape[0]
    G, gw, _ = wa_g.shape
    d_rnn = G * gw
    tt = _rnn_tile(T)
    rx_off, ry_off = cols
    rx, rx_halo, ry, cw, vec, wg, act, _, _ = _rnn_specs(T, gw, tt, rx_off // gw, ry_off // gw, False)

    def body(rx_ref, rxh_ref, ry_ref, cw_ref, cb_ref, wa_ref, wi_ref, ba_ref, bi_ref, lam_ref,
             b_ref, h_ref, carry):
        t = pl.program_id(1)

        @pl.when(t == 0)
        def _():
            carry[...] = jnp.zeros_like(carry)

        halo = jnp.where(t > 0, rxh_ref[...], 0.0)
        x_ext = jnp.concatenate([halo, rx_ref[...]], axis=0)
        _, cx, _, i, _, a, s = _rnn_gates(x_ext, cw_ref, cb_ref, wa_ref, wi_ref, ba_ref, bi_ref, lam_ref, tt)
        acc_a, acc_b = a, s * (i * cx)
        d = 1
        while d < tt:
            acc_b = acc_a * _shift_down(acc_b, d, 0.0) + acc_b
            acc_a = acc_a * _shift_down(acc_a, d, 1.0)
            d *= 2
        h = acc_b + acc_a * carry[7:8, :]
        carry[...] = h[tt - 8:, :]
        h_ref[...] = h
        b_ref[...] = (h * _gelu(ry_ref[...])).astype(BF16)

    return _call(
        body, name="rnn_fwd",
        out_shape=(jax.ShapeDtypeStruct((T, d_rnn), BF16), jax.ShapeDtypeStruct((T, d_rnn), F32)),
        grid=(G, T // tt),
        in_specs=[rx, rx_halo, ry, cw, vec, wg, wg, vec, vec, vec], out_specs=(act, act),
        scratch_shapes=[pltpu.VMEM((8, gw), F32)],
        args=(proj, proj, proj, conv_w, conv_b, wa_g, wi_g, ba, bi, lam), sem=("parallel", "arbitrary"), rider=rider)


def _rnn_bwd(proj, cols, h_all, d_b, conv_w, conv_b, wa_g, wi_g, ba, bi, lam, rider=None):
    T = proj.shape[0]
    G, gw, _ = wa_g.shape
    d_rnn = G * gw
    tt = _rnn_tile(T)
    nT = T // tt
    rx_off, ry_off = cols
    rx, rx_halo, ry, cw, vec, wg, act, act_halo, _ = _rnn_specs(T, gw, tt, rx_off // gw, ry_off // gw, True)
    dn_t = (((1,), (1,)), ((), ()))
    dn_r = (((0,), (0,)), ((), ()))

    def body(rx_ref, rxh_ref, ry_ref, h_ref, hh_ref, db_ref, cw_ref, cb_ref, wa_ref, wi_ref, ba_ref, bi_ref, lam_ref,
             drx_ref, dry_ref, dcw_ref, dcb_ref, dba_ref, dbi_ref, dlam_ref, dwa_ref, dwi_ref,
             lam_carry, dcx_carry):
        t = pl.program_id(1)
        first_tile = t == nT - 1

        @pl.when(t == 0)
        def _():
            lam_carry[...] = jnp.zeros_like(lam_carry)
            dcx_carry[...] = jnp.zeros_like(dcx_carry)
            dcw_ref[...] = jnp.zeros_like(dcw_ref)
            dcb_ref[...] = jnp.zeros_like(dcb_ref)
            dba_ref[...] = jnp.zeros_like(dba_ref)
            dbi_ref[...] = jnp.zeros_like(dbi_ref)
            dlam_ref[...] = jnp.zeros_like(dlam_ref)
            dwa_ref[...] = jnp.zeros_like(dwa_ref)
            dwi_ref[...] = jnp.zeros_like(dwi_ref)

        halo = jnp.where(first_tile, 0.0, rxh_ref[...])
        x_ext = jnp.concatenate([halo, rx_ref[...]], axis=0)
        xs, cx, r, i, sp, a, s = _rnn_gates(x_ext, cw_ref, cb_ref, wa_ref, wi_ref, ba_ref, bi_ref, lam_ref, tt)
        h = h_ref[...]
        h_halo = jnp.where(first_tile, 0.0, hh_ref[...])
        h_prev = pltpu.roll(jnp.concatenate([h_halo, h], axis=0), 1, 0)[8:, :]
        gel, dgel = _gelu_and_grad(ry_ref[...])
        d_b_t = db_ref[...]
        dry_ref[...] = (d_b_t * h * dgel).astype(BF16)
        dh = d_b_t * gel

        acc_c = _shift_up(a, 1, 1.0)
        acc_l = dh
        d = 1
        while d < tt:
            acc_l = acc_c * _shift_up(acc_l, d, 0.0) + acc_l
            acc_c = acc_c * _shift_up(acc_c, d, 1.0)
            d *= 2
        lam_t = acc_l + acc_c * lam_carry[0:1, :]
        lam_carry[...] = (a * lam_t)[0:8, :]

        icx = i * cx
        d_s = lam_t * icx
        d_i = lam_t * s * cx
        dcx = lam_t * s * i
        d_a = lam_t * h_prev - d_s * (a / s)
        dlog_a = d_a * a
        d_r = dlog_a * (-LRU_C * sp)
        lam = lam_ref[...]
        dlam_ref[...] += jnp.sum(dlog_a * r, axis=0, keepdims=True) * (LRU_C * jax.nn.sigmoid(-lam))
        dpr = d_r * r * (1.0 - r)
        dpi = d_i * i * (1.0 - i)
        dba_ref[...] += jnp.sum(dpr, axis=0, keepdims=True)
        dbi_ref[...] += jnp.sum(dpi, axis=0, keepdims=True)
        cxb = cx.astype(BF16)
        dprb, dpib = dpr.astype(BF16), dpi.astype(BF16)
        dwa_ref[...] += lax.dot_general(cxb, dprb, dn_r, preferred_element_type=F32)
        dwi_ref[...] += lax.dot_general(cxb, dpib, dn_r, preferred_element_type=F32)
        dcx = (dcx + lax.dot_general(dprb, wa_ref[...], dn_t, preferred_element_type=F32)
               + lax.dot_general(dpib, wi_ref[...], dn_t, preferred_element_type=F32))

        dcb_ref[...] += jnp.sum(dcx, axis=0, keepdims=True)
        for k in range(4):
            dcw_ref[k:k + 1, :] += jnp.sum(dcx * xs[k], axis=0, keepdims=True)
        d_ext = jnp.concatenate([dcx, dcx_carry[...]], axis=0)
        drx = dcx * cw_ref[3:4, :]
        for k in range(3):
            drx = drx + pltpu.roll(d_ext, tt + 8 - (3 - k), 0)[:tt, :] * cw_ref[k:k + 1, :]
        drx_ref[...] = drx.astype(BF16)
        dcx_carry[...] = dcx[0:8, :]

    out_shape = (jax.ShapeDtypeStruct((T, d_rnn), BF16), jax.ShapeDtypeStruct((T, d_rnn), BF16),
                 jax.ShapeDtypeStruct((4, d_rnn), F32), jax.ShapeDtypeStruct((1, d_rnn), F32),
                 jax.ShapeDtypeStruct((1, d_rnn), F32), jax.ShapeDtypeStruct((1, d_rnn), F32),
                 jax.ShapeDtypeStruct((1, d_rnn), F32), jax.ShapeDtypeStruct((G, gw, gw), F32),
                 jax.ShapeDtypeStruct((G, gw, gw), F32))
    return _call(
        body, name="rnn_bwd", out_shape=out_shape, grid=(G, nT),
        in_specs=[rx, rx_halo, ry, act, act_halo, act, cw, vec, wg, wg, vec, vec, vec],
        out_specs=(act, act, cw, vec, vec, vec, vec, wg, wg),
        scratch_shapes=[pltpu.VMEM((8, gw), F32), pltpu.VMEM((8, gw), F32)],
        args=(proj, proj, proj, h_all, h_all, d_b, conv_w, conv_b, wa_g, wi_g, ba, bi, lam),
        sem=("parallel", "arbitrary"), rider=rider)


def _merge_fwd(proj, gl_off, b_gate, y_attn, y_rnn, rider=None):
    T, D = y_attn.shape
    tm = _pick(T, (256, 128))
    ct = _pick(math.gcd(gl_off, D), (512, 256, 128))
    oa, orr, nd = gl_off // ct, (gl_off + D) // ct, D // ct

    def body(ga_ref, gr_ref, ba_ref, br_ref, ya_ref, yr_ref, m_ref):
        ga = jax.nn.sigmoid(ga_ref[...] + ba_ref[...])
        gr = jax.nn.sigmoid(gr_ref[...] + br_ref[...])
        m_ref[...] = (ga * ya_ref[...] + gr * yr_ref[...]).astype(BF16)

    blk = pl.BlockSpec((tm, ct), lambda i, j: (i, j))
    (merged,), carried = _call(
        body, name="merge_fwd", out_shape=[jax.ShapeDtypeStruct((T, D), BF16)], grid=(T // tm, nd),
        in_specs=[pl.BlockSpec((tm, ct), lambda i, j: (i, oa + j)), pl.BlockSpec((tm, ct), lambda i, j: (i, orr + j)),
                  pl.BlockSpec((1, ct), lambda i, j: (0, j)), pl.BlockSpec((1, ct), lambda i, j: (0, nd + j)),
                  blk, blk],
        out_specs=[blk], scratch_shapes=[], args=(proj, proj, b_gate, b_gate, y_attn, y_rnn),
        sem=("parallel", "parallel"), rider=rider)
    return merged, carried


def _merge_bwd(proj, gl_off, b_gate, y_attn, y_rnn, d_m):
    T, D = y_attn.shape
    tm = _pick(T, (256, 128))
    ct = _pick(math.gcd(gl_off, D), (512, 256, 128))
    oa, orr, nd = gl_off // ct, (gl_off + D) // ct, D // ct

    def body(ga_ref, gr_ref, ba_ref, br_ref, ya_ref, yr_ref, dm_ref,
             dya_ref, dyr_ref, dga_ref, dgr_ref, dba_ref, dbr_ref):
        i = pl.program_id(1)

        @pl.when(i == 0)
        def _():
            dba_ref[...] = jnp.zeros_like(dba_ref)
            dbr_ref[...] = jnp.zeros_like(dbr_ref)

        ga = jax.nn.sigmoid(ga_ref[...] + ba_ref[...])
        gr = jax.nn.sigmoid(gr_ref[...] + br_ref[...])
        dm = dm_ref[...]
        dya_ref[...] = (dm * ga).astype(BF16)
        dyr_ref[...] = (dm * gr).astype(BF16)
        dga = dm * ya_ref[...] * ga * (1.0 - ga)
        dgr = dm * yr_ref[...] * gr * (1.0 - gr)
        dga_ref[...] = dga.astype(BF16)
        dgr_ref[...] = dgr.astype(BF16)
        dba_ref[...] += jnp.sum(dga, axis=0, keepdims=True)
        dbr_ref[...] += jnp.sum(dgr, axis=0, keepdims=True)

    blk = pl.BlockSpec((tm, ct), lambda j, i: (i, j))
    vec = pl.BlockSpec((1, ct), lambda j, i: (0, j))
    act = jax.ShapeDtypeStruct((T, D), BF16)
    v1 = jax.ShapeDtypeStruct((1, D), F32)
    return pl.pallas_call(
        body, name="merge_bwd", out_shape=(act, act, act, act, v1, v1), grid=(nd, T // tm),
        in_specs=[pl.BlockSpec((tm, ct), lambda j, i: (i, oa + j)), pl.BlockSpec((tm, ct), lambda j, i: (i, orr + j)),
                  vec, pl.BlockSpec((1, ct), lambda j, i: (0, nd + j)), blk, blk, blk],
        out_specs=(blk, blk, blk, blk, vec, vec),
        compiler_params=_cparams(("parallel", "arbitrary")),
    )(proj, proj, b_gate, b_gate, y_attn, y_rnn, d_m)


def _ln_fwd(x_res, delta, g, b, name, rider=None):
    T, D = x_res.shape
    tm = _pick(T, (256, 128))

    def body(x_ref, d_ref, g_ref, b_ref, y_ref, yb_ref, xh_ref, rs_ref):
        z = ALPHA * x_ref[...] + d_ref[...]
        mu = jnp.mean(z, axis=1, keepdims=True)
        zc = z - mu
        var = jnp.mean(zc * zc, axis=1, keepdims=True)
        rstd = lax.rsqrt(var + LN_EPS)
        xh = zc * rstd
        xh_ref[...] = xh
        rs_ref[...] = rstd
        y = xh * g_ref[...] + b_ref[...]
        y_ref[...] = y
        yb_ref[...] = y.astype(BF16)

    row = pl.BlockSpec((tm, D), lambda i: (i, 0))
    vec = pl.BlockSpec((1, D), lambda i: (0, 0))
    return _call(
        body, name=name,
        out_shape=(jax.ShapeDtypeStruct((T, D), F32), jax.ShapeDtypeStruct((T, D), BF16),
                   jax.ShapeDtypeStruct((T, D), F32), jax.ShapeDtypeStruct((T, 1), F32)),
        grid=(T // tm,), in_specs=[row, row, vec, vec],
        out_specs=(row, row, row, pl.BlockSpec((tm, 1), lambda i: (i, 0))),
        scratch_shapes=[], args=(x_res, delta, g, b), sem=("parallel",), rider=rider)


def _ln_bwd_rows(dy, xh, rstd, g):
    dxh = dy * g
    m1 = jnp.mean(dxh, axis=1, keepdims=True)
    m2 = jnp.mean(dxh * xh, axis=1, keepdims=True)
    return rstd * (dxh - m1 - xh * m2)


def _ln_loss_bwd(x_res, delta, g, b, target):
    T, D = x_res.shape
    tm = _pick(T, (256, 128))

    def body(x_ref, d_ref, g_ref, b_ref, t_ref, dz_ref, dzb_ref, loss_ref, dg_ref, db_ref):
        i = pl.program_id(0)

        @pl.when(i == 0)
        def _():
            loss_ref[...] = jnp.zeros_like(loss_ref)
            dg_ref[...] = jnp.zeros_like(dg_ref)
            db_ref[...] = jnp.zeros_like(db_ref)

        z = ALPHA * x_ref[...] + d_ref[...]
        mu = jnp.mean(z, axis=1, keepdims=True)
        zc = z - mu
        var = jnp.mean(zc * zc, axis=1, keepdims=True)
        rstd = lax.rsqrt(var + LN_EPS)
        xh = zc * rstd
        gv = g_ref[...]
        err = xh * gv + b_ref[...] - t_ref[...]
        loss_ref[...] += 0.5 * jnp.sum(jnp.mean(err * err, axis=1, keepdims=True))
        dy = err * (1.0 / D)
        dg_ref[...] += jnp.sum(dy * xh, axis=0, keepdims=True)
        db_ref[...] += jnp.sum(dy, axis=0, keepdims=True)
        dz = _ln_bwd_rows(dy, xh, rstd, gv)
        dz_ref[...] = dz
        dzb_ref[...] = dz.astype(BF16)

    row = pl.BlockSpec((tm, D), lambda i: (i, 0))
    vec = pl.BlockSpec((1, D), lambda i: (0, 0))
    return pl.pallas_call(
        body, name="ln2_loss_bwd",
        out_shape=(jax.ShapeDtypeStruct((T, D), F32), jax.ShapeDtypeStruct((T, D), BF16),
                   jax.ShapeDtypeStruct((8, LANES), F32),
                   jax.ShapeDtypeStruct((1, D), F32), jax.ShapeDtypeStruct((1, D), F32)),
        grid=(T // tm,), in_specs=[row, row, vec, vec, row],
        out_specs=(row, row, pl.BlockSpec((8, LANES), lambda i: (0, 0)), vec, vec),
        compiler_params=_cparams(("arbitrary",)),
    )(x_res, delta, g, b, target)


def _ln_bwd(dy, xh, rstd, g):
    T, D = dy.shape
    tm = _pick(T, (256, 128))

    def body(dy_ref, xh_ref, rs_ref, g_ref, dz_ref, dzb_ref, dg_ref, db_ref):
        i = pl.program_id(0)

        @pl.when(i == 0)
        def _():
            dg_ref[...] = jnp.zeros_like(dg_ref)
            db_ref[...] = jnp.zeros_like(db_ref)

        dyv, xhv = dy_ref[...], xh_ref[...]
        dg_ref[...] += jnp.sum(dyv * xhv, axis=0, keepdims=True)
        db_ref[...] += jnp.sum(dyv, axis=0, keepdims=True)
        dz = _ln_bwd_rows(dyv, xhv, rs_ref[...], g_ref[...])
        dz_ref[...] = dz
        dzb_ref[...] = dz.astype(BF16)

    row = pl.BlockSpec((tm, D), lambda i: (i, 0))
    vec = pl.BlockSpec((1, D), lambda i: (0, 0))
    return pl.pallas_call(
        body, name="ln1_bwd",
        out_shape=(jax.ShapeDtypeStruct((T, D), F32), jax.ShapeDtypeStruct((T, D), BF16),
                   jax.ShapeDtypeStruct((1, D), F32), jax.ShapeDtypeStruct((1, D), F32)),
        grid=(T // tm,), in_specs=[row, row, pl.BlockSpec((tm, 1), lambda i: (i, 0)), vec],
        out_specs=(row, row, vec, vec), compiler_params=_cparams(("arbitrary",)),
    )(dy, xh, rstd, g)


def _ffn_col_tile(T, d_ff):
    return _pick(d_ff, (256, 128)) if T >= 1024 else _pick(d_ff, (512, 256, 128))


def _ffn_gate(gp, cw_ref, cb_ref):
    return (cb_ref[...] + gp * cw_ref[2:3, :] + _shift_down(gp, 1) * cw_ref[1:2, :]
            + _shift_down(gp, 2) * cw_ref[0:1, :])


def _ffn_fwd(up, gpre, conv_w, conv_b, rider=None):
    T, d_ff = up.shape
    ct = _ffn_col_tile(T, d_ff)

    def body(up_ref, gp_ref, cw_ref, cb_ref, f_ref):
        gate = _ffn_gate(gp_ref[...], cw_ref, cb_ref)
        f_ref[...] = (_gelu(gate) * up_ref[...]).astype(BF16)

    col = pl.BlockSpec((T, ct), lambda j: (0, j))
    (f,), carried = _call(
        body, name="ffn_act_fwd", out_shape=[jax.ShapeDtypeStruct((T, d_ff), BF16)], grid=(d_ff // ct,),
        in_specs=[col, col, pl.BlockSpec((3, ct), lambda j: (0, j)), pl.BlockSpec((1, ct), lambda j: (0, j))],
        out_specs=[col], scratch_shapes=[], args=(up, gpre, conv_w, conv_b), sem=("parallel",), rider=rider)
    return f, carried


def _ffn_bwd(up, gpre, conv_w, conv_b, d_f, after=None):
    T, d_ff = up.shape
    ct = _ffn_col_tile(T, d_ff)

    def body(up_ref, gp_ref, cw_ref, cb_ref, df_ref, dup_ref, dgp_ref, dcw_ref, dcb_ref):
        gp = gp_ref[...]
        gate = _ffn_gate(gp, cw_ref, cb_ref)
        gel, dgel = _gelu_and_grad(gate)
        df = df_ref[...]
        dup_ref[...] = (df * gel).astype(BF16)
        dgate = df * up_ref[...] * dgel
        dcb_ref[...] = jnp.sum(dgate, axis=0, keepdims=True)
        dcw_ref[2:3, :] = jnp.sum(dgate * gp, axis=0, keepdims=True)
        dcw_ref[1:2, :] = jnp.sum(dgate * _shift_down(gp, 1), axis=0, keepdims=True)
        dcw_ref[0:1, :] = jnp.sum(dgate * _shift_down(gp, 2), axis=0, keepdims=True)
        dgp = (dgate * cw_ref[2:3, :] + _shift_up(dgate, 1) * cw_ref[1:2, :]
               + _shift_up(dgate, 2) * cw_ref[0:1, :])
        dgp_ref[...] = dgp.astype(BF16)

    col = pl.BlockSpec((T, ct), lambda j: (0, j))
    w3 = pl.BlockSpec((3, ct), lambda j: (0, j))
    v1 = pl.BlockSpec((1, ct), lambda j: (0, j))
    return _call(
        body, name="ffn_act_bwd",
        out_shape=(jax.ShapeDtypeStruct((T, d_ff), BF16), jax.ShapeDtypeStruct((T, d_ff), BF16),
                   jax.ShapeDtypeStruct((3, d_ff), F32), jax.ShapeDtypeStruct((1, d_ff), F32)),
        grid=(d_ff // ct,), in_specs=[col, col, w3, v1, col], out_specs=(col, col, w3, v1),
        scratch_shapes=[], args=(up, gpre, conv_w, conv_b, d_f), sem=("parallel",), after=after)[0]


def _adamw(w, g, m, v, name, after=None):
    R, C = w.shape
    tr = _row_tile(R, C * 4, 8, budget=1280 * 1024)
    c1 = 1.0 / (1.0 - ADAM_B1 ** ADAM_STEP)
    c2 = 1.0 / (1.0 - ADAM_B2 ** ADAM_STEP)

    def body(w_ref, g_ref, m_ref, v_ref, go_ref, d_ref, nm_ref, nv_ref):
        gv = g_ref[...]
        go_ref[...] = gv
        nm = ADAM_B1 * m_ref[...] + (1.0 - ADAM_B1) * gv
        nv = ADAM_B2 * v_ref[...] + (1.0 - ADAM_B2) * (gv * gv)
        nm_ref[...] = nm
        nv_ref[...] = nv
        d_ref[...] = -ADAM_LR * ((nm * c1) / (jnp.sqrt(nv * c2) + ADAM_EPS) + ADAM_WD * w_ref[...])

    blk = pl.BlockSpec((tr, C), lambda r: (r, 0))
    sh = jax.ShapeDtypeStruct((R, C), F32)
    return _call(body, name=name, out_shape=(sh,) * 4, grid=(R // tr,), in_specs=[blk] * 4, out_specs=(blk,) * 4,
                 scratch_shapes=[], args=(w, g, m, v), sem=("parallel",), after=after)[0]


def _group_blocks(w_blocks, per):
    nb, bw, _ = w_blocks.shape
    G = nb // per
    w4 = w_blocks.reshape(G, per, bw, bw)
    rows = []
    for p in range(per):
        parts = [w4[:, p] if q == p else jnp.zeros((G, bw, bw), w_blocks.dtype) for q in range(per)]
        rows.append(jnp.concatenate(parts, axis=2))
    return jnp.concatenate(rows, axis=1)


def _ungroup_blocks(w_groups, per):
    G, gw, _ = w_groups.shape
    bw = gw // per
    blocks = [w_groups[:, p * bw:(p + 1) * bw, p * bw:(p + 1) * bw] for p in range(per)]
    return jnp.stack(blocks, axis=1).reshape(G * per, bw, bw)


def _pack(parts):
    flat = jnp.concatenate([p.reshape(-1).astype(F32) for p in parts])
    n = flat.shape[0]
    rows = -(-n // LANES)
    rows = -(-rows // PACK_ROW_MULT) * PACK_ROW_MULT
    flat = jnp.pad(flat, (0, rows * LANES - n))
    return flat.reshape(rows, LANES)


def _unpack(packed, shapes):
    flat = packed.reshape(-1)
    out, off = [], 0
    for s in shapes:
        n = math.prod(s)
        out.append(flat[off:off + n].reshape(s))
        off += n
    return out


def kernel(x, w_in, b_gate, rnn_conv_w, rnn_conv_b, lru_wa, lru_ba, lru_wi, lru_bi, lru_lambda, attn_sinks, w_attn_proj, w_rnn_proj, w_out, ln1_g, ln1_b, ffn_w_up, ffn_w_gate, ffn_conv_w, ffn_conv_b, ffn_w_down, ln2_g, ln2_b, loss_target, m_w_in, m_b_gate, m_rnn_conv_w, m_rnn_conv_b, m_lru_wa, m_lru_ba, m_lru_wi, m_lru_bi, m_lru_lambda, m_attn_sinks, m_w_attn_proj, m_w_rnn_proj, m_w_out, m_ln1_g, m_ln1_b, m_ffn_w_up, m_ffn_w_gate, m_ffn_conv_w, m_ffn_conv_b, m_ffn_w_down, m_ln2_g, m_ln2_b, v_w_in, v_b_gate, v_rnn_conv_w, v_rnn_conv_b, v_lru_wa, v_lru_ba, v_lru_wi, v_lru_bi, v_lru_lambda, v_attn_sinks, v_w_attn_proj, v_w_rnn_proj, v_w_out, v_ln1_g, v_ln1_b, v_ffn_w_up, v_ffn_w_gate, v_ffn_conv_w, v_ffn_conv_b, v_ffn_w_down, v_ln2_g, v_ln2_b):
    weights = dict(w_in=w_in, b_gate=b_gate, rnn_conv_w=rnn_conv_w, rnn_conv_b=rnn_conv_b, lru_wa=lru_wa,
                   lru_ba=lru_ba, lru_wi=lru_wi, lru_bi=lru_bi, lru_lambda=lru_lambda, attn_sinks=attn_sinks,
                   w_attn_proj=w_attn_proj, w_rnn_proj=w_rnn_proj, w_out=w_out, ln1_g=ln1_g, ln1_b=ln1_b,
                   ffn_w_up=ffn_w_up, ffn_w_gate=ffn_w_gate, ffn_conv_w=ffn_conv_w, ffn_conv_b=ffn_conv_b,
                   ffn_w_down=ffn_w_down, ln2_g=ln2_g, ln2_b=ln2_b)
    m_in = dict(w_in=m_w_in, b_gate=m_b_gate, rnn_conv_w=m_rnn_conv_w, rnn_conv_b=m_rnn_conv_b, lru_wa=m_lru_wa,
                lru_ba=m_lru_ba, lru_wi=m_lru_wi, lru_bi=m_lru_bi, lru_lambda=m_lru_lambda, attn_sinks=m_attn_sinks,
                w_attn_proj=m_w_attn_proj, w_rnn_proj=m_w_rnn_proj, w_out=m_w_out, ln1_g=m_ln1_g, ln1_b=m_ln1_b,
                ffn_w_up=m_ffn_w_up, ffn_w_gate=m_ffn_w_gate, ffn_conv_w=m_ffn_conv_w, ffn_conv_b=m_ffn_conv_b,
                ffn_w_down=m_ffn_w_down, ln2_g=m_ln2_g, ln2_b=m_ln2_b)
    v_in = dict(w_in=v_w_in, b_gate=v_b_gate, rnn_conv_w=v_rnn_conv_w, rnn_conv_b=v_rnn_conv_b, lru_wa=v_lru_wa,
                lru_ba=v_lru_ba, lru_wi=v_lru_wi, lru_bi=v_lru_bi, lru_lambda=v_lru_lambda, attn_sinks=v_attn_sinks,
                w_attn_proj=v_w_attn_proj, w_rnn_proj=v_w_rnn_proj, w_out=v_w_out, ln1_g=v_ln1_g, ln1_b=v_ln1_b,
                ffn_w_up=v_ffn_w_up, ffn_w_gate=v_ffn_w_gate, ffn_conv_w=v_ffn_conv_w, ffn_conv_b=v_ffn_conv_b,
                ffn_w_down=v_ffn_w_down, ln2_g=v_ln2_g, ln2_b=v_ln2_b)
    order = list(weights)

    assert x.shape[0] == 1 and w_in.shape[0] == 1, "one sequence per device, depth 1"
    T, D = x.shape[1], x.shape[2]
    nq = attn_sinks.shape[-1]
    nkv = nq // GROUP
    d_attn, d_kv = nq * HEAD_DIM, nkv * HEAD_DIM
    d_rnn = rnn_conv_b.shape[-1]
    d_ff = ffn_conv_b.shape[-1]
    n_blocks, bw = lru_wa.shape[1], lru_wa.shape[2]
    per = (bw * LANES // math.gcd(bw, LANES)) // bw
    gw = per * bw
    assert n_blocks % per == 0 and d_rnn == n_blocks * bw
    q_off, k_off, v_off = 0, d_attn, d_attn + d_kv
    rx_off = d_attn + 2 * d_kv
    ry_off = rx_off + d_rnn
    gl_off = ry_off + d_rnn
    d_in = gl_off + 2 * D
    assert w_in.shape[-1] * N_SHARDS == d_in
    assert k_off % d_kv == 0 and rx_off % gw == 0 and T % ATTN_BLOCK == 0

    xi, yi, ci = lax.axis_index("x"), lax.axis_index("y"), lax.axis_index("c")
    j_me = 2 * xi + yi
    jc_arr = jnp.stack([j_me, ci]).astype(jnp.int32)

    x0 = x[0]
    tgt = loss_target[0]
    big = ["w_in", "w_attn_proj", "w_rnn_proj", "w_out", "ffn_w_up", "ffn_w_gate", "ffn_w_down"]
    near, diag = (0, 1), (2,)
    order_arr = jnp.stack([j_me, j_me ^ 2, j_me ^ 1, j_me ^ 3]).astype(jnp.int32)

    rcw_s, fcw_s = _all_gather_small([rnn_conv_w[0], ffn_conv_w[0]])
    rcw = jnp.concatenate([rcw_s[j] for j in range(N_SHARDS)], axis=1)
    fcw = jnp.concatenate([fcw_s[j] for j in range(N_SHARDS)], axis=1)

    own = {"w_in": _cast_bf16_into_slot(w_in[0], jc_arr, "cast_w_in", fcw_s)}
    in_near = _gather_step(own["w_in"], fcw_s, "gather_start_w_in", start_d2d=False, relations=near)
    last = in_near[2]
    for n in big[1:]:
        own[n] = last = _cast_bf16_into_slot(weights[n][0], jc_arr, "cast_" + n, last)
    x0b = _cast_bf16(x0, "cast_x", last)

    wa_g = _group_blocks(lru_wa[0], per).astype(BF16)
    wi_g = _group_blocks(lru_wi[0], per).astype(BF16)

    proj = _mm_shards(x0b, in_near[1], order_arr, [0], "mm_proj_own", x0b)
    d2d_sems, relay_sems, buf, tok = _gather_relay_step(in_near[1], proj, "gather_forward_w_in_near", in_near[0])
    w_in_s = _gather_step(buf, tok, "gather_finish_w_in_near", sems_in=d2d_sems, relations=near)
    proj = _mm_shards(x0b, w_in_s, order_arr, [1, 2], "mm_proj_near", w_in_s, out=proj)
    d2d_sems, buf, tok = _gather_relay_step(w_in_s, proj, "gather_forward_w_in_diag", None, relay_in=relay_sems)
    w_in_s = _gather_step(buf, tok, "gather_finish_w_in_diag", sems_in=d2d_sems, relations=diag)
    proj = _mm_shards(x0b, w_in_s, order_arr, [3], "mm_proj_diag", w_in_s, out=proj)
    ici, last = {}, proj
    for n in big[1:]:
        ici[n] = _gather_step(own[n], last, "gather_start_" + n, start_d2d=False)
        last = ici[n][2]

    def forward_halves(n, after):
        sems, buf, _ = ici[n]
        return _gather_step(buf, after, "gather_forward_" + n, sems_in=sems, start_d2d=True)

    def gathered(d2d, after, n):
        sems, buf, _ = d2d
        return _gather_step(buf, after, "gather_finish_" + n, sems_in=sems)

    a_out = _attn_fwd(proj, attn_sinks, nq, (q_off, k_off, v_off), after=last)
    fw_ap = forward_halves("w_attn_proj", a_out)
    (b_out, h_all), _ = _rnn_fwd(proj, (rx_off, ry_off), rcw, rnn_conv_b, wa_g, wi_g, lru_ba, lru_bi, lru_lambda)
    fw_rp = forward_halves("w_rnn_proj", b_out)
    w_ap = gathered(fw_ap, b_out, "w_attn_proj").reshape(d_attn, D)
    y_attn = _mm(a_out, w_ap, name="mm_attn_proj")
    fw_o = forward_halves("w_out", y_attn)
    w_rp = gathered(fw_rp, y_attn, "w_rnn_proj").reshape(d_rnn, D)
    y_rnn = _mm(b_out, w_rp, name="mm_rnn_proj")
    merged, _ = _merge_fwd(proj, gl_off, b_gate, y_attn, y_rnn)
    w_o = gathered(fw_o, merged, "w_out").reshape(D, D)
    mix = _mm(merged, w_o, name="mm_out")
    fw_up = forward_halves("ffn_w_up", mix)
    (x1, x1b, xh1, rstd1), _ = _ln_fwd(x0, mix, ln1_g, ln1_b, "ln1_fwd")
    w_up_s = gathered(fw_up, x1b, "ffn_w_up")
    up = _mm(x1b, w_up_s, name="mm_up", b_shards=N_SHARDS)
    fw_gate = forward_halves("ffn_w_gate", up)
    w_gate_s = gathered(fw_gate, fw_gate[2], "ffn_w_gate")
    gpre = _mm(x1b, w_gate_s, name="mm_gate", b_shards=N_SHARDS)
    f_act, _ = _ffn_fwd(up, gpre, fcw, ffn_conv_b)
    fw_dn = forward_halves("ffn_w_down", f_act)
    w_dn = gathered(fw_dn, fw_dn[2], "ffn_w_down").reshape(d_ff, D)
    f_out = _mm(f_act, w_dn, name="mm_down")
    dz2, dz2b, loss_acc, dg2, db2 = _ln_loss_bwd(x1, f_out, ln2_g, ln2_b, tgt)

    def pair_sums(arrs, from_sibling, names):
        return [_pair_sum(g, la, jc_arr, "pair_sum_" + n) for g, la, n in zip(arrs, from_sibling, names)]

    def shard_sums(parts, landed, names):
        return [_shard_sum(cp, lb, jc_arr, "shard_sum_" + n) for cp, lb, n in zip(parts, landed, names)]

    halves = {}
    g_down = _mm(f_act, dz2b, name="mm_d_w_down", ta=True, out_dtype=BF16)
    g1 = [g_down.reshape(N_SHARDS, d_ff // N_SHARDS, D)]
    d_f, sib1 = _mm(dz2b, w_dn, name="mm_d_f", tb=True, rider=_pair_rider(g1))
    sent1 = _shard_exchange_start(pair_sums(g1, sib1, ["ffn_w_down"]), "shard_exchange_start_down")
    dup, dgp, d_fcw, d_fcb = _ffn_bwd(up, gpre, fcw, ffn_conv_b, d_f, after=sent1[4])
    g_up = _mm(x1b, dup, name="mm_d_w_up", ta=True, out_dtype=BF16, out_shards=N_SHARDS)
    g_gate = _mm(x1b, dgp, name="mm_d_w_gate", ta=True, out_dtype=BF16, out_shards=N_SHARDS)
    g2 = [g_up, g_gate]
    dx1_a, sib2 = _mm(dup, w_up_s, name="mm_dx1_up", tb=True, b_shards=N_SHARDS, adds=((ALPHA, dz2),),
                      rider=_pair_rider(g2))
    halves["ffn_w_down"], = shard_sums(*_shard_exchange_wait(sent1, dx1_a, "shard_exchange_wait_down"),
                                       ["ffn_w_down"])
    sent2 = _shard_exchange_start(pair_sums(g2, sib2, ["ffn_w_up", "ffn_w_gate"]), "shard_exchange_start_up_gate")
    dx1 = _mm(dgp, w_gate_s, name="mm_dx1_gate", tb=True, b_shards=N_SHARDS, adds=((1.0, dx1_a),), after=sent2[4])
    dz1, dz1b, dg1, db1 = _ln_bwd(dx1, xh1, rstd1, ln1_g)
    g_out = _mm(merged, dz1b, name="mm_d_w_out", ta=True, out_dtype=BF16)
    d_m = _mm(dz1b, w_o, name="mm_d_merged", tb=True)
    dya, dyr, dgl_a, dgl_r, dbg_a, dbg_r = _merge_bwd(proj, gl_off, b_gate, y_attn, y_rnn, d_m)
    g_ap = _mm(a_out, dya, name="mm_d_w_attn_proj", ta=True, out_dtype=BF16)
    g_rp = _mm(b_out, dyr, name="mm_d_w_rnn_proj", ta=True, out_dtype=BF16)
    names3 = ["w_out", "w_attn_proj", "w_rnn_proj"]
    g3 = [g_out.reshape(N_SHARDS, D // N_SHARDS, D), g_ap.reshape(N_SHARDS, d_attn // N_SHARDS, D),
          g_rp.reshape(N_SHARDS, d_rnn // N_SHARDS, D)]
    d_a = _mm(dya, w_ap, name="mm_d_attn", tb=True)
    d_b, sib3 = _mm(dyr, w_rp, name="mm_d_rnn", tb=True, rider=_pair_rider(g3))
    sent3 = _shard_exchange_start(pair_sums(g3, sib3, names3), "shard_exchange_start_mixers")
    dq, dk, dv, dsink = _attn_bwd(proj, d_a, attn_sinks, nq, (q_off, k_off, v_off), after=sent3[4])
    (drx, dry, d_rcw, d_rcb, d_ba, d_bi, d_lam, d_wa_g, d_wi_g), _ = _rnn_bwd(
        proj, (rx_off, ry_off), h_all, d_b, rcw, rnn_conv_b, wa_g, wi_g, lru_ba, lru_bi, lru_lambda)
    halves["ffn_w_up"], halves["ffn_w_gate"] = shard_sums(
        *_shard_exchange_wait(sent2, drx, "shard_exchange_wait_up_gate"), ["ffn_w_up", "ffn_w_gate"])
    d_proj = jnp.concatenate([dq, dk.astype(BF16), dv.astype(BF16), drx, dry, dgl_a, dgl_r], axis=1)
    ffn_names = ["ffn_w_down", "ffn_w_up", "ffn_w_gate"]
    g_in, shared_ffn = _mm(x0b, d_proj, name="mm_d_w_in", ta=True, out_dtype=BF16, out_shards=N_SHARDS,
                           rider=_share_rider([halves[n] for n in ffn_names]))
    halves["w_out"], halves["w_attn_proj"], halves["w_rnn_proj"] = shard_sums(
        *_shard_exchange_wait(sent3, g_in, "shard_exchange_wait_mixers"), names3)

    small_parts = [
        ("loss", loss_acc[0:1, 0:1]),
        ("b_gate", jnp.concatenate([dbg_a, dbg_r], axis=1)),
        ("rnn_conv_w", d_rcw), ("rnn_conv_b", d_rcb),
        ("lru_wa", _ungroup_blocks(d_wa_g, per)), ("lru_ba", d_ba),
        ("lru_wi", _ungroup_blocks(d_wi_g, per)), ("lru_bi", d_bi), ("lru_lambda", d_lam),
        ("attn_sinks", dsink[0:1, 0:nq]),
        ("ln1_g", dg1), ("ln1_b", db1),
        ("ffn_conv_w", d_fcw), ("ffn_conv_b", d_fcb),
        ("ln2_g", dg2), ("ln2_b", db2),
    ]
    packed = _pack([p for _, p in small_parts])
    rs = packed.shape[0]

    def whole(g):
        return g.reshape(2 * g.shape[1], g.shape[2])

    grads = {n: whole(g) for n, g in zip(ffn_names, shared_ffn)}
    out_g, out_d, out_m, out_v = {}, {}, {}, {}

    def adamw(n, after=None):
        shape = weights[n].shape
        two_d = (math.prod(shape[:-1]), shape[-1])
        g2, d2, m2, v2 = _adamw(weights[n].reshape(two_d), grads[n].reshape(two_d), m_in[n].reshape(two_d),
                                v_in[n].reshape(two_d), "adamw_" + n, after=after)
        out_g[n], out_d[n] = g2.reshape(shape), d2.reshape(shape)
        out_m[n], out_v[n] = m2.reshape(shape), v2.reshape(shape)

    g4 = [g_in, packed.reshape(N_SHARDS, rs // N_SHARDS, LANES)]
    sib4 = _run_rider(_pair_rider(g4), "pair_exchange_in_small")
    part4 = pair_sums(g4, sib4, ["w_in", "small"])
    grad_x, (lb_in, lb_small, *shared_mix) = _mm(
        d_proj, w_in_s, name="mm_d_x", tb=True, b_shards=N_SHARDS, adds=((ALPHA, dz1),),
        rider=_join_riders(_shard_exchange_rider(part4, _atoms([0], near) + _atoms([1])),
                           _share_rider([halves[n] for n in names3])))
    grads.update({n: whole(g) for n, g in zip(names3, shared_mix)})
    sent5 = _shard_exchange_start(part4[:1], "shard_exchange_start_in_diag", relations=diag, lands=[lb_in])
    for n in ffn_names + names3:
        adamw(n, after=sent5[4])
    (part_in,), (lb_in,) = _shard_exchange_wait(sent5, out_d[names3[-1]], "shard_exchange_wait_in_diag")
    part_small = part4[1]
    halves["w_in"], = shard_sums([part_in], [lb_in], ["w_in"])
    eighths = _shard_sum(part_small, lb_small, jc_arr, "shard_sum_small", all_slots=True)
    sharing = _half_share_start(halves["w_in"], "share_start_w_in")
    (reduced,) = _run_rider(_share_rider([], eighths), "share_small")
    reduced = reduced.reshape(rs, LANES)
    small = dict(zip([n for n, _ in small_parts], _unpack(reduced, [p.shape for _, p in small_parts])))
    loss = small.pop("loss").reshape(())
    rcw_n = d_rnn // N_SHARDS
    fcw_n = d_ff // N_SHARDS
    small["rnn_conv_w"] = lax.dynamic_slice(small["rnn_conv_w"], (0, j_me * rcw_n), (4, rcw_n))
    small["ffn_conv_w"] = lax.dynamic_slice(small["ffn_conv_w"], (0, j_me * fcw_n), (3, fcw_n))
    for n, g in small.items():
        grads[n] = g

    for n in order:
        if n not in out_g and n != "w_in":
            adamw(n)
    grads["w_in"] = whole(_half_share_wait(sharing, out_d[order[-1]], "share_wait_w_in"))
    adamw("w_in")

    return (loss, grad_x.reshape(x.shape), *[out_g[n] for n in order], *[out_d[n] for n in order],
            *[out_m[n] for n in order], *[out_v[n] for n in order])
```

```python
import functools
import math

import jax
import jax.numpy as jnp
from jax import lax
from jax.experimental import pallas as pl
from jax.experimental.pallas import tpu as pltpu

F32 = jnp.float32
BF16 = jnp.bfloat16
MESH = pl.DeviceIdType.MESH

HEAD_DIM = 64
GROUP = 8
ATTN_BLOCK = 128
LRU_C = 8.0
LN_EPS = 1e-5
ALPHA = 2.0 ** 0.25
LANES = 128
N_SHARDS = 4
N_DEV = 8
VMEM_LIMIT = 56 * 1024 * 1024
MM_VMEM_BUDGET = 40 * 1024 * 1024
MM_MAX_TILE = 3072
PACK_ROW_MULT = 8 * 64
NEG = -1e30

ADAM_LR, ADAM_B1, ADAM_B2, ADAM_EPS, ADAM_WD, ADAM_STEP = 0.001, 0.9, 0.999, 1e-08, 0.01, 10

GELU_C = math.sqrt(2.0 / math.pi)
GELU_A = 0.044715


def _cparams(sem=None):
    kw = dict(vmem_limit_bytes=VMEM_LIMIT)
    if sem is not None:
        kw["dimension_semantics"] = sem
    return pltpu.CompilerParams(**kw)


def _pick(n, prefs):
    for p in prefs:
        if n % p == 0:
            return p
    return n


def _row_tile(rows, row_bytes, mult, budget=2 * 1024 * 1024):
    best = None
    for d in range(mult, rows + 1, mult):
        if rows % d == 0 and d * row_bytes <= budget:
            best = d
    return best if best is not None else rows


def _gelu(x):
    return 0.5 * x * (1.0 + jnp.tanh(GELU_C * (x + GELU_A * x * x * x)))


def _gelu_and_grad(x):
    t = jnp.tanh(GELU_C * (x + GELU_A * x * x * x))
    g = 0.5 * x * (1.0 + t)
    dg = 0.5 * (1.0 + t) + 0.5 * x * (1.0 - t * t) * GELU_C * (1.0 + 3.0 * GELU_A * x * x)
    return g, dg


def _shift_down(x, s, fill=0.0):
    row = lax.broadcasted_iota(jnp.int32, x.shape, 0)
    return jnp.where(row >= s, pltpu.roll(x, s, 0), fill)


def _shift_up(x, s, fill=0.0):
    n = x.shape[0]
    row = lax.broadcasted_iota(jnp.int32, x.shape, 0)
    return jnp.where(row < n - s, pltpu.roll(x, n - s, 0), fill)


def _mm(a, b, *, name, ta=False, tb=False, out_dtype=F32, adds=(), b_shards=1, out_shards=1,
        tm=None, tn=None, tk=None, rider=None, after=None):
    if ta:
        K, M = a.shape
    else:
        M, K = a.shape
    if b_shards > 1:
        n_sh = b.shape[-1]
        if tb:
            N = b.shape[1]
            assert b_shards * n_sh == K
        else:
            N = b_shards * n_sh
            assert b.shape[1] == K
    else:
        n_sh = None
        if tb:
            N = b.shape[0]
            assert b.shape[1] == K
        else:
            N = b.shape[1]
            assert b.shape[0] == K
    wide = (1024, 1536, 1280, 768, 640, 512, 256, 128)
    if tn is None:
        if b_shards > 1 and not tb:
            tn = n_sh if n_sh <= MM_MAX_TILE else _pick(n_sh, wide)
        elif out_shards > 1:
            tn = N // out_shards if N // out_shards <= MM_MAX_TILE else _pick(N // out_shards, wide)
        else:
            tn = _pick(N, wide)
    if tk is None:
        if b_shards > 1 and tb:
            tk = n_sh if n_sh <= MM_MAX_TILE else _pick(n_sh, wide)
        else:
            tk = K if K <= MM_MAX_TILE else _pick(K, (2048,) + wide)
    assert N % tn == 0 and K % tk == 0, (name, M, N, K, tn, tk)
    nk = K // tk
    n_add = len(adds)
    sa, sb, so = a.dtype.itemsize, b.dtype.itemsize, jnp.dtype(out_dtype).itemsize

    def vmem_bytes(tm_):
        return (2 * (tm_ * tk * sa + tk * tn * sb + tm_ * tn * so + n_add * tm_ * tn * 4)
                + (tm_ * tn * 4 if nk > 1 else 0))

    if tm is None:
        tm = _pick(M, (1024, 512, 256, 128)) if nk > 1 else _pick(M, (512, 256, 128))
        while vmem_bytes(tm) > MM_VMEM_BUDGET and tm % 256 == 0:
            tm //= 2
    assert M % tm == 0, (name, M, tm)
    b_outer = b.size * sb >= a.size * sa

    def ij(g0, g1):
        return (g1, g0) if b_outer else (g0, g1)

    def amap(g0, g1, k):
        i, _ = ij(g0, g1)
        return (k, i) if ta else (i, k)

    def bmap(g0, g1, k):
        _, j = ij(g0, g1)
        if b_shards > 1 and not tb:
            per = n_sh // tn
            return (j // per, k, j % per)
        if b_shards > 1 and tb:
            per = n_sh // tk
            return (k // per, j, k % per)
        return (j, k) if tb else (k, j)

    def omap(g0, g1, k):
        i, j = ij(g0, g1)
        if out_shards > 1:
            per_o = (N // out_shards) // tn
            return (j // per_o, i, j % per_o)
        return (i, j)

    a_spec = pl.BlockSpec((tk, tm) if ta else (tm, tk), amap)
    if b_shards > 1:
        b_spec = pl.BlockSpec((None, tn, tk) if tb else (None, tk, tn), bmap)
    else:
        b_spec = pl.BlockSpec((tn, tk) if tb else (tk, tn), bmap)
    add_specs = [pl.BlockSpec((tm, tn), lambda g0, g1, k: ij(g0, g1)) for _ in adds]
    if out_shards > 1:
        out_spec = pl.BlockSpec((None, tm, tn), omap)
        out_shape = jax.ShapeDtypeStruct((out_shards, M, N // out_shards), out_dtype)
    else:
        out_spec = pl.BlockSpec((tm, tn), omap)
        out_shape = jax.ShapeDtypeStruct((M, N), out_dtype)

    if ta:
        dims = (((0,), (0,)), ((), ()))
    elif tb:
        dims = (((1,), (1,)), ((), ()))
    else:
        dims = (((1,), (0,)), ((), ()))
    scales = tuple(s for s, _ in adds)

    def finish(r, add_refs, o_ref):
        for s, ref in zip(scales, add_refs):
            r = r + s * ref[...].astype(F32)
        o_ref[...] = r.astype(out_dtype)

    def body(a_ref, b_ref, *rest):
        add_refs = rest[:n_add]
        o_ref = rest[n_add]
        part = lax.dot_general(a_ref[...].astype(BF16), b_ref[...].astype(BF16), dims, preferred_element_type=F32)
        if nk == 1:
            finish(part, add_refs, o_ref)
            return
        acc = rest[n_add + 1]
        k = pl.program_id(2)

        @pl.when(k == 0)
        def _():
            acc[...] = part

        @pl.when(k > 0)
        def _():
            acc[...] += part

        @pl.when(k == nk - 1)
        def _():
            finish(acc[...], add_refs, o_ref)

    grid = (N // tn, M // tm, nk) if b_outer else (M // tm, N // tn, nk)
    (res,), carried = _call(
        body, name=name, grid=grid, in_specs=[a_spec, b_spec] + add_specs, out_specs=[out_spec],
        out_shape=[out_shape], scratch_shapes=[pltpu.VMEM((tm, tn), F32)] if nk > 1 else [],
        args=(a, b, *[x for _, x in adds]), sem=("parallel", "parallel", "arbitrary"), rider=rider, after=after)
    return (res, carried) if rider is not None else res


def _cast_bf16(w, name, after):
    R, C = w.shape
    tr = _row_tile(R, C * 4, 16)

    def body(w_ref, after_ref, o_ref):
        o_ref[...] = w_ref[...].astype(BF16)

    return pl.pallas_call(
        body, name=name, out_shape=jax.ShapeDtypeStruct((R, C), BF16), grid=(R // tr,),
        in_specs=[pl.BlockSpec((tr, C), lambda r: (r, 0)), pl.BlockSpec(memory_space=pl.ANY)],
        out_specs=pl.BlockSpec((tr, C), lambda r: (r, 0)), compiler_params=_cparams(("parallel",)),
    )(w, after)


def _cast_bf16_into_slot(w, jc_arr, name, after):
    R, C = w.shape
    tr = _row_tile(R, C * 4, 16)

    def body(jc_ref, w_ref, after_ref, o_ref):
        o_ref[...] = w_ref[...].astype(BF16)

    gs = pltpu.PrefetchScalarGridSpec(
        num_scalar_prefetch=1, grid=(R // tr,),
        in_specs=[pl.BlockSpec((tr, C), lambda r, jc: (r, 0)), pl.BlockSpec(memory_space=pl.ANY)],
        out_specs=pl.BlockSpec((None, tr, C), lambda r, jc: (jc[0], r, 0)))
    return pl.pallas_call(body, name=name, out_shape=jax.ShapeDtypeStruct((N_SHARDS, R, C), BF16), grid_spec=gs,
                          compiler_params=_cparams(("parallel",)))(jc_arr, w, after)


def _pair_sum(g, la, jc_arr, name):
    S, R, C = g.shape
    half = R // 2
    tr = _row_tile(half, C * 4, 16)
    nrt = half // tr
    dt = g.dtype

    def body(jc_ref, g_ref, la_ref, o_ref):
        o_ref[...] = (g_ref[...].astype(F32) + la_ref[...].astype(F32)).astype(dt)

    gs = pltpu.PrefetchScalarGridSpec(
        num_scalar_prefetch=1, grid=(S, nrt),
        in_specs=[pl.BlockSpec((None, tr, C), lambda s, r, jc: (s, jc[1] * nrt + r, 0)),
                  pl.BlockSpec((None, tr, C), lambda s, r, jc: (s, r, 0))],
        out_specs=pl.BlockSpec((None, tr, C), lambda s, r, jc: (s, r, 0)))
    return pl.pallas_call(body, name=name, out_shape=jax.ShapeDtypeStruct((S, half, C), dt), grid_spec=gs,
                          compiler_params=_cparams(("parallel", "parallel")))(jc_arr, g, la)


def _shard_sum(cp, lb, jc_arr, name, all_slots=False):
    S, h, C = cp.shape
    tr = _row_tile(h, C * 4, 16)

    def body(jc_ref, cp_ref, l0, l1, l2, o_ref):
        o_ref[...] = ((cp_ref[...].astype(F32) + l0[...].astype(F32)) + l1[...].astype(F32)) + l2[...].astype(F32)

    def lspec(kk):
        return pl.BlockSpec((None, tr, C), lambda r, jc: (kk, r, 0))

    if all_slots:
        out_spec = pl.BlockSpec((None, None, tr, C), lambda r, jc: (jc[0], jc[1], r, 0))
        out_shape = jax.ShapeDtypeStruct((S, 2, h, C), F32)
    else:
        out_spec = pl.BlockSpec((None, tr, C), lambda r, jc: (jc[1], r, 0))
        out_shape = jax.ShapeDtypeStruct((2, h, C), F32)
    gs = pltpu.PrefetchScalarGridSpec(
        num_scalar_prefetch=1, grid=(h // tr,),
        in_specs=[pl.BlockSpec((None, tr, C), lambda r, jc: (jc[0], r, 0)), lspec(0), lspec(1), lspec(2)],
        out_specs=out_spec)
    return pl.pallas_call(body, name=name, out_shape=out_shape, grid_spec=gs,
                          compiler_params=_cparams(("parallel",)))(jc_arr, cp, lb, lb, lb)


ANY = pl.BlockSpec(memory_space=pl.ANY)


def _place():
    x, y, c = lax.axis_index("x"), lax.axis_index("y"), lax.axis_index("c")
    chips = [(1 - x, y), (x, 1 - y), (1 - x, 1 - y)]
    return x, y, c, chips


class _Rider:
    def __init__(self, inputs, out_shape, aliases, sems, start, finish):
        self.inputs, self.out_shape, self.aliases, self.sems = list(inputs), list(out_shape), dict(aliases), list(sems)
        self.start, self.finish = start, finish


def _join_riders(r1, r2):
    i1, o1, s1 = len(r1.inputs), len(r1.out_shape), len(r1.sems)
    aliases = dict(r1.aliases)
    aliases.update({i1 + i: o1 + o for i, o in r2.aliases.items()})

    def start(ins, outs, sems):
        r1.start(ins[:i1], outs[:o1], sems[:s1])
        r2.start(ins[i1:], outs[o1:], sems[s1:])

    def finish(ins, outs, sems):
        r1.finish(ins[:i1], outs[:o1], sems[:s1])
        r2.finish(ins[i1:], outs[o1:], sems[s1:])

    return _Rider(r1.inputs + r2.inputs, r1.out_shape + r2.out_shape, aliases, r1.sems + r2.sems, start, finish)


def _after_rider(x):
    return _Rider([x], [], {}, [], lambda *a: None, lambda *a: None)


def _call(body, *, name, grid, in_specs, out_specs, out_shape, scratch_shapes, args, sem, rider=None, after=None):
    out_specs, out_shape = tuple(out_specs), tuple(out_shape)
    if after is not None:
        rider = _after_rider(after) if rider is None else _join_riders(_after_rider(after), rider)
    if rider is None:
        res = pl.pallas_call(body, name=name, out_shape=out_shape, grid=grid, in_specs=list(in_specs),
                             out_specs=out_specs, scratch_shapes=list(scratch_shapes),
                             compiler_params=_cparams(sem))(*args)
        return tuple(res), []
    n_in, n_out, n_sc = len(in_specs), len(out_specs), len(scratch_shapes)
    r_in, r_out = len(rider.inputs), len(rider.out_shape)

    def wrapped(*refs):
        p = 0
        host_in = refs[p:p + n_in]; p += n_in
        rid_in = refs[p:p + r_in]; p += r_in
        host_out = refs[p:p + n_out]; p += n_out
        rid_out = refs[p:p + r_out]; p += r_out
        host_sc = refs[p:p + n_sc]; p += n_sc
        rid_sem = refs[p:]
        first = functools.reduce(jnp.logical_and, [pl.program_id(a) == 0 for a in range(len(grid))])
        last = functools.reduce(jnp.logical_and, [pl.program_id(a) == grid[a] - 1 for a in range(len(grid))])

        @pl.when(first)
        def _():
            rider.start(rid_in, rid_out, rid_sem)

        body(*host_in, *host_out, *host_sc)

        @pl.when(last)
        def _():
            rider.finish(rid_in, rid_out, rid_sem)

    res = pl.pallas_call(
        wrapped, name=name, out_shape=out_shape + tuple(rider.out_shape), grid=grid,
        in_specs=list(in_specs) + [ANY] * r_in, out_specs=out_specs + (ANY,) * r_out,
        input_output_aliases={n_in + i: n_out + o for i, o in rider.aliases.items()},
        scratch_shapes=list(scratch_shapes) + rider.sems,
        compiler_params=_cparams(("arbitrary",) * len(grid)),
    )(*args, *rider.inputs)
    return tuple(res[:n_out]), list(res[n_out:])


def _run_rider(rider, name):
    def body(*refs):
        r_in, r_out = len(rider.inputs), len(rider.out_shape)
        ins, outs, sems = refs[:r_in], refs[r_in:r_in + r_out], refs[r_in + r_out:]
        rider.start(ins, outs, sems)
        rider.finish(ins, outs, sems)

    return pl.pallas_call(
        body, name=name, out_shape=rider.out_shape, in_specs=[ANY] * len(rider.inputs),
        out_specs=[ANY] * len(rider.out_shape), input_output_aliases=rider.aliases, scratch_shapes=rider.sems,
    )(*rider.inputs)


def _atoms(indices, kks=(0, 1, 2), q=0, nq=1):
    return [(i, kk, q, nq) for i in indices for kk in kks]


def _mm_shards(a, buf, order_arr, which, name, after, out=None):
    M, K = a.shape
    S, _, n = buf.shape
    tm = _pick(M, (512, 256, 128))
    s0 = which[0]

    def body(order_ref, a_ref, b_ref, *rest):
        rest[-1][...] = jnp.dot(a_ref[...], b_ref[...], preferred_element_type=F32)

    gs = pltpu.PrefetchScalarGridSpec(
        num_scalar_prefetch=1, grid=(len(which), M // tm),
        in_specs=[pl.BlockSpec((tm, K), lambda g, i, order: (i, 0)),
                  pl.BlockSpec((None, K, n), lambda g, i, order: (order[s0 + g], 0, 0)), ANY]
        + ([ANY] if out is not None else []),
        out_specs=pl.BlockSpec((tm, n), lambda g, i, order: (i, order[s0 + g])))
    return pl.pallas_call(
        body, name=name, grid_spec=gs, out_shape=jax.ShapeDtypeStruct((M, S * n), F32),
        input_output_aliases={4: 0} if out is not None else {},
        compiler_params=_cparams(("arbitrary", "arbitrary")),
    )(order_arr, a, buf, after, *([out] if out is not None else []))


def _all_gather_small(shards):
    n = len(shards)

    def body(*refs):
        w = refs[:n]
        out = refs[n:2 * n]
        local_sem, s_sem, r_sem = refs[2 * n:]
        x, y, c, chips = _place()
        j_me = 2 * x + y
        cps = []
        for i in range(n):
            lc = pltpu.make_async_copy(w[i], out[i].at[j_me], local_sem.at[i])
            lc.start()
            cps.append(lc)
        sends = []
        for i in range(n):
            for kk, (px, py) in enumerate(chips):
                cp = pltpu.make_async_remote_copy(
                    src_ref=w[i], dst_ref=out[i].at[j_me], send_sem=s_sem.at[3 * i + kk],
                    recv_sem=r_sem.at[3 * i + kk], device_id=(px, py, c), device_id_type=MESH)
                cp.start()
                sends.append(cp)
        for i in range(n):
            for kk, (px, py) in enumerate(chips):
                sends[3 * i + kk].wait_send()
                pltpu.make_async_remote_copy(
                    src_ref=w[i], dst_ref=out[i].at[2 * px + py], send_sem=s_sem.at[3 * i + kk],
                    recv_sem=r_sem.at[3 * i + kk], device_id=(px, py, c), device_id_type=MESH).wait_recv()
        for lc in cps:
            lc.wait()

    out_shape = [jax.ShapeDtypeStruct((N_SHARDS,) + s.shape, s.dtype) for s in shards]
    return pl.pallas_call(
        body, name="all_gather_conv_weights", out_shape=out_shape, in_specs=[ANY] * n, out_specs=[ANY] * n,
        scratch_shapes=[pltpu.SemaphoreType.DMA((n,)), pltpu.SemaphoreType.DMA((3 * n,)),
                        pltpu.SemaphoreType.DMA((3 * n,))],
    )(*shards)


def _pair_rider(grads):
    n = len(grads)

    def copies(g, la, sems):
        x, y, c, _ = _place()
        return [pltpu.make_async_remote_copy(
            src_ref=g[i].at[:, pl.ds((1 - c) * (g[i].shape[1] // 2), g[i].shape[1] // 2), :], dst_ref=la[i],
            send_sem=sems[0].at[i], recv_sem=sems[1].at[i], device_id=(x, y, 1 - c), device_id_type=MESH)
            for i in range(n)]

    def start(g, la, sems):
        for cp in copies(g, la, sems):
            cp.start()

    def finish(g, la, sems):
        for cp in copies(g, la, sems):
            cp.wait()

    return _Rider(grads, [jax.ShapeDtypeStruct((s.shape[0], s.shape[1] // 2, s.shape[2]), s.dtype) for s in grads],
                  {}, [pltpu.SemaphoreType.DMA((n,)), pltpu.SemaphoreType.DMA((n,))], start, finish)


def _shard_exchange_rider(cps_in, atoms=None):
    n = len(cps_in)
    if atoms is None:
        atoms = _atoms(range(n))

    def copies(ins, lb, sems):
        x, y, c, chips = _place()
        out = []
        for a, (i, kk, q, nq) in enumerate(atoms):
            h = ins[i].shape[1]
            assert h % (16 * nq) == 0, (h, nq)
            rows = pl.ds(q * (h // nq), h // nq)
            px, py = chips[kk]
            out.append(pltpu.make_async_remote_copy(
                src_ref=ins[i].at[2 * px + py, rows, :], dst_ref=lb[i].at[kk, rows, :],
                send_sem=sems[0].at[a], recv_sem=sems[1].at[a], device_id=(px, py, c), device_id_type=MESH))
        return out

    def start(ins, lb, sems):
        for cp in copies(ins, lb, sems):
            cp.start()

    def finish(ins, lb, sems):
        for cp in copies(ins, lb, sems):
            cp.wait()

    return _Rider(cps_in, [jax.ShapeDtypeStruct((3,) + s.shape[1:], s.dtype) for s in cps_in], {},
                  [pltpu.SemaphoreType.DMA((len(atoms),)), pltpu.SemaphoreType.DMA((len(atoms),))], start, finish)


HBM = pl.BlockSpec(memory_space=pltpu.HBM)
SEM = pl.BlockSpec(memory_space=pltpu.SEMAPHORE)


def _shard_copies(part_refs, land_refs, send_sems, recv_sems, relations):
    x, y, c, chips = _place()
    nr = len(relations)
    return [pltpu.make_async_remote_copy(
        src_ref=part_refs[i].at[2 * chips[kk][0] + chips[kk][1]], dst_ref=land_refs[i].at[kk],
        send_sem=send_sems.at[nr * i + r], recv_sem=recv_sems.at[nr * i + r],
        device_id=(chips[kk][0], chips[kk][1], c), device_id_type=MESH)
        for i in range(len(part_refs)) for r, kk in enumerate(relations)]


SIDE_EFFECT = pltpu.SideEffectType.DATAFLOW_SIDE_EFFECTING


def _shard_exchange_start(parts, name, relations=(0, 1, 2), lands=None):
    n = len(parts)
    ns = n * len(relations)

    def body(*refs):
        part_refs, land_refs = refs[:n], refs[n:2 * n]
        send_sems, recv_sems = refs[2 * n], refs[2 * n + 1]
        token = refs[4 * n + 2]
        for cp in _shard_copies(part_refs, land_refs, send_sems, recv_sems, relations):
            cp.start()
        token[...] = jnp.zeros_like(token)

    if lands is None:
        lands = [lax.empty((3,) + p.shape[1:], p.dtype) for p in parts]
    bufs = list(parts) + list(lands)
    res = pl.pallas_call(
        body, name=name,
        out_shape=(pltpu.SemaphoreType.DMA((ns,)), pltpu.SemaphoreType.DMA((ns,)),
                   *[pltpu.HBM(b.shape, b.dtype) for b in bufs], jax.ShapeDtypeStruct((8, LANES), F32)),
        in_specs=(HBM,) * (2 * n), out_specs=(SEM, SEM) + (HBM,) * (2 * n) + (pl.BlockSpec(memory_space=pltpu.VMEM),),
        input_output_aliases={i: 2 + i for i in range(2 * n)},
        compiler_params=pltpu.CompilerParams(has_side_effects=SIDE_EFFECT),
    )(*[pltpu.with_memory_space_constraint(b, pltpu.HBM) for b in bufs])
    return res[0], res[1], list(res[2:2 + n]), list(res[2 + n:2 + 2 * n]), res[2 + 2 * n], relations


def _shard_exchange_wait(started, after, name):
    send_sems, recv_sems, parts, lands, _, relations = started
    n = len(parts)

    def body(*refs):
        part_refs, land_refs = refs[:n], refs[n:2 * n]
        send_sems_ref, recv_sems_ref = refs[2 * n], refs[2 * n + 1]
        for cp in _shard_copies(part_refs, land_refs, send_sems_ref, recv_sems_ref, relations):
            cp.wait_send()
            cp.wait_recv()

    bufs = parts + lands
    res = pl.pallas_call(
        body, name=name, out_shape=tuple(pltpu.HBM(b.shape, b.dtype) for b in bufs),
        in_specs=(HBM,) * (2 * n) + (SEM, SEM, ANY), out_specs=(HBM,) * (2 * n),
        input_output_aliases={i: i for i in range(2 * n)},
        compiler_params=pltpu.CompilerParams(has_side_effects=SIDE_EFFECT),
    )(*bufs, send_sems, recv_sems, after)
    return list(res[:n]), list(res[n:])


def _gather_copies(buf_ref, send_sems, recv_sems, over_d2d, arriving, relations):
    x, y, c, chips = _place()
    half = buf_ref.shape[1] // 2
    out = []
    for r, kk in enumerate(relations):
        px, py = chips[kk]
        if over_d2d:
            slot, core, peer = 2 * px + py, (1 - c) if arriving else c, (x, y, 1 - c)
        else:
            slot, core, peer = (2 * px + py) if arriving else (2 * x + y), c, (px, py, c)
        blk = buf_ref.at[slot, pl.ds(core * half, half), :]
        out.append(pltpu.make_async_remote_copy(src_ref=blk, dst_ref=blk, send_sem=send_sems.at[r],
                                                recv_sem=recv_sems.at[r], device_id=peer, device_id_type=MESH))
    return out


def _gather_step(buf, after, name, sems_in=None, start_d2d=None, relations=(0, 1, 2)):
    n_sem = 0 if sems_in is None else 2

    def body(*refs):
        buf_ref = refs[0]
        ins = refs[1:1 + n_sem]
        outs = refs[2 + n_sem:]
        if sems_in is not None:
            waited_d2d = start_d2d is None
            for mine, theirs in zip(_gather_copies(buf_ref, ins[0], ins[1], waited_d2d, False, relations),
                                    _gather_copies(buf_ref, ins[0], ins[1], waited_d2d, True, relations)):
                theirs.wait_recv()
                mine.wait_send()
        if start_d2d is not None:
            for cp in _gather_copies(buf_ref, outs[0], outs[1], start_d2d, False, relations):
                cp.start()
            outs[3][...] = jnp.zeros_like(outs[3])

    nr = len(relations)
    sem_out = () if start_d2d is None else (pltpu.SemaphoreType.DMA((nr,)), pltpu.SemaphoreType.DMA((nr,)))
    tok_out = () if start_d2d is None else (jax.ShapeDtypeStruct((8, LANES), F32),)
    res = pl.pallas_call(
        body, name=name,
        out_shape=sem_out + (pltpu.HBM(buf.shape, buf.dtype),) + tok_out,
        in_specs=(HBM,) + (SEM,) * n_sem + (ANY,),
        out_specs=(SEM,) * len(sem_out) + (HBM,) + (pl.BlockSpec(memory_space=pltpu.VMEM),) * len(tok_out),
        input_output_aliases={0: len(sem_out)},
        compiler_params=pltpu.CompilerParams(has_side_effects=SIDE_EFFECT),
    )(pltpu.with_memory_space_constraint(buf, pltpu.HBM), *(sems_in or ()), after)
    if start_d2d is None:
        return res[0]
    return (res[0], res[1]), res[2], res[3]


def _gather_start_all(bufs, after, name):
    n = len(bufs)

    def body(*refs):
        outs = refs[n + 1:]
        for i in range(n):
            for cp in _gather_copies(refs[i], outs[2 * i], outs[2 * i + 1], False, False, (0, 1, 2)):
                cp.start()
        outs[-1][...] = jnp.zeros_like(outs[-1])

    res = pl.pallas_call(
        body, name=name,
        out_shape=(pltpu.SemaphoreType.DMA((3,)),) * (2 * n) + tuple(pltpu.HBM(b.shape, b.dtype) for b in bufs)
        + (jax.ShapeDtypeStruct((8, LANES), F32),),
        in_specs=(HBM,) * n + (ANY,),
        out_specs=(SEM,) * (2 * n) + (HBM,) * n + (pl.BlockSpec(memory_space=pltpu.VMEM),),
        input_output_aliases={i: 2 * n + i for i in range(n)},
        compiler_params=pltpu.CompilerParams(has_side_effects=SIDE_EFFECT),
    )(*[pltpu.with_memory_space_constraint(b, pltpu.HBM) for b in bufs], after)
    return [((res[2 * i], res[2 * i + 1]), res[2 * n + i]) for i in range(n)], res[3 * n]


def _relay_copies(buf_ref, send_sems, recv_sems, arriving):
    x, y, c, chips = _place()
    quarter = buf_ref.shape[1] // 4
    out = []
    for r, (src_kk, dst_kk) in enumerate(((0, 1), (1, 0))):
        slot = (2 * chips[2][0] + chips[2][1]) if arriving else (2 * chips[src_kk][0] + chips[src_kk][1])
        blk = buf_ref.at[slot, pl.ds((2 * c + r) * quarter, quarter), :]
        out.append(pltpu.make_async_remote_copy(
            src_ref=blk, dst_ref=blk, send_sem=send_sems.at[r], recv_sem=recv_sems.at[r],
            device_id=(chips[dst_kk][0], chips[dst_kk][1], c), device_id_type=MESH))
    return out


def _gather_relay_step(buf, after, name, sems_in, relay_in=None):
    first = relay_in is None
    ins_sems = sems_in if first else relay_in
    near, diag = (0, 1), (2,)

    def body(*refs):
        buf_ref, in_s, in_r = refs[0], refs[1], refs[2]
        outs = refs[4:]
        if first:
            for mine, theirs in zip(_gather_copies(buf_ref, in_s, in_r, False, False, near),
                                    _gather_copies(buf_ref, in_s, in_r, False, True, near)):
                theirs.wait_recv()
                mine.wait_send()
            for cp in _gather_copies(buf_ref, outs[0], outs[1], True, False, near):
                cp.start()
            for cp in _relay_copies(buf_ref, outs[2], outs[3], False):
                cp.start()
        else:
            for mine, theirs in zip(_relay_copies(buf_ref, in_s, in_r, False), _relay_copies(buf_ref, in_s, in_r, True)):
                theirs.wait_recv()
                mine.wait_send()
            for cp in _gather_copies(buf_ref, outs[0], outs[1], True, False, diag):
                cp.start()
        outs[-1][...] = jnp.zeros_like(outs[-1])

    def sem(n):
        return pltpu.SemaphoreType.DMA((n,))

    sem_out = (sem(2), sem(2), sem(2), sem(2)) if first else (sem(1), sem(1))
    res = pl.pallas_call(
        body, name=name,
        out_shape=sem_out + (pltpu.HBM(buf.shape, buf.dtype), jax.ShapeDtypeStruct((8, LANES), F32)),
        in_specs=(HBM, SEM, SEM, ANY),
        out_specs=(SEM,) * len(sem_out) + (HBM, pl.BlockSpec(memory_space=pltpu.VMEM)),
        input_output_aliases={0: len(sem_out)},
        compiler_params=pltpu.CompilerParams(has_side_effects=SIDE_EFFECT),
    )(pltpu.with_memory_space_constraint(buf, pltpu.HBM), *ins_sems, after)
    if first:
        return (res[0], res[1]), (res[2], res[3]), res[4], res[5]
    return (res[0], res[1]), res[2], res[3]


def _share_rider(halves, eighths=None):
    n = len(halves)
    bufs = list(halves) + ([eighths] if eighths is not None else [])

    def half_copy(out, sems, i, core):
        x, y, c, _ = _place()
        blk = out[i].at[core]
        return pltpu.make_async_remote_copy(src_ref=blk, dst_ref=blk, send_sem=sems[0].at[i], recv_sem=sems[1].at[i],
                                            device_id=(x, y, 1 - c), device_id_type=MESH)

    def eighth_copy(out, sems, r, mine):
        x, y, c, _ = _place()
        px, py, pc = x ^ ((r >> 2) & 1), y ^ ((r >> 1) & 1), c ^ (r & 1)
        blk = out[n].at[2 * x + y, c] if mine else out[n].at[2 * px + py, pc]
        return pltpu.make_async_remote_copy(src_ref=blk, dst_ref=blk, send_sem=sems[2].at[r - 1],
                                            recv_sem=sems[3].at[r - 1], device_id=(px, py, pc), device_id_type=MESH)

    def start(ins, out, sems):
        c = lax.axis_index("c")
        for i in range(n):
            half_copy(out, sems, i, c).start()
        if eighths is not None:
            for r in range(1, N_DEV):
                eighth_copy(out, sems, r, True).start()

    def finish(ins, out, sems):
        c = lax.axis_index("c")
        for i in range(n):
            half_copy(out, sems, i, 1 - c).wait_recv()
        if eighths is not None:
            for r in range(1, N_DEV):
                eighth_copy(out, sems, r, False).wait_recv()
        for i in range(n):
            half_copy(out, sems, i, c).wait_send()
        if eighths is not None:
            for r in range(1, N_DEV):
                eighth_copy(out, sems, r, True).wait_send()

    return _Rider(bufs, [jax.ShapeDtypeStruct(s.shape, s.dtype) for s in bufs], {i: i for i in range(len(bufs))},
                  [pltpu.SemaphoreType.DMA((max(n, 1),)), pltpu.SemaphoreType.DMA((max(n, 1),)),
                   pltpu.SemaphoreType.DMA((N_DEV - 1,)), pltpu.SemaphoreType.DMA((N_DEV - 1,))], start, finish)


ATTN_ROWS = GROUP * ATTN_BLOCK
ATTN_KEYS = 2 * ATTN_BLOCK


def _attn_geometry(n):
    row = lax.broadcasted_iota(jnp.int32, (ATTN_ROWS, ATTN_KEYS), 0)
    col = lax.broadcasted_iota(jnp.int32, (ATTN_ROWS, ATTN_KEYS), 1)
    dist = ATTN_BLOCK + jnp.bitwise_and(row, ATTN_BLOCK - 1) - col
    valid = jnp.logical_and(jnp.logical_and(dist >= 0, dist < ATTN_BLOCK),
                            jnp.logical_or(col >= ATTN_BLOCK, n > 0))
    return dist.astype(F32), valid


def _per_head_column(values):
    head = lax.broadcasted_iota(jnp.int32, (ATTN_ROWS, 1), 0) // ATTN_BLOCK
    col = jnp.zeros((ATTN_ROWS, 1), F32)
    for hh, v in enumerate(values):
        col = jnp.where(head == hh, v, col)
    return col


def _stack_heads(ref, g):
    return jnp.concatenate(
        [ref[:, (g * GROUP + hh) * HEAD_DIM:(g * GROUP + hh + 1) * HEAD_DIM].astype(BF16) for hh in range(GROUP)],
        axis=0)


def _attn_probs(q_s, k2, slope_col, sink_col, dist, valid):
    s = lax.dot_general(q_s, k2, (((1,), (1,)), ((), ())), preferred_element_type=F32) * (HEAD_DIM ** -0.5)
    s = jnp.where(valid, s - slope_col * dist, NEG)
    m = jnp.maximum(jnp.max(s, axis=1, keepdims=True), sink_col)
    e = jnp.exp(s - m)
    es = jnp.exp(sink_col - m)
    inv = 1.0 / (jnp.sum(e, axis=1, keepdims=True) + es)
    return e * inv, es * inv


def _attn_specs(T, d_attn, d_kv, q_blk, k_blk, v_blk):
    bq = pl.BlockSpec((ATTN_BLOCK, d_attn), lambda n: (n, q_blk))
    kp = pl.BlockSpec((ATTN_BLOCK, d_kv), lambda n: (jnp.maximum(n - 1, 0), k_blk))
    kc = pl.BlockSpec((ATTN_BLOCK, d_kv), lambda n: (n, k_blk))
    vp = pl.BlockSpec((ATTN_BLOCK, d_kv), lambda n: (jnp.maximum(n - 1, 0), v_blk))
    vc = pl.BlockSpec((ATTN_BLOCK, d_kv), lambda n: (n, v_blk))
    return bq, kp, kc, vp, vc


def _attn_fwd(proj, sinks, nq, cols, after=None):
    T = proj.shape[0]
    nkv = nq // GROUP
    d_attn, d_kv = nq * HEAD_DIM, nkv * HEAD_DIM
    q_off, k_off, v_off = cols
    bq, kp, kc, vp, vc = _attn_specs(T, d_attn, d_kv, q_off // d_attn, k_off // d_kv, v_off // d_kv)

    def body(sink_ref, q_ref, kp_ref, kc_ref, vp_ref, vc_ref, o_ref):
        n = pl.program_id(0)
        dist, valid = _attn_geometry(n)
        for g in range(nkv):
            ks = slice(g * HEAD_DIM, (g + 1) * HEAD_DIM)
            k2 = jnp.concatenate([kp_ref[:, ks], kc_ref[:, ks]], axis=0).astype(BF16)
            v2 = jnp.concatenate([vp_ref[:, ks], vc_ref[:, ks]], axis=0).astype(BF16)
            slope_col = _per_head_column([2.0 ** (-8.0 * (g * GROUP + hh + 1) / nq) for hh in range(GROUP)])
            sink_col = _per_head_column([sink_ref[0, g * GROUP + hh] for hh in range(GROUP)])
            p, _ = _attn_probs(_stack_heads(q_ref, g), k2, slope_col, sink_col, dist, valid)
            o = jnp.dot(p.astype(BF16), v2, preferred_element_type=F32).astype(BF16)
            for hh in range(GROUP):
                h = g * GROUP + hh
                o_ref[:, h * HEAD_DIM:(h + 1) * HEAD_DIM] = o[hh * ATTN_BLOCK:(hh + 1) * ATTN_BLOCK, :]

    (out,), carried = _call(
        body, name="attn_fwd", out_shape=[jax.ShapeDtypeStruct((T, d_attn), BF16)], grid=(T // ATTN_BLOCK,),
        in_specs=[pl.BlockSpec(memory_space=pltpu.SMEM), bq, kp, kc, vp, vc],
        out_specs=[pl.BlockSpec((ATTN_BLOCK, d_attn), lambda n: (n, 0))], scratch_shapes=[],
        args=(sinks, proj, proj, proj, proj, proj), sem=("parallel",), after=after)
    return out


def _attn_bwd(proj, d_attn_out, sinks, nq, cols, after=None):
    T = proj.shape[0]
    nkv = nq // GROUP
    d_attn, d_kv = nq * HEAD_DIM, nkv * HEAD_DIM
    q_off, k_off, v_off = cols
    bq, kp, kc, vp, vc = _attn_specs(T, d_attn, d_kv, q_off // d_attn, k_off // d_kv, v_off // d_kv)
    scale = HEAD_DIM ** -0.5
    dn_t = (((1,), (1,)), ((), ()))
    dn_r = (((0,), (0,)), ((), ()))

    def body(sink_ref, q_ref, kp_ref, kc_ref, vp_ref, vc_ref, do_ref, dq_ref, dk_ref, dv_ref, ds_ref):
        n = pl.program_id(0)

        @pl.when(n == 0)
        def _():
            dk_ref[...] = jnp.zeros_like(dk_ref)
            dv_ref[...] = jnp.zeros_like(dv_ref)
            ds_ref[...] = jnp.zeros_like(ds_ref)

        dist, valid = _attn_geometry(n)
        rows_c = pl.ds(pl.multiple_of(n * ATTN_BLOCK, ATTN_BLOCK), ATTN_BLOCK)
        rows_p = pl.ds(pl.multiple_of(jnp.maximum(n - 1, 0) * ATTN_BLOCK, ATTN_BLOCK), ATTN_BLOCK)
        lane = lax.broadcasted_iota(jnp.int32, ds_ref.shape, 1)
        srow = lax.broadcasted_iota(jnp.int32, ds_ref.shape, 0)
        ds_acc = jnp.zeros(ds_ref.shape, F32)
        for g in range(nkv):
            ks = slice(g * HEAD_DIM, (g + 1) * HEAD_DIM)
            k2 = jnp.concatenate([kp_ref[:, ks], kc_ref[:, ks]], axis=0).astype(BF16)
            v2 = jnp.concatenate([vp_ref[:, ks], vc_ref[:, ks]], axis=0).astype(BF16)
            slope_col = _per_head_column([2.0 ** (-8.0 * (g * GROUP + hh + 1) / nq) for hh in range(GROUP)])
            sink_col = _per_head_column([sink_ref[0, g * GROUP + hh] for hh in range(GROUP)])
            q_s = _stack_heads(q_ref, g)
            do_s = _stack_heads(do_ref, g)
            p, p_sink = _attn_probs(q_s, k2, slope_col, sink_col, dist, valid)
            dp = lax.dot_general(do_s, v2, dn_t, preferred_element_type=F32)
            delta = jnp.sum(p * dp, axis=1, keepdims=True)
            ds = (p * (dp - delta)).astype(BF16)
            sink_part = p_sink * delta
            dq = (jnp.dot(ds, k2, preferred_element_type=F32) * scale).astype(BF16)
            for hh in range(GROUP):
                h = g * GROUP + hh
                blk = slice(hh * ATTN_BLOCK, (hh + 1) * ATTN_BLOCK)
                dq_ref[:, h * HEAD_DIM:(h + 1) * HEAD_DIM] = dq[blk, :]
                ds_acc = ds_acc + jnp.where(jnp.logical_and(lane == h, srow == 0), -jnp.sum(sink_part[blk, :]), 0.0)
            dk2 = lax.dot_general(ds, q_s, dn_r, preferred_element_type=F32) * scale
            dv2 = lax.dot_general(p.astype(BF16), do_s, dn_r, preferred_element_type=F32)
            dk_ref[rows_p, ks] += dk2[:ATTN_BLOCK, :]
            dv_ref[rows_p, ks] += dv2[:ATTN_BLOCK, :]
            dk_ref[rows_c, ks] += dk2[ATTN_BLOCK:, :]
            dv_ref[rows_c, ks] += dv2[ATTN_BLOCK:, :]
        ds_ref[...] += ds_acc

    out_shape = (jax.ShapeDtypeStruct((T, d_attn), BF16), jax.ShapeDtypeStruct((T, d_kv), F32),
                 jax.ShapeDtypeStruct((T, d_kv), F32), jax.ShapeDtypeStruct((8, LANES), F32))
    return _call(
        body, name="attn_bwd", out_shape=out_shape, grid=(T // ATTN_BLOCK,),
        in_specs=[pl.BlockSpec(memory_space=pltpu.SMEM), bq, kp, kc, vp, vc,
                  pl.BlockSpec((ATTN_BLOCK, d_attn), lambda n: (n, 0))],
        out_specs=(pl.BlockSpec((ATTN_BLOCK, d_attn), lambda n: (n, 0)),
                   pl.BlockSpec((T, d_kv), lambda n: (0, 0)), pl.BlockSpec((T, d_kv), lambda n: (0, 0)),
                   pl.BlockSpec((8, LANES), lambda n: (0, 0))),
        scratch_shapes=[], args=(sinks, proj, proj, proj, proj, proj, d_attn_out), sem=("arbitrary",), after=after)[0]


def _rnn_tile(T):
    return _pick(T, (256, 128))


def _rnn_gates(x_ext, cw_ref, cb_ref, wa_ref, wi_ref, ba_ref, bi_ref, lam_ref, tt):
    xs = [pltpu.roll(x_ext, 3 - k, 0)[8:, :] if k < 3 else x_ext[8:, :] for k in range(4)]
    cx = cb_ref[...] + xs[0] * cw_ref[0:1, :]
    for k in range(1, 4):
        cx = cx + xs[k] * cw_ref[k:k + 1, :]
    cxb = cx.astype(BF16)
    r = jax.nn.sigmoid(jnp.dot(cxb, wa_ref[...], preferred_element_type=F32) + ba_ref[...])
    i = jax.nn.sigmoid(jnp.dot(cxb, wi_ref[...], preferred_element_type=F32) + bi_ref[...])
    lam = lam_ref[...]
    sp = jnp.maximum(-lam, 0.0) + jnp.log1p(jnp.exp(-jnp.abs(lam)))
    log_a = -LRU_C * r * sp
    a = jnp.exp(log_a)
    z = 2.0 * log_a
    em1 = jnp.where(z > -1e-2, z * (1.0 + z * (0.5 + z * (1.0 / 6.0 + z * (1.0 / 24.0)))), jnp.exp(z) - 1.0)
    s = jnp.sqrt(-em1)
    return xs, cx, r, i, sp, a, s


def _rnn_specs(T, gw, tt, rx_blk, ry_blk, rev):
    nT = T // tt
    hb = tt // 8

    def tile(t):
        return (nT - 1 - t) if rev else t

    rx = pl.BlockSpec((tt, gw), lambda g, t: (tile(t), rx_blk + g))
    rx_halo = pl.BlockSpec((8, gw), lambda g, t: (jnp.maximum(tile(t) * hb - 1, 0), rx_blk + g))
    ry = pl.BlockSpec((tt, gw), lambda g, t: (tile(t), ry_blk + g))
    cw = pl.BlockSpec((4, gw), lambda g, t: (0, g))
    vec = pl.BlockSpec((1, gw), lambda g, t: (0, g))
    wg = pl.BlockSpec((None, gw, gw), lambda g, t: (g, 0, 0))
    act = pl.BlockSpec((tt, gw), lambda g, t: (tile(t), g))
    act_halo = pl.BlockSpec((8, gw), lambda g, t: (jnp.maximum(tile(t) * hb - 1, 0), g))
    return rx, rx_halo, ry, cw, vec, wg, act, act_halo, tile


def _rnn_fwd(proj, cols, conv_w, conv_b, wa_g, wi_g, ba, bi, lam, rider=None):
    T = proj.shape[0]
    G, gw, _ = wa_g.shape
    d_rnn = G * gw
    tt = _rnn_tile(T)
    rx_off, ry_off = cols
    rx, rx_halo, ry, cw, vec, wg, act, _, _ = _rnn_specs(T, gw, tt, rx_off // gw, ry_off // gw, False)

    def body(rx_ref, rxh_ref, ry_ref, cw_ref, cb_ref, wa_ref, wi_ref, ba_ref, bi_ref, lam_ref,
             b_ref, h_ref, carry):
        t = pl.program_id(1)

        @pl.when(t == 0)
        def _():
            carry[...] = jnp.zeros_like(carry)

        halo = jnp.where(t > 0, rxh_ref[...], 0.0)
        x_ext = jnp.concatenate([halo, rx_ref[...]], axis=0)
        _, cx, _, i, _, a, s = _rnn_gates(x_ext, cw_ref, cb_ref, wa_ref, wi_ref, ba_ref, bi_ref, lam_ref, tt)
        acc_a, acc_b = a, s * (i * cx)
        d = 1
        while d < tt:
            acc_b = acc_a * _shift_down(acc_b, d, 0.0) + acc_b
            acc_a = acc_a * _shift_down(acc_a, d, 1.0)
            d *= 2
        h = acc_b + acc_a * carry[7:8, :]
        carry[...] = h[tt - 8:, :]
        h_ref[...] = h
        b_ref[...] = (h * _gelu(ry_ref[...])).astype(BF16)

    return _call(
        body, name="rnn_fwd",
        out_shape=(jax.ShapeDtypeStruct((T, d_rnn), BF16), jax.ShapeDtypeStruct((T, d_rnn), F32)),
        grid=(G, T // tt),
        in_specs=[rx, rx_halo, ry, cw, vec, wg, wg, vec, vec, vec], out_specs=(act, act),
        scratch_shapes=[pltpu.VMEM((8, gw), F32)],
        args=(proj, proj, proj, conv_w, conv_b, wa_g, wi_g, ba, bi, lam), sem=("parallel", "arbitrary"), rider=rider)


def _rnn_bwd(proj, cols, h_all, d_b, conv_w, conv_b, wa_g, wi_g, ba, bi, lam, rider=None):
    T = proj.shape[0]
    G, gw, _ = wa_g.shape
    d_rnn = G * gw
    tt = _rnn_tile(T)
    nT = T // tt
    rx_off, ry_off = cols
    rx, rx_halo, ry, cw, vec, wg, act, act_halo, _ = _rnn_specs(T, gw, tt, rx_off // gw, ry_off // gw, True)
    dn_t = (((1,), (1,)), ((), ()))
    dn_r = (((0,), (0,)), ((), ()))

    def body(rx_ref, rxh_ref, ry_ref, h_ref, hh_ref, db_ref, cw_ref, cb_ref, wa_ref, wi_ref, ba_ref, bi_ref, lam_ref,
             drx_ref, dry_ref, dcw_ref, dcb_ref, dba_ref, dbi_ref, dlam_ref, dwa_ref, dwi_ref,
             lam_carry, dcx_carry):
        t = pl.program_id(1)
        first_tile = t == nT - 1

        @pl.when(t == 0)
        def _():
            lam_carry[...] = jnp.zeros_like(lam_carry)
            dcx_carry[...] = jnp.zeros_like(dcx_carry)
            dcw_ref[...] = jnp.zeros_like(dcw_ref)
            dcb_ref[...] = jnp.zeros_like(dcb_ref)
            dba_ref[...] = jnp.zeros_like(dba_ref)
            dbi_ref[...] = jnp.zeros_like(dbi_ref)
            dlam_ref[...] = jnp.zeros_like(dlam_ref)
            dwa_ref[...] = jnp.zeros_like(dwa_ref)
            dwi_ref[...] = jnp.zeros_like(dwi_ref)

        halo = jnp.where(first_tile, 0.0, rxh_ref[...])
        x_ext = jnp.concatenate([halo, rx_ref[...]], axis=0)
        xs, cx, r, i, sp, a, s = _rnn_gates(x_ext, cw_ref, cb_ref, wa_ref, wi_ref, ba_ref, bi_ref, lam_ref, tt)
        h = h_ref[...]
        h_halo = jnp.where(first_tile, 0.0, hh_ref[...])
        h_prev = pltpu.roll(jnp.concatenate([h_halo, h], axis=0), 1, 0)[8:, :]
        gel, dgel = _gelu_and_grad(ry_ref[...])
        d_b_t = db_ref[...]
        dry_ref[...] = (d_b_t * h * dgel).astype(BF16)
        dh = d_b_t * gel

        acc_c = _shift_up(a, 1, 1.0)
        acc_l = dh
        d = 1
        while d < tt:
            acc_l = acc_c * _shift_up(acc_l, d, 0.0) + acc_l
            acc_c = acc_c * _shift_up(acc_c, d, 1.0)
            d *= 2
        lam_t = acc_l + acc_c * lam_carry[0:1, :]
        lam_carry[...] = (a * lam_t)[0:8, :]

        icx = i * cx
        d_s = lam_t * icx
        d_i = lam_t * s * cx
        dcx = lam_t * s * i
        d_a = lam_t * h_prev - d_s * (a / s)
        dlog_a = d_a * a
        d_r = dlog_a * (-LRU_C * sp)
        lam = lam_ref[...]
        dlam_ref[...] += jnp.sum(dlog_a * r, axis=0, keepdims=True) * (LRU_C * jax.nn.sigmoid(-lam))
        dpr = d_r * r * (1.0 - r)
        dpi = d_i * i * (1.0 - i)
        dba_ref[...] += jnp.sum(dpr, axis=0, keepdims=True)
        dbi_ref[...] += jnp.sum(dpi, axis=0, keepdims=True)
        cxb = cx.astype(BF16)
        dprb, dpib = dpr.astype(BF16), dpi.astype(BF16)
        dwa_ref[...] += lax.dot_general(cxb, dprb, dn_r, preferred_element_type=F32)
        dwi_ref[...] += lax.dot_general(cxb, dpib, dn_r, preferred_element_type=F32)
        dcx = (dcx + lax.dot_general(dprb, wa_ref[...], dn_t, preferred_element_type=F32)
               + lax.dot_general(dpib, wi_ref[...], dn_t, preferred_element_type=F32))

        dcb_ref[...] += jnp.sum(dcx, axis=0, keepdims=True)
        for k in range(4):
            dcw_ref[k:k + 1, :] += jnp.sum(dcx * xs[k], axis=0, keepdims=True)
        d_ext = jnp.concatenate([dcx, dcx_carry[...]], axis=0)
        drx = dcx * cw_ref[3:4, :]
        for k in range(3):
            drx = drx + pltpu.roll(d_ext, tt + 8 - (3 - k), 0)[:tt, :] * cw_ref[k:k + 1, :]
        drx_ref[...] = drx.astype(BF16)
        dcx_carry[...] = dcx[0:8, :]

    out_shape = (jax.ShapeDtypeStruct((T, d_rnn), BF16), jax.ShapeDtypeStruct((T, d_rnn), BF16),
                 jax.ShapeDtypeStruct((4, d_rnn), F32), jax.ShapeDtypeStruct((1, d_rnn), F32),
                 jax.ShapeDtypeStruct((1, d_rnn), F32), jax.ShapeDtypeStruct((1, d_rnn), F32),
                 jax.ShapeDtypeStruct((1, d_rnn), F32), jax.ShapeDtypeStruct((G, gw, gw), F32),
                 jax.ShapeDtypeStruct((G, gw, gw), F32))
    return _call(
        body, name="rnn_bwd", out_shape=out_shape, grid=(G, nT),
        in_specs=[rx, rx_halo, ry, act, act_halo, act, cw, vec, wg, wg, vec, vec, vec],
        out_specs=(act, act, cw, vec, vec, vec, vec, wg, wg),
        scratch_shapes=[pltpu.VMEM((8, gw), F32), pltpu.VMEM((8, gw), F32)],
        args=(proj, proj, proj, h_all, h_all, d_b, conv_w, conv_b, wa_g, wi_g, ba, bi, lam),
        sem=("parallel", "arbitrary"), rider=rider)


def _merge_fwd(proj, gl_off, b_gate, y_attn, y_rnn, rider=None):
    T, D = y_attn.shape
    tm = _pick(T, (256, 128))
    ct = _pick(math.gcd(gl_off, D), (512, 256, 128))
    oa, orr, nd = gl_off // ct, (gl_off + D) // ct, D // ct

    def body(ga_ref, gr_ref, ba_ref, br_ref, ya_ref, yr_ref, m_ref):
        ga = jax.nn.sigmoid(ga_ref[...] + ba_ref[...])
        gr = jax.nn.sigmoid(gr_ref[...] + br_ref[...])
        m_ref[...] = (ga * ya_ref[...] + gr * yr_ref[...]).astype(BF16)

    blk = pl.BlockSpec((tm, ct), lambda i, j: (i, j))
    (merged,), carried = _call(
        body, name="merge_fwd", out_shape=[jax.ShapeDtypeStruct((T, D), BF16)], grid=(T // tm, nd),
        in_specs=[pl.BlockSpec((tm, ct), lambda i, j: (i, oa + j)), pl.BlockSpec((tm, ct), lambda i, j: (i, orr + j)),
                  pl.BlockSpec((1, ct), lambda i, j: (0, j)), pl.BlockSpec((1, ct), lambda i, j: (0, nd + j)),
                  blk, blk],
        out_specs=[blk], scratch_shapes=[], args=(proj, proj, b_gate, b_gate, y_attn, y_rnn),
        sem=("parallel", "parallel"), rider=rider)
    return merged, carried


def _merge_bwd(proj, gl_off, b_gate, y_attn, y_rnn, d_m):
    T, D = y_attn.shape
    tm = _pick(T, (256, 128))
    ct = _pick(math.gcd(gl_off, D), (512, 256, 128))
    oa, orr, nd = gl_off // ct, (gl_off + D) // ct, D // ct

    def body(ga_ref, gr_ref, ba_ref, br_ref, ya_ref, yr_ref, dm_ref,
             dya_ref, dyr_ref, dga_ref, dgr_ref, dba_ref, dbr_ref):
        i = pl.program_id(1)

        @pl.when(i == 0)
        def _():
            dba_ref[...] = jnp.zeros_like(dba_ref)
            dbr_ref[...] = jnp.zeros_like(dbr_ref)

        ga = jax.nn.sigmoid(ga_ref[...] + ba_ref[...])
        gr = jax.nn.sigmoid(gr_ref[...] + br_ref[...])
        dm = dm_ref[...]
        dya_ref[...] = (dm * ga).astype(BF16)
        dyr_ref[...] = (dm * gr).astype(BF16)
        dga = dm * ya_ref[...] * ga * (1.0 - ga)
        dgr = dm * yr_ref[...] * gr * (1.0 - gr)
        dga_ref[...] = dga.astype(BF16)
        dgr_ref[...] = dgr.astype(BF16)
        dba_ref[...] += jnp.sum(dga, axis=0, keepdims=True)
        dbr_ref[...] += jnp.sum(dgr, axis=0, keepdims=True)

    blk = pl.BlockSpec((tm, ct), lambda j, i: (i, j))
    vec = pl.BlockSpec((1, ct), lambda j, i: (0, j))
    act = jax.ShapeDtypeStruct((T, D), BF16)
    v1 = jax.ShapeDtypeStruct((1, D), F32)
    return pl.pallas_call(
        body, name="merge_bwd", out_shape=(act, act, act, act, v1, v1), grid=(nd, T // tm),
        in_specs=[pl.BlockSpec((tm, ct), lambda j, i: (i, oa + j)), pl.BlockSpec((tm, ct), lambda j, i: (i, orr + j)),
                  vec, pl.BlockSpec((1, ct), lambda j, i: (0, nd + j)), blk, blk, blk],
        out_specs=(blk, blk, blk, blk, vec, vec),
        compiler_params=_cparams(("parallel", "arbitrary")),
    )(proj, proj, b_gate, b_gate, y_attn, y_rnn, d_m)


def _ln_fwd(x_res, delta, g, b, name, rider=None):
    T, D = x_res.shape
    tm = _pick(T, (256, 128))

    def body(x_ref, d_ref, g_ref, b_ref, y_ref, yb_ref, xh_ref, rs_ref):
        z = ALPHA * x_ref[...] + d_ref[...]
        mu = jnp.mean(z, axis=1, keepdims=True)
        zc = z - mu
        var = jnp.mean(zc * zc, axis=1, keepdims=True)
        rstd = lax.rsqrt(var + LN_EPS)
        xh = zc * rstd
        xh_ref[...] = xh
        rs_ref[...] = rstd
        y = xh * g_ref[...] + b_ref[...]
        y_ref[...] = y
        yb_ref[...] = y.astype(BF16)

    row = pl.BlockSpec((tm, D), lambda i: (i, 0))
    vec = pl.BlockSpec((1, D), lambda i: (0, 0))
    return _call(
        body, name=name,
        out_shape=(jax.ShapeDtypeStruct((T, D), F32), jax.ShapeDtypeStruct((T, D), BF16),
                   jax.ShapeDtypeStruct((T, D), F32), jax.ShapeDtypeStruct((T, 1), F32)),
        grid=(T // tm,), in_specs=[row, row, vec, vec],
        out_specs=(row, row, row, pl.BlockSpec((tm, 1), lambda i: (i, 0))),
        scratch_shapes=[], args=(x_res, delta, g, b), sem=("parallel",), rider=rider)


def _ln_bwd_rows(dy, xh, rstd, g):
    dxh = dy * g
    m1 = jnp.mean(dxh, axis=1, keepdims=True)
    m2 = jnp.mean(dxh * xh, axis=1, keepdims=True)
    return rstd * (dxh - m1 - xh * m2)


def _ln_loss_bwd(x_res, delta, g, b, target):
    T, D = x_res.shape
    tm = _pick(T, (256, 128))

    def body(x_ref, d_ref, g_ref, b_ref, t_ref, dz_ref, dzb_ref, loss_ref, dg_ref, db_ref):
        i = pl.program_id(0)

        @pl.when(i == 0)
        def _():
            loss_ref[...] = jnp.zeros_like(loss_ref)
            dg_ref[...] = jnp.zeros_like(dg_ref)
            db_ref[...] = jnp.zeros_like(db_ref)

        z = ALPHA * x_ref[...] + d_ref[...]
        mu = jnp.mean(z, axis=1, keepdims=True)
        zc = z - mu
        var = jnp.mean(zc * zc, axis=1, keepdims=True)
        rstd = lax.rsqrt(var + LN_EPS)
        xh = zc * rstd
        gv = g_ref[...]
        err = xh * gv + b_ref[...] - t_ref[...]
        loss_ref[...] += 0.5 * jnp.sum(jnp.mean(err * err, axis=1, keepdims=True))
        dy = err * (1.0 / D)
        dg_ref[...] += jnp.sum(dy * xh, axis=0, keepdims=True)
        db_ref[...] += jnp.sum(dy, axis=0, keepdims=True)
        dz = _ln_bwd_rows(dy, xh, rstd, gv)
        dz_ref[...] = dz
        dzb_ref[...] = dz.astype(BF16)

    row = pl.BlockSpec((tm, D), lambda i: (i, 0))
    vec = pl.BlockSpec((1, D), lambda i: (0, 0))
    return pl.pallas_call(
        body, name="ln2_loss_bwd",
        out_shape=(jax.ShapeDtypeStruct((T, D), F32), jax.ShapeDtypeStruct((T, D), BF16),
                   jax.ShapeDtypeStruct((8, LANES), F32),
                   jax.ShapeDtypeStruct((1, D), F32), jax.ShapeDtypeStruct((1, D), F32)),
        grid=(T // tm,), in_specs=[row, row, vec, vec, row],
        out_specs=(row, row, pl.BlockSpec((8, LANES), lambda i: (0, 0)), vec, vec),
        compiler_params=_cparams(("arbitrary",)),
    )(x_res, delta, g, b, target)


def _ln_bwd(dy, xh, rstd, g):
    T, D = dy.shape
    tm = _pick(T, (256, 128))

    def body(dy_ref, xh_ref, rs_ref, g_ref, dz_ref, dzb_ref, dg_ref, db_ref):
        i = pl.program_id(0)

        @pl.when(i == 0)
        def _():
            dg_ref[...] = jnp.zeros_like(dg_ref)
            db_ref[...] = jnp.zeros_like(db_ref)

        dyv, xhv = dy_ref[...], xh_ref[...]
        dg_ref[...] += jnp.sum(dyv * xhv, axis=0, keepdims=True)
        db_ref[...] += jnp.sum(dyv, axis=0, keepdims=True)
        dz = _ln_bwd_rows(dyv, xhv, rs_ref[...], g_ref[...])
        dz_ref[...] = dz
        dzb_ref[...] = dz.astype(BF16)

    row = pl.BlockSpec((tm, D), lambda i: (i, 0))
    vec = pl.BlockSpec((1, D), lambda i: (0, 0))
    return pl.pallas_call(
        body, name="ln1_bwd",
        out_shape=(jax.ShapeDtypeStruct((T, D), F32), jax.ShapeDtypeStruct((T, D), BF16),
                   jax.ShapeDtypeStruct((1, D), F32), jax.ShapeDtypeStruct((1, D), F32)),
        grid=(T // tm,), in_specs=[row, row, pl.BlockSpec((tm, 1), lambda i: (i, 0)), vec],
        out_specs=(row, row, vec, vec), compiler_params=_cparams(("arbitrary",)),
    )(dy, xh, rstd, g)


def _ffn_col_tile(T, d_ff):
    return _pick(d_ff, (256, 128)) if T >= 1024 else _pick(d_ff, (512, 256, 128))


def _ffn_gate(gp, cw_ref, cb_ref):
    return (cb_ref[...] + gp * cw_ref[2:3, :] + _shift_down(gp, 1) * cw_ref[1:2, :]
            + _shift_down(gp, 2) * cw_ref[0:1, :])


def _ffn_fwd(up, gpre, conv_w, conv_b, rider=None):
    T, d_ff = up.shape
    ct = _ffn_col_tile(T, d_ff)

    def body(up_ref, gp_ref, cw_ref, cb_ref, f_ref):
        gate = _ffn_gate(gp_ref[...], cw_ref, cb_ref)
        f_ref[...] = (_gelu(gate) * up_ref[...]).astype(BF16)

    col = pl.BlockSpec((T, ct), lambda j: (0, j))
    (f,), carried = _call(
        body, name="ffn_act_fwd", out_shape=[jax.ShapeDtypeStruct((T, d_ff), BF16)], grid=(d_ff // ct,),
        in_specs=[col, col, pl.BlockSpec((3, ct), lambda j: (0, j)), pl.BlockSpec((1, ct), lambda j: (0, j))],
        out_specs=[col], scratch_shapes=[], args=(up, gpre, conv_w, conv_b), sem=("parallel",), rider=rider)
    return f, carried


def _ffn_bwd(up, gpre, conv_w, conv_b, d_f, after=None):
    T, d_ff = up.shape
    ct = _ffn_col_tile(T, d_ff)

    def body(up_ref, gp_ref, cw_ref, cb_ref, df_ref, dup_ref, dgp_ref, dcw_ref, dcb_ref):
        gp = gp_ref[...]
        gate = _ffn_gate(gp, cw_ref, cb_ref)
        gel, dgel = _gelu_and_grad(gate)
        df = df_ref[...]
        dup_ref[...] = (df * gel).astype(BF16)
        dgate = df * up_ref[...] * dgel
        dcb_ref[...] = jnp.sum(dgate, axis=0, keepdims=True)
        dcw_ref[2:3, :] = jnp.sum(dgate * gp, axis=0, keepdims=True)
        dcw_ref[1:2, :] = jnp.sum(dgate * _shift_down(gp, 1), axis=0, keepdims=True)
        dcw_ref[0:1, :] = jnp.sum(dgate * _shift_down(gp, 2), axis=0, keepdims=True)
        dgp = (dgate * cw_ref[2:3, :] + _shift_up(dgate, 1) * cw_ref[1:2, :]
               + _shift_up(dgate, 2) * cw_ref[0:1, :])
        dgp_ref[...] = dgp.astype(BF16)

    col = pl.BlockSpec((T, ct), lambda j: (0, j))
    w3 = pl.BlockSpec((3, ct), lambda j: (0, j))
    v1 = pl.BlockSpec((1, ct), lambda j: (0, j))
    return _call(
        body, name="ffn_act_bwd",
        out_shape=(jax.ShapeDtypeStruct((T, d_ff), BF16), jax.ShapeDtypeStruct((T, d_ff), BF16),
                   jax.ShapeDtypeStruct((3, d_ff), F32), jax.ShapeDtypeStruct((1, d_ff), F32)),
        grid=(d_ff // ct,), in_specs=[col, col, w3, v1, col], out_specs=(col, col, w3, v1),
        scratch_shapes=[], args=(up, gpre, conv_w, conv_b, d_f), sem=("parallel",), after=after)[0]


def _adamw(w, g, m, v, name, after=None):
    R, C = w.shape
    tr = _row_tile(R, C * 4, 8, budget=1280 * 1024)
    c1 = 1.0 / (1.0 - ADAM_B1 ** ADAM_STEP)
    c2 = 1.0 / (1.0 - ADAM_B2 ** ADAM_STEP)

    def body(w_ref, g_ref, m_ref, v_ref, go_ref, d_ref, nm_ref, nv_ref):
        gv = g_ref[...]
        go_ref[...] = gv
        nm = ADAM_B1 * m_ref[...] + (1.0 - ADAM_B1) * gv
        nv = ADAM_B2 * v_ref[...] + (1.0 - ADAM_B2) * (gv * gv)
        nm_ref[...] = nm
        nv_ref[...] = nv
        d_ref[...] = -ADAM_LR * ((nm * c1) / (jnp.sqrt(nv * c2) + ADAM_EPS) + ADAM_WD * w_ref[...])

    blk = pl.BlockSpec((tr, C), lambda r: (r, 0))
    sh = jax.ShapeDtypeStruct((R, C), F32)
    return _call(body, name=name, out_shape=(sh,) * 4, grid=(R // tr,), in_specs=[blk] * 4, out_specs=(blk,) * 4,
                 scratch_shapes=[], args=(w, g, m, v), sem=("parallel",), after=after)[0]


def _group_blocks(w_blocks, per):
    nb, bw, _ = w_blocks.shape
    G = nb // per
    w4 = w_blocks.reshape(G, per, bw, bw)
    rows = []
    for p in range(per):
        parts = [w4[:, p] if q == p else jnp.zeros((G, bw, bw), w_blocks.dtype) for q in range(per)]
        rows.append(jnp.concatenate(parts, axis=2))
    return jnp.concatenate(rows, axis=1)


def _ungroup_blocks(w_groups, per):
    G, gw, _ = w_groups.shape
    bw = gw // per
    blocks = [w_groups[:, p * bw:(p + 1) * bw, p * bw:(p + 1) * bw] for p in range(per)]
    return jnp.stack(blocks, axis=1).reshape(G * per, bw, bw)


def _pack(parts):
    flat = jnp.concatenate([p.reshape(-1).astype(F32) for p in parts])
    n = flat.shape[0]
    rows = -(-n // LANES)
    rows = -(-rows // PACK_ROW_MULT) * PACK_ROW_MULT
    flat = jnp.pad(flat, (0, rows * LANES - n))
    return flat.reshape(rows, LANES)


def _unpack(packed, shapes):
    flat = packed.reshape(-1)
    out, off = [], 0
    for s in shapes:
        n = math.prod(s)
        out.append(flat[off:off + n].reshape(s))
        off += n
    return out


def kernel(x, w_in, b_gate, rnn_conv_w, rnn_conv_b, lru_wa, lru_ba, lru_wi, lru_bi, lru_lambda, attn_sinks, w_attn_proj, w_rnn_proj, w_out, ln1_g, ln1_b, ffn_w_up, ffn_w_gate, ffn_conv_w, ffn_conv_b, ffn_w_down, ln2_g, ln2_b, loss_target, m_w_in, m_b_gate, m_rnn_conv_w, m_rnn_conv_b, m_lru_wa, m_lru_ba, m_lru_wi, m_lru_bi, m_lru_lambda, m_attn_sinks, m_w_attn_proj, m_w_rnn_proj, m_w_out, m_ln1_g, m_ln1_b, m_ffn_w_up, m_ffn_w_gate, m_ffn_conv_w, m_ffn_conv_b, m_ffn_w_down, m_ln2_g, m_ln2_b, v_w_in, v_b_gate, v_rnn_conv_w, v_rnn_conv_b, v_lru_wa, v_lru_ba, v_lru_wi, v_lru_bi, v_lru_lambda, v_attn_sinks, v_w_attn_proj, v_w_rnn_proj, v_w_out, v_ln1_g, v_ln1_b, v_ffn_w_up, v_ffn_w_gate, v_ffn_conv_w, v_ffn_conv_b, v_ffn_w_down, v_ln2_g, v_ln2_b):
    weights = dict(w_in=w_in, b_gate=b_gate, rnn_conv_w=rnn_conv_w, rnn_conv_b=rnn_conv_b, lru_wa=lru_wa,
                   lru_ba=lru_ba, lru_wi=lru_wi, lru_bi=lru_bi, lru_lambda=lru_lambda, attn_sinks=attn_sinks,
                   w_attn_proj=w_attn_proj, w_rnn_proj=w_rnn_proj, w_out=w_out, ln1_g=ln1_g, ln1_b=ln1_b,
                   ffn_w_up=ffn_w_up, ffn_w_gate=ffn_w_gate, ffn_conv_w=ffn_conv_w, ffn_conv_b=ffn_conv_b,
                   ffn_w_down=ffn_w_down, ln2_g=ln2_g, ln2_b=ln2_b)
    m_in = dict(w_in=m_w_in, b_gate=m_b_gate, rnn_conv_w=m_rnn_conv_w, rnn_conv_b=m_rnn_conv_b, lru_wa=m_lru_wa,
                lru_ba=m_lru_ba, lru_wi=m_lru_wi, lru_bi=m_lru_bi, lru_lambda=m_lru_lambda, attn_sinks=m_attn_sinks,
                w_attn_proj=m_w_attn_proj, w_rnn_proj=m_w_rnn_proj, w_out=m_w_out, ln1_g=m_ln1_g, ln1_b=m_ln1_b,
                ffn_w_up=m_ffn_w_up, ffn_w_gate=m_ffn_w_gate, ffn_conv_w=m_ffn_conv_w, ffn_conv_b=m_ffn_conv_b,
                ffn_w_down=m_ffn_w_down, ln2_g=m_ln2_g, ln2_b=m_ln2_b)
    v_in = dict(w_in=v_w_in, b_gate=v_b_gate, rnn_conv_w=v_rnn_conv_w, rnn_conv_b=v_rnn_conv_b, lru_wa=v_lru_wa,
                lru_ba=v_lru_ba, lru_wi=v_lru_wi, lru_bi=v_lru_bi, lru_lambda=v_lru_lambda, attn_sinks=v_attn_sinks,
                w_attn_proj=v_w_attn_proj, w_rnn_proj=v_w_rnn_proj, w_out=v_w_out, ln1_g=v_ln1_g, ln1_b=v_ln1_b,
                ffn_w_up=v_ffn_w_up, ffn_w_gate=v_ffn_w_gate, ffn_conv_w=v_ffn_conv_w, ffn_conv_b=v_ffn_conv_b,
                ffn_w_down=v_ffn_w_down, ln2_g=v_ln2_g, ln2_b=v_ln2_b)
    order = list(weights)

    assert x.shape[0] == 1 and w_in.shape[0] == 1, "one sequence per device, depth 1"
    T, D = x.shape[1], x.shape[2]
    nq = attn_sinks.shape[-1]
    nkv = nq // GROUP
    d_attn, d_kv = nq * HEAD_DIM, nkv * HEAD_DIM
    d_rnn = rnn_conv_b.shape[-1]
    d_ff = ffn_conv_b.shape[-1]
    n_blocks, bw = lru_wa.shape[1], lru_wa.shape[2]
    per = (bw * LANES // math.gcd(bw, LANES)) // bw
    gw = per * bw
    assert n_blocks % per == 0 and d_rnn == n_blocks * bw
    q_off, k_off, v_off = 0, d_attn, d_attn + d_kv
    rx_off = d_attn + 2 * d_kv
    ry_off = rx_off + d_rnn
    gl_off = ry_off + d_rnn
    d_in = gl_off + 2 * D
    assert w_in.shape[-1] * N_SHARDS == d_in
    assert k_off % d_kv == 0 and rx_off % gw == 0 and T % ATTN_BLOCK == 0

    xi, yi, ci = lax.axis_index("x"), lax.axis_index("y"), lax.axis_index("c")
    j_me = 2 * xi + yi
    jc_arr = jnp.stack([j_me, ci]).astype(jnp.int32)

    x0 = x[0]
    tgt = loss_target[0]
    big = ["w_in", "w_attn_proj", "w_rnn_proj", "w_out", "ffn_w_up", "ffn_w_gate", "ffn_w_down"]
    near, diag = (0, 1), (2,)
    order_arr = jnp.stack([j_me, j_me ^ 2, j_me ^ 1, j_me ^ 3]).astype(jnp.int32)

    rcw_s, fcw_s = _all_gather_small([rnn_conv_w[0], ffn_conv_w[0]])
    rcw = jnp.concatenate([rcw_s[j] for j in range(N_SHARDS)], axis=1)
    fcw = jnp.concatenate([fcw_s[j] for j in range(N_SHARDS)], axis=1)

    own = {"w_in": _cast_bf16_into_slot(w_in[0], jc_arr, "cast_w_in", fcw_s)}
    in_near = _gather_step(own["w_in"], fcw_s, "gather_start_w_in", start_d2d=False, relations=near)
    last = in_near[2]
    for n in big[1:]:
        own[n] = last = _cast_bf16_into_slot(weights[n][0], jc_arr, "cast_" + n, last)
    x0b = _cast_bf16(x0, "cast_x", last)

    wa_g = _group_blocks(lru_wa[0], per).astype(BF16)
    wi_g = _group_blocks(lru_wi[0], per).astype(BF16)

    proj = _mm_shards(x0b, in_near[1], order_arr, [0], "mm_proj_own", x0b)
    d2d_sems, relay_sems, buf, tok = _gather_relay_step(in_near[1], proj, "gather_forward_w_in_near", in_near[0])
    w_in_s = _gather_step(buf, tok, "gather_finish_w_in_near", sems_in=d2d_sems, relations=near)
    proj = _mm_shards(x0b, w_in_s, order_arr, [1, 2], "mm_proj_near", w_in_s, out=proj)
    d2d_sems, buf, tok = _gather_relay_step(w_in_s, proj, "gather_forward_w_in_diag", None, relay_in=relay_sems)
    w_in_s = _gather_step(buf, tok, "gather_finish_w_in_diag", sems_in=d2d_sems, relations=diag)
    proj = _mm_shards(x0b, w_in_s, order_arr, [3], "mm_proj_diag", w_in_s, out=proj)
    started, last = _gather_start_all([own[n] for n in big[1:]], proj, "gather_start_all")
    ici = dict(zip(big[1:], started))

    def forward_halves(n, after):
        sems, buf = ici[n]
        return _gather_step(buf, after, "gather_forward_" + n, sems_in=sems, start_d2d=True)

    def gathered(d2d, after, n):
        sems, buf, _ = d2d
        return _gather_step(buf, after, "gather_finish_" + n, sems_in=sems)

    a_out = _attn_fwd(proj, attn_sinks, nq, (q_off, k_off, v_off), after=last)
    fw_ap = forward_halves("w_attn_proj", a_out)
    (b_out, h_all), _ = _rnn_fwd(proj, (rx_off, ry_off), rcw, rnn_conv_b, wa_g, wi_g, lru_ba, lru_bi, lru_lambda)
    fw_rp = forward_halves("w_rnn_proj", b_out)
    w_ap = gathered(fw_ap, b_out, "w_attn_proj").reshape(d_attn, D)
    y_attn = _mm(a_out, w_ap, name="mm_attn_proj")
    fw_o = forward_halves("w_out", y_attn)
    w_rp = gathered(fw_rp, y_attn, "w_rnn_proj").reshape(d_rnn, D)
    y_rnn = _mm(b_out, w_rp, name="mm_rnn_proj")
    merged, _ = _merge_fwd(proj, gl_off, b_gate, y_attn, y_rnn)
    w_o = gathered(fw_o, merged, "w_out").reshape(D, D)
    mix = _mm(merged, w_o, name="mm_out")
    fw_up = forward_halves("ffn_w_up", mix)
    (x1, x1b, xh1, rstd1), _ = _ln_fwd(x0, mix, ln1_g, ln1_b, "ln1_fwd")
    w_up_s = gathered(fw_up, x1b, "ffn_w_up")
    up = _mm(x1b, w_up_s, name="mm_up", b_shards=N_SHARDS)
    fw_gate = forward_halves("ffn_w_gate", up)
    w_gate_s = gathered(fw_gate, fw_gate[2], "ffn_w_gate")
    gpre = _mm(x1b, w_gate_s, name="mm_gate", b_shards=N_SHARDS)
    f_act, _ = _ffn_fwd(up, gpre, fcw, ffn_conv_b)
    fw_dn = forward_halves("ffn_w_down", f_act)
    w_dn = gathered(fw_dn, fw_dn[2], "ffn_w_down").reshape(d_ff, D)
    f_out = _mm(f_act, w_dn, name="mm_down")
    dz2, dz2b, loss_acc, dg2, db2 = _ln_loss_bwd(x1, f_out, ln2_g, ln2_b, tgt)

    def pair_sums(arrs, from_sibling, names):
        return [_pair_sum(g, la, jc_arr, "pair_sum_" + n) for g, la, n in zip(arrs, from_sibling, names)]

    def shard_sums(parts, landed, names):
        return [_shard_sum(cp, lb, jc_arr, "shard_sum_" + n) for cp, lb, n in zip(parts, landed, names)]

    halves = {}
    g_down = _mm(f_act, dz2b, name="mm_d_w_down", ta=True, out_dtype=BF16)
    g1 = [g_down.reshape(N_SHARDS, d_ff // N_SHARDS, D)]
    d_f, sib1 = _mm(dz2b, w_dn, name="mm_d_f", tb=True, rider=_pair_rider(g1))
    sent1 = _shard_exchange_start(pair_sums(g1, sib1, ["ffn_w_down"]), "shard_exchange_start_down")
    dup, dgp, d_fcw, d_fcb = _ffn_bwd(up, gpre, fcw, ffn_conv_b, d_f, after=sent1[4])
    g_up = _mm(x1b, dup, name="mm_d_w_up", ta=True, out_dtype=BF16, out_shards=N_SHARDS)
    g_gate = _mm(x1b, dgp, name="mm_d_w_gate", ta=True, out_dtype=BF16, out_shards=N_SHARDS)
    g2 = [g_up, g_gate]
    dx1_a, sib2 = _mm(dup, w_up_s, name="mm_dx1_up", tb=True, b_shards=N_SHARDS, adds=((ALPHA, dz2),),
                      rider=_pair_rider(g2))
    halves["ffn_w_down"], = shard_sums(*_shard_exchange_wait(sent1, dx1_a, "shard_exchange_wait_down"),
                                       ["ffn_w_down"])
    sent2 = _shard_exchange_start(pair_sums(g2, sib2, ["ffn_w_up", "ffn_w_gate"]), "shard_exchange_start_up_gate")
    dx1 = _mm(dgp, w_gate_s, name="mm_dx1_gate", tb=True, b_shards=N_SHARDS, adds=((1.0, dx1_a),), after=sent2[4])
    dz1, dz1b, dg1, db1 = _ln_bwd(dx1, xh1, rstd1, ln1_g)
    g_out = _mm(merged, dz1b, name="mm_d_w_out", ta=True, out_dtype=BF16)
    d_m = _mm(dz1b, w_o, name="mm_d_merged", tb=True)
    dya, dyr, dgl_a, dgl_r, dbg_a, dbg_r = _merge_bwd(proj, gl_off, b_gate, y_attn, y_rnn, d_m)
    g_ap = _mm(a_out, dya, name="mm_d_w_attn_proj", ta=True, out_dtype=BF16)
    g_rp = _mm(b_out, dyr, name="mm_d_w_rnn_proj", ta=True, out_dtype=BF16)
    names3 = ["w_out", "w_attn_proj", "w_rnn_proj"]
    g3 = [g_out.reshape(N_SHARDS, D // N_SHARDS, D), g_ap.reshape(N_SHARDS, d_attn // N_SHARDS, D),
          g_rp.reshape(N_SHARDS, d_rnn // N_SHARDS, D)]
    d_a = _mm(dya, w_ap, name="mm_d_attn", tb=True)
    d_b, sib3 = _mm(dyr, w_rp, name="mm_d_rnn", tb=True, rider=_pair_rider(g3))
    sent3 = _shard_exchange_start(pair_sums(g3, sib3, names3), "shard_exchange_start_mixers")
    dq, dk, dv, dsink = _attn_bwd(proj, d_a, attn_sinks, nq, (q_off, k_off, v_off), after=sent3[4])
    (drx, dry, d_rcw, d_rcb, d_ba, d_bi, d_lam, d_wa_g, d_wi_g), _ = _rnn_bwd(
        proj, (rx_off, ry_off), h_all, d_b, rcw, rnn_conv_b, wa_g, wi_g, lru_ba, lru_bi, lru_lambda)
    halves["ffn_w_up"], halves["ffn_w_gate"] = shard_sums(
        *_shard_exchange_wait(sent2, drx, "shard_exchange_wait_up_gate"), ["ffn_w_up", "ffn_w_gate"])
    d_proj = jnp.concatenate([dq, dk.astype(BF16), dv.astype(BF16), drx, dry, dgl_a, dgl_r], axis=1)
    ffn_names = ["ffn_w_down", "ffn_w_up", "ffn_w_gate"]
    g_in, shared_ffn = _mm(x0b, d_proj, name="mm_d_w_in", ta=True, out_dtype=BF16, out_shards=N_SHARDS,
                           rider=_share_rider([halves[n] for n in ffn_names]))
    halves["w_out"], halves["w_attn_proj"], halves["w_rnn_proj"] = shard_sums(
        *_shard_exchange_wait(sent3, g_in, "shard_exchange_wait_mixers"), names3)

    small_parts = [
        ("loss", loss_acc[0:1, 0:1]),
        ("b_gate", jnp.concatenate([dbg_a, dbg_r], axis=1)),
        ("rnn_conv_w", d_rcw), ("rnn_conv_b", d_rcb),
        ("lru_wa", _ungroup_blocks(d_wa_g, per)), ("lru_ba", d_ba),
        ("lru_wi", _ungroup_blocks(d_wi_g, per)), ("lru_bi", d_bi), ("lru_lambda", d_lam),
        ("attn_sinks", dsink[0:1, 0:nq]),
        ("ln1_g", dg1), ("ln1_b", db1),
        ("ffn_conv_w", d_fcw), ("ffn_conv_b", d_fcb),
        ("ln2_g", dg2), ("ln2_b", db2),
    ]
    packed = _pack([p for _, p in small_parts])
    rs = packed.shape[0]

    def whole(g):
        return g.reshape(2 * g.shape[1], g.shape[2])

    grads = {n: whole(g) for n, g in zip(ffn_names, shared_ffn)}
    out_g, out_d, out_m, out_v = {}, {}, {}, {}

    def adamw(n, after=None):
        shape = weights[n].shape
        two_d = (math.prod(shape[:-1]), shape[-1])
        g2, d2, m2, v2 = _adamw(weights[n].reshape(two_d), grads[n].reshape(two_d), m_in[n].reshape(two_d),
                                v_in[n].reshape(two_d), "adamw_" + n, after=after)
        out_g[n], out_d[n] = g2.reshape(shape), d2.reshape(shape)
        out_m[n], out_v[n] = m2.reshape(shape), v2.reshape(shape)

    g4 = [g_in, packed.reshape(N_SHARDS, rs // N_SHARDS, LANES)]
    sib4 = _run_rider(_pair_rider(g4), "pair_exchange_in_small")
    part4 = pair_sums(g4, sib4, ["w_in", "small"])
    grad_x, (lb_in, lb_small, *shared_mix) = _mm(
        d_proj, w_in_s, name="mm_d_x", tb=True, b_shards=N_SHARDS, adds=((ALPHA, dz1),),
        rider=_join_riders(_shard_exchange_rider(part4, _atoms([0], near) + _atoms([1])),
                           _share_rider([halves[n] for n in names3])))
    grads.update({n: whole(g) for n, g in zip(names3, shared_mix)})
    sent5 = _shard_exchange_start(part4[:1], "shard_exchange_start_in_diag", relations=diag, lands=[lb_in])
    for n in ffn_names + names3:
        adamw(n, after=sent5[4])
    (part_in,), (lb_in,) = _shard_exchange_wait(sent5, out_d[names3[-1]], "shard_exchange_wait_in_diag")
    part_small = part4[1]
    halves["w_in"], = shard_sums([part_in], [lb_in], ["w_in"])
    eighths = _shard_sum(part_small, lb_small, jc_arr, "shard_sum_small", all_slots=True)
    shared_in, reduced = _run_rider(_share_rider([halves["w_in"]], eighths), "share_in_small")
    grads["w_in"] = whole(shared_in)
    reduced = reduced.reshape(rs, LANES)
    small = dict(zip([n for n, _ in small_parts], _unpack(reduced, [p.shape for _, p in small_parts])))
    loss = small.pop("loss").reshape(())
    rcw_n = d_rnn // N_SHARDS
    fcw_n = d_ff // N_SHARDS
    small["rnn_conv_w"] = lax.dynamic_slice(small["rnn_conv_w"], (0, j_me * rcw_n), (4, rcw_n))
    small["ffn_conv_w"] = lax.dynamic_slice(small["ffn_conv_w"], (0, j_me * fcw_n), (3, fcw_n))
    for n, g in small.items():
        grads[n] = g

    for n in order:
        if n not in out_g:
            adamw(n)

    return (loss, grad_x.reshape(x.shape), *[out_g[n] for n in order], *[out_d[n] for n in order],
            *[out_m[n] for n in order], *[out_v[n] for n in order])
```

```python
import functools
import math

import jax
import jax.numpy as jnp
from jax import lax
from jax.experimental import pallas as pl
from jax.experimental.pallas import tpu as pltpu

F32 = jnp.float32
BF16 = jnp.bfloat16
MESH = pl.DeviceIdType.MESH

HEAD_DIM = 64
GROUP = 8
ATTN_BLOCK = 128
LRU_C = 8.0
LN_EPS = 1e-5
ALPHA = 2.0 ** 0.25
LANES = 128
N_SHARDS = 4
N_DEV = 8
VMEM_LIMIT = 56 * 1024 * 1024
MM_VMEM_BUDGET = 40 * 1024 * 1024
MM_MAX_TILE = 3072
PACK_ROW_MULT = 8 * 64
NEG = -1e30

ADAM_LR, ADAM_B1, ADAM_B2, ADAM_EPS, ADAM_WD, ADAM_STEP = 0.001, 0.9, 0.999, 1e-08, 0.01, 10

GELU_C = math.sqrt(2.0 / math.pi)
GELU_A = 0.044715


def _cparams(sem=None):
    kw = dict(vmem_limit_bytes=VMEM_LIMIT)
    if sem is not None:
        kw["dimension_semantics"] = sem
    return pltpu.CompilerParams(**kw)


def _pick(n, prefs):
    for p in prefs:
        if n % p == 0:
            return p
    return n


def _row_tile(rows, row_bytes, mult, budget=2 * 1024 * 1024):
    best = None
    for d in range(mult, rows + 1, mult):
        if rows % d == 0 and d * row_bytes <= budget:
            best = d
    return best if best is not None else rows


def _gelu(x):
    return 0.5 * x * (1.0 + jnp.tanh(GELU_C * (x + GELU_A * x * x * x)))


def _gelu_and_grad(x):
    t = jnp.tanh(GELU_C * (x + GELU_A * x * x * x))
    g = 0.5 * x * (1.0 + t)
    dg = 0.5 * (1.0 + t) + 0.5 * x * (1.0 - t * t) * GELU_C * (1.0 + 3.0 * GELU_A * x * x)
    return g, dg


def _shift_down(x, s, fill=0.0):
    row = lax.broadcasted_iota(jnp.int32, x.shape, 0)
    return jnp.where(row >= s, pltpu.roll(x, s, 0), fill)


def _shift_up(x, s, fill=0.0):
    n = x.shape[0]
    row = lax.broadcasted_iota(jnp.int32, x.shape, 0)
    return jnp.where(row < n - s, pltpu.roll(x, n - s, 0), fill)


def _mm(a, b, *, name, ta=False, tb=False, out_dtype=F32, adds=(), b_shards=1, out_shards=1,
        tm=None, tn=None, tk=None, rider=None, after=None):
    if ta:
        K, M = a.shape
    else:
        M, K = a.shape
    if b_shards > 1:
        n_sh = b.shape[-1]
        if tb:
            N = b.shape[1]
            assert b_shards * n_sh == K
        else:
            N = b_shards * n_sh
            assert b.shape[1] == K
    else:
        n_sh = None
        if tb:
            N = b.shape[0]
            assert b.shape[1] == K
        else:
            N = b.shape[1]
            assert b.shape[0] == K
    wide = (1024, 1536, 1280, 768, 640, 512, 256, 128)
    if tn is None:
        if b_shards > 1 and not tb:
            tn = n_sh if n_sh <= MM_MAX_TILE else _pick(n_sh, wide)
        elif out_shards > 1:
            tn = N // out_shards if N // out_shards <= MM_MAX_TILE else _pick(N // out_shards, wide)
        else:
            tn = _pick(N, wide)
    if tk is None:
        if b_shards > 1 and tb:
            tk = n_sh if n_sh <= MM_MAX_TILE else _pick(n_sh, wide)
        else:
            tk = K if K <= MM_MAX_TILE else _pick(K, (2048,) + wide)
    assert N % tn == 0 and K % tk == 0, (name, M, N, K, tn, tk)
    nk = K // tk
    n_add = len(adds)
    sa, sb, so = a.dtype.itemsize, b.dtype.itemsize, jnp.dtype(out_dtype).itemsize

    def vmem_bytes(tm_):
        return (2 * (tm_ * tk * sa + tk * tn * sb + tm_ * tn * so + n_add * tm_ * tn * 4)
                + (tm_ * tn * 4 if nk > 1 else 0))

    if tm is None:
        tm = _pick(M, (1024, 512, 256, 128)) if nk > 1 else _pick(M, (512, 256, 128))
        while vmem_bytes(tm) > MM_VMEM_BUDGET and tm % 256 == 0:
            tm //= 2
    assert M % tm == 0, (name, M, tm)
    b_outer = b.size * sb >= a.size * sa

    def ij(g0, g1):
        return (g1, g0) if b_outer else (g0, g1)

    def amap(g0, g1, k):
        i, _ = ij(g0, g1)
        return (k, i) if ta else (i, k)

    def bmap(g0, g1, k):
        _, j = ij(g0, g1)
        if b_shards > 1 and not tb:
            per = n_sh // tn
            return (j // per, k, j % per)
        if b_shards > 1 and tb:
            per = n_sh // tk
            return (k // per, j, k % per)
        return (j, k) if tb else (k, j)

    def omap(g0, g1, k):
        i, j = ij(g0, g1)
        if out_shards > 1:
            per_o = (N // out_shards) // tn
            return (j // per_o, i, j % per_o)
        return (i, j)

    a_spec = pl.BlockSpec((tk, tm) if ta else (tm, tk), amap)
    if b_shards > 1:
        b_spec = pl.BlockSpec((None, tn, tk) if tb else (None, tk, tn), bmap)
    else:
        b_spec = pl.BlockSpec((tn, tk) if tb else (tk, tn), bmap)
    add_specs = [pl.BlockSpec((tm, tn), lambda g0, g1, k: ij(g0, g1)) for _ in adds]
    if out_shards > 1:
        out_spec = pl.BlockSpec((None, tm, tn), omap)
        out_shape = jax.ShapeDtypeStruct((out_shards, M, N // out_shards), out_dtype)
    else:
        out_spec = pl.BlockSpec((tm, tn), omap)
        out_shape = jax.ShapeDtypeStruct((M, N), out_dtype)

    if ta:
        dims = (((0,), (0,)), ((), ()))
    elif tb:
        dims = (((1,), (1,)), ((), ()))
    else:
        dims = (((1,), (0,)), ((), ()))
    scales = tuple(s for s, _ in adds)

    def finish(r, add_refs, o_ref):
        for s, ref in zip(scales, add_refs):
            r = r + s * ref[...].astype(F32)
        o_ref[...] = r.astype(out_dtype)

    def body(a_ref, b_ref, *rest):
        add_refs = rest[:n_add]
        o_ref = rest[n_add]
        part = lax.dot_general(a_ref[...].astype(BF16), b_ref[...].astype(BF16), dims, preferred_element_type=F32)
        if nk == 1:
            finish(part, add_refs, o_ref)
            return
        acc = rest[n_add + 1]
        k = pl.program_id(2)

        @pl.when(k == 0)
        def _():
            acc[...] = part

        @pl.when(k > 0)
        def _():
            acc[...] += part

        @pl.when(k == nk - 1)
        def _():
            finish(acc[...], add_refs, o_ref)

    grid = (N // tn, M // tm, nk) if b_outer else (M // tm, N // tn, nk)
    (res,), carried = _call(
        body, name=name, grid=grid, in_specs=[a_spec, b_spec] + add_specs, out_specs=[out_spec],
        out_shape=[out_shape], scratch_shapes=[pltpu.VMEM((tm, tn), F32)] if nk > 1 else [],
        args=(a, b, *[x for _, x in adds]), sem=("parallel", "parallel", "arbitrary"), rider=rider, after=after)
    return (res, carried) if rider is not None else res


def _cast_bf16(w, name, after):
    R, C = w.shape
    tr = _row_tile(R, C * 4, 16)

    def body(w_ref, after_ref, o_ref):
        o_ref[...] = w_ref[...].astype(BF16)

    return pl.pallas_call(
        body, name=name, out_shape=jax.ShapeDtypeStruct((R, C), BF16), grid=(R // tr,),
        in_specs=[pl.BlockSpec((tr, C), lambda r: (r, 0)), pl.BlockSpec(memory_space=pl.ANY)],
        out_specs=pl.BlockSpec((tr, C), lambda r: (r, 0)), compiler_params=_cparams(("parallel",)),
    )(w, after)


def _cast_bf16_into_slot(w, jc_arr, name, after):
    R, C = w.shape
    tr = _row_tile(R, C * 4, 16)

    def body(jc_ref, w_ref, after_ref, o_ref):
        o_ref[...] = w_ref[...].astype(BF16)

    gs = pltpu.PrefetchScalarGridSpec(
        num_scalar_prefetch=1, grid=(R // tr,),
        in_specs=[pl.BlockSpec((tr, C), lambda r, jc: (r, 0)), pl.BlockSpec(memory_space=pl.ANY)],
        out_specs=pl.BlockSpec((None, tr, C), lambda r, jc: (jc[0], r, 0)))
    return pl.pallas_call(body, name=name, out_shape=jax.ShapeDtypeStruct((N_SHARDS, R, C), BF16), grid_spec=gs,
                          compiler_params=_cparams(("parallel",)))(jc_arr, w, after)


def _pair_sum(g, la, jc_arr, name):
    S, R, C = g.shape
    half = R // 2
    tr = _row_tile(half, C * 4, 16)
    nrt = half // tr
    dt = g.dtype

    def body(jc_ref, g_ref, la_ref, o_ref):
        o_ref[...] = (g_ref[...].astype(F32) + la_ref[...].astype(F32)).astype(dt)

    gs = pltpu.PrefetchScalarGridSpec(
        num_scalar_prefetch=1, grid=(S, nrt),
        in_specs=[pl.BlockSpec((None, tr, C), lambda s, r, jc: (s, jc[1] * nrt + r, 0)),
                  pl.BlockSpec((None, tr, C), lambda s, r, jc: (s, r, 0))],
        out_specs=pl.BlockSpec((None, tr, C), lambda s, r, jc: (s, r, 0)))
    return pl.pallas_call(body, name=name, out_shape=jax.ShapeDtypeStruct((S, half, C), dt), grid_spec=gs,
                          compiler_params=_cparams(("parallel", "parallel")))(jc_arr, g, la)


def _shard_sum(cp, lb, jc_arr, name, all_slots=False):
    S, h, C = cp.shape
    tr = _row_tile(h, C * 4, 16)

    def body(jc_ref, cp_ref, l0, l1, l2, o_ref):
        o_ref[...] = ((cp_ref[...].astype(F32) + l0[...].astype(F32)) + l1[...].astype(F32)) + l2[...].astype(F32)

    def lspec(kk):
        return pl.BlockSpec((None, tr, C), lambda r, jc: (kk, r, 0))

    if all_slots:
        out_spec = pl.BlockSpec((None, None, tr, C), lambda r, jc: (jc[0], jc[1], r, 0))
        out_shape = jax.ShapeDtypeStruct((S, 2, h, C), F32)
    else:
        out_spec = pl.BlockSpec((None, tr, C), lambda r, jc: (jc[1], r, 0))
        out_shape = jax.ShapeDtypeStruct((2, h, C), F32)
    gs = pltpu.PrefetchScalarGridSpec(
        num_scalar_prefetch=1, grid=(h // tr,),
        in_specs=[pl.BlockSpec((None, tr, C), lambda r, jc: (jc[0], r, 0)), lspec(0), lspec(1), lspec(2)],
        out_specs=out_spec)
    return pl.pallas_call(body, name=name, out_shape=out_shape, grid_spec=gs,
                          compiler_params=_cparams(("parallel",)))(jc_arr, cp, lb, lb, lb)


ANY = pl.BlockSpec(memory_space=pl.ANY)


def _place():
    x, y, c = lax.axis_index("x"), lax.axis_index("y"), lax.axis_index("c")
    chips = [(1 - x, y), (x, 1 - y), (1 - x, 1 - y)]
    return x, y, c, chips


class _Rider:
    def __init__(self, inputs, out_shape, aliases, sems, start, finish):
        self.inputs, self.out_shape, self.aliases, self.sems = list(inputs), list(out_shape), dict(aliases), list(sems)
        self.start, self.finish = start, finish


def _join_riders(r1, r2):
    i1, o1, s1 = len(r1.inputs), len(r1.out_shape), len(r1.sems)
    aliases = dict(r1.aliases)
    aliases.update({i1 + i: o1 + o for i, o in r2.aliases.items()})

    def start(ins, outs, sems):
        r1.start(ins[:i1], outs[:o1], sems[:s1])
        r2.start(ins[i1:], outs[o1:], sems[s1:])

    def finish(ins, outs, sems):
        r1.finish(ins[:i1], outs[:o1], sems[:s1])
        r2.finish(ins[i1:], outs[o1:], sems[s1:])

    return _Rider(r1.inputs + r2.inputs, r1.out_shape + r2.out_shape, aliases, r1.sems + r2.sems, start, finish)


def _after_rider(x):
    return _Rider([x], [], {}, [], lambda *a: None, lambda *a: None)


def _call(body, *, name, grid, in_specs, out_specs, out_shape, scratch_shapes, args, sem, rider=None, after=None):
    out_specs, out_shape = tuple(out_specs), tuple(out_shape)
    if after is not None:
        rider = _after_rider(after) if rider is None else _join_riders(_after_rider(after), rider)
    if rider is None:
        res = pl.pallas_call(body, name=name, out_shape=out_shape, grid=grid, in_specs=list(in_specs),
                             out_specs=out_specs, scratch_shapes=list(scratch_shapes),
                             compiler_params=_cparams(sem))(*args)
        return tuple(res), []
    n_in, n_out, n_sc = len(in_specs), len(out_specs), len(scratch_shapes)
    r_in, r_out = len(rider.inputs), len(rider.out_shape)

    def wrapped(*refs):
        p = 0
        host_in = refs[p:p + n_in]; p += n_in
        rid_in = refs[p:p + r_in]; p += r_in
        host_out = refs[p:p + n_out]; p += n_out
        rid_out = refs[p:p + r_out]; p += r_out
        host_sc = refs[p:p + n_sc]; p += n_sc
        rid_sem = refs[p:]
        first = functools.reduce(jnp.logical_and, [pl.program_id(a) == 0 for a in range(len(grid))])
        last = functools.reduce(jnp.logical_and, [pl.program_id(a) == grid[a] - 1 for a in range(len(grid))])

        @pl.when(first)
        def _():
            rider.start(rid_in, rid_out, rid_sem)

        body(*host_in, *host_out, *host_sc)

        @pl.when(last)
        def _():
            rider.finish(rid_in, rid_out, rid_sem)

    res = pl.pallas_call(
        wrapped, name=name, out_shape=out_shape + tuple(rider.out_shape), grid=grid,
        in_specs=list(in_specs) + [ANY] * r_in, out_specs=out_specs + (ANY,) * r_out,
        input_output_aliases={n_in + i: n_out + o for i, o in rider.aliases.items()},
        scratch_shapes=list(scratch_shapes) + rider.sems,
        compiler_params=_cparams(("arbitrary",) * len(grid)),
    )(*args, *rider.inputs)
    return tuple(res[:n_out]), list(res[n_out:])


def _run_rider(rider, name):
    def body(*refs):
        r_in, r_out = len(rider.inputs), len(rider.out_shape)
        ins, outs, sems = refs[:r_in], refs[r_in:r_in + r_out], refs[r_in + r_out:]
        rider.start(ins, outs, sems)
        rider.finish(ins, outs, sems)

    return pl.pallas_call(
        body, name=name, out_shape=rider.out_shape, in_specs=[ANY] * len(rider.inputs),
        out_specs=[ANY] * len(rider.out_shape), input_output_aliases=rider.aliases, scratch_shapes=rider.sems,
    )(*rider.inputs)


def _atoms(indices, kks=(0, 1, 2), q=0, nq=1):
    return [(i, kk, q, nq) for i in indices for kk in kks]


def _mm_shards(a, buf, order_arr, which, name, after, out=None):
    M, K = a.shape
    S, _, n = buf.shape
    tm = _pick(M, (512, 256, 128))
    s0 = which[0]

    def body(order_ref, a_ref, b_ref, *rest):
        rest[-1][...] = jnp.dot(a_ref[...], b_ref[...], preferred_element_type=F32)

    gs = pltpu.PrefetchScalarGridSpec(
        num_scalar_prefetch=1, grid=(len(which), M // tm),
        in_specs=[pl.BlockSpec((tm, K), lambda g, i, order: (i, 0)),
                  pl.BlockSpec((None, K, n), lambda g, i, order: (order[s0 + g], 0, 0)), ANY]
        + ([ANY] if out is not None else []),
        out_specs=pl.BlockSpec((tm, n), lambda g, i, order: (i, order[s0 + g])))
    return pl.pallas_call(
        body, name=name, grid_spec=gs, out_shape=jax.ShapeDtypeStruct((M, S * n), F32),
        input_output_aliases={4: 0} if out is not None else {},
        compiler_params=_cparams(("arbitrary", "arbitrary")),
    )(order_arr, a, buf, after, *([out] if out is not None else []))


def _all_gather_small(shards):
    n = len(shards)

    def body(*refs):
        w = refs[:n]
        out = refs[n:2 * n]
        local_sem, s_sem, r_sem = refs[2 * n:]
        x, y, c, chips = _place()
        j_me = 2 * x + y
        cps = []
        for i in range(n):
            lc = pltpu.make_async_copy(w[i], out[i].at[j_me], local_sem.at[i])
            lc.start()
            cps.append(lc)
        sends = []
        for i in range(n):
            for kk, (px, py) in enumerate(chips):
                cp = pltpu.make_async_remote_copy(
                    src_ref=w[i], dst_ref=out[i].at[j_me], send_sem=s_sem.at[3 * i + kk],
                    recv_sem=r_sem.at[3 * i + kk], device_id=(px, py, c), device_id_type=MESH)
                cp.start()
                sends.append(cp)
        for i in range(n):
            for kk, (px, py) in enumerate(chips):
                sends[3 * i + kk].wait_send()
                pltpu.make_async_remote_copy(
                    src_ref=w[i], dst_ref=out[i].at[2 * px + py], send_sem=s_sem.at[3 * i + kk],
                    recv_sem=r_sem.at[3 * i + kk], device_id=(px, py, c), device_id_type=MESH).wait_recv()
        for lc in cps:
            lc.wait()

    out_shape = [jax.ShapeDtypeStruct((N_SHARDS,) + s.shape, s.dtype) for s in shards]
    return pl.pallas_call(
        body, name="all_gather_conv_weights", out_shape=out_shape, in_specs=[ANY] * n, out_specs=[ANY] * n,
        scratch_shapes=[pltpu.SemaphoreType.DMA((n,)), pltpu.SemaphoreType.DMA((3 * n,)),
                        pltpu.SemaphoreType.DMA((3 * n,))],
    )(*shards)


def _pair_rider(grads):
    n = len(grads)

    def copies(g, la, sems):
        x, y, c, _ = _place()
        return [pltpu.make_async_remote_copy(
            src_ref=g[i].at[:, pl.ds((1 - c) * (g[i].shape[1] // 2), g[i].shape[1] // 2), :], dst_ref=la[i],
            send_sem=sems[0].at[i], recv_sem=sems[1].at[i], device_id=(x, y, 1 - c), device_id_type=MESH)
            for i in range(n)]

    def start(g, la, sems):
        for cp in copies(g, la, sems):
            cp.start()

    def finish(g, la, sems):
        for cp in copies(g, la, sems):
            cp.wait()

    return _Rider(grads, [jax.ShapeDtypeStruct((s.shape[0], s.shape[1] // 2, s.shape[2]), s.dtype) for s in grads],
                  {}, [pltpu.SemaphoreType.DMA((n,)), pltpu.SemaphoreType.DMA((n,))], start, finish)


def _shard_exchange_rider(cps_in, atoms=None):
    n = len(cps_in)
    if atoms is None:
        atoms = _atoms(range(n))

    def copies(ins, lb, sems):
        x, y, c, chips = _place()
        out = []
        for a, (i, kk, q, nq) in enumerate(atoms):
            h = ins[i].shape[1]
            assert h % (16 * nq) == 0, (h, nq)
            rows = pl.ds(q * (h // nq), h // nq)
            px, py = chips[kk]
            out.append(pltpu.make_async_remote_copy(
                src_ref=ins[i].at[2 * px + py, rows, :], dst_ref=lb[i].at[kk, rows, :],
                send_sem=sems[0].at[a], recv_sem=sems[1].at[a], device_id=(px, py, c), device_id_type=MESH))
        return out

    def start(ins, lb, sems):
        for cp in copies(ins, lb, sems):
            cp.start()

    def finish(ins, lb, sems):
        for cp in copies(ins, lb, sems):
            cp.wait()

    return _Rider(cps_in, [jax.ShapeDtypeStruct((3,) + s.shape[1:], s.dtype) for s in cps_in], {},
                  [pltpu.SemaphoreType.DMA((len(atoms),)), pltpu.SemaphoreType.DMA((len(atoms),))], start, finish)


HBM = pl.BlockSpec(memory_space=pltpu.HBM)
SEM = pl.BlockSpec(memory_space=pltpu.SEMAPHORE)


def _shard_copies(part_refs, land_refs, send_sems, recv_sems, relations):
    x, y, c, chips = _place()
    nr = len(relations)
    return [pltpu.make_async_remote_copy(
        src_ref=part_refs[i].at[2 * chips[kk][0] + chips[kk][1]], dst_ref=land_refs[i].at[kk],
        send_sem=send_sems.at[nr * i + r], recv_sem=recv_sems.at[nr * i + r],
        device_id=(chips[kk][0], chips[kk][1], c), device_id_type=MESH)
        for i in range(len(part_refs)) for r, kk in enumerate(relations)]


SIDE_EFFECT = pltpu.SideEffectType.DATAFLOW_SIDE_EFFECTING


def _shard_exchange_start(parts, name, relations=(0, 1, 2), lands=None):
    n = len(parts)
    ns = n * len(relations)

    def body(*refs):
        part_refs, land_refs = refs[:n], refs[n:2 * n]
        send_sems, recv_sems = refs[2 * n], refs[2 * n + 1]
        token = refs[4 * n + 2]
        for cp in _shard_copies(part_refs, land_refs, send_sems, recv_sems, relations):
            cp.start()
        token[...] = jnp.zeros_like(token)

    if lands is None:
        lands = [lax.empty((3,) + p.shape[1:], p.dtype) for p in parts]
    bufs = list(parts) + list(lands)
    res = pl.pallas_call(
        body, name=name,
        out_shape=(pltpu.SemaphoreType.DMA((ns,)), pltpu.SemaphoreType.DMA((ns,)),
                   *[pltpu.HBM(b.shape, b.dtype) for b in bufs], jax.ShapeDtypeStruct((8, LANES), F32)),
        in_specs=(HBM,) * (2 * n), out_specs=(SEM, SEM) + (HBM,) * (2 * n) + (pl.BlockSpec(memory_space=pltpu.VMEM),),
        input_output_aliases={i: 2 + i for i in range(2 * n)},
        compiler_params=pltpu.CompilerParams(has_side_effects=SIDE_EFFECT),
    )(*[pltpu.with_memory_space_constraint(b, pltpu.HBM) for b in bufs])
    return res[0], res[1], list(res[2:2 + n]), list(res[2 + n:2 + 2 * n]), res[2 + 2 * n], relations


def _shard_exchange_wait(started, after, name):
    send_sems, recv_sems, parts, lands, _, relations = started
    n = len(parts)

    def body(*refs):
        part_refs, land_refs = refs[:n], refs[n:2 * n]
        send_sems_ref, recv_sems_ref = refs[2 * n], refs[2 * n + 1]
        for cp in _shard_copies(part_refs, land_refs, send_sems_ref, recv_sems_ref, relations):
            cp.wait_send()
            cp.wait_recv()

    bufs = parts + lands
    res = pl.pallas_call(
        body, name=name, out_shape=tuple(pltpu.HBM(b.shape, b.dtype) for b in bufs),
        in_specs=(HBM,) * (2 * n) + (SEM, SEM, ANY), out_specs=(HBM,) * (2 * n),
        input_output_aliases={i: i for i in range(2 * n)},
        compiler_params=pltpu.CompilerParams(has_side_effects=SIDE_EFFECT),
    )(*bufs, send_sems, recv_sems, after)
    return list(res[:n]), list(res[n:])


def _gather_copies(buf_ref, send_sems, recv_sems, over_d2d, arriving, relations):
    x, y, c, chips = _place()
    half = buf_ref.shape[1] // 2
    out = []
    for r, kk in enumerate(relations):
        px, py = chips[kk]
        if over_d2d:
            slot, core, peer = 2 * px + py, (1 - c) if arriving else c, (x, y, 1 - c)
        else:
            slot, core, peer = (2 * px + py) if arriving else (2 * x + y), c, (px, py, c)
        blk = buf_ref.at[slot, pl.ds(core * half, half), :]
        out.append(pltpu.make_async_remote_copy(src_ref=blk, dst_ref=blk, send_sem=send_sems.at[r],
                                                recv_sem=recv_sems.at[r], device_id=peer, device_id_type=MESH))
    return out


def _gather_step(buf, after, name, sems_in=None, start_d2d=None, relations=(0, 1, 2)):
    n_sem = 0 if sems_in is None else 2

    def body(*refs):
        buf_ref = refs[0]
        ins = refs[1:1 + n_sem]
        outs = refs[2 + n_sem:]
        if sems_in is not None:
            waited_d2d = start_d2d is None
            for mine, theirs in zip(_gather_copies(buf_ref, ins[0], ins[1], waited_d2d, False, relations),
                                    _gather_copies(buf_ref, ins[0], ins[1], waited_d2d, True, relations)):
                theirs.wait_recv()
                mine.wait_send()
        if start_d2d is not None:
            for cp in _gather_copies(buf_ref, outs[0], outs[1], start_d2d, False, relations):
                cp.start()
            outs[3][...] = jnp.zeros_like(outs[3])

    nr = len(relations)
    sem_out = () if start_d2d is None else (pltpu.SemaphoreType.DMA((nr,)), pltpu.SemaphoreType.DMA((nr,)))
    tok_out = () if start_d2d is None else (jax.ShapeDtypeStruct((8, LANES), F32),)
    res = pl.pallas_call(
        body, name=name,
        out_shape=sem_out + (pltpu.HBM(buf.shape, buf.dtype),) + tok_out,
        in_specs=(HBM,) + (SEM,) * n_sem + (ANY,),
        out_specs=(SEM,) * len(sem_out) + (HBM,) + (pl.BlockSpec(memory_space=pltpu.VMEM),) * len(tok_out),
        input_output_aliases={0: len(sem_out)},
        compiler_params=pltpu.CompilerParams(has_side_effects=SIDE_EFFECT),
    )(pltpu.with_memory_space_constraint(buf, pltpu.HBM), *(sems_in or ()), after)
    if start_d2d is None:
        return res[0]
    return (res[0], res[1]), res[2], res[3]


def _gather_start_all(bufs, after, name):
    n = len(bufs)

    def body(*refs):
        outs = refs[n + 1:]
        for i in range(n):
            for cp in _gather_copies(refs[i], outs[2 * i], outs[2 * i + 1], False, False, (0, 1, 2)):
                cp.start()
        outs[-1][...] = jnp.zeros_like(outs[-1])

    res = pl.pallas_call(
        body, name=name,
        out_shape=(pltpu.SemaphoreType.DMA((3,)),) * (2 * n) + tuple(pltpu.HBM(b.shape, b.dtype) for b in bufs)
        + (jax.ShapeDtypeStruct((8, LANES), F32),),
        in_specs=(HBM,) * n + (ANY,),
        out_specs=(SEM,) * (2 * n) + (HBM,) * n + (pl.BlockSpec(memory_space=pltpu.VMEM),),
        input_output_aliases={i: 2 * n + i for i in range(n)},
        compiler_params=pltpu.CompilerParams(has_side_effects=SIDE_EFFECT),
    )(*[pltpu.with_memory_space_constraint(b, pltpu.HBM) for b in bufs], after)
    return [((res[2 * i], res[2 * i + 1]), res[2 * n + i]) for i in range(n)], res[3 * n]


def _relay_copies(buf_ref, send_sems, recv_sems, arriving):
    x, y, c, chips = _place()
    quarter = buf_ref.shape[1] // 4
    out = []
    for r, (src_kk, dst_kk) in enumerate(((0, 1), (1, 0))):
        slot = (2 * chips[2][0] + chips[2][1]) if arriving else (2 * chips[src_kk][0] + chips[src_kk][1])
        blk = buf_ref.at[slot, pl.ds((2 * c + r) * quarter, quarter), :]
        out.append(pltpu.make_async_remote_copy(
            src_ref=blk, dst_ref=blk, send_sem=send_sems.at[r], recv_sem=recv_sems.at[r],
            device_id=(chips[dst_kk][0], chips[dst_kk][1], c), device_id_type=MESH))
    return out


def _gather_relay_step(buf, after, name, sems_in, relay_in=None):
    first = relay_in is None
    ins_sems = sems_in if first else relay_in
    near, diag = (0, 1), (2,)

    def body(*refs):
        buf_ref, in_s, in_r = refs[0], refs[1], refs[2]
        outs = refs[4:]
        if first:
            for mine, theirs in zip(_gather_copies(buf_ref, in_s, in_r, False, False, near),
                                    _gather_copies(buf_ref, in_s, in_r, False, True, near)):
                theirs.wait_recv()
                mine.wait_send()
            for cp in _gather_copies(buf_ref, outs[0], outs[1], True, False, near):
                cp.start()
            for cp in _relay_copies(buf_ref, outs[2], outs[3], False):
                cp.start()
        else:
            for mine, theirs in zip(_relay_copies(buf_ref, in_s, in_r, False), _relay_copies(buf_ref, in_s, in_r, True)):
                theirs.wait_recv()
                mine.wait_send()
            for cp in _gather_copies(buf_ref, outs[0], outs[1], True, False, diag):
                cp.start()
        outs[-1][...] = jnp.zeros_like(outs[-1])

    def sem(n):
        return pltpu.SemaphoreType.DMA((n,))

    sem_out = (sem(2), sem(2), sem(2), sem(2)) if first else (sem(1), sem(1))
    res = pl.pallas_call(
        body, name=name,
        out_shape=sem_out + (pltpu.HBM(buf.shape, buf.dtype), jax.ShapeDtypeStruct((8, LANES), F32)),
        in_specs=(HBM, SEM, SEM, ANY),
        out_specs=(SEM,) * len(sem_out) + (HBM, pl.BlockSpec(memory_space=pltpu.VMEM)),
        input_output_aliases={0: len(sem_out)},
        compiler_params=pltpu.CompilerParams(has_side_effects=SIDE_EFFECT),
    )(pltpu.with_memory_space_constraint(buf, pltpu.HBM), *ins_sems, after)
    if first:
        return (res[0], res[1]), (res[2], res[3]), res[4], res[5]
    return (res[0], res[1]), res[2], res[3]


def _share_rider(halves, eighths=None):
    n = len(halves)
    bufs = list(halves) + ([eighths] if eighths is not None else [])

    def half_copy(out, sems, i, core):
        x, y, c, _ = _place()
        blk = out[i].at[core]
        return pltpu.make_async_remote_copy(src_ref=blk, dst_ref=blk, send_sem=sems[0].at[i], recv_sem=sems[1].at[i],
                                            device_id=(x, y, 1 - c), device_id_type=MESH)

    def eighth_copy(out, sems, r, mine):
        x, y, c, _ = _place()
        px, py, pc = x ^ ((r >> 2) & 1), y ^ ((r >> 1) & 1), c ^ (r & 1)
        blk = out[n].at[2 * x + y, c] if mine else out[n].at[2 * px + py, pc]
        return pltpu.make_async_remote_copy(src_ref=blk, dst_ref=blk, send_sem=sems[2].at[r - 1],
                                            recv_sem=sems[3].at[r - 1], device_id=(px, py, pc), device_id_type=MESH)

    def start(ins, out, sems):
        c = lax.axis_index("c")
        for i in range(n):
            half_copy(out, sems, i, c).start()
        if eighths is not None:
            for r in range(1, N_DEV):
                eighth_copy(out, sems, r, True).start()

    def finish(ins, out, sems):
        c = lax.axis_index("c")
        for i in range(n):
            half_copy(out, sems, i, 1 - c).wait_recv()
        if eighths is not None:
            for r in range(1, N_DEV):
                eighth_copy(out, sems, r, False).wait_recv()
        for i in range(n):
            half_copy(out, sems, i, c).wait_send()
        if eighths is not None:
            for r in range(1, N_DEV):
                eighth_copy(out, sems, r, True).wait_send()

    return _Rider(bufs, [jax.ShapeDtypeStruct(s.shape, s.dtype) for s in bufs], {i: i for i in range(len(bufs))},
                  [pltpu.SemaphoreType.DMA((max(n, 1),)), pltpu.SemaphoreType.DMA((max(n, 1),)),
                   pltpu.SemaphoreType.DMA((N_DEV - 1,)), pltpu.SemaphoreType.DMA((N_DEV - 1,))], start, finish)


ATTN_ROWS = GROUP * ATTN_BLOCK
ATTN_KEYS = 2 * ATTN_BLOCK


def _attn_geometry(n):
    row = lax.broadcasted_iota(jnp.int32, (ATTN_ROWS, ATTN_KEYS), 0)
    col = lax.broadcasted_iota(jnp.int32, (ATTN_ROWS, ATTN_KEYS), 1)
    dist = ATTN_BLOCK + jnp.bitwise_and(row, ATTN_BLOCK - 1) - col
    valid = jnp.logical_and(jnp.logical_and(dist >= 0, dist < ATTN_BLOCK),
                            jnp.logical_or(col >= ATTN_BLOCK, n > 0))
    return dist.astype(F32), valid


def _per_head_column(values):
    head = lax.broadcasted_iota(jnp.int32, (ATTN_ROWS, 1), 0) // ATTN_BLOCK
    col = jnp.zeros((ATTN_ROWS, 1), F32)
    for hh, v in enumerate(values):
        col = jnp.where(head == hh, v, col)
    return col


def _stack_heads(ref, g):
    return jnp.concatenate(
        [ref[:, (g * GROUP + hh) * HEAD_DIM:(g * GROUP + hh + 1) * HEAD_DIM].astype(BF16) for hh in range(GROUP)],
        axis=0)


def _attn_probs(q_s, k2, slope_col, sink_col, dist, valid):
    s = lax.dot_general(q_s, k2, (((1,), (1,)), ((), ())), preferred_element_type=F32) * (HEAD_DIM ** -0.5)
    s = jnp.where(valid, s - slope_col * dist, NEG)
    m = jnp.maximum(jnp.max(s, axis=1, keepdims=True), sink_col)
    e = jnp.exp(s - m)
    es = jnp.exp(sink_col - m)
    inv = 1.0 / (jnp.sum(e, axis=1, keepdims=True) + es)
    return e * inv, es * inv


def _attn_specs(T, d_attn, d_kv, q_blk, k_blk, v_blk):
    bq = pl.BlockSpec((ATTN_BLOCK, d_attn), lambda n: (n, q_blk))
    kp = pl.BlockSpec((ATTN_BLOCK, d_kv), lambda n: (jnp.maximum(n - 1, 0), k_blk))
    kc = pl.BlockSpec((ATTN_BLOCK, d_kv), lambda n: (n, k_blk))
    vp = pl.BlockSpec((ATTN_BLOCK, d_kv), lambda n: (jnp.maximum(n - 1, 0), v_blk))
    vc = pl.BlockSpec((ATTN_BLOCK, d_kv), lambda n: (n, v_blk))
    return bq, kp, kc, vp, vc


def _attn_fwd(proj, sinks, nq, cols, after=None):
    T = proj.shape[0]
    nkv = nq // GROUP
    d_attn, d_kv = nq * HEAD_DIM, nkv * HEAD_DIM
    q_off, k_off, v_off = cols
    bq, kp, kc, vp, vc = _attn_specs(T, d_attn, d_kv, q_off // d_attn, k_off // d_kv, v_off // d_kv)

    def body(sink_ref, q_ref, kp_ref, kc_ref, vp_ref, vc_ref, o_ref):
        n = pl.program_id(0)
        dist, valid = _attn_geometry(n)
        for g in range(nkv):
            ks = slice(g * HEAD_DIM, (g + 1) * HEAD_DIM)
            k2 = jnp.concatenate([kp_ref[:, ks], kc_ref[:, ks]], axis=0).astype(BF16)
            v2 = jnp.concatenate([vp_ref[:, ks], vc_ref[:, ks]], axis=0).astype(BF16)
            slope_col = _per_head_column([2.0 ** (-8.0 * (g * GROUP + hh + 1) / nq) for hh in range(GROUP)])
            sink_col = _per_head_column([sink_ref[0, g * GROUP + hh] for hh in range(GROUP)])
            p, _ = _attn_probs(_stack_heads(q_ref, g), k2, slope_col, sink_col, dist, valid)
            o = jnp.dot(p.astype(BF16), v2, preferred_element_type=F32).astype(BF16)
            for hh in range(GROUP):
                h = g * GROUP + hh
                o_ref[:, h * HEAD_DIM:(h + 1) * HEAD_DIM] = o[hh * ATTN_BLOCK:(hh + 1) * ATTN_BLOCK, :]

    (out,), carried = _call(
        body, name="attn_fwd", out_shape=[jax.ShapeDtypeStruct((T, d_attn), BF16)], grid=(T // ATTN_BLOCK,),
        in_specs=[pl.BlockSpec(memory_space=pltpu.SMEM), bq, kp, kc, vp, vc],
        out_specs=[pl.BlockSpec((ATTN_BLOCK, d_attn), lambda n: (n, 0))], scratch_shapes=[],
        args=(sinks, proj, proj, proj, proj, proj), sem=("parallel",), after=after)
    return out


def _attn_bwd(proj, d_attn_out, sinks, nq, cols, after=None):
    T = proj.shape[0]
    nkv = nq // GROUP
    d_attn, d_kv = nq * HEAD_DIM, nkv * HEAD_DIM
    q_off, k_off, v_off = cols
    bq, kp, kc, vp, vc = _attn_specs(T, d_attn, d_kv, q_off // d_attn, k_off // d_kv, v_off // d_kv)
    scale = HEAD_DIM ** -0.5
    dn_t = (((1,), (1,)), ((), ()))
    dn_r = (((0,), (0,)), ((), ()))

    def body(sink_ref, q_ref, kp_ref, kc_ref, vp_ref, vc_ref, do_ref, dq_ref, dk_ref, dv_ref, ds_ref):
        n = pl.program_id(0)

        @pl.when(n == 0)
        def _():
            dk_ref[...] = jnp.zeros_like(dk_ref)
            dv_ref[...] = jnp.zeros_like(dv_ref)
            ds_ref[...] = jnp.zeros_like(ds_ref)

        dist, valid = _attn_geometry(n)
        rows_c = pl.ds(pl.multiple_of(n * ATTN_BLOCK, ATTN_BLOCK), ATTN_BLOCK)
        rows_p = pl.ds(pl.multiple_of(jnp.maximum(n - 1, 0) * ATTN_BLOCK, ATTN_BLOCK), ATTN_BLOCK)
        lane = lax.broadcasted_iota(jnp.int32, ds_ref.shape, 1)
        srow = lax.broadcasted_iota(jnp.int32, ds_ref.shape, 0)
        ds_acc = jnp.zeros(ds_ref.shape, F32)
        for g in range(nkv):
            ks = slice(g * HEAD_DIM, (g + 1) * HEAD_DIM)
            k2 = jnp.concatenate([kp_ref[:, ks], kc_ref[:, ks]], axis=0).astype(BF16)
            v2 = jnp.concatenate([vp_ref[:, ks], vc_ref[:, ks]], axis=0).astype(BF16)
            slope_col = _per_head_column([2.0 ** (-8.0 * (g * GROUP + hh + 1) / nq) for hh in range(GROUP)])
            sink_col = _per_head_column([sink_ref[0, g * GROUP + hh] for hh in range(GROUP)])
            q_s = _stack_heads(q_ref, g)
            do_s = _stack_heads(do_ref, g)
            p, p_sink = _attn_probs(q_s, k2, slope_col, sink_col, dist, valid)
            dp = lax.dot_general(do_s, v2, dn_t, preferred_element_type=F32)
            delta = jnp.sum(p * dp, axis=1, keepdims=True)
            ds = (p * (dp - delta)).astype(BF16)
            sink_part = p_sink * delta
            dq = (jnp.dot(ds, k2, preferred_element_type=F32) * scale).astype(BF16)
            for hh in range(GROUP):
                h = g * GROUP + hh
                blk = slice(hh * ATTN_BLOCK, (hh + 1) * ATTN_BLOCK)
                dq_ref[:, h * HEAD_DIM:(h + 1) * HEAD_DIM] = dq[blk, :]
                ds_acc = ds_acc + jnp.where(jnp.logical_and(lane == h, srow == 0), -jnp.sum(sink_part[blk, :]), 0.0)
            dk2 = lax.dot_general(ds, q_s, dn_r, preferred_element_type=F32) * scale
            dv2 = lax.dot_general(p.astype(BF16), do_s, dn_r, preferred_element_type=F32)
            dk_ref[rows_p, ks] += dk2[:ATTN_BLOCK, :]
            dv_ref[rows_p, ks] += dv2[:ATTN_BLOCK, :]
            dk_ref[rows_c, ks] += dk2[ATTN_BLOCK:, :]
            dv_ref[rows_c, ks] += dv2[ATTN_BLOCK:, :]
        ds_ref[...] += ds_acc

    out_shape = (jax.ShapeDtypeStruct((T, d_attn), BF16), jax.ShapeDtypeStruct((T, d_kv), F32),
                 jax.ShapeDtypeStruct((T, d_kv), F32), jax.ShapeDtypeStruct((8, LANES), F32))
    return _call(
        body, name="attn_bwd", out_shape=out_shape, grid=(T // ATTN_BLOCK,),
        in_specs=[pl.BlockSpec(memory_space=pltpu.SMEM), bq, kp, kc, vp, vc,
                  pl.BlockSpec((ATTN_BLOCK, d_attn), lambda n: (n, 0))],
        out_specs=(pl.BlockSpec((ATTN_BLOCK, d_attn), lambda n: (n, 0)),
                   pl.BlockSpec((T, d_kv), lambda n: (0, 0)), pl.BlockSpec((T, d_kv), lambda n: (0, 0)),
                   pl.BlockSpec((8, LANES), lambda n: (0, 0))),
        scratch_shapes=[], args=(sinks, proj, proj, proj, proj, proj, d_attn_out), sem=("arbitrary",), after=after)[0]


def _rnn_tile(T):
    return _pick(T, (128,))


def _rnn_gates(x_ext, cw_ref, cb_ref, wa_ref, wi_ref, ba_ref, bi_ref, lam_ref, tt):
    xs = [pltpu.roll(x_ext, 3 - k, 0)[8:, :] if k < 3 else x_ext[8:, :] for k in range(4)]
    cx = cb_ref[...] + xs[0] * cw_ref[0:1, :]
    for k in range(1, 4):
        cx = cx + xs[k] * cw_ref[k:k + 1, :]
    cxb = cx.astype(BF16)
    r = jax.nn.sigmoid(jnp.dot(cxb, wa_ref[...], preferred_element_type=F32) + ba_ref[...])
    i = jax.nn.sigmoid(jnp.dot(cxb, wi_ref[...], preferred_element_type=F32) + bi_ref[...])
    lam = lam_ref[...]
    sp = jnp.maximum(-lam, 0.0) + jnp.log1p(jnp.exp(-jnp.abs(lam)))
    log_a = -LRU_C * r * sp
    a = jnp.exp(log_a)
    z = 2.0 * log_a
    em1 = jnp.where(z > -1e-2, z * (1.0 + z * (0.5 + z * (1.0 / 6.0 + z * (1.0 / 24.0)))), jnp.exp(z) - 1.0)
    s = jnp.sqrt(-em1)
    return xs, cx, r, i, sp, a, s


def _rnn_specs(T, gw, tt, rx_blk, ry_blk, rev):
    nT = T // tt
    hb = tt // 8

    def tile(t):
        return (nT - 1 - t) if rev else t

    rx = pl.BlockSpec((tt, gw), lambda g, t: (tile(t), rx_blk + g))
    rx_halo = pl.BlockSpec((8, gw), lambda g, t: (jnp.maximum(tile(t) * hb - 1, 0), rx_blk + g))
    ry = pl.BlockSpec((tt, gw), lambda g, t: (tile(t), ry_blk + g))
    cw = pl.BlockSpec((4, gw), lambda g, t: (0, g))
    vec = pl.BlockSpec((1, gw), lambda g, t: (0, g))
    wg = pl.BlockSpec((None, gw, gw), lambda g, t: (g, 0, 0))
    act = pl.BlockSpec((tt, gw), lambda g, t: (tile(t), g))
    act_halo = pl.BlockSpec((8, gw), lambda g, t: (jnp.maximum(tile(t) * hb - 1, 0), g))
    return rx, rx_halo, ry, cw, vec, wg, act, act_halo, tile


def _rnn_fwd(proj, cols, conv_w, conv_b, wa_g, wi_g, ba, bi, lam, rider=None):
    T = proj.shape[0]
    G, gw, _ = wa_g.shape
    d_rnn = G * gw
    tt = _rnn_tile(T)
    rx_off, ry_off = cols
    rx, rx_halo, ry, cw, vec, wg, act, _, _ = _rnn_specs(T, gw, tt, rx_off // gw, ry_off // gw, False)

    def body(rx_ref, rxh_ref, ry_ref, cw_ref, cb_ref, wa_ref, wi_ref, ba_ref, bi_ref, lam_ref,
             b_ref, h_ref, carry):
        t = pl.program_id(1)

        @pl.when(t == 0)
        def _():
            carry[...] = jnp.zeros_like(carry)

        halo = jnp.where(t > 0, rxh_ref[...], 0.0)
        x_ext = jnp.concatenate([halo, rx_ref[...]], axis=0)
        _, cx, _, i, _, a, s = _rnn_gates(x_ext, cw_ref, cb_ref, wa_ref, wi_ref, ba_ref, bi_ref, lam_ref, tt)
        acc_a, acc_b = a, s * (i * cx)
        d = 1
        while d < tt:
            acc_b = acc_a * _shift_down(acc_b, d, 0.0) + acc_b
            acc_a = acc_a * _shift_down(acc_a, d, 1.0)
            d *= 2
        h = acc_b + acc_a * carry[7:8, :]
        carry[...] = h[tt - 8:, :]
        h_ref[...] = h
        b_ref[...] = (h * _gelu(ry_ref[...])).astype(BF16)

    return _call(
        body, name="rnn_fwd",
        out_shape=(jax.ShapeDtypeStruct((T, d_rnn), BF16), jax.ShapeDtypeStruct((T, d_rnn), F32)),
        grid=(G, T // tt),
        in_specs=[rx, rx_halo, ry, cw, vec, wg, wg, vec, vec, vec], out_specs=(act, act),
        scratch_shapes=[pltpu.VMEM((8, gw), F32)],
        args=(proj, proj, proj, conv_w, conv_b, wa_g, wi_g, ba, bi, lam), sem=("parallel", "arbitrary"), rider=rider)


def _rnn_bwd(proj, cols, h_all, d_b, conv_w, conv_b, wa_g, wi_g, ba, bi, lam, rider=None):
    T = proj.shape[0]
    G, gw, _ = wa_g.shape
    d_rnn = G * gw
    tt = _rnn_tile(T)
    nT = T // tt
    rx_off, ry_off = cols
    rx, rx_halo, ry, cw, vec, wg, act, act_halo, _ = _rnn_specs(T, gw, tt, rx_off // gw, ry_off // gw, True)
    dn_t = (((1,), (1,)), ((), ()))
    dn_r = (((0,), (0,)), ((), ()))

    def body(rx_ref, rxh_ref, ry_ref, h_ref, hh_ref, db_ref, cw_ref, cb_ref, wa_ref, wi_ref, ba_ref, bi_ref, lam_ref,
             drx_ref, dry_ref, dcw_ref, dcb_ref, dba_ref, dbi_ref, dlam_ref, dwa_ref, dwi_ref,
             lam_carry, dcx_carry):
        t = pl.program_id(1)
        first_tile = t == nT - 1

        @pl.when(t == 0)
        def _():
            lam_carry[...] = jnp.zeros_like(lam_carry)
            dcx_carry[...] = jnp.zeros_like(dcx_carry)
            dcw_ref[...] = jnp.zeros_like(dcw_ref)
            dcb_ref[...] = jnp.zeros_like(dcb_ref)
            dba_ref[...] = jnp.zeros_like(dba_ref)
            dbi_ref[...] = jnp.zeros_like(dbi_ref)
            dlam_ref[...] = jnp.zeros_like(dlam_ref)
            dwa_ref[...] = jnp.zeros_like(dwa_ref)
            dwi_ref[...] = jnp.zeros_like(dwi_ref)

        halo = jnp.where(first_tile, 0.0, rxh_ref[...])
        x_ext = jnp.concatenate([halo, rx_ref[...]], axis=0)
        xs, cx, r, i, sp, a, s = _rnn_gates(x_ext, cw_ref, cb_ref, wa_ref, wi_ref, ba_ref, bi_ref, lam_ref, tt)
        h = h_ref[...]
        h_halo = jnp.where(first_tile, 0.0, hh_ref[...])
        h_prev = pltpu.roll(jnp.concatenate([h_halo, h], axis=0), 1, 0)[8:, :]
        gel, dgel = _gelu_and_grad(ry_ref[...])
        d_b_t = db_ref[...]
        dry_ref[...] = (d_b_t * h * dgel).astype(BF16)
        dh = d_b_t * gel

        acc_c = _shift_up(a, 1, 1.0)
        acc_l = dh
        d = 1
        while d < tt:
            acc_l = acc_c * _shift_up(acc_l, d, 0.0) + acc_l
            acc_c = acc_c * _shift_up(acc_c, d, 1.0)
            d *= 2
        lam_t = acc_l + acc_c * lam_carry[0:1, :]
        lam_carry[...] = (a * lam_t)[0:8, :]

        icx = i * cx
        d_s = lam_t * icx
        d_i = lam_t * s * cx
        dcx = lam_t * s * i
        d_a = lam_t * h_prev - d_s * (a / s)
        dlog_a = d_a * a
        d_r = dlog_a * (-LRU_C * sp)
        lam = lam_ref[...]
        dlam_ref[...] += jnp.sum(dlog_a * r, axis=0, keepdims=True) * (LRU_C * jax.nn.sigmoid(-lam))
        dpr = d_r * r * (1.0 - r)
        dpi = d_i * i * (1.0 - i)
        dba_ref[...] += jnp.sum(dpr, axis=0, keepdims=True)
        dbi_ref[...] += jnp.sum(dpi, axis=0, keepdims=True)
        cxb = cx.astype(BF16)
        dprb, dpib = dpr.astype(BF16), dpi.astype(BF16)
        dwa_ref[...] += lax.dot_general(cxb, dprb, dn_r, preferred_element_type=F32)
        dwi_ref[...] += lax.dot_general(cxb, dpib, dn_r, preferred_element_type=F32)
        dcx = (dcx + lax.dot_general(dprb, wa_ref[...], dn_t, preferred_element_type=F32)
               + lax.dot_general(dpib, wi_ref[...], dn_t, preferred_element_type=F32))

        dcb_ref[...] += jnp.sum(dcx, axis=0, keepdims=True)
        for k in range(4):
            dcw_ref[k:k + 1, :] += jnp.sum(dcx * xs[k], axis=0, keepdims=True)
        d_ext = jnp.concatenate([dcx, dcx_carry[...]], axis=0)
        drx = dcx * cw_ref[3:4, :]
        for k in range(3):
            drx = drx + pltpu.roll(d_ext, tt + 8 - (3 - k), 0)[:tt, :] * cw_ref[k:k + 1, :]
        drx_ref[...] = drx.astype(BF16)
        dcx_carry[...] = dcx[0:8, :]

    out_shape = (jax.ShapeDtypeStruct((T, d_rnn), BF16), jax.ShapeDtypeStruct((T, d_rnn), BF16),
                 jax.ShapeDtypeStruct((4, d_rnn), F32), jax.ShapeDtypeStruct((1, d_rnn), F32),
                 jax.ShapeDtypeStruct((1, d_rnn), F32), jax.ShapeDtypeStruct((1, d_rnn), F32),
                 jax.ShapeDtypeStruct((1, d_rnn), F32), jax.ShapeDtypeStruct((G, gw, gw), F32),
                 jax.ShapeDtypeStruct((G, gw, gw), F32))
    return _call(
        body, name="rnn_bwd", out_shape=out_shape, grid=(G, nT),
        in_specs=[rx, rx_halo, ry, act, act_halo, act, cw, vec, wg, wg, vec, vec, vec],
        out_specs=(act, act, cw, vec, vec, vec, vec, wg, wg),
        scratch_shapes=[pltpu.VMEM((8, gw), F32), pltpu.VMEM((8, gw), F32)],
        args=(proj, proj, proj, h_all, h_all, d_b, conv_w, conv_b, wa_g, wi_g, ba, bi, lam),
        sem=("parallel", "arbitrary"), rider=rider)


def _merge_fwd(proj, gl_off, b_gate, y_attn, y_rnn, rider=None):
    T, D = y_attn.shape
    tm = _pick(T, (256, 128))
    ct = _pick(math.gcd(gl_off, D), (512, 256, 128))
    oa, orr, nd = gl_off // ct, (gl_off + D) // ct, D // ct

    def body(ga_ref, gr_ref, ba_ref, br_ref, ya_ref, yr_ref, m_ref):
        ga = jax.nn.sigmoid(ga_ref[...] + ba_ref[...])
        gr = jax.nn.sigmoid(gr_ref[...] + br_ref[...])
        m_ref[...] = (ga * ya_ref[...] + gr * yr_ref[...]).astype(BF16)

    blk = pl.BlockSpec((tm, ct), lambda i, j: (i, j))
    (merged,), carried = _call(
        body, name="merge_fwd", out_shape=[jax.ShapeDtypeStruct((T, D), BF16)], grid=(T // tm, nd),
        in_specs=[pl.BlockSpec((tm, ct), lambda i, j: (i, oa + j)), pl.BlockSpec((tm, ct), lambda i, j: (i, orr + j)),
                  pl.BlockSpec((1, ct), lambda i, j: (0, j)), pl.BlockSpec((1, ct), lambda i, j: (0, nd + j)),
                  blk, blk],
        out_specs=[blk], scratch_shapes=[], args=(proj, proj, b_gate, b_gate, y_attn, y_rnn),
        sem=("parallel", "parallel"), rider=rider)
    return merged, carried


def _merge_bwd(proj, gl_off, b_gate, y_attn, y_rnn, d_m):
    T, D = y_attn.shape
    tm = _pick(T, (256, 128))
    ct = _pick(math.gcd(gl_off, D), (512, 256, 128))
    oa, orr, nd = gl_off // ct, (gl_off + D) // ct, D // ct

    def body(ga_ref, gr_ref, ba_ref, br_ref, ya_ref, yr_ref, dm_ref,
             dya_ref, dyr_ref, dga_ref, dgr_ref, dba_ref, dbr_ref):
        i = pl.program_id(1)

        @pl.when(i == 0)
        def _():
            dba_ref[...] = jnp.zeros_like(dba_ref)
            dbr_ref[...] = jnp.zeros_like(dbr_ref)

        ga = jax.nn.sigmoid(ga_ref[...] + ba_ref[...])
        gr = jax.nn.sigmoid(gr_ref[...] + br_ref[...])
        dm = dm_ref[...]
        dya_ref[...] = (dm * ga).astype(BF16)
        dyr_ref[...] = (dm * gr).astype(BF16)
        dga = dm * ya_ref[...] * ga * (1.0 - ga)
        dgr = dm * yr_ref[...] * gr * (1.0 - gr)
        dga_ref[...] = dga.astype(BF16)
        dgr_ref[...] = dgr.astype(BF16)
        dba_ref[...] += jnp.sum(dga, axis=0, keepdims=True)
        dbr_ref[...] += jnp.sum(dgr, axis=0, keepdims=True)

    blk = pl.BlockSpec((tm, ct), lambda j, i: (i, j))
    vec = pl.BlockSpec((1, ct), lambda j, i: (0, j))
    act = jax.ShapeDtypeStruct((T, D), BF16)
    v1 = jax.ShapeDtypeStruct((1, D), F32)
    return pl.pallas_call(
        body, name="merge_bwd", out_shape=(act, act, act, act, v1, v1), grid=(nd, T // tm),
        in_specs=[pl.BlockSpec((tm, ct), lambda j, i: (i, oa + j)), pl.BlockSpec((tm, ct), lambda j, i: (i, orr + j)),
                  vec, pl.BlockSpec((1, ct), lambda j, i: (0, nd + j)), blk, blk, blk],
        out_specs=(blk, blk, blk, blk, vec, vec),
        compiler_params=_cparams(("parallel", "arbitrary")),
    )(proj, proj, b_gate, b_gate, y_attn, y_rnn, d_m)


def _ln_fwd(x_res, delta, g, b, name, rider=None):
    T, D = x_res.shape
    tm = _pick(T, (256, 128))

    def body(x_ref, d_ref, g_ref, b_ref, y_ref, yb_ref, xh_ref, rs_ref):
        z = ALPHA * x_ref[...] + d_ref[...]
        mu = jnp.mean(z, axis=1, keepdims=True)
        zc = z - mu
        var = jnp.mean(zc * zc, axis=1, keepdims=True)
        rstd = lax.rsqrt(var + LN_EPS)
        xh = zc * rstd
        xh_ref[...] = xh
        rs_ref[...] = rstd
        y = xh * g_ref[...] + b_ref[...]
        y_ref[...] = y
        yb_ref[...] = y.astype(BF16)

    row = pl.BlockSpec((tm, D), lambda i: (i, 0))
    vec = pl.BlockSpec((1, D), lambda i: (0, 0))
    return _call(
        body, name=name,
        out_shape=(jax.ShapeDtypeStruct((T, D), F32), jax.ShapeDtypeStruct((T, D), BF16),
                   jax.ShapeDtypeStruct((T, D), F32), jax.ShapeDtypeStruct((T, 1), F32)),
        grid=(T // tm,), in_specs=[row, row, vec, vec],
        out_specs=(row, row, row, pl.BlockSpec((tm, 1), lambda i: (i, 0))),
        scratch_shapes=[], args=(x_res, delta, g, b), sem=("parallel",), rider=rider)


def _ln_bwd_rows(dy, xh, rstd, g):
    dxh = dy * g
    m1 = jnp.mean(dxh, axis=1, keepdims=True)
    m2 = jnp.mean(dxh * xh, axis=1, keepdims=True)
    return rstd * (dxh - m1 - xh * m2)


def _ln_loss_bwd(x_res, delta, g, b, target):
    T, D = x_res.shape
    tm = _pick(T, (256, 128))

    def body(x_ref, d_ref, g_ref, b_ref, t_ref, dz_ref, dzb_ref, loss_ref, dg_ref, db_ref):
        i = pl.program_id(0)

        @pl.when(i == 0)
        def _():
            loss_ref[...] = jnp.zeros_like(loss_ref)
            dg_ref[...] = jnp.zeros_like(dg_ref)
            db_ref[...] = jnp.zeros_like(db_ref)

        z = ALPHA * x_ref[...] + d_ref[...]
        mu = jnp.mean(z, axis=1, keepdims=True)
        zc = z - mu
        var = jnp.mean(zc * zc, axis=1, keepdims=True)
        rstd = lax.rsqrt(var + LN_EPS)
        xh = zc * rstd
        gv = g_ref[...]
        err = xh * gv + b_ref[...] - t_ref[...]
        loss_ref[...] += 0.5 * jnp.sum(jnp.mean(err * err, axis=1, keepdims=True))
        dy = err * (1.0 / D)
        dg_ref[...] += jnp.sum(dy * xh, axis=0, keepdims=True)
        db_ref[...] += jnp.sum(dy, axis=0, keepdims=True)
        dz = _ln_bwd_rows(dy, xh, rstd, gv)
        dz_ref[...] = dz
        dzb_ref[...] = dz.astype(BF16)

    row = pl.BlockSpec((tm, D), lambda i: (i, 0))
    vec = pl.BlockSpec((1, D), lambda i: (0, 0))
    return pl.pallas_call(
        body, name="ln2_loss_bwd",
        out_shape=(jax.ShapeDtypeStruct((T, D), F32), jax.ShapeDtypeStruct((T, D), BF16),
                   jax.ShapeDtypeStruct((8, LANES), F32),
                   jax.ShapeDtypeStruct((1, D), F32), jax.ShapeDtypeStruct((1, D), F32)),
        grid=(T // tm,), in_specs=[row, row, vec, vec, row],
        out_specs=(row, row, pl.BlockSpec((8, LANES), lambda i: (0, 0)), vec, vec),
        compiler_params=_cparams(("arbitrary",)),
    )(x_res, delta, g, b, target)


def _ln_bwd(dy, xh, rstd, g):
    T, D = dy.shape
    tm = _pick(T, (256, 128))

    def body(dy_ref, xh_ref, rs_ref, g_ref, dz_ref, dzb_ref, dg_ref, db_ref):
        i = pl.program_id(0)

        @pl.when(i == 0)
        def _():
            dg_ref[...] = jnp.zeros_like(dg_ref)
            db_ref[...] = jnp.zeros_like(db_ref)

        dyv, xhv = dy_ref[...], xh_ref[...]
        dg_ref[...] += jnp.sum(dyv * xhv, axis=0, keepdims=True)
        db_ref[...] += jnp.sum(dyv, axis=0, keepdims=True)
        dz = _ln_bwd_rows(dyv, xhv, rs_ref[...], g_ref[...])
        dz_ref[...] = dz
        dzb_ref[...] = dz.astype(BF16)

    row = pl.BlockSpec((tm, D), lambda i: (i, 0))
    vec = pl.BlockSpec((1, D), lambda i: (0, 0))
    return pl.pallas_call(
        body, name="ln1_bwd",
        out_shape=(jax.ShapeDtypeStruct((T, D), F32), jax.ShapeDtypeStruct((T, D), BF16),
                   jax.ShapeDtypeStruct((1, D), F32), jax.ShapeDtypeStruct((1, D), F32)),
        grid=(T // tm,), in_specs=[row, row, pl.BlockSpec((tm, 1), lambda i: (i, 0)), vec],
        out_specs=(row, row, vec, vec), compiler_params=_cparams(("arbitrary",)),
    )(dy, xh, rstd, g)


def _ffn_col_tile(T, d_ff):
    return _pick(d_ff, (256, 128)) if T >= 1024 else _pick(d_ff, (512, 256, 128))


def _ffn_gate(gp, cw_ref, cb_ref):
    return (cb_ref[...] + gp * cw_ref[2:3, :] + _shift_down(gp, 1) * cw_ref[1:2, :]
            + _shift_down(gp, 2) * cw_ref[0:1, :])


def _ffn_fwd(up, gpre, conv_w, conv_b, rider=None):
    T, d_ff = up.shape
    ct = _ffn_col_tile(T, d_ff)

    def body(up_ref, gp_ref, cw_ref, cb_ref, f_ref):
        gate = _ffn_gate(gp_ref[...], cw_ref, cb_ref)
        f_ref[...] = (_gelu(gate) * up_ref[...]).astype(BF16)

    col = pl.BlockSpec((T, ct), lambda j: (0, j))
    (f,), carried = _call(
        body, name="ffn_act_fwd", out_shape=[jax.ShapeDtypeStruct((T, d_ff), BF16)], grid=(d_ff // ct,),
        in_specs=[col, col, pl.BlockSpec((3, ct), lambda j: (0, j)), pl.BlockSpec((1, ct), lambda j: (0, j))],
        out_specs=[col], scratch_shapes=[], args=(up, gpre, conv_w, conv_b), sem=("parallel",), rider=rider)
    return f, carried


def _ffn_bwd(up, gpre, conv_w, conv_b, d_f, after=None):
    T, d_ff = up.shape
    ct = _ffn_col_tile(T, d_ff)

    def body(up_ref, gp_ref, cw_ref, cb_ref, df_ref, dup_ref, dgp_ref, dcw_ref, dcb_ref):
        gp = gp_ref[...]
        gate = _ffn_gate(gp, cw_ref, cb_ref)
        gel, dgel = _gelu_and_grad(gate)
        df = df_ref[...]
        dup_ref[...] = (df * gel).astype(BF16)
        dgate = df * up_ref[...] * dgel
        dcb_ref[...] = jnp.sum(dgate, axis=0, keepdims=True)
        dcw_ref[2:3, :] = jnp.sum(dgate * gp, axis=0, keepdims=True)
        dcw_ref[1:2, :] = jnp.sum(dgate * _shift_down(gp, 1), axis=0, keepdims=True)
        dcw_ref[0:1, :] = jnp.sum(dgate * _shift_down(gp, 2), axis=0, keepdims=True)
        dgp = (dgate * cw_ref[2:3, :] + _shift_up(dgate, 1) * cw_ref[1:2, :]
               + _shift_up(dgate, 2) * cw_ref[0:1, :])
        dgp_ref[...] = dgp.astype(BF16)

    col = pl.BlockSpec((T, ct), lambda j: (0, j))
    w3 = pl.BlockSpec((3, ct), lambda j: (0, j))
    v1 = pl.BlockSpec((1, ct), lambda j: (0, j))
    return _call(
        body, name="ffn_act_bwd",
        out_shape=(jax.ShapeDtypeStruct((T, d_ff), BF16), jax.ShapeDtypeStruct((T, d_ff), BF16),
                   jax.ShapeDtypeStruct((3, d_ff), F32), jax.ShapeDtypeStruct((1, d_ff), F32)),
        grid=(d_ff // ct,), in_specs=[col, col, w3, v1, col], out_specs=(col, col, w3, v1),
        scratch_shapes=[], args=(up, gpre, conv_w, conv_b, d_f), sem=("parallel",), after=after)[0]


def _adamw(w, g, m, v, name, after=None):
    R, C = w.shape
    tr = _row_tile(R, C * 4, 8, budget=1280 * 1024)
    c1 = 1.0 / (1.0 - ADAM_B1 ** ADAM_STEP)
    c2 = 1.0 / (1.0 - ADAM_B2 ** ADAM_STEP)

    def body(w_ref, g_ref, m_ref, v_ref, go_ref, d_ref, nm_ref, nv_ref):
        gv = g_ref[...]
        go_ref[...] = gv
        nm = ADAM_B1 * m_ref[...] + (1.0 - ADAM_B1) * gv
        nv = ADAM_B2 * v_ref[...] + (1.0 - ADAM_B2) * (gv * gv)
        nm_ref[...] = nm
        nv_ref[...] = nv
        d_ref[...] = -ADAM_LR * ((nm * c1) / (jnp.sqrt(nv * c2) + ADAM_EPS) + ADAM_WD * w_ref[...])

    blk = pl.BlockSpec((tr, C), lambda r: (r, 0))
    sh = jax.ShapeDtypeStruct((R, C), F32)
    return _call(body, name=name, out_shape=(sh,) * 4, grid=(R // tr,), in_specs=[blk] * 4, out_specs=(blk,) * 4,
                 scratch_shapes=[], args=(w, g, m, v), sem=("parallel",), after=after)[0]


def _group_blocks(w_blocks, per):
    nb, bw, _ = w_blocks.shape
    G = nb // per
    w4 = w_blocks.reshape(G, per, bw, bw)
    rows = []
    for p in range(per):
        parts = [w4[:, p] if q == p else jnp.zeros((G, bw, bw), w_blocks.dtype) for q in range(per)]
        rows.append(jnp.concatenate(parts, axis=2))
    return jnp.concatenate(rows, axis=1)


def _ungroup_blocks(w_groups, per):
    G, gw, _ = w_groups.shape
    bw = gw // per
    blocks = [w_groups[:, p * bw:(p + 1) * bw, p * bw:(p + 1) * bw] for p in range(per)]
    return jnp.stack(blocks, axis=1).reshape(G * per, bw, bw)


def _pack(parts):
    flat = jnp.concatenate([p.reshape(-1).astype(F32) for p in parts])
    n = flat.shape[0]
    rows = -(-n // LANES)
    rows = -(-rows // PACK_ROW_MULT) * PACK_ROW_MULT
    flat = jnp.pad(flat, (0, rows * LANES - n))
    return flat.reshape(rows, LANES)


def _unpack(packed, shapes):
    flat = packed.reshape(-1)
    out, off = [], 0
    for s in shapes:
        n = math.prod(s)
        out.append(flat[off:off + n].reshape(s))
        off += n
    return out


def kernel(x, w_in, b_gate, rnn_conv_w, rnn_conv_b, lru_wa, lru_ba, lru_wi, lru_bi, lru_lambda, attn_sinks, w_attn_proj, w_rnn_proj, w_out, ln1_g, ln1_b, ffn_w_up, ffn_w_gate, ffn_conv_w, ffn_conv_b, ffn_w_down, ln2_g, ln2_b, loss_target, m_w_in, m_b_gate, m_rnn_conv_w, m_rnn_conv_b, m_lru_wa, m_lru_ba, m_lru_wi, m_lru_bi, m_lru_lambda, m_attn_sinks, m_w_attn_proj, m_w_rnn_proj, m_w_out, m_ln1_g, m_ln1_b, m_ffn_w_up, m_ffn_w_gate, m_ffn_conv_w, m_ffn_conv_b, m_ffn_w_down, m_ln2_g, m_ln2_b, v_w_in, v_b_gate, v_rnn_conv_w, v_rnn_conv_b, v_lru_wa, v_lru_ba, v_lru_wi, v_lru_bi, v_lru_lambda, v_attn_sinks, v_w_attn_proj, v_w_rnn_proj, v_w_out, v_ln1_g, v_ln1_b, v_ffn_w_up, v_ffn_w_gate, v_ffn_conv_w, v_ffn_conv_b, v_ffn_w_down, v_ln2_g, v_ln2_b):
    weights = dict(w_in=w_in, b_gate=b_gate, rnn_conv_w=rnn_conv_w, rnn_conv_b=rnn_conv_b, lru_wa=lru_wa,
                   lru_ba=lru_ba, lru_wi=lru_wi, lru_bi=lru_bi, lru_lambda=lru_lambda, attn_sinks=attn_sinks,
                   w_attn_proj=w_attn_proj, w_rnn_proj=w_rnn_proj, w_out=w_out, ln1_g=ln1_g, ln1_b=ln1_b,
                   ffn_w_up=ffn_w_up, ffn_w_gate=ffn_w_gate, ffn_conv_w=ffn_conv_w, ffn_conv_b=ffn_conv_b,
                   ffn_w_down=ffn_w_down, ln2_g=ln2_g, ln2_b=ln2_b)
    m_in = dict(w_in=m_w_in, b_gate=m_b_gate, rnn_conv_w=m_rnn_conv_w, rnn_conv_b=m_rnn_conv_b, lru_wa=m_lru_wa,
                lru_ba=m_lru_ba, lru_wi=m_lru_wi, lru_bi=m_lru_bi, lru_lambda=m_lru_lambda, attn_sinks=m_attn_sinks,
                w_attn_proj=m_w_attn_proj, w_rnn_proj=m_w_rnn_proj, w_out=m_w_out, ln1_g=m_ln1_g, ln1_b=m_ln1_b,
                ffn_w_up=m_ffn_w_up, ffn_w_gate=m_ffn_w_gate, ffn_conv_w=m_ffn_conv_w, ffn_conv_b=m_ffn_conv_b,
                ffn_w_down=m_ffn_w_down, ln2_g=m_ln2_g, ln2_b=m_ln2_b)
    v_in = dict(w_in=v_w_in, b_gate=v_b_gate, rnn_conv_w=v_rnn_conv_w, rnn_conv_b=v_rnn_conv_b, lru_wa=v_lru_wa,
                lru_ba=v_lru_ba, lru_wi=v_lru_wi, lru_bi=v_lru_bi, lru_lambda=v_lru_lambda, attn_sinks=v_attn_sinks,
                w_attn_proj=v_w_attn_proj, w_rnn_proj=v_w_rnn_proj, w_out=v_w_out, ln1_g=v_ln1_g, ln1_b=v_ln1_b,
                ffn_w_up=v_ffn_w_up, ffn_w_gate=v_ffn_w_gate, ffn_conv_w=v_ffn_conv_w, ffn_conv_b=v_ffn_conv_b,
                ffn_w_down=v_ffn_w_down, ln2_g=v_ln2_g, ln2_b=v_ln2_b)
    order = list(weights)

    assert x.shape[0] == 1 and w_in.shape[0] == 1, "one sequence per device, depth 1"
    T, D = x.shape[1], x.shape[2]
    nq = attn_sinks.shape[-1]
    nkv = nq // GROUP
    d_attn, d_kv = nq * HEAD_DIM, nkv * HEAD_DIM
    d_rnn = rnn_conv_b.shape[-1]
    d_ff = ffn_conv_b.shape[-1]
    n_blocks, bw = lru_wa.shape[1], lru_wa.shape[2]
    per = (bw * LANES // math.gcd(bw, LANES)) // bw
    gw = per * bw
    assert n_blocks % per == 0 and d_rnn == n_blocks * bw
    q_off, k_off, v_off = 0, d_attn, d_attn + d_kv
    rx_off = d_attn + 2 * d_kv
    ry_off = rx_off + d_rnn
    gl_off = ry_off + d_rnn
    d_in = gl_off + 2 * D
    assert w_in.shape[-1] * N_SHARDS == d_in
    assert k_off % d_kv == 0 and rx_off % gw == 0 and T % ATTN_BLOCK == 0

    xi, yi, ci = lax.axis_index("x"), lax.axis_index("y"), lax.axis_index("c")
    j_me = 2 * xi + yi
    jc_arr = jnp.stack([j_me, ci]).astype(jnp.int32)

    x0 = x[0]
    tgt = loss_target[0]
    big = ["w_in", "w_attn_proj", "w_rnn_proj", "w_out", "ffn_w_up", "ffn_w_gate", "ffn_w_down"]
    near, diag = (0, 1), (2,)
    order_arr = jnp.stack([j_me, j_me ^ 2, j_me ^ 1, j_me ^ 3]).astype(jnp.int32)

    rcw_s, fcw_s = _all_gather_small([rnn_conv_w[0], ffn_conv_w[0]])
    rcw = jnp.concatenate([rcw_s[j] for j in range(N_SHARDS)], axis=1)
    fcw = jnp.concatenate([fcw_s[j] for j in range(N_SHARDS)], axis=1)

    own = {"w_in": _cast_bf16_into_slot(w_in[0], jc_arr, "cast_w_in", fcw_s)}
    in_near = _gather_step(own["w_in"], fcw_s, "gather_start_w_in", start_d2d=False, relations=near)
    last = in_near[2]
    for n in big[1:]:
        own[n] = last = _cast_bf16_into_slot(weights[n][0], jc_arr, "cast_" + n, last)
    x0b = _cast_bf16(x0, "cast_x", last)

    wa_g = _group_blocks(lru_wa[0], per).astype(BF16)
    wi_g = _group_blocks(lru_wi[0], per).astype(BF16)

    proj = _mm_shards(x0b, in_near[1], order_arr, [0], "mm_proj_own", x0b)
    d2d_sems, relay_sems, buf, tok = _gather_relay_step(in_near[1], proj, "gather_forward_w_in_near", in_near[0])
    w_in_s = _gather_step(buf, tok, "gather_finish_w_in_near", sems_in=d2d_sems, relations=near)
    proj = _mm_shards(x0b, w_in_s, order_arr, [1, 2], "mm_proj_near", w_in_s, out=proj)
    d2d_sems, buf, tok = _gather_relay_step(w_in_s, proj, "gather_forward_w_in_diag", None, relay_in=relay_sems)
    w_in_s = _gather_step(buf, tok, "gather_finish_w_in_diag", sems_in=d2d_sems, relations=diag)
    proj = _mm_shards(x0b, w_in_s, order_arr, [3], "mm_proj_diag", w_in_s, out=proj)
    started, last = _gather_start_all([own[n] for n in big[1:]], proj, "gather_start_all")
    ici = dict(zip(big[1:], started))

    def forward_halves(n, after):
        sems, buf = ici[n]
        return _gather_step(buf, after, "gather_forward_" + n, sems_in=sems, start_d2d=True)

    def gathered(d2d, after, n):
        sems, buf, _ = d2d
        return _gather_step(buf, after, "gather_finish_" + n, sems_in=sems)

    a_out = _attn_fwd(proj, attn_sinks, nq, (q_off, k_off, v_off), after=last)
    fw_ap = forward_halves("w_attn_proj", a_out)
    (b_out, h_all), _ = _rnn_fwd(proj, (rx_off, ry_off), rcw, rnn_conv_b, wa_g, wi_g, lru_ba, lru_bi, lru_lambda)
    fw_rp = forward_halves("w_rnn_proj", b_out)
    w_ap = gathered(fw_ap, b_out, "w_attn_proj").reshape(d_attn, D)
    y_attn = _mm(a_out, w_ap, name="mm_attn_proj")
    fw_o = forward_halves("w_out", y_attn)
    w_rp = gathered(fw_rp, y_attn, "w_rnn_proj").reshape(d_rnn, D)
    y_rnn = _mm(b_out, w_rp, name="mm_rnn_proj")
    merged, _ = _merge_fwd(proj, gl_off, b_gate, y_attn, y_rnn)
    w_o = gathered(fw_o, merged, "w_out").reshape(D, D)
    mix = _mm(merged, w_o, name="mm_out")
    fw_up = forward_halves("ffn_w_up", mix)
    (x1, x1b, xh1, rstd1), _ = _ln_fwd(x0, mix, ln1_g, ln1_b, "ln1_fwd")
    w_up_s = gathered(fw_up, x1b, "ffn_w_up")
    up = _mm(x1b, w_up_s, name="mm_up", b_shards=N_SHARDS)
    fw_gate = forward_halves("ffn_w_gate", up)
    w_gate_s = gathered(fw_gate, fw_gate[2], "ffn_w_gate")
    gpre = _mm(x1b, w_gate_s, name="mm_gate", b_shards=N_SHARDS)
    f_act, _ = _ffn_fwd(up, gpre, fcw, ffn_conv_b)
    fw_dn = forward_halves("ffn_w_down", f_act)
    w_dn = gathered(fw_dn, fw_dn[2], "ffn_w_down").reshape(d_ff, D)
    f_out = _mm(f_act, w_dn, name="mm_down")
    dz2, dz2b, loss_acc, dg2, db2 = _ln_loss_bwd(x1, f_out, ln2_g, ln2_b, tgt)

    def pair_sums(arrs, from_sibling, names):
        return [_pair_sum(g, la, jc_arr, "pair_sum_" + n) for g, la, n in zip(arrs, from_sibling, names)]

    def shard_sums(parts, landed, names):
        return [_shard_sum(cp, lb, jc_arr, "shard_sum_" + n) for cp, lb, n in zip(parts, landed, names)]

    halves = {}
    g_down = _mm(f_act, dz2b, name="mm_d_w_down", ta=True, out_dtype=BF16)
    g1 = [g_down.reshape(N_SHARDS, d_ff // N_SHARDS, D)]
    d_f, sib1 = _mm(dz2b, w_dn, name="mm_d_f", tb=True, rider=_pair_rider(g1))
    sent1 = _shard_exchange_start(pair_sums(g1, sib1, ["ffn_w_down"]), "shard_exchange_start_down")
    dup, dgp, d_fcw, d_fcb = _ffn_bwd(up, gpre, fcw, ffn_conv_b, d_f, after=sent1[4])
    g_up = _mm(x1b, dup, name="mm_d_w_up", ta=True, out_dtype=BF16, out_shards=N_SHARDS)
    g_gate = _mm(x1b, dgp, name="mm_d_w_gate", ta=True, out_dtype=BF16, out_shards=N_SHARDS)
    g2 = [g_up, g_gate]
    dx1_a, sib2 = _mm(dup, w_up_s, name="mm_dx1_up", tb=True, b_shards=N_SHARDS, adds=((ALPHA, dz2),),
                      rider=_pair_rider(g2))
    halves["ffn_w_down"], = shard_sums(*_shard_exchange_wait(sent1, dx1_a, "shard_exchange_wait_down"),
                                       ["ffn_w_down"])
    sent2 = _shard_exchange_start(pair_sums(g2, sib2, ["ffn_w_up", "ffn_w_gate"]), "shard_exchange_start_up_gate")
    dx1 = _mm(dgp, w_gate_s, name="mm_dx1_gate", tb=True, b_shards=N_SHARDS, adds=((1.0, dx1_a),), after=sent2[4])
    dz1, dz1b, dg1, db1 = _ln_bwd(dx1, xh1, rstd1, ln1_g)
    g_out = _mm(merged, dz1b, name="mm_d_w_out", ta=True, out_dtype=BF16)
    d_m = _mm(dz1b, w_o, name="mm_d_merged", tb=True)
    dya, dyr, dgl_a, dgl_r, dbg_a, dbg_r = _merge_bwd(proj, gl_off, b_gate, y_attn, y_rnn, d_m)
    g_ap = _mm(a_out, dya, name="mm_d_w_attn_proj", ta=True, out_dtype=BF16)
    g_rp = _mm(b_out, dyr, name="mm_d_w_rnn_proj", ta=True, out_dtype=BF16)
    names3 = ["w_out", "w_attn_proj", "w_rnn_proj"]
    g3 = [g_out.reshape(N_SHARDS, D // N_SHARDS, D), g_ap.reshape(N_SHARDS, d_attn // N_SHARDS, D),
          g_rp.reshape(N_SHARDS, d_rnn // N_SHARDS, D)]
    d_a = _mm(dya, w_ap, name="mm_d_attn", tb=True)
    d_b, sib3 = _mm(dyr, w_rp, name="mm_d_rnn", tb=True, rider=_pair_rider(g3))
    sent3 = _shard_exchange_start(pair_sums(g3, sib3, names3), "shard_exchange_start_mixers")
    dq, dk, dv, dsink = _attn_bwd(proj, d_a, attn_sinks, nq, (q_off, k_off, v_off), after=sent3[4])
    (drx, dry, d_rcw, d_rcb, d_ba, d_bi, d_lam, d_wa_g, d_wi_g), _ = _rnn_bwd(
        proj, (rx_off, ry_off), h_all, d_b, rcw, rnn_conv_b, wa_g, wi_g, lru_ba, lru_bi, lru_lambda)
    halves["ffn_w_up"], halves["ffn_w_gate"] = shard_sums(
        *_shard_exchange_wait(sent2, drx, "shard_exchange_wait_up_gate"), ["ffn_w_up", "ffn_w_gate"])
    d_proj = jnp.concatenate([dq, dk.astype(BF16), dv.astype(BF16), drx, dry, dgl_a, dgl_r], axis=1)
    ffn_names = ["ffn_w_down", "ffn_w_up", "ffn_w_gate"]
    g_in, shared_ffn = _mm(x0b, d_proj, name="mm_d_w_in", ta=True, out_dtype=BF16, out_shards=N_SHARDS,
                           rider=_share_rider([halves[n] for n in ffn_names]))
    halves["w_out"], halves["w_attn_proj"], halves["w_rnn_proj"] = shard_sums(
        *_shard_exchange_wait(sent3, g_in, "shard_exchange_wait_mixers"), names3)

    small_parts = [
        ("loss", loss_acc[0:1, 0:1]),
        ("b_gate", jnp.concatenate([dbg_a, dbg_r], axis=1)),
        ("rnn_conv_w", d_rcw), ("rnn_conv_b", d_rcb),
        ("lru_wa", _ungroup_blocks(d_wa_g, per)), ("lru_ba", d_ba),
        ("lru_wi", _ungroup_blocks(d_wi_g, per)), ("lru_bi", d_bi), ("lru_lambda", d_lam),
        ("attn_sinks", dsink[0:1, 0:nq]),
        ("ln1_g", dg1), ("ln1_b", db1),
        ("ffn_conv_w", d_fcw), ("ffn_conv_b", d_fcb),
        ("ln2_g", dg2), ("ln2_b", db2),
    ]
    packed = _pack([p for _, p in small_parts])
    rs = packed.shape[0]

    def whole(g):
        return g.reshape(2 * g.shape[1], g.shape[2])

    grads = {n: whole(g) for n, g in zip(ffn_names, shared_ffn)}
    out_g, out_d, out_m, out_v = {}, {}, {}, {}

    def adamw(n, after=None):
        shape = weights[n].shape
        two_d = (math.prod(shape[:-1]), shape[-1])
        g2, d2, m2, v2 = _adamw(weights[n].reshape(two_d), grads[n].reshape(two_d), m_in[n].reshape(two_d),
                                v_in[n].reshape(two_d), "adamw_" + n, after=after)
        out_g[n], out_d[n] = g2.reshape(shape), d2.reshape(shape)
        out_m[n], out_v[n] = m2.reshape(shape), v2.reshape(shape)

    g4 = [g_in, packed.reshape(N_SHARDS, rs // N_SHARDS, LANES)]
    sib4 = _run_rider(_pair_rider(g4), "pair_exchange_in_small")
    part4 = pair_sums(g4, sib4, ["w_in", "small"])
    grad_x, (lb_in, lb_small, *shared_mix) = _mm(
        d_proj, w_in_s, name="mm_d_x", tb=True, b_shards=N_SHARDS, adds=((ALPHA, dz1),),
        rider=_join_riders(_shard_exchange_rider(part4, _atoms([0], near) + _atoms([1])),
                           _share_rider([halves[n] for n in names3])))
    grads.update({n: whole(g) for n, g in zip(names3, shared_mix)})
    sent5 = _shard_exchange_start(part4[:1], "shard_exchange_start_in_diag", relations=diag, lands=[lb_in])
    for n in ffn_names + names3:
        adamw(n, after=sent5[4])
    (part_in,), (lb_in,) = _shard_exchange_wait(sent5, out_d[names3[-1]], "shard_exchange_wait_in_diag")
    part_small = part4[1]
    halves["w_in"], = shard_sums([part_in], [lb_in], ["w_in"])
    eighths = _shard_sum(part_small, lb_small, jc_arr, "shard_sum_small", all_slots=True)
    shared_in, reduced = _run_rider(_share_rider([halves["w_in"]], eighths), "share_in_small")
    grads["w_in"] = whole(shared_in)
    reduced = reduced.reshape(rs, LANES)
    small = dict(zip([n for n, _ in small_parts], _unpack(reduced, [p.shape for _, p in small_parts])))
    loss = small.pop("loss").reshape(())
    rcw_n = d_rnn // N_SHARDS
    fcw_n = d_ff // N_SHARDS
    small["rnn_conv_w"] = lax.dynamic_slice(small["rnn_conv_w"], (0, j_me * rcw_n), (4, rcw_n))
    small["ffn_conv_w"] = lax.dynamic_slice(small["ffn_conv_w"], (0, j_me * fcw_n), (3, fcw_n))
    for n, g in small.items():
        grads[n] = g

    for n in order:
        if n not in out_g:
            adamw(n)

    return (loss, grad_x.reshape(x.shape), *[out_g[n] for n in order], *[out_d[n] for n in order],
            *[out_m[n] for n in order], *[out_v[n] for n in order])
```

```python
import functools
import math

import jax
import jax.numpy as jnp
from jax import lax
from jax.experimental import pallas as pl
from jax.experimental.pallas import tpu as pltpu

F32 = jnp.float32
BF16 = jnp.bfloat16
MESH = pl.DeviceIdType.MESH

HEAD_DIM = 64
GROUP = 8
ATTN_BLOCK = 128
LRU_C = 8.0
LN_EPS = 1e-5
ALPHA = 2.0 ** 0.25
LANES = 128
N_SHARDS = 4
N_DEV = 8
VMEM_LIMIT = 56 * 1024 * 1024
MM_VMEM_BUDGET = 40 * 1024 * 1024
MM_MAX_TILE = 3072
PACK_ROW_MULT = 8 * 64
NEG = -1e30

ADAM_LR, ADAM_B1, ADAM_B2, ADAM_EPS, ADAM_WD, ADAM_STEP = 0.001, 0.9, 0.999, 1e-08, 0.01, 10

GELU_C = math.sqrt(2.0 / math.pi)
GELU_A = 0.044715


def _cparams(sem=None):
    kw = dict(vmem_limit_bytes=VMEM_LIMIT)
    if sem is not None:
        kw["dimension_semantics"] = sem
    return pltpu.CompilerParams(**kw)


def _pick(n, prefs):
    for p in prefs:
        if n % p == 0:
            return p
    return n


def _row_tile(rows, row_bytes, mult, budget=2 * 1024 * 1024):
    best = None
    for d in range(mult, rows + 1, mult):
        if rows % d == 0 and d * row_bytes <= budget:
            best = d
    return best if best is not None else rows


def _gelu(x):
    return 0.5 * x * (1.0 + jnp.tanh(GELU_C * (x + GELU_A * x * x * x)))


def _gelu_and_grad(x):
    t = jnp.tanh(GELU_C * (x + GELU_A * x * x * x))
    g = 0.5 * x * (1.0 + t)
    dg = 0.5 * (1.0 + t) + 0.5 * x * (1.0 - t * t) * GELU_C * (1.0 + 3.0 * GELU_A * x * x)
    return g, dg


def _shift_down(x, s, fill=0.0):
    row = lax.broadcasted_iota(jnp.int32, x.shape, 0)
    return jnp.where(row >= s, pltpu.roll(x, s, 0), fill)


def _shift_up(x, s, fill=0.0):
    n = x.shape[0]
    row = lax.broadcasted_iota(jnp.int32, x.shape, 0)
    return jnp.where(row < n - s, pltpu.roll(x, n - s, 0), fill)


def _mm(a, b, *, name, ta=False, tb=False, out_dtype=F32, adds=(), b_shards=1, out_shards=1,
        tm=None, tn=None, tk=None, rider=None, after=None):
    if ta:
        K, M = a.shape
    else:
        M, K = a.shape
    if b_shards > 1:
        n_sh = b.shape[-1]
        if tb:
            N = b.shape[1]
            assert b_shards * n_sh == K
        else:
            N = b_shards * n_sh
            assert b.shape[1] == K
    else:
        n_sh = None
        if tb:
            N = b.shape[0]
            assert b.shape[1] == K
        else:
            N = b.shape[1]
            assert b.shape[0] == K
    wide = (1024, 1536, 1280, 768, 640, 512, 256, 128)
    if tn is None:
        if b_shards > 1 and not tb:
            tn = n_sh if n_sh <= MM_MAX_TILE else _pick(n_sh, wide)
        elif out_shards > 1:
            tn = N // out_shards if N // out_shards <= MM_MAX_TILE else _pick(N // out_shards, wide)
        else:
            tn = _pick(N, wide)
    if tk is None:
        if b_shards > 1 and tb:
            tk = n_sh if n_sh <= MM_MAX_TILE else _pick(n_sh, wide)
        else:
            tk = K if K <= MM_MAX_TILE else _pick(K, (2048,) + wide)
    assert N % tn == 0 and K % tk == 0, (name, M, N, K, tn, tk)
    nk = K // tk
    n_add = len(adds)
    sa, sb, so = a.dtype.itemsize, b.dtype.itemsize, jnp.dtype(out_dtype).itemsize

    def vmem_bytes(tm_):
        return (2 * (tm_ * tk * sa + tk * tn * sb + tm_ * tn * so + n_add * tm_ * tn * 4)
                + (tm_ * tn * 4 if nk > 1 else 0))

    if tm is None:
        tm = _pick(M, (1024, 512, 256, 128)) if nk > 1 else _pick(M, (512, 256, 128))
        while vmem_bytes(tm) > MM_VMEM_BUDGET and tm % 256 == 0:
            tm //= 2
    assert M % tm == 0, (name, M, tm)
    b_outer = b.size * sb >= a.size * sa

    def ij(g0, g1):
        return (g1, g0) if b_outer else (g0, g1)

    def amap(g0, g1, k):
        i, _ = ij(g0, g1)
        return (k, i) if ta else (i, k)

    def bmap(g0, g1, k):
        _, j = ij(g0, g1)
        if b_shards > 1 and not tb:
            per = n_sh // tn
            return (j // per, k, j % per)
        if b_shards > 1 and tb:
            per = n_sh // tk
            return (k // per, j, k % per)
        return (j, k) if tb else (k, j)

    def omap(g0, g1, k):
        i, j = ij(g0, g1)
        if out_shards > 1:
            per_o = (N // out_shards) // tn
            return (j // per_o, i, j % per_o)
        return (i, j)

    a_spec = pl.BlockSpec((tk, tm) if ta else (tm, tk), amap)
    if b_shards > 1:
        b_spec = pl.BlockSpec((None, tn, tk) if tb else (None, tk, tn), bmap)
    else:
        b_spec = pl.BlockSpec((tn, tk) if tb else (tk, tn), bmap)
    add_specs = [pl.BlockSpec((tm, tn), lambda g0, g1, k: ij(g0, g1)) for _ in adds]
    if out_shards > 1:
        out_spec = pl.BlockSpec((None, tm, tn), omap)
        out_shape = jax.ShapeDtypeStruct((out_shards, M, N // out_shards), out_dtype)
    else:
        out_spec = pl.BlockSpec((tm, tn), omap)
        out_shape = jax.ShapeDtypeStruct((M, N), out_dtype)

    if ta:
        dims = (((0,), (0,)), ((), ()))
    elif tb:
        dims = (((1,), (1,)), ((), ()))
    else:
        dims = (((1,), (0,)), ((), ()))
    scales = tuple(s for s, _ in adds)

    def finish(r, add_refs, o_ref):
        for s, ref in zip(scales, add_refs):
            r = r + s * ref[...].astype(F32)
        o_ref[...] = r.astype(out_dtype)

    def body(a_ref, b_ref, *rest):
        add_refs = rest[:n_add]
        o_ref = rest[n_add]
        part = lax.dot_general(a_ref[...].astype(BF16), b_ref[...].astype(BF16), dims, preferred_element_type=F32)
        if nk == 1:
            finish(part, add_refs, o_ref)
            return
        acc = rest[n_add + 1]
        k = pl.program_id(2)

        @pl.when(k == 0)
        def _():
            acc[...] = part

        @pl.when(k > 0)
        def _():
            acc[...] += part

        @pl.when(k == nk - 1)
        def _():
            finish(acc[...], add_refs, o_ref)

    grid = (N // tn, M // tm, nk) if b_outer else (M // tm, N // tn, nk)
    (res,), carried = _call(
        body, name=name, grid=grid, in_specs=[a_spec, b_spec] + add_specs, out_specs=[out_spec],
        out_shape=[out_shape], scratch_shapes=[pltpu.VMEM((tm, tn), F32)] if nk > 1 else [],
        args=(a, b, *[x for _, x in adds]), sem=("parallel", "parallel", "arbitrary"), rider=rider, after=after)
    return (res, carried) if rider is not None else res


def _cast_bf16(w, name, after):
    R, C = w.shape
    tr = _row_tile(R, C * 4, 16)

    def body(w_ref, after_ref, o_ref):
        o_ref[...] = w_ref[...].astype(BF16)

    return pl.pallas_call(
        body, name=name, out_shape=jax.ShapeDtypeStruct((R, C), BF16), grid=(R // tr,),
        in_specs=[pl.BlockSpec((tr, C), lambda r: (r, 0)), pl.BlockSpec(memory_space=pl.ANY)],
        out_specs=pl.BlockSpec((tr, C), lambda r: (r, 0)), compiler_params=_cparams(("parallel",)),
    )(w, after)


def _cast_bf16_into_slot(w, jc_arr, name, after):
    R, C = w.shape
    tr = _row_tile(R, C * 4, 16)

    def body(jc_ref, w_ref, after_ref, o_ref):
        o_ref[...] = w_ref[...].astype(BF16)

    gs = pltpu.PrefetchScalarGridSpec(
        num_scalar_prefetch=1, grid=(R // tr,),
        in_specs=[pl.BlockSpec((tr, C), lambda r, jc: (r, 0)), pl.BlockSpec(memory_space=pl.ANY)],
        out_specs=pl.BlockSpec((None, tr, C), lambda r, jc: (jc[0], r, 0)))
    return pl.pallas_call(body, name=name, out_shape=jax.ShapeDtypeStruct((N_SHARDS, R, C), BF16), grid_spec=gs,
                          compiler_params=_cparams(("parallel",)))(jc_arr, w, after)


def _pair_sum(g, la, jc_arr, name):
    S, R, C = g.shape
    half = R // 2
    tr = _row_tile(half, C * 4, 16)
    nrt = half // tr
    dt = g.dtype

    def body(jc_ref, g_ref, la_ref, o_ref):
        o_ref[...] = (g_ref[...].astype(F32) + la_ref[...].astype(F32)).astype(dt)

    gs = pltpu.PrefetchScalarGridSpec(
        num_scalar_prefetch=1, grid=(S, nrt),
        in_specs=[pl.BlockSpec((None, tr, C), lambda s, r, jc: (s, jc[1] * nrt + r, 0)),
                  pl.BlockSpec((None, tr, C), lambda s, r, jc: (s, r, 0))],
        out_specs=pl.BlockSpec((None, tr, C), lambda s, r, jc: (s, r, 0)))
    return pl.pallas_call(body, name=name, out_shape=jax.ShapeDtypeStruct((S, half, C), dt), grid_spec=gs,
                          compiler_params=_cparams(("parallel", "parallel")))(jc_arr, g, la)


def _shard_sum(cp, lb, jc_arr, name, all_slots=False):
    S, h, C = cp.shape
    tr = _row_tile(h, C * 4, 16)

    def body(jc_ref, cp_ref, l0, l1, l2, o_ref):
        o_ref[...] = ((cp_ref[...].astype(F32) + l0[...].astype(F32)) + l1[...].astype(F32)) + l2[...].astype(F32)

    def lspec(kk):
        return pl.BlockSpec((None, tr, C), lambda r, jc: (kk, r, 0))

    if all_slots:
        out_spec = pl.BlockSpec((None, None, tr, C), lambda r, jc: (jc[0], jc[1], r, 0))
        out_shape = jax.ShapeDtypeStruct((S, 2, h, C), F32)
    else:
        out_spec = pl.BlockSpec((None, tr, C), lambda r, jc: (jc[1], r, 0))
        out_shape = jax.ShapeDtypeStruct((2, h, C), F32)
    gs = pltpu.PrefetchScalarGridSpec(
        num_scalar_prefetch=1, grid=(h // tr,),
        in_specs=[pl.BlockSpec((None, tr, C), lambda r, jc: (jc[0], r, 0)), lspec(0), lspec(1), lspec(2)],
        out_specs=out_spec)
    return pl.pallas_call(body, name=name, out_shape=out_shape, grid_spec=gs,
                          compiler_params=_cparams(("parallel",)))(jc_arr, cp, lb, lb, lb)


ANY = pl.BlockSpec(memory_space=pl.ANY)


def _place():
    x, y, c = lax.axis_index("x"), lax.axis_index("y"), lax.axis_index("c")
    chips = [(1 - x, y), (x, 1 - y), (1 - x, 1 - y)]
    return x, y, c, chips


class _Rider:
    def __init__(self, inputs, out_shape, aliases, sems, start, finish):
        self.inputs, self.out_shape, self.aliases, self.sems = list(inputs), list(out_shape), dict(aliases), list(sems)
        self.start, self.finish = start, finish


def _join_riders(r1, r2):
    i1, o1, s1 = len(r1.inputs), len(r1.out_shape), len(r1.sems)
    aliases = dict(r1.aliases)
    aliases.update({i1 + i: o1 + o for i, o in r2.aliases.items()})

    def start(ins, outs, sems):
        r1.start(ins[:i1], outs[:o1], sems[:s1])
        r2.start(ins[i1:], outs[o1:], sems[s1:])

    def finish(ins, outs, sems):
        r1.finish(ins[:i1], outs[:o1], sems[:s1])
        r2.finish(ins[i1:], outs[o1:], sems[s1:])

    return _Rider(r1.inputs + r2.inputs, r1.out_shape + r2.out_shape, aliases, r1.sems + r2.sems, start, finish)


def _after_rider(x):
    return _Rider([x], [], {}, [], lambda *a: None, lambda *a: None)


def _call(body, *, name, grid, in_specs, out_specs, out_shape, scratch_shapes, args, sem, rider=None, after=None):
    out_specs, out_shape = tuple(out_specs), tuple(out_shape)
    if after is not None:
        rider = _after_rider(after) if rider is None else _join_riders(_after_rider(after), rider)
    if rider is None:
        res = pl.pallas_call(body, name=name, out_shape=out_shape, grid=grid, in_specs=list(in_specs),
                             out_specs=out_specs, scratch_shapes=list(scratch_shapes),
                             compiler_params=_cparams(sem))(*args)
        return tuple(res), []
    n_in, n_out, n_sc = len(in_specs), len(out_specs), len(scratch_shapes)
    r_in, r_out = len(rider.inputs), len(rider.out_shape)

    def wrapped(*refs):
        p = 0
        host_in = refs[p:p + n_in]; p += n_in
        rid_in = refs[p:p + r_in]; p += r_in
        host_out = refs[p:p + n_out]; p += n_out
        rid_out = refs[p:p + r_out]; p += r_out
        host_sc = refs[p:p + n_sc]; p += n_sc
        rid_sem = refs[p:]
        first = functools.reduce(jnp.logical_and, [pl.program_id(a) == 0 for a in range(len(grid))])
        last = functools.reduce(jnp.logical_and, [pl.program_id(a) == grid[a] - 1 for a in range(len(grid))])

        @pl.when(first)
        def _():
            rider.start(rid_in, rid_out, rid_sem)

        body(*host_in, *host_out, *host_sc)

        @pl.when(last)
        def _():
            rider.finish(rid_in, rid_out, rid_sem)

    res = pl.pallas_call(
        wrapped, name=name, out_shape=out_shape + tuple(rider.out_shape), grid=grid,
        in_specs=list(in_specs) + [ANY] * r_in, out_specs=out_specs + (ANY,) * r_out,
        input_output_aliases={n_in + i: n_out + o for i, o in rider.aliases.items()},
        scratch_shapes=list(scratch_shapes) + rider.sems,
        compiler_params=_cparams(("arbitrary",) * len(grid)),
    )(*args, *rider.inputs)
    return tuple(res[:n_out]), list(res[n_out:])


def _run_rider(rider, name):
    def body(*refs):
        r_in, r_out = len(rider.inputs), len(rider.out_shape)
        ins, outs, sems = refs[:r_in], refs[r_in:r_in + r_out], refs[r_in + r_out:]
        rider.start(ins, outs, sems)
        rider.finish(ins, outs, sems)

    return pl.pallas_call(
        body, name=name, out_shape=rider.out_shape, in_specs=[ANY] * len(rider.inputs),
        out_specs=[ANY] * len(rider.out_shape), input_output_aliases=rider.aliases, scratch_shapes=rider.sems,
    )(*rider.inputs)


def _atoms(indices, kks=(0, 1, 2), q=0, nq=1):
    return [(i, kk, q, nq) for i in indices for kk in kks]


def _mm_shards(a, buf, order_arr, which, name, after, out=None):
    M, K = a.shape
    S, _, n = buf.shape
    tm = _pick(M, (512, 256, 128))
    s0 = which[0]

    def body(order_ref, a_ref, b_ref, *rest):
        rest[-1][...] = jnp.dot(a_ref[...], b_ref[...], preferred_element_type=F32)

    gs = pltpu.PrefetchScalarGridSpec(
        num_scalar_prefetch=1, grid=(len(which), M // tm),
        in_specs=[pl.BlockSpec((tm, K), lambda g, i, order: (i, 0)),
                  pl.BlockSpec((None, K, n), lambda g, i, order: (order[s0 + g], 0, 0)), ANY]
        + ([ANY] if out is not None else []),
        out_specs=pl.BlockSpec((tm, n), lambda g, i, order: (i, order[s0 + g])))
    return pl.pallas_call(
        body, name=name, grid_spec=gs, out_shape=jax.ShapeDtypeStruct((M, S * n), F32),
        input_output_aliases={4: 0} if out is not None else {},
        compiler_params=_cparams(("arbitrary", "arbitrary")),
    )(order_arr, a, buf, after, *([out] if out is not None else []))


def _all_gather_small(shards):
    n = len(shards)

    def body(*refs):
        w = refs[:n]
        out = refs[n:2 * n]
        local_sem, s_sem, r_sem = refs[2 * n:]
        x, y, c, chips = _place()
        j_me = 2 * x + y
        cps = []
        for i in range(n):
            lc = pltpu.make_async_copy(w[i], out[i].at[j_me], local_sem.at[i])
            lc.start()
            cps.append(lc)
        sends = []
        for i in range(n):
            for kk, (px, py) in enumerate(chips):
                cp = pltpu.make_async_remote_copy(
                    src_ref=w[i], dst_ref=out[i].at[j_me], send_sem=s_sem.at[3 * i + kk],
                    recv_sem=r_sem.at[3 * i + kk], device_id=(px, py, c), device_id_type=MESH)
                cp.start()
                sends.append(cp)
        for i in range(n):
            for kk, (px, py) in enumerate(chips):
                sends[3 * i + kk].wait_send()
                pltpu.make_async_remote_copy(
                    src_ref=w[i], dst_ref=out[i].at[2 * px + py], send_sem=s_sem.at[3 * i + kk],
                    recv_sem=r_sem.at[3 * i + kk], device_id=(px, py, c), device_id_type=MESH).wait_recv()
        for lc in cps:
            lc.wait()

    out_shape = [jax.ShapeDtypeStruct((N_SHARDS,) + s.shape, s.dtype) for s in shards]
    return pl.pallas_call(
        body, name="all_gather_conv_weights", out_shape=out_shape, in_specs=[ANY] * n, out_specs=[ANY] * n,
        scratch_shapes=[pltpu.SemaphoreType.DMA((n,)), pltpu.SemaphoreType.DMA((3 * n,)),
                        pltpu.SemaphoreType.DMA((3 * n,))],
    )(*shards)


def _pair_rider(grads):
    n = len(grads)

    def copies(g, la, sems):
        x, y, c, _ = _place()
        return [pltpu.make_async_remote_copy(
            src_ref=g[i].at[:, pl.ds((1 - c) * (g[i].shape[1] // 2), g[i].shape[1] // 2), :], dst_ref=la[i],
            send_sem=sems[0].at[i], recv_sem=sems[1].at[i], device_id=(x, y, 1 - c), device_id_type=MESH)
            for i in range(n)]

    def start(g, la, sems):
        for cp in copies(g, la, sems):
            cp.start()

    def finish(g, la, sems):
        for cp in copies(g, la, sems):
            cp.wait()

    return _Rider(grads, [jax.ShapeDtypeStruct((s.shape[0], s.shape[1] // 2, s.shape[2]), s.dtype) for s in grads],
                  {}, [pltpu.SemaphoreType.DMA((n,)), pltpu.SemaphoreType.DMA((n,))], start, finish)


def _shard_exchange_rider(cps_in, atoms=None):
    n = len(cps_in)
    if atoms is None:
        atoms = _atoms(range(n))

    def copies(ins, lb, sems):
        x, y, c, chips = _place()
        out = []
        for a, (i, kk, q, nq) in enumerate(atoms):
            h = ins[i].shape[1]
            assert h % (16 * nq) == 0, (h, nq)
            rows = pl.ds(q * (h // nq), h // nq)
            px, py = chips[kk]
            out.append(pltpu.make_async_remote_copy(
                src_ref=ins[i].at[2 * px + py, rows, :], dst_ref=lb[i].at[kk, rows, :],
                send_sem=sems[0].at[a], recv_sem=sems[1].at[a], device_id=(px, py, c), device_id_type=MESH))
        return out

    def start(ins, lb, sems):
        for cp in copies(ins, lb, sems):
            cp.start()

    def finish(ins, lb, sems):
        for cp in copies(ins, lb, sems):
            cp.wait()

    return _Rider(cps_in, [jax.ShapeDtypeStruct((3,) + s.shape[1:], s.dtype) for s in cps_in], {},
                  [pltpu.SemaphoreType.DMA((len(atoms),)), pltpu.SemaphoreType.DMA((len(atoms),))], start, finish)


HBM = pl.BlockSpec(memory_space=pltpu.HBM)
SEM = pl.BlockSpec(memory_space=pltpu.SEMAPHORE)


def _shard_copies(part_refs, land_refs, send_sems, recv_sems, relations):
    x, y, c, chips = _place()
    nr = len(relations)
    return [pltpu.make_async_remote_copy(
        src_ref=part_refs[i].at[2 * chips[kk][0] + chips[kk][1]], dst_ref=land_refs[i].at[kk],
        send_sem=send_sems.at[nr * i + r], recv_sem=recv_sems.at[nr * i + r],
        device_id=(chips[kk][0], chips[kk][1], c), device_id_type=MESH)
        for i in range(len(part_refs)) for r, kk in enumerate(relations)]


SIDE_EFFECT = pltpu.SideEffectType.DATAFLOW_SIDE_EFFECTING


def _shard_exchange_start(parts, name, relations=(0, 1, 2), lands=None):
    n = len(parts)
    ns = n * len(relations)

    def body(*refs):
        part_refs, land_refs = refs[:n], refs[n:2 * n]
        send_sems, recv_sems = refs[2 * n], refs[2 * n + 1]
        token = refs[4 * n + 2]
        for cp in _shard_copies(part_refs, land_refs, send_sems, recv_sems, relations):
            cp.start()
        token[...] = jnp.zeros_like(token)

    if lands is None:
        lands = [lax.empty((3,) + p.shape[1:], p.dtype) for p in parts]
    bufs = list(parts) + list(lands)
    res = pl.pallas_call(
        body, name=name,
        out_shape=(pltpu.SemaphoreType.DMA((ns,)), pltpu.SemaphoreType.DMA((ns,)),
                   *[pltpu.HBM(b.shape, b.dtype) for b in bufs], jax.ShapeDtypeStruct((8, LANES), F32)),
        in_specs=(HBM,) * (2 * n), out_specs=(SEM, SEM) + (HBM,) * (2 * n) + (pl.BlockSpec(memory_space=pltpu.VMEM),),
        input_output_aliases={i: 2 + i for i in range(2 * n)},
        compiler_params=pltpu.CompilerParams(has_side_effects=SIDE_EFFECT),
    )(*[pltpu.with_memory_space_constraint(b, pltpu.HBM) for b in bufs])
    return res[0], res[1], list(res[2:2 + n]), list(res[2 + n:2 + 2 * n]), res[2 + 2 * n], relations


def _shard_exchange_wait(started, after, name):
    send_sems, recv_sems, parts, lands, _, relations = started
    n = len(parts)

    def body(*refs):
        part_refs, land_refs = refs[:n], refs[n:2 * n]
        send_sems_ref, recv_sems_ref = refs[2 * n], refs[2 * n + 1]
        for cp in _shard_copies(part_refs, land_refs, send_sems_ref, recv_sems_ref, relations):
            cp.wait_send()
            cp.wait_recv()

    bufs = parts + lands
    res = pl.pallas_call(
        body, name=name, out_shape=tuple(pltpu.HBM(b.shape, b.dtype) for b in bufs),
        in_specs=(HBM,) * (2 * n) + (SEM, SEM, ANY), out_specs=(HBM,) * (2 * n),
        input_output_aliases={i: i for i in range(2 * n)},
        compiler_params=pltpu.CompilerParams(has_side_effects=SIDE_EFFECT),
    )(*bufs, send_sems, recv_sems, after)
    return list(res[:n]), list(res[n:])


def _gather_copies(buf_ref, send_sems, recv_sems, over_d2d, arriving, relations):
    x, y, c, chips = _place()
    half = buf_ref.shape[1] // 2
    out = []
    for r, kk in enumerate(relations):
        px, py = chips[kk]
        if over_d2d:
            slot, core, peer = 2 * px + py, (1 - c) if arriving else c, (x, y, 1 - c)
        else:
            slot, core, peer = (2 * px + py) if arriving else (2 * x + y), c, (px, py, c)
        blk = buf_ref.at[slot, pl.ds(core * half, half), :]
        out.append(pltpu.make_async_remote_copy(src_ref=blk, dst_ref=blk, send_sem=send_sems.at[r],
                                                recv_sem=recv_sems.at[r], device_id=peer, device_id_type=MESH))
    return out


def _gather_step(buf, after, name, sems_in=None, start_d2d=None, relations=(0, 1, 2)):
    n_sem = 0 if sems_in is None else 2

    def body(*refs):
        buf_ref = refs[0]
        ins = refs[1:1 + n_sem]
        outs = refs[2 + n_sem:]
        if sems_in is not None:
            waited_d2d = start_d2d is None
            for mine, theirs in zip(_gather_copies(buf_ref, ins[0], ins[1], waited_d2d, False, relations),
                                    _gather_copies(buf_ref, ins[0], ins[1], waited_d2d, True, relations)):
                theirs.wait_recv()
                mine.wait_send()
        if start_d2d is not None:
            for cp in _gather_copies(buf_ref, outs[0], outs[1], start_d2d, False, relations):
                cp.start()
            outs[3][...] = jnp.zeros_like(outs[3])

    nr = len(relations)
    sem_out = () if start_d2d is None else (pltpu.SemaphoreType.DMA((nr,)), pltpu.SemaphoreType.DMA((nr,)))
    tok_out = () if start_d2d is None else (jax.ShapeDtypeStruct((8, LANES), F32),)
    res = pl.pallas_call(
        body, name=name,
        out_shape=sem_out + (pltpu.HBM(buf.shape, buf.dtype),) + tok_out,
        in_specs=(HBM,) + (SEM,) * n_sem + (ANY,),
        out_specs=(SEM,) * len(sem_out) + (HBM,) + (pl.BlockSpec(memory_space=pltpu.VMEM),) * len(tok_out),
        input_output_aliases={0: len(sem_out)},
        compiler_params=pltpu.CompilerParams(has_side_effects=SIDE_EFFECT),
    )(pltpu.with_memory_space_constraint(buf, pltpu.HBM), *(sems_in or ()), after)
    if start_d2d is None:
        return res[0]
    return (res[0], res[1]), res[2], res[3]


def _gather_start_all(bufs, after, name):
    n = len(bufs)

    def body(*refs):
        outs = refs[n + 1:]
        for i in range(n):
            for cp in _gather_copies(refs[i], outs[2 * i], outs[2 * i + 1], False, False, (0, 1, 2)):
                cp.start()
        outs[-1][...] = jnp.zeros_like(outs[-1])

    res = pl.pallas_call(
        body, name=name,
        out_shape=(pltpu.SemaphoreType.DMA((3,)),) * (2 * n) + tuple(pltpu.HBM(b.shape, b.dtype) for b in bufs)
        + (jax.ShapeDtypeStruct((8, LANES), F32),),
        in_specs=(HBM,) * n + (ANY,),
        out_specs=(SEM,) * (2 * n) + (HBM,) * n + (pl.BlockSpec(memory_space=pltpu.VMEM),),
        input_output_aliases={i: 2 * n + i for i in range(n)},
        compiler_params=pltpu.CompilerParams(has_side_effects=SIDE_EFFECT),
    )(*[pltpu.with_memory_space_constraint(b, pltpu.HBM) for b in bufs], after)
    return [((res[2 * i], res[2 * i + 1]), res[2 * n + i]) for i in range(n)], res[3 * n]


def _relay_copies(buf_ref, send_sems, recv_sems, arriving):
    x, y, c, chips = _place()
    quarter = buf_ref.shape[1] // 4
    out = []
    for r, (src_kk, dst_kk) in enumerate(((0, 1), (1, 0))):
        slot = (2 * chips[2][0] + chips[2][1]) if arriving else (2 * chips[src_kk][0] + chips[src_kk][1])
        blk = buf_ref.at[slot, pl.ds((2 * c + r) * quarter, quarter), :]
        out.append(pltpu.make_async_remote_copy(
            src_ref=blk, dst_ref=blk, send_sem=send_sems.at[r], recv_sem=recv_sems.at[r],
            device_id=(chips[dst_kk][0], chips[dst_kk][1], c), device_id_type=MESH))
    return out


def _gather_relay_step(buf, after, name, sems_in, relay_in=None):
    first = relay_in is None
    ins_sems = sems_in if first else relay_in
    near, diag = (0, 1), (2,)

    def body(*refs):
        buf_ref, in_s, in_r = refs[0], refs[1], refs[2]
        outs = refs[4:]
        if first:
            for mine, theirs in zip(_gather_copies(buf_ref, in_s, in_r, False, False, near),
                                    _gather_copies(buf_ref, in_s, in_r, False, True, near)):
                theirs.wait_recv()
                mine.wait_send()
            for cp in _gather_copies(buf_ref, outs[0], outs[1], True, False, near):
                cp.start()
            for cp in _relay_copies(buf_ref, outs[2], outs[3], False):
                cp.start()
        else:
            for mine, theirs in zip(_relay_copies(buf_ref, in_s, in_r, False), _relay_copies(buf_ref, in_s, in_r, True)):
                theirs.wait_recv()
                mine.wait_send()
            for cp in _gather_copies(buf_ref, outs[0], outs[1], True, False, diag):
                cp.start()
        outs[-1][...] = jnp.zeros_like(outs[-1])

    def sem(n):
        return pltpu.SemaphoreType.DMA((n,))

    sem_out = (sem(2), sem(2), sem(2), sem(2)) if first else (sem(1), sem(1))
    res = pl.pallas_call(
        body, name=name,
        out_shape=sem_out + (pltpu.HBM(buf.shape, buf.dtype), jax.ShapeDtypeStruct((8, LANES), F32)),
        in_specs=(HBM, SEM, SEM, ANY),
        out_specs=(SEM,) * len(sem_out) + (HBM, pl.BlockSpec(memory_space=pltpu.VMEM)),
        input_output_aliases={0: len(sem_out)},
        compiler_params=pltpu.CompilerParams(has_side_effects=SIDE_EFFECT),
    )(pltpu.with_memory_space_constraint(buf, pltpu.HBM), *ins_sems, after)
    if first:
        return (res[0], res[1]), (res[2], res[3]), res[4], res[5]
    return (res[0], res[1]), res[2], res[3]


def _share_rider(halves, eighths=None):
    n = len(halves)
    bufs = list(halves) + ([eighths] if eighths is not None else [])

    def half_copy(out, sems, i, core):
        x, y, c, _ = _place()
        blk = out[i].at[core]
        return pltpu.make_async_remote_copy(src_ref=blk, dst_ref=blk, send_sem=sems[0].at[i], recv_sem=sems[1].at[i],
                                            device_id=(x, y, 1 - c), device_id_type=MESH)

    def eighth_copy(out, sems, r, mine):
        x, y, c, _ = _place()
        px, py, pc = x ^ ((r >> 2) & 1), y ^ ((r >> 1) & 1), c ^ (r & 1)
        blk = out[n].at[2 * x + y, c] if mine else out[n].at[2 * px + py, pc]
        return pltpu.make_async_remote_copy(src_ref=blk, dst_ref=blk, send_sem=sems[2].at[r - 1],
                                            recv_sem=sems[3].at[r - 1], device_id=(px, py, pc), device_id_type=MESH)

    def start(ins, out, sems):
        c = lax.axis_index("c")
        for i in range(n):
            half_copy(out, sems, i, c).start()
        if eighths is not None:
            for r in range(1, N_DEV):
                eighth_copy(out, sems, r, True).start()

    def finish(ins, out, sems):
        c = lax.axis_index("c")
        for i in range(n):
            half_copy(out, sems, i, 1 - c).wait_recv()
        if eighths is not None:
            for r in range(1, N_DEV):
                eighth_copy(out, sems, r, False).wait_recv()
        for i in range(n):
            half_copy(out, sems, i, c).wait_send()
        if eighths is not None:
            for r in range(1, N_DEV):
                eighth_copy(out, sems, r, True).wait_send()

    return _Rider(bufs, [jax.ShapeDtypeStruct(s.shape, s.dtype) for s in bufs], {i: i for i in range(len(bufs))},
                  [pltpu.SemaphoreType.DMA((max(n, 1),)), pltpu.SemaphoreType.DMA((max(n, 1),)),
                   pltpu.SemaphoreType.DMA((N_DEV - 1,)), pltpu.SemaphoreType.DMA((N_DEV - 1,))], start, finish)


ATTN_ROWS = GROUP * ATTN_BLOCK
ATTN_KEYS = 2 * ATTN_BLOCK


def _attn_geometry(n):
    row = lax.broadcasted_iota(jnp.int32, (ATTN_ROWS, ATTN_KEYS), 0)
    col = lax.broadcasted_iota(jnp.int32, (ATTN_ROWS, ATTN_KEYS), 1)
    dist = ATTN_BLOCK + jnp.bitwise_and(row, ATTN_BLOCK - 1) - col
    valid = jnp.logical_and(jnp.logical_and(dist >= 0, dist < ATTN_BLOCK),
                            jnp.logical_or(col >= ATTN_BLOCK, n > 0))
    return dist.astype(F32), valid


def _per_head_column(values):
    head = lax.broadcasted_iota(jnp.int32, (ATTN_ROWS, 1), 0) // ATTN_BLOCK
    col = jnp.zeros((ATTN_ROWS, 1), F32)
    for hh, v in enumerate(values):
        col = jnp.where(head == hh, v, col)
    return col


def _stack_heads(ref, g):
    return jnp.concatenate(
        [ref[:, (g * GROUP + hh) * HEAD_DIM:(g * GROUP + hh + 1) * HEAD_DIM].astype(BF16) for hh in range(GROUP)],
        axis=0)


def _attn_probs(q_s, k2, slope_col, sink_col, dist, valid):
    s = lax.dot_general(q_s, k2, (((1,), (1,)), ((), ())), preferred_element_type=F32) * (HEAD_DIM ** -0.5)
    s = jnp.where(valid, s - slope_col * dist, NEG)
    m = jnp.maximum(jnp.max(s, axis=1, keepdims=True), sink_col)
    e = jnp.exp(s - m)
    es = jnp.exp(sink_col - m)
    inv = 1.0 / (jnp.sum(e, axis=1, keepdims=True) + es)
    return e * inv, es * inv


def _attn_specs(T, d_attn, d_kv, q_blk, k_blk, v_blk):
    bq = pl.BlockSpec((ATTN_BLOCK, d_attn), lambda n: (n, q_blk))
    kp = pl.BlockSpec((ATTN_BLOCK, d_kv), lambda n: (jnp.maximum(n - 1, 0), k_blk))
    kc = pl.BlockSpec((ATTN_BLOCK, d_kv), lambda n: (n, k_blk))
    vp = pl.BlockSpec((ATTN_BLOCK, d_kv), lambda n: (jnp.maximum(n - 1, 0), v_blk))
    vc = pl.BlockSpec((ATTN_BLOCK, d_kv), lambda n: (n, v_blk))
    return bq, kp, kc, vp, vc


def _attn_fwd(proj, sinks, nq, cols, after=None):
    T = proj.shape[0]
    nkv = nq // GROUP
    d_attn, d_kv = nq * HEAD_DIM, nkv * HEAD_DIM
    q_off, k_off, v_off = cols
    bq, kp, kc, vp, vc = _attn_specs(T, d_attn, d_kv, q_off // d_attn, k_off // d_kv, v_off // d_kv)

    def body(sink_ref, q_ref, kp_ref, kc_ref, vp_ref, vc_ref, o_ref):
        n = pl.program_id(0)
        dist, valid = _attn_geometry(n)
        for g in range(nkv):
            ks = slice(g * HEAD_DIM, (g + 1) * HEAD_DIM)
            k2 = jnp.concatenate([kp_ref[:, ks], kc_ref[:, ks]], axis=0).astype(BF16)
            v2 = jnp.concatenate([vp_ref[:, ks], vc_ref[:, ks]], axis=0).astype(BF16)
            slope_col = _per_head_column([2.0 ** (-8.0 * (g * GROUP + hh + 1) / nq) for hh in range(GROUP)])
            sink_col = _per_head_column([sink_ref[0, g * GROUP + hh] for hh in range(GROUP)])
            p, _ = _attn_probs(_stack_heads(q_ref, g), k2, slope_col, sink_col, dist, valid)
            o = jnp.dot(p.astype(BF16), v2, preferred_element_type=F32).astype(BF16)
            for hh in range(GROUP):
                h = g * GROUP + hh
                o_ref[:, h * HEAD_DIM:(h + 1) * HEAD_DIM] = o[hh * ATTN_BLOCK:(hh + 1) * ATTN_BLOCK, :]

    (out,), carried = _call(
        body, name="attn_fwd", out_shape=[jax.ShapeDtypeStruct((T, d_attn), BF16)], grid=(T // ATTN_BLOCK,),
        in_specs=[pl.BlockSpec(memory_space=pltpu.SMEM), bq, kp, kc, vp, vc],
        out_specs=[pl.BlockSpec((ATTN_BLOCK, d_attn), lambda n: (n, 0))], scratch_shapes=[],
        args=(sinks, proj, proj, proj, proj, proj), sem=("parallel",), after=after)
    return out


def _attn_bwd(proj, d_attn_out, sinks, nq, cols, after=None):
    T = proj.shape[0]
    nkv = nq // GROUP
    d_attn, d_kv = nq * HEAD_DIM, nkv * HEAD_DIM
    q_off, k_off, v_off = cols
    bq, kp, kc, vp, vc = _attn_specs(T, d_attn, d_kv, q_off // d_attn, k_off // d_kv, v_off // d_kv)
    scale = HEAD_DIM ** -0.5
    dn_t = (((1,), (1,)), ((), ()))
    dn_r = (((0,), (0,)), ((), ()))

    def body(sink_ref, q_ref, kp_ref, kc_ref, vp_ref, vc_ref, do_ref, dq_ref, dk_ref, dv_ref, ds_ref):
        n = pl.program_id(0)

        @pl.when(n == 0)
        def _():
            dk_ref[...] = jnp.zeros_like(dk_ref)
            dv_ref[...] = jnp.zeros_like(dv_ref)
            ds_ref[...] = jnp.zeros_like(ds_ref)

        dist, valid = _attn_geometry(n)
        rows_c = pl.ds(pl.multiple_of(n * ATTN_BLOCK, ATTN_BLOCK), ATTN_BLOCK)
        rows_p = pl.ds(pl.multiple_of(jnp.maximum(n - 1, 0) * ATTN_BLOCK, ATTN_BLOCK), ATTN_BLOCK)
        lane = lax.broadcasted_iota(jnp.int32, ds_ref.shape, 1)
        srow = lax.broadcasted_iota(jnp.int32, ds_ref.shape, 0)
        ds_acc = jnp.zeros(ds_ref.shape, F32)
        for g in range(nkv):
            ks = slice(g * HEAD_DIM, (g + 1) * HEAD_DIM)
            k2 = jnp.concatenate([kp_ref[:, ks], kc_ref[:, ks]], axis=0).astype(BF16)
            v2 = jnp.concatenate([vp_ref[:, ks], vc_ref[:, ks]], axis=0).astype(BF16)
            slope_col = _per_head_column([2.0 ** (-8.0 * (g * GROUP + hh + 1) / nq) for hh in range(GROUP)])
            sink_col = _per_head_column([sink_ref[0, g * GROUP + hh] for hh in range(GROUP)])
            q_s = _stack_heads(q_ref, g)
            do_s = _stack_heads(do_ref, g)
            p, p_sink = _attn_probs(q_s, k2, slope_col, sink_col, dist, valid)
            dp = lax.dot_general(do_s, v2, dn_t, preferred_element_type=F32)
            delta = jnp.sum(p * dp, axis=1, keepdims=True)
            ds = (p * (dp - delta)).astype(BF16)
            sink_part = p_sink * delta
            dq = (jnp.dot(ds, k2, preferred_element_type=F32) * scale).astype(BF16)
            for hh in range(GROUP):
                h = g * GROUP + hh
                blk = slice(hh * ATTN_BLOCK, (hh + 1) * ATTN_BLOCK)
                dq_ref[:, h * HEAD_DIM:(h + 1) * HEAD_DIM] = dq[blk, :]
                ds_acc = ds_acc + jnp.where(jnp.logical_and(lane == h, srow == 0), -jnp.sum(sink_part[blk, :]), 0.0)
            dk2 = lax.dot_general(ds, q_s, dn_r, preferred_element_type=F32) * scale
            dv2 = lax.dot_general(p.astype(BF16), do_s, dn_r, preferred_element_type=F32)
            dk_ref[rows_p, ks] += dk2[:ATTN_BLOCK, :]
            dv_ref[rows_p, ks] += dv2[:ATTN_BLOCK, :]
            dk_ref[rows_c, ks] += dk2[ATTN_BLOCK:, :]
            dv_ref[rows_c, ks] += dv2[ATTN_BLOCK:, :]
        ds_ref[...] += ds_acc

    out_shape = (jax.ShapeDtypeStruct((T, d_attn), BF16), jax.ShapeDtypeStruct((T, d_kv), F32),
                 jax.ShapeDtypeStruct((T, d_kv), F32), jax.ShapeDtypeStruct((8, LANES), F32))
    return _call(
        body, name="attn_bwd", out_shape=out_shape, grid=(T // ATTN_BLOCK,),
        in_specs=[pl.BlockSpec(memory_space=pltpu.SMEM), bq, kp, kc, vp, vc,
                  pl.BlockSpec((ATTN_BLOCK, d_attn), lambda n: (n, 0))],
        out_specs=(pl.BlockSpec((ATTN_BLOCK, d_attn), lambda n: (n, 0)),
                   pl.BlockSpec((T, d_kv), lambda n: (0, 0)), pl.BlockSpec((T, d_kv), lambda n: (0, 0)),
                   pl.BlockSpec((8, LANES), lambda n: (0, 0))),
        scratch_shapes=[], args=(sinks, proj, proj, proj, proj, proj, d_attn_out), sem=("arbitrary",), after=after)[0]


def _rnn_tile(T):
    return _pick(T, (512, 256, 128))


def _rnn_gates(x_ext, cw_ref, cb_ref, wa_ref, wi_ref, ba_ref, bi_ref, lam_ref, tt):
    xs = [pltpu.roll(x_ext, 3 - k, 0)[8:, :] if k < 3 else x_ext[8:, :] for k in range(4)]
    cx = cb_ref[...] + xs[0] * cw_ref[0:1, :]
    for k in range(1, 4):
        cx = cx + xs[k] * cw_ref[k:k + 1, :]
    cxb = cx.astype(BF16)
    r = jax.nn.sigmoid(jnp.dot(cxb, wa_ref[...], preferred_element_type=F32) + ba_ref[...])
    i = jax.nn.sigmoid(jnp.dot(cxb, wi_ref[...], preferred_element_type=F32) + bi_ref[...])
    lam = lam_ref[...]
    sp = jnp.maximum(-lam, 0.0) + jnp.log1p(jnp.exp(-jnp.abs(lam)))
    log_a = -LRU_C * r * sp
    a = jnp.exp(log_a)
    z = 2.0 * log_a
    em1 = jnp.where(z > -1e-2, z * (1.0 + z * (0.5 + z * (1.0 / 6.0 + z * (1.0 / 24.0)))), jnp.exp(z) - 1.0)
    s = jnp.sqrt(-em1)
    return xs, cx, r, i, sp, a, s


def _rnn_specs(T, gw, tt, rx_blk, ry_blk, rev):
    nT = T // tt
    hb = tt // 8

    def tile(t):
        return (nT - 1 - t) if rev else t

    rx = pl.BlockSpec((tt, gw), lambda g, t: (tile(t), rx_blk + g))
    rx_halo = pl.BlockSpec((8, gw), lambda g, t: (jnp.maximum(tile(t) * hb - 1, 0), rx_blk + g))
    ry = pl.BlockSpec((tt, gw), lambda g, t: (tile(t), ry_blk + g))
    cw = pl.BlockSpec((4, gw), lambda g, t: (0, g))
    vec = pl.BlockSpec((1, gw), lambda g, t: (0, g))
    wg = pl.BlockSpec((None, gw, gw), lambda g, t: (g, 0, 0))
    act = pl.BlockSpec((tt, gw), lambda g, t: (tile(t), g))
    act_halo = pl.BlockSpec((8, gw), lambda g, t: (jnp.maximum(tile(t) * hb - 1, 0), g))
    return rx, rx_halo, ry, cw, vec, wg, act, act_halo, tile


def _rnn_fwd(proj, cols, conv_w, conv_b, wa_g, wi_g, ba, bi, lam, rider=None):
    T = proj.shape[0]
    G, gw, _ = wa_g.shape
    d_rnn = G * gw
    tt = _rnn_tile(T)
    rx_off, ry_off = cols
    rx, rx_halo, ry, cw, vec, wg, act, _, _ = _rnn_specs(T, gw, tt, rx_off // gw, ry_off // gw, False)

    def body(rx_ref, rxh_ref, ry_ref, cw_ref, cb_ref, wa_ref, wi_ref, ba_ref, bi_ref, lam_ref,
             b_ref, h_ref, carry):
        t = pl.program_id(1)

        @pl.when(t == 0)
        def _():
            carry[...] = jnp.zeros_like(carry)

        halo = jnp.where(t > 0, rxh_ref[...], 0.0)
        x_ext = jnp.concatenate([halo, rx_ref[...]], axis=0)
        _, cx, _, i, _, a, s = _rnn_gates(x_ext, cw_ref, cb_ref, wa_ref, wi_ref, ba_ref, bi_ref, lam_ref, tt)
        acc_a, acc_b = a, s * (i * cx)
        d = 1
        while d < tt:
            acc_b = acc_a * _shift_down(acc_b, d, 0.0) + acc_b
            acc_a = acc_a * _shift_down(acc_a, d, 1.0)
            d *= 2
        h = acc_b + acc_a * carry[7:8, :]
        carry[...] = h[tt - 8:, :]
        h_ref[...] = h
        b_ref[...] = (h * _gelu(ry_ref[...])).astype(BF16)

    return _call(
        body, name="rnn_fwd",
        out_shape=(jax.ShapeDtypeStruct((T, d_rnn), BF16), jax.ShapeDtypeStruct((T, d_rnn), F32)),
        grid=(G, T // tt),
        in_specs=[rx, rx_halo, ry, cw, vec, wg, wg, vec, vec, vec], out_specs=(act, act),
        scratch_shapes=[pltpu.VMEM((8, gw), F32)],
        args=(proj, proj, proj, conv_w, conv_b, wa_g, wi_g, ba, bi, lam), sem=("parallel", "arbitrary"), rider=rider)


def _rnn_bwd(proj, cols, h_all, d_b, conv_w, conv_b, wa_g, wi_g, ba, bi, lam, rider=None):
    T = proj.shape[0]
    G, gw, _ = wa_g.shape
    d_rnn = G * gw
    tt = _rnn_tile(T)
    nT = T // tt
    rx_off, ry_off = cols
    rx, rx_halo, ry, cw, vec, wg, act, act_halo, _ = _rnn_specs(T, gw, tt, rx_off // gw, ry_off // gw, True)
    dn_t = (((1,), (1,)), ((), ()))
    dn_r = (((0,), (0,)), ((), ()))

    def body(rx_ref, rxh_ref, ry_ref, h_ref, hh_ref, db_ref, cw_ref, cb_ref, wa_ref, wi_ref, ba_ref, bi_ref, lam_ref,
             drx_ref, dry_ref, dcw_ref, dcb_ref, dba_ref, dbi_ref, dlam_ref, dwa_ref, dwi_ref,
             lam_carry, dcx_carry):
        t = pl.program_id(1)
        first_tile = t == nT - 1

        @pl.when(t == 0)
        def _():
            lam_carry[...] = jnp.zeros_like(lam_carry)
            dcx_carry[...] = jnp.zeros_like(dcx_carry)
            dcw_ref[...] = jnp.zeros_like(dcw_ref)
            dcb_ref[...] = jnp.zeros_like(dcb_ref)
            dba_ref[...] = jnp.zeros_like(dba_ref)
            dbi_ref[...] = jnp.zeros_like(dbi_ref)
            dlam_ref[...] = jnp.zeros_like(dlam_ref)
            dwa_ref[...] = jnp.zeros_like(dwa_ref)
            dwi_ref[...] = jnp.zeros_like(dwi_ref)

        halo = jnp.where(first_tile, 0.0, rxh_ref[...])
        x_ext = jnp.concatenate([halo, rx_ref[...]], axis=0)
        xs, cx, r, i, sp, a, s = _rnn_gates(x_ext, cw_ref, cb_ref, wa_ref, wi_ref, ba_ref, bi_ref, lam_ref, tt)
        h = h_ref[...]
        h_halo = jnp.where(first_tile, 0.0, hh_ref[...])
        h_prev = pltpu.roll(jnp.concatenate([h_halo, h], axis=0), 1, 0)[8:, :]
        gel, dgel = _gelu_and_grad(ry_ref[...])
        d_b_t = db_ref[...]
        dry_ref[...] = (d_b_t * h * dgel).astype(BF16)
        dh = d_b_t * gel

        acc_c = _shift_up(a, 1, 1.0)
        acc_l = dh
        d = 1
        while d < tt:
            acc_l = acc_c * _shift_up(acc_l, d, 0.0) + acc_l
            acc_c = acc_c * _shift_up(acc_c, d, 1.0)
            d *= 2
        lam_t = acc_l + acc_c * lam_carry[0:1, :]
        lam_carry[...] = (a * lam_t)[0:8, :]

        icx = i * cx
        d_s = lam_t * icx
        d_i = lam_t * s * cx
        dcx = lam_t * s * i
        d_a = lam_t * h_prev - d_s * (a / s)
        dlog_a = d_a * a
        d_r = dlog_a * (-LRU_C * sp)
        lam = lam_ref[...]
        dlam_ref[...] += jnp.sum(dlog_a * r, axis=0, keepdims=True) * (LRU_C * jax.nn.sigmoid(-lam))
        dpr = d_r * r * (1.0 - r)
        dpi = d_i * i * (1.0 - i)
        dba_ref[...] += jnp.sum(dpr, axis=0, keepdims=True)
        dbi_ref[...] += jnp.sum(dpi, axis=0, keepdims=True)
        cxb = cx.astype(BF16)
        dprb, dpib = dpr.astype(BF16), dpi.astype(BF16)
        dwa_ref[...] += lax.dot_general(cxb, dprb, dn_r, preferred_element_type=F32)
        dwi_ref[...] += lax.dot_general(cxb, dpib, dn_r, preferred_element_type=F32)
        dcx = (dcx + lax.dot_general(dprb, wa_ref[...], dn_t, preferred_element_type=F32)
               + lax.dot_general(dpib, wi_ref[...], dn_t, preferred_element_type=F32))

        dcb_ref[...] += jnp.sum(dcx, axis=0, keepdims=True)
        for k in range(4):
            dcw_ref[k:k + 1, :] += jnp.sum(dcx * xs[k], axis=0, keepdims=True)
        d_ext = jnp.concatenate([dcx, dcx_carry[...]], axis=0)
        drx = dcx * cw_ref[3:4, :]
        for k in range(3):
            drx = drx + pltpu.roll(d_ext, tt + 8 - (3 - k), 0)[:tt, :] * cw_ref[k:k + 1, :]
        drx_ref[...] = drx.astype(BF16)
        dcx_carry[...] = dcx[0:8, :]

    out_shape = (jax.ShapeDtypeStruct((T, d_rnn), BF16), jax.ShapeDtypeStruct((T, d_rnn), BF16),
                 jax.ShapeDtypeStruct((4, d_rnn), F32), jax.ShapeDtypeStruct((1, d_rnn), F32),
                 jax.ShapeDtypeStruct((1, d_rnn), F32), jax.ShapeDtypeStruct((1, d_rnn), F32),
                 jax.ShapeDtypeStruct((1, d_rnn), F32), jax.ShapeDtypeStruct((G, gw, gw), F32),
                 jax.ShapeDtypeStruct((G, gw, gw), F32))
    return _call(
        body, name="rnn_bwd", out_shape=out_shape, grid=(G, nT),
        in_specs=[rx, rx_halo, ry, act, act_halo, act, cw, vec, wg, wg, vec, vec, vec],
        out_specs=(act, act, cw, vec, vec, vec, vec, wg, wg),
        scratch_shapes=[pltpu.VMEM((8, gw), F32), pltpu.VMEM((8, gw), F32)],
        args=(proj, proj, proj, h_all, h_all, d_b, conv_w, conv_b, wa_g, wi_g, ba, bi, lam),
        sem=("parallel", "arbitrary"), rider=rider)


def _merge_fwd(proj, gl_off, b_gate, y_attn, y_rnn, rider=None):
    T, D = y_attn.shape
    tm = _pick(T, (256, 128))
    ct = _pick(math.gcd(gl_off, D), (512, 256, 128))
    oa, orr, nd = gl_off // ct, (gl_off + D) // ct, D // ct

    def body(ga_ref, gr_ref, ba_ref, br_ref, ya_ref, yr_ref, m_ref):
        ga = jax.nn.sigmoid(ga_ref[...] + ba_ref[...])
        gr = jax.nn.sigmoid(gr_ref[...] + br_ref[...])
        m_ref[...] = (ga * ya_ref[...] + gr * yr_ref[...]).astype(BF16)

    blk = pl.BlockSpec((tm, ct), lambda i, j: (i, j))
    (merged,), carried = _call(
        body, name="merge_fwd", out_shape=[jax.ShapeDtypeStruct((T, D), BF16)], grid=(T // tm, nd),
        in_specs=[pl.BlockSpec((tm, ct), lambda i, j: (i, oa + j)), pl.BlockSpec((tm, ct), lambda i, j: (i, orr + j)),
                  pl.BlockSpec((1, ct), lambda i, j: (0, j)), pl.BlockSpec((1, ct), lambda i, j: (0, nd + j)),
                  blk, blk],
        out_specs=[blk], scratch_shapes=[], args=(proj, proj, b_gate, b_gate, y_attn, y_rnn),
        sem=("parallel", "parallel"), rider=rider)
    return merged, carried


def _merge_bwd(proj, gl_off, b_gate, y_attn, y_rnn, d_m):
    T, D = y_attn.shape
    tm = _pick(T, (256, 128))
    ct = _pick(math.gcd(gl_off, D), (512, 256, 128))
    oa, orr, nd = gl_off // ct, (gl_off + D) // ct, D // ct

    def body(ga_ref, gr_ref, ba_ref, br_ref, ya_ref, yr_ref, dm_ref,
             dya_ref, dyr_ref, dga_ref, dgr_ref, dba_ref, dbr_ref):
        i = pl.program_id(1)

        @pl.when(i == 0)
        def _():
            dba_ref[...] = jnp.zeros_like(dba_ref)
            dbr_ref[...] = jnp.zeros_like(dbr_ref)

        ga = jax.nn.sigmoid(ga_ref[...] + ba_ref[...])
        gr = jax.nn.sigmoid(gr_ref[...] + br_ref[...])
        dm = dm_ref[...]
        dya_ref[...] = (dm * ga).astype(BF16)
        dyr_ref[...] = (dm * gr).astype(BF16)
        dga = dm * ya_ref[...] * ga * (1.0 - ga)
        dgr = dm * yr_ref[...] * gr * (1.0 - gr)
        dga_ref[...] = dga.astype(BF16)
        dgr_ref[...] = dgr.astype(BF16)
        dba_ref[...] += jnp.sum(dga, axis=0, keepdims=True)
        dbr_ref[...] += jnp.sum(dgr, axis=0, keepdims=True)

    blk = pl.BlockSpec((tm, ct), lambda j, i: (i, j))
    vec = pl.BlockSpec((1, ct), lambda j, i: (0, j))
    act = jax.ShapeDtypeStruct((T, D), BF16)
    v1 = jax.ShapeDtypeStruct((1, D), F32)
    return pl.pallas_call(
        body, name="merge_bwd", out_shape=(act, act, act, act, v1, v1), grid=(nd, T // tm),
        in_specs=[pl.BlockSpec((tm, ct), lambda j, i: (i, oa + j)), pl.BlockSpec((tm, ct), lambda j, i: (i, orr + j)),
                  vec, pl.BlockSpec((1, ct), lambda j, i: (0, nd + j)), blk, blk, blk],
        out_specs=(blk, blk, blk, blk, vec, vec),
        compiler_params=_cparams(("parallel", "arbitrary")),
    )(proj, proj, b_gate, b_gate, y_attn, y_rnn, d_m)


def _ln_fwd(x_res, delta, g, b, name, rider=None):
    T, D = x_res.shape
    tm = _pick(T, (256, 128))

    def body(x_ref, d_ref, g_ref, b_ref, y_ref, yb_ref, xh_ref, rs_ref):
        z = ALPHA * x_ref[...] + d_ref[...]
        mu = jnp.mean(z, axis=1, keepdims=True)
        zc = z - mu
        var = jnp.mean(zc * zc, axis=1, keepdims=True)
        rstd = lax.rsqrt(var + LN_EPS)
        xh = zc * rstd
        xh_ref[...] = xh
        rs_ref[...] = rstd
        y = xh * g_ref[...] + b_ref[...]
        y_ref[...] = y
        yb_ref[...] = y.astype(BF16)

    row = pl.BlockSpec((tm, D), lambda i: (i, 0))
    vec = pl.BlockSpec((1, D), lambda i: (0, 0))
    return _call(
        body, name=name,
        out_shape=(jax.ShapeDtypeStruct((T, D), F32), jax.ShapeDtypeStruct((T, D), BF16),
                   jax.ShapeDtypeStruct((T, D), F32), jax.ShapeDtypeStruct((T, 1), F32)),
        grid=(T // tm,), in_specs=[row, row, vec, vec],
        out_specs=(row, row, row, pl.BlockSpec((tm, 1), lambda i: (i, 0))),
        scratch_shapes=[], args=(x_res, delta, g, b), sem=("parallel",), rider=rider)


def _ln_bwd_rows(dy, xh, rstd, g):
    dxh = dy * g
    m1 = jnp.mean(dxh, axis=1, keepdims=True)
    m2 = jnp.mean(dxh * xh, axis=1, keepdims=True)
    return rstd * (dxh - m1 - xh * m2)


def _ln_loss_bwd(x_res, delta, g, b, target):
    T, D = x_res.shape
    tm = _pick(T, (256, 128))

    def body(x_ref, d_ref, g_ref, b_ref, t_ref, dz_ref, dzb_ref, loss_ref, dg_ref, db_ref):
        i = pl.program_id(0)

        @pl.when(i == 0)
        def _():
            loss_ref[...] = jnp.zeros_like(loss_ref)
            dg_ref[...] = jnp.zeros_like(dg_ref)
            db_ref[...] = jnp.zeros_like(db_ref)

        z = ALPHA * x_ref[...] + d_ref[...]
        mu = jnp.mean(z, axis=1, keepdims=True)
        zc = z - mu
        var = jnp.mean(zc * zc, axis=1, keepdims=True)
        rstd = lax.rsqrt(var + LN_EPS)
        xh = zc * rstd
        gv = g_ref[...]
        err = xh * gv + b_ref[...] - t_ref[...]
        loss_ref[...] += 0.5 * jnp.sum(jnp.mean(err * err, axis=1, keepdims=True))
        dy = err * (1.0 / D)
        dg_ref[...] += jnp.sum(dy * xh, axis=0, keepdims=True)
        db_ref[...] += jnp.sum(dy, axis=0, keepdims=True)
        dz = _ln_bwd_rows(dy, xh, rstd, gv)
        dz_ref[...] = dz
        dzb_ref[...] = dz.astype(BF16)

    row = pl.BlockSpec((tm, D), lambda i: (i, 0))
    vec = pl.BlockSpec((1, D), lambda i: (0, 0))
    return pl.pallas_call(
        body, name="ln2_loss_bwd",
        out_shape=(jax.ShapeDtypeStruct((T, D), F32), jax.ShapeDtypeStruct((T, D), BF16),
                   jax.ShapeDtypeStruct((8, LANES), F32),
                   jax.ShapeDtypeStruct((1, D), F32), jax.ShapeDtypeStruct((1, D), F32)),
        grid=(T // tm,), in_specs=[row, row, vec, vec, row],
        out_specs=(row, row, pl.BlockSpec((8, LANES), lambda i: (0, 0)), vec, vec),
        compiler_params=_cparams(("arbitrary",)),
    )(x_res, delta, g, b, target)


def _ln_bwd(dy, xh, rstd, g):
    T, D = dy.shape
    tm = _pick(T, (256, 128))

    def body(dy_ref, xh_ref, rs_ref, g_ref, dz_ref, dzb_ref, dg_ref, db_ref):
        i = pl.program_id(0)

        @pl.when(i == 0)
        def _():
            dg_ref[...] = jnp.zeros_like(dg_ref)
            db_ref[...] = jnp.zeros_like(db_ref)

        dyv, xhv = dy_ref[...], xh_ref[...]
        dg_ref[...] += jnp.sum(dyv * xhv, axis=0, keepdims=True)
        db_ref[...] += jnp.sum(dyv, axis=0, keepdims=True)
        dz = _ln_bwd_rows(dyv, xhv, rs_ref[...], g_ref[...])
        dz_ref[...] = dz
        dzb_ref[...] = dz.astype(BF16)

    row = pl.BlockSpec((tm, D), lambda i: (i, 0))
    vec = pl.BlockSpec((1, D), lambda i: (0, 0))
    return pl.pallas_call(
        body, name="ln1_bwd",
        out_shape=(jax.ShapeDtypeStruct((T, D), F32), jax.ShapeDtypeStruct((T, D), BF16),
                   jax.ShapeDtypeStruct((1, D), F32), jax.ShapeDtypeStruct((1, D), F32)),
        grid=(T // tm,), in_specs=[row, row, pl.BlockSpec((tm, 1), lambda i: (i, 0)), vec],
        out_specs=(row, row, vec, vec), compiler_params=_cparams(("arbitrary",)),
    )(dy, xh, rstd, g)


def _ffn_col_tile(T, d_ff):
    return _pick(d_ff, (256, 128)) if T >= 1024 else _pick(d_ff, (512, 256, 128))


def _ffn_gate(gp, cw_ref, cb_ref):
    return (cb_ref[...] + gp * cw_ref[2:3, :] + _shift_down(gp, 1) * cw_ref[1:2, :]
            + _shift_down(gp, 2) * cw_ref[0:1, :])


def _ffn_fwd(up, gpre, conv_w, conv_b, rider=None):
    T, d_ff = up.shape
    ct = _ffn_col_tile(T, d_ff)

    def body(up_ref, gp_ref, cw_ref, cb_ref, f_ref):
        gate = _ffn_gate(gp_ref[...], cw_ref, cb_ref)
        f_ref[...] = (_gelu(gate) * up_ref[...]).astype(BF16)

    col = pl.BlockSpec((T, ct), lambda j: (0, j))
    (f,), carried = _call(
        body, name="ffn_act_fwd", out_shape=[jax.ShapeDtypeStruct((T, d_ff), BF16)], grid=(d_ff // ct,),
        in_specs=[col, col, pl.BlockSpec((3, ct), lambda j: (0, j)), pl.BlockSpec((1, ct), lambda j: (0, j))],
        out_specs=[col], scratch_shapes=[], args=(up, gpre, conv_w, conv_b), sem=("parallel",), rider=rider)
    return f, carried


def _ffn_bwd(up, gpre, conv_w, conv_b, d_f, after=None):
    T, d_ff = up.shape
    ct = _ffn_col_tile(T, d_ff)

    def body(up_ref, gp_ref, cw_ref, cb_ref, df_ref, dup_ref, dgp_ref, dcw_ref, dcb_ref):
        gp = gp_ref[...]
        gate = _ffn_gate(gp, cw_ref, cb_ref)
        gel, dgel = _gelu_and_grad(gate)
        df = df_ref[...]
        dup_ref[...] = (df * gel).astype(BF16)
        dgate = df * up_ref[...] * dgel
        dcb_ref[...] = jnp.sum(dgate, axis=0, keepdims=True)
        dcw_ref[2:3, :] = jnp.sum(dgate * gp, axis=0, keepdims=True)
        dcw_ref[1:2, :] = jnp.sum(dgate * _shift_down(gp, 1), axis=0, keepdims=True)
        dcw_ref[0:1, :] = jnp.sum(dgate * _shift_down(gp, 2), axis=0, keepdims=True)
        dgp = (dgate * cw_ref[2:3, :] + _shift_up(dgate, 1) * cw_ref[1:2, :]
               + _shift_up(dgate, 2) * cw_ref[0:1, :])
        dgp_ref[...] = dgp.astype(BF16)

    col = pl.BlockSpec((T, ct), lambda j: (0, j))
    w3 = pl.BlockSpec((3, ct), lambda j: (0, j))
    v1 = pl.BlockSpec((1, ct), lambda j: (0, j))
    return _call(
        body, name="ffn_act_bwd",
        out_shape=(jax.ShapeDtypeStruct((T, d_ff), BF16), jax.ShapeDtypeStruct((T, d_ff), BF16),
                   jax.ShapeDtypeStruct((3, d_ff), F32), jax.ShapeDtypeStruct((1, d_ff), F32)),
        grid=(d_ff // ct,), in_specs=[col, col, w3, v1, col], out_specs=(col, col, w3, v1),
        scratch_shapes=[], args=(up, gpre, conv_w, conv_b, d_f), sem=("parallel",), after=after)[0]


def _adamw(w, g, m, v, name, after=None):
    R, C = w.shape
    tr = _row_tile(R, C * 4, 8, budget=1280 * 1024)
    c1 = 1.0 / (1.0 - ADAM_B1 ** ADAM_STEP)
    c2 = 1.0 / (1.0 - ADAM_B2 ** ADAM_STEP)

    def body(w_ref, g_ref, m_ref, v_ref, go_ref, d_ref, nm_ref, nv_ref):
        gv = g_ref[...]
        go_ref[...] = gv
        nm = ADAM_B1 * m_ref[...] + (1.0 - ADAM_B1) * gv
        nv = ADAM_B2 * v_ref[...] + (1.0 - ADAM_B2) * (gv * gv)
        nm_ref[...] = nm
        nv_ref[...] = nv
        d_ref[...] = -ADAM_LR * ((nm * c1) / (jnp.sqrt(nv * c2) + ADAM_EPS) + ADAM_WD * w_ref[...])

    blk = pl.BlockSpec((tr, C), lambda r: (r, 0))
    sh = jax.ShapeDtypeStruct((R, C), F32)
    return _call(body, name=name, out_shape=(sh,) * 4, grid=(R // tr,), in_specs=[blk] * 4, out_specs=(blk,) * 4,
                 scratch_shapes=[], args=(w, g, m, v), sem=("parallel",), after=after)[0]


def _group_blocks(w_blocks, per):
    nb, bw, _ = w_blocks.shape
    G = nb // per
    w4 = w_blocks.reshape(G, per, bw, bw)
    rows = []
    for p in range(per):
        parts = [w4[:, p] if q == p else jnp.zeros((G, bw, bw), w_blocks.dtype) for q in range(per)]
        rows.append(jnp.concatenate(parts, axis=2))
    return jnp.concatenate(rows, axis=1)


def _ungroup_blocks(w_groups, per):
    G, gw, _ = w_groups.shape
    bw = gw // per
    blocks = [w_groups[:, p * bw:(p + 1) * bw, p * bw:(p + 1) * bw] for p in range(per)]
    return jnp.stack(blocks, axis=1).reshape(G * per, bw, bw)


def _pack(parts):
    flat = jnp.concatenate([p.reshape(-1).astype(F32) for p in parts])
    n = flat.shape[0]
    rows = -(-n // LANES)
    rows = -(-rows // PACK_ROW_MULT) * PACK_ROW_MULT
    flat = jnp.pad(flat, (0, rows * LANES - n))
    return flat.reshape(rows, LANES)


def _unpack(packed, shapes):
    flat = packed.reshape(-1)
    out, off = [], 0
    for s in shapes:
        n = math.prod(s)
        out.append(flat[off:off + n].reshape(s))
        off += n
    return out


def kernel(x, w_in, b_gate, rnn_conv_w, rnn_conv_b, lru_wa, lru_ba, lru_wi, lru_bi, lru_lambda, attn_sinks, w_attn_proj, w_rnn_proj, w_out, ln1_g, ln1_b, ffn_w_up, ffn_w_gate, ffn_conv_w, ffn_conv_b, ffn_w_down, ln2_g, ln2_b, loss_target, m_w_in, m_b_gate, m_rnn_conv_w, m_rnn_conv_b, m_lru_wa, m_lru_ba, m_lru_wi, m_lru_bi, m_lru_lambda, m_attn_sinks, m_w_attn_proj, m_w_rnn_proj, m_w_out, m_ln1_g, m_ln1_b, m_ffn_w_up, m_ffn_w_gate, m_ffn_conv_w, m_ffn_conv_b, m_ffn_w_down, m_ln2_g, m_ln2_b, v_w_in, v_b_gate, v_rnn_conv_w, v_rnn_conv_b, v_lru_wa, v_lru_ba, v_lru_wi, v_lru_bi, v_lru_lambda, v_attn_sinks, v_w_attn_proj, v_w_rnn_proj, v_w_out, v_ln1_g, v_ln1_b, v_ffn_w_up, v_ffn_w_gate, v_ffn_conv_w, v_ffn_conv_b, v_ffn_w_down, v_ln2_g, v_ln2_b):
    weights = dict(w_in=w_in, b_gate=b_gate, rnn_conv_w=rnn_conv_w, rnn_conv_b=rnn_conv_b, lru_wa=lru_wa,
                   lru_ba=lru_ba, lru_wi=lru_wi, lru_bi=lru_bi, lru_lambda=lru_lambda, attn_sinks=attn_sinks,
                   w_attn_proj=w_attn_proj, w_rnn_proj=w_rnn_proj, w_out=w_out, ln1_g=ln1_g, ln1_b=ln1_b,
                   ffn_w_up=ffn_w_up, ffn_w_gate=ffn_w_gate, ffn_conv_w=ffn_conv_w, ffn_conv_b=ffn_conv_b,
                   ffn_w_down=ffn_w_down, ln2_g=ln2_g, ln2_b=ln2_b)
    m_in = dict(w_in=m_w_in, b_gate=m_b_gate, rnn_conv_w=m_rnn_conv_w, rnn_conv_b=m_rnn_conv_b, lru_wa=m_lru_wa,
                lru_ba=m_lru_ba, lru_wi=m_lru_wi, lru_bi=m_lru_bi, lru_lambda=m_lru_lambda, attn_sinks=m_attn_sinks,
                w_attn_proj=m_w_attn_proj, w_rnn_proj=m_w_rnn_proj, w_out=m_w_out, ln1_g=m_ln1_g, ln1_b=m_ln1_b,
                ffn_w_up=m_ffn_w_up, ffn_w_gate=m_ffn_w_gate, ffn_conv_w=m_ffn_conv_w, ffn_conv_b=m_ffn_conv_b,
                ffn_w_down=m_ffn_w_down, ln2_g=m_ln2_g, ln2_b=m_ln2_b)
    v_in = dict(w_in=v_w_in, b_gate=v_b_gate, rnn_conv_w=v_rnn_conv_w, rnn_conv_b=v_rnn_conv_b, lru_wa=v_lru_wa,
                lru_ba=v_lru_ba, lru_wi=v_lru_wi, lru_bi=v_lru_bi, lru_lambda=v_lru_lambda, attn_sinks=v_attn_sinks,
                w_attn_proj=v_w_attn_proj, w_rnn_proj=v_w_rnn_proj, w_out=v_w_out, ln1_g=v_ln1_g, ln1_b=v_ln1_b,
                ffn_w_up=v_ffn_w_up, ffn_w_gate=v_ffn_w_gate, ffn_conv_w=v_ffn_conv_w, ffn_conv_b=v_ffn_conv_b,
                ffn_w_down=v_ffn_w_down, ln2_g=v_ln2_g, ln2_b=v_ln2_b)
    order = list(weights)

    assert x.shape[0] == 1 and w_in.shape[0] == 1, "one sequence per device, depth 1"
    T, D = x.shape[1], x.shape[2]
    nq = attn_sinks.shape[-1]
    nkv = nq // GROUP
    d_attn, d_kv = nq * HEAD_DIM, nkv * HEAD_DIM
    d_rnn = rnn_conv_b.shape[-1]
    d_ff = ffn_conv_b.shape[-1]
    n_blocks, bw = lru_wa.shape[1], lru_wa.shape[2]
    per = (bw * LANES // math.gcd(bw, LANES)) // bw
    gw = per * bw
    assert n_blocks % per == 0 and d_rnn == n_blocks * bw
    q_off, k_off, v_off = 0, d_attn, d_attn + d_kv
    rx_off = d_attn + 2 * d_kv
    ry_off = rx_off + d_rnn
    gl_off = ry_off + d_rnn
    d_in = gl_off + 2 * D
    assert w_in.shape[-1] * N_SHARDS == d_in
    assert k_off % d_kv == 0 and rx_off % gw == 0 and T % ATTN_BLOCK == 0

    xi, yi, ci = lax.axis_index("x"), lax.axis_index("y"), lax.axis_index("c")
    j_me = 2 * xi + yi
    jc_arr = jnp.stack([j_me, ci]).astype(jnp.int32)

    x0 = x[0]
    tgt = loss_target[0]
    big = ["w_in", "w_attn_proj", "w_rnn_proj", "w_out", "ffn_w_up", "ffn_w_gate", "ffn_w_down"]
    near, diag = (0, 1), (2,)
    order_arr = jnp.stack([j_me, j_me ^ 2, j_me ^ 1, j_me ^ 3]).astype(jnp.int32)

    rcw_s, fcw_s = _all_gather_small([rnn_conv_w[0], ffn_conv_w[0]])
    rcw = jnp.concatenate([rcw_s[j] for j in range(N_SHARDS)], axis=1)
    fcw = jnp.concatenate([fcw_s[j] for j in range(N_SHARDS)], axis=1)

    own = {"w_in": _cast_bf16_into_slot(w_in[0], jc_arr, "cast_w_in", fcw_s)}
    in_near = _gather_step(own["w_in"], fcw_s, "gather_start_w_in", start_d2d=False, relations=near)
    last = in_near[2]
    for n in big[1:]:
        own[n] = last = _cast_bf16_into_slot(weights[n][0], jc_arr, "cast_" + n, last)
    x0b = _cast_bf16(x0, "cast_x", last)

    wa_g = _group_blocks(lru_wa[0], per).astype(BF16)
    wi_g = _group_blocks(lru_wi[0], per).astype(BF16)

    proj = _mm_shards(x0b, in_near[1], order_arr, [0], "mm_proj_own", x0b)
    d2d_sems, relay_sems, buf, tok = _gather_relay_step(in_near[1], proj, "gather_forward_w_in_near", in_near[0])
    w_in_s = _gather_step(buf, tok, "gather_finish_w_in_near", sems_in=d2d_sems, relations=near)
    proj = _mm_shards(x0b, w_in_s, order_arr, [1, 2], "mm_proj_near", w_in_s, out=proj)
    d2d_sems, buf, tok = _gather_relay_step(w_in_s, proj, "gather_forward_w_in_diag", None, relay_in=relay_sems)
    w_in_s = _gather_step(buf, tok, "gather_finish_w_in_diag", sems_in=d2d_sems, relations=diag)
    proj = _mm_shards(x0b, w_in_s, order_arr, [3], "mm_proj_diag", w_in_s, out=proj)
    started, last = _gather_start_all([own[n] for n in big[1:]], proj, "gather_start_all")
    ici = dict(zip(big[1:], started))

    def forward_halves(n, after):
        sems, buf = ici[n]
        return _gather_step(buf, after, "gather_forward_" + n, sems_in=sems, start_d2d=True)

    def gathered(d2d, after, n):
        sems, buf, _ = d2d
        return _gather_step(buf, after, "gather_finish_" + n, sems_in=sems)

    a_out = _attn_fwd(proj, attn_sinks, nq, (q_off, k_off, v_off), after=last)
    fw_ap = forward_halves("w_attn_proj", a_out)
    (b_out, h_all), _ = _rnn_fwd(proj, (rx_off, ry_off), rcw, rnn_conv_b, wa_g, wi_g, lru_ba, lru_bi, lru_lambda)
    fw_rp = forward_halves("w_rnn_proj", b_out)
    w_ap = gathered(fw_ap, b_out, "w_attn_proj").reshape(d_attn, D)
    y_attn = _mm(a_out, w_ap, name="mm_attn_proj")
    fw_o = forward_halves("w_out", y_attn)
    w_rp = gathered(fw_rp, y_attn, "w_rnn_proj").reshape(d_rnn, D)
    y_rnn = _mm(b_out, w_rp, name="mm_rnn_proj")
    merged, _ = _merge_fwd(proj, gl_off, b_gate, y_attn, y_rnn)
    w_o = gathered(fw_o, merged, "w_out").reshape(D, D)
    mix = _mm(merged, w_o, name="mm_out")
    fw_up = forward_halves("ffn_w_up", mix)
    (x1, x1b, xh1, rstd1), _ = _ln_fwd(x0, mix, ln1_g, ln1_b, "ln1_fwd")
    w_up_s = gathered(fw_up, x1b, "ffn_w_up")
    up = _mm(x1b, w_up_s, name="mm_up", b_shards=N_SHARDS)
    fw_gate = forward_halves("ffn_w_gate", up)
    w_gate_s = gathered(fw_gate, fw_gate[2], "ffn_w_gate")
    gpre = _mm(x1b, w_gate_s, name="mm_gate", b_shards=N_SHARDS)
    f_act, _ = _ffn_fwd(up, gpre, fcw, ffn_conv_b)
    fw_dn = forward_halves("ffn_w_down", f_act)
    w_dn = gathered(fw_dn, fw_dn[2], "ffn_w_down").reshape(d_ff, D)
    f_out = _mm(f_act, w_dn, name="mm_down")
    dz2, dz2b, loss_acc, dg2, db2 = _ln_loss_bwd(x1, f_out, ln2_g, ln2_b, tgt)

    def pair_sums(arrs, from_sibling, names):
        return [_pair_sum(g, la, jc_arr, "pair_sum_" + n) for g, la, n in zip(arrs, from_sibling, names)]

    def shard_sums(parts, landed, names):
        return [_shard_sum(cp, lb, jc_arr, "shard_sum_" + n) for cp, lb, n in zip(parts, landed, names)]

    halves = {}
    g_down = _mm(f_act, dz2b, name="mm_d_w_down", ta=True, out_dtype=BF16)
    g1 = [g_down.reshape(N_SHARDS, d_ff // N_SHARDS, D)]
    d_f, sib1 = _mm(dz2b, w_dn, name="mm_d_f", tb=True, rider=_pair_rider(g1))
    sent1 = _shard_exchange_start(pair_sums(g1, sib1, ["ffn_w_down"]), "shard_exchange_start_down")
    dup, dgp, d_fcw, d_fcb = _ffn_bwd(up, gpre, fcw, ffn_conv_b, d_f, after=sent1[4])
    g_up = _mm(x1b, dup, name="mm_d_w_up", ta=True, out_dtype=BF16, out_shards=N_SHARDS)
    g_gate = _mm(x1b, dgp, name="mm_d_w_gate", ta=True, out_dtype=BF16, out_shards=N_SHARDS)
    g2 = [g_up, g_gate]
    dx1_a, sib2 = _mm(dup, w_up_s, name="mm_dx1_up", tb=True, b_shards=N_SHARDS, adds=((ALPHA, dz2),),
                      rider=_pair_rider(g2))
    halves["ffn_w_down"], = shard_sums(*_shard_exchange_wait(sent1, dx1_a, "shard_exchange_wait_down"),
                                       ["ffn_w_down"])
    sent2 = _shard_exchange_start(pair_sums(g2, sib2, ["ffn_w_up", "ffn_w_gate"]), "shard_exchange_start_up_gate")
    dx1 = _mm(dgp, w_gate_s, name="mm_dx1_gate", tb=True, b_shards=N_SHARDS, adds=((1.0, dx1_a),), after=sent2[4])
    dz1, dz1b, dg1, db1 = _ln_bwd(dx1, xh1, rstd1, ln1_g)
    g_out = _mm(merged, dz1b, name="mm_d_w_out", ta=True, out_dtype=BF16)
    d_m = _mm(dz1b, w_o, name="mm_d_merged", tb=True)
    dya, dyr, dgl_a, dgl_r, dbg_a, dbg_r = _merge_bwd(proj, gl_off, b_gate, y_attn, y_rnn, d_m)
    g_ap = _mm(a_out, dya, name="mm_d_w_attn_proj", ta=True, out_dtype=BF16)
    g_rp = _mm(b_out, dyr, name="mm_d_w_rnn_proj", ta=True, out_dtype=BF16)
    names3 = ["w_out", "w_attn_proj", "w_rnn_proj"]
    g3 = [g_out.reshape(N_SHARDS, D // N_SHARDS, D), g_ap.reshape(N_SHARDS, d_attn // N_SHARDS, D),
          g_rp.reshape(N_SHARDS, d_rnn // N_SHARDS, D)]
    d_a = _mm(dya, w_ap, name="mm_d_attn", tb=True)
    d_b, sib3 = _mm(dyr, w_rp, name="mm_d_rnn", tb=True, rider=_pair_rider(g3))
    sent3 = _shard_exchange_start(pair_sums(g3, sib3, names3), "shard_exchange_start_mixers")
    dq, dk, dv, dsink = _attn_bwd(proj, d_a, attn_sinks, nq, (q_off, k_off, v_off), after=sent3[4])
    (drx, dry, d_rcw, d_rcb, d_ba, d_bi, d_lam, d_wa_g, d_wi_g), _ = _rnn_bwd(
        proj, (rx_off, ry_off), h_all, d_b, rcw, rnn_conv_b, wa_g, wi_g, lru_ba, lru_bi, lru_lambda)
    halves["ffn_w_up"], halves["ffn_w_gate"] = shard_sums(
        *_shard_exchange_wait(sent2, drx, "shard_exchange_wait_up_gate"), ["ffn_w_up", "ffn_w_gate"])
    d_proj = jnp.concatenate([dq, dk.astype(BF16), dv.astype(BF16), drx, dry, dgl_a, dgl_r], axis=1)
    ffn_names = ["ffn_w_down", "ffn_w_up", "ffn_w_gate"]
    g_in, shared_ffn = _mm(x0b, d_proj, name="mm_d_w_in", ta=True, out_dtype=BF16, out_shards=N_SHARDS,
                           rider=_share_rider([halves[n] for n in ffn_names]))
    halves["w_out"], halves["w_attn_proj"], halves["w_rnn_proj"] = shard_sums(
        *_shard_exchange_wait(sent3, g_in, "shard_exchange_wait_mixers"), names3)

    small_parts = [
        ("loss", loss_acc[0:1, 0:1]),
        ("b_gate", jnp.concatenate([dbg_a, dbg_r], axis=1)),
        ("rnn_conv_w", d_rcw), ("rnn_conv_b", d_rcb),
        ("lru_wa", _ungroup_blocks(d_wa_g, per)), ("lru_ba", d_ba),
        ("lru_wi", _ungroup_blocks(d_wi_g, per)), ("lru_bi", d_bi), ("lru_lambda", d_lam),
        ("attn_sinks", dsink[0:1, 0:nq]),
        ("ln1_g", dg1), ("ln1_b", db1),
        ("ffn_conv_w", d_fcw), ("ffn_conv_b", d_fcb),
        ("ln2_g", dg2), ("ln2_b", db2),
    ]
    packed = _pack([p for _, p in small_parts])
    rs = packed.shape[0]

    def whole(g):
        return g.reshape(2 * g.shape[1], g.shape[2])

    grads = {n: whole(g) for n, g in zip(ffn_names, shared_ffn)}
    out_g, out_d, out_m, out_v = {}, {}, {}, {}

    def adamw(n, after=None):
        shape = weights[n].shape
        two_d = (math.prod(shape[:-1]), shape[-1])
        g2, d2, m2, v2 = _adamw(weights[n].reshape(two_d), grads[n].reshape(two_d), m_in[n].reshape(two_d),
                                v_in[n].reshape(two_d), "adamw_" + n, after=after)
        out_g[n], out_d[n] = g2.reshape(shape), d2.reshape(shape)
        out_m[n], out_v[n] = m2.reshape(shape), v2.reshape(shape)

    g4 = [g_in, packed.reshape(N_SHARDS, rs // N_SHARDS, LANES)]
    sib4 = _run_rider(_pair_rider(g4), "pair_exchange_in_small")
    part4 = pair_sums(g4, sib4, ["w_in", "small"])
    grad_x, (lb_in, lb_small, *shared_mix) = _mm(
        d_proj, w_in_s, name="mm_d_x", tb=True, b_shards=N_SHARDS, adds=((ALPHA, dz1),),
        rider=_join_riders(_shard_exchange_rider(part4, _atoms([0], near) + _atoms([1])),
                           _share_rider([halves[n] for n in names3])))
    grads.update({n: whole(g) for n, g in zip(names3, shared_mix)})
    sent5 = _shard_exchange_start(part4[:1], "shard_exchange_start_in_diag", relations=diag, lands=[lb_in])
    for n in ffn_names + names3:
        adamw(n, after=sent5[4])
    (part_in,), (lb_in,) = _shard_exchange_wait(sent5, out_d[names3[-1]], "shard_exchange_wait_in_diag")
    part_small = part4[1]
    halves["w_in"], = shard_sums([part_in], [lb_in], ["w_in"])
    eighths = _shard_sum(part_small, lb_small, jc_arr, "shard_sum_small", all_slots=True)
    shared_in, reduced = _run_rider(_share_rider([halves["w_in"]], eighths), "share_in_small")
    grads["w_in"] = whole(shared_in)
    reduced = reduced.reshape(rs, LANES)
    small = dict(zip([n for n, _ in small_parts], _unpack(reduced, [p.shape for _, p in small_parts])))
    loss = small.pop("loss").reshape(())
    rcw_n = d_rnn // N_SHARDS
    fcw_n = d_ff // N_SHARDS
    small["rnn_conv_w"] = lax.dynamic_slice(small["rnn_conv_w"], (0, j_me * rcw_n), (4, rcw_n))
    small["ffn_conv_w"] = lax.dynamic_slice(small["ffn_conv_w"], (0, j_me * fcw_n), (3, fcw_n))
    for n, g in small.items():
        grads[n] = g

    for n in order:
        if n not in out_g:
            adamw(n)

    return (loss, grad_x.reshape(x.shape), *[out_g[n] for n in order], *[out_d[n] for n in order],
            *[out_m[n] for n in order], *[out_v[n] for n in order])
```
